```python
import math
import jax
import jax.numpy as jnp
from jax import lax
import numpy as np

D_MODEL = 2048
BATCH = 8
SEQ = 8192
DEPTH = 2

MLA_HEADS = 8
Q_LORA = 512
KV_LORA = 512
NOPE_DIM = 128
ROPE_DIM = 64
V_DIM = 128
ROPE_THETA = 10000.0
DIL_HEADS = 8
DIL_HEAD_DIM = 128
DIL_BRANCHES = ((128, 1), (512, 4), (2048, 16))
DIL_WIDTH = DIL_HEADS * DIL_HEAD_DIM
BLK = 128
IN_COLS = Q_LORA + KV_LORA + ROPE_DIM + 3 * DIL_WIDTH
MIX_WIDTH = MLA_HEADS * V_DIM + DIL_WIDTH
S5_GROUP = 16
S5_GROUPS = D_MODEL // S5_GROUP
S5_STATE = 64
FFN_HIDDEN = -(-8 * D_MODEL // (3 * 256)) * 256
PLE_DIM = 256
DEEPNORM_ALPHA = (2.0 * DEPTH) ** 0.25
DEEPNORM_BETA = (8.0 * DEPTH) ** -0.25
NEG = -1e30

kernel_name = 'hybrid_mla_dilated_s5_block'


def _layernorm(x, g, b, eps=1e-5):
    xf = x.astype(jnp.float32)
    mu = xf.mean(-1, keepdims=True)
    var = jnp.square(xf - mu).mean(-1, keepdims=True)
    y = (xf - mu) * lax.rsqrt(var + eps) * g.astype(jnp.float32) + b.astype(jnp.float32)
    return y.astype(x.dtype)


def _rmsnorm(x, g, eps=1e-6):
    xf = x.astype(jnp.float32)
    y = xf * lax.rsqrt(jnp.square(xf).mean(-1, keepdims=True) + eps) * g.astype(jnp.float32)
    return y.astype(x.dtype)


def _rope(t, positions):
    half = t.shape[-1] // 2
    inv_freq = ROPE_THETA ** (-jnp.arange(half, dtype=jnp.float32) / half)
    ang = positions.astype(jnp.float32)[..., None] * inv_freq
    cos, sin = jnp.cos(ang)[:, :, None, :], jnp.sin(ang)[:, :, None, :]
    t1, t2 = t[..., :half], t[..., half:]
    return jnp.concatenate([t1 * cos - t2 * sin, t1 * sin + t2 * cos], axis=-1).astype(t.dtype)


def _alibi_slopes(n):
    return jnp.asarray(2.0 ** (-8.0 * np.arange(1, n + 1) / n), dtype=jnp.float32)


def _mla_causal(q, k, v):
    B, S, H, Dqk = q.shape
    nb = S // BLK
    scale = Dqk ** -0.5
    qb = q.reshape(B, nb, BLK, H, Dqk).transpose(1, 0, 2, 3, 4)
    kpos = jnp.arange(S)

    def block(args):
        qblk, bi = args
        s = jnp.einsum('bqhd,bkhd->bhqk', qblk, k).astype(jnp.float32) * scale
        qpos = bi * BLK + jnp.arange(BLK)
        s = jnp.where(kpos[None, :] <= qpos[:, None], s, NEG)
        pr = jax.nn.softmax(s, axis=-1).astype(v.dtype)
        return jnp.einsum('bhqk,bkhd->bqhd', pr, v)

    out = lax.map(block, (qb, jnp.arange(nb)))
    return out.transpose(1, 0, 2, 3, 4).reshape(B, S, H * v.shape[-1])


def _dilated_branch(q, k, v, window, dilation, slopes):
    B, S, H, Dh = q.shape
    d = dilation
    L = S // d
    steps = window // d
    nb = -(-L // BLK)
    Lp = nb * BLK

    def to_classes(t):
        t = t.reshape(B, L, d, H, Dh).transpose(0, 2, 1, 3, 4).reshape(B * d, L, H, Dh)
        return jnp.pad(t, ((0, 0), (0, Lp - L), (0, 0), (0, 0)))

    def windows(t):
        tb = jnp.pad(t, ((0, 0), (BLK, 0), (0, 0), (0, 0))).reshape(B * d, nb + 1, BLK, H, Dh)
        return jnp.concatenate([tb[:, :-1], tb[:, 1:]], axis=2)

    qb = to_classes(q).reshape(B * d, nb, BLK, H, Dh)
    kw = windows(to_classes(k))
    vw = windows(to_classes(v))
    s = jnp.einsum('zbqhd,zbkhd->zbhqk', qb, kw).astype(jnp.float32) * (Dh ** -0.5)
    qi = jnp.arange(BLK)[:, None]
    ki = jnp.arange(2 * BLK)[None, :]
    dist = qi + BLK - ki
    kabs = jnp.arange(nb)[:, None, None] * BLK + ki[None] - BLK
    valid = (dist >= 0)[None] & (dist <= steps)[None] & (kabs >= 0)
    bias = -slopes[:, None, None] * (d * dist).astype(jnp.float32)[None]
    s = jnp.where(valid[None, :, None], s + bias[None, None], NEG)
    m = s.max(-1, keepdims=True)
    e = jnp.exp(s - m)
    l = e.sum(-1, keepdims=True)
    o = jnp.einsum('zbhqk,zbkhd->zbqhd', (e / l).astype(v.dtype), vw)
    lse = (m + jnp.log(l))[..., 0]
    o = o.reshape(B * d, Lp, H, Dh)[:, :L].reshape(B, d, L, H, Dh).transpose(0, 2, 1, 3, 4).reshape(B, S, H, Dh)
    lse = lse.transpose(0, 1, 3, 2).reshape(B * d, Lp, H)[:, :L].reshape(B, d, L, H).transpose(0, 2, 1, 3).reshape(B, S, H)
    return o, lse


def _hybrid_attention(h, positions, w_in, q_norm, w_q_b, kv_norm, w_kv_b, w_out):
    B, S, _ = h.shape
    splits = list(np.cumsum([Q_LORA, KV_LORA, ROPE_DIM, DIL_WIDTH, DIL_WIDTH]))
    q_lat, kv_lat, k_pe, qd, kd, vd = jnp.split(h @ w_in, splits, axis=-1)
    q = (_rmsnorm(q_lat, q_norm) @ w_q_b).reshape(B, S, MLA_HEADS, NOPE_DIM + ROPE_DIM)
    q = jnp.concatenate([q[..., :NOPE_DIM], _rope(q[..., NOPE_DIM:], positions)], axis=-1)
    kv = (_rmsnorm(kv_lat, kv_norm) @ w_kv_b).reshape(B, S, MLA_HEADS, NOPE_DIM + V_DIM)
    k_pe = jnp.broadcast_to(_rope(k_pe[:, :, None, :], positions), (B, S, MLA_HEADS, ROPE_DIM))
    k = jnp.concatenate([kv[..., :NOPE_DIM], k_pe], axis=-1)
    out_a = _mla_causal(q, k, kv[..., NOPE_DIM:])
    qd = qd.reshape(B, S, DIL_HEADS, DIL_HEAD_DIM)
    kd = kd.reshape(B, S, DIL_HEADS, DIL_HEAD_DIM)
    vd = vd.reshape(B, S, DIL_HEADS, DIL_HEAD_DIM)
    slopes = _alibi_slopes(DIL_HEADS)
    outs, lses = [], []
    for window, dilation in DIL_BRANCHES:
        o_g, lse_g = _dilated_branch(qd, kd, vd, window, dilation, slopes)
        outs.append(o_g)
        lses.append(lse_g)
    wts = jax.nn.softmax(jnp.stack(lses), axis=0)
    out_b = jnp.einsum('gbsh,gbshd->bshd', wts, jnp.stack(outs).astype(jnp.float32))
    out_b = out_b.astype(h.dtype).reshape(B, S, DIL_WIDTH)
    return jnp.concatenate([out_a, out_b], axis=-1) @ w_out


def _s5_glu(h, a_re, a_im, log_dt, b_re, b_im, c_re, c_im, d_skip, w_glu):
    B, S, D = h.shape
    f32 = jnp.float32
    u = h.astype(f32).reshape(B, S, S5_GROUPS, S5_GROUP)
    A = lax.complex(a_re.astype(f32), a_im.astype(f32))
    dt = jnp.exp(log_dt.astype(f32))[:, None]
    A_bar = jnp.exp(A * dt)
    B_bar = ((A_bar - 1.0) / A)[..., None] * lax.complex(b_re.astype(f32), b_im.astype(f32))
    C_mat = lax.complex(c_re.astype(f32), c_im.astype(f32))
    bu = jnp.einsum('bsgc,gpc->bsgp', u.astype(jnp.complex64), B_bar)
    a = jnp.broadcast_to(A_bar, bu.shape)

    def combine(left, right):
        a_l, b_l = left
        a_r, b_r = right
        return a_r * a_l, a_r * b_l + b_r

    _, states = lax.associative_scan(combine, (a, bu), axis=1)
    y = jnp.einsum('bsgp,gcp->bsgc', states, C_mat).real.reshape(B, S, D)
    y = y + d_skip.astype(f32) * h.astype(f32)
    z = jax.nn.gelu(y).astype(h.dtype)
    val, gate = jnp.split(z @ w_glu, 2, axis=-1)
    return val * jax.nn.sigmoid(gate)


def _swiglu(h, w_in, w_out):
    g, u = jnp.split(h @ w_in, 2, axis=-1)
    return (jax.nn.silu(g) * u) @ w_out


def _fwd_setup_inputs(seed: int = 0) -> dict:
    key = jax.random.key(seed)
    ks = iter(jax.random.split(key, 40))
    ne, no = (DEPTH + 1) // 2, DEPTH // 2
    f32 = jnp.float32

    def nrm(shape, scale):
        return jax.random.normal(next(ks), shape, f32) * scale

    x = nrm((BATCH, SEQ, D_MODEL), 1.0)
    p = nrm((DEPTH, BATCH, SEQ, PLE_DIM), 1.0)
    positions = jnp.broadcast_to(jnp.arange(SEQ, dtype=jnp.int32), (BATCH, SEQ))
    attn_w_in = nrm((ne, D_MODEL, IN_COLS), D_MODEL ** -0.5)
    mla_q_norm = 1.0 + nrm((ne, Q_LORA), 0.02)
    mla_w_q_b = nrm((ne, Q_LORA, MLA_HEADS * (NOPE_DIM + ROPE_DIM)), Q_LORA ** -0.5)
    mla_kv_norm = 1.0 + nrm((ne, KV_LORA), 0.02)
    mla_w_kv_b = nrm((ne, KV_LORA, MLA_HEADS * (NOPE_DIM + V_DIM)), KV_LORA ** -0.5)
    attn_w_out = nrm((ne, MIX_WIDTH, D_MODEL), MIX_WIDTH ** -0.5 * DEEPNORM_BETA)
    s5_a_re = -0.5 + nrm((no, S5_GROUPS, S5_STATE), 0.01)
    s5_a_im = math.pi * jnp.arange(S5_STATE, dtype=f32) + nrm((no, S5_GROUPS, S5_STATE), 0.01)
    s5_log_dt = jax.random.uniform(next(ks), (no, S5_GROUPS), f32, math.log(1e-3), math.log(1e-1))
    s5_b_re = nrm((no, S5_GROUPS, S5_STATE, S5_GROUP), (2 * S5_GROUP) ** -0.5)
    s5_b_im = nrm((no, S5_GROUPS, S5_STATE, S5_GROUP), (2 * S5_GROUP) ** -0.5)
    s5_c_re = nrm((no, S5_GROUPS, S5_GROUP, S5_STATE), (2 * S5_STATE) ** -0.5)
    s5_c_im = nrm((no, S5_GROUPS, S5_GROUP, S5_STATE), (2 * S5_STATE) ** -0.5)
    s5_d = nrm((no, D_MODEL), 1.0)
    s5_w_glu = jnp.concatenate([nrm((no, D_MODEL, D_MODEL), D_MODEL ** -0.5 * DEEPNORM_BETA),
                                nrm((no, D_MODEL, D_MODEL), D_MODEL ** -0.5)], axis=-1)
    ln1_g = 1.0 + nrm((DEPTH, D_MODEL), 0.02)
    ln1_b = nrm((DEPTH, D_MODEL), 0.02)
    ffn_w_in = nrm((DEPTH, D_MODEL, 2 * FFN_HIDDEN), D_MODEL ** -0.5)
    ffn_w_out = nrm((DEPTH, FFN_HIDDEN, D_MODEL), FFN_HIDDEN ** -0.5 * DEEPNORM_BETA)
    ple_w = nrm((DEPTH, PLE_DIM, D_MODEL), PLE_DIM ** -0.5)
    ple_gate_w = nrm((DEPTH, D_MODEL, D_MODEL), D_MODEL ** -0.5)
    ln2_g = 1.0 + nrm((DEPTH, D_MODEL), 0.02)
    ln2_b = nrm((DEPTH, D_MODEL), 0.02)
    return {'x': x, 'p': p, 'positions': positions,
            'attn_w_in': attn_w_in, 'mla_q_norm': mla_q_norm, 'mla_w_q_b': mla_w_q_b,
            'mla_kv_norm': mla_kv_norm, 'mla_w_kv_b': mla_w_kv_b, 'attn_w_out': attn_w_out,
            's5_a_re': s5_a_re, 's5_a_im': s5_a_im, 's5_log_dt': s5_log_dt,
            's5_b_re': s5_b_re, 's5_b_im': s5_b_im, 's5_c_re': s5_c_re, 's5_c_im': s5_c_im,
            's5_d': s5_d, 's5_w_glu': s5_w_glu,
            'ln1_g': ln1_g, 'ln1_b': ln1_b, 'ffn_w_in': ffn_w_in, 'ffn_w_out': ffn_w_out,
            'ple_w': ple_w, 'ple_gate_w': ple_gate_w, 'ln2_g': ln2_g, 'ln2_b': ln2_b}


def _fwd_reference(x, p, positions, attn_w_in, mla_q_norm, mla_w_q_b, mla_kv_norm, mla_w_kv_b, attn_w_out,
              s5_a_re, s5_a_im, s5_log_dt, s5_b_re, s5_b_im, s5_c_re, s5_c_im, s5_d, s5_w_glu,
              ln1_g, ln1_b, ffn_w_in, ffn_w_out, ple_w, ple_gate_w, ln2_g, ln2_b):
    h = x
    for i in range(DEPTH):
        j = i // 2
        if i % 2 == 0:
            mix = _hybrid_attention(h, positions, attn_w_in[j], mla_q_norm[j], mla_w_q_b[j],
                                    mla_kv_norm[j], mla_w_kv_b[j], attn_w_out[j])
        else:
            mix = _s5_glu(h, s5_a_re[j], s5_a_im[j], s5_log_dt[j], s5_b_re[j], s5_b_im[j],
                          s5_c_re[j], s5_c_im[j], s5_d[j], s5_w_glu[j])
        h = _layernorm(DEEPNORM_ALPHA * h + mix, ln1_g[i], ln1_b[i])
        ple = (p[i] @ ple_w[i]) * jax.nn.sigmoid(h @ ple_gate_w[i])
        h = _layernorm(DEEPNORM_ALPHA * h + _swiglu(h, ffn_w_in[i], ffn_w_out[i]) + ple, ln2_g[i], ln2_b[i])
    return h


import jax as _jax
import jax.numpy as _jnp

TWIN_FORMAT = 'train_step'
FWD_PARAMS = ['x', 'p', 'positions', 'attn_w_in', 'mla_q_norm', 'mla_w_q_b', 'mla_kv_norm', 'mla_w_kv_b', 'attn_w_out', 's5_a_re', 's5_a_im', 's5_log_dt', 's5_b_re', 's5_b_im', 's5_c_re', 's5_c_im', 's5_d', 's5_w_glu', 'ln1_g', 'ln1_b', 'ffn_w_in', 'ffn_w_out', 'ple_w', 'ple_gate_w', 'ln2_g', 'ln2_b']
TWIN_WEIGHTS = ['attn_w_in', 'mla_q_norm', 'mla_w_q_b', 'mla_kv_norm', 'mla_w_kv_b', 'attn_w_out', 's5_a_re', 's5_a_im', 's5_log_dt', 's5_b_re', 's5_b_im', 's5_c_re', 's5_c_im', 's5_d', 's5_w_glu', 'ln1_g', 'ln1_b', 'ffn_w_in', 'ffn_w_out', 'ple_w', 'ple_gate_w', 'ln2_g', 'ln2_b']
TWIN_DIFF_INPUT = 'x'
TWIN_INPUTS = ['x', 'p', 'positions', 'attn_w_in', 'mla_q_norm', 'mla_w_q_b', 'mla_kv_norm', 'mla_w_kv_b', 'attn_w_out', 's5_a_re', 's5_a_im', 's5_log_dt', 's5_b_re', 's5_b_im', 's5_c_re', 's5_c_im', 's5_d', 's5_w_glu', 'ln1_g', 'ln1_b', 'ffn_w_in', 'ffn_w_out', 'ple_w', 'ple_gate_w', 'ln2_g', 'ln2_b', 'loss_target', 'm_attn_w_in', 'm_mla_q_norm', 'm_mla_w_q_b', 'm_mla_kv_norm', 'm_mla_w_kv_b', 'm_attn_w_out', 'm_s5_a_re', 'm_s5_a_im', 'm_s5_log_dt', 'm_s5_b_re', 'm_s5_b_im', 'm_s5_c_re', 'm_s5_c_im', 'm_s5_d', 'm_s5_w_glu', 'm_ln1_g', 'm_ln1_b', 'm_ffn_w_in', 'm_ffn_w_out', 'm_ple_w', 'm_ple_gate_w', 'm_ln2_g', 'm_ln2_b', 'v_attn_w_in', 'v_mla_q_norm', 'v_mla_w_q_b', 'v_mla_kv_norm', 'v_mla_w_kv_b', 'v_attn_w_out', 'v_s5_a_re', 'v_s5_a_im', 'v_s5_log_dt', 'v_s5_b_re', 'v_s5_b_im', 'v_s5_c_re', 'v_s5_c_im', 'v_s5_d', 'v_s5_w_glu', 'v_ln1_g', 'v_ln1_b', 'v_ffn_w_in', 'v_ffn_w_out', 'v_ple_w', 'v_ple_gate_w', 'v_ln2_g', 'v_ln2_b']
TWIN_OUTPUTS = ['loss', 'grad_x', 'grad_attn_w_in', 'grad_mla_q_norm', 'grad_mla_w_q_b', 'grad_mla_kv_norm', 'grad_mla_w_kv_b', 'grad_attn_w_out', 'grad_s5_a_re', 'grad_s5_a_im', 'grad_s5_log_dt', 'grad_s5_b_re', 'grad_s5_b_im', 'grad_s5_c_re', 'grad_s5_c_im', 'grad_s5_d', 'grad_s5_w_glu', 'grad_ln1_g', 'grad_ln1_b', 'grad_ffn_w_in', 'grad_ffn_w_out', 'grad_ple_w', 'grad_ple_gate_w', 'grad_ln2_g', 'grad_ln2_b', 'delta_attn_w_in', 'delta_mla_q_norm', 'delta_mla_w_q_b', 'delta_mla_kv_norm', 'delta_mla_w_kv_b', 'delta_attn_w_out', 'delta_s5_a_re', 'delta_s5_a_im', 'delta_s5_log_dt', 'delta_s5_b_re', 'delta_s5_b_im', 'delta_s5_c_re', 'delta_s5_c_im', 'delta_s5_d', 'delta_s5_w_glu', 'delta_ln1_g', 'delta_ln1_b', 'delta_ffn_w_in', 'delta_ffn_w_out', 'delta_ple_w', 'delta_ple_gate_w', 'delta_ln2_g', 'delta_ln2_b', 'new_m_attn_w_in', 'new_m_mla_q_norm', 'new_m_mla_w_q_b', 'new_m_mla_kv_norm', 'new_m_mla_w_kv_b', 'new_m_attn_w_out', 'new_m_s5_a_re', 'new_m_s5_a_im', 'new_m_s5_log_dt', 'new_m_s5_b_re', 'new_m_s5_b_im', 'new_m_s5_c_re', 'new_m_s5_c_im', 'new_m_s5_d', 'new_m_s5_w_glu', 'new_m_ln1_g', 'new_m_ln1_b', 'new_m_ffn_w_in', 'new_m_ffn_w_out', 'new_m_ple_w', 'new_m_ple_gate_w', 'new_m_ln2_g', 'new_m_ln2_b', 'new_v_attn_w_in', 'new_v_mla_q_norm', 'new_v_mla_w_q_b', 'new_v_mla_kv_norm', 'new_v_mla_w_kv_b', 'new_v_attn_w_out', 'new_v_s5_a_re', 'new_v_s5_a_im', 'new_v_s5_log_dt', 'new_v_s5_b_re', 'new_v_s5_b_im', 'new_v_s5_c_re', 'new_v_s5_c_im', 'new_v_s5_d', 'new_v_s5_w_glu', 'new_v_ln1_g', 'new_v_ln1_b', 'new_v_ffn_w_in', 'new_v_ffn_w_out', 'new_v_ple_w', 'new_v_ple_gate_w', 'new_v_ln2_g', 'new_v_ln2_b']
TWIN_LEAF_KINDS = {'loss': 'loss', 'grad_x': 'grad_x', 'grad_attn_w_in': 'grad_w', 'grad_mla_q_norm': 'grad_w', 'grad_mla_w_q_b': 'grad_w', 'grad_mla_kv_norm': 'grad_w', 'grad_mla_w_kv_b': 'grad_w', 'grad_attn_w_out': 'grad_w', 'grad_s5_a_re': 'grad_w', 'grad_s5_a_im': 'grad_w', 'grad_s5_log_dt': 'grad_w', 'grad_s5_b_re': 'grad_w', 'grad_s5_b_im': 'grad_w', 'grad_s5_c_re': 'grad_w', 'grad_s5_c_im': 'grad_w', 'grad_s5_d': 'grad_w', 'grad_s5_w_glu': 'grad_w', 'grad_ln1_g': 'grad_w', 'grad_ln1_b': 'grad_w', 'grad_ffn_w_in': 'grad_w', 'grad_ffn_w_out': 'grad_w', 'grad_ple_w': 'grad_w', 'grad_ple_gate_w': 'grad_w', 'grad_ln2_g': 'grad_w', 'grad_ln2_b': 'grad_w', 'delta_attn_w_in': 'delta_w', 'delta_mla_q_norm': 'delta_w', 'delta_mla_w_q_b': 'delta_w', 'delta_mla_kv_norm': 'delta_w', 'delta_mla_w_kv_b': 'delta_w', 'delta_attn_w_out': 'delta_w', 'delta_s5_a_re': 'delta_w', 'delta_s5_a_im': 'delta_w', 'delta_s5_log_dt': 'delta_w', 'delta_s5_b_re': 'delta_w', 'delta_s5_b_im': 'delta_w', 'delta_s5_c_re': 'delta_w', 'delta_s5_c_im': 'delta_w', 'delta_s5_d': 'delta_w', 'delta_s5_w_glu': 'delta_w', 'delta_ln1_g': 'delta_w', 'delta_ln1_b': 'delta_w', 'delta_ffn_w_in': 'delta_w', 'delta_ffn_w_out': 'delta_w', 'delta_ple_w': 'delta_w', 'delta_ple_gate_w': 'delta_w', 'delta_ln2_g': 'delta_w', 'delta_ln2_b': 'delta_w', 'new_m_attn_w_in': 'new_m', 'new_m_mla_q_norm': 'new_m', 'new_m_mla_w_q_b': 'new_m', 'new_m_mla_kv_norm': 'new_m', 'new_m_mla_w_kv_b': 'new_m', 'new_m_attn_w_out': 'new_m', 'new_m_s5_a_re': 'new_m', 'new_m_s5_a_im': 'new_m', 'new_m_s5_log_dt': 'new_m', 'new_m_s5_b_re': 'new_m', 'new_m_s5_b_im': 'new_m', 'new_m_s5_c_re': 'new_m', 'new_m_s5_c_im': 'new_m', 'new_m_s5_d': 'new_m', 'new_m_s5_w_glu': 'new_m', 'new_m_ln1_g': 'new_m', 'new_m_ln1_b': 'new_m', 'new_m_ffn_w_in': 'new_m', 'new_m_ffn_w_out': 'new_m', 'new_m_ple_w': 'new_m', 'new_m_ple_gate_w': 'new_m', 'new_m_ln2_g': 'new_m', 'new_m_ln2_b': 'new_m', 'new_v_attn_w_in': 'new_v', 'new_v_mla_q_norm': 'new_v', 'new_v_mla_w_q_b': 'new_v', 'new_v_mla_kv_norm': 'new_v', 'new_v_mla_w_kv_b': 'new_v', 'new_v_attn_w_out': 'new_v', 'new_v_s5_a_re': 'new_v', 'new_v_s5_a_im': 'new_v', 'new_v_s5_log_dt': 'new_v', 'new_v_s5_b_re': 'new_v', 'new_v_s5_b_im': 'new_v', 'new_v_s5_c_re': 'new_v', 'new_v_s5_c_im': 'new_v', 'new_v_s5_d': 'new_v', 'new_v_s5_w_glu': 'new_v', 'new_v_ln1_g': 'new_v', 'new_v_ln1_b': 'new_v', 'new_v_ffn_w_in': 'new_v', 'new_v_ffn_w_out': 'new_v', 'new_v_ple_w': 'new_v', 'new_v_ple_gate_w': 'new_v', 'new_v_ln2_g': 'new_v', 'new_v_ln2_b': 'new_v'}


def _forward(args):
    return _fwd_reference(*[args[k] for k in FWD_PARAMS])


def _output_shape():
    def fwd():
        inp = _fwd_setup_inputs(0)
        return _fwd_reference(*[inp[k] for k in FWD_PARAMS])
    out = _jax.eval_shape(fwd)
    return out.shape, out.dtype

N_MICROBATCH = 1
ADAM_LR = 0.001
ADAM_B1 = 0.9
ADAM_B2 = 0.999
ADAM_EPS = 1e-08
ADAM_WD = 0.01
ADAM_STEP = 10
PER_EXAMPLE_BATCH_AXIS = {'x': 0, 'p': 1, 'positions': 0, 'loss_target': 0}
SHARED_INPUTS = []
_WEIGHT_DTYPES = {'attn_w_in': _jnp.float32, 'mla_q_norm': _jnp.float32, 'mla_w_q_b': _jnp.float32, 'mla_kv_norm': _jnp.float32, 'mla_w_kv_b': _jnp.float32, 'attn_w_out': _jnp.float32, 's5_a_re': _jnp.float32, 's5_a_im': _jnp.float32, 's5_log_dt': _jnp.float32, 's5_b_re': _jnp.float32, 's5_b_im': _jnp.float32, 's5_c_re': _jnp.float32, 's5_c_im': _jnp.float32, 's5_d': _jnp.float32, 's5_w_glu': _jnp.float32, 'ln1_g': _jnp.float32, 'ln1_b': _jnp.float32, 'ffn_w_in': _jnp.float32, 'ffn_w_out': _jnp.float32, 'ple_w': _jnp.float32, 'ple_gate_w': _jnp.float32, 'ln2_g': _jnp.float32, 'ln2_b': _jnp.float32}
MOMENT_SCALE = {'attn_w_in': 1.286886e-02, 'mla_q_norm': 1.086143e-02, 'mla_w_q_b': 6.009641e-03, 'mla_kv_norm': 1.573477e-02, 'mla_w_kv_b': 7.614070e-03, 'attn_w_out': 2.601582e-02, 's5_a_re': 8.113116e-04, 's5_a_im': 7.982711e-04, 's5_log_dt': 3.748838e-01, 's5_b_re': 5.224155e-04, 's5_b_im': 5.139520e-04, 's5_c_re': 1.018848e-03, 's5_c_im': 1.022796e-03, 's5_d': 2.559863e-02, 's5_w_glu': 3.776928e-02, 'ln1_g': 8.503765e-01, 'ln1_b': 4.461411e-01, 'ffn_w_in': 1.464964e-02, 'ffn_w_out': 4.784977e-02, 'ple_w': 4.352498e-02, 'ple_gate_w': 1.696775e-02, 'ln2_g': 2.264610e+01, 'ln2_b': 1.313960e+00}


def _to_microbatches(a, axis):
    t = _jnp.moveaxis(a, axis, 0)
    t = t.reshape((N_MICROBATCH, t.shape[0] // N_MICROBATCH) + t.shape[1:])
    return _jnp.moveaxis(t, 1, axis + 1)


def setup_inputs(seed: int = 0) -> dict:
    inp = _fwd_setup_inputs(seed)
    key = _jax.random.fold_in(_jax.random.key(seed), 7919)
    shape, _ = _output_shape()
    out = dict(inp)
    out["loss_target"] = _jax.random.normal(_jax.random.fold_in(key, 0), shape, _jnp.float32)
    for i, name in enumerate(TWIN_WEIGHTS):
        w = inp[name].astype(_jnp.float32)
        if MOMENT_SCALE is None:
            s = _jnp.sqrt(_jnp.mean(_jnp.square(w)) + 1e-30)
        else:
            s = MOMENT_SCALE[name]
        km, kv = _jax.random.split(_jax.random.fold_in(key, i + 1))
        out[name] = w
        out["m_" + name] = s * _jax.random.normal(km, w.shape, _jnp.float32)
        out["v_" + name] = (s * s) * _jax.random.uniform(kv, w.shape, _jnp.float32, 0.5, 1.5)
    if N_MICROBATCH > 1:
        for name, axis in PER_EXAMPLE_BATCH_AXIS.items():
            out[name] = _to_microbatches(out[name], axis)
    return {'x': out['x'], 'p': out['p'], 'positions': out['positions'], 'attn_w_in': out['attn_w_in'], 'mla_q_norm': out['mla_q_norm'], 'mla_w_q_b': out['mla_w_q_b'], 'mla_kv_norm': out['mla_kv_norm'], 'mla_w_kv_b': out['mla_w_kv_b'], 'attn_w_out': out['attn_w_out'], 's5_a_re': out['s5_a_re'], 's5_a_im': out['s5_a_im'], 's5_log_dt': out['s5_log_dt'], 's5_b_re': out['s5_b_re'], 's5_b_im': out['s5_b_im'], 's5_c_re': out['s5_c_re'], 's5_c_im': out['s5_c_im'], 's5_d': out['s5_d'], 's5_w_glu': out['s5_w_glu'], 'ln1_g': out['ln1_g'], 'ln1_b': out['ln1_b'], 'ffn_w_in': out['ffn_w_in'], 'ffn_w_out': out['ffn_w_out'], 'ple_w': out['ple_w'], 'ple_gate_w': out['ple_gate_w'], 'ln2_g': out['ln2_g'], 'ln2_b': out['ln2_b'], 'loss_target': out['loss_target'], 'm_attn_w_in': out['m_attn_w_in'], 'm_mla_q_norm': out['m_mla_q_norm'], 'm_mla_w_q_b': out['m_mla_w_q_b'], 'm_mla_kv_norm': out['m_mla_kv_norm'], 'm_mla_w_kv_b': out['m_mla_w_kv_b'], 'm_attn_w_out': out['m_attn_w_out'], 'm_s5_a_re': out['m_s5_a_re'], 'm_s5_a_im': out['m_s5_a_im'], 'm_s5_log_dt': out['m_s5_log_dt'], 'm_s5_b_re': out['m_s5_b_re'], 'm_s5_b_im': out['m_s5_b_im'], 'm_s5_c_re': out['m_s5_c_re'], 'm_s5_c_im': out['m_s5_c_im'], 'm_s5_d': out['m_s5_d'], 'm_s5_w_glu': out['m_s5_w_glu'], 'm_ln1_g': out['m_ln1_g'], 'm_ln1_b': out['m_ln1_b'], 'm_ffn_w_in': out['m_ffn_w_in'], 'm_ffn_w_out': out['m_ffn_w_out'], 'm_ple_w': out['m_ple_w'], 'm_ple_gate_w': out['m_ple_gate_w'], 'm_ln2_g': out['m_ln2_g'], 'm_ln2_b': out['m_ln2_b'], 'v_attn_w_in': out['v_attn_w_in'], 'v_mla_q_norm': out['v_mla_q_norm'], 'v_mla_w_q_b': out['v_mla_w_q_b'], 'v_mla_kv_norm': out['v_mla_kv_norm'], 'v_mla_w_kv_b': out['v_mla_w_kv_b'], 'v_attn_w_out': out['v_attn_w_out'], 'v_s5_a_re': out['v_s5_a_re'], 'v_s5_a_im': out['v_s5_a_im'], 'v_s5_log_dt': out['v_s5_log_dt'], 'v_s5_b_re': out['v_s5_b_re'], 'v_s5_b_im': out['v_s5_b_im'], 'v_s5_c_re': out['v_s5_c_re'], 'v_s5_c_im': out['v_s5_c_im'], 'v_s5_d': out['v_s5_d'], 'v_s5_w_glu': out['v_s5_w_glu'], 'v_ln1_g': out['v_ln1_g'], 'v_ln1_b': out['v_ln1_b'], 'v_ffn_w_in': out['v_ffn_w_in'], 'v_ffn_w_out': out['v_ffn_w_out'], 'v_ple_w': out['v_ple_w'], 'v_ple_gate_w': out['v_ple_gate_w'], 'v_ln2_g': out['v_ln2_g'], 'v_ln2_b': out['v_ln2_b']}


def _loss(weights, diff, rest, loss_target):
    with _jax.named_scope("forward"):
        args = {**rest, TWIN_DIFF_INPUT: diff, **{k: w.astype(_WEIGHT_DTYPES[k]) for k, w in weights.items()}}
        y = _forward(args)
    with _jax.named_scope("loss_head"):
        err = _jnp.square(y.astype(_jnp.float32) - loss_target)
        return 0.5 * _jnp.sum(_jnp.mean(err, axis=-1)) if err.ndim else 0.5 * err


def _adamw(w, g, m, v):
    m = ADAM_B1 * m + (1.0 - ADAM_B1) * g
    v = ADAM_B2 * v + (1.0 - ADAM_B2) * _jnp.square(g)
    m_hat = m / (1.0 - ADAM_B1 ** ADAM_STEP)
    v_hat = v / (1.0 - ADAM_B2 ** ADAM_STEP)
    delta = -ADAM_LR * (m_hat / (_jnp.sqrt(v_hat) + ADAM_EPS) + ADAM_WD * w)
    return delta, m, v


def reference(x, p, positions, attn_w_in, mla_q_norm, mla_w_q_b, mla_kv_norm, mla_w_kv_b, attn_w_out, s5_a_re, s5_a_im, s5_log_dt, s5_b_re, s5_b_im, s5_c_re, s5_c_im, s5_d, s5_w_glu, ln1_g, ln1_b, ffn_w_in, ffn_w_out, ple_w, ple_gate_w, ln2_g, ln2_b, loss_target, m_attn_w_in, m_mla_q_norm, m_mla_w_q_b, m_mla_kv_norm, m_mla_w_kv_b, m_attn_w_out, m_s5_a_re, m_s5_a_im, m_s5_log_dt, m_s5_b_re, m_s5_b_im, m_s5_c_re, m_s5_c_im, m_s5_d, m_s5_w_glu, m_ln1_g, m_ln1_b, m_ffn_w_in, m_ffn_w_out, m_ple_w, m_ple_gate_w, m_ln2_g, m_ln2_b, v_attn_w_in, v_mla_q_norm, v_mla_w_q_b, v_mla_kv_norm, v_mla_w_kv_b, v_attn_w_out, v_s5_a_re, v_s5_a_im, v_s5_log_dt, v_s5_b_re, v_s5_b_im, v_s5_c_re, v_s5_c_im, v_s5_d, v_s5_w_glu, v_ln1_g, v_ln1_b, v_ffn_w_in, v_ffn_w_out, v_ple_w, v_ple_gate_w, v_ln2_g, v_ln2_b):
    given = dict(x=x, p=p, positions=positions, attn_w_in=attn_w_in, mla_q_norm=mla_q_norm, mla_w_q_b=mla_w_q_b, mla_kv_norm=mla_kv_norm, mla_w_kv_b=mla_w_kv_b, attn_w_out=attn_w_out, s5_a_re=s5_a_re, s5_a_im=s5_a_im, s5_log_dt=s5_log_dt, s5_b_re=s5_b_re, s5_b_im=s5_b_im, s5_c_re=s5_c_re, s5_c_im=s5_c_im, s5_d=s5_d, s5_w_glu=s5_w_glu, ln1_g=ln1_g, ln1_b=ln1_b, ffn_w_in=ffn_w_in, ffn_w_out=ffn_w_out, ple_w=ple_w, ple_gate_w=ple_gate_w, ln2_g=ln2_g, ln2_b=ln2_b, loss_target=loss_target, m_attn_w_in=m_attn_w_in, m_mla_q_norm=m_mla_q_norm, m_mla_w_q_b=m_mla_w_q_b, m_mla_kv_norm=m_mla_kv_norm, m_mla_w_kv_b=m_mla_w_kv_b, m_attn_w_out=m_attn_w_out, m_s5_a_re=m_s5_a_re, m_s5_a_im=m_s5_a_im, m_s5_log_dt=m_s5_log_dt, m_s5_b_re=m_s5_b_re, m_s5_b_im=m_s5_b_im, m_s5_c_re=m_s5_c_re, m_s5_c_im=m_s5_c_im, m_s5_d=m_s5_d, m_s5_w_glu=m_s5_w_glu, m_ln1_g=m_ln1_g, m_ln1_b=m_ln1_b, m_ffn_w_in=m_ffn_w_in, m_ffn_w_out=m_ffn_w_out, m_ple_w=m_ple_w, m_ple_gate_w=m_ple_gate_w, m_ln2_g=m_ln2_g, m_ln2_b=m_ln2_b, v_attn_w_in=v_attn_w_in, v_mla_q_norm=v_mla_q_norm, v_mla_w_q_b=v_mla_w_q_b, v_mla_kv_norm=v_mla_kv_norm, v_mla_w_kv_b=v_mla_w_kv_b, v_attn_w_out=v_attn_w_out, v_s5_a_re=v_s5_a_re, v_s5_a_im=v_s5_a_im, v_s5_log_dt=v_s5_log_dt, v_s5_b_re=v_s5_b_re, v_s5_b_im=v_s5_b_im, v_s5_c_re=v_s5_c_re, v_s5_c_im=v_s5_c_im, v_s5_d=v_s5_d, v_s5_w_glu=v_s5_w_glu, v_ln1_g=v_ln1_g, v_ln1_b=v_ln1_b, v_ffn_w_in=v_ffn_w_in, v_ffn_w_out=v_ffn_w_out, v_ple_w=v_ple_w, v_ple_gate_w=v_ple_gate_w, v_ln2_g=v_ln2_g, v_ln2_b=v_ln2_b)
    weights = {n: given[n] for n in TWIN_WEIGHTS}
    shared = {n: given[n] for n in SHARED_INPUTS}
    per_example = {n: given[n] for n in ['x', 'p', 'positions']}
    grad_fn = _jax.value_and_grad(_loss, argnums=(0, 1))

    def one_microbatch(ex, loss_target):
        ex = dict(ex)
        diff = ex.pop(TWIN_DIFF_INPUT)
        return grad_fn(weights, diff, {**shared, **ex}, loss_target)

    if N_MICROBATCH == 1:
        loss, (grad_w, grad_x) = one_microbatch(per_example, given["loss_target"])
    else:
        def body(carry, xs):
            loss_sum, grad_sum = carry
            l_k, (gw_k, gx_k) = one_microbatch(xs[0], xs[1])
            with _jax.named_scope("update"):
                return (loss_sum + l_k, _jax.tree.map(_jnp.add, grad_sum, gw_k)), gx_k

        init = (_jnp.zeros((), _jnp.float32), _jax.tree.map(_jnp.zeros_like, weights))
        (loss, grad_w), grad_x = _jax.lax.scan(body, init, (per_example, given["loss_target"]))
    with _jax.named_scope("update"):
        delta_w, new_m, new_v = {}, {}, {}
        for n in TWIN_WEIGHTS:
            delta_w[n], new_m[n], new_v[n] = _adamw(weights[n], grad_w[n], given["m_" + n], given["v_" + n])
    return (loss, grad_x, *[grad_w[n] for n in TWIN_WEIGHTS], *[delta_w[n] for n in TWIN_WEIGHTS],
            *[new_m[n] for n in TWIN_WEIGHTS], *[new_v[n] for n in TWIN_WEIGHTS])
```

```python
import functools
import math

import jax
import jax.numpy as jnp
from jax import lax
from jax.experimental import pallas as pl
from jax.experimental.pallas import tpu as pltpu

F32 = jnp.float32
BF16 = jnp.bfloat16
MESH = pl.DeviceIdType.MESH

LANES = 128
SUBLANES = 8
VMEM_LIMIT = 48 * 2 ** 20
N_CHIPS = 4
N_DEV = 8

NOPE = 128
ROPE = 64
VDIM = 128
QK_PAD = 256
DHD = 128
DIL_STEPS = 128
DIL_BRANCHES = ((128, 1), (512, 4), (2048, 16))
ROPE_THETA = 10000.0
S5_GROUP = 16
S5_STATE = 64
SLAB_GROUPS = LANES // S5_GROUP
SLAB_COLS = SLAB_GROUPS * S5_STATE
NEG = -1e30
LN_EPS = 1e-5
RMS_EPS = 1e-6

ADAM_LR = 0.001
ADAM_B1 = 0.9
ADAM_B2 = 0.999
ADAM_EPS = 1e-08
ADAM_WD = 0.01
ADAM_STEP = 10

NN = ((1,), (0,))
NT = ((1,), (1,))
TN = ((0,), (0,))


def _dot(a, b, dims):
    return lax.dot_general(a, b, (dims, ((), ())), preferred_element_type=F32)


def _bf(v):
    return v.astype(BF16)


def _pick(n, target, q=LANES, also=0):
    g = math.gcd(n, also) if also else n
    if g <= target and g == n:
        return n
    best = None
    for t in range(q, min(g, target) + 1, q):
        if g % t == 0:
            best = t
    assert best is not None, (n, target, q, also)
    return best


def _params(*sem):
    return pltpu.CompilerParams(dimension_semantics=sem, vmem_limit_bytes=VMEM_LIMIT)


def _sigmoid(v):
    return 1.0 / (1.0 + jnp.exp(-v))


def _matmul(name, a, b, form, out_dtype=F32, a_win=None, b_win=None, tm=1024, tn=1024, tk=512):
    c0, aw = a_win if a_win else (0, a.shape[1])
    if form == 'nt':
        assert b_win is None
        n, kdim = b.shape
        d0 = 0
    else:
        kdim = b.shape[0]
        d0, n = b_win if b_win else (0, b.shape[1])
    if form == 'tn':
        m = aw
        assert a.shape[0] == kdim, (name, a.shape, b.shape)
        tm = _pick(m, tm, also=c0)
        tk = _pick(kdim, tk)
        a_off = c0 // tm
    else:
        m = a.shape[0]
        assert aw == kdim, (name, a.shape, b.shape, a_win)
        tm = _pick(m, tm)
        tk = _pick(kdim, tk, also=c0)
        a_off = c0 // tk
    tn = _pick(n, tn, also=d0)
    b_off = d0 // tn
    nk = kdim // tk
    dims = {'nn': NN, 'nt': NT, 'tn': TN}[form]

    def body(a_ref, b_ref, o_ref, acc_ref):
        k = pl.program_id(2)

        @pl.when(k == 0)
        def _():
            acc_ref[...] = jnp.zeros_like(acc_ref)

        acc_ref[...] += _dot(_bf(a_ref[...]), _bf(b_ref[...]), dims)

        @pl.when(k == nk - 1)
        def _():
            o_ref[...] = acc_ref[...].astype(o_ref.dtype)

    if form == 'tn':
        a_spec = pl.BlockSpec((tk, tm), lambda i, j, k: (k, i + a_off))
    else:
        a_spec = pl.BlockSpec((tm, tk), lambda i, j, k: (i, k + a_off))
    if form == 'nt':
        b_spec = pl.BlockSpec((tn, tk), lambda i, j, k: (j, k))
    else:
        b_spec = pl.BlockSpec((tk, tn), lambda i, j, k: (k, j + b_off))
    return pl.pallas_call(
        body, name=name,
        grid=(m // tm, n // tn, nk),
        in_specs=[a_spec, b_spec],
        out_specs=pl.BlockSpec((tm, tn), lambda i, j, k: (i, j)),
        out_shape=jax.ShapeDtypeStruct((m, n), out_dtype),
        scratch_shapes=[pltpu.VMEM((tm, tn), F32)],
        compiler_params=_params("parallel", "parallel", "arbitrary"),
    )(a, b)


def _nat(tile, width, cb=0):
    return pl.BlockSpec((tile, width), lambda i: (i, cb))


def _perm(tile, width, seg_tiles, ncb=1, cb=0):
    return pl.BlockSpec((tile, width), lambda i: (i % seg_tiles, (i // seg_tiles) * ncb + cb))


def _whole(shape):
    return pl.BlockSpec(shape, lambda i: (0,) * len(shape))


def _perm_view(a):
    s, w = a.shape
    return a.reshape(s // SUBLANES, SUBLANES * w)


def _row_spec(a, layout, tile, width, ncb=1, cb=0):
    if layout == 'nat':
        return a, _nat(tile, width, cb)
    seg_tiles = a.shape[0] // SUBLANES // tile
    return _perm_view(a), _perm(tile, width, seg_tiles, ncb, cb)


def _ln_fwd(name, alpha, a, adds, gate, g, b, want_perm=False, tile=256):
    s, d = a.shape
    n_add = len(adds)
    has_gate = gate is not None

    def body(*refs):
        a_ref = refs[0]
        add_refs = refs[1:1 + n_add]
        pos = 1 + n_add
        if has_gate:
            val_ref, pre_ref = refs[pos], refs[pos + 1]
            pos += 2
        g_ref, b_ref = refs[pos], refs[pos + 1]
        outs = refs[pos + 2:]
        z = alpha * a_ref[...]
        for r in add_refs:
            z = z + r[...]
        if has_gate:
            z = z + val_ref[...] * _sigmoid(pre_ref[...])
        mu = jnp.mean(z, axis=-1, keepdims=True)
        zc = z - mu
        var = jnp.mean(zc * zc, axis=-1, keepdims=True)
        rstd = lax.rsqrt(var + LN_EPS)
        xhat = zc * rstd
        h = xhat * g_ref[...] + b_ref[...]
        outs[0][...] = h
        outs[1][...] = xhat
        outs[2][...] = jnp.broadcast_to(rstd, (tile, LANES))
        if want_perm:
            outs[3][...] = h

    ins, specs = [a], [_nat(tile, d)]
    for arr, layout in adds:
        x_, sp = _row_spec(arr, layout, tile, d)
        ins.append(x_)
        specs.append(sp)
    if has_gate:
        layout = gate[0]
        for arr, ncb, cb in gate[1:]:
            x_, sp = _row_spec(arr, layout, tile, d, ncb=ncb, cb=cb)
            ins.append(x_)
            specs.append(sp)
    ins += [g.reshape(1, d), b.reshape(1, d)]
    specs += [_whole((1, d)), _whole((1, d))]
    out_shape = [jax.ShapeDtypeStruct((s, d), F32), jax.ShapeDtypeStruct((s, d), F32),
                 jax.ShapeDtypeStruct((s, LANES), F32)]
    out_specs = [_nat(tile, d), _nat(tile, d), _nat(tile, LANES)]
    if want_perm:
        seg_tiles = s // SUBLANES // tile
        out_shape.append(jax.ShapeDtypeStruct((s // SUBLANES, SUBLANES * d), F32))
        out_specs.append(_perm(tile, d, seg_tiles))
    res = pl.pallas_call(
        body, name=name, grid=(s // tile,), in_specs=specs, out_specs=out_specs, out_shape=out_shape,
        compiler_params=_params("parallel"),
    )(*ins)
    if want_perm:
        return res[0], res[1], res[2], res[3].reshape(s, d)
    return res[0], res[1], res[2]


def _ln_bwd(name, dparts, xhat, rstd, g, gate=None, dz_perm=False, tile=256):
    s, d = xhat.shape
    n_part = len(dparts)
    coefs = [c for _, _, c in dparts]
    has_gate = gate is not None

    def body(*refs):
        part_refs = refs[:n_part]
        xhat_ref, rstd_ref, g_ref = refs[n_part:n_part + 3]
        pos = n_part + 3
        if has_gate:
            val_ref, pre_ref = refs[pos], refs[pos + 1]
            pos += 2
        outs = list(refs[pos:])
        dz_ref = outs.pop(0)
        dzp_ref = outs.pop(0) if dz_perm else None
        dgate_ref = outs.pop(0) if has_gate else None
        dg_ref, db_ref = outs
        dh = coefs[0] * part_refs[0][...]
        for c, r in zip(coefs[1:], part_refs[1:]):
            dh = dh + c * r[...]
        xh = xhat_ref[...]
        dxh = dh * g_ref[...]
        m1 = jnp.mean(dxh, axis=-1, keepdims=True)
        m2 = jnp.mean(dxh * xh, axis=-1, keepdims=True)
        dz = rstd_ref[:, 0:1] * (dxh - m1 - xh * m2)
        dz_ref[...] = dz
        if dz_perm:
            dzp_ref[...] = dz
        if has_gate:
            sg = _sigmoid(pre_ref[...])
            dval = dz * sg
            dpre = dz * val_ref[...] * sg * (1.0 - sg)
            dgate_ref[...] = jnp.concatenate([_bf(dval), _bf(dpre)], axis=1)

        @pl.when(pl.program_id(0) == 0)
        def _():
            dg_ref[...] = jnp.zeros_like(dg_ref)
            db_ref[...] = jnp.zeros_like(db_ref)

        dg_ref[0:1, :] += jnp.sum(dh * xh, axis=0, keepdims=True)
        db_ref[0:1, :] += jnp.sum(dh, axis=0, keepdims=True)

    ins, specs = [], []
    for arr, layout, _ in dparts:
        x_, sp = _row_spec(arr, layout, tile, d)
        ins.append(x_)
        specs.append(sp)
    ins += [xhat, rstd, g.reshape(1, d)]
    specs += [_nat(tile, d), _nat(tile, LANES), _whole((1, d))]
    gate_layout = None
    if has_gate:
        gate_layout = gate[0]
        for arr, ncb, cb in gate[1:]:
            x_, sp = _row_spec(arr, gate_layout, tile, d, ncb=ncb, cb=cb)
            ins.append(x_)
            specs.append(sp)
    seg_tiles = s // SUBLANES // tile
    out_shape = [jax.ShapeDtypeStruct((s, d), F32)]
    out_specs = [_nat(tile, d)]
    if dz_perm:
        out_shape.append(jax.ShapeDtypeStruct((s // SUBLANES, SUBLANES * d), F32))
        out_specs.append(_perm(tile, d, seg_tiles))
    if has_gate:
        if gate_layout == 'nat':
            out_shape.append(jax.ShapeDtypeStruct((s, 2 * d), BF16))
            out_specs.append(_nat(tile, 2 * d))
        else:
            out_shape.append(jax.ShapeDtypeStruct((s // SUBLANES, SUBLANES * 2 * d), BF16))
            out_specs.append(_perm(tile, 2 * d, seg_tiles))
    out_shape += [jax.ShapeDtypeStruct((SUBLANES, d), F32)] * 2
    out_specs += [_whole((SUBLANES, d))] * 2
    res = list(pl.pallas_call(
        body, name=name, grid=(s // tile,), in_specs=specs, out_specs=out_specs, out_shape=out_shape,
        compiler_params=_params("arbitrary"),
    )(*ins))
    out = [res.pop(0)]
    if dz_perm:
        out.append(res.pop(0).reshape(s, d))
    if has_gate:
        out.append(res.pop(0).reshape(s, 2 * d))
    out += [res[0][0], res[1][0]]
    return out


def _loss_partial(h, target, tile=256):
    s, d = h.shape

    def body(h_ref, t_ref, o_ref):
        @pl.when(pl.program_id(0) == 0)
        def _():
            o_ref[...] = jnp.zeros_like(o_ref)

        e = h_ref[...] - t_ref[...]
        sq = e * e
        part = sq[:, 0:LANES]
        for k in range(1, d // LANES):
            part = part + sq[:, k * LANES:(k + 1) * LANES]
        o_ref[0:1, :] += jnp.sum(part, axis=0, keepdims=True) * (0.5 / d)

    return pl.pallas_call(
        body, name="loss_partial", grid=(s // tile,), in_specs=[_nat(tile, d), _nat(tile, d)],
        out_specs=_whole((SUBLANES, LANES)), out_shape=jax.ShapeDtypeStruct((SUBLANES, LANES), F32),
        compiler_params=_params("arbitrary"),
    )(h, target)


def _swiglu_fwd(name, gu, tile=256):
    s, f2 = gu.shape
    f = f2 // 2
    cw = _pick(f, 1408)
    ncb = f // cw

    def body(g_ref, u_ref, o_ref):
        gg = g_ref[...]
        o_ref[...] = _bf(gg * _sigmoid(gg) * u_ref[...])

    return pl.pallas_call(
        body, name=name, grid=(s // tile, ncb),
        in_specs=[pl.BlockSpec((tile, cw), lambda i, j: (i, j)), pl.BlockSpec((tile, cw), lambda i, j: (i, j + ncb))],
        out_specs=pl.BlockSpec((tile, cw), lambda i, j: (i, j)),
        out_shape=jax.ShapeDtypeStruct((s, f), BF16), compiler_params=_params("parallel", "parallel"),
    )(gu, gu)


def _swiglu_bwd(name, gu, dact, tile=256):
    s, f2 = gu.shape
    f = f2 // 2
    cw = _pick(f, 1408)
    ncb = f // cw

    def body(g_ref, u_ref, da_ref, o_ref):
        gg = g_ref[...]
        sg = _sigmoid(gg)
        da = da_ref[...].astype(F32)
        silu = gg * sg
        dgv = da * u_ref[...] * (sg + silu * (1.0 - sg))
        duv = da * silu
        o_ref[...] = _bf(jnp.where(pl.program_id(1) >= ncb, duv, dgv))

    blk = lambda off: pl.BlockSpec((tile, cw), lambda i, j: (i, j % ncb + off))
    return pl.pallas_call(
        body, name=name, grid=(s // tile, 2 * ncb),
        in_specs=[blk(0), blk(ncb), blk(0)], out_specs=pl.BlockSpec((tile, cw), lambda i, j: (i, j)),
        out_shape=jax.ShapeDtypeStruct((s, f2), BF16), compiler_params=_params("parallel", "parallel"),
    )(gu, gu, dact)


def _rms_fwd(proj, ql, kvl, gq, gkv, tile=256):
    s = proj.shape[0]
    assert ql == kvl

    def body(q_ref, kv_ref, gq_ref, gkv_ref, o_ref):
        def nrm(x, gg):
            return x * lax.rsqrt(jnp.mean(x * x, axis=-1, keepdims=True) + RMS_EPS) * gg

        o_ref[...] = jnp.concatenate([_bf(nrm(q_ref[...], gq_ref[...])), _bf(nrm(kv_ref[...], gkv_ref[...]))], axis=1)

    return pl.pallas_call(
        body, name="mla_rms_fwd", grid=(s // tile,),
        in_specs=[_nat(tile, ql, 0), _nat(tile, kvl, 1), _whole((1, ql)), _whole((1, kvl))],
        out_specs=_nat(tile, ql + kvl), out_shape=jax.ShapeDtypeStruct((s, ql + kvl), BF16),
        compiler_params=_params("parallel"),
    )(proj, proj, gq.reshape(1, ql), gkv.reshape(1, kvl))


def _rope_coeffs(pos, invf):
    ang = pos * invf
    cs, sn = jnp.cos(ang), jnp.sin(ang)
    lane = lax.broadcasted_iota(jnp.int32, ang.shape, 1)
    half = ROPE // 2
    c = jnp.where(lane < ROPE, cs, 0.0)
    sa = jnp.where(lane < half, -sn, 0.0)
    sb = jnp.where((lane >= half) & (lane < ROPE), sn, 0.0)
    return c, sa, sb


def _rope_prep(q_raw, kv, proj, kpe_cb, pos, invf, heads, tile=256):
    s = q_raw.shape[0]
    half = ROPE // 2

    def body(q_ref, kv_ref, kpe_ref, pos_ref, invf_ref, qf_ref, kf_ref, v_ref):
        c, sa, sb = _rope_coeffs(pos_ref[...], invf_ref[...])

        def rope(t):
            return t * c + pltpu.roll(t, LANES - half, 1) * sa + pltpu.roll(t, half, 1) * sb

        kr = _bf(rope(kpe_ref[...]))
        for hh in range(heads):
            o = hh * QK_PAD
            qf_ref[:, o:o + NOPE] = _bf(q_ref[:, o:o + NOPE])
            qf_ref[:, o + NOPE:o + QK_PAD] = _bf(rope(q_ref[:, o + NOPE:o + QK_PAD]))
            kf_ref[:, o:o + NOPE] = _bf(kv_ref[:, o:o + NOPE])
            kf_ref[:, o + NOPE:o + QK_PAD] = kr
            v_ref[:, hh * VDIM:(hh + 1) * VDIM] = _bf(kv_ref[:, o + NOPE:o + QK_PAD])

    w = heads * QK_PAD
    return pl.pallas_call(
        body, name="mla_rope_prep", grid=(s // tile,),
        in_specs=[_nat(tile, w), _nat(tile, w), _nat(tile, LANES, kpe_cb), _nat(tile, 1), _whole((1, LANES))],
        out_specs=[_nat(tile, w), _nat(tile, w), _nat(tile, heads * VDIM)],
        out_shape=[jax.ShapeDtypeStruct((s, w), BF16), jax.ShapeDtypeStruct((s, w), BF16),
                   jax.ShapeDtypeStruct((s, heads * VDIM), BF16)],
        compiler_params=_params("parallel"),
    )(q_raw, kv, proj, pos, invf)


def _rope_unprep(dqf, dkf, dv, pos, invf, heads, tile=256):
    s = dqf.shape[0]
    half = ROPE // 2

    def body(dq_ref, dk_ref, dv_ref, pos_ref, invf_ref, dqr_ref, dkv_ref, dkpe_ref):
        c, sa, sb = _rope_coeffs(pos_ref[...], invf_ref[...])

        def unrope(gt):
            return gt * c + pltpu.roll(gt * sa, half, 1) + pltpu.roll(gt * sb, LANES - half, 1)

        dkpe = jnp.zeros((tile, LANES), F32)
        for hh in range(heads):
            o = hh * QK_PAD
            dqr_ref[:, o:o + NOPE] = _bf(dq_ref[:, o:o + NOPE])
            dqr_ref[:, o + NOPE:o + QK_PAD] = _bf(unrope(dq_ref[:, o + NOPE:o + QK_PAD]))
            dkv_ref[:, o:o + NOPE] = _bf(dk_ref[:, o:o + NOPE])
            dkv_ref[:, o + NOPE:o + QK_PAD] = _bf(dv_ref[:, hh * VDIM:(hh + 1) * VDIM])
            dkpe = dkpe + dk_ref[:, o + NOPE:o + QK_PAD]
        dkpe_ref[...] = unrope(dkpe)

    w = heads * QK_PAD
    return pl.pallas_call(
        body, name="mla_rope_unprep", grid=(s // tile,),
        in_specs=[_nat(tile, w), _nat(tile, w), _nat(tile, heads * VDIM), _nat(tile, 1), _whole((1, LANES))],
        out_specs=[_nat(tile, w), _nat(tile, w), _nat(tile, LANES)],
        out_shape=[jax.ShapeDtypeStruct((s, w), BF16), jax.ShapeDtypeStruct((s, w), BF16),
                   jax.ShapeDtypeStruct((s, LANES), F32)],
        compiler_params=_params("parallel"),
    )(dqf, dkf, dv, pos, invf)


def _causal_scores(q, k, i, j, t, scale):
    sc = _dot(q, k, NT) * scale
    row = lax.broadcasted_iota(jnp.int32, (t, t), 0) + i * t
    col = lax.broadcasted_iota(jnp.int32, (t, t), 1) + j * t
    return jnp.where(col <= row, sc, NEG)


def _mla_fwd(qf, kf, v, heads, t=512):
    s = qf.shape[0]
    t = min(t, s)
    nq = s // t
    scale = (NOPE + ROPE) ** -0.5

    def body(q_ref, k_ref, v_ref, o_ref, lse_ref, m_ref, l_ref, acc_ref):
        i, j = pl.program_id(1), pl.program_id(2)

        @pl.when(j == 0)
        def _():
            m_ref[...] = jnp.full_like(m_ref, NEG)
            l_ref[...] = jnp.zeros_like(l_ref)
            acc_ref[...] = jnp.zeros_like(acc_ref)

        @pl.when(j <= i)
        def _():
            sc = _causal_scores(q_ref[...], k_ref[...], i, j, t, scale)
            m_prev = m_ref[:, 0:1]
            m_new = jnp.maximum(m_prev, jnp.max(sc, axis=-1, keepdims=True))
            corr = jnp.exp(m_prev - m_new)
            p = jnp.exp(sc - m_new)
            l_new = corr * l_ref[:, 0:1] + jnp.sum(p, axis=-1, keepdims=True)
            acc_ref[...] = corr * acc_ref[...] + _dot(_bf(p), v_ref[...], NN)
            m_ref[...] = jnp.broadcast_to(m_new, (t, LANES))
            l_ref[...] = jnp.broadcast_to(l_new, (t, LANES))

        @pl.when(j == i)
        def _():
            o_ref[...] = acc_ref[...] / l_ref[...]
            lse_ref[...] = m_ref[...] + jnp.log(l_ref[...])

    qs = lambda w: pl.BlockSpec((t, w), lambda h, i, j: (i, h))
    ks = lambda w: pl.BlockSpec((t, w), lambda h, i, j: (jnp.minimum(j, i), h))
    return pl.pallas_call(
        body, name="mla_flash_fwd", grid=(heads, nq, nq),
        in_specs=[qs(QK_PAD), ks(QK_PAD), ks(VDIM)], out_specs=[qs(VDIM), qs(LANES)],
        out_shape=[jax.ShapeDtypeStruct((s, heads * VDIM), F32), jax.ShapeDtypeStruct((s, heads * LANES), F32)],
        scratch_shapes=[pltpu.VMEM((t, LANES), F32), pltpu.VMEM((t, LANES), F32), pltpu.VMEM((t, VDIM), F32)],
        compiler_params=_params("parallel", "parallel", "arbitrary"),
    )(qf, kf, v)


def _mla_bwd_dq(qf, kf, v, do, lse, delta, heads, do_cb0, t=512):
    s = qf.shape[0]
    t = min(t, s)
    nq = s // t
    scale = (NOPE + ROPE) ** -0.5

    def body(q_ref, k_ref, v_ref, do_ref, lse_ref, dl_ref, dq_ref, acc_ref):
        i, j = pl.program_id(1), pl.program_id(2)

        @pl.when(j == 0)
        def _():
            acc_ref[...] = jnp.zeros_like(acc_ref)

        @pl.when(j <= i)
        def _():
            sc = _causal_scores(q_ref[...], k_ref[...], i, j, t, scale)
            p = jnp.exp(sc - lse_ref[:, 0:1])
            dp = _dot(do_ref[...], v_ref[...], NT)
            ds = p * (dp - dl_ref[:, 0:1]) * scale
            acc_ref[...] += _dot(_bf(ds), k_ref[...], NN)

        @pl.when(j == i)
        def _():
            dq_ref[...] = acc_ref[...]

    qs = lambda w, off=0: pl.BlockSpec((t, w), lambda h, i, j: (i, h + off))
    ks = lambda w: pl.BlockSpec((t, w), lambda h, i, j: (jnp.minimum(j, i), h))
    return pl.pallas_call(
        body, name="mla_flash_bwd_dq", grid=(heads, nq, nq),
        in_specs=[qs(QK_PAD), ks(QK_PAD), ks(VDIM), qs(VDIM, do_cb0), qs(LANES), qs(LANES, do_cb0)],
        out_specs=qs(QK_PAD), out_shape=jax.ShapeDtypeStruct((s, heads * QK_PAD), F32),
        scratch_shapes=[pltpu.VMEM((t, QK_PAD), F32)],
        compiler_params=_params("parallel", "parallel", "arbitrary"),
    )(qf, kf, v, do, lse, delta)


def _mla_bwd_dkv(qf, kf, v, do, lse, delta, heads, do_cb0, t=512):
    s = qf.shape[0]
    t = min(t, s)
    nq = s // t
    scale = (NOPE + ROPE) ** -0.5

    def body(q_ref, k_ref, v_ref, do_ref, lse_ref, dl_ref, dk_ref, dv_ref, dk_acc, dv_acc):
        j, i = pl.program_id(1), pl.program_id(2)

        @pl.when(i == 0)
        def _():
            dk_acc[...] = jnp.zeros_like(dk_acc)
            dv_acc[...] = jnp.zeros_like(dv_acc)

        @pl.when(i >= j)
        def _():
            sc = _causal_scores(q_ref[...], k_ref[...], i, j, t, scale)
            p = jnp.exp(sc - lse_ref[:, 0:1])
            dp = _dot(do_ref[...], v_ref[...], NT)
            ds = p * (dp - dl_ref[:, 0:1]) * scale
            dv_acc[...] += _dot(_bf(p), do_ref[...], TN)
            dk_acc[...] += _dot(_bf(ds), q_ref[...], TN)

        @pl.when(i == nq - 1)
        def _():
            dk_ref[...] = dk_acc[...]
            dv_ref[...] = dv_acc[...]

    qs = lambda w, off=0: pl.BlockSpec((t, w), lambda h, j, i: (jnp.maximum(i, j), h + off))
    ks = lambda w: pl.BlockSpec((t, w), lambda h, j, i: (j, h))
    return pl.pallas_call(
        body, name="mla_flash_bwd_dkv", grid=(heads, nq, nq),
        in_specs=[qs(QK_PAD), ks(QK_PAD), ks(VDIM), qs(VDIM, do_cb0), qs(LANES), qs(LANES, do_cb0)],
        out_specs=[ks(QK_PAD), ks(VDIM)],
        out_shape=[jax.ShapeDtypeStruct((s, heads * QK_PAD), F32), jax.ShapeDtypeStruct((s, heads * VDIM), F32)],
        scratch_shapes=[pltpu.VMEM((t, QK_PAD), F32), pltpu.VMEM((t, VDIM), F32)],
        compiler_params=_params("parallel", "parallel", "arbitrary"),
    )(qf, kf, v, do, lse, delta)


def _band_mask(tq, first_block):
    row = lax.broadcasted_iota(jnp.int32, (tq, DIL_STEPS + tq), 0)
    col = lax.broadcasted_iota(jnp.int32, (tq, DIL_STEPS + tq), 1)
    dist = row + DIL_STEPS - col
    valid = (dist >= 0) & (dist <= DIL_STEPS) & (jnp.logical_not(first_block) | (col >= DIL_STEPS))
    return dist, valid


def _dil_scores(q, kp, kc, slope, dil, tq, first_block):
    sc = jnp.concatenate([_dot(q, kp, NT), _dot(q, kc, NT)], axis=1) * (DHD ** -0.5)
    dist, valid = _band_mask(tq, first_block)
    return jnp.where(valid, sc - slope * (dil * dist).astype(F32), NEG)


def _dil_specs(proj_w, dh, tq):
    pwb = proj_w // LANES
    r_of = lambda cb: cb // dh
    h_of = lambda cb: cb % dh
    cur = lambda off: pl.BlockSpec((tq, DHD), lambda cb, i: (i, r_of(cb) * pwb + off + h_of(cb)))
    prev = lambda off: pl.BlockSpec(
        (DIL_STEPS, DHD), lambda cb, i: (jnp.maximum(i * (tq // DIL_STEPS) - 1, 0), r_of(cb) * pwb + off + h_of(cb)))
    return cur, prev


def _dil_fwd(name, proj, slopes, dil, dh, q_cb, tq=512):
    s, pw = proj.shape
    l = s // dil
    tq = min(tq, l)
    nb = l // tq
    k_cb, v_cb = q_cb + dh, q_cb + 2 * dh
    cur, prev = _dil_specs(pw, dh, tq)
    pv = proj.reshape(l, dil * pw)

    def body(q_ref, kc_ref, kp_ref, vc_ref, vp_ref, sl_ref, o_ref, lse_ref):
        i = pl.program_id(1)
        sc = _dil_scores(_bf(q_ref[...]), _bf(kp_ref[...]), _bf(kc_ref[...]), sl_ref[0:1, 0:1], dil, tq, i == 0)
        m = jnp.max(sc, axis=-1, keepdims=True)
        e = jnp.exp(sc - m)
        lsum = jnp.sum(e, axis=-1, keepdims=True)
        pn = e / lsum
        o_ref[...] = (_dot(_bf(pn[:, :DIL_STEPS]), _bf(vp_ref[...]), NN)
                      + _dot(_bf(pn[:, DIL_STEPS:]), _bf(vc_ref[...]), NN))
        lse_ref[...] = jnp.broadcast_to(m + jnp.log(lsum), (tq, LANES))

    ospec = pl.BlockSpec((tq, DHD), lambda cb, i: (i, cb))
    o, lse = pl.pallas_call(
        body, name=name, grid=(dil * dh, nb),
        in_specs=[cur(q_cb), cur(k_cb), prev(k_cb), cur(v_cb), prev(v_cb),
                  pl.BlockSpec((SUBLANES, LANES), lambda cb, i: (cb % dh, 0))],
        out_specs=[ospec, ospec],
        out_shape=[jax.ShapeDtypeStruct((l, dil * dh * DHD), F32)] * 2,
        compiler_params=_params("parallel", "parallel"),
    )(pv, pv, pv, pv, pv, slopes)
    return o.reshape(s, dh * DHD), lse.reshape(s, dh * DHD)


def _dil_bwd_dq(name, proj, slopes, do, lse, delta, dil, dh, q_cb, b_cb0, tq=512):
    s, pw = proj.shape
    mixw = do.shape[1]
    l = s // dil
    tq = min(tq, l)
    nb = l // tq
    k_cb, v_cb = q_cb + dh, q_cb + 2 * dh
    cur, prev = _dil_specs(pw, dh, tq)
    pv = proj.reshape(l, dil * pw)
    mb = mixw // LANES
    mspec = pl.BlockSpec((tq, DHD), lambda cb, i: (i, (cb // dh) * mb + b_cb0 + cb % dh))
    ospec = pl.BlockSpec((tq, DHD), lambda cb, i: (i, cb))

    def body(q_ref, kc_ref, kp_ref, vc_ref, vp_ref, sl_ref, do_ref, lse_ref, dl_ref, dq_ref):
        i = pl.program_id(1)
        kp, kc = _bf(kp_ref[...]), _bf(kc_ref[...])
        sc = _dil_scores(_bf(q_ref[...]), kp, kc, sl_ref[0:1, 0:1], dil, tq, i == 0)
        p = jnp.exp(sc - lse_ref[:, 0:1])
        dob = do_ref[...]
        dp = jnp.concatenate([_dot(dob, _bf(vp_ref[...]), NT), _dot(dob, _bf(vc_ref[...]), NT)], axis=1)
        ds = _bf(p * (dp - dl_ref[:, 0:1]) * (DHD ** -0.5))
        dq_ref[...] = _dot(ds[:, :DIL_STEPS], kp, NN) + _dot(ds[:, DIL_STEPS:], kc, NN)

    dq = pl.pallas_call(
        body, name=name, grid=(dil * dh, nb),
        in_specs=[cur(q_cb), cur(k_cb), prev(k_cb), cur(v_cb), prev(v_cb),
                  pl.BlockSpec((SUBLANES, LANES), lambda cb, i: (cb % dh, 0)), mspec, ospec, mspec],
        out_specs=ospec, out_shape=jax.ShapeDtypeStruct((l, dil * dh * DHD), F32),
        compiler_params=_params("parallel", "parallel"),
    )(pv, pv, pv, pv, pv, slopes, do.reshape(l, dil * mixw), lse.reshape(l, dil * dh * DHD), delta.reshape(l, dil * mixw))
    return dq.reshape(s, dh * DHD)


def _dil_bwd_dkv(name, proj, slopes, do, lse, delta, dil, dh, q_cb, b_cb0, tk=512):
    s, pw = proj.shape
    mixw = do.shape[1]
    l = s // dil
    tk = min(tk, l)
    nb = l // tk
    k_cb, v_cb = q_cb + dh, q_cb + 2 * dh
    pwb, mb = pw // LANES, mixw // LANES
    sub = tk // DIL_STEPS
    last128 = l // DIL_STEPS - 1
    pv = proj.reshape(l, dil * pw)

    def cur(width_blocks, off):
        return pl.BlockSpec((tk, DHD), lambda cb, j: (j, (cb // dh) * width_blocks + off + cb % dh))

    def nxt(width_blocks, off):
        return pl.BlockSpec((DIL_STEPS, DHD), lambda cb, j: (jnp.minimum((j + 1) * sub, last128),
                                                               (cb // dh) * width_blocks + off + cb % dh))

    ocur = pl.BlockSpec((tk, DHD), lambda cb, j: (j, cb))
    onxt = pl.BlockSpec((DIL_STEPS, DHD), lambda cb, j: (jnp.minimum((j + 1) * sub, last128), cb))

    def body(k_ref, v_ref, qc_ref, qn_ref, sl_ref, doc_ref, don_ref, lsec_ref, lsen_ref, dlc_ref, dln_ref,
             dk_ref, dv_ref):
        j = pl.program_id(1)
        slope = sl_ref[0:1, 0:1]
        scale = DHD ** -0.5
        k, v = _bf(k_ref[...]), _bf(v_ref[...])
        qc = _bf(qc_ref[...])
        row = lax.broadcasted_iota(jnp.int32, (tk, tk), 0)
        col = lax.broadcasted_iota(jnp.int32, (tk, tk), 1)
        dist = row - col
        valid = (dist >= 0) & (dist <= DIL_STEPS)
        sc = jnp.where(valid, _dot(qc, k, NT) * scale - slope * (dil * dist).astype(F32), NEG)
        p = jnp.exp(sc - lsec_ref[:, 0:1])
        doc = doc_ref[...]
        ds = _bf(p * (_dot(doc, v, NT) - dlc_ref[:, 0:1]) * scale)
        dv_ref[...] = _dot(_bf(p), doc, TN)
        dk_ref[...] = _dot(ds, qc, TN)
        kl, vl = k[tk - DIL_STEPS:, :], v[tk - DIL_STEPS:, :]
        qn = _bf(qn_ref[...])
        row = lax.broadcasted_iota(jnp.int32, (DIL_STEPS, DIL_STEPS), 0)
        col = lax.broadcasted_iota(jnp.int32, (DIL_STEPS, DIL_STEPS), 1)
        dist = DIL_STEPS + row - col
        valid = (dist <= DIL_STEPS) & (j < nb - 1)
        sc = jnp.where(valid, _dot(qn, kl, NT) * scale - slope * (dil * dist).astype(F32), NEG)
        p = jnp.exp(sc - lsen_ref[:, 0:1])
        don = don_ref[...]
        ds = _bf(p * (_dot(don, vl, NT) - dln_ref[:, 0:1]) * scale)
        dv_ref[tk - DIL_STEPS:, :] += _dot(_bf(p), don, TN)
        dk_ref[tk - DIL_STEPS:, :] += _dot(ds, qn, TN)

    dov = do.reshape(l, dil * mixw)
    dlv = delta.reshape(l, dil * mixw)
    lsv = lse.reshape(l, dil * dh * DHD)
    dk, dv = pl.pallas_call(
        body, name=name, grid=(dil * dh, nb),
        in_specs=[cur(pwb, k_cb), cur(pwb, v_cb), cur(pwb, q_cb), nxt(pwb, q_cb),
                  pl.BlockSpec((SUBLANES, LANES), lambda cb, j: (cb % dh, 0)),
                  cur(mb, b_cb0), nxt(mb, b_cb0), ocur, onxt, cur(mb, b_cb0), nxt(mb, b_cb0)],
        out_specs=[ocur, ocur], out_shape=[jax.ShapeDtypeStruct((l, dil * dh * DHD), F32)] * 2,
        compiler_params=_params("parallel", "parallel"),
    )(pv, pv, pv, pv, slopes, dov, dov, lsv, lsv, dlv, dlv)
    return dk.reshape(s, dh * DHD), dv.reshape(s, dh * DHD)


def _dil_merge(out_a, outs, lses, tile=256):
    s, wa = out_a.shape
    wb = outs[0].shape[1]
    nbr = len(outs)

    def body(*refs):
        a_ref = refs[0]
        o_refs, l_refs = refs[1:1 + nbr], refs[1 + nbr:1 + 2 * nbr]
        att_ref, ob_ref, lse_ref = refs[1 + 2 * nbr:]
        ls = [r[...] for r in l_refs]
        m = ls[0]
        for x_ in ls[1:]:
            m = jnp.maximum(m, x_)
        es = [jnp.exp(x_ - m) for x_ in ls]
        tot = es[0]
        for e in es[1:]:
            tot = tot + e
        ob = (es[0] / tot) * o_refs[0][...]
        for e, r in zip(es[1:], o_refs[1:]):
            ob = ob + (e / tot) * r[...]
        ob_ref[...] = ob
        lse_ref[...] = m + jnp.log(tot)
        att_ref[...] = jnp.concatenate([_bf(a_ref[...]), _bf(ob)], axis=1)

    return pl.pallas_call(
        body, name="dil_merge", grid=(s // tile,),
        in_specs=[_nat(tile, wa)] + [_nat(tile, wb)] * (2 * nbr),
        out_specs=[_nat(tile, wa + wb), _nat(tile, wb), _nat(tile, wb)],
        out_shape=[jax.ShapeDtypeStruct((s, wa + wb), BF16), jax.ShapeDtypeStruct((s, wb), F32),
                   jax.ShapeDtypeStruct((s, wb), F32)],
        compiler_params=_params("parallel"),
    )(out_a, *outs, *lses)


def _attn_bwd_prep(datt, out_a, out_b, tile=256):
    s, mixw = datt.shape
    wa = out_a.shape[1]

    def body(d_ref, a_ref, b_ref, do_ref, dl_ref):
        d = d_ref[...]
        do_ref[...] = _bf(d)
        prod = d * jnp.concatenate([a_ref[...], b_ref[...]], axis=1)
        for hh in range(mixw // LANES):
            sl = slice(hh * LANES, (hh + 1) * LANES)
            dl_ref[:, sl] = jnp.broadcast_to(jnp.sum(prod[:, sl], axis=-1, keepdims=True), (tile, LANES))

    return pl.pallas_call(
        body, name="attn_bwd_prep", grid=(s // tile,),
        in_specs=[_nat(tile, mixw), _nat(tile, wa), _nat(tile, mixw - wa)],
        out_specs=[_nat(tile, mixw), _nat(tile, mixw)],
        out_shape=[jax.ShapeDtypeStruct((s, mixw), BF16), jax.ShapeDtypeStruct((s, mixw), F32)],
        compiler_params=_params("parallel"),
    )(datt, out_a, out_b)


def _dproj_assemble(proj, dnq, dnkv, dkpe, dqs, dks, dvs, gq, gkv, ql, tile=256):
    s, pw = proj.shape
    dw = dqs[0].shape[1]
    nbr = len(dqs)

    def body(*refs):
        ql_ref, kvl_ref, dnq_ref, dnkv_ref, dkpe_ref = refs[:5]
        br = refs[5:5 + 3 * nbr]
        gq_ref, gkv_ref = refs[5 + 3 * nbr:7 + 3 * nbr]
        dp_ref, dgq_ref, dgkv_ref = refs[7 + 3 * nbr:]

        @pl.when(pl.program_id(0) == 0)
        def _():
            dgq_ref[...] = jnp.zeros_like(dgq_ref)
            dgkv_ref[...] = jnp.zeros_like(dgkv_ref)

        def rms_bwd(x, dy, gg, dg_ref):
            r = lax.rsqrt(jnp.mean(x * x, axis=-1, keepdims=True) + RMS_EPS)
            xh = x * r
            dxh = dy * gg
            dg_ref[0:1, :] += jnp.sum(dy * xh, axis=0, keepdims=True)
            return r * (dxh - xh * jnp.mean(dxh * xh, axis=-1, keepdims=True))

        pieces = [_bf(rms_bwd(ql_ref[...], dnq_ref[...], gq_ref[...], dgq_ref)),
                  _bf(rms_bwd(kvl_ref[...], dnkv_ref[...], gkv_ref[...], dgkv_ref)),
                  _bf(dkpe_ref[...])]
        for k in range(3):
            acc = br[k * nbr][...]
            for r in br[k * nbr + 1:(k + 1) * nbr]:
                acc = acc + r[...]
            pieces.append(_bf(acc))
        dp_ref[...] = jnp.concatenate(pieces, axis=1)

    res = pl.pallas_call(
        body, name="dproj_assemble", grid=(s // tile,),
        in_specs=[_nat(tile, ql, 0), _nat(tile, ql, 1), _nat(tile, ql), _nat(tile, ql), _nat(tile, LANES)]
        + [_nat(tile, dw)] * (3 * nbr) + [_whole((1, ql)), _whole((1, ql))],
        out_specs=[_nat(tile, pw), _whole((SUBLANES, ql)), _whole((SUBLANES, ql))],
        out_shape=[jax.ShapeDtypeStruct((s, pw), BF16), jax.ShapeDtypeStruct((SUBLANES, ql), F32),
                   jax.ShapeDtypeStruct((SUBLANES, ql), F32)],
        compiler_params=_params("arbitrary"),
    )(proj, proj, dnq, dnkv, dkpe, *dqs, *dks, *dvs, gq.reshape(1, ql), gkv.reshape(1, ql))
    return res[0], res[1][0], res[2][0]


def _axpy(name, alpha, a, b, tile=256):
    s, d = a.shape

    def body(a_ref, b_ref, o_ref):
        o_ref[...] = alpha * a_ref[...] + b_ref[...]

    return pl.pallas_call(
        body, name=name, grid=(s // tile,), in_specs=[_nat(tile, d), _nat(tile, d)], out_specs=_nat(tile, d),
        out_shape=jax.ShapeDtypeStruct((s, d), F32), compiler_params=_params("parallel"),
    )(a, b)


def _cmul(ar, ai, br, bi):
    return ar * br - ai * bi, ar * bi + ai * br


def _s5_discretise(a_re, a_im, log_dt, b_re, b_im, n_sq):
    shape = a_re.shape

    def body(ar_ref, ai_ref, ldt_ref, br_ref, bi_ref, abr_ref, abi_ref, apr_ref, api_ref, bbr_ref, bbi_ref):
        ar, ai = ar_ref[...], ai_ref[...]
        dt = jnp.exp(ldt_ref[...])
        e = jnp.exp(ar * dt)
        abr, abi = e * jnp.cos(ai * dt), e * jnp.sin(ai * dt)
        den = ar * ar + ai * ai
        qr = ((abr - 1.0) * ar + abi * ai) / den
        qi = (abi * ar - (abr - 1.0) * ai) / den
        bbr, bbi = _cmul(qr, qi, br_ref[...], bi_ref[...])
        abr_ref[...], abi_ref[...] = abr, abi
        bbr_ref[...], bbi_ref[...] = bbr, bbi
        pr, pi = abr, abi
        for _ in range(n_sq):
            pr, pi = _cmul(pr, pi, pr, pi)
        apr_ref[...], api_ref[...] = pr, pi

    return pl.pallas_call(
        body, name="s5_discretise", out_shape=[jax.ShapeDtypeStruct(shape, F32)] * 6,
        compiler_params=pltpu.CompilerParams(vmem_limit_bytes=VMEM_LIMIT),
    )(a_re, a_im, log_dt, b_re, b_im)


def _s5_discretise_bwd(a16, b16, ag, gab, gbb):
    rows, p = a16[0].shape
    g = rows // S5_GROUP

    def disc(ar, ai, ldt):
        dt = jnp.exp(ldt)
        e = jnp.exp(ar * dt)
        abr, abi = e * jnp.cos(ai * dt), e * jnp.sin(ai * dt)
        den = ar * ar + ai * ai
        inv_r, inv_i = ar / den, -ai / den
        qr, qi = _cmul(abr - 1.0, abi, inv_r, inv_i)
        return dt, abr, abi, inv_r, inv_i, qr, qi

    def body(ar16_ref, ai16_ref, ldt16_ref, br_ref, bi_ref, ar_ref, ai_ref, ldt_ref, gar_ref, gai_ref, gbr_ref, gbi_ref,
             dar_ref, dai_ref, dldt_ref, dbr_ref, dbi_ref):
        _, _, _, _, _, qr16, qi16 = disc(ar16_ref[...], ai16_ref[...], ldt16_ref[...])
        gbr, gbi = gbr_ref[...], gbi_ref[...]
        dbr_ref[...], dbi_ref[...] = _cmul(qr16, -qi16, gbr, gbi)
        cr, ci = _cmul(br_ref[...], -bi_ref[...], gbr, gbi)
        gqr = jnp.sum(cr.reshape(g, S5_GROUP, p), axis=1)
        gqi = jnp.sum(ci.reshape(g, S5_GROUP, p), axis=1)
        ar, ai = ar_ref[...], ai_ref[...]
        dt, abr, abi, inv_r, inv_i, qr, qi = disc(ar, ai, ldt_ref[...])
        t_r, t_i = _cmul(inv_r, -inv_i, gqr, gqi)
        gab_r = gar_ref[...] + t_r
        gab_i = gai_ref[...] + t_i
        qa_r, qa_i = _cmul(qr, qi, inv_r, inv_i)
        a1_r, a1_i = _cmul(qa_r, -qa_i, gqr, gqi)
        gl_r, gl_i = _cmul(abr, -abi, gab_r, gab_i)
        dar_ref[...] = dt * gl_r - a1_r
        dai_ref[...] = dt * gl_i - a1_i
        gdt = jnp.sum(ar * gl_r + ai * gl_i, axis=-1, keepdims=True)
        dldt_ref[...] = gdt * dt[:, 0:1]

    return pl.pallas_call(
        body, name="s5_discretise_bwd",
        out_shape=[jax.ShapeDtypeStruct((g, p), F32), jax.ShapeDtypeStruct((g, p), F32),
                   jax.ShapeDtypeStruct((g, 1), F32), jax.ShapeDtypeStruct((rows, p), F32),
                   jax.ShapeDtypeStruct((rows, p), F32)],
        compiler_params=pltpu.CompilerParams(vmem_limit_bytes=VMEM_LIMIT),
    )(*a16, *b16, *ag, *gab, *gbb)


def _slab_tile(re, im, nsl):
    row = jnp.concatenate([re.reshape(nsl, SLAB_COLS), im.reshape(nsl, SLAB_COLS)], axis=-1)
    return jnp.repeat(row, SUBLANES, axis=0)


def _slab_in_matrix(b_re, b_im, nsl):
    eye = jnp.eye(SLAB_GROUPS, dtype=F32)

    def blk(b):
        b = b.reshape(nsl, SLAB_GROUPS, S5_GROUP, S5_STATE)
        return jnp.einsum('sgcp,gh->sgchp', b, eye).reshape(nsl, LANES, SLAB_COLS)

    return jnp.concatenate([blk(b_re), blk(b_im)], axis=-1)


def _slab_in_extract(m, nsl):
    eye = jnp.eye(SLAB_GROUPS, dtype=F32)

    def ext(x_):
        x_ = x_.reshape(nsl, SLAB_GROUPS, S5_GROUP, SLAB_GROUPS, S5_STATE)
        return jnp.einsum('sgchp,gh->sgcp', x_, eye).reshape(nsl * LANES, S5_STATE)

    return ext(m[..., :SLAB_COLS]), ext(m[..., SLAB_COLS:])


def _slab_out_matrix(c_re, c_im, nsl):
    eye = jnp.eye(SLAB_GROUPS, dtype=F32)

    def blk(c):
        c = c.reshape(nsl, SLAB_GROUPS, S5_GROUP, S5_STATE)
        return jnp.einsum('sgcp,gh->sgphc', c, eye).reshape(nsl, SLAB_COLS, LANES)

    return jnp.concatenate([blk(c_re), -blk(c_im)], axis=1)


def _slab_out_extract(m, nsl):
    eye = jnp.eye(SLAB_GROUPS, dtype=F32)

    def ext(x_):
        x_ = x_.reshape(nsl, SLAB_GROUPS, S5_STATE, SLAB_GROUPS, S5_GROUP)
        return jnp.einsum('sgphc,gh->sgcp', x_, eye).reshape(nsl * SLAB_GROUPS, S5_GROUP, S5_STATE)

    return ext(m[:, :SLAB_COLS]), -ext(m[:, SLAB_COLS:])


def _gelu(y):
    t = jnp.tanh(0.7978845608028654 * (y + 0.044715 * y * y * y))
    return 0.5 * y * (1.0 + t)


def _gelu_grad(y):
    t = jnp.tanh(0.7978845608028654 * (y + 0.044715 * y * y * y))
    return 0.5 * (1.0 + t) + 0.5 * y * (1.0 - t * t) * 0.7978845608028654 * (1.0 + 3.0 * 0.044715 * y * y)


def _scan_rows(ref, n_steps, ar, ai, state, reverse, conj):
    sgn = -1.0 if conj else 1.0

    def step(k, carry):
        xr, xi = carry
        t = (n_steps - 1 - k) if reverse else k
        r0 = pl.multiple_of(t * SUBLANES, SUBLANES)
        nr = ar * xr - sgn * ai * xi + ref[pl.ds(r0, SUBLANES), :SLAB_COLS]
        ni = ar * xi + sgn * ai * xr + ref[pl.ds(r0, SUBLANES), SLAB_COLS:]
        ref[pl.ds(r0, SUBLANES), :SLAB_COLS] = nr
        ref[pl.ds(r0, SUBLANES), SLAB_COLS:] = ni
        return nr, ni

    return lax.fori_loop(0, n_steps, step, state, unroll=4)


def _s5_pass1(hp, bblk, ab_tile, rc=1024):
    s, d = hp.shape
    nsl = d // LANES
    rc = min(rc, s)
    nch = s // rc
    w = 2 * SLAB_COLS

    def body(u_ref, b_ref, ab_ref, x_ref, end_ref, st_ref):
        j = pl.program_id(1)

        @pl.when(j == 0)
        def _():
            st_ref[...] = jnp.zeros_like(st_ref)

        x_ref[...] = _dot(_bf(u_ref[...]), b_ref[0], NN)
        xr, xi = _scan_rows(x_ref, rc // SUBLANES, ab_ref[:, :SLAB_COLS], ab_ref[:, SLAB_COLS:],
                            (st_ref[:, :SLAB_COLS], st_ref[:, SLAB_COLS:]), False, False)
        st_ref[:, :SLAB_COLS] = xr
        st_ref[:, SLAB_COLS:] = xi

        @pl.when(j == nch - 1)
        def _():
            end_ref[...] = st_ref[...]

    return pl.pallas_call(
        body, name="s5_scan_local", grid=(nsl, nch),
        in_specs=[pl.BlockSpec((rc, LANES), lambda sl, j: (j, sl)), pl.BlockSpec((1, LANES, w), lambda sl, j: (sl, 0, 0)),
                  pl.BlockSpec((SUBLANES, w), lambda sl, j: (sl, 0))],
        out_specs=[pl.BlockSpec((rc, w), lambda sl, j: (j, sl)), pl.BlockSpec((SUBLANES, w), lambda sl, j: (sl, 0))],
        out_shape=[jax.ShapeDtypeStruct((s, nsl * w), F32), jax.ShapeDtypeStruct((nsl * SUBLANES, w), F32)],
        scratch_shapes=[pltpu.VMEM((SUBLANES, w), F32)],
        compiler_params=_params("parallel", "arbitrary"),
    )(hp, bblk, ab_tile)


def _s5_carry(name, ends, ap_tile, reverse):
    rows, w = ends.shape
    nsl = rows // SUBLANES
    sgn = -1.0 if reverse else 1.0

    def body(e_ref, ap_ref, c_ref):
        pr, pi = ap_ref[0:1, :SLAB_COLS], sgn * ap_ref[0:1, SLAB_COLS:]
        tr = jnp.zeros((1, SLAB_COLS), F32)
        ti = jnp.zeros((1, SLAB_COLS), F32)
        order = range(SUBLANES - 1, -1, -1) if reverse else range(SUBLANES)
        for seg in order:
            c_ref[seg:seg + 1, :SLAB_COLS] = tr
            c_ref[seg:seg + 1, SLAB_COLS:] = ti
            mr, mi = _cmul(pr, pi, tr, ti)
            tr = e_ref[seg:seg + 1, :SLAB_COLS] + mr
            ti = e_ref[seg:seg + 1, SLAB_COLS:] + mi

    spec = pl.BlockSpec((SUBLANES, w), lambda sl: (sl, 0))
    return pl.pallas_call(
        body, name=name, grid=(nsl,), in_specs=[spec, spec], out_specs=spec,
        out_shape=jax.ShapeDtypeStruct((rows, w), F32), compiler_params=_params("parallel"),
    )(ends, ap_tile)


def _s5_pass2(xloc, cin, ab_tile, cblk, hp, dvec, rc=1024):
    s, d = hp.shape
    nsl = d // LANES
    rc = min(rc, s)
    nch = s // rc
    w = 2 * SLAB_COLS

    def body(xl_ref, cin_ref, ab_ref, c_ref, h_ref, d_ref, x_ref, y_ref, z_ref, st_ref):
        j = pl.program_id(1)

        @pl.when(j == 0)
        def _():
            st_ref[...] = cin_ref[...]

        x_ref[...] = jnp.zeros_like(x_ref)
        zr, zi = _scan_rows(x_ref, rc // SUBLANES, ab_ref[:, :SLAB_COLS], ab_ref[:, SLAB_COLS:],
                            (st_ref[:, :SLAB_COLS], st_ref[:, SLAB_COLS:]), False, False)
        st_ref[:, :SLAB_COLS] = zr
        st_ref[:, SLAB_COLS:] = zi
        x = x_ref[...] + xl_ref[...]
        x_ref[...] = x
        y = _dot(_bf(x), c_ref[0], NN) + d_ref[...] * h_ref[...]
        y_ref[...] = y
        z_ref[...] = _bf(_gelu(y))

    tile = lambda wd: pl.BlockSpec((rc, wd), lambda sl, j: (j, sl))
    small = pl.BlockSpec((SUBLANES, w), lambda sl, j: (sl, 0))
    return pl.pallas_call(
        body, name="s5_scan_carry_out", grid=(nsl, nch),
        in_specs=[tile(w), small, small, pl.BlockSpec((1, w, LANES), lambda sl, j: (sl, 0, 0)), tile(LANES),
                  pl.BlockSpec((1, LANES), lambda sl, j: (0, sl))],
        out_specs=[tile(w), tile(LANES), tile(LANES)],
        out_shape=[jax.ShapeDtypeStruct((s, nsl * w), F32), jax.ShapeDtypeStruct((s, d), F32),
                   jax.ShapeDtypeStruct((s, d), BF16)],
        scratch_shapes=[pltpu.VMEM((SUBLANES, w), F32)],
        compiler_params=_params("parallel", "arbitrary"),
    )(xloc, cin, ab_tile, cblk, hp, dvec)


def _s5_bwd_pass1(dzg, ypre, cblk, ab_tile, hp, rc=1024):
    s, d = hp.shape
    nsl = d // LANES
    rc = min(rc, s)
    nch = s // rc
    w = 2 * SLAB_COLS

    def body(dz_ref, y_ref, c_ref, ab_ref, h_ref, lam_ref, st_out_ref, dy_ref, dd_ref, st_ref):
        j = pl.program_id(1)

        @pl.when(j == 0)
        def _():
            st_ref[...] = jnp.zeros_like(st_ref)
            dd_ref[...] = jnp.zeros_like(dd_ref)

        dy = dz_ref[...] * _gelu_grad(y_ref[...])
        dy_ref[...] = dy
        dd_ref[0:1, :] += jnp.sum(dy * h_ref[...], axis=0, keepdims=True)
        lam_ref[...] = _dot(_bf(dy), c_ref[0], NT)
        lr, li = _scan_rows(lam_ref, rc // SUBLANES, ab_ref[:, :SLAB_COLS], ab_ref[:, SLAB_COLS:],
                            (st_ref[:, :SLAB_COLS], st_ref[:, SLAB_COLS:]), True, True)
        st_ref[:, :SLAB_COLS] = lr
        st_ref[:, SLAB_COLS:] = li

        @pl.when(j == nch - 1)
        def _():
            st_out_ref[...] = st_ref[...]

    tile = lambda wd: pl.BlockSpec((rc, wd), lambda sl, j: (nch - 1 - j, sl))
    small = pl.BlockSpec((SUBLANES, w), lambda sl, j: (sl, 0))
    return pl.pallas_call(
        body, name="s5_adjoint_local", grid=(nsl, nch),
        in_specs=[tile(LANES), tile(LANES), pl.BlockSpec((1, w, LANES), lambda sl, j: (sl, 0, 0)), small, tile(LANES)],
        out_specs=[tile(w), small, tile(LANES), pl.BlockSpec((SUBLANES, LANES), lambda sl, j: (0, sl))],
        out_shape=[jax.ShapeDtypeStruct((s, nsl * w), F32), jax.ShapeDtypeStruct((nsl * SUBLANES, w), F32),
                   jax.ShapeDtypeStruct((s, d), F32), jax.ShapeDtypeStruct((SUBLANES, d), F32)],
        scratch_shapes=[pltpu.VMEM((SUBLANES, w), F32)],
        compiler_params=_params("parallel", "arbitrary"),
    )(dzg, ypre, cblk, ab_tile, hp)


def _s5_bwd_pass2(lamloc, cinl, ab_tile, xtrue, cinx, hp, dy, bblk, dvec, rc=1024):
    s, d = hp.shape
    nsl = d // LANES
    rc = min(rc, s)
    nch = s // rc
    w = 2 * SLAB_COLS
    n_steps = rc // SUBLANES

    def body(ll_ref, cl_ref, ab_ref, x_ref, xp_ref, cx_ref, h_ref, dy_ref, b_ref, d_ref,
             du_ref, db_ref, dc_ref, da_ref, st_ref, lam_ref, acc_ref):
        j = pl.program_id(1)

        @pl.when(j == 0)
        def _():
            st_ref[...] = cl_ref[...]
            acc_ref[...] = jnp.zeros_like(acc_ref)
            db_ref[...] = jnp.zeros_like(db_ref)
            dc_ref[...] = jnp.zeros_like(dc_ref)

        ar, ai = ab_ref[:, :SLAB_COLS], ab_ref[:, SLAB_COLS:]
        lam_ref[...] = jnp.zeros_like(lam_ref)
        zr, zi = _scan_rows(lam_ref, n_steps, ar, ai, (st_ref[:, :SLAB_COLS], st_ref[:, SLAB_COLS:]), True, True)
        st_ref[:, :SLAB_COLS] = zr
        st_ref[:, SLAB_COLS:] = zi
        lam_ref[...] = lam_ref[...] + ll_ref[...]

        def step(k, carry):
            dr, di = carry
            r0 = pl.multiple_of(k * SUBLANES, SUBLANES)
            r1 = pl.multiple_of((k + 1) * SUBLANES, SUBLANES)
            xr, xi = x_ref[pl.ds(r0, SUBLANES), :SLAB_COLS], x_ref[pl.ds(r0, SUBLANES), SLAB_COLS:]
            lr, li = lam_ref[pl.ds(r1, SUBLANES), :SLAB_COLS], lam_ref[pl.ds(r1, SUBLANES), SLAB_COLS:]
            return dr + xr * lr + xi * li, di + xr * li - xi * lr

        dr, di = lax.fori_loop(0, n_steps - 1, step, (acc_ref[:, :SLAB_COLS], acc_ref[:, SLAB_COLS:]), unroll=4)
        first_chunk = j == nch - 1
        xr = jnp.where(first_chunk, cx_ref[:, :SLAB_COLS], xp_ref[:, :SLAB_COLS])
        xi = jnp.where(first_chunk, cx_ref[:, SLAB_COLS:], xp_ref[:, SLAB_COLS:])
        lr, li = lam_ref[0:SUBLANES, :SLAB_COLS], lam_ref[0:SUBLANES, SLAB_COLS:]
        acc_ref[:, :SLAB_COLS] = dr + xr * lr + xi * li
        acc_ref[:, SLAB_COLS:] = di + xr * li - xi * lr

        lam_b = _bf(lam_ref[...])
        dyv = dy_ref[...]
        db_ref[0] += _dot(_bf(h_ref[...]), lam_b, TN)
        dc_ref[0] += _dot(_bf(x_ref[...]), _bf(dyv), TN)
        du_ref[...] = _dot(lam_b, b_ref[0], NT) + d_ref[...] * dyv

        @pl.when(j == nch - 1)
        def _():
            da_ref[...] = jnp.broadcast_to(jnp.sum(acc_ref[...], axis=0, keepdims=True), (SUBLANES, w))

    sub = rc // SUBLANES
    tile = lambda wd: pl.BlockSpec((rc, wd), lambda sl, j: (nch - 1 - j, sl))
    small = pl.BlockSpec((SUBLANES, w), lambda sl, j: (sl, 0))
    prev = pl.BlockSpec((SUBLANES, w), lambda sl, j: (jnp.maximum((nch - 1 - j) * sub - 1, 0), sl))
    return pl.pallas_call(
        body, name="s5_adjoint_carry_grads", grid=(nsl, nch),
        in_specs=[tile(w), small, small, tile(w), prev, small, tile(LANES), tile(LANES),
                  pl.BlockSpec((1, LANES, w), lambda sl, j: (sl, 0, 0)), pl.BlockSpec((1, LANES), lambda sl, j: (0, sl))],
        out_specs=[tile(LANES), pl.BlockSpec((1, LANES, w), lambda sl, j: (sl, 0, 0)),
                   pl.BlockSpec((1, w, LANES), lambda sl, j: (sl, 0, 0)), small],
        out_shape=[jax.ShapeDtypeStruct((s, d), F32), jax.ShapeDtypeStruct((nsl, LANES, w), F32),
                   jax.ShapeDtypeStruct((nsl, w, LANES), F32), jax.ShapeDtypeStruct((nsl * SUBLANES, w), F32)],
        scratch_shapes=[pltpu.VMEM((SUBLANES, w), F32), pltpu.VMEM((rc, w), F32), pltpu.VMEM((SUBLANES, w), F32)],
        compiler_params=_params("parallel", "arbitrary"),
    )(lamloc, cinl, ab_tile, xtrue, xtrue, cinx, hp, dy, bblk, dvec)


def _adamw(name, w, g, m, v):
    r, c = w.shape
    tile = r if r * c <= 512 * 1024 else _pick(r, max(SUBLANES, (512 * 1024 // c) // SUBLANES * SUBLANES), q=SUBLANES)
    c1 = 1.0 / (1.0 - ADAM_B1 ** ADAM_STEP)
    c2 = 1.0 / (1.0 - ADAM_B2 ** ADAM_STEP)

    def body(w_ref, g_ref, m_ref, v_ref, d_ref, nm_ref, nv_ref):
        gg = g_ref[...]
        nm = ADAM_B1 * m_ref[...] + (1.0 - ADAM_B1) * gg
        nv = ADAM_B2 * v_ref[...] + (1.0 - ADAM_B2) * gg * gg
        d_ref[...] = -ADAM_LR * ((nm * c1) / (jnp.sqrt(nv * c2) + ADAM_EPS) + ADAM_WD * w_ref[...])
        nm_ref[...] = nm
        nv_ref[...] = nv

    spec = _nat(tile, c)
    return pl.pallas_call(
        body, name=name, grid=(r // tile,), in_specs=[spec] * 4, out_specs=[spec] * 3,
        out_shape=[jax.ShapeDtypeStruct((r, c), F32)] * 3, compiler_params=_params("parallel"),
    )(w, g, m, v)


def _place():
    x, y, c = lax.axis_index("x"), lax.axis_index("y"), lax.axis_index("c")
    return x, y, c, [(1 - x, y), (x, 1 - y), (1 - x, 1 - y)]


_ANY = pl.BlockSpec(memory_space=pl.ANY)


def _gather_weights(shards):
    n = len(shards)

    def body(*refs):
        ins, outs = refs[:n], refs[n:2 * n]
        send_sems, recv_sems, local_sems = refs[2 * n:]
        x, y, c, chips = _place()
        me = 2 * x + y
        sibling = (x, y, 1 - c)
        started = []
        for a in range(n):
            local = pltpu.make_async_copy(ins[a], outs[a].at[me], local_sems.at[a])
            local.start()
            started.append(local)

        def half(a, chip, h):
            hw = ins[a].shape[1] // 2
            return outs[a].at[chip, :, pl.ds(pl.multiple_of(h * hw, LANES), hw)]

        def copy(a, k, src, chip, h, to):
            return pltpu.make_async_remote_copy(
                src_ref=src, dst_ref=half(a, chip, h), send_sem=send_sems.at[a, k], recv_sem=recv_sems.at[a, k],
                device_id=to, device_id_type=MESH)

        sends = []
        for a in range(n):
            hw = ins[a].shape[1] // 2
            mine = ins[a].at[:, pl.ds(pl.multiple_of(c * hw, LANES), hw)]
            for k, chip in enumerate(chips):
                cp = copy(a, k, mine, me, c, (*chip, c))
                cp.start()
                sends.append(cp)
        for a in range(n):
            for k, (cx, cy) in enumerate(chips):
                src_chip = 2 * cx + cy
                copy(a, k, half(a, src_chip, c), src_chip, c, (x, y, c)).wait_recv()
                fwd = copy(a, 3 + k, half(a, src_chip, c), src_chip, c, sibling)
                fwd.start()
                sends.append(fwd)
        for a in range(n):
            for k, (cx, cy) in enumerate(chips):
                src_chip = 2 * cx + cy
                copy(a, 3 + k, half(a, src_chip, 1 - c), src_chip, 1 - c, (x, y, c)).wait_recv()
        for cp in sends:
            cp.wait_send()
        for cp in started:
            cp.wait()

    return pl.pallas_call(
        body, name="gather_weights",
        in_specs=[_ANY] * n, out_specs=[_ANY] * n,
        out_shape=[jax.ShapeDtypeStruct((N_CHIPS,) + s_.shape, s_.dtype) for s_ in shards],
        scratch_shapes=[pltpu.SemaphoreType.DMA((n, 6)), pltpu.SemaphoreType.DMA((n, 6)), pltpu.SemaphoreType.DMA((n,))],

    )(*shards)


def _swap_halves_to_sibling(name, grads):
    n = len(grads)

    def body(*refs):
        ins, outs = refs[:n], refs[n:2 * n]
        send_sems, recv_sems = refs[2 * n:]
        x, y, c, _ = _place()
        cps = []
        for a in range(n):
            hw = ins[a].shape[2] // 2
            src = ins[a].at[:, :, pl.ds(pl.multiple_of((1 - c) * hw, LANES), hw)]
            cp = pltpu.make_async_remote_copy(src_ref=src, dst_ref=outs[a], send_sem=send_sems.at[a],
                                              recv_sem=recv_sems.at[a], device_id=(x, y, 1 - c), device_id_type=MESH)
            cp.start()
            cps.append(cp)
        for cp in cps:
            cp.wait()

    return pl.pallas_call(
        body, name=name, in_specs=[_ANY] * n, out_specs=[_ANY] * n,
        out_shape=[jax.ShapeDtypeStruct(g.shape[:2] + (g.shape[2] // 2,), g.dtype) for g in grads],
        scratch_shapes=[pltpu.SemaphoreType.DMA((n,)), pltpu.SemaphoreType.DMA((n,))],

    )(*grads)


def _exchange_quarters(name, parts):
    n = len(parts)

    def body(*refs):
        ins, outs = refs[:n], refs[n:2 * n]
        send_sems, recv_sems = refs[2 * n:]
        x, y, c, chips = _place()
        cps = []
        for a in range(n):
            for k, (cx, cy) in enumerate(chips):
                cp = pltpu.make_async_remote_copy(
                    src_ref=ins[a].at[2 * cx + cy], dst_ref=outs[a].at[k], send_sem=send_sems.at[a, k],
                    recv_sem=recv_sems.at[a, k], device_id=(cx, cy, c), device_id_type=MESH)
                cp.start()
                cps.append(cp)
        for cp in cps:
            cp.wait()

    return pl.pallas_call(
        body, name=name, in_specs=[_ANY] * n, out_specs=[_ANY] * n,
        out_shape=[jax.ShapeDtypeStruct((3,) + p_.shape[1:], p_.dtype) for p_ in parts],
        scratch_shapes=[pltpu.SemaphoreType.DMA((n, 3)), pltpu.SemaphoreType.DMA((n, 3))],

    )(*parts)


def _join_halves(name, halves):
    n = len(halves)

    def body(*refs):
        ins, outs = refs[:n], refs[n:2 * n]
        send_sems, recv_sems, local_sems = refs[2 * n:]
        x, y, c, _ = _place()
        cps = []
        for a in range(n):
            hw = ins[a].shape[1]
            dst = outs[a].at[:, pl.ds(pl.multiple_of(c * hw, LANES), hw)]
            loc = pltpu.make_async_copy(ins[a], dst, local_sems.at[a])
            loc.start()
            cp = pltpu.make_async_remote_copy(src_ref=ins[a], dst_ref=dst, send_sem=send_sems.at[a],
                                              recv_sem=recv_sems.at[a], device_id=(x, y, 1 - c), device_id_type=MESH)
            cp.start()
            cps += [loc, cp]
        for cp in cps:
            cp.wait()

    return pl.pallas_call(
        body, name=name, in_specs=[_ANY] * n, out_specs=[_ANY] * n,
        out_shape=[jax.ShapeDtypeStruct((h.shape[0], 2 * h.shape[1]), h.dtype) for h in halves],
        scratch_shapes=[pltpu.SemaphoreType.DMA((n,)), pltpu.SemaphoreType.DMA((n,)), pltpu.SemaphoreType.DMA((n,))],

    )(*halves)


def _add_half(name, grad, recv):
    nchip, r, cfull = grad.shape
    hw = cfull // 2
    tile = _pick(r, max(SUBLANES, (256 * 1024 // hw) // SUBLANES * SUBLANES), q=SUBLANES)
    c = lax.axis_index("c")

    def body(c_ref, g_ref, r_ref, o_ref):
        o_ref[...] = g_ref[...] + r_ref[...]

    return pl.pallas_call(
        body, name=name,
        grid_spec=pltpu.PrefetchScalarGridSpec(
            num_scalar_prefetch=1, grid=(nchip, r // tile),
            in_specs=[pl.BlockSpec((1, tile, hw), lambda k, i, cr: (k, i, cr[0])),
                      pl.BlockSpec((1, tile, hw), lambda k, i, cr: (k, i, 0))],
            out_specs=pl.BlockSpec((1, tile, hw), lambda k, i, cr: (k, i, 0))),
        out_shape=jax.ShapeDtypeStruct((nchip, r, hw), F32), compiler_params=_params("parallel", "parallel"),
    )(c.reshape(1).astype(jnp.int32), grad, recv)


def _add_quarters(name, part, recv):
    _, r, hw = part.shape
    tile = _pick(r, max(SUBLANES, (256 * 1024 // hw) // SUBLANES * SUBLANES), q=SUBLANES)
    me = 2 * lax.axis_index("x") + lax.axis_index("y")

    def body(me_ref, p_ref, r_ref, o_ref):
        o_ref[...] = ((p_ref[0] + r_ref[0]) + r_ref[1]) + r_ref[2]

    return pl.pallas_call(
        body, name=name,
        grid_spec=pltpu.PrefetchScalarGridSpec(
            num_scalar_prefetch=1, grid=(r // tile,),
            in_specs=[pl.BlockSpec((1, tile, hw), lambda i, mr: (mr[0], i, 0)),
                      pl.BlockSpec((3, tile, hw), lambda i, mr: (0, i, 0))],
            out_specs=pl.BlockSpec((tile, hw), lambda i, mr: (i, 0))),
        out_shape=jax.ShapeDtypeStruct((r, hw), F32), compiler_params=_params("parallel"),
    )(me.reshape(1).astype(jnp.int32), part, recv)


def _reduce_scatter(grads):
    stacks = [g.reshape(N_CHIPS, g.shape[0] // N_CHIPS, g.shape[1]) for g in grads]
    recv = _swap_halves_to_sibling("rs_swap_halves", stacks)
    parts = [_add_half(f"rs_add_half_{a}", g, r) for a, (g, r) in enumerate(zip(stacks, recv))]
    quarters = _exchange_quarters("rs_exchange", parts)
    halves = [_add_quarters(f"rs_add_quarters_{a}", p_, q_) for a, (p_, q_) in enumerate(zip(parts, quarters))]
    return _join_halves("rs_join_halves", halves)


def _allgather_small(pack):
    m_per, n = pack.shape

    def body(x_ref, out_ref, send_sems, recv_sems, local_sem):
        x, y, c, chips = _place()
        me, sibling = (x, y, c), (x, y, 1 - c)

        def rows(px, py, pc):
            return out_ref.at[pl.ds(pl.multiple_of((4 * px + 2 * py + pc) * m_per, SUBLANES), m_per), :]

        def copy(k, block, to, src=None):
            return pltpu.make_async_remote_copy(
                src_ref=rows(*block) if src is None else src, dst_ref=rows(*block),
                send_sem=send_sems.at[k], recv_sem=recv_sems.at[k], device_id=to, device_id_type=MESH)

        mine = pltpu.make_async_copy(x_ref, rows(*me), local_sem)
        mine.start()
        first = [copy(0, me, sibling, src=x_ref)]
        first += [copy(1 + j, me, (*chip, c), src=x_ref) for j, chip in enumerate(chips)]
        for cp in first:
            cp.start()
        passed = [copy(4 + j, (*chip, c), sibling) for j, chip in enumerate(chips)]
        for j, chip in enumerate(chips):
            copy(1 + j, (*chip, c), me).wait_recv()
            passed[j].start()
        copy(0, sibling, me).wait_recv()
        for j, chip in enumerate(chips):
            copy(4 + j, (*chip, 1 - c), me).wait_recv()
        for cp in first + passed:
            cp.wait_send()
        mine.wait()

    return pl.pallas_call(
        body, name="allgather_small_grads",
        out_shape=jax.ShapeDtypeStruct((N_DEV * m_per, n), pack.dtype),
        in_specs=[pl.BlockSpec(memory_space=pltpu.VMEM)], out_specs=pl.BlockSpec(memory_space=pltpu.VMEM),
        scratch_shapes=[pltpu.SemaphoreType.DMA((7,)), pltpu.SemaphoreType.DMA((7,)), pltpu.SemaphoreType.DMA],
        compiler_params=pltpu.CompilerParams(vmem_limit_bytes=VMEM_LIMIT),
    )(pack)


def _sum_devices(packs, m_per):
    tile = _pick(m_per, 512, q=SUBLANES)
    nt = m_per // tile

    def body(*refs):
        acc = refs[0][...]
        for r in refs[1:N_DEV]:
            acc = acc + r[...]
        refs[N_DEV][...] = acc

    return pl.pallas_call(
        body, name="sum_small_grads", grid=(nt,),
        in_specs=[pl.BlockSpec((tile, LANES), functools.partial(lambda i, k: (k * nt + i, 0), k=k)) for k in range(N_DEV)],
        out_specs=_nat(tile, LANES), out_shape=jax.ShapeDtypeStruct((m_per, LANES), F32),
        compiler_params=_params("parallel"),
    )(*([packs] * N_DEV))


def _tail_fwd(tag, alpha, h_in, adds, mix_gate, ln1, ln2, p_l, w, want_perm):
    h_mid, xh1, rs1 = _ln_fwd(f"ln1_fwd_{tag}", alpha, h_in, adds, mix_gate, *ln1)
    gp = _matmul(f"ple_gate_fwd_{tag}", h_mid, w['wg'], 'nn')
    pw = _matmul(f"ple_proj_fwd_{tag}", p_l, w['plet'], 'nt')
    gu = _matmul(f"ffn_in_fwd_{tag}", h_mid, w['wit'], 'nt', tn=1408)
    act = _swiglu_fwd(f"swiglu_fwd_{tag}", gu)
    ffn = _matmul(f"ffn_out_fwd_{tag}", act, w['wo'], 'nn')
    res = _ln_fwd(f"ln2_fwd_{tag}", alpha, h_mid, [(ffn, 'nat')], ('nat', (pw, 1, 0), (gp, 1, 0)), *ln2,
                  want_perm=want_perm)
    saved = dict(h_mid=h_mid, xh1=xh1, rs1=rs1, gp=gp, pw=pw, gu=gu, act=act, xh2=res[1], rs2=res[2])
    return res[0], (res[3] if want_perm else None), saved


def _tail_bwd(tag, alpha, dparts, sv, ln1_g, ln2_g, p_l, w, mix_gate, dz_perm):
    d = sv['h_mid'].shape[1]
    dz2, dgate, dg2, db2 = _ln_bwd(f"ln2_bwd_{tag}", dparts, sv['xh2'], sv['rs2'], ln2_g,
                                   gate=('nat', (sv['pw'], 1, 0), (sv['gp'], 1, 0)))
    grads = dict(ln2_g=dg2, ln2_b=db2)
    grads['plet'] = _matmul(f"ple_proj_dw_{tag}", dgate, p_l, 'tn', a_win=(0, d))
    grads['wg'] = _matmul(f"ple_gate_dw_{tag}", sv['h_mid'], dgate, 'tn', b_win=(d, d))
    dx_gate = _matmul(f"ple_gate_dx_{tag}", dgate, w['wg'], 'nt', a_win=(d, d))
    dact = _matmul(f"ffn_out_dx_{tag}", dz2, w['wo'], 'nt', out_dtype=BF16, tn=1408)
    grads['wo'] = _matmul(f"ffn_out_dw_{tag}", sv['act'], dz2, 'tn', tm=1408)
    dgu = _swiglu_bwd(f"swiglu_bwd_{tag}", sv['gu'], dact)
    grads['wit'] = _matmul(f"ffn_in_dw_{tag}", dgu, sv['h_mid'], 'tn')
    dx_ffn = _matmul(f"ffn_in_dx_{tag}", dgu, w['wit'], 'nn')
    res = _ln_bwd(f"ln1_bwd_{tag}", [(dz2, 'nat', alpha), (dx_gate, 'nat', 1.0), (dx_ffn, 'nat', 1.0)],
                  sv['xh1'], sv['rs1'], ln1_g, gate=mix_gate, dz_perm=dz_perm)
    grads['ln1_g'], grads['ln1_b'] = res[-2], res[-1]
    return res[:-2], grads


def kernel(x, p, positions, attn_w_in, mla_q_norm, mla_w_q_b, mla_kv_norm, mla_w_kv_b, attn_w_out, s5_a_re, s5_a_im, s5_log_dt, s5_b_re, s5_b_im, s5_c_re, s5_c_im, s5_d, s5_w_glu, ln1_g, ln1_b, ffn_w_in, ffn_w_out, ple_w, ple_gate_w, ln2_g, ln2_b, loss_target, m_attn_w_in, m_mla_q_norm, m_mla_w_q_b, m_mla_kv_norm, m_mla_w_kv_b, m_attn_w_out, m_s5_a_re, m_s5_a_im, m_s5_log_dt, m_s5_b_re, m_s5_b_im, m_s5_c_re, m_s5_c_im, m_s5_d, m_s5_w_glu, m_ln1_g, m_ln1_b, m_ffn_w_in, m_ffn_w_out, m_ple_w, m_ple_gate_w, m_ln2_g, m_ln2_b, v_attn_w_in, v_mla_q_norm, v_mla_w_q_b, v_mla_kv_norm, v_mla_w_kv_b, v_attn_w_out, v_s5_a_re, v_s5_a_im, v_s5_log_dt, v_s5_b_re, v_s5_b_im, v_s5_c_re, v_s5_c_im, v_s5_d, v_s5_w_glu, v_ln1_g, v_ln1_b, v_ffn_w_in, v_ffn_w_out, v_ple_w, v_ple_gate_w, v_ln2_g, v_ln2_b):
    weights = dict(attn_w_in=attn_w_in, mla_q_norm=mla_q_norm, mla_w_q_b=mla_w_q_b, mla_kv_norm=mla_kv_norm,
                   mla_w_kv_b=mla_w_kv_b, attn_w_out=attn_w_out, s5_a_re=s5_a_re, s5_a_im=s5_a_im, s5_log_dt=s5_log_dt,
                   s5_b_re=s5_b_re, s5_b_im=s5_b_im, s5_c_re=s5_c_re, s5_c_im=s5_c_im, s5_d=s5_d, s5_w_glu=s5_w_glu,
                   ln1_g=ln1_g, ln1_b=ln1_b, ffn_w_in=ffn_w_in, ffn_w_out=ffn_w_out, ple_w=ple_w, ple_gate_w=ple_gate_w,
                   ln2_g=ln2_g, ln2_b=ln2_b)
    m_in = dict(attn_w_in=m_attn_w_in, mla_q_norm=m_mla_q_norm, mla_w_q_b=m_mla_w_q_b, mla_kv_norm=m_mla_kv_norm,
                mla_w_kv_b=m_mla_w_kv_b, attn_w_out=m_attn_w_out, s5_a_re=m_s5_a_re, s5_a_im=m_s5_a_im,
                s5_log_dt=m_s5_log_dt, s5_b_re=m_s5_b_re, s5_b_im=m_s5_b_im, s5_c_re=m_s5_c_re, s5_c_im=m_s5_c_im,
                s5_d=m_s5_d, s5_w_glu=m_s5_w_glu, ln1_g=m_ln1_g, ln1_b=m_ln1_b, ffn_w_in=m_ffn_w_in,
                ffn_w_out=m_ffn_w_out, ple_w=m_ple_w, ple_gate_w=m_ple_gate_w, ln2_g=m_ln2_g, ln2_b=m_ln2_b)
    v_in = dict(attn_w_in=v_attn_w_in, mla_q_norm=v_mla_q_norm, mla_w_q_b=v_mla_w_q_b, mla_kv_norm=v_mla_kv_norm,
                mla_w_kv_b=v_mla_w_kv_b, attn_w_out=v_attn_w_out, s5_a_re=v_s5_a_re, s5_a_im=v_s5_a_im,
                s5_log_dt=v_s5_log_dt, s5_b_re=v_s5_b_re, s5_b_im=v_s5_b_im, s5_c_re=v_s5_c_re, s5_c_im=v_s5_c_im,
                s5_d=v_s5_d, s5_w_glu=v_s5_w_glu, ln1_g=v_ln1_g, ln1_b=v_ln1_b, ffn_w_in=v_ffn_w_in,
                ffn_w_out=v_ffn_w_out, ple_w=v_ple_w, ple_gate_w=v_ple_gate_w, ln2_g=v_ln2_g, ln2_b=v_ln2_b)
    names = list(weights)

    s, d = x.shape[1], x.shape[2]
    depth = ln1_g.shape[0]
    assert depth == 2
    alpha = (2.0 * depth) ** 0.25
    ql, kvl = mla_q_norm.shape[1], mla_kv_norm.shape[1]
    in_cols = N_CHIPS * attn_w_in.shape[2]
    heads = N_CHIPS * mla_w_q_b.shape[2] // (NOPE + ROPE)
    hps = heads // N_CHIPS
    dw = (in_cols - ql - kvl - ROPE) // 3
    dh = dw // DHD
    assert ql % LANES == 0 and kvl == ql and dw % DHD == 0 and heads % N_CHIPS == 0
    ngroups, nstate = s5_a_re.shape[1], s5_a_re.shape[2]
    assert nstate == S5_STATE and ngroups * S5_GROUP == d and d % LANES == 0
    nsl = d // LANES
    seg_len = s // SUBLANES
    n_sq = seg_len.bit_length() - 1
    assert 1 << n_sq == seg_len, "the segment length of the S5 scan must be a power of two"
    for window, dil in DIL_BRANCHES:
        assert window // dil == DIL_STEPS and (s // dil) % DIL_STEPS == 0
    me = 2 * lax.axis_index("x") + lax.axis_index("y")

    xb = x[0]
    target = loss_target[0]
    p_layers = [p[0, 0], p[1, 0]]
    pos = positions[0].astype(F32).reshape(s, 1)
    inv_freq = ROPE_THETA ** (-jnp.arange(ROPE // 2, dtype=F32) / (ROPE // 2))
    invf = jnp.concatenate([inv_freq, inv_freq, jnp.zeros((LANES - ROPE,), F32)]).reshape(1, LANES)
    slopes = 2.0 ** (-8.0 * jnp.arange(1, dh + 1, dtype=F32) / dh)
    slopes = jnp.broadcast_to(jnp.repeat(slopes, SUBLANES)[:, None], (dh * SUBLANES, LANES))

    wqb_t = mla_w_q_b[0].T.reshape(hps, NOPE + ROPE, ql)
    wqb_t = jnp.pad(wqb_t, ((0, 0), (0, QK_PAD - NOPE - ROPE), (0, 0))).reshape(hps * QK_PAD, ql)
    d_cols = max(d // N_CHIPS, 2 * LANES)
    d_pad = jnp.zeros((SUBLANES, d_cols), F32).at[0, :d // N_CHIPS].set(s5_d[0])
    shards = [_bf(attn_w_in[0].T), _bf(wqb_t), _bf(mla_w_kv_b[0].T), _bf(attn_w_out[0]), _bf(s5_w_glu[0].T)]
    for l in range(depth):
        shards += [_bf(ffn_w_in[l].T), _bf(ffn_w_out[l]), _bf(ple_w[l].T), _bf(ple_gate_w[l])]
    shards.append(d_pad)
    full = [g.reshape(N_CHIPS * g.shape[1], g.shape[2]) for g in _gather_weights(shards)]
    win_t, wqb_t_f, wkv_t, wout, wglu_t = full[:5]
    lw = [dict(wit=full[5 + 4 * l], wo=full[6 + 4 * l], plet=full[7 + 4 * l], wg=full[8 + 4 * l]) for l in range(depth)]
    dvec = full[-1].reshape(N_CHIPS, SUBLANES, d_cols)[:, 0, :d // N_CHIPS].reshape(1, d)
    lat = ql + kvl
    win_t = jnp.concatenate([win_t[:lat + ROPE], jnp.zeros((LANES - ROPE, d), BF16), win_t[lat + ROPE:]], axis=0)
    kpe_cb = lat // LANES
    q_cb = kpe_cb + 1
    a_cb = heads * VDIM // LANES

    proj = _matmul("attn_in_fwd", xb, win_t, 'nt', tn=1408)
    nrm = _rms_fwd(proj, ql, kvl, mla_q_norm[0], mla_kv_norm[0])
    q_raw = _matmul("mla_q_up_fwd", nrm, wqb_t_f, 'nt', a_win=(0, ql))
    kv = _matmul("mla_kv_up_fwd", nrm, wkv_t, 'nt', a_win=(ql, kvl))
    qf, kf, vv = _rope_prep(q_raw, kv, proj, kpe_cb, pos, invf, heads)
    out_a, lse_a = _mla_fwd(qf, kf, vv, heads)
    outs, lses = [], []
    for window, dil in DIL_BRANCHES:
        o_g, l_g = _dil_fwd(f"dil_fwd_d{dil}", proj, slopes, dil, dh, q_cb)
        outs.append(o_g)
        lses.append(l_g)
    att, out_b, lse_b = _dil_merge(out_a, outs, lses)
    mix0 = _matmul("attn_out_fwd", att, wout, 'nn')
    h2, h2p, sv0 = _tail_fwd("l0", alpha, xb, [(mix0, 'nat')], None, (ln1_g[0], ln1_b[0]), (ln2_g[0], ln2_b[0]),
                             p_layers[0], lw[0], want_perm=True)

    rep = lambda a: jnp.repeat(a, S5_GROUP, axis=0)
    ag = (s5_a_re[0], s5_a_im[0], jnp.broadcast_to(s5_log_dt[0][:, None], (ngroups, nstate)))
    a16 = tuple(rep(a) for a in ag)
    b16 = tuple(b[0].transpose(0, 2, 1).reshape(ngroups * S5_GROUP, nstate) for b in (s5_b_re, s5_b_im))
    abr, abi, apr, api, bbr, bbi = _s5_discretise(*a16, *b16, n_sq)
    ab_tile = _slab_tile(abr[::S5_GROUP], abi[::S5_GROUP], nsl)
    ap_tile = _slab_tile(apr[::S5_GROUP], api[::S5_GROUP], nsl)
    bblk = _bf(_slab_in_matrix(bbr.reshape(ngroups, S5_GROUP, nstate), bbi.reshape(ngroups, S5_GROUP, nstate), nsl))
    cblk = _bf(_slab_out_matrix(s5_c_re[0], s5_c_im[0], nsl))
    xloc, ends = _s5_pass1(h2p, bblk, ab_tile)
    cinx = _s5_carry("s5_carry_fwd", ends, ap_tile, False)
    xtrue, ypre, zg = _s5_pass2(xloc, cinx, ab_tile, cblk, h2p, dvec)
    vg = _matmul("s5_glu_fwd", zg, wglu_t, 'nt')
    glu_gate = ('perm', (vg, 2, 0), (vg, 2, 1))
    h4, _, sv1 = _tail_fwd("l1", alpha, h2, [], glu_gate, (ln1_g[1], ln1_b[1]), (ln2_g[1], ln2_b[1]),
                           p_layers[1], lw[1], want_perm=False)
    loss = lax.psum(jnp.sum(_loss_partial(h4, target)), ("x", "y", "c"))

    (dz1_1, dvg), g1 = _tail_bwd("l1", alpha, [(h4, 'nat', 1.0 / d), (target, 'nat', -1.0 / d)], sv1, ln1_g[1], ln2_g[1],
                                 p_layers[1], lw[1], glu_gate, dz_perm=False)
    d_wglu_t = _matmul("s5_glu_dw", dvg, zg, 'tn')
    dzg = _matmul("s5_glu_dx", dvg, wglu_t, 'nn')
    lamloc, starts, dy, dd = _s5_bwd_pass1(dzg, ypre, cblk, ab_tile, h2p)
    cinl = _s5_carry("s5_carry_bwd", starts, ap_tile, True)
    du_p, d_bblk, d_cblk, d_ab = _s5_bwd_pass2(lamloc, cinl, ab_tile, xtrue, cinx, h2p, dy, bblk, dvec)
    gbb = _slab_in_extract(d_bblk, nsl)
    g_c_re, g_c_im = _slab_out_extract(d_cblk, nsl)
    d_ab = d_ab[::SUBLANES]
    gab = (d_ab[:, :SLAB_COLS].reshape(ngroups, nstate), d_ab[:, SLAB_COLS:].reshape(ngroups, nstate))
    g_a_re, g_a_im, g_log_dt, g_b_re, g_b_im = _s5_discretise_bwd(a16, b16, ag, gab, gbb)
    unt = lambda b: b.reshape(ngroups, S5_GROUP, nstate).transpose(0, 2, 1)

    (dz1_0,), g0 = _tail_bwd("l0", alpha, [(dz1_1, 'nat', alpha), (du_p, 'perm', 1.0)], sv0, ln1_g[0], ln2_g[0],
                             p_layers[0], lw[0], None, dz_perm=False)
    datt = _matmul("attn_out_dx", dz1_0, wout, 'nt')
    d_wout = _matmul("attn_out_dw", att, dz1_0, 'tn')
    do, delta = _attn_bwd_prep(datt, out_a, out_b)
    dqf = _mla_bwd_dq(qf, kf, vv, do, lse_a, delta, heads, 0)
    dkf, dvv = _mla_bwd_dkv(qf, kf, vv, do, lse_a, delta, heads, 0)
    dq_raw, dkv, dkpe = _rope_unprep(dqf, dkf, dvv, pos, invf, heads)
    d_wqb_t = _matmul("mla_q_up_dw", dq_raw, nrm, 'tn', b_win=(0, ql))
    d_wkv_t = _matmul("mla_kv_up_dw", dkv, nrm, 'tn', b_win=(ql, kvl))
    dnq = _matmul("mla_q_up_dx", dq_raw, wqb_t_f, 'nn')
    dnkv = _matmul("mla_kv_up_dx", dkv, wkv_t, 'nn')
    dqs, dks, dvs = [], [], []
    for window, dil in DIL_BRANCHES:
        dqs.append(_dil_bwd_dq(f"dil_bwd_dq_d{dil}", proj, slopes, do, lse_b, delta, dil, dh, q_cb, a_cb))
        dk_g, dv_g = _dil_bwd_dkv(f"dil_bwd_dkv_d{dil}", proj, slopes, do, lse_b, delta, dil, dh, q_cb, a_cb)
        dks.append(dk_g)
        dvs.append(dv_g)
    dproj, g_gq, g_gkv = _dproj_assemble(proj, dnq, dnkv, dkpe, dqs, dks, dvs, mla_q_norm[0], mla_kv_norm[0], ql)
    d_win_t = _matmul("attn_in_dw", dproj, xb, 'tn', tm=1408)
    dx_attn = _matmul("attn_in_dx", dproj, win_t, 'nn')
    grad_x = _axpy("grad_x", alpha, dz1_0, dx_attn)

    d_win_t = jnp.concatenate([d_win_t[:lat + ROPE], d_win_t[lat + LANES:]], axis=0)
    big = [d_win_t, d_wqb_t, d_wkv_t, d_wout, d_wglu_t]
    for gl in (g0, g1):
        big += [gl['wit'], gl['wo'], gl['plet'], gl['wg']]
    red = _reduce_scatter(big)
    r_wqb = red[1].reshape(hps, QK_PAD, ql)[:, :NOPE + ROPE].reshape(hps * (NOPE + ROPE), ql)
    grads = dict(attn_w_in=red[0].T[None], mla_w_q_b=r_wqb.T[None], mla_w_kv_b=red[2].T[None], attn_w_out=red[3][None],
                 s5_w_glu=red[4].T[None],
                 ffn_w_in=jnp.stack([red[5].T, red[9].T]), ffn_w_out=jnp.stack([red[6], red[10]]),
                 ple_w=jnp.stack([red[7].T, red[11].T]), ple_gate_w=jnp.stack([red[8], red[12]]))

    small = dict(mla_q_norm=g_gq, mla_kv_norm=g_gkv, s5_a_re=g_a_re, s5_a_im=g_a_im, s5_log_dt=g_log_dt,
                 s5_b_re=unt(g_b_re), s5_b_im=unt(g_b_im), s5_c_re=g_c_re, s5_c_im=g_c_im, s5_d=dd[0],
                 ln1_g=jnp.stack([g0['ln1_g'], g1['ln1_g']]), ln1_b=jnp.stack([g0['ln1_b'], g1['ln1_b']]),
                 ln2_g=jnp.stack([g0['ln2_g'], g1['ln2_g']]), ln2_b=jnp.stack([g0['ln2_b'], g1['ln2_b']]))
    flat = jnp.concatenate([v_.reshape(-1) for v_ in small.values()])
    m_per = -(-flat.shape[0] // (LANES * SUBLANES)) * SUBLANES
    pack = jnp.pad(flat, (0, m_per * LANES - flat.shape[0])).reshape(m_per, LANES)
    total = _sum_devices(_allgather_small(pack), m_per).reshape(-1)
    off = 0
    for k_, v_ in small.items():
        n_ = v_.size
        piece = total[off:off + n_]
        off += n_
        if k_ == 's5_d':
            grads[k_] = lax.dynamic_slice(piece, (me * (d // N_CHIPS),), (d // N_CHIPS,)).reshape(weights[k_].shape)
        else:
            grads[k_] = piece.reshape(weights[k_].shape)

    deltas, new_m, new_v = {}, {}, {}
    for k_ in names:
        w_ = weights[k_]
        shape = w_.shape
        if w_.ndim == 3 and w_.shape[-1] >= LANES:
            two_d = (shape[0] * shape[1], shape[2])
        elif w_.ndim == 4:
            two_d = (shape[0] * shape[1], shape[2] * shape[3])
        else:
            two_d = (1, w_.size) if w_.ndim == 2 and shape[0] == 1 else (shape[0], w_.size // shape[0])
        dl, nm, nv = _adamw(f"adamw_{k_}", w_.reshape(two_d), grads[k_].reshape(two_d), m_in[k_].reshape(two_d),
                            v_in[k_].reshape(two_d))
        deltas[k_], new_m[k_], new_v[k_] = dl.reshape(shape), nm.reshape(shape), nv.reshape(shape)

    return (loss, grad_x[None], *[grads[k_] for k_ in names], *[deltas[k_] for k_ in names],
            *[new_m[k_] for k_ in names], *[new_v[k_] for k_ in names])
```

```python
import functools
import math

import jax
import jax.numpy as jnp
from jax import lax
from jax.experimental import pallas as pl
from jax.experimental.pallas import tpu as pltpu

F32 = jnp.float32
BF16 = jnp.bfloat16
MESH = pl.DeviceIdType.MESH

LANES = 128
SUBLANES = 8
BF16_ROWS = 16
VMEM_LIMIT = 48 * 2 ** 20
N_CHIPS = 4
N_DEV = 8

NOPE = 128
ROPE = 64
VDIM = 128
QK_PAD = 256
DHD = 128
DIL_STEPS = 128
DIL_BRANCHES = ((128, 1), (512, 4), (2048, 16))
ROPE_THETA = 10000.0
S5_GROUP = 16
S5_STATE = 64
SLAB_GROUPS = LANES // S5_GROUP
SLAB_COLS = SLAB_GROUPS * S5_STATE
NEG = -1e30
LN_EPS = 1e-5
RMS_EPS = 1e-6

ADAM_LR = 0.001
ADAM_B1 = 0.9
ADAM_B2 = 0.999
ADAM_EPS = 1e-08
ADAM_WD = 0.01
ADAM_STEP = 10

NN = ((1,), (0,))
NT = ((1,), (1,))
TN = ((0,), (0,))


def _dot(a, b, dims):
    return lax.dot_general(a, b, (dims, ((), ())), preferred_element_type=F32)


def _bf(v):
    return v.astype(BF16)


def _pick(n, target, q=LANES, also=0):
    g = math.gcd(n, also) if also else n
    if g <= target and g == n:
        return n
    best = None
    for t in range(q, min(g, target) + 1, q):
        if g % t == 0:
            best = t
    assert best is not None, (n, target, q, also)
    return best


def _params(*sem):
    return pltpu.CompilerParams(dimension_semantics=sem, vmem_limit_bytes=VMEM_LIMIT)


def _sigmoid(v):
    return 1.0 / (1.0 + jnp.exp(-v))


def _matmul(name, a, b, form, out_dtype=F32, a_win=None, b_win=None, tm=1024, tn=1024, tk=2048):
    c0, aw = a_win if a_win else (0, a.shape[1])
    if form == 'nt':
        assert b_win is None
        n, kdim = b.shape
        d0 = 0
    else:
        kdim = b.shape[0]
        d0, n = b_win if b_win else (0, b.shape[1])
    if form == 'tn':
        m = aw
        assert a.shape[0] == kdim, (name, a.shape, b.shape)
        tm = _pick(m, tm, also=c0)
        tk = _pick(kdim, tk)
        a_off = c0 // tm
    else:
        m = a.shape[0]
        assert aw == kdim, (name, a.shape, b.shape, a_win)
        tm = _pick(m, tm)
        tk = _pick(kdim, tk, also=c0)
        a_off = c0 // tk
    tn = _pick(n, tn, also=d0)
    b_off = d0 // tn
    nk = kdim // tk
    dims = {'nn': NN, 'nt': NT, 'tn': TN}[form]

    def body(a_ref, b_ref, o_ref, *acc):
        prod = _dot(_bf(a_ref[...]), _bf(b_ref[...]), dims)
        if nk == 1:
            o_ref[...] = prod.astype(o_ref.dtype)
            return
        acc_ref, = acc
        k = pl.program_id(2)

        @pl.when(k == 0)
        def _():
            acc_ref[...] = prod

        @pl.when((k > 0) & (k < nk - 1))
        def _():
            acc_ref[...] += prod

        @pl.when(k == nk - 1)
        def _():
            o_ref[...] = (acc_ref[...] + prod).astype(o_ref.dtype)

    if form == 'tn':
        a_spec = pl.BlockSpec((tk, tm), lambda i, j, k: (k, i + a_off))
    else:
        a_spec = pl.BlockSpec((tm, tk), lambda i, j, k: (i, k + a_off))
    if form == 'nt':
        b_spec = pl.BlockSpec((tn, tk), lambda i, j, k: (j, k))
    else:
        b_spec = pl.BlockSpec((tk, tn), lambda i, j, k: (k, j + b_off))
    return pl.pallas_call(
        body, name=name,
        grid=(m // tm, n // tn, nk),
        in_specs=[a_spec, b_spec],
        out_specs=pl.BlockSpec((tm, tn), lambda i, j, k: (i, j)),
        out_shape=jax.ShapeDtypeStruct((m, n), out_dtype),
        scratch_shapes=[pltpu.VMEM((tm, tn), F32)] if nk > 1 else [],
        compiler_params=_params("parallel", "parallel", "arbitrary"),
    )(a, b)


def _nat(tile, width, cb=0):
    return pl.BlockSpec((tile, width), lambda i: (i, cb))


def _perm(tile, width, seg_tiles, ncb=1, cb=0):
    return pl.BlockSpec((tile, width), lambda i: (i % seg_tiles, (i // seg_tiles) * ncb + cb))


def _whole(shape):
    return pl.BlockSpec(shape, lambda i: (0,) * len(shape))


def _perm_view(a):
    s, w = a.shape
    return a.reshape(s // SUBLANES, SUBLANES * w)


def _row_spec(a, layout, tile, width, ncb=1, cb=0):
    if layout == 'nat':
        return a, _nat(tile, width, cb)
    seg_tiles = a.shape[0] // SUBLANES // tile
    return _perm_view(a), _perm(tile, width, seg_tiles, ncb, cb)


def _ln_fwd(name, alpha, a, adds, gate, g, b, want_perm=False, tile=256):
    s, d = a.shape
    n_add = len(adds)
    has_gate = gate is not None

    def body(*refs):
        a_ref = refs[0]
        add_refs = refs[1:1 + n_add]
        pos = 1 + n_add
        if has_gate:
            val_ref, pre_ref = refs[pos], refs[pos + 1]
            pos += 2
        g_ref, b_ref = refs[pos], refs[pos + 1]
        outs = refs[pos + 2:]
        z = alpha * a_ref[...]
        for r in add_refs:
            z = z + r[...]
        if has_gate:
            z = z + val_ref[...] * _sigmoid(pre_ref[...])
        mu = jnp.mean(z, axis=-1, keepdims=True)
        zc = z - mu
        var = jnp.mean(zc * zc, axis=-1, keepdims=True)
        rstd = lax.rsqrt(var + LN_EPS)
        xhat = zc * rstd
        h = xhat * g_ref[...] + b_ref[...]
        outs[0][...] = h
        outs[1][...] = xhat
        outs[2][...] = jnp.broadcast_to(rstd, (tile, LANES))
        outs[3][...] = _bf(h)
        if want_perm:
            outs[4][...] = h

    ins, specs = [a], [_nat(tile, d)]
    for arr, layout in adds:
        x_, sp = _row_spec(arr, layout, tile, d)
        ins.append(x_)
        specs.append(sp)
    if has_gate:
        layout = gate[0]
        for arr, ncb, cb in gate[1:]:
            x_, sp = _row_spec(arr, layout, tile, d, ncb=ncb, cb=cb)
            ins.append(x_)
            specs.append(sp)
    ins += [g.reshape(1, d), b.reshape(1, d)]
    specs += [_whole((1, d)), _whole((1, d))]
    out_shape = [jax.ShapeDtypeStruct((s, d), F32), jax.ShapeDtypeStruct((s, d), F32),
                 jax.ShapeDtypeStruct((s, LANES), F32), jax.ShapeDtypeStruct((s, d), BF16)]
    out_specs = [_nat(tile, d), _nat(tile, d), _nat(tile, LANES), _nat(tile, d)]
    if want_perm:
        seg_tiles = s // SUBLANES // tile
        out_shape.append(jax.ShapeDtypeStruct((s // SUBLANES, SUBLANES * d), F32))
        out_specs.append(_perm(tile, d, seg_tiles))
    res = pl.pallas_call(
        body, name=name, grid=(s // tile,), in_specs=specs, out_specs=out_specs, out_shape=out_shape,
        compiler_params=_params("parallel"),
    )(*ins)
    return res[0], res[1], res[2], res[3], (res[4].reshape(s, d) if want_perm else None)


def _ln_bwd(name, dparts, xhat, rstd, g, gate=None, tile=256):
    s, d = xhat.shape
    n_part = len(dparts)
    coefs = [c for _, _, c in dparts]
    has_gate = gate is not None

    def body(*refs):
        part_refs = refs[:n_part]
        xhat_ref, rstd_ref, g_ref = refs[n_part:n_part + 3]
        pos = n_part + 3
        if has_gate:
            val_ref, pre_ref = refs[pos], refs[pos + 1]
            pos += 2
        outs = list(refs[pos:])
        dz_ref = outs.pop(0)
        dzb_ref = outs.pop(0)
        dgate_ref = outs.pop(0) if has_gate else None
        dg_ref, db_ref = outs
        dh = coefs[0] * part_refs[0][...]
        for c, r in zip(coefs[1:], part_refs[1:]):
            dh = dh + c * r[...]
        xh = xhat_ref[...]
        dxh = dh * g_ref[...]
        m1 = jnp.mean(dxh, axis=-1, keepdims=True)
        m2 = jnp.mean(dxh * xh, axis=-1, keepdims=True)
        dz = rstd_ref[:, 0:1] * (dxh - m1 - xh * m2)
        dz_ref[...] = dz
        dzb_ref[...] = _bf(dz)
        if has_gate:
            sg = _sigmoid(pre_ref[...])
            dval = dz * sg
            dpre = dz * val_ref[...] * sg * (1.0 - sg)
            dgate_ref[...] = jnp.concatenate([_bf(dval), _bf(dpre)], axis=1)

        @pl.when(pl.program_id(0) == 0)
        def _():
            dg_ref[...] = jnp.zeros_like(dg_ref)
            db_ref[...] = jnp.zeros_like(db_ref)

        dg_ref[0:1, :] += jnp.sum(dh * xh, axis=0, keepdims=True)
        db_ref[0:1, :] += jnp.sum(dh, axis=0, keepdims=True)

    ins, specs = [], []
    for arr, layout, _ in dparts:
        x_, sp = _row_spec(arr, layout, tile, d)
        ins.append(x_)
        specs.append(sp)
    ins += [xhat, rstd, g.reshape(1, d)]
    specs += [_nat(tile, d), _nat(tile, LANES), _whole((1, d))]
    gate_layout = None
    if has_gate:
        gate_layout = gate[0]
        for arr, ncb, cb in gate[1:]:
            x_, sp = _row_spec(arr, gate_layout, tile, d, ncb=ncb, cb=cb)
            ins.append(x_)
            specs.append(sp)
    seg_tiles = s // SUBLANES // tile
    out_shape = [jax.ShapeDtypeStruct((s, d), F32), jax.ShapeDtypeStruct((s, d), BF16)]
    out_specs = [_nat(tile, d), _nat(tile, d)]
    if has_gate:
        if gate_layout == 'nat':
            out_shape.append(jax.ShapeDtypeStruct((s, 2 * d), BF16))
            out_specs.append(_nat(tile, 2 * d))
        else:
            out_shape.append(jax.ShapeDtypeStruct((s // SUBLANES, SUBLANES * 2 * d), BF16))
            out_specs.append(_perm(tile, 2 * d, seg_tiles))
    out_shape += [jax.ShapeDtypeStruct((SUBLANES, d), F32)] * 2
    out_specs += [_whole((SUBLANES, d))] * 2
    res = list(pl.pallas_call(
        body, name=name, grid=(s // tile,), in_specs=specs, out_specs=out_specs, out_shape=out_shape,
        compiler_params=_params("arbitrary"),
    )(*ins))
    out = [res.pop(0), res.pop(0)]
    if has_gate:
        out.append(res.pop(0).reshape(s, 2 * d))
    out += [res[0][0], res[1][0]]
    return out


def _loss_partial(h, target, tile=256):
    s, d = h.shape

    def body(h_ref, t_ref, o_ref):
        @pl.when(pl.program_id(0) == 0)
        def _():
            o_ref[...] = jnp.zeros_like(o_ref)

        e = h_ref[...] - t_ref[...]
        sq = e * e
        part = sq[:, 0:LANES]
        for k in range(1, d // LANES):
            part = part + sq[:, k * LANES:(k + 1) * LANES]
        o_ref[0:1, :] += jnp.sum(part, axis=0, keepdims=True) * (0.5 / d)

    return pl.pallas_call(
        body, name="loss_partial", grid=(s // tile,), in_specs=[_nat(tile, d), _nat(tile, d)],
        out_specs=_whole((SUBLANES, LANES)), out_shape=jax.ShapeDtypeStruct((SUBLANES, LANES), F32),
        compiler_params=_params("arbitrary"),
    )(h, target)


def _swiglu_fwd(name, gu, tile=256):
    s, f2 = gu.shape
    f = f2 // 2
    cw = _pick(f, 1408)
    ncb = f // cw

    def body(g_ref, u_ref, o_ref):
        gg = g_ref[...]
        o_ref[...] = _bf(gg * _sigmoid(gg) * u_ref[...])

    return pl.pallas_call(
        body, name=name, grid=(s // tile, ncb),
        in_specs=[pl.BlockSpec((tile, cw), lambda i, j: (i, j)), pl.BlockSpec((tile, cw), lambda i, j: (i, j + ncb))],
        out_specs=pl.BlockSpec((tile, cw), lambda i, j: (i, j)),
        out_shape=jax.ShapeDtypeStruct((s, f), BF16), compiler_params=_params("parallel", "parallel"),
    )(gu, gu)


def _swiglu_bwd(name, gu, dact, tile=128):
    s, f2 = gu.shape
    f = f2 // 2

    def body(g_ref, u_ref, da_ref, o_ref):
        gg = g_ref[...]
        sg = _sigmoid(gg)
        da = da_ref[...].astype(F32)
        silu = gg * sg
        o_ref[:, :f] = _bf(da * u_ref[...] * (sg + silu * (1.0 - sg)))
        o_ref[:, f:] = _bf(da * silu)

    return pl.pallas_call(
        body, name=name, grid=(s // tile,),
        in_specs=[_nat(tile, f, 0), _nat(tile, f, 1), _nat(tile, f)], out_specs=_nat(tile, f2),
        out_shape=jax.ShapeDtypeStruct((s, f2), BF16), compiler_params=_params("parallel"),
    )(gu, gu, dact)


def _rms_fwd(proj, ql, kvl, gq, gkv, tile=256):
    s = proj.shape[0]
    assert ql == kvl

    def body(q_ref, kv_ref, gq_ref, gkv_ref, o_ref):
        def nrm(x, gg):
            return x * lax.rsqrt(jnp.mean(x * x, axis=-1, keepdims=True) + RMS_EPS) * gg

        o_ref[...] = jnp.concatenate([_bf(nrm(q_ref[...], gq_ref[...])), _bf(nrm(kv_ref[...], gkv_ref[...]))], axis=1)

    return pl.pallas_call(
        body, name="mla_rms_fwd", grid=(s // tile,),
        in_specs=[_nat(tile, ql, 0), _nat(tile, kvl, 1), _whole((1, ql)), _whole((1, kvl))],
        out_specs=_nat(tile, ql + kvl), out_shape=jax.ShapeDtypeStruct((s, ql + kvl), BF16),
        compiler_params=_params("parallel"),
    )(proj, proj, gq.reshape(1, ql), gkv.reshape(1, kvl))


def _rope_coeffs(pos, invf):
    ang = pos * invf
    cs, sn = jnp.cos(ang), jnp.sin(ang)
    lane = lax.broadcasted_iota(jnp.int32, ang.shape, 1)
    half = ROPE // 2
    c = jnp.where(lane < ROPE, cs, 0.0)
    sa = jnp.where(lane < half, -sn, 0.0)
    sb = jnp.where((lane >= half) & (lane < ROPE), sn, 0.0)
    return c, sa, sb


def _rope_prep(q_raw, kv, proj, kpe_cb, pos, invf, heads, tile=256):
    s = q_raw.shape[0]
    half = ROPE // 2

    def body(q_ref, kv_ref, kpe_ref, pos_ref, invf_ref, qf_ref, kf_ref, v_ref):
        c, sa, sb = _rope_coeffs(pos_ref[...], invf_ref[...])

        def rope(t):
            return t * c + pltpu.roll(t, LANES - half, 1) * sa + pltpu.roll(t, half, 1) * sb

        kr = _bf(rope(kpe_ref[...]))
        for hh in range(heads):
            o = hh * QK_PAD
            qf_ref[:, o:o + NOPE] = _bf(q_ref[:, o:o + NOPE])
            qf_ref[:, o + NOPE:o + QK_PAD] = _bf(rope(q_ref[:, o + NOPE:o + QK_PAD]))
            kf_ref[:, o:o + NOPE] = _bf(kv_ref[:, o:o + NOPE])
            kf_ref[:, o + NOPE:o + QK_PAD] = kr
            v_ref[:, hh * VDIM:(hh + 1) * VDIM] = _bf(kv_ref[:, o + NOPE:o + QK_PAD])

    w = heads * QK_PAD
    return pl.pallas_call(
        body, name="mla_rope_prep", grid=(s // tile,),
        in_specs=[_nat(tile, w), _nat(tile, w), _nat(tile, LANES, kpe_cb), _nat(tile, 1), _whole((1, LANES))],
        out_specs=[_nat(tile, w), _nat(tile, w), _nat(tile, heads * VDIM)],
        out_shape=[jax.ShapeDtypeStruct((s, w), BF16), jax.ShapeDtypeStruct((s, w), BF16),
                   jax.ShapeDtypeStruct((s, heads * VDIM), BF16)],
        compiler_params=_params("parallel"),
    )(q_raw, kv, proj, pos, invf)


def _rope_unprep(dqf, dkf, dv, pos, invf, heads, tile=256):
    s = dqf.shape[0]
    half = ROPE // 2

    def body(dq_ref, dk_ref, dv_ref, pos_ref, invf_ref, dqr_ref, dkv_ref, dkpe_ref):
        c, sa, sb = _rope_coeffs(pos_ref[...], invf_ref[...])

        def unrope(gt):
            return gt * c + pltpu.roll(gt * sa, half, 1) + pltpu.roll(gt * sb, LANES - half, 1)

        dkpe = jnp.zeros((tile, LANES), F32)
        for hh in range(heads):
            o = hh * QK_PAD
            dqr_ref[:, o:o + NOPE] = _bf(dq_ref[:, o:o + NOPE])
            dqr_ref[:, o + NOPE:o + QK_PAD] = _bf(unrope(dq_ref[:, o + NOPE:o + QK_PAD]))
            dkv_ref[:, o:o + NOPE] = _bf(dk_ref[:, o:o + NOPE])
            dkv_ref[:, o + NOPE:o + QK_PAD] = _bf(dv_ref[:, hh * VDIM:(hh + 1) * VDIM])
            dkpe = dkpe + dk_ref[:, o + NOPE:o + QK_PAD]
        dkpe_ref[...] = unrope(dkpe)

    w = heads * QK_PAD
    return pl.pallas_call(
        body, name="mla_rope_unprep", grid=(s // tile,),
        in_specs=[_nat(tile, w), _nat(tile, w), _nat(tile, heads * VDIM), _nat(tile, 1), _whole((1, LANES))],
        out_specs=[_nat(tile, w), _nat(tile, w), _nat(tile, LANES)],
        out_shape=[jax.ShapeDtypeStruct((s, w), BF16), jax.ShapeDtypeStruct((s, w), BF16),
                   jax.ShapeDtypeStruct((s, LANES), F32)],
        compiler_params=_params("parallel"),
    )(dqf, dkf, dv, pos, invf)


LOG2E = 1.4426950408889634
MLA_SCALE = (NOPE + ROPE) ** -0.5


def _mla_scores_t(k, q, t, masked):
    sc = _dot(k, q, NT) * (MLA_SCALE * LOG2E)
    if masked:
        row = lax.broadcasted_iota(jnp.int32, (t, t), 0)
        col = lax.broadcasted_iota(jnp.int32, (t, t), 1)
        sc = jnp.where(row <= col, sc, NEG)
    return sc


def _mla_fwd(qf, kf, vt, heads, t=512):
    s = qf.shape[0]
    t = min(t, s)
    nq = s // t

    def body(q_ref, k_ref, vt_ref, o_ref, lse_ref, m_ref, l_ref, acc_ref):
        i = pl.program_id(1)
        m_ref[...] = jnp.full_like(m_ref, NEG)
        l_ref[...] = jnp.zeros_like(l_ref)
        acc_ref[...] = jnp.zeros_like(acc_ref)
        q = q_ref[...]

        def block(j, masked):
            r0 = pl.multiple_of(j * t, t)
            sc = _mla_scores_t(k_ref[pl.ds(r0, t), :], q, t, masked)
            m_prev = m_ref[0:1, :]
            m_new = jnp.maximum(m_prev, jnp.max(sc, axis=0, keepdims=True))
            corr = jnp.exp2(m_prev - m_new)
            p = jnp.exp2(sc - m_new)
            l_new = corr * l_ref[0:1, :] + jnp.sum(p, axis=0, keepdims=True)
            acc_ref[...] = corr * acc_ref[...] + _dot(vt_ref[:, pl.ds(r0, t)], _bf(p), NN)
            m_ref[...] = jnp.broadcast_to(m_new, (SUBLANES, t))
            l_ref[...] = jnp.broadcast_to(l_new, (SUBLANES, t))

        def unmasked(j, carry):
            block(j, False)
            return carry

        lax.fori_loop(0, i, unmasked, 0)
        block(i, True)
        o_ref[...] = (acc_ref[...] / l_ref[0:1, :]).T
        lse_ref[...] = m_ref[...] + jnp.log(l_ref[...]) * LOG2E

    return pl.pallas_call(
        body, name="mla_flash_fwd", grid=(heads, nq),
        in_specs=[pl.BlockSpec((t, QK_PAD), lambda h, i: (i, h)), pl.BlockSpec((s, QK_PAD), lambda h, i: (0, h)),
                  pl.BlockSpec((VDIM, s), lambda h, i: (h, 0))],
        out_specs=[pl.BlockSpec((t, VDIM), lambda h, i: (i, h)), pl.BlockSpec((SUBLANES, t), lambda h, i: (h, i))],
        out_shape=[jax.ShapeDtypeStruct((s, heads * VDIM), F32), jax.ShapeDtypeStruct((heads * SUBLANES, s), F32)],
        scratch_shapes=[pltpu.VMEM((SUBLANES, t), F32), pltpu.VMEM((SUBLANES, t), F32), pltpu.VMEM((VDIM, t), F32)],
        compiler_params=_params("parallel", "arbitrary"),
    )(qf, kf, vt)


def _mla_bwd(qf, kf, v, do, lse_t, delta_t, heads, do_cb0, t=512):
    s = qf.shape[0]
    t = min(t, s)
    nq = s // t

    def body(q_ref, k_ref, v_ref, do_ref, lse_ref, dl_ref, dq_ref, dk_ref, dv_ref, acc_ref):
        i = pl.program_id(1)

        @pl.when(i == 0)
        def _():
            dk_ref[...] = jnp.zeros_like(dk_ref)
            dv_ref[...] = jnp.zeros_like(dv_ref)

        acc_ref[...] = jnp.zeros_like(acc_ref)
        q, dob = q_ref[...], do_ref[...]
        lse, dl = lse_ref[0:1, :], dl_ref[0:1, :]

        def block(j, masked):
            r0 = pl.multiple_of(j * t, t)
            k = k_ref[pl.ds(r0, t), :]
            p = jnp.exp2(_mla_scores_t(k, q, t, masked) - lse)
            dp = _dot(v_ref[pl.ds(r0, t), :], dob, NT)
            ds = _bf(p * (dp - dl) * MLA_SCALE)
            acc_ref[...] += _dot(ds, k, TN)
            dk_ref[pl.ds(r0, t), :] += _dot(ds, q, NN)
            dv_ref[pl.ds(r0, t), :] += _dot(_bf(p), dob, NN)

        def unmasked(j, carry):
            block(j, False)
            return carry

        lax.fori_loop(0, i, unmasked, 0)
        block(i, True)
        dq_ref[...] = acc_ref[...]

    qs = lambda w, off=0: pl.BlockSpec((t, w), lambda h, i: (i, h + off))
    ks = lambda w: pl.BlockSpec((s, w), lambda h, i: (0, h))
    st = pl.BlockSpec((SUBLANES, t), lambda h, i: (h, i))
    return pl.pallas_call(
        body, name="mla_flash_bwd", grid=(heads, nq),
        in_specs=[qs(QK_PAD), ks(QK_PAD), ks(VDIM), qs(VDIM, do_cb0), st, st],
        out_specs=[qs(QK_PAD), ks(QK_PAD), ks(VDIM)],
        out_shape=[jax.ShapeDtypeStruct((s, heads * QK_PAD), F32), jax.ShapeDtypeStruct((s, heads * QK_PAD), F32),
                   jax.ShapeDtypeStruct((s, heads * VDIM), F32)],
        scratch_shapes=[pltpu.VMEM((t, QK_PAD), F32)],
        compiler_params=_params("parallel", "arbitrary"),
    )(qf, kf, v, do, lse_t, delta_t)


def _band_mask(tq, first_block):
    row = lax.broadcasted_iota(jnp.int32, (tq, DIL_STEPS + tq), 0)
    col = lax.broadcasted_iota(jnp.int32, (tq, DIL_STEPS + tq), 1)
    dist = row + DIL_STEPS - col
    valid = (dist >= 0) & (dist <= DIL_STEPS) & (jnp.logical_not(first_block) | (col >= DIL_STEPS))
    return dist, valid


def _dil_scores(q, kp, kc, slope, dil, tq, first_block):
    sc = jnp.concatenate([_dot(q, kp, NT), _dot(q, kc, NT)], axis=1) * (DHD ** -0.5)
    dist, valid = _band_mask(tq, first_block)
    return jnp.where(valid, sc - slope * (dil * dist).astype(F32), NEG)


def _dil_specs(proj_w, dh, tq):
    pwb = proj_w // LANES
    r_of = lambda cb: cb // dh
    h_of = lambda cb: cb % dh
    cur = lambda off: pl.BlockSpec((tq, DHD), lambda cb, i: (i, r_of(cb) * pwb + off + h_of(cb)))
    prev = lambda off: pl.BlockSpec(
        (DIL_STEPS, DHD), lambda cb, i: (jnp.maximum(i * (tq // DIL_STEPS) - 1, 0), r_of(cb) * pwb + off + h_of(cb)))
    return cur, prev


def _dil_fwd(name, proj, slopes, dil, dh, q_cb, tq=512):
    s, pw = proj.shape
    l = s // dil
    tq = min(tq, l)
    nb = l // tq
    k_cb, v_cb = q_cb + dh, q_cb + 2 * dh
    cur, prev = _dil_specs(pw, dh, tq)
    pv = proj.reshape(l, dil * pw)

    def body(q_ref, kc_ref, kp_ref, vc_ref, vp_ref, sl_ref, o_ref, lse_ref):
        i = pl.program_id(1)
        sc = _dil_scores(_bf(q_ref[...]), _bf(kp_ref[...]), _bf(kc_ref[...]), sl_ref[0:1, 0:1], dil, tq, i == 0)
        m = jnp.max(sc, axis=-1, keepdims=True)
        e = jnp.exp(sc - m)
        lsum = jnp.sum(e, axis=-1, keepdims=True)
        pn = e / lsum
        o_ref[...] = (_dot(_bf(pn[:, :DIL_STEPS]), _bf(vp_ref[...]), NN)
                      + _dot(_bf(pn[:, DIL_STEPS:]), _bf(vc_ref[...]), NN))
        lse_ref[...] = jnp.broadcast_to(m + jnp.log(lsum), (tq, LANES))

    ospec = pl.BlockSpec((tq, DHD), lambda cb, i: (i, cb))
    o, lse = pl.pallas_call(
        body, name=name, grid=(dil * dh, nb),
        in_specs=[cur(q_cb), cur(k_cb), prev(k_cb), cur(v_cb), prev(v_cb),
                  pl.BlockSpec((SUBLANES, LANES), lambda cb, i: (cb % dh, 0))],
        out_specs=[ospec, ospec],
        out_shape=[jax.ShapeDtypeStruct((l, dil * dh * DHD), F32)] * 2,
        compiler_params=_params("parallel", "parallel"),
    )(pv, pv, pv, pv, pv, slopes)
    return o.reshape(s, dh * DHD), lse.reshape(s, dh * DHD)


def _dil_bwd_dq(name, proj, slopes, do, lse, delta, dil, dh, q_cb, b_cb0, tq=512):
    s, pw = proj.shape
    mixw = do.shape[1]
    l = s // dil
    tq = min(tq, l)
    nb = l // tq
    k_cb, v_cb = q_cb + dh, q_cb + 2 * dh
    cur, prev = _dil_specs(pw, dh, tq)
    pv = proj.reshape(l, dil * pw)
    mb = mixw // LANES
    mspec = pl.BlockSpec((tq, DHD), lambda cb, i: (i, (cb // dh) * mb + b_cb0 + cb % dh))
    ospec = pl.BlockSpec((tq, DHD), lambda cb, i: (i, cb))

    def body(q_ref, kc_ref, kp_ref, vc_ref, vp_ref, sl_ref, do_ref, lse_ref, dl_ref, dq_ref):
        i = pl.program_id(1)
        kp, kc = _bf(kp_ref[...]), _bf(kc_ref[...])
        sc = _dil_scores(_bf(q_ref[...]), kp, kc, sl_ref[0:1, 0:1], dil, tq, i == 0)
        p = jnp.exp(sc - lse_ref[:, 0:1])
        dob = do_ref[...]
        dp = jnp.concatenate([_dot(dob, _bf(vp_ref[...]), NT), _dot(dob, _bf(vc_ref[...]), NT)], axis=1)
        ds = _bf(p * (dp - dl_ref[:, 0:1]) * (DHD ** -0.5))
        dq_ref[...] = _dot(ds[:, :DIL_STEPS], kp, NN) + _dot(ds[:, DIL_STEPS:], kc, NN)

    dq = pl.pallas_call(
        body, name=name, grid=(dil * dh, nb),
        in_specs=[cur(q_cb), cur(k_cb), prev(k_cb), cur(v_cb), prev(v_cb),
                  pl.BlockSpec((SUBLANES, LANES), lambda cb, i: (cb % dh, 0)), mspec, ospec, mspec],
        out_specs=ospec, out_shape=jax.ShapeDtypeStruct((l, dil * dh * DHD), F32),
        compiler_params=_params("parallel", "parallel"),
    )(pv, pv, pv, pv, pv, slopes, do.reshape(l, dil * mixw), lse.reshape(l, dil * dh * DHD), delta.reshape(l, dil * mixw))
    return dq.reshape(s, dh * DHD)


def _dil_bwd_dkv(name, proj, slopes, do, lse, delta, dil, dh, q_cb, b_cb0, tk=512):
    s, pw = proj.shape
    mixw = do.shape[1]
    l = s // dil
    tk = min(tk, l)
    nb = l // tk
    k_cb, v_cb = q_cb + dh, q_cb + 2 * dh
    pwb, mb = pw // LANES, mixw // LANES
    sub = tk // DIL_STEPS
    last128 = l // DIL_STEPS - 1
    pv = proj.reshape(l, dil * pw)

    def cur(width_blocks, off):
        return pl.BlockSpec((tk, DHD), lambda cb, j: (j, (cb // dh) * width_blocks + off + cb % dh))

    def nxt(width_blocks, off):
        return pl.BlockSpec((DIL_STEPS, DHD), lambda cb, j: (jnp.minimum((j + 1) * sub, last128),
                                                               (cb // dh) * width_blocks + off + cb % dh))

    ocur = pl.BlockSpec((tk, DHD), lambda cb, j: (j, cb))
    onxt = pl.BlockSpec((DIL_STEPS, DHD), lambda cb, j: (jnp.minimum((j + 1) * sub, last128), cb))

    def body(k_ref, v_ref, qc_ref, qn_ref, sl_ref, doc_ref, don_ref, lsec_ref, lsen_ref, dlc_ref, dln_ref,
             dk_ref, dv_ref):
        j = pl.program_id(1)
        slope = sl_ref[0:1, 0:1]
        scale = DHD ** -0.5
        k, v = _bf(k_ref[...]), _bf(v_ref[...])
        qc = _bf(qc_ref[...])
        row = lax.broadcasted_iota(jnp.int32, (tk, tk), 0)
        col = lax.broadcasted_iota(jnp.int32, (tk, tk), 1)
        dist = row - col
        valid = (dist >= 0) & (dist <= DIL_STEPS)
        sc = jnp.where(valid, _dot(qc, k, NT) * scale - slope * (dil * dist).astype(F32), NEG)
        p = jnp.exp(sc - lsec_ref[:, 0:1])
        doc = doc_ref[...]
        ds = _bf(p * (_dot(doc, v, NT) - dlc_ref[:, 0:1]) * scale)
        dv_ref[...] = _dot(_bf(p), doc, TN)
        dk_ref[...] = _dot(ds, qc, TN)
        kl, vl = k[tk - DIL_STEPS:, :], v[tk - DIL_STEPS:, :]
        qn = _bf(qn_ref[...])
        row = lax.broadcasted_iota(jnp.int32, (DIL_STEPS, DIL_STEPS), 0)
        col = lax.broadcasted_iota(jnp.int32, (DIL_STEPS, DIL_STEPS), 1)
        dist = DIL_STEPS + row - col
        valid = (dist <= DIL_STEPS) & (j < nb - 1)
        sc = jnp.where(valid, _dot(qn, kl, NT) * scale - slope * (dil * dist).astype(F32), NEG)
        p = jnp.exp(sc - lsen_ref[:, 0:1])
        don = don_ref[...]
        ds = _bf(p * (_dot(don, vl, NT) - dln_ref[:, 0:1]) * scale)
        dv_ref[tk - DIL_STEPS:, :] += _dot(_bf(p), don, TN)
        dk_ref[tk - DIL_STEPS:, :] += _dot(ds, qn, TN)

    dov = do.reshape(l, dil * mixw)
    dlv = delta.reshape(l, dil * mixw)
    lsv = lse.reshape(l, dil * dh * DHD)
    dk, dv = pl.pallas_call(
        body, name=name, grid=(dil * dh, nb),
        in_specs=[cur(pwb, k_cb), cur(pwb, v_cb), cur(pwb, q_cb), nxt(pwb, q_cb),
                  pl.BlockSpec((SUBLANES, LANES), lambda cb, j: (cb % dh, 0)),
                  cur(mb, b_cb0), nxt(mb, b_cb0), ocur, onxt, cur(mb, b_cb0), nxt(mb, b_cb0)],
        out_specs=[ocur, ocur], out_shape=[jax.ShapeDtypeStruct((l, dil * dh * DHD), F32)] * 2,
        compiler_params=_params("parallel", "parallel"),
    )(pv, pv, pv, pv, slopes, dov, dov, lsv, lsv, dlv, dlv)
    return dk.reshape(s, dh * DHD), dv.reshape(s, dh * DHD)


def _dil_merge(out_a, outs, lses, tile=256):
    s, wa = out_a.shape
    wb = outs[0].shape[1]
    nbr = len(outs)

    def body(*refs):
        a_ref = refs[0]
        o_refs, l_refs = refs[1:1 + nbr], refs[1 + nbr:1 + 2 * nbr]
        att_ref, ob_ref, lse_ref = refs[1 + 2 * nbr:]
        ls = [r[...] for r in l_refs]
        m = ls[0]
        for x_ in ls[1:]:
            m = jnp.maximum(m, x_)
        es = [jnp.exp(x_ - m) for x_ in ls]
        tot = es[0]
        for e in es[1:]:
            tot = tot + e
        ob = (es[0] / tot) * o_refs[0][...]
        for e, r in zip(es[1:], o_refs[1:]):
            ob = ob + (e / tot) * r[...]
        ob_ref[...] = ob
        lse_ref[...] = m + jnp.log(tot)
        att_ref[...] = jnp.concatenate([_bf(a_ref[...]), _bf(ob)], axis=1)

    return pl.pallas_call(
        body, name="dil_merge", grid=(s // tile,),
        in_specs=[_nat(tile, wa)] + [_nat(tile, wb)] * (2 * nbr),
        out_specs=[_nat(tile, wa + wb), _nat(tile, wb), _nat(tile, wb)],
        out_shape=[jax.ShapeDtypeStruct((s, wa + wb), BF16), jax.ShapeDtypeStruct((s, wb), F32),
                   jax.ShapeDtypeStruct((s, wb), F32)],
        compiler_params=_params("parallel"),
    )(out_a, *outs, *lses)


def _attn_bwd_prep(datt, out_a, out_b, tile=256):
    s, mixw = datt.shape
    wa = out_a.shape[1]
    heads_a = wa // LANES

    def body(d_ref, a_ref, b_ref, do_ref, dl_ref, dlt_ref):
        d = d_ref[...]
        do_ref[...] = _bf(d)
        prod = d * jnp.concatenate([a_ref[...], b_ref[...]], axis=1)
        for hh in range(mixw // LANES):
            sl = slice(hh * LANES, (hh + 1) * LANES)
            dl = jnp.broadcast_to(jnp.sum(prod[:, sl], axis=-1, keepdims=True), (tile, LANES))
            dl_ref[:, sl] = dl
            if hh < heads_a:
                dlt_ref[hh * SUBLANES:(hh + 1) * SUBLANES, :] = dl.T[0:SUBLANES, :]

    return pl.pallas_call(
        body, name="attn_bwd_prep", grid=(s // tile,),
        in_specs=[_nat(tile, mixw), _nat(tile, wa), _nat(tile, mixw - wa)],
        out_specs=[_nat(tile, mixw), _nat(tile, mixw), pl.BlockSpec((heads_a * SUBLANES, tile), lambda i: (0, i))],
        out_shape=[jax.ShapeDtypeStruct((s, mixw), BF16), jax.ShapeDtypeStruct((s, mixw), F32),
                   jax.ShapeDtypeStruct((heads_a * SUBLANES, s), F32)],
        compiler_params=_params("parallel"),
    )(datt, out_a, out_b)


def _dproj_assemble(proj, dnq, dnkv, dkpe, dqs, dks, dvs, gq, gkv, ql, tile=256):
    s, pw = proj.shape
    dw = dqs[0].shape[1]
    nbr = len(dqs)

    def body(*refs):
        ql_ref, kvl_ref, dnq_ref, dnkv_ref, dkpe_ref = refs[:5]
        br = refs[5:5 + 3 * nbr]
        gq_ref, gkv_ref = refs[5 + 3 * nbr:7 + 3 * nbr]
        dp_ref, dgq_ref, dgkv_ref = refs[7 + 3 * nbr:]

        @pl.when(pl.program_id(0) == 0)
        def _():
            dgq_ref[...] = jnp.zeros_like(dgq_ref)
            dgkv_ref[...] = jnp.zeros_like(dgkv_ref)

        def rms_bwd(x, dy, gg, dg_ref):
            r = lax.rsqrt(jnp.mean(x * x, axis=-1, keepdims=True) + RMS_EPS)
            xh = x * r
            dxh = dy * gg
            dg_ref[0:1, :] += jnp.sum(dy * xh, axis=0, keepdims=True)
            return r * (dxh - xh * jnp.mean(dxh * xh, axis=-1, keepdims=True))

        pieces = [_bf(rms_bwd(ql_ref[...], dnq_ref[...], gq_ref[...], dgq_ref)),
                  _bf(rms_bwd(kvl_ref[...], dnkv_ref[...], gkv_ref[...], dgkv_ref)),
                  _bf(dkpe_ref[...])]
        for k in range(3):
            acc = br[k * nbr][...]
            for r in br[k * nbr + 1:(k + 1) * nbr]:
                acc = acc + r[...]
            pieces.append(_bf(acc))
        dp_ref[...] = jnp.concatenate(pieces, axis=1)

    res = pl.pallas_call(
        body, name="dproj_assemble", grid=(s // tile,),
        in_specs=[_nat(tile, ql, 0), _nat(tile, ql, 1), _nat(tile, ql), _nat(tile, ql), _nat(tile, LANES)]
        + [_nat(tile, dw)] * (3 * nbr) + [_whole((1, ql)), _whole((1, ql))],
        out_specs=[_nat(tile, pw), _whole((SUBLANES, ql)), _whole((SUBLANES, ql))],
        out_shape=[jax.ShapeDtypeStruct((s, pw), BF16), jax.ShapeDtypeStruct((SUBLANES, ql), F32),
                   jax.ShapeDtypeStruct((SUBLANES, ql), F32)],
        compiler_params=_params("arbitrary"),
    )(proj, proj, dnq, dnkv, dkpe, *dqs, *dks, *dvs, gq.reshape(1, ql), gkv.reshape(1, ql))
    return res[0], res[1][0], res[2][0]


def _axpy(name, alpha, a, b, tile=256):
    s, d = a.shape

    def body(a_ref, b_ref, o_ref):
        o_ref[...] = alpha * a_ref[...] + b_ref[...]

    return pl.pallas_call(
        body, name=name, grid=(s // tile,), in_specs=[_nat(tile, d), _nat(tile, d)], out_specs=_nat(tile, d),
        out_shape=jax.ShapeDtypeStruct((s, d), F32), compiler_params=_params("parallel"),
    )(a, b)


def _cmul(ar, ai, br, bi):
    return ar * br - ai * bi, ar * bi + ai * br


def _s5_discretise(a_re, a_im, log_dt, b_re, b_im, n_sq):
    shape = a_re.shape

    def body(ar_ref, ai_ref, ldt_ref, br_ref, bi_ref, abr_ref, abi_ref, apr_ref, api_ref, bbr_ref, bbi_ref):
        ar, ai = ar_ref[...], ai_ref[...]
        dt = jnp.exp(ldt_ref[...])
        e = jnp.exp(ar * dt)
        abr, abi = e * jnp.cos(ai * dt), e * jnp.sin(ai * dt)
        den = ar * ar + ai * ai
        qr = ((abr - 1.0) * ar + abi * ai) / den
        qi = (abi * ar - (abr - 1.0) * ai) / den
        bbr, bbi = _cmul(qr, qi, br_ref[...], bi_ref[...])
        abr_ref[...], abi_ref[...] = abr, abi
        bbr_ref[...], bbi_ref[...] = bbr, bbi
        pr, pi = abr, abi
        for _ in range(n_sq):
            pr, pi = _cmul(pr, pi, pr, pi)
        apr_ref[...], api_ref[...] = pr, pi

    return pl.pallas_call(
        body, name="s5_discretise", out_shape=[jax.ShapeDtypeStruct(shape, F32)] * 6,
        compiler_params=pltpu.CompilerParams(vmem_limit_bytes=VMEM_LIMIT),
    )(a_re, a_im, log_dt, b_re, b_im)


def _s5_discretise_bwd(a16, b16, ag, gab, gbb):
    rows, p = a16[0].shape
    g = rows // S5_GROUP

    def disc(ar, ai, ldt):
        dt = jnp.exp(ldt)
        e = jnp.exp(ar * dt)
        abr, abi = e * jnp.cos(ai * dt), e * jnp.sin(ai * dt)
        den = ar * ar + ai * ai
        inv_r, inv_i = ar / den, -ai / den
        qr, qi = _cmul(abr - 1.0, abi, inv_r, inv_i)
        return dt, abr, abi, inv_r, inv_i, qr, qi

    def body(ar16_ref, ai16_ref, ldt16_ref, br_ref, bi_ref, ar_ref, ai_ref, ldt_ref, gar_ref, gai_ref, gbr_ref, gbi_ref,
             dar_ref, dai_ref, dldt_ref, dbr_ref, dbi_ref):
        _, _, _, _, _, qr16, qi16 = disc(ar16_ref[...], ai16_ref[...], ldt16_ref[...])
        gbr, gbi = gbr_ref[...], gbi_ref[...]
        dbr_ref[...], dbi_ref[...] = _cmul(qr16, -qi16, gbr, gbi)
        cr, ci = _cmul(br_ref[...], -bi_ref[...], gbr, gbi)
        gqr = jnp.sum(cr.reshape(g, S5_GROUP, p), axis=1)
        gqi = jnp.sum(ci.reshape(g, S5_GROUP, p), axis=1)
        ar, ai = ar_ref[...], ai_ref[...]
        dt, abr, abi, inv_r, inv_i, qr, qi = disc(ar, ai, ldt_ref[...])
        t_r, t_i = _cmul(inv_r, -inv_i, gqr, gqi)
        gab_r = gar_ref[...] + t_r
        gab_i = gai_ref[...] + t_i
        qa_r, qa_i = _cmul(qr, qi, inv_r, inv_i)
        a1_r, a1_i = _cmul(qa_r, -qa_i, gqr, gqi)
        gl_r, gl_i = _cmul(abr, -abi, gab_r, gab_i)
        dar_ref[...] = dt * gl_r - a1_r
        dai_ref[...] = dt * gl_i - a1_i
        gdt = jnp.sum(ar * gl_r + ai * gl_i, axis=-1, keepdims=True)
        dldt_ref[...] = gdt * dt[:, 0:1]

    return pl.pallas_call(
        body, name="s5_discretise_bwd",
        out_shape=[jax.ShapeDtypeStruct((g, p), F32), jax.ShapeDtypeStruct((g, p), F32),
                   jax.ShapeDtypeStruct((g, 1), F32), jax.ShapeDtypeStruct((rows, p), F32),
                   jax.ShapeDtypeStruct((rows, p), F32)],
        compiler_params=pltpu.CompilerParams(vmem_limit_bytes=VMEM_LIMIT),
    )(*a16, *b16, *ag, *gab, *gbb)


def _slab_tile(re, im, nsl):
    row = jnp.concatenate([re.reshape(nsl, SLAB_COLS), im.reshape(nsl, SLAB_COLS)], axis=-1)
    return jnp.repeat(row, SUBLANES, axis=0)


def _slab_in_matrix(b_re, b_im, nsl):
    eye = jnp.eye(SLAB_GROUPS, dtype=F32)

    def blk(b):
        b = b.reshape(nsl, SLAB_GROUPS, S5_GROUP, S5_STATE)
        return jnp.einsum('sgcp,gh->sgchp', b, eye).reshape(nsl, LANES, SLAB_COLS)

    return jnp.concatenate([blk(b_re), blk(b_im)], axis=-1)


def _slab_in_extract(m, nsl):
    eye = jnp.eye(SLAB_GROUPS, dtype=F32)

    def ext(x_):
        x_ = x_.reshape(nsl, SLAB_GROUPS, S5_GROUP, SLAB_GROUPS, S5_STATE)
        return jnp.einsum('sgchp,gh->sgcp', x_, eye).reshape(nsl * LANES, S5_STATE)

    return ext(m[..., :SLAB_COLS]), ext(m[..., SLAB_COLS:])


def _slab_out_matrix(c_re, c_im, nsl):
    eye = jnp.eye(SLAB_GROUPS, dtype=F32)

    def blk(c):
        c = c.reshape(nsl, SLAB_GROUPS, S5_GROUP, S5_STATE)
        return jnp.einsum('sgcp,gh->sgphc', c, eye).reshape(nsl, SLAB_COLS, LANES)

    return jnp.concatenate([blk(c_re), -blk(c_im)], axis=1)


def _slab_out_extract(m, nsl):
    eye = jnp.eye(SLAB_GROUPS, dtype=F32)

    def ext(x_):
        x_ = x_.reshape(nsl, SLAB_GROUPS, S5_STATE, SLAB_GROUPS, S5_GROUP)
        return jnp.einsum('sgphc,gh->sgcp', x_, eye).reshape(nsl * SLAB_GROUPS, S5_GROUP, S5_STATE)

    return ext(m[:, :SLAB_COLS]), -ext(m[:, SLAB_COLS:])


def _gelu(y):
    t = jnp.tanh(0.7978845608028654 * (y + 0.044715 * y * y * y))
    return 0.5 * y * (1.0 + t)


def _gelu_grad(y):
    t = jnp.tanh(0.7978845608028654 * (y + 0.044715 * y * y * y))
    return 0.5 * (1.0 + t) + 0.5 * y * (1.0 - t * t) * 0.7978845608028654 * (1.0 + 3.0 * 0.044715 * y * y)


def _scan_rows(ref, n_steps, ar, ai, state, reverse, conj):
    sgn = -1.0 if conj else 1.0

    def step(k, carry):
        xr, xi = carry
        t = (n_steps - 1 - k) if reverse else k
        r0 = pl.multiple_of(t * SUBLANES, SUBLANES)
        nr = ar * xr - sgn * ai * xi + ref[pl.ds(r0, SUBLANES), :SLAB_COLS]
        ni = ar * xi + sgn * ai * xr + ref[pl.ds(r0, SUBLANES), SLAB_COLS:]
        ref[pl.ds(r0, SUBLANES), :SLAB_COLS] = nr
        ref[pl.ds(r0, SUBLANES), SLAB_COLS:] = ni
        return nr, ni

    return lax.fori_loop(0, n_steps, step, state, unroll=4)


def _s5_pass1(hp, bblk, ab_tile, rc=1024):
    s, d = hp.shape
    nsl = d // LANES
    rc = min(rc, s)
    nch = s // rc
    w = 2 * SLAB_COLS

    def body(u_ref, b_ref, ab_ref, x_ref, end_ref, st_ref):
        j = pl.program_id(1)

        @pl.when(j == 0)
        def _():
            st_ref[...] = jnp.zeros_like(st_ref)

        x_ref[...] = _dot(_bf(u_ref[...]), b_ref[0], NN)
        xr, xi = _scan_rows(x_ref, rc // SUBLANES, ab_ref[:, :SLAB_COLS], ab_ref[:, SLAB_COLS:],
                            (st_ref[:, :SLAB_COLS], st_ref[:, SLAB_COLS:]), False, False)
        st_ref[:, :SLAB_COLS] = xr
        st_ref[:, SLAB_COLS:] = xi

        @pl.when(j == nch - 1)
        def _():
            end_ref[...] = st_ref[...]

    return pl.pallas_call(
        body, name="s5_scan_local", grid=(nsl, nch),
        in_specs=[pl.BlockSpec((rc, LANES), lambda sl, j: (j, sl)), pl.BlockSpec((1, LANES, w), lambda sl, j: (sl, 0, 0)),
                  pl.BlockSpec((SUBLANES, w), lambda sl, j: (sl, 0))],
        out_specs=[pl.BlockSpec((rc, w), lambda sl, j: (j, sl)), pl.BlockSpec((SUBLANES, w), lambda sl, j: (sl, 0))],
        out_shape=[jax.ShapeDtypeStruct((s, nsl * w), F32), jax.ShapeDtypeStruct((nsl * SUBLANES, w), F32)],
        scratch_shapes=[pltpu.VMEM((SUBLANES, w), F32)],
        compiler_params=_params("parallel", "arbitrary"),
    )(hp, bblk, ab_tile)


def _s5_carry(name, ends, ap_tile, reverse):
    rows, w = ends.shape
    nsl = rows // SUBLANES
    sgn = -1.0 if reverse else 1.0

    def body(e_ref, ap_ref, c_ref):
        pr, pi = ap_ref[0:1, :SLAB_COLS], sgn * ap_ref[0:1, SLAB_COLS:]
        tr = jnp.zeros((1, SLAB_COLS), F32)
        ti = jnp.zeros((1, SLAB_COLS), F32)
        order = range(SUBLANES - 1, -1, -1) if reverse else range(SUBLANES)
        for seg in order:
            c_ref[seg:seg + 1, :SLAB_COLS] = tr
            c_ref[seg:seg + 1, SLAB_COLS:] = ti
            mr, mi = _cmul(pr, pi, tr, ti)
            tr = e_ref[seg:seg + 1, :SLAB_COLS] + mr
            ti = e_ref[seg:seg + 1, SLAB_COLS:] + mi

    spec = pl.BlockSpec((SUBLANES, w), lambda sl: (sl, 0))
    return pl.pallas_call(
        body, name=name, grid=(nsl,), in_specs=[spec, spec], out_specs=spec,
        out_shape=jax.ShapeDtypeStruct((rows, w), F32), compiler_params=_params("parallel"),
    )(ends, ap_tile)


def _s5_pass2(xloc, cin, ab_tile, cblk, hp, dvec, rc=1024):
    s, d = hp.shape
    nsl = d // LANES
    rc = min(rc, s)
    nch = s // rc
    w = 2 * SLAB_COLS

    def body(xl_ref, cin_ref, ab_ref, c_ref, h_ref, d_ref, x_ref, y_ref, z_ref, st_ref):
        j = pl.program_id(1)

        @pl.when(j == 0)
        def _():
            st_ref[...] = cin_ref[...]

        x_ref[...] = jnp.zeros_like(x_ref)
        zr, zi = _scan_rows(x_ref, rc // SUBLANES, ab_ref[:, :SLAB_COLS], ab_ref[:, SLAB_COLS:],
                            (st_ref[:, :SLAB_COLS], st_ref[:, SLAB_COLS:]), False, False)
        st_ref[:, :SLAB_COLS] = zr
        st_ref[:, SLAB_COLS:] = zi
        x = x_ref[...] + xl_ref[...]
        x_ref[...] = x
        y = _dot(_bf(x), c_ref[0], NN) + d_ref[...] * h_ref[...]
        y_ref[...] = y
        z_ref[...] = _bf(_gelu(y))

    tile = lambda wd: pl.BlockSpec((rc, wd), lambda sl, j: (j, sl))
    small = pl.BlockSpec((SUBLANES, w), lambda sl, j: (sl, 0))
    return pl.pallas_call(
        body, name="s5_scan_carry_out", grid=(nsl, nch),
        in_specs=[tile(w), small, small, pl.BlockSpec((1, w, LANES), lambda sl, j: (sl, 0, 0)), tile(LANES),
                  pl.BlockSpec((1, LANES), lambda sl, j: (0, sl))],
        out_specs=[tile(w), tile(LANES), tile(LANES)],
        out_shape=[jax.ShapeDtypeStruct((s, nsl * w), F32), jax.ShapeDtypeStruct((s, d), F32),
                   jax.ShapeDtypeStruct((s, d), BF16)],
        scratch_shapes=[pltpu.VMEM((SUBLANES, w), F32)],
        compiler_params=_params("parallel", "arbitrary"),
    )(xloc, cin, ab_tile, cblk, hp, dvec)


def _s5_bwd_pass1(dzg, ypre, cblk, ab_tile, hp, rc=1024):
    s, d = hp.shape
    nsl = d // LANES
    rc = min(rc, s)
    nch = s // rc
    w = 2 * SLAB_COLS

    def body(dz_ref, y_ref, c_ref, ab_ref, h_ref, lam_ref, st_out_ref, dy_ref, dd_ref, st_ref):
        j = pl.program_id(1)

        @pl.when(j == 0)
        def _():
            st_ref[...] = jnp.zeros_like(st_ref)
            dd_ref[...] = jnp.zeros_like(dd_ref)

        dy = dz_ref[...] * _gelu_grad(y_ref[...])
        dy_ref[...] = dy
        dd_ref[0:1, :] += jnp.sum(dy * h_ref[...], axis=0, keepdims=True)
        lam_ref[...] = _dot(_bf(dy), c_ref[0], NT)
        lr, li = _scan_rows(lam_ref, rc // SUBLANES, ab_ref[:, :SLAB_COLS], ab_ref[:, SLAB_COLS:],
                            (st_ref[:, :SLAB_COLS], st_ref[:, SLAB_COLS:]), True, True)
        st_ref[:, :SLAB_COLS] = lr
        st_ref[:, SLAB_COLS:] = li

        @pl.when(j == nch - 1)
        def _():
            st_out_ref[...] = st_ref[...]

    tile = lambda wd: pl.BlockSpec((rc, wd), lambda sl, j: (nch - 1 - j, sl))
    small = pl.BlockSpec((SUBLANES, w), lambda sl, j: (sl, 0))
    return pl.pallas_call(
        body, name="s5_adjoint_local", grid=(nsl, nch),
        in_specs=[tile(LANES), tile(LANES), pl.BlockSpec((1, w, LANES), lambda sl, j: (sl, 0, 0)), small, tile(LANES)],
        out_specs=[tile(w), small, tile(LANES), pl.BlockSpec((SUBLANES, LANES), lambda sl, j: (0, sl))],
        out_shape=[jax.ShapeDtypeStruct((s, nsl * w), F32), jax.ShapeDtypeStruct((nsl * SUBLANES, w), F32),
                   jax.ShapeDtypeStruct((s, d), F32), jax.ShapeDtypeStruct((SUBLANES, d), F32)],
        scratch_shapes=[pltpu.VMEM((SUBLANES, w), F32)],
        compiler_params=_params("parallel", "arbitrary"),
    )(dzg, ypre, cblk, ab_tile, hp)


def _s5_bwd_pass2(lamloc, cinl, ab_tile, xtrue, cinx, hp, dy, bblk, dvec, rc=1024):
    s, d = hp.shape
    nsl = d // LANES
    rc = min(rc, s)
    nch = s // rc
    w = 2 * SLAB_COLS
    n_steps = rc // SUBLANES

    def body(ll_ref, cl_ref, ab_ref, x_ref, xp_ref, cx_ref, h_ref, dy_ref, b_ref, d_ref,
             du_ref, db_ref, dc_ref, da_ref, st_ref, lam_ref, acc_ref):
        j = pl.program_id(1)

        @pl.when(j == 0)
        def _():
            st_ref[...] = cl_ref[...]
            acc_ref[...] = jnp.zeros_like(acc_ref)
            db_ref[...] = jnp.zeros_like(db_ref)
            dc_ref[...] = jnp.zeros_like(dc_ref)

        ar, ai = ab_ref[:, :SLAB_COLS], ab_ref[:, SLAB_COLS:]
        lam_ref[...] = jnp.zeros_like(lam_ref)
        zr, zi = _scan_rows(lam_ref, n_steps, ar, ai, (st_ref[:, :SLAB_COLS], st_ref[:, SLAB_COLS:]), True, True)
        st_ref[:, :SLAB_COLS] = zr
        st_ref[:, SLAB_COLS:] = zi
        lam_ref[...] = lam_ref[...] + ll_ref[...]

        def step(k, carry):
            dr, di = carry
            r0 = pl.multiple_of(k * SUBLANES, SUBLANES)
            r1 = pl.multiple_of((k + 1) * SUBLANES, SUBLANES)
            xr, xi = x_ref[pl.ds(r0, SUBLANES), :SLAB_COLS], x_ref[pl.ds(r0, SUBLANES), SLAB_COLS:]
            lr, li = lam_ref[pl.ds(r1, SUBLANES), :SLAB_COLS], lam_ref[pl.ds(r1, SUBLANES), SLAB_COLS:]
            return dr + xr * lr + xi * li, di + xr * li - xi * lr

        dr, di = lax.fori_loop(0, n_steps - 1, step, (acc_ref[:, :SLAB_COLS], acc_ref[:, SLAB_COLS:]), unroll=4)
        first_chunk = j == nch - 1
        xr = jnp.where(first_chunk, cx_ref[:, :SLAB_COLS], xp_ref[:, :SLAB_COLS])
        xi = jnp.where(first_chunk, cx_ref[:, SLAB_COLS:], xp_ref[:, SLAB_COLS:])
        lr, li = lam_ref[0:SUBLANES, :SLAB_COLS], lam_ref[0:SUBLANES, SLAB_COLS:]
        acc_ref[:, :SLAB_COLS] = dr + xr * lr + xi * li
        acc_ref[:, SLAB_COLS:] = di + xr * li - xi * lr

        lam_b = _bf(lam_ref[...])
        dyv = dy_ref[...]
        db_ref[0] += _dot(_bf(h_ref[...]), lam_b, TN)
        dc_ref[0] += _dot(_bf(x_ref[...]), _bf(dyv), TN)
        du_ref[...] = _dot(lam_b, b_ref[0], NT) + d_ref[...] * dyv

        @pl.when(j == nch - 1)
        def _():
            da_ref[...] = jnp.broadcast_to(jnp.sum(acc_ref[...], axis=0, keepdims=True), (SUBLANES, w))

    sub = rc // SUBLANES
    tile = lambda wd: pl.BlockSpec((rc, wd), lambda sl, j: (nch - 1 - j, sl))
    small = pl.BlockSpec((SUBLANES, w), lambda sl, j: (sl, 0))
    prev = pl.BlockSpec((SUBLANES, w), lambda sl, j: (jnp.maximum((nch - 1 - j) * sub - 1, 0), sl))
    return pl.pallas_call(
        body, name="s5_adjoint_carry_grads", grid=(nsl, nch),
        in_specs=[tile(w), small, small, tile(w), prev, small, tile(LANES), tile(LANES),
                  pl.BlockSpec((1, LANES, w), lambda sl, j: (sl, 0, 0)), pl.BlockSpec((1, LANES), lambda sl, j: (0, sl))],
        out_specs=[tile(LANES), pl.BlockSpec((1, LANES, w), lambda sl, j: (sl, 0, 0)),
                   pl.BlockSpec((1, w, LANES), lambda sl, j: (sl, 0, 0)), small],
        out_shape=[jax.ShapeDtypeStruct((s, d), F32), jax.ShapeDtypeStruct((nsl, LANES, w), F32),
                   jax.ShapeDtypeStruct((nsl, w, LANES), F32), jax.ShapeDtypeStruct((nsl * SUBLANES, w), F32)],
        scratch_shapes=[pltpu.VMEM((SUBLANES, w), F32), pltpu.VMEM((rc, w), F32), pltpu.VMEM((SUBLANES, w), F32)],
        compiler_params=_params("parallel", "arbitrary"),
    )(lamloc, cinl, ab_tile, xtrue, xtrue, cinx, hp, dy, bblk, dvec)


def _adamw(name, w, g, m, v):
    r, c = w.shape
    tile = r if r * c <= 512 * 1024 else _pick(r, max(SUBLANES, (512 * 1024 // c) // SUBLANES * SUBLANES), q=SUBLANES)
    c1 = 1.0 / (1.0 - ADAM_B1 ** ADAM_STEP)
    c2 = 1.0 / (1.0 - ADAM_B2 ** ADAM_STEP)

    def body(w_ref, g_ref, m_ref, v_ref, d_ref, nm_ref, nv_ref):
        gg = g_ref[...]
        nm = ADAM_B1 * m_ref[...] + (1.0 - ADAM_B1) * gg
        nv = ADAM_B2 * v_ref[...] + (1.0 - ADAM_B2) * gg * gg
        d_ref[...] = -ADAM_LR * ((nm * c1) / (jnp.sqrt(nv * c2) + ADAM_EPS) + ADAM_WD * w_ref[...])
        nm_ref[...] = nm
        nv_ref[...] = nv

    spec = _nat(tile, c)
    return pl.pallas_call(
        body, name=name, grid=(r // tile,), in_specs=[spec] * 4, out_specs=[spec] * 3,
        out_shape=[jax.ShapeDtypeStruct((r, c), F32)] * 3, compiler_params=_params("parallel"),
    )(w, g, m, v)


def _place():
    x, y, c = lax.axis_index("x"), lax.axis_index("y"), lax.axis_index("c")
    return x, y, c, [(1 - x, y), (x, 1 - y), (1 - x, 1 - y)]


_ANY = pl.BlockSpec(memory_space=pl.ANY)


def _gather_weights(shards):
    n = len(shards)

    def body(*refs):
        ins, outs = refs[:n], refs[n:2 * n]
        send_sems, recv_sems, local_sems = refs[2 * n:]
        x, y, c, chips = _place()
        me = 2 * x + y
        sibling = (x, y, 1 - c)
        started = []
        for a in range(n):
            local = pltpu.make_async_copy(ins[a], outs[a].at[me], local_sems.at[a])
            local.start()
            started.append(local)

        def half(a, chip, h):
            hw = ins[a].shape[1] // 2
            return outs[a].at[chip, :, pl.ds(pl.multiple_of(h * hw, LANES), hw)]

        def copy(a, k, src, chip, h, to):
            return pltpu.make_async_remote_copy(
                src_ref=src, dst_ref=half(a, chip, h), send_sem=send_sems.at[a, k], recv_sem=recv_sems.at[a, k],
                device_id=to, device_id_type=MESH)

        sends = []
        for a in range(n):
            hw = ins[a].shape[1] // 2
            mine = ins[a].at[:, pl.ds(pl.multiple_of(c * hw, LANES), hw)]
            for k, chip in enumerate(chips):
                cp = copy(a, k, mine, me, c, (*chip, c))
                cp.start()
                sends.append(cp)
        for a in range(n):
            for k, (cx, cy) in enumerate(chips):
                src_chip = 2 * cx + cy
                copy(a, k, half(a, src_chip, c), src_chip, c, (x, y, c)).wait_recv()
                fwd = copy(a, 3 + k, half(a, src_chip, c), src_chip, c, sibling)
                fwd.start()
                sends.append(fwd)
        for a in range(n):
            for k, (cx, cy) in enumerate(chips):
                src_chip = 2 * cx + cy
                copy(a, 3 + k, half(a, src_chip, 1 - c), src_chip, 1 - c, (x, y, c)).wait_recv()
        for cp in sends:
            cp.wait_send()
        for cp in started:
            cp.wait()

    return pl.pallas_call(
        body, name="gather_weights",
        in_specs=[_ANY] * n, out_specs=[_ANY] * n,
        out_shape=[jax.ShapeDtypeStruct((N_CHIPS,) + s_.shape, s_.dtype) for s_ in shards],
        scratch_shapes=[pltpu.SemaphoreType.DMA((n, 6)), pltpu.SemaphoreType.DMA((n, 6)), pltpu.SemaphoreType.DMA((n,))],

    )(*shards)


def _swap_halves_to_sibling(name, grads):
    n = len(grads)

    def body(*refs):
        ins, outs = refs[:n], refs[n:2 * n]
        send_sems, recv_sems = refs[2 * n:]
        x, y, c, _ = _place()
        cps = []
        for a in range(n):
            hw = ins[a].shape[2] // 2
            src = ins[a].at[:, :, pl.ds(pl.multiple_of((1 - c) * hw, LANES), hw)]
            cp = pltpu.make_async_remote_copy(src_ref=src, dst_ref=outs[a], send_sem=send_sems.at[a],
                                              recv_sem=recv_sems.at[a], device_id=(x, y, 1 - c), device_id_type=MESH)
            cp.start()
            cps.append(cp)
        for cp in cps:
            cp.wait()

    return pl.pallas_call(
        body, name=name, in_specs=[_ANY] * n, out_specs=[_ANY] * n,
        out_shape=[jax.ShapeDtypeStruct(g.shape[:2] + (g.shape[2] // 2,), g.dtype) for g in grads],
        scratch_shapes=[pltpu.SemaphoreType.DMA((n,)), pltpu.SemaphoreType.DMA((n,))],

    )(*grads)


def _exchange_quarters(name, parts):
    n = len(parts)

    def body(*refs):
        ins, outs = refs[:n], refs[n:2 * n]
        send_sems, recv_sems = refs[2 * n:]
        x, y, c, chips = _place()
        cps = []
        for a in range(n):
            for k, (cx, cy) in enumerate(chips):
                cp = pltpu.make_async_remote_copy(
                    src_ref=ins[a].at[2 * cx + cy], dst_ref=outs[a].at[k], send_sem=send_sems.at[a, k],
                    recv_sem=recv_sems.at[a, k], device_id=(cx, cy, c), device_id_type=MESH)
                cp.start()
                cps.append(cp)
        for cp in cps:
            cp.wait()

    return pl.pallas_call(
        body, name=name, in_specs=[_ANY] * n, out_specs=[_ANY] * n,
        out_shape=[jax.ShapeDtypeStruct((3,) + p_.shape[1:], p_.dtype) for p_ in parts],
        scratch_shapes=[pltpu.SemaphoreType.DMA((n, 3)), pltpu.SemaphoreType.DMA((n, 3))],

    )(*parts)


def _swap_final_halves(name, halves):
    n = len(halves)

    def body(*refs):
        ins, outs = refs[:n], refs[n:2 * n]
        send_sems, recv_sems = refs[2 * n:]
        x, y, c, _ = _place()
        cps = []
        for a in range(n):
            cp = pltpu.make_async_remote_copy(src_ref=ins[a], dst_ref=outs[a], send_sem=send_sems.at[a],
                                              recv_sem=recv_sems.at[a], device_id=(x, y, 1 - c), device_id_type=MESH)
            cp.start()
            cps.append(cp)
        for cp in cps:
            cp.wait()

    return pl.pallas_call(
        body, name=name, in_specs=[_ANY] * n, out_specs=[_ANY] * n,
        out_shape=[jax.ShapeDtypeStruct(h.shape, h.dtype) for h in halves],
        scratch_shapes=[pltpu.SemaphoreType.DMA((n,)), pltpu.SemaphoreType.DMA((n,))],
    )(*halves)


def _add_half(name, grad, recv):
    nchip, r, cfull = grad.shape
    hw = cfull // 2
    tile = _pick(r, max(BF16_ROWS, (256 * 1024 // hw) // BF16_ROWS * BF16_ROWS), q=BF16_ROWS)
    c = lax.axis_index("c")

    def body(c_ref, g_ref, r_ref, o_ref):
        o_ref[...] = _bf(g_ref[...] + r_ref[...])

    return pl.pallas_call(
        body, name=name,
        grid_spec=pltpu.PrefetchScalarGridSpec(
            num_scalar_prefetch=1, grid=(nchip, r // tile),
            in_specs=[pl.BlockSpec((1, tile, hw), lambda k, i, cr: (k, i, cr[0])),
                      pl.BlockSpec((1, tile, hw), lambda k, i, cr: (k, i, 0))],
            out_specs=pl.BlockSpec((1, tile, hw), lambda k, i, cr: (k, i, 0))),
        out_shape=jax.ShapeDtypeStruct((nchip, r, hw), BF16), compiler_params=_params("parallel", "parallel"),
    )(c.reshape(1).astype(jnp.int32), grad, recv)


def _add_quarters(name, part, recv):
    _, r, hw = part.shape
    tile = _pick(r, max(BF16_ROWS, (256 * 1024 // hw) // BF16_ROWS * BF16_ROWS), q=BF16_ROWS)
    me = 2 * lax.axis_index("x") + lax.axis_index("y")

    def body(me_ref, p_ref, r_ref, o_ref):
        f = lambda v: v.astype(F32)
        o_ref[...] = ((f(p_ref[0]) + f(r_ref[0])) + f(r_ref[1])) + f(r_ref[2])

    return pl.pallas_call(
        body, name=name,
        grid_spec=pltpu.PrefetchScalarGridSpec(
            num_scalar_prefetch=1, grid=(r // tile,),
            in_specs=[pl.BlockSpec((1, tile, hw), lambda i, mr: (mr[0], i, 0)),
                      pl.BlockSpec((3, tile, hw), lambda i, mr: (0, i, 0))],
            out_specs=pl.BlockSpec((tile, hw), lambda i, mr: (i, 0))),
        out_shape=jax.ShapeDtypeStruct((r, hw), F32), compiler_params=_params("parallel"),
    )(me.reshape(1).astype(jnp.int32), part, recv)


def _reduce_scatter(grads):
    stacks = [g.reshape(N_CHIPS, g.shape[0] // N_CHIPS, g.shape[1]) for g in grads]
    recv = _swap_halves_to_sibling("rs_swap_halves", stacks)
    parts = [_add_half(f"rs_add_half_{a}", g, r) for a, (g, r) in enumerate(zip(stacks, recv))]
    quarters = _exchange_quarters("rs_exchange", parts)
    halves = [_add_quarters(f"rs_add_quarters_{a}", p_, q_) for a, (p_, q_) in enumerate(zip(parts, quarters))]
    others = _swap_final_halves("rs_swap_final", halves)
    south = lax.axis_index("c") == 0
    return [jnp.concatenate([jnp.where(south, h, o), jnp.where(south, o, h)], axis=1) for h, o in zip(halves, others)]


def _allgather_small(pack):
    m_per, n = pack.shape

    def body(x_ref, out_ref, send_sems, recv_sems, local_sem):
        x, y, c, chips = _place()
        me, sibling = (x, y, c), (x, y, 1 - c)

        def rows(px, py, pc):
            return out_ref.at[pl.ds(pl.multiple_of((4 * px + 2 * py + pc) * m_per, SUBLANES), m_per), :]

        def copy(k, block, to, src=None):
            return pltpu.make_async_remote_copy(
                src_ref=rows(*block) if src is None else src, dst_ref=rows(*block),
                send_sem=send_sems.at[k], recv_sem=recv_sems.at[k], device_id=to, device_id_type=MESH)

        mine = pltpu.make_async_copy(x_ref, rows(*me), local_sem)
        mine.start()
        first = [copy(0, me, sibling, src=x_ref)]
        first += [copy(1 + j, me, (*chip, c), src=x_ref) for j, chip in enumerate(chips)]
        for cp in first:
            cp.start()
        passed = [copy(4 + j, (*chip, c), sibling) for j, chip in enumerate(chips)]
        for j, chip in enumerate(chips):
            copy(1 + j, (*chip, c), me).wait_recv()
            passed[j].start()
        copy(0, sibling, me).wait_recv()
        for j, chip in enumerate(chips):
            copy(4 + j, (*chip, 1 - c), me).wait_recv()
        for cp in first + passed:
            cp.wait_send()
        mine.wait()

    return pl.pallas_call(
        body, name="allgather_small_grads",
        out_shape=jax.ShapeDtypeStruct((N_DEV * m_per, n), pack.dtype),
        in_specs=[pl.BlockSpec(memory_space=pltpu.VMEM)], out_specs=pl.BlockSpec(memory_space=pltpu.VMEM),
        scratch_shapes=[pltpu.SemaphoreType.DMA((7,)), pltpu.SemaphoreType.DMA((7,)), pltpu.SemaphoreType.DMA],
        compiler_params=pltpu.CompilerParams(vmem_limit_bytes=VMEM_LIMIT),
    )(pack)


def _sum_devices(packs, m_per):
    tile = _pick(m_per, 512, q=SUBLANES)
    nt = m_per // tile

    def body(*refs):
        acc = refs[0][...]
        for r in refs[1:N_DEV]:
            acc = acc + r[...]
        refs[N_DEV][...] = acc

    return pl.pallas_call(
        body, name="sum_small_grads", grid=(nt,),
        in_specs=[pl.BlockSpec((tile, LANES), functools.partial(lambda i, k: (k * nt + i, 0), k=k)) for k in range(N_DEV)],
        out_specs=_nat(tile, LANES), out_shape=jax.ShapeDtypeStruct((m_per, LANES), F32),
        compiler_params=_params("parallel"),
    )(*([packs] * N_DEV))


def _tail_fwd(tag, alpha, h_in, adds, mix_gate, ln1, ln2, p_l, w, want_perm):
    h_mid, xh1, rs1, h_mid_b, _ = _ln_fwd(f"ln1_fwd_{tag}", alpha, h_in, adds, mix_gate, *ln1)
    gp = _matmul(f"ple_gate_fwd_{tag}", h_mid_b, w['wg'], 'nn')
    pw = _matmul(f"ple_proj_fwd_{tag}", p_l, w['plet'], 'nt')
    gu = _matmul(f"ffn_in_fwd_{tag}", h_mid_b, w['wit'], 'nt', tn=1408)
    act = _swiglu_fwd(f"swiglu_fwd_{tag}", gu)
    ffn = _matmul(f"ffn_out_fwd_{tag}", act, w['wo'], 'nn', tk=2816)
    h_out, xh2, rs2, _, h_perm = _ln_fwd(f"ln2_fwd_{tag}", alpha, h_mid, [(ffn, 'nat')],
                                         ('nat', (pw, 1, 0), (gp, 1, 0)), *ln2, want_perm=want_perm)
    saved = dict(h_mid_b=h_mid_b, xh1=xh1, rs1=rs1, gp=gp, pw=pw, gu=gu, act=act, xh2=xh2, rs2=rs2)
    return h_out, h_perm, saved


def _tail_bwd(tag, alpha, dparts, sv, ln1_g, ln2_g, p_l, w, mix_gate):
    d = sv['h_mid_b'].shape[1]
    dz2, dz2b, dgate, dg2, db2 = _ln_bwd(f"ln2_bwd_{tag}", dparts, sv['xh2'], sv['rs2'], ln2_g,
                                         gate=('nat', (sv['pw'], 1, 0), (sv['gp'], 1, 0)))
    grads = dict(ln2_g=dg2, ln2_b=db2)
    grads['plet'] = _matmul(f"ple_proj_dw_{tag}", dgate, p_l, 'tn', a_win=(0, d))
    grads['wg'] = _matmul(f"ple_gate_dw_{tag}", sv['h_mid_b'], dgate, 'tn', b_win=(d, d))
    dx_gate = _matmul(f"ple_gate_dx_{tag}", dgate, w['wg'], 'nt', a_win=(d, d))
    dact = _matmul(f"ffn_out_dx_{tag}", dz2b, w['wo'], 'nt', out_dtype=BF16, tn=1408)
    grads['wo'] = _matmul(f"ffn_out_dw_{tag}", sv['act'], dz2b, 'tn', tm=1408)
    dgu = _swiglu_bwd(f"swiglu_bwd_{tag}", sv['gu'], dact)
    grads['wit'] = _matmul(f"ffn_in_dw_{tag}", dgu, sv['h_mid_b'], 'tn')
    dx_ffn = _matmul(f"ffn_in_dx_{tag}", dgu, w['wit'], 'nn', tk=2816)
    res = _ln_bwd(f"ln1_bwd_{tag}", [(dz2, 'nat', alpha), (dx_gate, 'nat', 1.0), (dx_ffn, 'nat', 1.0)],
                  sv['xh1'], sv['rs1'], ln1_g, gate=mix_gate)
    grads['ln1_g'], grads['ln1_b'] = res[-2], res[-1]
    return res[:-2], grads


def kernel(x, p, positions, attn_w_in, mla_q_norm, mla_w_q_b, mla_kv_norm, mla_w_kv_b, attn_w_out, s5_a_re, s5_a_im, s5_log_dt, s5_b_re, s5_b_im, s5_c_re, s5_c_im, s5_d, s5_w_glu, ln1_g, ln1_b, ffn_w_in, ffn_w_out, ple_w, ple_gate_w, ln2_g, ln2_b, loss_target, m_attn_w_in, m_mla_q_norm, m_mla_w_q_b, m_mla_kv_norm, m_mla_w_kv_b, m_attn_w_out, m_s5_a_re, m_s5_a_im, m_s5_log_dt, m_s5_b_re, m_s5_b_im, m_s5_c_re, m_s5_c_im, m_s5_d, m_s5_w_glu, m_ln1_g, m_ln1_b, m_ffn_w_in, m_ffn_w_out, m_ple_w, m_ple_gate_w, m_ln2_g, m_ln2_b, v_attn_w_in, v_mla_q_norm, v_mla_w_q_b, v_mla_kv_norm, v_mla_w_kv_b, v_attn_w_out, v_s5_a_re, v_s5_a_im, v_s5_log_dt, v_s5_b_re, v_s5_b_im, v_s5_c_re, v_s5_c_im, v_s5_d, v_s5_w_glu, v_ln1_g, v_ln1_b, v_ffn_w_in, v_ffn_w_out, v_ple_w, v_ple_gate_w, v_ln2_g, v_ln2_b):
    weights = dict(attn_w_in=attn_w_in, mla_q_norm=mla_q_norm, mla_w_q_b=mla_w_q_b, mla_kv_norm=mla_kv_norm,
                   mla_w_kv_b=mla_w_kv_b, attn_w_out=attn_w_out, s5_a_re=s5_a_re, s5_a_im=s5_a_im, s5_log_dt=s5_log_dt,
                   s5_b_re=s5_b_re, s5_b_im=s5_b_im, s5_c_re=s5_c_re, s5_c_im=s5_c_im, s5_d=s5_d, s5_w_glu=s5_w_glu,
                   ln1_g=ln1_g, ln1_b=ln1_b, ffn_w_in=ffn_w_in, ffn_w_out=ffn_w_out, ple_w=ple_w, ple_gate_w=ple_gate_w,
                   ln2_g=ln2_g, ln2_b=ln2_b)
    m_in = dict(attn_w_in=m_attn_w_in, mla_q_norm=m_mla_q_norm, mla_w_q_b=m_mla_w_q_b, mla_kv_norm=m_mla_kv_norm,
                mla_w_kv_b=m_mla_w_kv_b, attn_w_out=m_attn_w_out, s5_a_re=m_s5_a_re, s5_a_im=m_s5_a_im,
                s5_log_dt=m_s5_log_dt, s5_b_re=m_s5_b_re, s5_b_im=m_s5_b_im, s5_c_re=m_s5_c_re, s5_c_im=m_s5_c_im,
                s5_d=m_s5_d, s5_w_glu=m_s5_w_glu, ln1_g=m_ln1_g, ln1_b=m_ln1_b, ffn_w_in=m_ffn_w_in,
                ffn_w_out=m_ffn_w_out, ple_w=m_ple_w, ple_gate_w=m_ple_gate_w, ln2_g=m_ln2_g, ln2_b=m_ln2_b)
    v_in = dict(attn_w_in=v_attn_w_in, mla_q_norm=v_mla_q_norm, mla_w_q_b=v_mla_w_q_b, mla_kv_norm=v_mla_kv_norm,
                mla_w_kv_b=v_mla_w_kv_b, attn_w_out=v_attn_w_out, s5_a_re=v_s5_a_re, s5_a_im=v_s5_a_im,
                s5_log_dt=v_s5_log_dt, s5_b_re=v_s5_b_re, s5_b_im=v_s5_b_im, s5_c_re=v_s5_c_re, s5_c_im=v_s5_c_im,
                s5_d=v_s5_d, s5_w_glu=v_s5_w_glu, ln1_g=v_ln1_g, ln1_b=v_ln1_b, ffn_w_in=v_ffn_w_in,
                ffn_w_out=v_ffn_w_out, ple_w=v_ple_w, ple_gate_w=v_ple_gate_w, ln2_g=v_ln2_g, ln2_b=v_ln2_b)
    names = list(weights)

    s, d = x.shape[1], x.shape[2]
    depth = ln1_g.shape[0]
    assert depth == 2
    alpha = (2.0 * depth) ** 0.25
    ql, kvl = mla_q_norm.shape[1], mla_kv_norm.shape[1]
    in_cols = N_CHIPS * attn_w_in.shape[2]
    heads = N_CHIPS * mla_w_q_b.shape[2] // (NOPE + ROPE)
    hps = heads // N_CHIPS
    dw = (in_cols - ql - kvl - ROPE) // 3
    dh = dw // DHD
    assert ql % LANES == 0 and kvl == ql and dw % DHD == 0 and heads % N_CHIPS == 0
    ngroups, nstate = s5_a_re.shape[1], s5_a_re.shape[2]
    assert nstate == S5_STATE and ngroups * S5_GROUP == d and d % LANES == 0
    nsl = d // LANES
    seg_len = s // SUBLANES
    n_sq = seg_len.bit_length() - 1
    assert 1 << n_sq == seg_len, "the segment length of the S5 scan must be a power of two"
    for window, dil in DIL_BRANCHES:
        assert window // dil == DIL_STEPS and (s // dil) % DIL_STEPS == 0
    me = 2 * lax.axis_index("x") + lax.axis_index("y")

    xb = x[0]
    target = loss_target[0]
    p_layers = [p[0, 0], p[1, 0]]
    pos = positions[0].astype(F32).reshape(s, 1)
    inv_freq = ROPE_THETA ** (-jnp.arange(ROPE // 2, dtype=F32) / (ROPE // 2))
    invf = jnp.concatenate([inv_freq, inv_freq, jnp.zeros((LANES - ROPE,), F32)]).reshape(1, LANES)
    slopes = 2.0 ** (-8.0 * jnp.arange(1, dh + 1, dtype=F32) / dh)
    slopes = jnp.broadcast_to(jnp.repeat(slopes, SUBLANES)[:, None], (dh * SUBLANES, LANES))

    wqb_t = mla_w_q_b[0].T.reshape(hps, NOPE + ROPE, ql)
    wqb_t = jnp.pad(wqb_t, ((0, 0), (0, QK_PAD - NOPE - ROPE), (0, 0))).reshape(hps * QK_PAD, ql)
    d_cols = max(d // N_CHIPS, 2 * LANES)
    d_pad = jnp.zeros((SUBLANES, d_cols), F32).at[0, :d // N_CHIPS].set(s5_d[0])
    shards = [_bf(attn_w_in[0].T), _bf(wqb_t), _bf(mla_w_kv_b[0].T), _bf(attn_w_out[0]), _bf(s5_w_glu[0].T)]
    for l in range(depth):
        shards += [_bf(ffn_w_in[l].T), _bf(ffn_w_out[l]), _bf(ple_w[l].T), _bf(ple_gate_w[l])]
    shards.append(d_pad)
    full = [g.reshape(N_CHIPS * g.shape[1], g.shape[2]) for g in _gather_weights(shards)]
    win_t, wqb_t_f, wkv_t, wout, wglu_t = full[:5]
    lw = [dict(wit=full[5 + 4 * l], wo=full[6 + 4 * l], plet=full[7 + 4 * l], wg=full[8 + 4 * l]) for l in range(depth)]
    dvec = full[-1].reshape(N_CHIPS, SUBLANES, d_cols)[:, 0, :d // N_CHIPS].reshape(1, d)
    lat = ql + kvl
    win_t = jnp.concatenate([win_t[:lat + ROPE], jnp.zeros((LANES - ROPE, d), BF16), win_t[lat + ROPE:]], axis=0)
    kpe_cb = lat // LANES
    q_cb = kpe_cb + 1
    a_cb = heads * VDIM // LANES

    xbb = _bf(xb)
    proj = _matmul("attn_in_fwd", xbb, win_t, 'nt', tn=1408)
    nrm = _rms_fwd(proj, ql, kvl, mla_q_norm[0], mla_kv_norm[0])
    q_raw = _matmul("mla_q_up_fwd", nrm, wqb_t_f, 'nt', a_win=(0, ql))
    kv = _matmul("mla_kv_up_fwd", nrm, wkv_t, 'nt', a_win=(ql, kvl))
    qf, kf, vv = _rope_prep(q_raw, kv, proj, kpe_cb, pos, invf, heads)
    out_a, lse_a = _mla_fwd(qf, kf, vv.T, heads)
    outs, lses = [], []
    for window, dil in DIL_BRANCHES:
        o_g, l_g = _dil_fwd(f"dil_fwd_d{dil}", proj, slopes, dil, dh, q_cb)
        outs.append(o_g)
        lses.append(l_g)
    att, out_b, lse_b = _dil_merge(out_a, outs, lses)
    mix0 = _matmul("attn_out_fwd", att, wout, 'nn')
    h2, h2p, sv0 = _tail_fwd("l0", alpha, xb, [(mix0, 'nat')], None, (ln1_g[0], ln1_b[0]), (ln2_g[0], ln2_b[0]),
                             p_layers[0], lw[0], want_perm=True)

    rep = lambda a: jnp.repeat(a, S5_GROUP, axis=0)
    ag = (s5_a_re[0], s5_a_im[0], jnp.broadcast_to(s5_log_dt[0][:, None], (ngroups, nstate)))
    a16 = tuple(rep(a) for a in ag)
    b16 = tuple(b[0].transpose(0, 2, 1).reshape(ngroups * S5_GROUP, nstate) for b in (s5_b_re, s5_b_im))
    abr, abi, apr, api, bbr, bbi = _s5_discretise(*a16, *b16, n_sq)
    ab_tile = _slab_tile(abr[::S5_GROUP], abi[::S5_GROUP], nsl)
    ap_tile = _slab_tile(apr[::S5_GROUP], api[::S5_GROUP], nsl)
    bblk = _bf(_slab_in_matrix(bbr.reshape(ngroups, S5_GROUP, nstate), bbi.reshape(ngroups, S5_GROUP, nstate), nsl))
    cblk = _bf(_slab_out_matrix(s5_c_re[0], s5_c_im[0], nsl))
    xloc, ends = _s5_pass1(h2p, bblk, ab_tile)
    cinx = _s5_carry("s5_carry_fwd", ends, ap_tile, False)
    xtrue, ypre, zg = _s5_pass2(xloc, cinx, ab_tile, cblk, h2p, dvec)
    vg = _matmul("s5_glu_fwd", zg, wglu_t, 'nt')
    glu_gate = ('perm', (vg, 2, 0), (vg, 2, 1))
    h4, _, sv1 = _tail_fwd("l1", alpha, h2, [], glu_gate, (ln1_g[1], ln1_b[1]), (ln2_g[1], ln2_b[1]),
                           p_layers[1], lw[1], want_perm=False)
    loss = lax.psum(jnp.sum(_loss_partial(h4, target)), ("x", "y", "c"))

    (dz1_1, _, dvg), g1 = _tail_bwd("l1", alpha, [(h4, 'nat', 1.0 / d), (target, 'nat', -1.0 / d)], sv1, ln1_g[1],
                                    ln2_g[1], p_layers[1], lw[1], glu_gate)
    d_wglu_t = _matmul("s5_glu_dw", dvg, zg, 'tn')
    dzg = _matmul("s5_glu_dx", dvg, wglu_t, 'nn')
    lamloc, starts, dy, dd = _s5_bwd_pass1(dzg, ypre, cblk, ab_tile, h2p)
    cinl = _s5_carry("s5_carry_bwd", starts, ap_tile, True)
    du_p, d_bblk, d_cblk, d_ab = _s5_bwd_pass2(lamloc, cinl, ab_tile, xtrue, cinx, h2p, dy, bblk, dvec)
    gbb = _slab_in_extract(d_bblk, nsl)
    g_c_re, g_c_im = _slab_out_extract(d_cblk, nsl)
    d_ab = d_ab[::SUBLANES]
    gab = (d_ab[:, :SLAB_COLS].reshape(ngroups, nstate), d_ab[:, SLAB_COLS:].reshape(ngroups, nstate))
    g_a_re, g_a_im, g_log_dt, g_b_re, g_b_im = _s5_discretise_bwd(a16, b16, ag, gab, gbb)
    unt = lambda b: b.reshape(ngroups, S5_GROUP, nstate).transpose(0, 2, 1)

    (dz1_0, dz1_0b), g0 = _tail_bwd("l0", alpha, [(dz1_1, 'nat', alpha), (du_p, 'perm', 1.0)], sv0, ln1_g[0], ln2_g[0],
                                    p_layers[0], lw[0], None)
    datt = _matmul("attn_out_dx", dz1_0b, wout, 'nt')
    d_wout = _matmul("attn_out_dw", att, dz1_0b, 'tn')
    do, delta, delta_t = _attn_bwd_prep(datt, out_a, out_b)
    dqf, dkf, dvv = _mla_bwd(qf, kf, vv, do, lse_a, delta_t, heads, 0)
    dq_raw, dkv, dkpe = _rope_unprep(dqf, dkf, dvv, pos, invf, heads)
    d_wqb_t = _matmul("mla_q_up_dw", dq_raw, nrm, 'tn', b_win=(0, ql))
    d_wkv_t = _matmul("mla_kv_up_dw", dkv, nrm, 'tn', b_win=(ql, kvl))
    dnq = _matmul("mla_q_up_dx", dq_raw, wqb_t_f, 'nn')
    dnkv = _matmul("mla_kv_up_dx", dkv, wkv_t, 'nn')
    dqs, dks, dvs = [], [], []
    for window, dil in DIL_BRANCHES:
        dqs.append(_dil_bwd_dq(f"dil_bwd_dq_d{dil}", proj, slopes, do, lse_b, delta, dil, dh, q_cb, a_cb))
        dk_g, dv_g = _dil_bwd_dkv(f"dil_bwd_dkv_d{dil}", proj, slopes, do, lse_b, delta, dil, dh, q_cb, a_cb)
        dks.append(dk_g)
        dvs.append(dv_g)
    dproj, g_gq, g_gkv = _dproj_assemble(proj, dnq, dnkv, dkpe, dqs, dks, dvs, mla_q_norm[0], mla_kv_norm[0], ql)
    d_win_t = _matmul("attn_in_dw", dproj, xbb, 'tn', tm=1408)
    dx_attn = _matmul("attn_in_dx", dproj, win_t, 'nn')
    grad_x = _axpy("grad_x", alpha, dz1_0, dx_attn)

    d_win_t = jnp.concatenate([d_win_t[:lat + ROPE], d_win_t[lat + LANES:]], axis=0)
    big = [d_win_t, d_wqb_t, d_wkv_t, d_wout, d_wglu_t]
    for gl in (g0, g1):
        big += [gl['wit'], gl['wo'], gl['plet'], gl['wg']]
    red = _reduce_scatter(big)
    r_wqb = red[1].reshape(hps, QK_PAD, ql)[:, :NOPE + ROPE].reshape(hps * (NOPE + ROPE), ql)
    grads = dict(attn_w_in=red[0].T[None], mla_w_q_b=r_wqb.T[None], mla_w_kv_b=red[2].T[None], attn_w_out=red[3][None],
                 s5_w_glu=red[4].T[None],
                 ffn_w_in=jnp.stack([red[5].T, red[9].T]), ffn_w_out=jnp.stack([red[6], red[10]]),
                 ple_w=jnp.stack([red[7].T, red[11].T]), ple_gate_w=jnp.stack([red[8], red[12]]))

    small = dict(mla_q_norm=g_gq, mla_kv_norm=g_gkv, s5_a_re=g_a_re, s5_a_im=g_a_im, s5_log_dt=g_log_dt,
                 s5_b_re=unt(g_b_re), s5_b_im=unt(g_b_im), s5_c_re=g_c_re, s5_c_im=g_c_im, s5_d=dd[0],
                 ln1_g=jnp.stack([g0['ln1_g'], g1['ln1_g']]), ln1_b=jnp.stack([g0['ln1_b'], g1['ln1_b']]),
                 ln2_g=jnp.stack([g0['ln2_g'], g1['ln2_g']]), ln2_b=jnp.stack([g0['ln2_b'], g1['ln2_b']]))
    flat = jnp.concatenate([v_.reshape(-1) for v_ in small.values()])
    m_per = -(-flat.shape[0] // (LANES * SUBLANES)) * SUBLANES
    pack = jnp.pad(flat, (0, m_per * LANES - flat.shape[0])).reshape(m_per, LANES)
    total = _sum_devices(_allgather_small(pack), m_per).reshape(-1)
    off = 0
    for k_, v_ in small.items():
        n_ = v_.size
        piece = total[off:off + n_]
        off += n_
        if k_ == 's5_d':
            grads[k_] = lax.dynamic_slice(piece, (me * (d // N_CHIPS),), (d // N_CHIPS,)).reshape(weights[k_].shape)
        else:
            grads[k_] = piece.reshape(weights[k_].shape)

    deltas, new_m, new_v = {}, {}, {}
    for k_ in names:
        w_ = weights[k_]
        shape = w_.shape
        if w_.ndim == 3 and w_.shape[-1] >= LANES:
            two_d = (shape[0] * shape[1], shape[2])
        elif w_.ndim == 4:
            two_d = (shape[0] * shape[1], shape[2] * shape[3])
        else:
            two_d = (1, w_.size) if w_.ndim == 2 and shape[0] == 1 else (shape[0], w_.size // shape[0])
        dl, nm, nv = _adamw(f"adamw_{k_}", w_.reshape(two_d), grads[k_].reshape(two_d), m_in[k_].reshape(two_d),
                            v_in[k_].reshape(two_d))
        deltas[k_], new_m[k_], new_v[k_] = dl.reshape(shape), nm.reshape(shape), nv.reshape(shape)

    return (loss, grad_x[None], *[grads[k_] for k_ in names], *[deltas[k_] for k_ in names],
            *[new_m[k_] for k_ in names], *[new_v[k_] for k_ in names])
```

```python
import functools
import math

import jax
import jax.numpy as jnp
from jax import lax
from jax.experimental import pallas as pl
from jax.experimental.pallas import tpu as pltpu
from jax.experimental.pallas import tpu_sc as plsc

F32 = jnp.float32
BF16 = jnp.bfloat16
MESH = pl.DeviceIdType.MESH

LANES = 128
SUBLANES = 8
BF16_ROWS = 16
VMEM_LIMIT = 48 * 2 ** 20
N_CHIPS = 4
N_DEV = 8

NOPE = 128
ROPE = 64
VDIM = 128
QK_PAD = 256
DHD = 128
DIL_STEPS = 128
DIL_BRANCHES = ((128, 1), (512, 4), (2048, 16))
ROPE_THETA = 10000.0
S5_GROUP = 16
S5_STATE = 64
SLAB_GROUPS = LANES // S5_GROUP
SLAB_COLS = SLAB_GROUPS * S5_STATE
NEG = -1e30
LN_EPS = 1e-5
RMS_EPS = 1e-6

ADAM_LR = 0.001
ADAM_B1 = 0.9
ADAM_B2 = 0.999
ADAM_EPS = 1e-08
ADAM_WD = 0.01
ADAM_STEP = 10

NN = ((1,), (0,))
NT = ((1,), (1,))
TN = ((0,), (0,))


def _dot(a, b, dims):
    return lax.dot_general(a, b, (dims, ((), ())), preferred_element_type=F32)


def _bf(v):
    return v.astype(BF16)


def _pick(n, target, q=LANES, also=0):
    g = math.gcd(n, also) if also else n
    if g <= target and g == n:
        return n
    best = None
    for t in range(q, min(g, target) + 1, q):
        if g % t == 0:
            best = t
    assert best is not None, (n, target, q, also)
    return best


def _params(*sem):
    return pltpu.CompilerParams(dimension_semantics=sem, vmem_limit_bytes=VMEM_LIMIT)


def _sigmoid(v):
    return 1.0 / (1.0 + jnp.exp(-v))


def _matmul(name, a, b, form, out_dtype=F32, a_win=None, b_win=None, tm=1024, tn=1024, tk=2048):
    c0, aw = a_win if a_win else (0, a.shape[1])
    if form == 'nt':
        assert b_win is None
        n, kdim = b.shape
        d0 = 0
    else:
        kdim = b.shape[0]
        d0, n = b_win if b_win else (0, b.shape[1])
    if form == 'tn':
        m = aw
        assert a.shape[0] == kdim, (name, a.shape, b.shape)
        tm = _pick(m, tm, also=c0)
        tk = _pick(kdim, tk)
        a_off = c0 // tm
    else:
        m = a.shape[0]
        assert aw == kdim, (name, a.shape, b.shape, a_win)
        tm = _pick(m, tm)
        tk = _pick(kdim, tk, also=c0)
        a_off = c0 // tk
    tn = _pick(n, tn, also=d0)
    b_off = d0 // tn
    nk = kdim // tk
    dims = {'nn': NN, 'nt': NT, 'tn': TN}[form]

    def body(a_ref, b_ref, o_ref, *acc):
        prod = _dot(_bf(a_ref[...]), _bf(b_ref[...]), dims)
        if nk == 1:
            o_ref[...] = prod.astype(o_ref.dtype)
            return
        acc_ref, = acc
        k = pl.program_id(2)

        @pl.when(k == 0)
        def _():
            acc_ref[...] = prod

        @pl.when((k > 0) & (k < nk - 1))
        def _():
            acc_ref[...] += prod

        @pl.when(k == nk - 1)
        def _():
            o_ref[...] = (acc_ref[...] + prod).astype(o_ref.dtype)

    if form == 'tn':
        a_spec = pl.BlockSpec((tk, tm), lambda i, j, k: (k, i + a_off))
    else:
        a_spec = pl.BlockSpec((tm, tk), lambda i, j, k: (i, k + a_off))
    if form == 'nt':
        b_spec = pl.BlockSpec((tn, tk), lambda i, j, k: (j, k))
    else:
        b_spec = pl.BlockSpec((tk, tn), lambda i, j, k: (k, j + b_off))
    return pl.pallas_call(
        body, name=name,
        grid=(m // tm, n // tn, nk),
        in_specs=[a_spec, b_spec],
        out_specs=pl.BlockSpec((tm, tn), lambda i, j, k: (i, j)),
        out_shape=jax.ShapeDtypeStruct((m, n), out_dtype),
        scratch_shapes=[pltpu.VMEM((tm, tn), F32)] if nk > 1 else [],
        compiler_params=_params("parallel", "parallel", "arbitrary"),
    )(a, b)


def _nat(tile, width, cb=0):
    return pl.BlockSpec((tile, width), lambda i: (i, cb))


def _perm(tile, width, seg_tiles, ncb=1, cb=0):
    return pl.BlockSpec((tile, width), lambda i: (i % seg_tiles, (i // seg_tiles) * ncb + cb))


def _whole(shape):
    return pl.BlockSpec(shape, lambda i: (0,) * len(shape))


def _perm_view(a):
    s, w = a.shape
    return a.reshape(s // SUBLANES, SUBLANES * w)


def _row_spec(a, layout, tile, width, ncb=1, cb=0):
    if layout == 'nat':
        return a, _nat(tile, width, cb)
    seg_tiles = a.shape[0] // SUBLANES // tile
    return _perm_view(a), _perm(tile, width, seg_tiles, ncb, cb)


def _ln_fwd(name, alpha, a, adds, gate, g, b, want_perm=False, tile=256):
    s, d = a.shape
    n_add = len(adds)
    has_gate = gate is not None

    def body(*refs):
        a_ref = refs[0]
        add_refs = refs[1:1 + n_add]
        pos = 1 + n_add
        if has_gate:
            val_ref, pre_ref = refs[pos], refs[pos + 1]
            pos += 2
        g_ref, b_ref = refs[pos], refs[pos + 1]
        outs = refs[pos + 2:]
        z = alpha * a_ref[...]
        for r in add_refs:
            z = z + r[...]
        if has_gate:
            z = z + val_ref[...] * _sigmoid(pre_ref[...])
        mu = jnp.mean(z, axis=-1, keepdims=True)
        zc = z - mu
        var = jnp.mean(zc * zc, axis=-1, keepdims=True)
        rstd = lax.rsqrt(var + LN_EPS)
        xhat = zc * rstd
        h = xhat * g_ref[...] + b_ref[...]
        outs[0][...] = h
        outs[1][...] = xhat
        outs[2][...] = jnp.broadcast_to(rstd, (tile, LANES))
        outs[3][...] = _bf(h)
        if want_perm:
            outs[4][...] = h

    ins, specs = [a], [_nat(tile, d)]
    for arr, layout in adds:
        x_, sp = _row_spec(arr, layout, tile, d)
        ins.append(x_)
        specs.append(sp)
    if has_gate:
        layout = gate[0]
        for arr, ncb, cb in gate[1:]:
            x_, sp = _row_spec(arr, layout, tile, d, ncb=ncb, cb=cb)
            ins.append(x_)
            specs.append(sp)
    ins += [g.reshape(1, d), b.reshape(1, d)]
    specs += [_whole((1, d)), _whole((1, d))]
    out_shape = [jax.ShapeDtypeStruct((s, d), F32), jax.ShapeDtypeStruct((s, d), F32),
                 jax.ShapeDtypeStruct((s, LANES), F32), jax.ShapeDtypeStruct((s, d), BF16)]
    out_specs = [_nat(tile, d), _nat(tile, d), _nat(tile, LANES), _nat(tile, d)]
    if want_perm:
        seg_tiles = s // SUBLANES // tile
        out_shape.append(jax.ShapeDtypeStruct((s // SUBLANES, SUBLANES * d), F32))
        out_specs.append(_perm(tile, d, seg_tiles))
    res = pl.pallas_call(
        body, name=name, grid=(s // tile,), in_specs=specs, out_specs=out_specs, out_shape=out_shape,
        compiler_params=_params("parallel"),
    )(*ins)
    return res[0], res[1], res[2], res[3], (res[4].reshape(s, d) if want_perm else None)


def _ln_bwd(name, dparts, xhat, rstd, g, gate=None, tile=256):
    s, d = xhat.shape
    n_part = len(dparts)
    coefs = [c for _, _, c in dparts]
    has_gate = gate is not None

    def body(*refs):
        part_refs = refs[:n_part]
        xhat_ref, rstd_ref, g_ref = refs[n_part:n_part + 3]
        pos = n_part + 3
        if has_gate:
            val_ref, pre_ref = refs[pos], refs[pos + 1]
            pos += 2
        outs = list(refs[pos:])
        dz_ref = outs.pop(0)
        dzb_ref = outs.pop(0)
        dgate_ref = outs.pop(0) if has_gate else None
        dg_ref, db_ref = outs
        dh = coefs[0] * part_refs[0][...]
        for c, r in zip(coefs[1:], part_refs[1:]):
            dh = dh + c * r[...]
        xh = xhat_ref[...]
        dxh = dh * g_ref[...]
        m1 = jnp.mean(dxh, axis=-1, keepdims=True)
        m2 = jnp.mean(dxh * xh, axis=-1, keepdims=True)
        dz = rstd_ref[:, 0:1] * (dxh - m1 - xh * m2)
        dz_ref[...] = dz
        dzb_ref[...] = _bf(dz)
        if has_gate:
            sg = _sigmoid(pre_ref[...])
            dval = dz * sg
            dpre = dz * val_ref[...] * sg * (1.0 - sg)
            dgate_ref[...] = jnp.concatenate([_bf(dval), _bf(dpre)], axis=1)

        @pl.when(pl.program_id(0) == 0)
        def _():
            dg_ref[...] = jnp.zeros_like(dg_ref)
            db_ref[...] = jnp.zeros_like(db_ref)

        dg_ref[0:1, :] += jnp.sum(dh * xh, axis=0, keepdims=True)
        db_ref[0:1, :] += jnp.sum(dh, axis=0, keepdims=True)

    ins, specs = [], []
    for arr, layout, _ in dparts:
        x_, sp = _row_spec(arr, layout, tile, d)
        ins.append(x_)
        specs.append(sp)
    ins += [xhat, rstd, g.reshape(1, d)]
    specs += [_nat(tile, d), _nat(tile, LANES), _whole((1, d))]
    gate_layout = None
    if has_gate:
        gate_layout = gate[0]
        for arr, ncb, cb in gate[1:]:
            x_, sp = _row_spec(arr, gate_layout, tile, d, ncb=ncb, cb=cb)
            ins.append(x_)
            specs.append(sp)
    seg_tiles = s // SUBLANES // tile
    out_shape = [jax.ShapeDtypeStruct((s, d), F32), jax.ShapeDtypeStruct((s, d), BF16)]
    out_specs = [_nat(tile, d), _nat(tile, d)]
    if has_gate:
        if gate_layout == 'nat':
            out_shape.append(jax.ShapeDtypeStruct((s, 2 * d), BF16))
            out_specs.append(_nat(tile, 2 * d))
        else:
            out_shape.append(jax.ShapeDtypeStruct((s // SUBLANES, SUBLANES * 2 * d), BF16))
            out_specs.append(_perm(tile, 2 * d, seg_tiles))
    out_shape += [jax.ShapeDtypeStruct((SUBLANES, d), F32)] * 2
    out_specs += [_whole((SUBLANES, d))] * 2
    res = list(pl.pallas_call(
        body, name=name, grid=(s // tile,), in_specs=specs, out_specs=out_specs, out_shape=out_shape,
        compiler_params=_params("arbitrary"),
    )(*ins))
    out = [res.pop(0), res.pop(0)]
    if has_gate:
        out.append(res.pop(0).reshape(s, 2 * d))
    out += [res[0][0], res[1][0]]
    return out


def _loss_partial(h, target, tile=256):
    s, d = h.shape

    def body(h_ref, t_ref, o_ref):
        @pl.when(pl.program_id(0) == 0)
        def _():
            o_ref[...] = jnp.zeros_like(o_ref)

        e = h_ref[...] - t_ref[...]
        sq = e * e
        part = sq[:, 0:LANES]
        for k in range(1, d // LANES):
            part = part + sq[:, k * LANES:(k + 1) * LANES]
        o_ref[0:1, :] += jnp.sum(part, axis=0, keepdims=True) * (0.5 / d)

    return pl.pallas_call(
        body, name="loss_partial", grid=(s // tile,), in_specs=[_nat(tile, d), _nat(tile, d)],
        out_specs=_whole((SUBLANES, LANES)), out_shape=jax.ShapeDtypeStruct((SUBLANES, LANES), F32),
        compiler_params=_params("arbitrary"),
    )(h, target)


def _swiglu_fwd(name, gu, tile=256):
    s, f2 = gu.shape
    f = f2 // 2
    cw = _pick(f, 1408)
    ncb = f // cw

    def body(g_ref, u_ref, o_ref):
        gg = g_ref[...]
        o_ref[...] = _bf(gg * _sigmoid(gg) * u_ref[...])

    return pl.pallas_call(
        body, name=name, grid=(s // tile, ncb),
        in_specs=[pl.BlockSpec((tile, cw), lambda i, j: (i, j)), pl.BlockSpec((tile, cw), lambda i, j: (i, j + ncb))],
        out_specs=pl.BlockSpec((tile, cw), lambda i, j: (i, j)),
        out_shape=jax.ShapeDtypeStruct((s, f), BF16), compiler_params=_params("parallel", "parallel"),
    )(gu, gu)


def _swiglu_bwd(name, gu, dact, tile=128):
    s, f2 = gu.shape
    f = f2 // 2

    def body(g_ref, u_ref, da_ref, o_ref):
        gg = g_ref[...]
        sg = _sigmoid(gg)
        da = da_ref[...].astype(F32)
        silu = gg * sg
        o_ref[:, :f] = _bf(da * u_ref[...] * (sg + silu * (1.0 - sg)))
        o_ref[:, f:] = _bf(da * silu)

    return pl.pallas_call(
        body, name=name, grid=(s // tile,),
        in_specs=[_nat(tile, f, 0), _nat(tile, f, 1), _nat(tile, f)], out_specs=_nat(tile, f2),
        out_shape=jax.ShapeDtypeStruct((s, f2), BF16), compiler_params=_params("parallel"),
    )(gu, gu, dact)


def _rms_fwd(proj, ql, kvl, gq, gkv, tile=256):
    s = proj.shape[0]
    assert ql == kvl

    def body(q_ref, kv_ref, gq_ref, gkv_ref, o_ref):
        def nrm(x, gg):
            return x * lax.rsqrt(jnp.mean(x * x, axis=-1, keepdims=True) + RMS_EPS) * gg

        o_ref[...] = jnp.concatenate([_bf(nrm(q_ref[...], gq_ref[...])), _bf(nrm(kv_ref[...], gkv_ref[...]))], axis=1)

    return pl.pallas_call(
        body, name="mla_rms_fwd", grid=(s // tile,),
        in_specs=[_nat(tile, ql, 0), _nat(tile, kvl, 1), _whole((1, ql)), _whole((1, kvl))],
        out_specs=_nat(tile, ql + kvl), out_shape=jax.ShapeDtypeStruct((s, ql + kvl), BF16),
        compiler_params=_params("parallel"),
    )(proj, proj, gq.reshape(1, ql), gkv.reshape(1, kvl))


def _rope_coeffs(pos, invf):
    ang = pos * invf
    cs, sn = jnp.cos(ang), jnp.sin(ang)
    lane = lax.broadcasted_iota(jnp.int32, ang.shape, 1)
    half = ROPE // 2
    c = jnp.where(lane < ROPE, cs, 0.0)
    sa = jnp.where(lane < half, -sn, 0.0)
    sb = jnp.where((lane >= half) & (lane < ROPE), sn, 0.0)
    return c, sa, sb


def _rope_prep(q_raw, kv, proj, kpe_cb, pos, invf, heads, tile=256):
    s = q_raw.shape[0]
    half = ROPE // 2

    def body(q_ref, kv_ref, kpe_ref, pos_ref, invf_ref, qf_ref, kf_ref, v_ref):
        c, sa, sb = _rope_coeffs(pos_ref[...], invf_ref[...])

        def rope(t):
            return t * c + pltpu.roll(t, LANES - half, 1) * sa + pltpu.roll(t, half, 1) * sb

        kr = _bf(rope(kpe_ref[...]))
        for hh in range(heads):
            o = hh * QK_PAD
            qf_ref[:, o:o + NOPE] = _bf(q_ref[:, o:o + NOPE])
            qf_ref[:, o + NOPE:o + QK_PAD] = _bf(rope(q_ref[:, o + NOPE:o + QK_PAD]))
            kf_ref[:, o:o + NOPE] = _bf(kv_ref[:, o:o + NOPE])
            kf_ref[:, o + NOPE:o + QK_PAD] = kr
            v_ref[:, hh * VDIM:(hh + 1) * VDIM] = _bf(kv_ref[:, o + NOPE:o + QK_PAD])

    w = heads * QK_PAD
    return pl.pallas_call(
        body, name="mla_rope_prep", grid=(s // tile,),
        in_specs=[_nat(tile, w), _nat(tile, w), _nat(tile, LANES, kpe_cb), _nat(tile, 1), _whole((1, LANES))],
        out_specs=[_nat(tile, w), _nat(tile, w), _nat(tile, heads * VDIM)],
        out_shape=[jax.ShapeDtypeStruct((s, w), BF16), jax.ShapeDtypeStruct((s, w), BF16),
                   jax.ShapeDtypeStruct((s, heads * VDIM), BF16)],
        compiler_params=_params("parallel"),
    )(q_raw, kv, proj, pos, invf)


def _rope_unprep(dqf, dkf, dv, pos, invf, heads, tile=256):
    s = dqf.shape[0]
    half = ROPE // 2

    def body(dq_ref, dk_ref, dv_ref, pos_ref, invf_ref, dqr_ref, dkv_ref, dkpe_ref):
        c, sa, sb = _rope_coeffs(pos_ref[...], invf_ref[...])

        def unrope(gt):
            return gt * c + pltpu.roll(gt * sa, half, 1) + pltpu.roll(gt * sb, LANES - half, 1)

        dkpe = jnp.zeros((tile, LANES), F32)
        for hh in range(heads):
            o = hh * QK_PAD
            dqr_ref[:, o:o + NOPE] = _bf(dq_ref[:, o:o + NOPE])
            dqr_ref[:, o + NOPE:o + QK_PAD] = _bf(unrope(dq_ref[:, o + NOPE:o + QK_PAD]))
            dkv_ref[:, o:o + NOPE] = _bf(dk_ref[:, o:o + NOPE])
            dkv_ref[:, o + NOPE:o + QK_PAD] = _bf(dv_ref[:, hh * VDIM:(hh + 1) * VDIM])
            dkpe = dkpe + dk_ref[:, o + NOPE:o + QK_PAD]
        dkpe_ref[...] = unrope(dkpe)

    w = heads * QK_PAD
    return pl.pallas_call(
        body, name="mla_rope_unprep", grid=(s // tile,),
        in_specs=[_nat(tile, w), _nat(tile, w), _nat(tile, heads * VDIM), _nat(tile, 1), _whole((1, LANES))],
        out_specs=[_nat(tile, w), _nat(tile, w), _nat(tile, LANES)],
        out_shape=[jax.ShapeDtypeStruct((s, w), BF16), jax.ShapeDtypeStruct((s, w), BF16),
                   jax.ShapeDtypeStruct((s, LANES), F32)],
        compiler_params=_params("parallel"),
    )(dqf, dkf, dv, pos, invf)


LOG2E = 1.4426950408889634
MLA_SCALE = (NOPE + ROPE) ** -0.5


def _mla_scores_t(k, q, t, masked):
    sc = _dot(k, q, NT) * (MLA_SCALE * LOG2E)
    if masked:
        row = lax.broadcasted_iota(jnp.int32, (t, t), 0)
        col = lax.broadcasted_iota(jnp.int32, (t, t), 1)
        sc = jnp.where(row <= col, sc, NEG)
    return sc


def _mla_fwd(qf, kf, vt, heads, t=512):
    s = qf.shape[0]
    t = min(t, s)
    nq = s // t

    def body(q_ref, k_ref, vt_ref, o_ref, lse_ref, m_ref, l_ref, acc_ref):
        i = pl.program_id(1)
        m_ref[...] = jnp.full_like(m_ref, NEG)
        l_ref[...] = jnp.zeros_like(l_ref)
        acc_ref[...] = jnp.zeros_like(acc_ref)
        q = q_ref[...]

        def block(j, masked):
            r0 = pl.multiple_of(j * t, t)
            sc = _mla_scores_t(k_ref[pl.ds(r0, t), :], q, t, masked)
            m_prev = m_ref[0:1, :]
            m_new = jnp.maximum(m_prev, jnp.max(sc, axis=0, keepdims=True))
            corr = jnp.exp2(m_prev - m_new)
            p = jnp.exp2(sc - m_new)
            l_new = corr * l_ref[0:1, :] + jnp.sum(p, axis=0, keepdims=True)
            acc_ref[...] = corr * acc_ref[...] + _dot(vt_ref[:, pl.ds(r0, t)], _bf(p), NN)
            m_ref[...] = jnp.broadcast_to(m_new, (SUBLANES, t))
            l_ref[...] = jnp.broadcast_to(l_new, (SUBLANES, t))

        def unmasked(j, carry):
            block(j, False)
            return carry

        lax.fori_loop(0, i, unmasked, 0)
        block(i, True)
        o_ref[...] = (acc_ref[...] / l_ref[0:1, :]).T
        lse_ref[...] = m_ref[...] + jnp.log(l_ref[...]) * LOG2E

    return pl.pallas_call(
        body, name="mla_flash_fwd", grid=(heads, nq),
        in_specs=[pl.BlockSpec((t, QK_PAD), lambda h, i: (i, h)), pl.BlockSpec((s, QK_PAD), lambda h, i: (0, h)),
                  pl.BlockSpec((VDIM, s), lambda h, i: (h, 0))],
        out_specs=[pl.BlockSpec((t, VDIM), lambda h, i: (i, h)), pl.BlockSpec((SUBLANES, t), lambda h, i: (h, i))],
        out_shape=[jax.ShapeDtypeStruct((s, heads * VDIM), F32), jax.ShapeDtypeStruct((heads * SUBLANES, s), F32)],
        scratch_shapes=[pltpu.VMEM((SUBLANES, t), F32), pltpu.VMEM((SUBLANES, t), F32), pltpu.VMEM((VDIM, t), F32)],
        compiler_params=_params("parallel", "arbitrary"),
    )(qf, kf, vt)


def _mla_bwd(qf, kf, v, do, lse_t, delta_t, heads, do_cb0, t=512):
    s = qf.shape[0]
    t = min(t, s)
    nq = s // t

    def body(q_ref, k_ref, v_ref, do_ref, lse_ref, dl_ref, dq_ref, dk_ref, dv_ref, acc_ref):
        i = pl.program_id(1)

        @pl.when(i == 0)
        def _():
            dk_ref[...] = jnp.zeros_like(dk_ref)
            dv_ref[...] = jnp.zeros_like(dv_ref)

        acc_ref[...] = jnp.zeros_like(acc_ref)
        q, dob = q_ref[...], do_ref[...]
        lse, dl = lse_ref[0:1, :], dl_ref[0:1, :]

        def block(j, masked):
            r0 = pl.multiple_of(j * t, t)
            k = k_ref[pl.ds(r0, t), :]
            p = jnp.exp2(_mla_scores_t(k, q, t, masked) - lse)
            dp = _dot(v_ref[pl.ds(r0, t), :], dob, NT)
            ds = _bf(p * (dp - dl) * MLA_SCALE)
            acc_ref[...] += _dot(ds, k, TN)
            dk_ref[pl.ds(r0, t), :] += _dot(ds, q, NN)
            dv_ref[pl.ds(r0, t), :] += _dot(_bf(p), dob, NN)

        def unmasked(j, carry):
            block(j, False)
            return carry

        lax.fori_loop(0, i, unmasked, 0)
        block(i, True)
        dq_ref[...] = acc_ref[...]

    qs = lambda w, off=0: pl.BlockSpec((t, w), lambda h, i: (i, h + off))
    ks = lambda w: pl.BlockSpec((s, w), lambda h, i: (0, h))
    st = pl.BlockSpec((SUBLANES, t), lambda h, i: (h, i))
    return pl.pallas_call(
        body, name="mla_flash_bwd", grid=(heads, nq),
        in_specs=[qs(QK_PAD), ks(QK_PAD), ks(VDIM), qs(VDIM, do_cb0), st, st],
        out_specs=[qs(QK_PAD), ks(QK_PAD), ks(VDIM)],
        out_shape=[jax.ShapeDtypeStruct((s, heads * QK_PAD), F32), jax.ShapeDtypeStruct((s, heads * QK_PAD), F32),
                   jax.ShapeDtypeStruct((s, heads * VDIM), F32)],
        scratch_shapes=[pltpu.VMEM((t, QK_PAD), F32)],
        compiler_params=_params("parallel", "arbitrary"),
    )(qf, kf, v, do, lse_t, delta_t)


def _band_mask(tq, first_block):
    row = lax.broadcasted_iota(jnp.int32, (tq, DIL_STEPS + tq), 0)
    col = lax.broadcasted_iota(jnp.int32, (tq, DIL_STEPS + tq), 1)
    dist = row + DIL_STEPS - col
    valid = (dist >= 0) & (dist <= DIL_STEPS) & (jnp.logical_not(first_block) | (col >= DIL_STEPS))
    return dist, valid


def _dil_scores(q, kp, kc, slope, dil, tq, first_block):
    sc = jnp.concatenate([_dot(q, kp, NT), _dot(q, kc, NT)], axis=1) * (DHD ** -0.5)
    dist, valid = _band_mask(tq, first_block)
    return jnp.where(valid, sc - slope * (dil * dist).astype(F32), NEG)


def _dil_specs(proj_w, dh, tq):
    pwb = proj_w // LANES
    r_of = lambda cb: cb // dh
    h_of = lambda cb: cb % dh
    cur = lambda off: pl.BlockSpec((tq, DHD), lambda cb, i: (i, r_of(cb) * pwb + off + h_of(cb)))
    prev = lambda off: pl.BlockSpec(
        (DIL_STEPS, DHD), lambda cb, i: (jnp.maximum(i * (tq // DIL_STEPS) - 1, 0), r_of(cb) * pwb + off + h_of(cb)))
    return cur, prev


def _dil_fwd(name, proj, slopes, dil, dh, q_cb, tq=512):
    s, pw = proj.shape
    l = s // dil
    tq = min(tq, l)
    nb = l // tq
    k_cb, v_cb = q_cb + dh, q_cb + 2 * dh
    cur, prev = _dil_specs(pw, dh, tq)
    pv = proj.reshape(l, dil * pw)

    def body(q_ref, kc_ref, kp_ref, vc_ref, vp_ref, sl_ref, o_ref, lse_ref):
        i = pl.program_id(1)
        sc = _dil_scores(_bf(q_ref[...]), _bf(kp_ref[...]), _bf(kc_ref[...]), sl_ref[0:1, 0:1], dil, tq, i == 0)
        m = jnp.max(sc, axis=-1, keepdims=True)
        e = jnp.exp(sc - m)
        lsum = jnp.sum(e, axis=-1, keepdims=True)
        pn = e / lsum
        o_ref[...] = (_dot(_bf(pn[:, :DIL_STEPS]), _bf(vp_ref[...]), NN)
                      + _dot(_bf(pn[:, DIL_STEPS:]), _bf(vc_ref[...]), NN))
        lse_ref[...] = jnp.broadcast_to(m + jnp.log(lsum), (tq, LANES))

    ospec = pl.BlockSpec((tq, DHD), lambda cb, i: (i, cb))
    o, lse = pl.pallas_call(
        body, name=name, grid=(dil * dh, nb),
        in_specs=[cur(q_cb), cur(k_cb), prev(k_cb), cur(v_cb), prev(v_cb),
                  pl.BlockSpec((SUBLANES, LANES), lambda cb, i: (cb % dh, 0))],
        out_specs=[ospec, ospec],
        out_shape=[jax.ShapeDtypeStruct((l, dil * dh * DHD), F32)] * 2,
        compiler_params=_params("parallel", "parallel"),
    )(pv, pv, pv, pv, pv, slopes)
    return o.reshape(s, dh * DHD), lse.reshape(s, dh * DHD)


def _dil_bwd_dq(name, proj, slopes, do, lse, delta, dil, dh, q_cb, b_cb0, tq=512):
    s, pw = proj.shape
    mixw = do.shape[1]
    l = s // dil
    tq = min(tq, l)
    nb = l // tq
    k_cb, v_cb = q_cb + dh, q_cb + 2 * dh
    cur, prev = _dil_specs(pw, dh, tq)
    pv = proj.reshape(l, dil * pw)
    mb = mixw // LANES
    mspec = pl.BlockSpec((tq, DHD), lambda cb, i: (i, (cb // dh) * mb + b_cb0 + cb % dh))
    ospec = pl.BlockSpec((tq, DHD), lambda cb, i: (i, cb))

    def body(q_ref, kc_ref, kp_ref, vc_ref, vp_ref, sl_ref, do_ref, lse_ref, dl_ref, dq_ref):
        i = pl.program_id(1)
        kp, kc = _bf(kp_ref[...]), _bf(kc_ref[...])
        sc = _dil_scores(_bf(q_ref[...]), kp, kc, sl_ref[0:1, 0:1], dil, tq, i == 0)
        p = jnp.exp(sc - lse_ref[:, 0:1])
        dob = do_ref[...]
        dp = jnp.concatenate([_dot(dob, _bf(vp_ref[...]), NT), _dot(dob, _bf(vc_ref[...]), NT)], axis=1)
        ds = _bf(p * (dp - dl_ref[:, 0:1]) * (DHD ** -0.5))
        dq_ref[...] = _dot(ds[:, :DIL_STEPS], kp, NN) + _dot(ds[:, DIL_STEPS:], kc, NN)

    dq = pl.pallas_call(
        body, name=name, grid=(dil * dh, nb),
        in_specs=[cur(q_cb), cur(k_cb), prev(k_cb), cur(v_cb), prev(v_cb),
                  pl.BlockSpec((SUBLANES, LANES), lambda cb, i: (cb % dh, 0)), mspec, ospec, mspec],
        out_specs=ospec, out_shape=jax.ShapeDtypeStruct((l, dil * dh * DHD), F32),
        compiler_params=_params("parallel", "parallel"),
    )(pv, pv, pv, pv, pv, slopes, do.reshape(l, dil * mixw), lse.reshape(l, dil * dh * DHD), delta.reshape(l, dil * mixw))
    return dq.reshape(s, dh * DHD)


def _dil_bwd_dkv(name, proj, slopes, do, lse, delta, dil, dh, q_cb, b_cb0, tk=512):
    s, pw = proj.shape
    mixw = do.shape[1]
    l = s // dil
    tk = min(tk, l)
    nb = l // tk
    k_cb, v_cb = q_cb + dh, q_cb + 2 * dh
    pwb, mb = pw // LANES, mixw // LANES
    sub = tk // DIL_STEPS
    last128 = l // DIL_STEPS - 1
    pv = proj.reshape(l, dil * pw)

    def cur(width_blocks, off):
        return pl.BlockSpec((tk, DHD), lambda cb, j: (j, (cb // dh) * width_blocks + off + cb % dh))

    def nxt(width_blocks, off):
        return pl.BlockSpec((DIL_STEPS, DHD), lambda cb, j: (jnp.minimum((j + 1) * sub, last128),
                                                               (cb // dh) * width_blocks + off + cb % dh))

    ocur = pl.BlockSpec((tk, DHD), lambda cb, j: (j, cb))
    onxt = pl.BlockSpec((DIL_STEPS, DHD), lambda cb, j: (jnp.minimum((j + 1) * sub, last128), cb))

    def body(k_ref, v_ref, qc_ref, qn_ref, sl_ref, doc_ref, don_ref, lsec_ref, lsen_ref, dlc_ref, dln_ref,
             dk_ref, dv_ref):
        j = pl.program_id(1)
        slope = sl_ref[0:1, 0:1]
        scale = DHD ** -0.5
        k, v = _bf(k_ref[...]), _bf(v_ref[...])
        qc = _bf(qc_ref[...])
        row = lax.broadcasted_iota(jnp.int32, (tk, tk), 0)
        col = lax.broadcasted_iota(jnp.int32, (tk, tk), 1)
        dist = row - col
        valid = (dist >= 0) & (dist <= DIL_STEPS)
        sc = jnp.where(valid, _dot(qc, k, NT) * scale - slope * (dil * dist).astype(F32), NEG)
        p = jnp.exp(sc - lsec_ref[:, 0:1])
        doc = doc_ref[...]
        ds = _bf(p * (_dot(doc, v, NT) - dlc_ref[:, 0:1]) * scale)
        dv_ref[...] = _dot(_bf(p), doc, TN)
        dk_ref[...] = _dot(ds, qc, TN)
        kl, vl = k[tk - DIL_STEPS:, :], v[tk - DIL_STEPS:, :]
        qn = _bf(qn_ref[...])
        row = lax.broadcasted_iota(jnp.int32, (DIL_STEPS, DIL_STEPS), 0)
        col = lax.broadcasted_iota(jnp.int32, (DIL_STEPS, DIL_STEPS), 1)
        dist = DIL_STEPS + row - col
        valid = (dist <= DIL_STEPS) & (j < nb - 1)
        sc = jnp.where(valid, _dot(qn, kl, NT) * scale - slope * (dil * dist).astype(F32), NEG)
        p = jnp.exp(sc - lsen_ref[:, 0:1])
        don = don_ref[...]
        ds = _bf(p * (_dot(don, vl, NT) - dln_ref[:, 0:1]) * scale)
        dv_ref[tk - DIL_STEPS:, :] += _dot(_bf(p), don, TN)
        dk_ref[tk - DIL_STEPS:, :] += _dot(ds, qn, TN)

    dov = do.reshape(l, dil * mixw)
    dlv = delta.reshape(l, dil * mixw)
    lsv = lse.reshape(l, dil * dh * DHD)
    dk, dv = pl.pallas_call(
        body, name=name, grid=(dil * dh, nb),
        in_specs=[cur(pwb, k_cb), cur(pwb, v_cb), cur(pwb, q_cb), nxt(pwb, q_cb),
                  pl.BlockSpec((SUBLANES, LANES), lambda cb, j: (cb % dh, 0)),
                  cur(mb, b_cb0), nxt(mb, b_cb0), ocur, onxt, cur(mb, b_cb0), nxt(mb, b_cb0)],
        out_specs=[ocur, ocur], out_shape=[jax.ShapeDtypeStruct((l, dil * dh * DHD), F32)] * 2,
        compiler_params=_params("parallel", "parallel"),
    )(pv, pv, pv, pv, slopes, dov, dov, lsv, lsv, dlv, dlv)
    return dk.reshape(s, dh * DHD), dv.reshape(s, dh * DHD)


def _dil_merge(out_a, outs, lses, tile=256):
    s, wa = out_a.shape
    wb = outs[0].shape[1]
    nbr = len(outs)

    def body(*refs):
        a_ref = refs[0]
        o_refs, l_refs = refs[1:1 + nbr], refs[1 + nbr:1 + 2 * nbr]
        att_ref, ob_ref, lse_ref = refs[1 + 2 * nbr:]
        ls = [r[...] for r in l_refs]
        m = ls[0]
        for x_ in ls[1:]:
            m = jnp.maximum(m, x_)
        es = [jnp.exp(x_ - m) for x_ in ls]
        tot = es[0]
        for e in es[1:]:
            tot = tot + e
        ob = (es[0] / tot) * o_refs[0][...]
        for e, r in zip(es[1:], o_refs[1:]):
            ob = ob + (e / tot) * r[...]
        ob_ref[...] = ob
        lse_ref[...] = m + jnp.log(tot)
        att_ref[...] = jnp.concatenate([_bf(a_ref[...]), _bf(ob)], axis=1)

    return pl.pallas_call(
        body, name="dil_merge", grid=(s // tile,),
        in_specs=[_nat(tile, wa)] + [_nat(tile, wb)] * (2 * nbr),
        out_specs=[_nat(tile, wa + wb), _nat(tile, wb), _nat(tile, wb)],
        out_shape=[jax.ShapeDtypeStruct((s, wa + wb), BF16), jax.ShapeDtypeStruct((s, wb), F32),
                   jax.ShapeDtypeStruct((s, wb), F32)],
        compiler_params=_params("parallel"),
    )(out_a, *outs, *lses)


def _attn_bwd_prep(datt, out_a, out_b, tile=256):
    s, mixw = datt.shape
    wa = out_a.shape[1]
    heads_a = wa // LANES

    def body(d_ref, a_ref, b_ref, do_ref, dl_ref, dlt_ref):
        d = d_ref[...]
        do_ref[...] = _bf(d)
        prod = d * jnp.concatenate([a_ref[...], b_ref[...]], axis=1)
        for hh in range(mixw // LANES):
            sl = slice(hh * LANES, (hh + 1) * LANES)
            dl = jnp.broadcast_to(jnp.sum(prod[:, sl], axis=-1, keepdims=True), (tile, LANES))
            dl_ref[:, sl] = dl
            if hh < heads_a:
                dlt_ref[hh * SUBLANES:(hh + 1) * SUBLANES, :] = dl.T[0:SUBLANES, :]

    return pl.pallas_call(
        body, name="attn_bwd_prep", grid=(s // tile,),
        in_specs=[_nat(tile, mixw), _nat(tile, wa), _nat(tile, mixw - wa)],
        out_specs=[_nat(tile, mixw), _nat(tile, mixw), pl.BlockSpec((heads_a * SUBLANES, tile), lambda i: (0, i))],
        out_shape=[jax.ShapeDtypeStruct((s, mixw), BF16), jax.ShapeDtypeStruct((s, mixw), F32),
                   jax.ShapeDtypeStruct((heads_a * SUBLANES, s), F32)],
        compiler_params=_params("parallel"),
    )(datt, out_a, out_b)


def _dproj_assemble(proj, dnq, dnkv, dkpe, dqs, dks, dvs, gq, gkv, ql, tile=256):
    s, pw = proj.shape
    dw = dqs[0].shape[1]
    nbr = len(dqs)

    def body(*refs):
        ql_ref, kvl_ref, dnq_ref, dnkv_ref, dkpe_ref = refs[:5]
        br = refs[5:5 + 3 * nbr]
        gq_ref, gkv_ref = refs[5 + 3 * nbr:7 + 3 * nbr]
        dp_ref, dgq_ref, dgkv_ref = refs[7 + 3 * nbr:]

        @pl.when(pl.program_id(0) == 0)
        def _():
            dgq_ref[...] = jnp.zeros_like(dgq_ref)
            dgkv_ref[...] = jnp.zeros_like(dgkv_ref)

        def rms_bwd(x, dy, gg, dg_ref):
            r = lax.rsqrt(jnp.mean(x * x, axis=-1, keepdims=True) + RMS_EPS)
            xh = x * r
            dxh = dy * gg
            dg_ref[0:1, :] += jnp.sum(dy * xh, axis=0, keepdims=True)
            return r * (dxh - xh * jnp.mean(dxh * xh, axis=-1, keepdims=True))

        pieces = [_bf(rms_bwd(ql_ref[...], dnq_ref[...], gq_ref[...], dgq_ref)),
                  _bf(rms_bwd(kvl_ref[...], dnkv_ref[...], gkv_ref[...], dgkv_ref)),
                  _bf(dkpe_ref[...])]
        for k in range(3):
            acc = br[k * nbr][...]
            for r in br[k * nbr + 1:(k + 1) * nbr]:
                acc = acc + r[...]
            pieces.append(_bf(acc))
        dp_ref[...] = jnp.concatenate(pieces, axis=1)

    res = pl.pallas_call(
        body, name="dproj_assemble", grid=(s // tile,),
        in_specs=[_nat(tile, ql, 0), _nat(tile, ql, 1), _nat(tile, ql), _nat(tile, ql), _nat(tile, LANES)]
        + [_nat(tile, dw)] * (3 * nbr) + [_whole((1, ql)), _whole((1, ql))],
        out_specs=[_nat(tile, pw), _whole((SUBLANES, ql)), _whole((SUBLANES, ql))],
        out_shape=[jax.ShapeDtypeStruct((s, pw), BF16), jax.ShapeDtypeStruct((SUBLANES, ql), F32),
                   jax.ShapeDtypeStruct((SUBLANES, ql), F32)],
        compiler_params=_params("arbitrary"),
    )(proj, proj, dnq, dnkv, dkpe, *dqs, *dks, *dvs, gq.reshape(1, ql), gkv.reshape(1, ql))
    return res[0], res[1][0], res[2][0]


def _axpy(name, alpha, a, b, tile=256):
    s, d = a.shape

    def body(a_ref, b_ref, o_ref):
        o_ref[...] = alpha * a_ref[...] + b_ref[...]

    return pl.pallas_call(
        body, name=name, grid=(s // tile,), in_specs=[_nat(tile, d), _nat(tile, d)], out_specs=_nat(tile, d),
        out_shape=jax.ShapeDtypeStruct((s, d), F32), compiler_params=_params("parallel"),
    )(a, b)


def _cmul(ar, ai, br, bi):
    return ar * br - ai * bi, ar * bi + ai * br


def _s5_discretise(a_re, a_im, log_dt, b_re, b_im, n_sq):
    shape = a_re.shape

    def body(ar_ref, ai_ref, ldt_ref, br_ref, bi_ref, abr_ref, abi_ref, apr_ref, api_ref, bbr_ref, bbi_ref):
        ar, ai = ar_ref[...], ai_ref[...]
        dt = jnp.exp(ldt_ref[...])
        e = jnp.exp(ar * dt)
        abr, abi = e * jnp.cos(ai * dt), e * jnp.sin(ai * dt)
        den = ar * ar + ai * ai
        qr = ((abr - 1.0) * ar + abi * ai) / den
        qi = (abi * ar - (abr - 1.0) * ai) / den
        bbr, bbi = _cmul(qr, qi, br_ref[...], bi_ref[...])
        abr_ref[...], abi_ref[...] = abr, abi
        bbr_ref[...], bbi_ref[...] = bbr, bbi
        pr, pi = abr, abi
        for _ in range(n_sq):
            pr, pi = _cmul(pr, pi, pr, pi)
        apr_ref[...], api_ref[...] = pr, pi

    return pl.pallas_call(
        body, name="s5_discretise", out_shape=[jax.ShapeDtypeStruct(shape, F32)] * 6,
        compiler_params=pltpu.CompilerParams(vmem_limit_bytes=VMEM_LIMIT),
    )(a_re, a_im, log_dt, b_re, b_im)


def _s5_discretise_bwd(a16, b16, ag, gab, gbb):
    rows, p = a16[0].shape
    g = rows // S5_GROUP

    def disc(ar, ai, ldt):
        dt = jnp.exp(ldt)
        e = jnp.exp(ar * dt)
        abr, abi = e * jnp.cos(ai * dt), e * jnp.sin(ai * dt)
        den = ar * ar + ai * ai
        inv_r, inv_i = ar / den, -ai / den
        qr, qi = _cmul(abr - 1.0, abi, inv_r, inv_i)
        return dt, abr, abi, inv_r, inv_i, qr, qi

    def body(ar16_ref, ai16_ref, ldt16_ref, br_ref, bi_ref, ar_ref, ai_ref, ldt_ref, gar_ref, gai_ref, gbr_ref, gbi_ref,
             dar_ref, dai_ref, dldt_ref, dbr_ref, dbi_ref):
        _, _, _, _, _, qr16, qi16 = disc(ar16_ref[...], ai16_ref[...], ldt16_ref[...])
        gbr, gbi = gbr_ref[...], gbi_ref[...]
        dbr_ref[...], dbi_ref[...] = _cmul(qr16, -qi16, gbr, gbi)
        cr, ci = _cmul(br_ref[...], -bi_ref[...], gbr, gbi)
        gqr = jnp.sum(cr.reshape(g, S5_GROUP, p), axis=1)
        gqi = jnp.sum(ci.reshape(g, S5_GROUP, p), axis=1)
        ar, ai = ar_ref[...], ai_ref[...]
        dt, abr, abi, inv_r, inv_i, qr, qi = disc(ar, ai, ldt_ref[...])
        t_r, t_i = _cmul(inv_r, -inv_i, gqr, gqi)
        gab_r = gar_ref[...] + t_r
        gab_i = gai_ref[...] + t_i
        qa_r, qa_i = _cmul(qr, qi, inv_r, inv_i)
        a1_r, a1_i = _cmul(qa_r, -qa_i, gqr, gqi)
        gl_r, gl_i = _cmul(abr, -abi, gab_r, gab_i)
        dar_ref[...] = dt * gl_r - a1_r
        dai_ref[...] = dt * gl_i - a1_i
        gdt = jnp.sum(ar * gl_r + ai * gl_i, axis=-1, keepdims=True)
        dldt_ref[...] = gdt * dt[:, 0:1]

    return pl.pallas_call(
        body, name="s5_discretise_bwd",
        out_shape=[jax.ShapeDtypeStruct((g, p), F32), jax.ShapeDtypeStruct((g, p), F32),
                   jax.ShapeDtypeStruct((g, 1), F32), jax.ShapeDtypeStruct((rows, p), F32),
                   jax.ShapeDtypeStruct((rows, p), F32)],
        compiler_params=pltpu.CompilerParams(vmem_limit_bytes=VMEM_LIMIT),
    )(*a16, *b16, *ag, *gab, *gbb)


def _slab_tile(re, im, nsl):
    row = jnp.concatenate([re.reshape(nsl, SLAB_COLS), im.reshape(nsl, SLAB_COLS)], axis=-1)
    return jnp.repeat(row, SUBLANES, axis=0)


def _slab_in_matrix(b_re, b_im, nsl):
    eye = jnp.eye(SLAB_GROUPS, dtype=F32)

    def blk(b):
        b = b.reshape(nsl, SLAB_GROUPS, S5_GROUP, S5_STATE)
        return jnp.einsum('sgcp,gh->sgchp', b, eye).reshape(nsl, LANES, SLAB_COLS)

    return jnp.concatenate([blk(b_re), blk(b_im)], axis=-1)


def _slab_in_extract(m, nsl):
    eye = jnp.eye(SLAB_GROUPS, dtype=F32)

    def ext(x_):
        x_ = x_.reshape(nsl, SLAB_GROUPS, S5_GROUP, SLAB_GROUPS, S5_STATE)
        return jnp.einsum('sgchp,gh->sgcp', x_, eye).reshape(nsl * LANES, S5_STATE)

    return ext(m[..., :SLAB_COLS]), ext(m[..., SLAB_COLS:])


def _slab_out_matrix(c_re, c_im, nsl):
    eye = jnp.eye(SLAB_GROUPS, dtype=F32)

    def blk(c):
        c = c.reshape(nsl, SLAB_GROUPS, S5_GROUP, S5_STATE)
        return jnp.einsum('sgcp,gh->sgphc', c, eye).reshape(nsl, SLAB_COLS, LANES)

    return jnp.concatenate([blk(c_re), -blk(c_im)], axis=1)


def _slab_out_extract(m, nsl):
    eye = jnp.eye(SLAB_GROUPS, dtype=F32)

    def ext(x_):
        x_ = x_.reshape(nsl, SLAB_GROUPS, S5_STATE, SLAB_GROUPS, S5_GROUP)
        return jnp.einsum('sgphc,gh->sgcp', x_, eye).reshape(nsl * SLAB_GROUPS, S5_GROUP, S5_STATE)

    return ext(m[:, :SLAB_COLS]), -ext(m[:, SLAB_COLS:])


def _gelu(y):
    t = jnp.tanh(0.7978845608028654 * (y + 0.044715 * y * y * y))
    return 0.5 * y * (1.0 + t)


def _gelu_grad(y):
    t = jnp.tanh(0.7978845608028654 * (y + 0.044715 * y * y * y))
    return 0.5 * (1.0 + t) + 0.5 * y * (1.0 - t * t) * 0.7978845608028654 * (1.0 + 3.0 * 0.044715 * y * y)


def _scan_rows(ref, n_steps, ar, ai, state, reverse, conj):
    sgn = -1.0 if conj else 1.0

    def step(k, carry):
        xr, xi = carry
        t = (n_steps - 1 - k) if reverse else k
        r0 = pl.multiple_of(t * SUBLANES, SUBLANES)
        nr = ar * xr - sgn * ai * xi + ref[pl.ds(r0, SUBLANES), :SLAB_COLS]
        ni = ar * xi + sgn * ai * xr + ref[pl.ds(r0, SUBLANES), SLAB_COLS:]
        ref[pl.ds(r0, SUBLANES), :SLAB_COLS] = nr
        ref[pl.ds(r0, SUBLANES), SLAB_COLS:] = ni
        return nr, ni

    return lax.fori_loop(0, n_steps, step, state, unroll=4)


def _s5_pass1(hp, bblk, ab_tile, rc=1024):
    s, d = hp.shape
    nsl = d // LANES
    rc = min(rc, s)
    nch = s // rc
    w = 2 * SLAB_COLS

    def body(u_ref, b_ref, ab_ref, x_ref, end_ref, st_ref):
        j = pl.program_id(1)

        @pl.when(j == 0)
        def _():
            st_ref[...] = jnp.zeros_like(st_ref)

        x_ref[...] = _dot(_bf(u_ref[...]), b_ref[0], NN)
        xr, xi = _scan_rows(x_ref, rc // SUBLANES, ab_ref[:, :SLAB_COLS], ab_ref[:, SLAB_COLS:],
                            (st_ref[:, :SLAB_COLS], st_ref[:, SLAB_COLS:]), False, False)
        st_ref[:, :SLAB_COLS] = xr
        st_ref[:, SLAB_COLS:] = xi

        @pl.when(j == nch - 1)
        def _():
            end_ref[...] = st_ref[...]

    return pl.pallas_call(
        body, name="s5_scan_local", grid=(nsl, nch),
        in_specs=[pl.BlockSpec((rc, LANES), lambda sl, j: (j, sl)), pl.BlockSpec((1, LANES, w), lambda sl, j: (sl, 0, 0)),
                  pl.BlockSpec((SUBLANES, w), lambda sl, j: (sl, 0))],
        out_specs=[pl.BlockSpec((rc, w), lambda sl, j: (j, sl)), pl.BlockSpec((SUBLANES, w), lambda sl, j: (sl, 0))],
        out_shape=[jax.ShapeDtypeStruct((s, nsl * w), F32), jax.ShapeDtypeStruct((nsl * SUBLANES, w), F32)],
        scratch_shapes=[pltpu.VMEM((SUBLANES, w), F32)],
        compiler_params=_params("parallel", "arbitrary"),
    )(hp, bblk, ab_tile)


def _s5_carry(name, ends, ap_tile, reverse):
    rows, w = ends.shape
    nsl = rows // SUBLANES
    sgn = -1.0 if reverse else 1.0

    def body(e_ref, ap_ref, c_ref):
        pr, pi = ap_ref[0:1, :SLAB_COLS], sgn * ap_ref[0:1, SLAB_COLS:]
        tr = jnp.zeros((1, SLAB_COLS), F32)
        ti = jnp.zeros((1, SLAB_COLS), F32)
        order = range(SUBLANES - 1, -1, -1) if reverse else range(SUBLANES)
        for seg in order:
            c_ref[seg:seg + 1, :SLAB_COLS] = tr
            c_ref[seg:seg + 1, SLAB_COLS:] = ti
            mr, mi = _cmul(pr, pi, tr, ti)
            tr = e_ref[seg:seg + 1, :SLAB_COLS] + mr
            ti = e_ref[seg:seg + 1, SLAB_COLS:] + mi

    spec = pl.BlockSpec((SUBLANES, w), lambda sl: (sl, 0))
    return pl.pallas_call(
        body, name=name, grid=(nsl,), in_specs=[spec, spec], out_specs=spec,
        out_shape=jax.ShapeDtypeStruct((rows, w), F32), compiler_params=_params("parallel"),
    )(ends, ap_tile)


def _s5_pass2(xloc, cin, ab_tile, cblk, hp, dvec, rc=1024):
    s, d = hp.shape
    nsl = d // LANES
    rc = min(rc, s)
    nch = s // rc
    w = 2 * SLAB_COLS

    def body(xl_ref, cin_ref, ab_ref, c_ref, h_ref, d_ref, x_ref, y_ref, z_ref, st_ref):
        j = pl.program_id(1)

        @pl.when(j == 0)
        def _():
            st_ref[...] = cin_ref[...]

        x_ref[...] = jnp.zeros_like(x_ref)
        zr, zi = _scan_rows(x_ref, rc // SUBLANES, ab_ref[:, :SLAB_COLS], ab_ref[:, SLAB_COLS:],
                            (st_ref[:, :SLAB_COLS], st_ref[:, SLAB_COLS:]), False, False)
        st_ref[:, :SLAB_COLS] = zr
        st_ref[:, SLAB_COLS:] = zi
        x = x_ref[...] + xl_ref[...]
        x_ref[...] = x
        y = _dot(_bf(x), c_ref[0], NN) + d_ref[...] * h_ref[...]
        y_ref[...] = y
        z_ref[...] = _bf(_gelu(y))

    tile = lambda wd: pl.BlockSpec((rc, wd), lambda sl, j: (j, sl))
    small = pl.BlockSpec((SUBLANES, w), lambda sl, j: (sl, 0))
    return pl.pallas_call(
        body, name="s5_scan_carry_out", grid=(nsl, nch),
        in_specs=[tile(w), small, small, pl.BlockSpec((1, w, LANES), lambda sl, j: (sl, 0, 0)), tile(LANES),
                  pl.BlockSpec((1, LANES), lambda sl, j: (0, sl))],
        out_specs=[tile(w), tile(LANES), tile(LANES)],
        out_shape=[jax.ShapeDtypeStruct((s, nsl * w), F32), jax.ShapeDtypeStruct((s, d), F32),
                   jax.ShapeDtypeStruct((s, d), BF16)],
        scratch_shapes=[pltpu.VMEM((SUBLANES, w), F32)],
        compiler_params=_params("parallel", "arbitrary"),
    )(xloc, cin, ab_tile, cblk, hp, dvec)


def _s5_bwd_pass1(dzg, ypre, cblk, ab_tile, hp, rc=1024):
    s, d = hp.shape
    nsl = d // LANES
    rc = min(rc, s)
    nch = s // rc
    w = 2 * SLAB_COLS

    def body(dz_ref, y_ref, c_ref, ab_ref, h_ref, lam_ref, st_out_ref, dy_ref, dd_ref, st_ref):
        j = pl.program_id(1)

        @pl.when(j == 0)
        def _():
            st_ref[...] = jnp.zeros_like(st_ref)
            dd_ref[...] = jnp.zeros_like(dd_ref)

        dy = dz_ref[...] * _gelu_grad(y_ref[...])
        dy_ref[...] = dy
        dd_ref[0:1, :] += jnp.sum(dy * h_ref[...], axis=0, keepdims=True)
        lam_ref[...] = _dot(_bf(dy), c_ref[0], NT)
        lr, li = _scan_rows(lam_ref, rc // SUBLANES, ab_ref[:, :SLAB_COLS], ab_ref[:, SLAB_COLS:],
                            (st_ref[:, :SLAB_COLS], st_ref[:, SLAB_COLS:]), True, True)
        st_ref[:, :SLAB_COLS] = lr
        st_ref[:, SLAB_COLS:] = li

        @pl.when(j == nch - 1)
        def _():
            st_out_ref[...] = st_ref[...]

    tile = lambda wd: pl.BlockSpec((rc, wd), lambda sl, j: (nch - 1 - j, sl))
    small = pl.BlockSpec((SUBLANES, w), lambda sl, j: (sl, 0))
    return pl.pallas_call(
        body, name="s5_adjoint_local", grid=(nsl, nch),
        in_specs=[tile(LANES), tile(LANES), pl.BlockSpec((1, w, LANES), lambda sl, j: (sl, 0, 0)), small, tile(LANES)],
        out_specs=[tile(w), small, tile(LANES), pl.BlockSpec((SUBLANES, LANES), lambda sl, j: (0, sl))],
        out_shape=[jax.ShapeDtypeStruct((s, nsl * w), F32), jax.ShapeDtypeStruct((nsl * SUBLANES, w), F32),
                   jax.ShapeDtypeStruct((s, d), F32), jax.ShapeDtypeStruct((SUBLANES, d), F32)],
        scratch_shapes=[pltpu.VMEM((SUBLANES, w), F32)],
        compiler_params=_params("parallel", "arbitrary"),
    )(dzg, ypre, cblk, ab_tile, hp)


def _s5_bwd_pass2(lamloc, cinl, ab_tile, xtrue, cinx, hp, dy, bblk, dvec, rc=1024):
    s, d = hp.shape
    nsl = d // LANES
    rc = min(rc, s)
    nch = s // rc
    w = 2 * SLAB_COLS
    n_steps = rc // SUBLANES

    def body(ll_ref, cl_ref, ab_ref, x_ref, xp_ref, cx_ref, h_ref, dy_ref, b_ref, d_ref,
             du_ref, db_ref, dc_ref, da_ref, st_ref, lam_ref, acc_ref):
        j = pl.program_id(1)

        @pl.when(j == 0)
        def _():
            st_ref[...] = cl_ref[...]
            acc_ref[...] = jnp.zeros_like(acc_ref)
            db_ref[...] = jnp.zeros_like(db_ref)
            dc_ref[...] = jnp.zeros_like(dc_ref)

        ar, ai = ab_ref[:, :SLAB_COLS], ab_ref[:, SLAB_COLS:]
        lam_ref[...] = jnp.zeros_like(lam_ref)
        zr, zi = _scan_rows(lam_ref, n_steps, ar, ai, (st_ref[:, :SLAB_COLS], st_ref[:, SLAB_COLS:]), True, True)
        st_ref[:, :SLAB_COLS] = zr
        st_ref[:, SLAB_COLS:] = zi
        lam_ref[...] = lam_ref[...] + ll_ref[...]

        def step(k, carry):
            dr, di = carry
            r0 = pl.multiple_of(k * SUBLANES, SUBLANES)
            r1 = pl.multiple_of((k + 1) * SUBLANES, SUBLANES)
            xr, xi = x_ref[pl.ds(r0, SUBLANES), :SLAB_COLS], x_ref[pl.ds(r0, SUBLANES), SLAB_COLS:]
            lr, li = lam_ref[pl.ds(r1, SUBLANES), :SLAB_COLS], lam_ref[pl.ds(r1, SUBLANES), SLAB_COLS:]
            return dr + xr * lr + xi * li, di + xr * li - xi * lr

        dr, di = lax.fori_loop(0, n_steps - 1, step, (acc_ref[:, :SLAB_COLS], acc_ref[:, SLAB_COLS:]), unroll=4)
        first_chunk = j == nch - 1
        xr = jnp.where(first_chunk, cx_ref[:, :SLAB_COLS], xp_ref[:, :SLAB_COLS])
        xi = jnp.where(first_chunk, cx_ref[:, SLAB_COLS:], xp_ref[:, SLAB_COLS:])
        lr, li = lam_ref[0:SUBLANES, :SLAB_COLS], lam_ref[0:SUBLANES, SLAB_COLS:]
        acc_ref[:, :SLAB_COLS] = dr + xr * lr + xi * li
        acc_ref[:, SLAB_COLS:] = di + xr * li - xi * lr

        lam_b = _bf(lam_ref[...])
        dyv = dy_ref[...]
        db_ref[0] += _dot(_bf(h_ref[...]), lam_b, TN)
        dc_ref[0] += _dot(_bf(x_ref[...]), _bf(dyv), TN)
        du_ref[...] = _dot(lam_b, b_ref[0], NT) + d_ref[...] * dyv

        @pl.when(j == nch - 1)
        def _():
            da_ref[...] = jnp.broadcast_to(jnp.sum(acc_ref[...], axis=0, keepdims=True), (SUBLANES, w))

    sub = rc // SUBLANES
    tile = lambda wd: pl.BlockSpec((rc, wd), lambda sl, j: (nch - 1 - j, sl))
    small = pl.BlockSpec((SUBLANES, w), lambda sl, j: (sl, 0))
    prev = pl.BlockSpec((SUBLANES, w), lambda sl, j: (jnp.maximum((nch - 1 - j) * sub - 1, 0), sl))
    return pl.pallas_call(
        body, name="s5_adjoint_carry_grads", grid=(nsl, nch),
        in_specs=[tile(w), small, small, tile(w), prev, small, tile(LANES), tile(LANES),
                  pl.BlockSpec((1, LANES, w), lambda sl, j: (sl, 0, 0)), pl.BlockSpec((1, LANES), lambda sl, j: (0, sl))],
        out_specs=[tile(LANES), pl.BlockSpec((1, LANES, w), lambda sl, j: (sl, 0, 0)),
                   pl.BlockSpec((1, w, LANES), lambda sl, j: (sl, 0, 0)), small],
        out_shape=[jax.ShapeDtypeStruct((s, d), F32), jax.ShapeDtypeStruct((nsl, LANES, w), F32),
                   jax.ShapeDtypeStruct((nsl, w, LANES), F32), jax.ShapeDtypeStruct((nsl * SUBLANES, w), F32)],
        scratch_shapes=[pltpu.VMEM((SUBLANES, w), F32), pltpu.VMEM((rc, w), F32), pltpu.VMEM((SUBLANES, w), F32)],
        compiler_params=_params("parallel", "arbitrary"),
    )(lamloc, cinl, ab_tile, xtrue, xtrue, cinx, hp, dy, bblk, dvec)


def _adamw(name, w, g, m, v):
    r, c = w.shape
    tile = r if r * c <= 512 * 1024 else _pick(r, max(SUBLANES, (512 * 1024 // c) // SUBLANES * SUBLANES), q=SUBLANES)
    c1 = 1.0 / (1.0 - ADAM_B1 ** ADAM_STEP)
    c2 = 1.0 / (1.0 - ADAM_B2 ** ADAM_STEP)

    def body(w_ref, g_ref, m_ref, v_ref, d_ref, nm_ref, nv_ref):
        gg = g_ref[...]
        nm = ADAM_B1 * m_ref[...] + (1.0 - ADAM_B1) * gg
        nv = ADAM_B2 * v_ref[...] + (1.0 - ADAM_B2) * gg * gg
        d_ref[...] = -ADAM_LR * ((nm * c1) / (jnp.sqrt(nv * c2) + ADAM_EPS) + ADAM_WD * w_ref[...])
        nm_ref[...] = nm
        nv_ref[...] = nv

    spec = _nat(tile, c)
    return pl.pallas_call(
        body, name=name, grid=(r // tile,), in_specs=[spec] * 4, out_specs=[spec] * 3,
        out_shape=[jax.ShapeDtypeStruct((r, c), F32)] * 3, compiler_params=_params("parallel"),
    )(w, g, m, v)


def _place():
    x, y, c = lax.axis_index("x"), lax.axis_index("y"), lax.axis_index("c")
    return x, y, c, [(1 - x, y), (x, 1 - y), (1 - x, 1 - y)]


_ANY = pl.BlockSpec(memory_space=pl.ANY)


def _gather_weights(shards):
    n = len(shards)

    def body(*refs):
        ins, outs = refs[:n], refs[n:2 * n]
        send_sems, recv_sems, local_sems = refs[2 * n:]
        x, y, c, chips = _place()
        me = 2 * x + y
        sibling = (x, y, 1 - c)
        started = []
        for a in range(n):
            local = pltpu.make_async_copy(ins[a], outs[a].at[me], local_sems.at[a])
            local.start()
            started.append(local)

        def half(a, chip, h):
            hw = ins[a].shape[1] // 2
            return outs[a].at[chip, :, pl.ds(pl.multiple_of(h * hw, LANES), hw)]

        def copy(a, k, src, chip, h, to):
            return pltpu.make_async_remote_copy(
                src_ref=src, dst_ref=half(a, chip, h), send_sem=send_sems.at[a, k], recv_sem=recv_sems.at[a, k],
                device_id=to, device_id_type=MESH)

        sends = []
        for a in range(n):
            hw = ins[a].shape[1] // 2
            mine = ins[a].at[:, pl.ds(pl.multiple_of(c * hw, LANES), hw)]
            for k, chip in enumerate(chips):
                cp = copy(a, k, mine, me, c, (*chip, c))
                cp.start()
                sends.append(cp)
        for a in range(n):
            for k, (cx, cy) in enumerate(chips):
                src_chip = 2 * cx + cy
                copy(a, k, half(a, src_chip, c), src_chip, c, (x, y, c)).wait_recv()
                fwd = copy(a, 3 + k, half(a, src_chip, c), src_chip, c, sibling)
                fwd.start()
                sends.append(fwd)
        for a in range(n):
            for k, (cx, cy) in enumerate(chips):
                src_chip = 2 * cx + cy
                copy(a, 3 + k, half(a, src_chip, 1 - c), src_chip, 1 - c, (x, y, c)).wait_recv()
        for cp in sends:
            cp.wait_send()
        for cp in started:
            cp.wait()

    return pl.pallas_call(
        body, name="gather_weights",
        in_specs=[_ANY] * n, out_specs=[_ANY] * n,
        out_shape=[jax.ShapeDtypeStruct((N_CHIPS,) + s_.shape, s_.dtype) for s_ in shards],
        scratch_shapes=[pltpu.SemaphoreType.DMA((n, 6)), pltpu.SemaphoreType.DMA((n, 6)), pltpu.SemaphoreType.DMA((n,))],

    )(*shards)


def _gather_weights_async(shards):
    n = len(shards)
    srcs = [jax.new_ref(s_, memory_space=pltpu.MemorySpace.HBM) for s_ in shards]
    outs = [jax.empty_ref(jax.ShapeDtypeStruct((N_CHIPS,) + s_.shape, s_.dtype), memory_space=pltpu.MemorySpace.HBM)
            for s_ in shards]

    @pl.kernel(mesh=plsc.ScalarSubcoreMesh(axis_name="seq", num_cores=1), name="gather_weights_async",
               scratch_types=(pltpu.SemaphoreType.DMA((n, 6)), pltpu.SemaphoreType.DMA((n, 6)),
                              pltpu.SemaphoreType.DMA((n,))),
               compiler_params=pltpu.CompilerParams(collective_id=1))
    def launch(send_sems, recv_sems, local_sems):
        x, y, c, chips = _place()
        me = 2 * x + y
        sibling = (x, y, 1 - c)
        barrier = pltpu.get_barrier_semaphore()
        for peer in [sibling] + [(*chip, c) for chip in chips]:
            pl.semaphore_signal(barrier, inc=1, device_id=peer, device_id_type=MESH)
        pl.semaphore_wait(barrier, 4)

        def half(a, chip, h):
            hw = srcs[a].shape[1] // 2
            return outs[a].at[chip, :, pl.ds(pl.multiple_of(h * hw, LANES), hw)]

        def copy(a, k, src, chip, h, to):
            return pltpu.make_async_remote_copy(
                src_ref=src, dst_ref=half(a, chip, h), send_sem=send_sems.at[a, k], recv_sem=recv_sems.at[a, k],
                device_id=to, device_id_type=MESH)

        locals_, sends = [], []
        for a in range(n):
            local = pltpu.make_async_copy(srcs[a], outs[a].at[me], local_sems.at[a])
            local.start()
            locals_.append(local)
            hw = srcs[a].shape[1] // 2
            mine = srcs[a].at[:, pl.ds(pl.multiple_of(c * hw, LANES), hw)]
            for k, chip in enumerate(chips):
                cp = copy(a, k, mine, me, c, (*chip, c))
                cp.start()
                sends.append(cp)
        for a in range(n):
            for k, (cx, cy) in enumerate(chips):
                src_chip = 2 * cx + cy
                copy(a, k, half(a, src_chip, c), src_chip, c, (x, y, c)).wait_recv()
                fwd = copy(a, 3 + k, half(a, src_chip, c), src_chip, c, sibling)
                fwd.start()
                sends.append(fwd)
        for a in range(n):
            for k, (cx, cy) in enumerate(chips):
                src_chip = 2 * cx + cy
                copy(a, 3 + k, half(a, src_chip, 1 - c), src_chip, 1 - c, (x, y, c)).wait_recv()
        for cp in sends:
            cp.wait_send()
        for cp in locals_:
            cp.wait()

    launch()
    return [o[...] for o in outs]


def _on_sequencer(name, cid, inputs, out_shapes, sem_types, peers, body):
    srcs = [jax.new_ref(a, memory_space=pltpu.MemorySpace.HBM) for a in inputs]
    outs = [jax.empty_ref(sd, memory_space=pltpu.MemorySpace.HBM) for sd in out_shapes]

    @pl.kernel(mesh=plsc.ScalarSubcoreMesh(axis_name="seq", num_cores=1), name=name, scratch_types=tuple(sem_types),
               compiler_params=pltpu.CompilerParams(collective_id=cid))
    def launch(*sems):
        x, y, c, chips = _place()
        barrier = pltpu.get_barrier_semaphore()
        ps = peers(x, y, c, chips)
        for peer in ps:
            pl.semaphore_signal(barrier, inc=1, device_id=peer, device_id_type=MESH)
        pl.semaphore_wait(barrier, len(ps))
        body(srcs, outs, *sems)

    launch()
    return [o[...] for o in outs]


def _sibling_only(x, y, c, chips):
    return [(x, y, 1 - c)]


def _same_core_of_other_chips(x, y, c, chips):
    return [(*chip, c) for chip in chips]


def _swap_halves_to_sibling(name, cid, grads):
    n = len(grads)

    def body(ins, outs, send_sems, recv_sems):
        x, y, c, _ = _place()
        cps = []
        for a in range(n):
            hw = ins[a].shape[2] // 2
            src = ins[a].at[:, :, pl.ds(pl.multiple_of((1 - c) * hw, LANES), hw)]
            cp = pltpu.make_async_remote_copy(src_ref=src, dst_ref=outs[a], send_sem=send_sems.at[a],
                                              recv_sem=recv_sems.at[a], device_id=(x, y, 1 - c), device_id_type=MESH)
            cp.start()
            cps.append(cp)
        for cp in cps:
            cp.wait()

    return _on_sequencer(
        name, cid, grads, [jax.ShapeDtypeStruct(g.shape[:2] + (g.shape[2] // 2,), g.dtype) for g in grads],
        [pltpu.SemaphoreType.DMA((n,)), pltpu.SemaphoreType.DMA((n,))], _sibling_only, body)


def _exchange_quarters(name, cid, parts):
    n = len(parts)

    def body(ins, outs, send_sems, recv_sems):
        x, y, c, chips = _place()
        cps = []
        for a in range(n):
            for k, (cx, cy) in enumerate(chips):
                cp = pltpu.make_async_remote_copy(
                    src_ref=ins[a].at[2 * cx + cy], dst_ref=outs[a].at[k], send_sem=send_sems.at[a, k],
                    recv_sem=recv_sems.at[a, k], device_id=(cx, cy, c), device_id_type=MESH)
                cp.start()
                cps.append(cp)
        for cp in cps:
            cp.wait()

    return _on_sequencer(
        name, cid, parts, [jax.ShapeDtypeStruct((3,) + p_.shape[1:], p_.dtype) for p_ in parts],
        [pltpu.SemaphoreType.DMA((n, 3)), pltpu.SemaphoreType.DMA((n, 3))], _same_core_of_other_chips, body)


def _swap_final_halves(name, cid, halves):
    n = len(halves)

    def body(ins, outs, send_sems, recv_sems):
        x, y, c, _ = _place()
        cps = []
        for a in range(n):
            cp = pltpu.make_async_remote_copy(src_ref=ins[a], dst_ref=outs[a], send_sem=send_sems.at[a],
                                              recv_sem=recv_sems.at[a], device_id=(x, y, 1 - c), device_id_type=MESH)
            cp.start()
            cps.append(cp)
        for cp in cps:
            cp.wait()

    return _on_sequencer(
        name, cid, halves, [jax.ShapeDtypeStruct(h.shape, h.dtype) for h in halves],
        [pltpu.SemaphoreType.DMA((n,)), pltpu.SemaphoreType.DMA((n,))], _sibling_only, body)


def _add_half(name, grad, recv):
    nchip, r, cfull = grad.shape
    hw = cfull // 2
    tile = _pick(r, max(BF16_ROWS, (256 * 1024 // hw) // BF16_ROWS * BF16_ROWS), q=BF16_ROWS)
    c = lax.axis_index("c")

    def body(c_ref, g_ref, r_ref, o_ref):
        o_ref[...] = _bf(g_ref[...] + r_ref[...])

    return pl.pallas_call(
        body, name=name,
        grid_spec=pltpu.PrefetchScalarGridSpec(
            num_scalar_prefetch=1, grid=(nchip, r // tile),
            in_specs=[pl.BlockSpec((1, tile, hw), lambda k, i, cr: (k, i, cr[0])),
                      pl.BlockSpec((1, tile, hw), lambda k, i, cr: (k, i, 0))],
            out_specs=pl.BlockSpec((1, tile, hw), lambda k, i, cr: (k, i, 0))),
        out_shape=jax.ShapeDtypeStruct((nchip, r, hw), BF16), compiler_params=_params("parallel", "parallel"),
    )(c.reshape(1).astype(jnp.int32), grad, recv)


def _add_quarters(name, part, recv):
    _, r, hw = part.shape
    tile = _pick(r, max(BF16_ROWS, (256 * 1024 // hw) // BF16_ROWS * BF16_ROWS), q=BF16_ROWS)
    me = 2 * lax.axis_index("x") + lax.axis_index("y")

    def body(me_ref, p_ref, r_ref, o_ref):
        f = lambda v: v.astype(F32)
        o_ref[...] = ((f(p_ref[0]) + f(r_ref[0])) + f(r_ref[1])) + f(r_ref[2])

    return pl.pallas_call(
        body, name=name,
        grid_spec=pltpu.PrefetchScalarGridSpec(
            num_scalar_prefetch=1, grid=(r // tile,),
            in_specs=[pl.BlockSpec((1, tile, hw), lambda i, mr: (mr[0], i, 0)),
                      pl.BlockSpec((3, tile, hw), lambda i, mr: (0, i, 0))],
            out_specs=pl.BlockSpec((tile, hw), lambda i, mr: (i, 0))),
        out_shape=jax.ShapeDtypeStruct((r, hw), F32), compiler_params=_params("parallel"),
    )(me.reshape(1).astype(jnp.int32), part, recv)


class _ReduceScatter:
    def __init__(self, tag, first_cid, grads):
        self.tag, self.cid = tag, first_cid
        self.stacks = [g.reshape(N_CHIPS, g.shape[0] // N_CHIPS, g.shape[1]) for g in grads]

    def start(self):
        self.recv = _swap_halves_to_sibling(f"rs_swap_halves_{self.tag}", self.cid, self.stacks)

    def exchange(self):
        self.parts = [_add_half(f"rs_add_half_{self.tag}{a}", g, r)
                      for a, (g, r) in enumerate(zip(self.stacks, self.recv))]
        self.quarters = _exchange_quarters(f"rs_exchange_{self.tag}", self.cid + 1, self.parts)

    def join(self):
        self.halves = [_add_quarters(f"rs_add_quarters_{self.tag}{a}", p_, q_)
                       for a, (p_, q_) in enumerate(zip(self.parts, self.quarters))]
        self.others = _swap_final_halves(f"rs_swap_final_{self.tag}", self.cid + 2, self.halves)

    def result(self):
        south = lax.axis_index("c") == 0
        return [jnp.concatenate([jnp.where(south, h, o), jnp.where(south, o, h)], axis=1)
                for h, o in zip(self.halves, self.others)]


def _allgather_small(pack):
    m_per, n = pack.shape

    def body(x_ref, out_ref, send_sems, recv_sems, local_sem):
        x, y, c, chips = _place()
        me, sibling = (x, y, c), (x, y, 1 - c)

        def rows(px, py, pc):
            return out_ref.at[pl.ds(pl.multiple_of((4 * px + 2 * py + pc) * m_per, SUBLANES), m_per), :]

        def copy(k, block, to, src=None):
            return pltpu.make_async_remote_copy(
                src_ref=rows(*block) if src is None else src, dst_ref=rows(*block),
                send_sem=send_sems.at[k], recv_sem=recv_sems.at[k], device_id=to, device_id_type=MESH)

        mine = pltpu.make_async_copy(x_ref, rows(*me), local_sem)
        mine.start()
        first = [copy(0, me, sibling, src=x_ref)]
        first += [copy(1 + j, me, (*chip, c), src=x_ref) for j, chip in enumerate(chips)]
        for cp in first:
            cp.start()
        passed = [copy(4 + j, (*chip, c), sibling) for j, chip in enumerate(chips)]
        for j, chip in enumerate(chips):
            copy(1 + j, (*chip, c), me).wait_recv()
            passed[j].start()
        copy(0, sibling, me).wait_recv()
        for j, chip in enumerate(chips):
            copy(4 + j, (*chip, 1 - c), me).wait_recv()
        for cp in first + passed:
            cp.wait_send()
        mine.wait()

    return pl.pallas_call(
        body, name="allgather_small_grads",
        out_shape=jax.ShapeDtypeStruct((N_DEV * m_per, n), pack.dtype),
        in_specs=[pl.BlockSpec(memory_space=pltpu.VMEM)], out_specs=pl.BlockSpec(memory_space=pltpu.VMEM),
        scratch_shapes=[pltpu.SemaphoreType.DMA((7,)), pltpu.SemaphoreType.DMA((7,)), pltpu.SemaphoreType.DMA],
        compiler_params=pltpu.CompilerParams(vmem_limit_bytes=VMEM_LIMIT),
    )(pack)


def _sum_devices(packs, m_per):
    tile = _pick(m_per, 512, q=SUBLANES)
    nt = m_per // tile

    def body(*refs):
        acc = refs[0][...]
        for r in refs[1:N_DEV]:
            acc = acc + r[...]
        refs[N_DEV][...] = acc

    return pl.pallas_call(
        body, name="sum_small_grads", grid=(nt,),
        in_specs=[pl.BlockSpec((tile, LANES), functools.partial(lambda i, k: (k * nt + i, 0), k=k)) for k in range(N_DEV)],
        out_specs=_nat(tile, LANES), out_shape=jax.ShapeDtypeStruct((m_per, LANES), F32),
        compiler_params=_params("parallel"),
    )(*([packs] * N_DEV))


def _tail_fwd(tag, alpha, h_in, adds, mix_gate, ln1, ln2, p_l, w, want_perm):
    h_mid, xh1, rs1, h_mid_b, _ = _ln_fwd(f"ln1_fwd_{tag}", alpha, h_in, adds, mix_gate, *ln1)
    gp = _matmul(f"ple_gate_fwd_{tag}", h_mid_b, w['wg'], 'nn')
    pw = _matmul(f"ple_proj_fwd_{tag}", p_l, w['plet'], 'nt')
    gu = _matmul(f"ffn_in_fwd_{tag}", h_mid_b, w['wit'], 'nt', tn=1408)
    act = _swiglu_fwd(f"swiglu_fwd_{tag}", gu)
    ffn = _matmul(f"ffn_out_fwd_{tag}", act, w['wo'], 'nn', tk=2816)
    h_out, xh2, rs2, _, h_perm = _ln_fwd(f"ln2_fwd_{tag}", alpha, h_mid, [(ffn, 'nat')],
                                         ('nat', (pw, 1, 0), (gp, 1, 0)), *ln2, want_perm=want_perm)
    saved = dict(h_mid_b=h_mid_b, xh1=xh1, rs1=rs1, gp=gp, pw=pw, gu=gu, act=act, xh2=xh2, rs2=rs2)
    return h_out, h_perm, saved


def _tail_bwd(tag, alpha, dparts, sv, ln1_g, ln2_g, p_l, w, mix_gate):
    d = sv['h_mid_b'].shape[1]
    dz2, dz2b, dgate, dg2, db2 = _ln_bwd(f"ln2_bwd_{tag}", dparts, sv['xh2'], sv['rs2'], ln2_g,
                                         gate=('nat', (sv['pw'], 1, 0), (sv['gp'], 1, 0)))
    grads = dict(ln2_g=dg2, ln2_b=db2)
    grads['plet'] = _matmul(f"ple_proj_dw_{tag}", dgate, p_l, 'tn', a_win=(0, d))
    grads['wg'] = _matmul(f"ple_gate_dw_{tag}", sv['h_mid_b'], dgate, 'tn', b_win=(d, d))
    dx_gate = _matmul(f"ple_gate_dx_{tag}", dgate, w['wg'], 'nt', a_win=(d, d))
    dact = _matmul(f"ffn_out_dx_{tag}", dz2b, w['wo'], 'nt', out_dtype=BF16, tn=1408)
    grads['wo'] = _matmul(f"ffn_out_dw_{tag}", sv['act'], dz2b, 'tn', tm=1408)
    dgu = _swiglu_bwd(f"swiglu_bwd_{tag}", sv['gu'], dact)
    grads['wit'] = _matmul(f"ffn_in_dw_{tag}", dgu, sv['h_mid_b'], 'tn')
    dx_ffn = _matmul(f"ffn_in_dx_{tag}", dgu, w['wit'], 'nn', tk=2816)
    res = _ln_bwd(f"ln1_bwd_{tag}", [(dz2, 'nat', alpha), (dx_gate, 'nat', 1.0), (dx_ffn, 'nat', 1.0)],
                  sv['xh1'], sv['rs1'], ln1_g, gate=mix_gate)
    grads['ln1_g'], grads['ln1_b'] = res[-2], res[-1]
    return res[:-2], grads


def kernel(x, p, positions, attn_w_in, mla_q_norm, mla_w_q_b, mla_kv_norm, mla_w_kv_b, attn_w_out, s5_a_re, s5_a_im, s5_log_dt, s5_b_re, s5_b_im, s5_c_re, s5_c_im, s5_d, s5_w_glu, ln1_g, ln1_b, ffn_w_in, ffn_w_out, ple_w, ple_gate_w, ln2_g, ln2_b, loss_target, m_attn_w_in, m_mla_q_norm, m_mla_w_q_b, m_mla_kv_norm, m_mla_w_kv_b, m_attn_w_out, m_s5_a_re, m_s5_a_im, m_s5_log_dt, m_s5_b_re, m_s5_b_im, m_s5_c_re, m_s5_c_im, m_s5_d, m_s5_w_glu, m_ln1_g, m_ln1_b, m_ffn_w_in, m_ffn_w_out, m_ple_w, m_ple_gate_w, m_ln2_g, m_ln2_b, v_attn_w_in, v_mla_q_norm, v_mla_w_q_b, v_mla_kv_norm, v_mla_w_kv_b, v_attn_w_out, v_s5_a_re, v_s5_a_im, v_s5_log_dt, v_s5_b_re, v_s5_b_im, v_s5_c_re, v_s5_c_im, v_s5_d, v_s5_w_glu, v_ln1_g, v_ln1_b, v_ffn_w_in, v_ffn_w_out, v_ple_w, v_ple_gate_w, v_ln2_g, v_ln2_b):
    weights = dict(attn_w_in=attn_w_in, mla_q_norm=mla_q_norm, mla_w_q_b=mla_w_q_b, mla_kv_norm=mla_kv_norm,
                   mla_w_kv_b=mla_w_kv_b, attn_w_out=attn_w_out, s5_a_re=s5_a_re, s5_a_im=s5_a_im, s5_log_dt=s5_log_dt,
                   s5_b_re=s5_b_re, s5_b_im=s5_b_im, s5_c_re=s5_c_re, s5_c_im=s5_c_im, s5_d=s5_d, s5_w_glu=s5_w_glu,
                   ln1_g=ln1_g, ln1_b=ln1_b, ffn_w_in=ffn_w_in, ffn_w_out=ffn_w_out, ple_w=ple_w, ple_gate_w=ple_gate_w,
                   ln2_g=ln2_g, ln2_b=ln2_b)
    m_in = dict(attn_w_in=m_attn_w_in, mla_q_norm=m_mla_q_norm, mla_w_q_b=m_mla_w_q_b, mla_kv_norm=m_mla_kv_norm,
                mla_w_kv_b=m_mla_w_kv_b, attn_w_out=m_attn_w_out, s5_a_re=m_s5_a_re, s5_a_im=m_s5_a_im,
                s5_log_dt=m_s5_log_dt, s5_b_re=m_s5_b_re, s5_b_im=m_s5_b_im, s5_c_re=m_s5_c_re, s5_c_im=m_s5_c_im,
                s5_d=m_s5_d, s5_w_glu=m_s5_w_glu, ln1_g=m_ln1_g, ln1_b=m_ln1_b, ffn_w_in=m_ffn_w_in,
                ffn_w_out=m_ffn_w_out, ple_w=m_ple_w, ple_gate_w=m_ple_gate_w, ln2_g=m_ln2_g, ln2_b=m_ln2_b)
    v_in = dict(attn_w_in=v_attn_w_in, mla_q_norm=v_mla_q_norm, mla_w_q_b=v_mla_w_q_b, mla_kv_norm=v_mla_kv_norm,
                mla_w_kv_b=v_mla_w_kv_b, attn_w_out=v_attn_w_out, s5_a_re=v_s5_a_re, s5_a_im=v_s5_a_im,
                s5_log_dt=v_s5_log_dt, s5_b_re=v_s5_b_re, s5_b_im=v_s5_b_im, s5_c_re=v_s5_c_re, s5_c_im=v_s5_c_im,
                s5_d=v_s5_d, s5_w_glu=v_s5_w_glu, ln1_g=v_ln1_g, ln1_b=v_ln1_b, ffn_w_in=v_ffn_w_in,
                ffn_w_out=v_ffn_w_out, ple_w=v_ple_w, ple_gate_w=v_ple_gate_w, ln2_g=v_ln2_g, ln2_b=v_ln2_b)
    names = list(weights)

    s, d = x.shape[1], x.shape[2]
    depth = ln1_g.shape[0]
    assert depth == 2
    alpha = (2.0 * depth) ** 0.25
    ql, kvl = mla_q_norm.shape[1], mla_kv_norm.shape[1]
    in_cols = N_CHIPS * attn_w_in.shape[2]
    heads = N_CHIPS * mla_w_q_b.shape[2] // (NOPE + ROPE)
    hps = heads // N_CHIPS
    dw = (in_cols - ql - kvl - ROPE) // 3
    dh = dw // DHD
    assert ql % LANES == 0 and kvl == ql and dw % DHD == 0 and heads % N_CHIPS == 0
    ngroups, nstate = s5_a_re.shape[1], s5_a_re.shape[2]
    assert nstate == S5_STATE and ngroups * S5_GROUP == d and d % LANES == 0
    nsl = d // LANES
    seg_len = s // SUBLANES
    n_sq = seg_len.bit_length() - 1
    assert 1 << n_sq == seg_len, "the segment length of the S5 scan must be a power of two"
    for window, dil in DIL_BRANCHES:
        assert window // dil == DIL_STEPS and (s // dil) % DIL_STEPS == 0
    me = 2 * lax.axis_index("x") + lax.axis_index("y")

    xb = x[0]
    target = loss_target[0]
    p_layers = [p[0, 0], p[1, 0]]
    pos = positions[0].astype(F32).reshape(s, 1)
    inv_freq = ROPE_THETA ** (-jnp.arange(ROPE // 2, dtype=F32) / (ROPE // 2))
    invf = jnp.concatenate([inv_freq, inv_freq, jnp.zeros((LANES - ROPE,), F32)]).reshape(1, LANES)
    slopes = 2.0 ** (-8.0 * jnp.arange(1, dh + 1, dtype=F32) / dh)
    slopes = jnp.broadcast_to(jnp.repeat(slopes, SUBLANES)[:, None], (dh * SUBLANES, LANES))

    wqb_t = mla_w_q_b[0].T.reshape(hps, NOPE + ROPE, ql)
    wqb_t = jnp.pad(wqb_t, ((0, 0), (0, QK_PAD - NOPE - ROPE), (0, 0))).reshape(hps * QK_PAD, ql)
    d_cols = max(d // N_CHIPS, 2 * LANES)
    d_pad = jnp.zeros((SUBLANES, d_cols), F32).at[0, :d // N_CHIPS].set(s5_d[0])
    shards = [_bf(attn_w_in[0].T), _bf(wqb_t), _bf(mla_w_kv_b[0].T), _bf(attn_w_out[0]), _bf(s5_w_glu[0].T)]
    for l in range(depth):
        shards += [_bf(ffn_w_in[l].T), _bf(ffn_w_out[l]), _bf(ple_w[l].T), _bf(ple_gate_w[l])]
    shards.append(d_pad)
    gathered = list(_gather_weights(shards[:3])) + _gather_weights_async(shards[3:])
    full = [g.reshape(N_CHIPS * g.shape[1], g.shape[2]) for g in gathered]
    win_t, wqb_t_f, wkv_t, wout, wglu_t = full[:5]
    lw = [dict(wit=full[5 + 4 * l], wo=full[6 + 4 * l], plet=full[7 + 4 * l], wg=full[8 + 4 * l]) for l in range(depth)]
    dvec = full[-1].reshape(N_CHIPS, SUBLANES, d_cols)[:, 0, :d // N_CHIPS].reshape(1, d)
    lat = ql + kvl
    win_t = jnp.concatenate([win_t[:lat + ROPE], jnp.zeros((LANES - ROPE, d), BF16), win_t[lat + ROPE:]], axis=0)
    kpe_cb = lat // LANES
    q_cb = kpe_cb + 1
    a_cb = heads * VDIM // LANES

    xbb = _bf(xb)
    proj = _matmul("attn_in_fwd", xbb, win_t, 'nt', tn=1408)
    nrm = _rms_fwd(proj, ql, kvl, mla_q_norm[0], mla_kv_norm[0])
    q_raw = _matmul("mla_q_up_fwd", nrm, wqb_t_f, 'nt', a_win=(0, ql))
    kv = _matmul("mla_kv_up_fwd", nrm, wkv_t, 'nt', a_win=(ql, kvl))
    qf, kf, vv = _rope_prep(q_raw, kv, proj, kpe_cb, pos, invf, heads)
    out_a, lse_a = _mla_fwd(qf, kf, vv.T, heads)
    outs, lses = [], []
    for window, dil in DIL_BRANCHES:
        o_g, l_g = _dil_fwd(f"dil_fwd_d{dil}", proj, slopes, dil, dh, q_cb)
        outs.append(o_g)
        lses.append(l_g)
    att, out_b, lse_b = _dil_merge(out_a, outs, lses)
    mix0 = _matmul("attn_out_fwd", att, wout, 'nn')
    h2, h2p, sv0 = _tail_fwd("l0", alpha, xb, [(mix0, 'nat')], None, (ln1_g[0], ln1_b[0]), (ln2_g[0], ln2_b[0]),
                             p_layers[0], lw[0], want_perm=True)

    rep = lambda a: jnp.repeat(a, S5_GROUP, axis=0)
    ag = (s5_a_re[0], s5_a_im[0], jnp.broadcast_to(s5_log_dt[0][:, None], (ngroups, nstate)))
    a16 = tuple(rep(a) for a in ag)
    b16 = tuple(b[0].transpose(0, 2, 1).reshape(ngroups * S5_GROUP, nstate) for b in (s5_b_re, s5_b_im))
    abr, abi, apr, api, bbr, bbi = _s5_discretise(*a16, *b16, n_sq)
    ab_tile = _slab_tile(abr[::S5_GROUP], abi[::S5_GROUP], nsl)
    ap_tile = _slab_tile(apr[::S5_GROUP], api[::S5_GROUP], nsl)
    bblk = _bf(_slab_in_matrix(bbr.reshape(ngroups, S5_GROUP, nstate), bbi.reshape(ngroups, S5_GROUP, nstate), nsl))
    cblk = _bf(_slab_out_matrix(s5_c_re[0], s5_c_im[0], nsl))
    xloc, ends = _s5_pass1(h2p, bblk, ab_tile)
    cinx = _s5_carry("s5_carry_fwd", ends, ap_tile, False)
    xtrue, ypre, zg = _s5_pass2(xloc, cinx, ab_tile, cblk, h2p, dvec)
    vg = _matmul("s5_glu_fwd", zg, wglu_t, 'nt')
    glu_gate = ('perm', (vg, 2, 0), (vg, 2, 1))
    h4, _, sv1 = _tail_fwd("l1", alpha, h2, [], glu_gate, (ln1_g[1], ln1_b[1]), (ln2_g[1], ln2_b[1]),
                           p_layers[1], lw[1], want_perm=False)
    loss = lax.psum(jnp.sum(_loss_partial(h4, target)), ("x", "y", "c"))

    (dz1_1, _, dvg), g1 = _tail_bwd("l1", alpha, [(h4, 'nat', 1.0 / d), (target, 'nat', -1.0 / d)], sv1, ln1_g[1],
                                    ln2_g[1], p_layers[1], lw[1], glu_gate)
    d_wglu_t = _matmul("s5_glu_dw", dvg, zg, 'tn')
    dzg = _matmul("s5_glu_dx", dvg, wglu_t, 'nn')
    rs_l1 = _ReduceScatter("l1", 2, [d_wglu_t, g1['wit'], g1['wo'], g1['plet'], g1['wg']])
    rs_l1.start()
    lamloc, starts, dy, dd = _s5_bwd_pass1(dzg, ypre, cblk, ab_tile, h2p)
    cinl = _s5_carry("s5_carry_bwd", starts, ap_tile, True)
    du_p, d_bblk, d_cblk, d_ab = _s5_bwd_pass2(lamloc, cinl, ab_tile, xtrue, cinx, h2p, dy, bblk, dvec)
    gbb = _slab_in_extract(d_bblk, nsl)
    g_c_re, g_c_im = _slab_out_extract(d_cblk, nsl)
    d_ab = d_ab[::SUBLANES]
    gab = (d_ab[:, :SLAB_COLS].reshape(ngroups, nstate), d_ab[:, SLAB_COLS:].reshape(ngroups, nstate))
    g_a_re, g_a_im, g_log_dt, g_b_re, g_b_im = _s5_discretise_bwd(a16, b16, ag, gab, gbb)
    unt = lambda b: b.reshape(ngroups, S5_GROUP, nstate).transpose(0, 2, 1)

    rs_l1.exchange()
    (dz1_0, dz1_0b), g0 = _tail_bwd("l0", alpha, [(dz1_1, 'nat', alpha), (du_p, 'perm', 1.0)], sv0, ln1_g[0], ln2_g[0],
                                    p_layers[0], lw[0], None)
    rs_l1.join()
    rs_l0 = _ReduceScatter("l0", 5, [g0['wit'], g0['wo'], g0['plet'], g0['wg']])
    rs_l0.start()
    datt = _matmul("attn_out_dx", dz1_0b, wout, 'nt')
    d_wout = _matmul("attn_out_dw", att, dz1_0b, 'tn')
    do, delta, delta_t = _attn_bwd_prep(datt, out_a, out_b)
    dqf, dkf, dvv = _mla_bwd(qf, kf, vv, do, lse_a, delta_t, heads, 0)
    rs_l0.exchange()
    dq_raw, dkv, dkpe = _rope_unprep(dqf, dkf, dvv, pos, invf, heads)
    d_wqb_t = _matmul("mla_q_up_dw", dq_raw, nrm, 'tn', b_win=(0, ql))
    d_wkv_t = _matmul("mla_kv_up_dw", dkv, nrm, 'tn', b_win=(ql, kvl))
    dnq = _matmul("mla_q_up_dx", dq_raw, wqb_t_f, 'nn')
    dnkv = _matmul("mla_kv_up_dx", dkv, wkv_t, 'nn')
    dqs, dks, dvs = [], [], []
    for window, dil in DIL_BRANCHES:
        dqs.append(_dil_bwd_dq(f"dil_bwd_dq_d{dil}", proj, slopes, do, lse_b, delta, dil, dh, q_cb, a_cb))
        dk_g, dv_g = _dil_bwd_dkv(f"dil_bwd_dkv_d{dil}", proj, slopes, do, lse_b, delta, dil, dh, q_cb, a_cb)
        dks.append(dk_g)
        dvs.append(dv_g)
    rs_l0.join()
    dproj, g_gq, g_gkv = _dproj_assemble(proj, dnq, dnkv, dkpe, dqs, dks, dvs, mla_q_norm[0], mla_kv_norm[0], ql)
    d_win_t = _matmul("attn_in_dw", dproj, xbb, 'tn', tm=1408)
    dx_attn = _matmul("attn_in_dx", dproj, win_t, 'nn')
    grad_x = _axpy("grad_x", alpha, dz1_0, dx_attn)

    d_win_t = jnp.concatenate([d_win_t[:lat + ROPE], d_win_t[lat + LANES:]], axis=0)
    rs_at = _ReduceScatter("attn", 8, [d_win_t, d_wqb_t, d_wkv_t, d_wout])
    rs_at.start()
    rs_at.exchange()
    rs_at.join()
    r_win, r_wqb, r_wkv, r_wout = rs_at.result()
    r_wglu, r_wit1, r_wo1, r_plet1, r_wg1 = rs_l1.result()
    r_wit0, r_wo0, r_plet0, r_wg0 = rs_l0.result()
    r_wqb = r_wqb.reshape(hps, QK_PAD, ql)[:, :NOPE + ROPE].reshape(hps * (NOPE + ROPE), ql)
    grads = dict(attn_w_in=r_win.T[None], mla_w_q_b=r_wqb.T[None], mla_w_kv_b=r_wkv.T[None], attn_w_out=r_wout[None],
                 s5_w_glu=r_wglu.T[None],
                 ffn_w_in=jnp.stack([r_wit0.T, r_wit1.T]), ffn_w_out=jnp.stack([r_wo0, r_wo1]),
                 ple_w=jnp.stack([r_plet0.T, r_plet1.T]), ple_gate_w=jnp.stack([r_wg0, r_wg1]))

    small = dict(mla_q_norm=g_gq, mla_kv_norm=g_gkv, s5_a_re=g_a_re, s5_a_im=g_a_im, s5_log_dt=g_log_dt,
                 s5_b_re=unt(g_b_re), s5_b_im=unt(g_b_im), s5_c_re=g_c_re, s5_c_im=g_c_im, s5_d=dd[0],
                 ln1_g=jnp.stack([g0['ln1_g'], g1['ln1_g']]), ln1_b=jnp.stack([g0['ln1_b'], g1['ln1_b']]),
                 ln2_g=jnp.stack([g0['ln2_g'], g1['ln2_g']]), ln2_b=jnp.stack([g0['ln2_b'], g1['ln2_b']]))
    flat = jnp.concatenate([v_.reshape(-1) for v_ in small.values()])
    m_per = -(-flat.shape[0] // (LANES * SUBLANES)) * SUBLANES
    pack = jnp.pad(flat, (0, m_per * LANES - flat.shape[0])).reshape(m_per, LANES)
    total = _sum_devices(_allgather_small(pack), m_per).reshape(-1)
    off = 0
    for k_, v_ in small.items():
        n_ = v_.size
        piece = total[off:off + n_]
        off += n_
        if k_ == 's5_d':
            grads[k_] = lax.dynamic_slice(piece, (me * (d // N_CHIPS),), (d // N_CHIPS,)).reshape(weights[k_].shape)
        else:
            grads[k_] = piece.reshape(weights[k_].shape)

    deltas, new_m, new_v = {}, {}, {}
    for k_ in names:
        w_ = weights[k_]
        shape = w_.shape
        if w_.ndim == 3 and w_.shape[-1] >= LANES:
            two_d = (shape[0] * shape[1], shape[2])
        elif w_.ndim == 4:
            two_d = (shape[0] * shape[1], shape[2] * shape[3])
        else:
            two_d = (1, w_.size) if w_.ndim == 2 and shape[0] == 1 else (shape[0], w_.size // shape[0])
        dl, nm, nv = _adamw(f"adamw_{k_}", w_.reshape(two_d), grads[k_].reshape(two_d), m_in[k_].reshape(two_d),
                            v_in[k_].reshape(two_d))
        deltas[k_], new_m[k_], new_v[k_] = dl.reshape(shape), nm.reshape(shape), nv.reshape(shape)

    return (loss, grad_x[None], *[grads[k_] for k_ in names], *[deltas[k_] for k_ in names],
            *[new_m[k_] for k_ in names], *[new_v[k_] for k_ in names])
```

```python
import functools
import math

import jax
import jax.numpy as jnp
from jax import lax
from jax.experimental import pallas as pl
from jax.experimental.pallas import tpu as pltpu
from jax.experimental.pallas import tpu_sc as plsc

F32 = jnp.float32
BF16 = jnp.bfloat16
MESH = pl.DeviceIdType.MESH

LANES = 128
SUBLANES = 8
BF16_ROWS = 16
VMEM_LIMIT = 48 * 2 ** 20
N_CHIPS = 4
N_DEV = 8

NOPE = 128
ROPE = 64
VDIM = 128
QK_PAD = 256
DHD = 128
DIL_STEPS = 128
DIL_BRANCHES = ((128, 1), (512, 4), (2048, 16))
ROPE_THETA = 10000.0
S5_GROUP = 16
S5_STATE = 64
SLAB_GROUPS = LANES // S5_GROUP
SLAB_COLS = SLAB_GROUPS * S5_STATE
NEG = -1e30
LN_EPS = 1e-5
RMS_EPS = 1e-6

ADAM_LR = 0.001
ADAM_B1 = 0.9
ADAM_B2 = 0.999
ADAM_EPS = 1e-08
ADAM_WD = 0.01
ADAM_STEP = 10

NN = ((1,), (0,))
NT = ((1,), (1,))
TN = ((0,), (0,))


def _dot(a, b, dims):
    return lax.dot_general(a, b, (dims, ((), ())), preferred_element_type=F32)


def _bf(v):
    return v.astype(BF16)


def _pick(n, target, q=LANES, also=0):
    g = math.gcd(n, also) if also else n
    if g <= target and g == n:
        return n
    best = None
    for t in range(q, min(g, target) + 1, q):
        if g % t == 0:
            best = t
    assert best is not None, (n, target, q, also)
    return best


def _params(*sem):
    return pltpu.CompilerParams(dimension_semantics=sem, vmem_limit_bytes=VMEM_LIMIT)


def _sigmoid(v):
    return 1.0 / (1.0 + jnp.exp(-v))


def _matmul(name, a, b, form, out_dtype=F32, a_win=None, b_win=None, tm=1024, tn=1024, tk=2048):
    c0, aw = a_win if a_win else (0, a.shape[1])
    if form == 'nt':
        assert b_win is None
        n, kdim = b.shape
        d0 = 0
    else:
        kdim = b.shape[0]
        d0, n = b_win if b_win else (0, b.shape[1])
    if form == 'tn':
        m = aw
        assert a.shape[0] == kdim, (name, a.shape, b.shape)
        tm = _pick(m, tm, also=c0)
        tk = _pick(kdim, tk)
        a_off = c0 // tm
    else:
        m = a.shape[0]
        assert aw == kdim, (name, a.shape, b.shape, a_win)
        tm = _pick(m, tm)
        tk = _pick(kdim, tk, also=c0)
        a_off = c0 // tk
    tn = _pick(n, tn, also=d0)
    b_off = d0 // tn
    nk = kdim // tk
    dims = {'nn': NN, 'nt': NT, 'tn': TN}[form]

    def body(a_ref, b_ref, o_ref, *acc):
        prod = _dot(_bf(a_ref[...]), _bf(b_ref[...]), dims)
        if nk == 1:
            o_ref[...] = prod.astype(o_ref.dtype)
            return
        acc_ref, = acc
        k = pl.program_id(2)

        @pl.when(k == 0)
        def _():
            acc_ref[...] = prod

        @pl.when((k > 0) & (k < nk - 1))
        def _():
            acc_ref[...] += prod

        @pl.when(k == nk - 1)
        def _():
            o_ref[...] = (acc_ref[...] + prod).astype(o_ref.dtype)

    if form == 'tn':
        a_spec = pl.BlockSpec((tk, tm), lambda i, j, k: (k, i + a_off))
    else:
        a_spec = pl.BlockSpec((tm, tk), lambda i, j, k: (i, k + a_off))
    if form == 'nt':
        b_spec = pl.BlockSpec((tn, tk), lambda i, j, k: (j, k))
    else:
        b_spec = pl.BlockSpec((tk, tn), lambda i, j, k: (k, j + b_off))
    return pl.pallas_call(
        body, name=name,
        grid=(m // tm, n // tn, nk),
        in_specs=[a_spec, b_spec],
        out_specs=pl.BlockSpec((tm, tn), lambda i, j, k: (i, j)),
        out_shape=jax.ShapeDtypeStruct((m, n), out_dtype),
        scratch_shapes=[pltpu.VMEM((tm, tn), F32)] if nk > 1 else [],
        compiler_params=_params("parallel", "parallel", "arbitrary"),
    )(a, b)


def _nat(tile, width, cb=0):
    return pl.BlockSpec((tile, width), lambda i: (i, cb))


def _perm(tile, width, seg_tiles, ncb=1, cb=0):
    return pl.BlockSpec((tile, width), lambda i: (i % seg_tiles, (i // seg_tiles) * ncb + cb))


def _whole(shape):
    return pl.BlockSpec(shape, lambda i: (0,) * len(shape))


def _perm_view(a):
    s, w = a.shape
    return a.reshape(s // SUBLANES, SUBLANES * w)


def _row_spec(a, layout, tile, width, ncb=1, cb=0):
    if layout == 'nat':
        return a, _nat(tile, width, cb)
    seg_tiles = a.shape[0] // SUBLANES // tile
    return _perm_view(a), _perm(tile, width, seg_tiles, ncb, cb)


def _ln_fwd(name, alpha, a, adds, gate, g, b, want_perm=False, tile=256):
    s, d = a.shape
    n_add = len(adds)
    has_gate = gate is not None

    def body(*refs):
        a_ref = refs[0]
        add_refs = refs[1:1 + n_add]
        pos = 1 + n_add
        if has_gate:
            val_ref, pre_ref = refs[pos], refs[pos + 1]
            pos += 2
        g_ref, b_ref = refs[pos], refs[pos + 1]
        outs = refs[pos + 2:]
        z = alpha * a_ref[...]
        for r in add_refs:
            z = z + r[...]
        if has_gate:
            z = z + val_ref[...] * _sigmoid(pre_ref[...])
        mu = jnp.mean(z, axis=-1, keepdims=True)
        zc = z - mu
        var = jnp.mean(zc * zc, axis=-1, keepdims=True)
        rstd = lax.rsqrt(var + LN_EPS)
        xhat = zc * rstd
        h = xhat * g_ref[...] + b_ref[...]
        outs[0][...] = h
        outs[1][...] = xhat
        outs[2][...] = jnp.broadcast_to(rstd, (tile, LANES))
        outs[3][...] = _bf(h)
        if want_perm:
            outs[4][...] = h

    ins, specs = [a], [_nat(tile, d)]
    for arr, layout in adds:
        x_, sp = _row_spec(arr, layout, tile, d)
        ins.append(x_)
        specs.append(sp)
    if has_gate:
        layout = gate[0]
        for arr, ncb, cb in gate[1:]:
            x_, sp = _row_spec(arr, layout, tile, d, ncb=ncb, cb=cb)
            ins.append(x_)
            specs.append(sp)
    ins += [g.reshape(1, d), b.reshape(1, d)]
    specs += [_whole((1, d)), _whole((1, d))]
    out_shape = [jax.ShapeDtypeStruct((s, d), F32), jax.ShapeDtypeStruct((s, d), F32),
                 jax.ShapeDtypeStruct((s, LANES), F32), jax.ShapeDtypeStruct((s, d), BF16)]
    out_specs = [_nat(tile, d), _nat(tile, d), _nat(tile, LANES), _nat(tile, d)]
    if want_perm:
        seg_tiles = s // SUBLANES // tile
        out_shape.append(jax.ShapeDtypeStruct((s // SUBLANES, SUBLANES * d), F32))
        out_specs.append(_perm(tile, d, seg_tiles))
    res = pl.pallas_call(
        body, name=name, grid=(s // tile,), in_specs=specs, out_specs=out_specs, out_shape=out_shape,
        compiler_params=_params("parallel"),
    )(*ins)
    return res[0], res[1], res[2], res[3], (res[4].reshape(s, d) if want_perm else None)


def _ln_bwd(name, dparts, xhat, rstd, g, gate=None, tile=256):
    s, d = xhat.shape
    n_part = len(dparts)
    coefs = [c for _, _, c in dparts]
    has_gate = gate is not None

    def body(*refs):
        part_refs = refs[:n_part]
        xhat_ref, rstd_ref, g_ref = refs[n_part:n_part + 3]
        pos = n_part + 3
        if has_gate:
            val_ref, pre_ref = refs[pos], refs[pos + 1]
            pos += 2
        outs = list(refs[pos:])
        dz_ref = outs.pop(0)
        dzb_ref = outs.pop(0)
        dgate_ref = outs.pop(0) if has_gate else None
        dg_ref, db_ref = outs
        dh = coefs[0] * part_refs[0][...]
        for c, r in zip(coefs[1:], part_refs[1:]):
            dh = dh + c * r[...]
        xh = xhat_ref[...]
        dxh = dh * g_ref[...]
        m1 = jnp.mean(dxh, axis=-1, keepdims=True)
        m2 = jnp.mean(dxh * xh, axis=-1, keepdims=True)
        dz = rstd_ref[:, 0:1] * (dxh - m1 - xh * m2)
        dz_ref[...] = dz
        dzb_ref[...] = _bf(dz)
        if has_gate:
            sg = _sigmoid(pre_ref[...])
            dval = dz * sg
            dpre = dz * val_ref[...] * sg * (1.0 - sg)
            dgate_ref[...] = jnp.concatenate([_bf(dval), _bf(dpre)], axis=1)

        @pl.when(pl.program_id(0) == 0)
        def _():
            dg_ref[...] = jnp.zeros_like(dg_ref)
            db_ref[...] = jnp.zeros_like(db_ref)

        dg_ref[0:1, :] += jnp.sum(dh * xh, axis=0, keepdims=True)
        db_ref[0:1, :] += jnp.sum(dh, axis=0, keepdims=True)

    ins, specs = [], []
    for arr, layout, _ in dparts:
        x_, sp = _row_spec(arr, layout, tile, d)
        ins.append(x_)
        specs.append(sp)
    ins += [xhat, rstd, g.reshape(1, d)]
    specs += [_nat(tile, d), _nat(tile, LANES), _whole((1, d))]
    gate_layout = None
    if has_gate:
        gate_layout = gate[0]
        for arr, ncb, cb in gate[1:]:
            x_, sp = _row_spec(arr, gate_layout, tile, d, ncb=ncb, cb=cb)
            ins.append(x_)
            specs.append(sp)
    seg_tiles = s // SUBLANES // tile
    out_shape = [jax.ShapeDtypeStruct((s, d), F32), jax.ShapeDtypeStruct((s, d), BF16)]
    out_specs = [_nat(tile, d), _nat(tile, d)]
    if has_gate:
        if gate_layout == 'nat':
            out_shape.append(jax.ShapeDtypeStruct((s, 2 * d), BF16))
            out_specs.append(_nat(tile, 2 * d))
        else:
            out_shape.append(jax.ShapeDtypeStruct((s // SUBLANES, SUBLANES * 2 * d), BF16))
            out_specs.append(_perm(tile, 2 * d, seg_tiles))
    out_shape += [jax.ShapeDtypeStruct((SUBLANES, d), F32)] * 2
    out_specs += [_whole((SUBLANES, d))] * 2
    res = list(pl.pallas_call(
        body, name=name, grid=(s // tile,), in_specs=specs, out_specs=out_specs, out_shape=out_shape,
        compiler_params=_params("arbitrary"),
    )(*ins))
    out = [res.pop(0), res.pop(0)]
    if has_gate:
        out.append(res.pop(0).reshape(s, 2 * d))
    out += [res[0][0], res[1][0]]
    return out


def _loss_partial(h, target, tile=256):
    s, d = h.shape

    def body(h_ref, t_ref, o_ref):
        @pl.when(pl.program_id(0) == 0)
        def _():
            o_ref[...] = jnp.zeros_like(o_ref)

        e = h_ref[...] - t_ref[...]
        sq = e * e
        part = sq[:, 0:LANES]
        for k in range(1, d // LANES):
            part = part + sq[:, k * LANES:(k + 1) * LANES]
        o_ref[0:1, :] += jnp.sum(part, axis=0, keepdims=True) * (0.5 / d)

    return pl.pallas_call(
        body, name="loss_partial", grid=(s // tile,), in_specs=[_nat(tile, d), _nat(tile, d)],
        out_specs=_whole((SUBLANES, LANES)), out_shape=jax.ShapeDtypeStruct((SUBLANES, LANES), F32),
        compiler_params=_params("arbitrary"),
    )(h, target)


def _swiglu_fwd(name, gu, tile=256):
    s, f2 = gu.shape
    f = f2 // 2
    cw = _pick(f, 1408)
    ncb = f // cw

    def body(g_ref, u_ref, o_ref):
        gg = g_ref[...]
        o_ref[...] = _bf(gg * _sigmoid(gg) * u_ref[...])

    return pl.pallas_call(
        body, name=name, grid=(s // tile, ncb),
        in_specs=[pl.BlockSpec((tile, cw), lambda i, j: (i, j)), pl.BlockSpec((tile, cw), lambda i, j: (i, j + ncb))],
        out_specs=pl.BlockSpec((tile, cw), lambda i, j: (i, j)),
        out_shape=jax.ShapeDtypeStruct((s, f), BF16), compiler_params=_params("parallel", "parallel"),
    )(gu, gu)


def _swiglu_bwd(name, gu, dact, tile=128):
    s, f2 = gu.shape
    f = f2 // 2

    def body(g_ref, u_ref, da_ref, o_ref):
        gg = g_ref[...]
        sg = _sigmoid(gg)
        da = da_ref[...].astype(F32)
        silu = gg * sg
        o_ref[:, :f] = _bf(da * u_ref[...] * (sg + silu * (1.0 - sg)))
        o_ref[:, f:] = _bf(da * silu)

    return pl.pallas_call(
        body, name=name, grid=(s // tile,),
        in_specs=[_nat(tile, f, 0), _nat(tile, f, 1), _nat(tile, f)], out_specs=_nat(tile, f2),
        out_shape=jax.ShapeDtypeStruct((s, f2), BF16), compiler_params=_params("parallel"),
    )(gu, gu, dact)


def _rms_fwd(proj, ql, kvl, gq, gkv, tile=256):
    s = proj.shape[0]
    assert ql == kvl

    def body(q_ref, kv_ref, gq_ref, gkv_ref, o_ref):
        def nrm(x, gg):
            return x * lax.rsqrt(jnp.mean(x * x, axis=-1, keepdims=True) + RMS_EPS) * gg

        o_ref[...] = jnp.concatenate([_bf(nrm(q_ref[...], gq_ref[...])), _bf(nrm(kv_ref[...], gkv_ref[...]))], axis=1)

    return pl.pallas_call(
        body, name="mla_rms_fwd", grid=(s // tile,),
        in_specs=[_nat(tile, ql, 0), _nat(tile, kvl, 1), _whole((1, ql)), _whole((1, kvl))],
        out_specs=_nat(tile, ql + kvl), out_shape=jax.ShapeDtypeStruct((s, ql + kvl), BF16),
        compiler_params=_params("parallel"),
    )(proj, proj, gq.reshape(1, ql), gkv.reshape(1, kvl))


def _rope_coeffs(pos, invf):
    ang = pos * invf
    cs, sn = jnp.cos(ang), jnp.sin(ang)
    lane = lax.broadcasted_iota(jnp.int32, ang.shape, 1)
    half = ROPE // 2
    c = jnp.where(lane < ROPE, cs, 0.0)
    sa = jnp.where(lane < half, -sn, 0.0)
    sb = jnp.where((lane >= half) & (lane < ROPE), sn, 0.0)
    return c, sa, sb


def _rope_prep(q_raw, kv, proj, kpe_cb, pos, invf, heads, tile=256):
    s = q_raw.shape[0]
    half = ROPE // 2

    def body(q_ref, kv_ref, kpe_ref, pos_ref, invf_ref, qf_ref, kf_ref, v_ref):
        c, sa, sb = _rope_coeffs(pos_ref[...], invf_ref[...])

        def rope(t):
            return t * c + pltpu.roll(t, LANES - half, 1) * sa + pltpu.roll(t, half, 1) * sb

        kr = _bf(rope(kpe_ref[...]))
        for hh in range(heads):
            o = hh * QK_PAD
            qf_ref[:, o:o + NOPE] = _bf(q_ref[:, o:o + NOPE])
            qf_ref[:, o + NOPE:o + QK_PAD] = _bf(rope(q_ref[:, o + NOPE:o + QK_PAD]))
            kf_ref[:, o:o + NOPE] = _bf(kv_ref[:, o:o + NOPE])
            kf_ref[:, o + NOPE:o + QK_PAD] = kr
            v_ref[:, hh * VDIM:(hh + 1) * VDIM] = _bf(kv_ref[:, o + NOPE:o + QK_PAD])

    w = heads * QK_PAD
    return pl.pallas_call(
        body, name="mla_rope_prep", grid=(s // tile,),
        in_specs=[_nat(tile, w), _nat(tile, w), _nat(tile, LANES, kpe_cb), _nat(tile, 1), _whole((1, LANES))],
        out_specs=[_nat(tile, w), _nat(tile, w), _nat(tile, heads * VDIM)],
        out_shape=[jax.ShapeDtypeStruct((s, w), BF16), jax.ShapeDtypeStruct((s, w), BF16),
                   jax.ShapeDtypeStruct((s, heads * VDIM), BF16)],
        compiler_params=_params("parallel"),
    )(q_raw, kv, proj, pos, invf)


def _rope_unprep(dqf, dkf, dv, pos, invf, heads, tile=256):
    s = dqf.shape[0]
    half = ROPE // 2

    def body(dq_ref, dk_ref, dv_ref, pos_ref, invf_ref, dqr_ref, dkv_ref, dkpe_ref):
        c, sa, sb = _rope_coeffs(pos_ref[...], invf_ref[...])

        def unrope(gt):
            return gt * c + pltpu.roll(gt * sa, half, 1) + pltpu.roll(gt * sb, LANES - half, 1)

        dkpe = jnp.zeros((tile, LANES), F32)
        for hh in range(heads):
            o = hh * QK_PAD
            dqr_ref[:, o:o + NOPE] = _bf(dq_ref[:, o:o + NOPE])
            dqr_ref[:, o + NOPE:o + QK_PAD] = _bf(unrope(dq_ref[:, o + NOPE:o + QK_PAD]))
            dkv_ref[:, o:o + NOPE] = _bf(dk_ref[:, o:o + NOPE])
            dkv_ref[:, o + NOPE:o + QK_PAD] = _bf(dv_ref[:, hh * VDIM:(hh + 1) * VDIM])
            dkpe = dkpe + dk_ref[:, o + NOPE:o + QK_PAD]
        dkpe_ref[...] = unrope(dkpe)

    w = heads * QK_PAD
    return pl.pallas_call(
        body, name="mla_rope_unprep", grid=(s // tile,),
        in_specs=[_nat(tile, w), _nat(tile, w), _nat(tile, heads * VDIM), _nat(tile, 1), _whole((1, LANES))],
        out_specs=[_nat(tile, w), _nat(tile, w), _nat(tile, LANES)],
        out_shape=[jax.ShapeDtypeStruct((s, w), BF16), jax.ShapeDtypeStruct((s, w), BF16),
                   jax.ShapeDtypeStruct((s, LANES), F32)],
        compiler_params=_params("parallel"),
    )(dqf, dkf, dv, pos, invf)


LOG2E = 1.4426950408889634
MLA_SCALE = (NOPE + ROPE) ** -0.5


def _mla_scores_t(k, q, t, masked):
    sc = _dot(k, q, NT) * (MLA_SCALE * LOG2E)
    if masked:
        row = lax.broadcasted_iota(jnp.int32, (t, t), 0)
        col = lax.broadcasted_iota(jnp.int32, (t, t), 1)
        sc = jnp.where(row <= col, sc, NEG)
    return sc


def _mla_fwd(qf, kf, vt, heads, t=512):
    s = qf.shape[0]
    t = min(t, s)
    nq = s // t

    def body(q_ref, k_ref, vt_ref, o_ref, lse_ref, m_ref, l_ref, acc_ref):
        i = pl.program_id(1)
        m_ref[...] = jnp.full_like(m_ref, NEG)
        l_ref[...] = jnp.zeros_like(l_ref)
        acc_ref[...] = jnp.zeros_like(acc_ref)
        q = q_ref[...]

        def block(j, masked):
            r0 = pl.multiple_of(j * t, t)
            sc = _mla_scores_t(k_ref[pl.ds(r0, t), :], q, t, masked)
            m_prev = m_ref[0:1, :]
            m_new = jnp.maximum(m_prev, jnp.max(sc, axis=0, keepdims=True))
            corr = jnp.exp2(m_prev - m_new)
            p = jnp.exp2(sc - m_new)
            l_new = corr * l_ref[0:1, :] + jnp.sum(p, axis=0, keepdims=True)
            acc_ref[...] = corr * acc_ref[...] + _dot(vt_ref[:, pl.ds(r0, t)], _bf(p), NN)
            m_ref[...] = jnp.broadcast_to(m_new, (SUBLANES, t))
            l_ref[...] = jnp.broadcast_to(l_new, (SUBLANES, t))

        def unmasked(j, carry):
            block(j, False)
            return carry

        lax.fori_loop(0, i, unmasked, 0)
        block(i, True)
        o_ref[...] = (acc_ref[...] / l_ref[0:1, :]).T
        lse_ref[...] = m_ref[...] + jnp.log(l_ref[...]) * LOG2E

    return pl.pallas_call(
        body, name="mla_flash_fwd", grid=(heads, nq),
        in_specs=[pl.BlockSpec((t, QK_PAD), lambda h, i: (i, h)), pl.BlockSpec((s, QK_PAD), lambda h, i: (0, h)),
                  pl.BlockSpec((VDIM, s), lambda h, i: (h, 0))],
        out_specs=[pl.BlockSpec((t, VDIM), lambda h, i: (i, h)), pl.BlockSpec((SUBLANES, t), lambda h, i: (h, i))],
        out_shape=[jax.ShapeDtypeStruct((s, heads * VDIM), F32), jax.ShapeDtypeStruct((heads * SUBLANES, s), F32)],
        scratch_shapes=[pltpu.VMEM((SUBLANES, t), F32), pltpu.VMEM((SUBLANES, t), F32), pltpu.VMEM((VDIM, t), F32)],
        compiler_params=_params("parallel", "arbitrary"),
    )(qf, kf, vt)


def _mla_bwd(qf, kf, v, do, lse_t, delta_t, heads, do_cb0, t=512):
    s = qf.shape[0]
    t = min(t, s)
    nq = s // t

    def body(q_ref, k_ref, v_ref, do_ref, lse_ref, dl_ref, dq_ref, dk_ref, dv_ref, acc_ref):
        i = pl.program_id(1)

        @pl.when(i == 0)
        def _():
            dk_ref[...] = jnp.zeros_like(dk_ref)
            dv_ref[...] = jnp.zeros_like(dv_ref)

        acc_ref[...] = jnp.zeros_like(acc_ref)
        q, dob = q_ref[...], do_ref[...]
        lse, dl = lse_ref[0:1, :], dl_ref[0:1, :]

        def block(j, masked):
            r0 = pl.multiple_of(j * t, t)
            k = k_ref[pl.ds(r0, t), :]
            p = jnp.exp2(_mla_scores_t(k, q, t, masked) - lse)
            dp = _dot(v_ref[pl.ds(r0, t), :], dob, NT)
            ds = _bf(p * (dp - dl) * MLA_SCALE)
            acc_ref[...] += _dot(ds, k, TN)
            dk_ref[pl.ds(r0, t), :] += _dot(ds, q, NN)
            dv_ref[pl.ds(r0, t), :] += _dot(_bf(p), dob, NN)

        def unmasked(j, carry):
            block(j, False)
            return carry

        lax.fori_loop(0, i, unmasked, 0)
        block(i, True)
        dq_ref[...] = acc_ref[...]

    qs = lambda w, off=0: pl.BlockSpec((t, w), lambda h, i: (i, h + off))
    ks = lambda w: pl.BlockSpec((s, w), lambda h, i: (0, h))
    st = pl.BlockSpec((SUBLANES, t), lambda h, i: (h, i))
    return pl.pallas_call(
        body, name="mla_flash_bwd", grid=(heads, nq),
        in_specs=[qs(QK_PAD), ks(QK_PAD), ks(VDIM), qs(VDIM, do_cb0), st, st],
        out_specs=[qs(QK_PAD), ks(QK_PAD), ks(VDIM)],
        out_shape=[jax.ShapeDtypeStruct((s, heads * QK_PAD), F32), jax.ShapeDtypeStruct((s, heads * QK_PAD), F32),
                   jax.ShapeDtypeStruct((s, heads * VDIM), F32)],
        scratch_shapes=[pltpu.VMEM((t, QK_PAD), F32)],
        compiler_params=_params("parallel", "arbitrary"),
    )(qf, kf, v, do, lse_t, delta_t)


def _band_mask(tq, first_block):
    row = lax.broadcasted_iota(jnp.int32, (tq, DIL_STEPS + tq), 0)
    col = lax.broadcasted_iota(jnp.int32, (tq, DIL_STEPS + tq), 1)
    dist = row + DIL_STEPS - col
    valid = (dist >= 0) & (dist <= DIL_STEPS) & (jnp.logical_not(first_block) | (col >= DIL_STEPS))
    return dist, valid


def _dil_scores(q, kp, kc, slope, dil, tq, first_block):
    sc = jnp.concatenate([_dot(q, kp, NT), _dot(q, kc, NT)], axis=1) * (DHD ** -0.5)
    dist, valid = _band_mask(tq, first_block)
    return jnp.where(valid, sc - slope * (dil * dist).astype(F32), NEG)


def _dil_specs(proj_w, dh, tq):
    pwb = proj_w // LANES
    r_of = lambda cb: cb // dh
    h_of = lambda cb: cb % dh
    cur = lambda off: pl.BlockSpec((tq, DHD), lambda cb, i: (i, r_of(cb) * pwb + off + h_of(cb)))
    prev = lambda off: pl.BlockSpec(
        (DIL_STEPS, DHD), lambda cb, i: (jnp.maximum(i * (tq // DIL_STEPS) - 1, 0), r_of(cb) * pwb + off + h_of(cb)))
    return cur, prev


def _dil_fwd(name, proj, slopes, dil, dh, q_cb, tq=512):
    s, pw = proj.shape
    l = s // dil
    tq = min(tq, l)
    nb = l // tq
    k_cb, v_cb = q_cb + dh, q_cb + 2 * dh
    cur, prev = _dil_specs(pw, dh, tq)
    pv = proj.reshape(l, dil * pw)

    def body(q_ref, kc_ref, kp_ref, vc_ref, vp_ref, sl_ref, o_ref, lse_ref):
        i = pl.program_id(1)
        sc = _dil_scores(_bf(q_ref[...]), _bf(kp_ref[...]), _bf(kc_ref[...]), sl_ref[0:1, 0:1], dil, tq, i == 0)
        m = jnp.max(sc, axis=-1, keepdims=True)
        e = jnp.exp(sc - m)
        lsum = jnp.sum(e, axis=-1, keepdims=True)
        pn = e / lsum
        o_ref[...] = (_dot(_bf(pn[:, :DIL_STEPS]), _bf(vp_ref[...]), NN)
                      + _dot(_bf(pn[:, DIL_STEPS:]), _bf(vc_ref[...]), NN))
        lse_ref[...] = jnp.broadcast_to(m + jnp.log(lsum), (tq, LANES))

    ospec = pl.BlockSpec((tq, DHD), lambda cb, i: (i, cb))
    o, lse = pl.pallas_call(
        body, name=name, grid=(dil * dh, nb),
        in_specs=[cur(q_cb), cur(k_cb), prev(k_cb), cur(v_cb), prev(v_cb),
                  pl.BlockSpec((SUBLANES, LANES), lambda cb, i: (cb % dh, 0))],
        out_specs=[ospec, ospec],
        out_shape=[jax.ShapeDtypeStruct((l, dil * dh * DHD), F32)] * 2,
        compiler_params=_params("parallel", "parallel"),
    )(pv, pv, pv, pv, pv, slopes)
    return o.reshape(s, dh * DHD), lse.reshape(s, dh * DHD)


def _dil_bwd_dq(name, proj, slopes, do, lse, delta, dil, dh, q_cb, b_cb0, tq=512):
    s, pw = proj.shape
    mixw = do.shape[1]
    l = s // dil
    tq = min(tq, l)
    nb = l // tq
    k_cb, v_cb = q_cb + dh, q_cb + 2 * dh
    cur, prev = _dil_specs(pw, dh, tq)
    pv = proj.reshape(l, dil * pw)
    mb = mixw // LANES
    mspec = pl.BlockSpec((tq, DHD), lambda cb, i: (i, (cb // dh) * mb + b_cb0 + cb % dh))
    ospec = pl.BlockSpec((tq, DHD), lambda cb, i: (i, cb))

    def body(q_ref, kc_ref, kp_ref, vc_ref, vp_ref, sl_ref, do_ref, lse_ref, dl_ref, dq_ref):
        i = pl.program_id(1)
        kp, kc = _bf(kp_ref[...]), _bf(kc_ref[...])
        sc = _dil_scores(_bf(q_ref[...]), kp, kc, sl_ref[0:1, 0:1], dil, tq, i == 0)
        p = jnp.exp(sc - lse_ref[:, 0:1])
        dob = do_ref[...]
        dp = jnp.concatenate([_dot(dob, _bf(vp_ref[...]), NT), _dot(dob, _bf(vc_ref[...]), NT)], axis=1)
        ds = _bf(p * (dp - dl_ref[:, 0:1]) * (DHD ** -0.5))
        dq_ref[...] = _dot(ds[:, :DIL_STEPS], kp, NN) + _dot(ds[:, DIL_STEPS:], kc, NN)

    dq = pl.pallas_call(
        body, name=name, grid=(dil * dh, nb),
        in_specs=[cur(q_cb), cur(k_cb), prev(k_cb), cur(v_cb), prev(v_cb),
                  pl.BlockSpec((SUBLANES, LANES), lambda cb, i: (cb % dh, 0)), mspec, ospec, mspec],
        out_specs=ospec, out_shape=jax.ShapeDtypeStruct((l, dil * dh * DHD), F32),
        compiler_params=_params("parallel", "parallel"),
    )(pv, pv, pv, pv, pv, slopes, do.reshape(l, dil * mixw), lse.reshape(l, dil * dh * DHD), delta.reshape(l, dil * mixw))
    return dq.reshape(s, dh * DHD)


def _dil_bwd_dkv(name, proj, slopes, do, lse, delta, dil, dh, q_cb, b_cb0, tk=512):
    s, pw = proj.shape
    mixw = do.shape[1]
    l = s // dil
    tk = min(tk, l)
    nb = l // tk
    k_cb, v_cb = q_cb + dh, q_cb + 2 * dh
    pwb, mb = pw // LANES, mixw // LANES
    sub = tk // DIL_STEPS
    last128 = l // DIL_STEPS - 1
    pv = proj.reshape(l, dil * pw)

    def cur(width_blocks, off):
        return pl.BlockSpec((tk, DHD), lambda cb, j: (j, (cb // dh) * width_blocks + off + cb % dh))

    def nxt(width_blocks, off):
        return pl.BlockSpec((DIL_STEPS, DHD), lambda cb, j: (jnp.minimum((j + 1) * sub, last128),
                                                               (cb // dh) * width_blocks + off + cb % dh))

    ocur = pl.BlockSpec((tk, DHD), lambda cb, j: (j, cb))
    onxt = pl.BlockSpec((DIL_STEPS, DHD), lambda cb, j: (jnp.minimum((j + 1) * sub, last128), cb))

    def body(k_ref, v_ref, qc_ref, qn_ref, sl_ref, doc_ref, don_ref, lsec_ref, lsen_ref, dlc_ref, dln_ref,
             dk_ref, dv_ref):
        j = pl.program_id(1)
        slope = sl_ref[0:1, 0:1]
        scale = DHD ** -0.5
        k, v = _bf(k_ref[...]), _bf(v_ref[...])
        qc = _bf(qc_ref[...])
        row = lax.broadcasted_iota(jnp.int32, (tk, tk), 0)
        col = lax.broadcasted_iota(jnp.int32, (tk, tk), 1)
        dist = row - col
        valid = (dist >= 0) & (dist <= DIL_STEPS)
        sc = jnp.where(valid, _dot(qc, k, NT) * scale - slope * (dil * dist).astype(F32), NEG)
        p = jnp.exp(sc - lsec_ref[:, 0:1])
        doc = doc_ref[...]
        ds = _bf(p * (_dot(doc, v, NT) - dlc_ref[:, 0:1]) * scale)
        dv_ref[...] = _dot(_bf(p), doc, TN)
        dk_ref[...] = _dot(ds, qc, TN)
        kl, vl = k[tk - DIL_STEPS:, :], v[tk - DIL_STEPS:, :]
        qn = _bf(qn_ref[...])
        row = lax.broadcasted_iota(jnp.int32, (DIL_STEPS, DIL_STEPS), 0)
        col = lax.broadcasted_iota(jnp.int32, (DIL_STEPS, DIL_STEPS), 1)
        dist = DIL_STEPS + row - col
        valid = (dist <= DIL_STEPS) & (j < nb - 1)
        sc = jnp.where(valid, _dot(qn, kl, NT) * scale - slope * (dil * dist).astype(F32), NEG)
        p = jnp.exp(sc - lsen_ref[:, 0:1])
        don = don_ref[...]
        ds = _bf(p * (_dot(don, vl, NT) - dln_ref[:, 0:1]) * scale)
        dv_ref[tk - DIL_STEPS:, :] += _dot(_bf(p), don, TN)
        dk_ref[tk - DIL_STEPS:, :] += _dot(ds, qn, TN)

    dov = do.reshape(l, dil * mixw)
    dlv = delta.reshape(l, dil * mixw)
    lsv = lse.reshape(l, dil * dh * DHD)
    dk, dv = pl.pallas_call(
        body, name=name, grid=(dil * dh, nb),
        in_specs=[cur(pwb, k_cb), cur(pwb, v_cb), cur(pwb, q_cb), nxt(pwb, q_cb),
                  pl.BlockSpec((SUBLANES, LANES), lambda cb, j: (cb % dh, 0)),
                  cur(mb, b_cb0), nxt(mb, b_cb0), ocur, onxt, cur(mb, b_cb0), nxt(mb, b_cb0)],
        out_specs=[ocur, ocur], out_shape=[jax.ShapeDtypeStruct((l, dil * dh * DHD), F32)] * 2,
        compiler_params=_params("parallel", "parallel"),
    )(pv, pv, pv, pv, slopes, dov, dov, lsv, lsv, dlv, dlv)
    return dk.reshape(s, dh * DHD), dv.reshape(s, dh * DHD)


def _dil_merge(out_a, outs, lses, tile=256):
    s, wa = out_a.shape
    wb = outs[0].shape[1]
    nbr = len(outs)

    def body(*refs):
        a_ref = refs[0]
        o_refs, l_refs = refs[1:1 + nbr], refs[1 + nbr:1 + 2 * nbr]
        att_ref, ob_ref, lse_ref = refs[1 + 2 * nbr:]
        ls = [r[...] for r in l_refs]
        m = ls[0]
        for x_ in ls[1:]:
            m = jnp.maximum(m, x_)
        es = [jnp.exp(x_ - m) for x_ in ls]
        tot = es[0]
        for e in es[1:]:
            tot = tot + e
        ob = (es[0] / tot) * o_refs[0][...]
        for e, r in zip(es[1:], o_refs[1:]):
            ob = ob + (e / tot) * r[...]
        ob_ref[...] = ob
        lse_ref[...] = m + jnp.log(tot)
        att_ref[...] = jnp.concatenate([_bf(a_ref[...]), _bf(ob)], axis=1)

    return pl.pallas_call(
        body, name="dil_merge", grid=(s // tile,),
        in_specs=[_nat(tile, wa)] + [_nat(tile, wb)] * (2 * nbr),
        out_specs=[_nat(tile, wa + wb), _nat(tile, wb), _nat(tile, wb)],
        out_shape=[jax.ShapeDtypeStruct((s, wa + wb), BF16), jax.ShapeDtypeStruct((s, wb), F32),
                   jax.ShapeDtypeStruct((s, wb), F32)],
        compiler_params=_params("parallel"),
    )(out_a, *outs, *lses)


def _attn_bwd_prep(datt, out_a, out_b, tile=256):
    s, mixw = datt.shape
    wa = out_a.shape[1]
    heads_a = wa // LANES

    def body(d_ref, a_ref, b_ref, do_ref, dl_ref, dlt_ref):
        d = d_ref[...]
        do_ref[...] = _bf(d)
        prod = d * jnp.concatenate([a_ref[...], b_ref[...]], axis=1)
        for hh in range(mixw // LANES):
            sl = slice(hh * LANES, (hh + 1) * LANES)
            dl = jnp.broadcast_to(jnp.sum(prod[:, sl], axis=-1, keepdims=True), (tile, LANES))
            dl_ref[:, sl] = dl
            if hh < heads_a:
                dlt_ref[hh * SUBLANES:(hh + 1) * SUBLANES, :] = dl.T[0:SUBLANES, :]

    return pl.pallas_call(
        body, name="attn_bwd_prep", grid=(s // tile,),
        in_specs=[_nat(tile, mixw), _nat(tile, wa), _nat(tile, mixw - wa)],
        out_specs=[_nat(tile, mixw), _nat(tile, mixw), pl.BlockSpec((heads_a * SUBLANES, tile), lambda i: (0, i))],
        out_shape=[jax.ShapeDtypeStruct((s, mixw), BF16), jax.ShapeDtypeStruct((s, mixw), F32),
                   jax.ShapeDtypeStruct((heads_a * SUBLANES, s), F32)],
        compiler_params=_params("parallel"),
    )(datt, out_a, out_b)


def _dproj_assemble(proj, dnq, dnkv, dkpe, dqs, dks, dvs, gq, gkv, ql, tile=256):
    s, pw = proj.shape
    dw = dqs[0].shape[1]
    nbr = len(dqs)

    def body(*refs):
        ql_ref, kvl_ref, dnq_ref, dnkv_ref, dkpe_ref = refs[:5]
        br = refs[5:5 + 3 * nbr]
        gq_ref, gkv_ref = refs[5 + 3 * nbr:7 + 3 * nbr]
        dp_ref, dgq_ref, dgkv_ref = refs[7 + 3 * nbr:]

        @pl.when(pl.program_id(0) == 0)
        def _():
            dgq_ref[...] = jnp.zeros_like(dgq_ref)
            dgkv_ref[...] = jnp.zeros_like(dgkv_ref)

        def rms_bwd(x, dy, gg, dg_ref):
            r = lax.rsqrt(jnp.mean(x * x, axis=-1, keepdims=True) + RMS_EPS)
            xh = x * r
            dxh = dy * gg
            dg_ref[0:1, :] += jnp.sum(dy * xh, axis=0, keepdims=True)
            return r * (dxh - xh * jnp.mean(dxh * xh, axis=-1, keepdims=True))

        pieces = [_bf(rms_bwd(ql_ref[...], dnq_ref[...], gq_ref[...], dgq_ref)),
                  _bf(rms_bwd(kvl_ref[...], dnkv_ref[...], gkv_ref[...], dgkv_ref)),
                  _bf(dkpe_ref[...])]
        for k in range(3):
            acc = br[k * nbr][...]
            for r in br[k * nbr + 1:(k + 1) * nbr]:
                acc = acc + r[...]
            pieces.append(_bf(acc))
        dp_ref[...] = jnp.concatenate(pieces, axis=1)

    res = pl.pallas_call(
        body, name="dproj_assemble", grid=(s // tile,),
        in_specs=[_nat(tile, ql, 0), _nat(tile, ql, 1), _nat(tile, ql), _nat(tile, ql), _nat(tile, LANES)]
        + [_nat(tile, dw)] * (3 * nbr) + [_whole((1, ql)), _whole((1, ql))],
        out_specs=[_nat(tile, pw), _whole((SUBLANES, ql)), _whole((SUBLANES, ql))],
        out_shape=[jax.ShapeDtypeStruct((s, pw), BF16), jax.ShapeDtypeStruct((SUBLANES, ql), F32),
                   jax.ShapeDtypeStruct((SUBLANES, ql), F32)],
        compiler_params=_params("arbitrary"),
    )(proj, proj, dnq, dnkv, dkpe, *dqs, *dks, *dvs, gq.reshape(1, ql), gkv.reshape(1, ql))
    return res[0], res[1][0], res[2][0]


def _axpy(name, alpha, a, b, tile=256):
    s, d = a.shape

    def body(a_ref, b_ref, o_ref):
        o_ref[...] = alpha * a_ref[...] + b_ref[...]

    return pl.pallas_call(
        body, name=name, grid=(s // tile,), in_specs=[_nat(tile, d), _nat(tile, d)], out_specs=_nat(tile, d),
        out_shape=jax.ShapeDtypeStruct((s, d), F32), compiler_params=_params("parallel"),
    )(a, b)


def _cmul(ar, ai, br, bi):
    return ar * br - ai * bi, ar * bi + ai * br


def _s5_discretise(a_re, a_im, log_dt, b_re, b_im, n_sq):
    shape = a_re.shape

    def body(ar_ref, ai_ref, ldt_ref, br_ref, bi_ref, abr_ref, abi_ref, apr_ref, api_ref, bbr_ref, bbi_ref):
        ar, ai = ar_ref[...], ai_ref[...]
        dt = jnp.exp(ldt_ref[...])
        e = jnp.exp(ar * dt)
        abr, abi = e * jnp.cos(ai * dt), e * jnp.sin(ai * dt)
        den = ar * ar + ai * ai
        qr = ((abr - 1.0) * ar + abi * ai) / den
        qi = (abi * ar - (abr - 1.0) * ai) / den
        bbr, bbi = _cmul(qr, qi, br_ref[...], bi_ref[...])
        abr_ref[...], abi_ref[...] = abr, abi
        bbr_ref[...], bbi_ref[...] = bbr, bbi
        pr, pi = abr, abi
        for _ in range(n_sq):
            pr, pi = _cmul(pr, pi, pr, pi)
        apr_ref[...], api_ref[...] = pr, pi

    return pl.pallas_call(
        body, name="s5_discretise", out_shape=[jax.ShapeDtypeStruct(shape, F32)] * 6,
        compiler_params=pltpu.CompilerParams(vmem_limit_bytes=VMEM_LIMIT),
    )(a_re, a_im, log_dt, b_re, b_im)


def _s5_discretise_bwd(a16, b16, ag, gab, gbb):
    rows, p = a16[0].shape
    g = rows // S5_GROUP

    def disc(ar, ai, ldt):
        dt = jnp.exp(ldt)
        e = jnp.exp(ar * dt)
        abr, abi = e * jnp.cos(ai * dt), e * jnp.sin(ai * dt)
        den = ar * ar + ai * ai
        inv_r, inv_i = ar / den, -ai / den
        qr, qi = _cmul(abr - 1.0, abi, inv_r, inv_i)
        return dt, abr, abi, inv_r, inv_i, qr, qi

    def body(ar16_ref, ai16_ref, ldt16_ref, br_ref, bi_ref, ar_ref, ai_ref, ldt_ref, gar_ref, gai_ref, gbr_ref, gbi_ref,
             dar_ref, dai_ref, dldt_ref, dbr_ref, dbi_ref):
        _, _, _, _, _, qr16, qi16 = disc(ar16_ref[...], ai16_ref[...], ldt16_ref[...])
        gbr, gbi = gbr_ref[...], gbi_ref[...]
        dbr_ref[...], dbi_ref[...] = _cmul(qr16, -qi16, gbr, gbi)
        cr, ci = _cmul(br_ref[...], -bi_ref[...], gbr, gbi)
        gqr = jnp.sum(cr.reshape(g, S5_GROUP, p), axis=1)
        gqi = jnp.sum(ci.reshape(g, S5_GROUP, p), axis=1)
        ar, ai = ar_ref[...], ai_ref[...]
        dt, abr, abi, inv_r, inv_i, qr, qi = disc(ar, ai, ldt_ref[...])
        t_r, t_i = _cmul(inv_r, -inv_i, gqr, gqi)
        gab_r = gar_ref[...] + t_r
        gab_i = gai_ref[...] + t_i
        qa_r, qa_i = _cmul(qr, qi, inv_r, inv_i)
        a1_r, a1_i = _cmul(qa_r, -qa_i, gqr, gqi)
        gl_r, gl_i = _cmul(abr, -abi, gab_r, gab_i)
        dar_ref[...] = dt * gl_r - a1_r
        dai_ref[...] = dt * gl_i - a1_i
        gdt = jnp.sum(ar * gl_r + ai * gl_i, axis=-1, keepdims=True)
        dldt_ref[...] = gdt * dt[:, 0:1]

    return pl.pallas_call(
        body, name="s5_discretise_bwd",
        out_shape=[jax.ShapeDtypeStruct((g, p), F32), jax.ShapeDtypeStruct((g, p), F32),
                   jax.ShapeDtypeStruct((g, 1), F32), jax.ShapeDtypeStruct((rows, p), F32),
                   jax.ShapeDtypeStruct((rows, p), F32)],
        compiler_params=pltpu.CompilerParams(vmem_limit_bytes=VMEM_LIMIT),
    )(*a16, *b16, *ag, *gab, *gbb)


def _slab_tile(re, im, nsl):
    row = jnp.concatenate([re.reshape(nsl, SLAB_COLS), im.reshape(nsl, SLAB_COLS)], axis=-1)
    return jnp.repeat(row, SUBLANES, axis=0)


def _slab_in_matrix(b_re, b_im, nsl):
    eye = jnp.eye(SLAB_GROUPS, dtype=F32)

    def blk(b):
        b = b.reshape(nsl, SLAB_GROUPS, S5_GROUP, S5_STATE)
        return jnp.einsum('sgcp,gh->sgchp', b, eye).reshape(nsl, LANES, SLAB_COLS)

    return jnp.concatenate([blk(b_re), blk(b_im)], axis=-1)


def _slab_in_extract(m, nsl):
    eye = jnp.eye(SLAB_GROUPS, dtype=F32)

    def ext(x_):
        x_ = x_.reshape(nsl, SLAB_GROUPS, S5_GROUP, SLAB_GROUPS, S5_STATE)
        return jnp.einsum('sgchp,gh->sgcp', x_, eye).reshape(nsl * LANES, S5_STATE)

    return ext(m[..., :SLAB_COLS]), ext(m[..., SLAB_COLS:])


def _slab_out_matrix(c_re, c_im, nsl):
    eye = jnp.eye(SLAB_GROUPS, dtype=F32)

    def blk(c):
        c = c.reshape(nsl, SLAB_GROUPS, S5_GROUP, S5_STATE)
        return jnp.einsum('sgcp,gh->sgphc', c, eye).reshape(nsl, SLAB_COLS, LANES)

    return jnp.concatenate([blk(c_re), -blk(c_im)], axis=1)


def _slab_out_extract(m, nsl):
    eye = jnp.eye(SLAB_GROUPS, dtype=F32)

    def ext(x_):
        x_ = x_.reshape(nsl, SLAB_GROUPS, S5_STATE, SLAB_GROUPS, S5_GROUP)
        return jnp.einsum('sgphc,gh->sgcp', x_, eye).reshape(nsl * SLAB_GROUPS, S5_GROUP, S5_STATE)

    return ext(m[:, :SLAB_COLS]), -ext(m[:, SLAB_COLS:])


def _gelu(y):
    t = jnp.tanh(0.7978845608028654 * (y + 0.044715 * y * y * y))
    return 0.5 * y * (1.0 + t)


def _gelu_grad(y):
    t = jnp.tanh(0.7978845608028654 * (y + 0.044715 * y * y * y))
    return 0.5 * (1.0 + t) + 0.5 * y * (1.0 - t * t) * 0.7978845608028654 * (1.0 + 3.0 * 0.044715 * y * y)


def _scan_rows(ref, n_steps, ar, ai, state, reverse, conj):
    sgn = -1.0 if conj else 1.0

    def step(k, carry):
        xr, xi = carry
        t = (n_steps - 1 - k) if reverse else k
        r0 = pl.multiple_of(t * SUBLANES, SUBLANES)
        nr = ar * xr - sgn * ai * xi + ref[pl.ds(r0, SUBLANES), :SLAB_COLS]
        ni = ar * xi + sgn * ai * xr + ref[pl.ds(r0, SUBLANES), SLAB_COLS:]
        ref[pl.ds(r0, SUBLANES), :SLAB_COLS] = nr
        ref[pl.ds(r0, SUBLANES), SLAB_COLS:] = ni
        return nr, ni

    return lax.fori_loop(0, n_steps, step, state, unroll=4)


def _s5_pass1(hp, bblk, ab_tile, rc=1024):
    s, d = hp.shape
    nsl = d // LANES
    rc = min(rc, s)
    nch = s // rc
    w = 2 * SLAB_COLS

    def body(u_ref, b_ref, ab_ref, x_ref, end_ref, st_ref):
        j = pl.program_id(1)

        @pl.when(j == 0)
        def _():
            st_ref[...] = jnp.zeros_like(st_ref)

        x_ref[...] = _dot(_bf(u_ref[...]), b_ref[0], NN)
        xr, xi = _scan_rows(x_ref, rc // SUBLANES, ab_ref[:, :SLAB_COLS], ab_ref[:, SLAB_COLS:],
                            (st_ref[:, :SLAB_COLS], st_ref[:, SLAB_COLS:]), False, False)
        st_ref[:, :SLAB_COLS] = xr
        st_ref[:, SLAB_COLS:] = xi

        @pl.when(j == nch - 1)
        def _():
            end_ref[...] = st_ref[...]

    return pl.pallas_call(
        body, name="s5_scan_local", grid=(nsl, nch),
        in_specs=[pl.BlockSpec((rc, LANES), lambda sl, j: (j, sl)), pl.BlockSpec((1, LANES, w), lambda sl, j: (sl, 0, 0)),
                  pl.BlockSpec((SUBLANES, w), lambda sl, j: (sl, 0))],
        out_specs=[pl.BlockSpec((rc, w), lambda sl, j: (j, sl)), pl.BlockSpec((SUBLANES, w), lambda sl, j: (sl, 0))],
        out_shape=[jax.ShapeDtypeStruct((s, nsl * w), F32), jax.ShapeDtypeStruct((nsl * SUBLANES, w), F32)],
        scratch_shapes=[pltpu.VMEM((SUBLANES, w), F32)],
        compiler_params=_params("parallel", "arbitrary"),
    )(hp, bblk, ab_tile)


def _s5_carry(name, ends, ap_tile, reverse):
    rows, w = ends.shape
    nsl = rows // SUBLANES
    sgn = -1.0 if reverse else 1.0

    def body(e_ref, ap_ref, c_ref):
        pr, pi = ap_ref[0:1, :SLAB_COLS], sgn * ap_ref[0:1, SLAB_COLS:]
        tr = jnp.zeros((1, SLAB_COLS), F32)
        ti = jnp.zeros((1, SLAB_COLS), F32)
        order = range(SUBLANES - 1, -1, -1) if reverse else range(SUBLANES)
        for seg in order:
            c_ref[seg:seg + 1, :SLAB_COLS] = tr
            c_ref[seg:seg + 1, SLAB_COLS:] = ti
            mr, mi = _cmul(pr, pi, tr, ti)
            tr = e_ref[seg:seg + 1, :SLAB_COLS] + mr
            ti = e_ref[seg:seg + 1, SLAB_COLS:] + mi

    spec = pl.BlockSpec((SUBLANES, w), lambda sl: (sl, 0))
    return pl.pallas_call(
        body, name=name, grid=(nsl,), in_specs=[spec, spec], out_specs=spec,
        out_shape=jax.ShapeDtypeStruct((rows, w), F32), compiler_params=_params("parallel"),
    )(ends, ap_tile)


def _s5_pass2(xloc, cin, ab_tile, cblk, hp, dvec, rc=1024):
    s, d = hp.shape
    nsl = d // LANES
    rc = min(rc, s)
    nch = s // rc
    w = 2 * SLAB_COLS

    def body(xl_ref, cin_ref, ab_ref, c_ref, h_ref, d_ref, x_ref, y_ref, z_ref, st_ref):
        j = pl.program_id(1)

        @pl.when(j == 0)
        def _():
            st_ref[...] = cin_ref[...]

        x_ref[...] = jnp.zeros_like(x_ref)
        zr, zi = _scan_rows(x_ref, rc // SUBLANES, ab_ref[:, :SLAB_COLS], ab_ref[:, SLAB_COLS:],
                            (st_ref[:, :SLAB_COLS], st_ref[:, SLAB_COLS:]), False, False)
        st_ref[:, :SLAB_COLS] = zr
        st_ref[:, SLAB_COLS:] = zi
        x = x_ref[...] + xl_ref[...]
        x_ref[...] = x
        y = _dot(_bf(x), c_ref[0], NN) + d_ref[...] * h_ref[...]
        y_ref[...] = y
        z_ref[...] = _bf(_gelu(y))

    tile = lambda wd: pl.BlockSpec((rc, wd), lambda sl, j: (j, sl))
    small = pl.BlockSpec((SUBLANES, w), lambda sl, j: (sl, 0))
    return pl.pallas_call(
        body, name="s5_scan_carry_out", grid=(nsl, nch),
        in_specs=[tile(w), small, small, pl.BlockSpec((1, w, LANES), lambda sl, j: (sl, 0, 0)), tile(LANES),
                  pl.BlockSpec((1, LANES), lambda sl, j: (0, sl))],
        out_specs=[tile(w), tile(LANES), tile(LANES)],
        out_shape=[jax.ShapeDtypeStruct((s, nsl * w), F32), jax.ShapeDtypeStruct((s, d), F32),
                   jax.ShapeDtypeStruct((s, d), BF16)],
        scratch_shapes=[pltpu.VMEM((SUBLANES, w), F32)],
        compiler_params=_params("parallel", "arbitrary"),
    )(xloc, cin, ab_tile, cblk, hp, dvec)


def _s5_bwd_pass1(dzg, ypre, cblk, ab_tile, hp, rc=1024):
    s, d = hp.shape
    nsl = d // LANES
    rc = min(rc, s)
    nch = s // rc
    w = 2 * SLAB_COLS

    def body(dz_ref, y_ref, c_ref, ab_ref, h_ref, lam_ref, st_out_ref, dy_ref, dd_ref, st_ref):
        j = pl.program_id(1)

        @pl.when(j == 0)
        def _():
            st_ref[...] = jnp.zeros_like(st_ref)
            dd_ref[...] = jnp.zeros_like(dd_ref)

        dy = dz_ref[...] * _gelu_grad(y_ref[...])
        dy_ref[...] = dy
        dd_ref[0:1, :] += jnp.sum(dy * h_ref[...], axis=0, keepdims=True)
        lam_ref[...] = _dot(_bf(dy), c_ref[0], NT)
        lr, li = _scan_rows(lam_ref, rc // SUBLANES, ab_ref[:, :SLAB_COLS], ab_ref[:, SLAB_COLS:],
                            (st_ref[:, :SLAB_COLS], st_ref[:, SLAB_COLS:]), True, True)
        st_ref[:, :SLAB_COLS] = lr
        st_ref[:, SLAB_COLS:] = li

        @pl.when(j == nch - 1)
        def _():
            st_out_ref[...] = st_ref[...]

    tile = lambda wd: pl.BlockSpec((rc, wd), lambda sl, j: (nch - 1 - j, sl))
    small = pl.BlockSpec((SUBLANES, w), lambda sl, j: (sl, 0))
    return pl.pallas_call(
        body, name="s5_adjoint_local", grid=(nsl, nch),
        in_specs=[tile(LANES), tile(LANES), pl.BlockSpec((1, w, LANES), lambda sl, j: (sl, 0, 0)), small, tile(LANES)],
        out_specs=[tile(w), small, tile(LANES), pl.BlockSpec((SUBLANES, LANES), lambda sl, j: (0, sl))],
        out_shape=[jax.ShapeDtypeStruct((s, nsl * w), F32), jax.ShapeDtypeStruct((nsl * SUBLANES, w), F32),
                   jax.ShapeDtypeStruct((s, d), F32), jax.ShapeDtypeStruct((SUBLANES, d), F32)],
        scratch_shapes=[pltpu.VMEM((SUBLANES, w), F32)],
        compiler_params=_params("parallel", "arbitrary"),
    )(dzg, ypre, cblk, ab_tile, hp)


def _s5_bwd_pass2(lamloc, cinl, ab_tile, xtrue, cinx, hp, dy, bblk, dvec, rc=1024):
    s, d = hp.shape
    nsl = d // LANES
    rc = min(rc, s)
    nch = s // rc
    w = 2 * SLAB_COLS
    n_steps = rc // SUBLANES

    def body(ll_ref, cl_ref, ab_ref, x_ref, xp_ref, cx_ref, h_ref, dy_ref, b_ref, d_ref,
             du_ref, db_ref, dc_ref, da_ref, st_ref, lam_ref, acc_ref):
        j = pl.program_id(1)

        @pl.when(j == 0)
        def _():
            st_ref[...] = cl_ref[...]
            acc_ref[...] = jnp.zeros_like(acc_ref)
            db_ref[...] = jnp.zeros_like(db_ref)
            dc_ref[...] = jnp.zeros_like(dc_ref)

        ar, ai = ab_ref[:, :SLAB_COLS], ab_ref[:, SLAB_COLS:]
        lam_ref[...] = jnp.zeros_like(lam_ref)
        zr, zi = _scan_rows(lam_ref, n_steps, ar, ai, (st_ref[:, :SLAB_COLS], st_ref[:, SLAB_COLS:]), True, True)
        st_ref[:, :SLAB_COLS] = zr
        st_ref[:, SLAB_COLS:] = zi
        lam_ref[...] = lam_ref[...] + ll_ref[...]

        def step(k, carry):
            dr, di = carry
            r0 = pl.multiple_of(k * SUBLANES, SUBLANES)
            r1 = pl.multiple_of((k + 1) * SUBLANES, SUBLANES)
            xr, xi = x_ref[pl.ds(r0, SUBLANES), :SLAB_COLS], x_ref[pl.ds(r0, SUBLANES), SLAB_COLS:]
            lr, li = lam_ref[pl.ds(r1, SUBLANES), :SLAB_COLS], lam_ref[pl.ds(r1, SUBLANES), SLAB_COLS:]
            return dr + xr * lr + xi * li, di + xr * li - xi * lr

        dr, di = lax.fori_loop(0, n_steps - 1, step, (acc_ref[:, :SLAB_COLS], acc_ref[:, SLAB_COLS:]), unroll=4)
        first_chunk = j == nch - 1
        xr = jnp.where(first_chunk, cx_ref[:, :SLAB_COLS], xp_ref[:, :SLAB_COLS])
        xi = jnp.where(first_chunk, cx_ref[:, SLAB_COLS:], xp_ref[:, SLAB_COLS:])
        lr, li = lam_ref[0:SUBLANES, :SLAB_COLS], lam_ref[0:SUBLANES, SLAB_COLS:]
        acc_ref[:, :SLAB_COLS] = dr + xr * lr + xi * li
        acc_ref[:, SLAB_COLS:] = di + xr * li - xi * lr

        lam_b = _bf(lam_ref[...])
        dyv = dy_ref[...]
        db_ref[0] += _dot(_bf(h_ref[...]), lam_b, TN)
        dc_ref[0] += _dot(_bf(x_ref[...]), _bf(dyv), TN)
        du_ref[...] = _dot(lam_b, b_ref[0], NT) + d_ref[...] * dyv

        @pl.when(j == nch - 1)
        def _():
            da_ref[...] = jnp.broadcast_to(jnp.sum(acc_ref[...], axis=0, keepdims=True), (SUBLANES, w))

    sub = rc // SUBLANES
    tile = lambda wd: pl.BlockSpec((rc, wd), lambda sl, j: (nch - 1 - j, sl))
    small = pl.BlockSpec((SUBLANES, w), lambda sl, j: (sl, 0))
    prev = pl.BlockSpec((SUBLANES, w), lambda sl, j: (jnp.maximum((nch - 1 - j) * sub - 1, 0), sl))
    return pl.pallas_call(
        body, name="s5_adjoint_carry_grads", grid=(nsl, nch),
        in_specs=[tile(w), small, small, tile(w), prev, small, tile(LANES), tile(LANES),
                  pl.BlockSpec((1, LANES, w), lambda sl, j: (sl, 0, 0)), pl.BlockSpec((1, LANES), lambda sl, j: (0, sl))],
        out_specs=[tile(LANES), pl.BlockSpec((1, LANES, w), lambda sl, j: (sl, 0, 0)),
                   pl.BlockSpec((1, w, LANES), lambda sl, j: (sl, 0, 0)), small],
        out_shape=[jax.ShapeDtypeStruct((s, d), F32), jax.ShapeDtypeStruct((nsl, LANES, w), F32),
                   jax.ShapeDtypeStruct((nsl, w, LANES), F32), jax.ShapeDtypeStruct((nsl * SUBLANES, w), F32)],
        scratch_shapes=[pltpu.VMEM((SUBLANES, w), F32), pltpu.VMEM((rc, w), F32), pltpu.VMEM((SUBLANES, w), F32)],
        compiler_params=_params("parallel", "arbitrary"),
    )(lamloc, cinl, ab_tile, xtrue, xtrue, cinx, hp, dy, bblk, dvec)


def _adamw(name, w, g, m, v):
    r, c = w.shape
    tile = r if r * c <= 512 * 1024 else _pick(r, max(SUBLANES, (512 * 1024 // c) // SUBLANES * SUBLANES), q=SUBLANES)
    c1 = 1.0 / (1.0 - ADAM_B1 ** ADAM_STEP)
    c2 = 1.0 / (1.0 - ADAM_B2 ** ADAM_STEP)

    def body(w_ref, g_ref, m_ref, v_ref, d_ref, nm_ref, nv_ref):
        gg = g_ref[...]
        nm = ADAM_B1 * m_ref[...] + (1.0 - ADAM_B1) * gg
        nv = ADAM_B2 * v_ref[...] + (1.0 - ADAM_B2) * gg * gg
        d_ref[...] = -ADAM_LR * ((nm * c1) / (jnp.sqrt(nv * c2) + ADAM_EPS) + ADAM_WD * w_ref[...])
        nm_ref[...] = nm
        nv_ref[...] = nv

    spec = _nat(tile, c)
    return pl.pallas_call(
        body, name=name, grid=(r // tile,), in_specs=[spec] * 4, out_specs=[spec] * 3,
        out_shape=[jax.ShapeDtypeStruct((r, c), F32)] * 3, compiler_params=_params("parallel"),
    )(w, g, m, v)


def _place():
    x, y, c = lax.axis_index("x"), lax.axis_index("y"), lax.axis_index("c")
    return x, y, c, [(1 - x, y), (x, 1 - y), (1 - x, 1 - y)]


_ANY = pl.BlockSpec(memory_space=pl.ANY)


def _gather_weights(shards):
    n = len(shards)

    def body(*refs):
        ins, outs = refs[:n], refs[n:2 * n]
        send_sems, recv_sems, local_sems = refs[2 * n:]
        x, y, c, chips = _place()
        me = 2 * x + y
        sibling = (x, y, 1 - c)
        started = []
        for a in range(n):
            local = pltpu.make_async_copy(ins[a], outs[a].at[me], local_sems.at[a])
            local.start()
            started.append(local)

        def half(a, chip, h):
            hw = ins[a].shape[1] // 2
            return outs[a].at[chip, :, pl.ds(pl.multiple_of(h * hw, LANES), hw)]

        def copy(a, k, src, chip, h, to):
            return pltpu.make_async_remote_copy(
                src_ref=src, dst_ref=half(a, chip, h), send_sem=send_sems.at[a, k], recv_sem=recv_sems.at[a, k],
                device_id=to, device_id_type=MESH)

        sends = []
        for a in range(n):
            hw = ins[a].shape[1] // 2
            mine = ins[a].at[:, pl.ds(pl.multiple_of(c * hw, LANES), hw)]
            for k, chip in enumerate(chips):
                cp = copy(a, k, mine, me, c, (*chip, c))
                cp.start()
                sends.append(cp)
        for a in range(n):
            for k, (cx, cy) in enumerate(chips):
                src_chip = 2 * cx + cy
                copy(a, k, half(a, src_chip, c), src_chip, c, (x, y, c)).wait_recv()
                fwd = copy(a, 3 + k, half(a, src_chip, c), src_chip, c, sibling)
                fwd.start()
                sends.append(fwd)
        for a in range(n):
            for k, (cx, cy) in enumerate(chips):
                src_chip = 2 * cx + cy
                copy(a, 3 + k, half(a, src_chip, 1 - c), src_chip, 1 - c, (x, y, c)).wait_recv()
        for cp in sends:
            cp.wait_send()
        for cp in started:
            cp.wait()

    return pl.pallas_call(
        body, name="gather_weights",
        in_specs=[_ANY] * n, out_specs=[_ANY] * n,
        out_shape=[jax.ShapeDtypeStruct((N_CHIPS,) + s_.shape, s_.dtype) for s_ in shards],
        scratch_shapes=[pltpu.SemaphoreType.DMA((n, 6)), pltpu.SemaphoreType.DMA((n, 6)), pltpu.SemaphoreType.DMA((n,))],

    )(*shards)


def _gather_weights_async(shards):
    n = len(shards)
    srcs = [jax.new_ref(s_, memory_space=pltpu.MemorySpace.HBM) for s_ in shards]
    outs = [jax.empty_ref(jax.ShapeDtypeStruct((N_CHIPS,) + s_.shape, s_.dtype), memory_space=pltpu.MemorySpace.HBM)
            for s_ in shards]

    @pl.kernel(mesh=plsc.ScalarSubcoreMesh(axis_name="seq", num_cores=1), name="gather_weights_async",
               scratch_types=(pltpu.SemaphoreType.DMA((n, 6)), pltpu.SemaphoreType.DMA((n, 6)),
                              pltpu.SemaphoreType.DMA((n,))),
               compiler_params=pltpu.CompilerParams(collective_id=1))
    def launch(send_sems, recv_sems, local_sems):
        x, y, c, chips = _place()
        me = 2 * x + y
        sibling = (x, y, 1 - c)
        barrier = pltpu.get_barrier_semaphore()
        for peer in [sibling] + [(*chip, c) for chip in chips]:
            pl.semaphore_signal(barrier, inc=1, device_id=peer, device_id_type=MESH)
        pl.semaphore_wait(barrier, 4)

        def half(a, chip, h):
            hw = srcs[a].shape[1] // 2
            return outs[a].at[chip, :, pl.ds(pl.multiple_of(h * hw, LANES), hw)]

        def copy(a, k, src, chip, h, to):
            return pltpu.make_async_remote_copy(
                src_ref=src, dst_ref=half(a, chip, h), send_sem=send_sems.at[a, k], recv_sem=recv_sems.at[a, k],
                device_id=to, device_id_type=MESH)

        locals_, sends = [], []
        for a in range(n):
            local = pltpu.make_async_copy(srcs[a], outs[a].at[me], local_sems.at[a])
            local.start()
            locals_.append(local)
            hw = srcs[a].shape[1] // 2
            mine = srcs[a].at[:, pl.ds(pl.multiple_of(c * hw, LANES), hw)]
            for k, chip in enumerate(chips):
                cp = copy(a, k, mine, me, c, (*chip, c))
                cp.start()
                sends.append(cp)
        for a in range(n):
            for k, (cx, cy) in enumerate(chips):
                src_chip = 2 * cx + cy
                copy(a, k, half(a, src_chip, c), src_chip, c, (x, y, c)).wait_recv()
                fwd = copy(a, 3 + k, half(a, src_chip, c), src_chip, c, sibling)
                fwd.start()
                sends.append(fwd)
        for a in range(n):
            for k, (cx, cy) in enumerate(chips):
                src_chip = 2 * cx + cy
                copy(a, 3 + k, half(a, src_chip, 1 - c), src_chip, 1 - c, (x, y, c)).wait_recv()
        for cp in sends:
            cp.wait_send()
        for cp in locals_:
            cp.wait()

    launch()
    return [o[...] for o in outs]


def _on_sequencer(name, cid, inputs, out_shapes, sem_types, peers, body):
    srcs = [jax.new_ref(a, memory_space=pltpu.MemorySpace.HBM) for a in inputs]
    outs = [jax.empty_ref(sd, memory_space=pltpu.MemorySpace.HBM) for sd in out_shapes]

    @pl.kernel(mesh=plsc.ScalarSubcoreMesh(axis_name="seq", num_cores=1), name=name, scratch_types=tuple(sem_types),
               compiler_params=pltpu.CompilerParams(collective_id=cid))
    def launch(*sems):
        x, y, c, chips = _place()
        barrier = pltpu.get_barrier_semaphore()
        ps = peers(x, y, c, chips)
        for peer in ps:
            pl.semaphore_signal(barrier, inc=1, device_id=peer, device_id_type=MESH)
        pl.semaphore_wait(barrier, len(ps))
        body(srcs, outs, *sems)

    launch()
    return [o[...] for o in outs]


def _sibling_only(x, y, c, chips):
    return [(x, y, 1 - c)]


def _same_core_of_other_chips(x, y, c, chips):
    return [(*chip, c) for chip in chips]


def _swap_halves_to_sibling(name, cid, grads):
    n = len(grads)

    def body(ins, outs, send_sems, recv_sems):
        x, y, c, _ = _place()
        cps = []
        for a in range(n):
            hw = ins[a].shape[2] // 2
            src = ins[a].at[:, :, pl.ds(pl.multiple_of((1 - c) * hw, LANES), hw)]
            cp = pltpu.make_async_remote_copy(src_ref=src, dst_ref=outs[a], send_sem=send_sems.at[a],
                                              recv_sem=recv_sems.at[a], device_id=(x, y, 1 - c), device_id_type=MESH)
            cp.start()
            cps.append(cp)
        for cp in cps:
            cp.wait()

    return _on_sequencer(
        name, cid, grads, [jax.ShapeDtypeStruct(g.shape[:2] + (g.shape[2] // 2,), g.dtype) for g in grads],
        [pltpu.SemaphoreType.DMA((n,)), pltpu.SemaphoreType.DMA((n,))], _sibling_only, body)


def _exchange_quarters(name, cid, parts):
    n = len(parts)

    def body(ins, outs, send_sems, recv_sems):
        x, y, c, chips = _place()
        cps = []
        for a in range(n):
            for k, (cx, cy) in enumerate(chips):
                cp = pltpu.make_async_remote_copy(
                    src_ref=ins[a].at[2 * cx + cy], dst_ref=outs[a].at[k], send_sem=send_sems.at[a, k],
                    recv_sem=recv_sems.at[a, k], device_id=(cx, cy, c), device_id_type=MESH)
                cp.start()
                cps.append(cp)
        for cp in cps:
            cp.wait()

    return _on_sequencer(
        name, cid, parts, [jax.ShapeDtypeStruct((3,) + p_.shape[1:], p_.dtype) for p_ in parts],
        [pltpu.SemaphoreType.DMA((n, 3)), pltpu.SemaphoreType.DMA((n, 3))], _same_core_of_other_chips, body)


def _swap_final_halves(name, cid, halves):
    n = len(halves)

    def body(ins, outs, send_sems, recv_sems):
        x, y, c, _ = _place()
        cps = []
        for a in range(n):
            cp = pltpu.make_async_remote_copy(src_ref=ins[a], dst_ref=outs[a], send_sem=send_sems.at[a],
                                              recv_sem=recv_sems.at[a], device_id=(x, y, 1 - c), device_id_type=MESH)
            cp.start()
            cps.append(cp)
        for cp in cps:
            cp.wait()

    return _on_sequencer(
        name, cid, halves, [jax.ShapeDtypeStruct(h.shape, h.dtype) for h in halves],
        [pltpu.SemaphoreType.DMA((n,)), pltpu.SemaphoreType.DMA((n,))], _sibling_only, body)


def _add_half(name, grad, recv):
    nchip, r, cfull = grad.shape
    hw = cfull // 2
    tile = _pick(r, max(BF16_ROWS, (256 * 1024 // hw) // BF16_ROWS * BF16_ROWS), q=BF16_ROWS)
    c = lax.axis_index("c")

    def body(c_ref, g_ref, r_ref, o_ref):
        o_ref[...] = _bf(g_ref[...] + r_ref[...])

    return pl.pallas_call(
        body, name=name,
        grid_spec=pltpu.PrefetchScalarGridSpec(
            num_scalar_prefetch=1, grid=(nchip, r // tile),
            in_specs=[pl.BlockSpec((1, tile, hw), lambda k, i, cr: (k, i, cr[0])),
                      pl.BlockSpec((1, tile, hw), lambda k, i, cr: (k, i, 0))],
            out_specs=pl.BlockSpec((1, tile, hw), lambda k, i, cr: (k, i, 0))),
        out_shape=jax.ShapeDtypeStruct((nchip, r, hw), BF16), compiler_params=_params("parallel", "parallel"),
    )(c.reshape(1).astype(jnp.int32), grad, recv)


def _add_quarters(name, part, recv):
    _, r, hw = part.shape
    tile = _pick(r, max(BF16_ROWS, (256 * 1024 // hw) // BF16_ROWS * BF16_ROWS), q=BF16_ROWS)
    me = 2 * lax.axis_index("x") + lax.axis_index("y")

    def body(me_ref, p_ref, r_ref, o_ref):
        f = lambda v: v.astype(F32)
        o_ref[...] = ((f(p_ref[0]) + f(r_ref[0])) + f(r_ref[1])) + f(r_ref[2])

    return pl.pallas_call(
        body, name=name,
        grid_spec=pltpu.PrefetchScalarGridSpec(
            num_scalar_prefetch=1, grid=(r // tile,),
            in_specs=[pl.BlockSpec((1, tile, hw), lambda i, mr: (mr[0], i, 0)),
                      pl.BlockSpec((3, tile, hw), lambda i, mr: (0, i, 0))],
            out_specs=pl.BlockSpec((tile, hw), lambda i, mr: (i, 0))),
        out_shape=jax.ShapeDtypeStruct((r, hw), F32), compiler_params=_params("parallel"),
    )(me.reshape(1).astype(jnp.int32), part, recv)


class _ReduceScatter:
    def __init__(self, tag, first_cid, grads):
        self.tag, self.cid = tag, first_cid
        self.stacks = [g.reshape(N_CHIPS, g.shape[0] // N_CHIPS, g.shape[1]) for g in grads]

    def start(self, anchor):
        self.stacks, anchor = lax.optimization_barrier((self.stacks, anchor))
        self.recv = _swap_halves_to_sibling(f"rs_swap_halves_{self.tag}", self.cid, self.stacks)
        return anchor

    def exchange(self, anchor):
        parts = [_add_half(f"rs_add_half_{self.tag}{a}", g, r) for a, (g, r) in enumerate(zip(self.stacks, self.recv))]
        self.parts, anchor = lax.optimization_barrier((parts, anchor))
        self.quarters = _exchange_quarters(f"rs_exchange_{self.tag}", self.cid + 1, self.parts)
        return anchor

    def join(self, anchor):
        halves = [_add_quarters(f"rs_add_quarters_{self.tag}{a}", p_, q_)
                  for a, (p_, q_) in enumerate(zip(self.parts, self.quarters))]
        self.halves, anchor = lax.optimization_barrier((halves, anchor))
        self.others = _swap_final_halves(f"rs_swap_final_{self.tag}", self.cid + 2, self.halves)
        return anchor

    def result(self):
        south = lax.axis_index("c") == 0
        return [jnp.concatenate([jnp.where(south, h, o), jnp.where(south, o, h)], axis=1)
                for h, o in zip(self.halves, self.others)]


def _allgather_small(pack):
    m_per, n = pack.shape

    def body(x_ref, out_ref, send_sems, recv_sems, local_sem):
        x, y, c, chips = _place()
        me, sibling = (x, y, c), (x, y, 1 - c)

        def rows(px, py, pc):
            return out_ref.at[pl.ds(pl.multiple_of((4 * px + 2 * py + pc) * m_per, SUBLANES), m_per), :]

        def copy(k, block, to, src=None):
            return pltpu.make_async_remote_copy(
                src_ref=rows(*block) if src is None else src, dst_ref=rows(*block),
                send_sem=send_sems.at[k], recv_sem=recv_sems.at[k], device_id=to, device_id_type=MESH)

        mine = pltpu.make_async_copy(x_ref, rows(*me), local_sem)
        mine.start()
        first = [copy(0, me, sibling, src=x_ref)]
        first += [copy(1 + j, me, (*chip, c), src=x_ref) for j, chip in enumerate(chips)]
        for cp in first:
            cp.start()
        passed = [copy(4 + j, (*chip, c), sibling) for j, chip in enumerate(chips)]
        for j, chip in enumerate(chips):
            copy(1 + j, (*chip, c), me).wait_recv()
            passed[j].start()
        copy(0, sibling, me).wait_recv()
        for j, chip in enumerate(chips):
            copy(4 + j, (*chip, 1 - c), me).wait_recv()
        for cp in first + passed:
            cp.wait_send()
        mine.wait()

    return pl.pallas_call(
        body, name="allgather_small_grads",
        out_shape=jax.ShapeDtypeStruct((N_DEV * m_per, n), pack.dtype),
        in_specs=[pl.BlockSpec(memory_space=pltpu.VMEM)], out_specs=pl.BlockSpec(memory_space=pltpu.VMEM),
        scratch_shapes=[pltpu.SemaphoreType.DMA((7,)), pltpu.SemaphoreType.DMA((7,)), pltpu.SemaphoreType.DMA],
        compiler_params=pltpu.CompilerParams(vmem_limit_bytes=VMEM_LIMIT),
    )(pack)


def _sum_devices(packs, m_per):
    tile = _pick(m_per, 512, q=SUBLANES)
    nt = m_per // tile

    def body(*refs):
        acc = refs[0][...]
        for r in refs[1:N_DEV]:
            acc = acc + r[...]
        refs[N_DEV][...] = acc

    return pl.pallas_call(
        body, name="sum_small_grads", grid=(nt,),
        in_specs=[pl.BlockSpec((tile, LANES), functools.partial(lambda i, k: (k * nt + i, 0), k=k)) for k in range(N_DEV)],
        out_specs=_nat(tile, LANES), out_shape=jax.ShapeDtypeStruct((m_per, LANES), F32),
        compiler_params=_params("parallel"),
    )(*([packs] * N_DEV))


def _tail_fwd(tag, alpha, h_in, adds, mix_gate, ln1, ln2, p_l, w, want_perm):
    h_mid, xh1, rs1, h_mid_b, _ = _ln_fwd(f"ln1_fwd_{tag}", alpha, h_in, adds, mix_gate, *ln1)
    gp = _matmul(f"ple_gate_fwd_{tag}", h_mid_b, w['wg'], 'nn')
    pw = _matmul(f"ple_proj_fwd_{tag}", p_l, w['plet'], 'nt')
    gu = _matmul(f"ffn_in_fwd_{tag}", h_mid_b, w['wit'], 'nt', tn=1408)
    act = _swiglu_fwd(f"swiglu_fwd_{tag}", gu)
    ffn = _matmul(f"ffn_out_fwd_{tag}", act, w['wo'], 'nn', tk=2816)
    h_out, xh2, rs2, _, h_perm = _ln_fwd(f"ln2_fwd_{tag}", alpha, h_mid, [(ffn, 'nat')],
                                         ('nat', (pw, 1, 0), (gp, 1, 0)), *ln2, want_perm=want_perm)
    saved = dict(h_mid_b=h_mid_b, xh1=xh1, rs1=rs1, gp=gp, pw=pw, gu=gu, act=act, xh2=xh2, rs2=rs2)
    return h_out, h_perm, saved


def _tail_bwd(tag, alpha, dparts, sv, ln1_g, ln2_g, p_l, w, mix_gate):
    d = sv['h_mid_b'].shape[1]
    dz2, dz2b, dgate, dg2, db2 = _ln_bwd(f"ln2_bwd_{tag}", dparts, sv['xh2'], sv['rs2'], ln2_g,
                                         gate=('nat', (sv['pw'], 1, 0), (sv['gp'], 1, 0)))
    grads = dict(ln2_g=dg2, ln2_b=db2)
    grads['plet'] = _matmul(f"ple_proj_dw_{tag}", dgate, p_l, 'tn', a_win=(0, d))
    grads['wg'] = _matmul(f"ple_gate_dw_{tag}", sv['h_mid_b'], dgate, 'tn', b_win=(d, d))
    dx_gate = _matmul(f"ple_gate_dx_{tag}", dgate, w['wg'], 'nt', a_win=(d, d))
    dact = _matmul(f"ffn_out_dx_{tag}", dz2b, w['wo'], 'nt', out_dtype=BF16, tn=1408)
    grads['wo'] = _matmul(f"ffn_out_dw_{tag}", sv['act'], dz2b, 'tn', tm=1408)
    dgu = _swiglu_bwd(f"swiglu_bwd_{tag}", sv['gu'], dact)
    grads['wit'] = _matmul(f"ffn_in_dw_{tag}", dgu, sv['h_mid_b'], 'tn')
    dx_ffn = _matmul(f"ffn_in_dx_{tag}", dgu, w['wit'], 'nn', tk=2816)
    res = _ln_bwd(f"ln1_bwd_{tag}", [(dz2, 'nat', alpha), (dx_gate, 'nat', 1.0), (dx_ffn, 'nat', 1.0)],
                  sv['xh1'], sv['rs1'], ln1_g, gate=mix_gate)
    grads['ln1_g'], grads['ln1_b'] = res[-2], res[-1]
    return res[:-2], grads


def kernel(x, p, positions, attn_w_in, mla_q_norm, mla_w_q_b, mla_kv_norm, mla_w_kv_b, attn_w_out, s5_a_re, s5_a_im, s5_log_dt, s5_b_re, s5_b_im, s5_c_re, s5_c_im, s5_d, s5_w_glu, ln1_g, ln1_b, ffn_w_in, ffn_w_out, ple_w, ple_gate_w, ln2_g, ln2_b, loss_target, m_attn_w_in, m_mla_q_norm, m_mla_w_q_b, m_mla_kv_norm, m_mla_w_kv_b, m_attn_w_out, m_s5_a_re, m_s5_a_im, m_s5_log_dt, m_s5_b_re, m_s5_b_im, m_s5_c_re, m_s5_c_im, m_s5_d, m_s5_w_glu, m_ln1_g, m_ln1_b, m_ffn_w_in, m_ffn_w_out, m_ple_w, m_ple_gate_w, m_ln2_g, m_ln2_b, v_attn_w_in, v_mla_q_norm, v_mla_w_q_b, v_mla_kv_norm, v_mla_w_kv_b, v_attn_w_out, v_s5_a_re, v_s5_a_im, v_s5_log_dt, v_s5_b_re, v_s5_b_im, v_s5_c_re, v_s5_c_im, v_s5_d, v_s5_w_glu, v_ln1_g, v_ln1_b, v_ffn_w_in, v_ffn_w_out, v_ple_w, v_ple_gate_w, v_ln2_g, v_ln2_b):
    weights = dict(attn_w_in=attn_w_in, mla_q_norm=mla_q_norm, mla_w_q_b=mla_w_q_b, mla_kv_norm=mla_kv_norm,
                   mla_w_kv_b=mla_w_kv_b, attn_w_out=attn_w_out, s5_a_re=s5_a_re, s5_a_im=s5_a_im, s5_log_dt=s5_log_dt,
                   s5_b_re=s5_b_re, s5_b_im=s5_b_im, s5_c_re=s5_c_re, s5_c_im=s5_c_im, s5_d=s5_d, s5_w_glu=s5_w_glu,
                   ln1_g=ln1_g, ln1_b=ln1_b, ffn_w_in=ffn_w_in, ffn_w_out=ffn_w_out, ple_w=ple_w, ple_gate_w=ple_gate_w,
                   ln2_g=ln2_g, ln2_b=ln2_b)
    m_in = dict(attn_w_in=m_attn_w_in, mla_q_norm=m_mla_q_norm, mla_w_q_b=m_mla_w_q_b, mla_kv_norm=m_mla_kv_norm,
                mla_w_kv_b=m_mla_w_kv_b, attn_w_out=m_attn_w_out, s5_a_re=m_s5_a_re, s5_a_im=m_s5_a_im,
                s5_log_dt=m_s5_log_dt, s5_b_re=m_s5_b_re, s5_b_im=m_s5_b_im, s5_c_re=m_s5_c_re, s5_c_im=m_s5_c_im,
                s5_d=m_s5_d, s5_w_glu=m_s5_w_glu, ln1_g=m_ln1_g, ln1_b=m_ln1_b, ffn_w_in=m_ffn_w_in,
                ffn_w_out=m_ffn_w_out, ple_w=m_ple_w, ple_gate_w=m_ple_gate_w, ln2_g=m_ln2_g, ln2_b=m_ln2_b)
    v_in = dict(attn_w_in=v_attn_w_in, mla_q_norm=v_mla_q_norm, mla_w_q_b=v_mla_w_q_b, mla_kv_norm=v_mla_kv_norm,
                mla_w_kv_b=v_mla_w_kv_b, attn_w_out=v_attn_w_out, s5_a_re=v_s5_a_re, s5_a_im=v_s5_a_im,
                s5_log_dt=v_s5_log_dt, s5_b_re=v_s5_b_re, s5_b_im=v_s5_b_im, s5_c_re=v_s5_c_re, s5_c_im=v_s5_c_im,
                s5_d=v_s5_d, s5_w_glu=v_s5_w_glu, ln1_g=v_ln1_g, ln1_b=v_ln1_b, ffn_w_in=v_ffn_w_in,
                ffn_w_out=v_ffn_w_out, ple_w=v_ple_w, ple_gate_w=v_ple_gate_w, ln2_g=v_ln2_g, ln2_b=v_ln2_b)
    names = list(weights)

    s, d = x.shape[1], x.shape[2]
    depth = ln1_g.shape[0]
    assert depth == 2
    alpha = (2.0 * depth) ** 0.25
    ql, kvl = mla_q_norm.shape[1], mla_kv_norm.shape[1]
    in_cols = N_CHIPS * attn_w_in.shape[2]
    heads = N_CHIPS * mla_w_q_b.shape[2] // (NOPE + ROPE)
    hps = heads // N_CHIPS
    dw = (in_cols - ql - kvl - ROPE) // 3
    dh = dw // DHD
    assert ql % LANES == 0 and kvl == ql and dw % DHD == 0 and heads % N_CHIPS == 0
    ngroups, nstate = s5_a_re.shape[1], s5_a_re.shape[2]
    assert nstate == S5_STATE and ngroups * S5_GROUP == d and d % LANES == 0
    nsl = d // LANES
    seg_len = s // SUBLANES
    n_sq = seg_len.bit_length() - 1
    assert 1 << n_sq == seg_len, "the segment length of the S5 scan must be a power of two"
    for window, dil in DIL_BRANCHES:
        assert window // dil == DIL_STEPS and (s // dil) % DIL_STEPS == 0
    me = 2 * lax.axis_index("x") + lax.axis_index("y")

    xb = x[0]
    target = loss_target[0]
    p_layers = [p[0, 0], p[1, 0]]
    pos = positions[0].astype(F32).reshape(s, 1)
    inv_freq = ROPE_THETA ** (-jnp.arange(ROPE // 2, dtype=F32) / (ROPE // 2))
    invf = jnp.concatenate([inv_freq, inv_freq, jnp.zeros((LANES - ROPE,), F32)]).reshape(1, LANES)
    slopes = 2.0 ** (-8.0 * jnp.arange(1, dh + 1, dtype=F32) / dh)
    slopes = jnp.broadcast_to(jnp.repeat(slopes, SUBLANES)[:, None], (dh * SUBLANES, LANES))

    wqb_t = mla_w_q_b[0].T.reshape(hps, NOPE + ROPE, ql)
    wqb_t = jnp.pad(wqb_t, ((0, 0), (0, QK_PAD - NOPE - ROPE), (0, 0))).reshape(hps * QK_PAD, ql)
    d_cols = max(d // N_CHIPS, 2 * LANES)
    d_pad = jnp.zeros((SUBLANES, d_cols), F32).at[0, :d // N_CHIPS].set(s5_d[0])
    shards = [_bf(attn_w_in[0].T), _bf(wqb_t), _bf(mla_w_kv_b[0].T), _bf(attn_w_out[0]), _bf(s5_w_glu[0].T)]
    for l in range(depth):
        shards += [_bf(ffn_w_in[l].T), _bf(ffn_w_out[l]), _bf(ple_w[l].T), _bf(ple_gate_w[l])]
    shards.append(d_pad)
    first, later = lax.optimization_barrier((list(_gather_weights(shards[:3])), shards[3:]))
    gathered = first + _gather_weights_async(later)
    full = [g.reshape(N_CHIPS * g.shape[1], g.shape[2]) for g in gathered]
    win_t, wqb_t_f, wkv_t, wout, wglu_t = full[:5]
    lw = [dict(wit=full[5 + 4 * l], wo=full[6 + 4 * l], plet=full[7 + 4 * l], wg=full[8 + 4 * l]) for l in range(depth)]
    dvec = full[-1].reshape(N_CHIPS, SUBLANES, d_cols)[:, 0, :d // N_CHIPS].reshape(1, d)
    lat = ql + kvl
    win_t = jnp.concatenate([win_t[:lat + ROPE], jnp.zeros((LANES - ROPE, d), BF16), win_t[lat + ROPE:]], axis=0)
    kpe_cb = lat // LANES
    q_cb = kpe_cb + 1
    a_cb = heads * VDIM // LANES

    xbb = _bf(xb)
    proj = _matmul("attn_in_fwd", xbb, win_t, 'nt', tn=1408)
    nrm = _rms_fwd(proj, ql, kvl, mla_q_norm[0], mla_kv_norm[0])
    q_raw = _matmul("mla_q_up_fwd", nrm, wqb_t_f, 'nt', a_win=(0, ql))
    kv = _matmul("mla_kv_up_fwd", nrm, wkv_t, 'nt', a_win=(ql, kvl))
    qf, kf, vv = _rope_prep(q_raw, kv, proj, kpe_cb, pos, invf, heads)
    out_a, lse_a = _mla_fwd(qf, kf, vv.T, heads)
    outs, lses = [], []
    for window, dil in DIL_BRANCHES:
        o_g, l_g = _dil_fwd(f"dil_fwd_d{dil}", proj, slopes, dil, dh, q_cb)
        outs.append(o_g)
        lses.append(l_g)
    att, out_b, lse_b = _dil_merge(out_a, outs, lses)
    mix0 = _matmul("attn_out_fwd", att, wout, 'nn')
    h2, h2p, sv0 = _tail_fwd("l0", alpha, xb, [(mix0, 'nat')], None, (ln1_g[0], ln1_b[0]), (ln2_g[0], ln2_b[0]),
                             p_layers[0], lw[0], want_perm=True)

    rep = lambda a: jnp.repeat(a, S5_GROUP, axis=0)
    ag = (s5_a_re[0], s5_a_im[0], jnp.broadcast_to(s5_log_dt[0][:, None], (ngroups, nstate)))
    a16 = tuple(rep(a) for a in ag)
    b16 = tuple(b[0].transpose(0, 2, 1).reshape(ngroups * S5_GROUP, nstate) for b in (s5_b_re, s5_b_im))
    abr, abi, apr, api, bbr, bbi = _s5_discretise(*a16, *b16, n_sq)
    ab_tile = _slab_tile(abr[::S5_GROUP], abi[::S5_GROUP], nsl)
    ap_tile = _slab_tile(apr[::S5_GROUP], api[::S5_GROUP], nsl)
    bblk = _bf(_slab_in_matrix(bbr.reshape(ngroups, S5_GROUP, nstate), bbi.reshape(ngroups, S5_GROUP, nstate), nsl))
    cblk = _bf(_slab_out_matrix(s5_c_re[0], s5_c_im[0], nsl))
    xloc, ends = _s5_pass1(h2p, bblk, ab_tile)
    cinx = _s5_carry("s5_carry_fwd", ends, ap_tile, False)
    xtrue, ypre, zg = _s5_pass2(xloc, cinx, ab_tile, cblk, h2p, dvec)
    vg = _matmul("s5_glu_fwd", zg, wglu_t, 'nt')
    glu_gate = ('perm', (vg, 2, 0), (vg, 2, 1))
    h4, _, sv1 = _tail_fwd("l1", alpha, h2, [], glu_gate, (ln1_g[1], ln1_b[1]), (ln2_g[1], ln2_b[1]),
                           p_layers[1], lw[1], want_perm=False)
    loss = lax.psum(jnp.sum(_loss_partial(h4, target)), ("x", "y", "c"))

    (dz1_1, _, dvg), g1 = _tail_bwd("l1", alpha, [(h4, 'nat', 1.0 / d), (target, 'nat', -1.0 / d)], sv1, ln1_g[1],
                                    ln2_g[1], p_layers[1], lw[1], glu_gate)
    d_wglu_t = _matmul("s5_glu_dw", dvg, zg, 'tn')
    dzg = _matmul("s5_glu_dx", dvg, wglu_t, 'nn')
    rs_l1 = _ReduceScatter("l1", 2, [d_wglu_t, g1['wit'], g1['wo'], g1['plet'], g1['wg']])
    dzg = rs_l1.start(dzg)
    lamloc, starts, dy, dd = _s5_bwd_pass1(dzg, ypre, cblk, ab_tile, h2p)
    cinl = _s5_carry("s5_carry_bwd", starts, ap_tile, True)
    du_p, d_bblk, d_cblk, d_ab = _s5_bwd_pass2(lamloc, cinl, ab_tile, xtrue, cinx, h2p, dy, bblk, dvec)
    gbb = _slab_in_extract(d_bblk, nsl)
    g_c_re, g_c_im = _slab_out_extract(d_cblk, nsl)
    d_ab = d_ab[::SUBLANES]
    gab = (d_ab[:, :SLAB_COLS].reshape(ngroups, nstate), d_ab[:, SLAB_COLS:].reshape(ngroups, nstate))
    g_a_re, g_a_im, g_log_dt, g_b_re, g_b_im = _s5_discretise_bwd(a16, b16, ag, gab, gbb)
    unt = lambda b: b.reshape(ngroups, S5_GROUP, nstate).transpose(0, 2, 1)

    du_p = rs_l1.exchange(du_p)
    (dz1_0, dz1_0b), g0 = _tail_bwd("l0", alpha, [(dz1_1, 'nat', alpha), (du_p, 'perm', 1.0)], sv0, ln1_g[0], ln2_g[0],
                                    p_layers[0], lw[0], None)
    dz1_0b = rs_l1.join(dz1_0b)
    rs_l0 = _ReduceScatter("l0", 5, [g0['wit'], g0['wo'], g0['plet'], g0['wg']])
    dz1_0b = rs_l0.start(dz1_0b)
    datt = _matmul("attn_out_dx", dz1_0b, wout, 'nt')
    d_wout = _matmul("attn_out_dw", att, dz1_0b, 'tn')
    do, delta, delta_t = _attn_bwd_prep(datt, out_a, out_b)
    dqf, dkf, dvv = _mla_bwd(qf, kf, vv, do, lse_a, delta_t, heads, 0)
    dqf = rs_l0.exchange(dqf)
    dq_raw, dkv, dkpe = _rope_unprep(dqf, dkf, dvv, pos, invf, heads)
    d_wqb_t = _matmul("mla_q_up_dw", dq_raw, nrm, 'tn', b_win=(0, ql))
    d_wkv_t = _matmul("mla_kv_up_dw", dkv, nrm, 'tn', b_win=(ql, kvl))
    dnq = _matmul("mla_q_up_dx", dq_raw, wqb_t_f, 'nn')
    dnkv = _matmul("mla_kv_up_dx", dkv, wkv_t, 'nn')
    dqs, dks, dvs = [], [], []
    for window, dil in DIL_BRANCHES:
        dqs.append(_dil_bwd_dq(f"dil_bwd_dq_d{dil}", proj, slopes, do, lse_b, delta, dil, dh, q_cb, a_cb))
        dk_g, dv_g = _dil_bwd_dkv(f"dil_bwd_dkv_d{dil}", proj, slopes, do, lse_b, delta, dil, dh, q_cb, a_cb)
        dks.append(dk_g)
        dvs.append(dv_g)
    dkpe = rs_l0.join(dkpe)
    dproj, g_gq, g_gkv = _dproj_assemble(proj, dnq, dnkv, dkpe, dqs, dks, dvs, mla_q_norm[0], mla_kv_norm[0], ql)
    d_win_t = _matmul("attn_in_dw", dproj, xbb, 'tn', tm=1408)
    dx_attn = _matmul("attn_in_dx", dproj, win_t, 'nn')
    grad_x = _axpy("grad_x", alpha, dz1_0, dx_attn)

    d_win_t = jnp.concatenate([d_win_t[:lat + ROPE], d_win_t[lat + LANES:]], axis=0)
    rs_at = _ReduceScatter("attn", 8, [d_win_t, d_wqb_t, d_wkv_t, d_wout])
    grad_x = rs_at.start(grad_x)
    grad_x = rs_at.exchange(grad_x)
    grad_x = rs_at.join(grad_x)
    r_win, r_wqb, r_wkv, r_wout = rs_at.result()
    r_wglu, r_wit1, r_wo1, r_plet1, r_wg1 = rs_l1.result()
    r_wit0, r_wo0, r_plet0, r_wg0 = rs_l0.result()
    r_wqb = r_wqb.reshape(hps, QK_PAD, ql)[:, :NOPE + ROPE].reshape(hps * (NOPE + ROPE), ql)
    grads = dict(attn_w_in=r_win.T[None], mla_w_q_b=r_wqb.T[None], mla_w_kv_b=r_wkv.T[None], attn_w_out=r_wout[None],
                 s5_w_glu=r_wglu.T[None],
                 ffn_w_in=jnp.stack([r_wit0.T, r_wit1.T]), ffn_w_out=jnp.stack([r_wo0, r_wo1]),
                 ple_w=jnp.stack([r_plet0.T, r_plet1.T]), ple_gate_w=jnp.stack([r_wg0, r_wg1]))

    small = dict(mla_q_norm=g_gq, mla_kv_norm=g_gkv, s5_a_re=g_a_re, s5_a_im=g_a_im, s5_log_dt=g_log_dt,
                 s5_b_re=unt(g_b_re), s5_b_im=unt(g_b_im), s5_c_re=g_c_re, s5_c_im=g_c_im, s5_d=dd[0],
                 ln1_g=jnp.stack([g0['ln1_g'], g1['ln1_g']]), ln1_b=jnp.stack([g0['ln1_b'], g1['ln1_b']]),
                 ln2_g=jnp.stack([g0['ln2_g'], g1['ln2_g']]), ln2_b=jnp.stack([g0['ln2_b'], g1['ln2_b']]))
    flat = jnp.concatenate([v_.reshape(-1) for v_ in small.values()])
    m_per = -(-flat.shape[0] // (LANES * SUBLANES)) * SUBLANES
    pack = jnp.pad(flat, (0, m_per * LANES - flat.shape[0])).reshape(m_per, LANES)
    total = _sum_devices(_allgather_small(pack), m_per).reshape(-1)
    off = 0
    for k_, v_ in small.items():
        n_ = v_.size
        piece = total[off:off + n_]
        off += n_
        if k_ == 's5_d':
            grads[k_] = lax.dynamic_slice(piece, (me * (d // N_CHIPS),), (d // N_CHIPS,)).reshape(weights[k_].shape)
        else:
            grads[k_] = piece.reshape(weights[k_].shape)

    deltas, new_m, new_v = {}, {}, {}
    for k_ in names:
        w_ = weights[k_]
        shape = w_.shape
        if w_.ndim == 3 and w_.shape[-1] >= LANES:
            two_d = (shape[0] * shape[1], shape[2])
        elif w_.ndim == 4:
            two_d = (shape[0] * shape[1], shape[2] * shape[3])
        else:
            two_d = (1, w_.size) if w_.ndim == 2 and shape[0] == 1 else (shape[0], w_.size // shape[0])
        dl, nm, nv = _adamw(f"adamw_{k_}", w_.reshape(two_d), grads[k_].reshape(two_d), m_in[k_].reshape(two_d),
                            v_in[k_].reshape(two_d))
        deltas[k_], new_m[k_], new_v[k_] = dl.reshape(shape), nm.reshape(shape), nv.reshape(shape)

    return (loss, grad_x[None], *[grads[k_] for k_ in names], *[deltas[k_] for k_ in names],
            *[new_m[k_] for k_ in names], *[new_v[k_] for k_ in names])
```

```python
import functools
import math

import jax
import jax.numpy as jnp
from jax import lax
from jax.experimental import pallas as pl
from jax.experimental.pallas import tpu as pltpu
from jax.experimental.pallas import tpu_sc as plsc

F32 = jnp.float32
BF16 = jnp.bfloat16
MESH = pl.DeviceIdType.MESH

LANES = 128
SUBLANES = 8
BF16_ROWS = 16
VMEM_LIMIT = 48 * 2 ** 20
N_CHIPS = 4
N_DEV = 8

NOPE = 128
ROPE = 64
VDIM = 128
QK_PAD = 256
DHD = 128
DIL_STEPS = 128
DIL_BRANCHES = ((128, 1), (512, 4), (2048, 16))
ROPE_THETA = 10000.0
S5_GROUP = 16
S5_STATE = 64
SLAB_GROUPS = LANES // S5_GROUP
SLAB_COLS = SLAB_GROUPS * S5_STATE
NEG = -1e30
LN_EPS = 1e-5
RMS_EPS = 1e-6

ADAM_LR = 0.001
ADAM_B1 = 0.9
ADAM_B2 = 0.999
ADAM_EPS = 1e-08
ADAM_WD = 0.01
ADAM_STEP = 10

NN = ((1,), (0,))
NT = ((1,), (1,))
TN = ((0,), (0,))


def _dot(a, b, dims):
    return lax.dot_general(a, b, (dims, ((), ())), preferred_element_type=F32)


def _bf(v):
    return v.astype(BF16)


def _pick(n, target, q=LANES, also=0):
    g = math.gcd(n, also) if also else n
    if g <= target and g == n:
        return n
    best = None
    for t in range(q, min(g, target) + 1, q):
        if g % t == 0:
            best = t
    assert best is not None, (n, target, q, also)
    return best


def _params(*sem):
    return pltpu.CompilerParams(dimension_semantics=sem, vmem_limit_bytes=VMEM_LIMIT)


def _sigmoid(v):
    return 1.0 / (1.0 + jnp.exp(-v))


def _matmul(name, a, b, form, out_dtype=F32, a_win=None, b_win=None, tm=1024, tn=1024, tk=2048):
    c0, aw = a_win if a_win else (0, a.shape[1])
    if form == 'nt':
        assert b_win is None
        n, kdim = b.shape
        d0 = 0
    else:
        kdim = b.shape[0]
        d0, n = b_win if b_win else (0, b.shape[1])
    if form == 'tn':
        m = aw
        assert a.shape[0] == kdim, (name, a.shape, b.shape)
        tm = _pick(m, tm, also=c0)
        tk = _pick(kdim, tk)
        a_off = c0 // tm
    else:
        m = a.shape[0]
        assert aw == kdim, (name, a.shape, b.shape, a_win)
        tm = _pick(m, tm)
        tk = _pick(kdim, tk, also=c0)
        a_off = c0 // tk
    tn = _pick(n, tn, also=d0)
    b_off = d0 // tn
    nk = kdim // tk
    dims = {'nn': NN, 'nt': NT, 'tn': TN}[form]

    def body(a_ref, b_ref, o_ref, *acc):
        prod = _dot(_bf(a_ref[...]), _bf(b_ref[...]), dims)
        if nk == 1:
            o_ref[...] = prod.astype(o_ref.dtype)
            return
        acc_ref, = acc
        k = pl.program_id(2)

        @pl.when(k == 0)
        def _():
            acc_ref[...] = prod

        @pl.when((k > 0) & (k < nk - 1))
        def _():
            acc_ref[...] += prod

        @pl.when(k == nk - 1)
        def _():
            o_ref[...] = (acc_ref[...] + prod).astype(o_ref.dtype)

    if form == 'tn':
        a_spec = pl.BlockSpec((tk, tm), lambda i, j, k: (k, i + a_off))
    else:
        a_spec = pl.BlockSpec((tm, tk), lambda i, j, k: (i, k + a_off))
    if form == 'nt':
        b_spec = pl.BlockSpec((tn, tk), lambda i, j, k: (j, k))
    else:
        b_spec = pl.BlockSpec((tk, tn), lambda i, j, k: (k, j + b_off))
    return pl.pallas_call(
        body, name=name,
        grid=(m // tm, n // tn, nk),
        in_specs=[a_spec, b_spec],
        out_specs=pl.BlockSpec((tm, tn), lambda i, j, k: (i, j)),
        out_shape=jax.ShapeDtypeStruct((m, n), out_dtype),
        scratch_shapes=[pltpu.VMEM((tm, tn), F32)] if nk > 1 else [],
        compiler_params=_params("parallel", "parallel", "arbitrary"),
    )(a, b)


def _nat(tile, width, cb=0):
    return pl.BlockSpec((tile, width), lambda i: (i, cb))


def _perm(tile, width, seg_tiles, ncb=1, cb=0):
    return pl.BlockSpec((tile, width), lambda i: (i % seg_tiles, (i // seg_tiles) * ncb + cb))


def _whole(shape):
    return pl.BlockSpec(shape, lambda i: (0,) * len(shape))


def _perm_view(a):
    s, w = a.shape
    return a.reshape(s // SUBLANES, SUBLANES * w)


def _row_spec(a, layout, tile, width, ncb=1, cb=0):
    if layout == 'nat':
        return a, _nat(tile, width, cb)
    seg_tiles = a.shape[0] // SUBLANES // tile
    return _perm_view(a), _perm(tile, width, seg_tiles, ncb, cb)


def _ln_fwd(name, alpha, a, adds, gate, g, b, want_perm=False, tile=256):
    s, d = a.shape
    n_add = len(adds)
    has_gate = gate is not None

    def body(*refs):
        a_ref = refs[0]
        add_refs = refs[1:1 + n_add]
        pos = 1 + n_add
        if has_gate:
            val_ref, pre_ref = refs[pos], refs[pos + 1]
            pos += 2
        g_ref, b_ref = refs[pos], refs[pos + 1]
        outs = refs[pos + 2:]
        z = alpha * a_ref[...]
        for r in add_refs:
            z = z + r[...]
        if has_gate:
            z = z + val_ref[...] * _sigmoid(pre_ref[...])
        mu = jnp.mean(z, axis=-1, keepdims=True)
        zc = z - mu
        var = jnp.mean(zc * zc, axis=-1, keepdims=True)
        rstd = lax.rsqrt(var + LN_EPS)
        xhat = zc * rstd
        h = xhat * g_ref[...] + b_ref[...]
        outs[0][...] = h
        outs[1][...] = xhat
        outs[2][...] = jnp.broadcast_to(rstd, (tile, LANES))
        outs[3][...] = _bf(h)
        if want_perm:
            outs[4][...] = h

    ins, specs = [a], [_nat(tile, d)]
    for arr, layout in adds:
        x_, sp = _row_spec(arr, layout, tile, d)
        ins.append(x_)
        specs.append(sp)
    if has_gate:
        layout = gate[0]
        for arr, ncb, cb in gate[1:]:
            x_, sp = _row_spec(arr, layout, tile, d, ncb=ncb, cb=cb)
            ins.append(x_)
            specs.append(sp)
    ins += [g.reshape(1, d), b.reshape(1, d)]
    specs += [_whole((1, d)), _whole((1, d))]
    out_shape = [jax.ShapeDtypeStruct((s, d), F32), jax.ShapeDtypeStruct((s, d), F32),
                 jax.ShapeDtypeStruct((s, LANES), F32), jax.ShapeDtypeStruct((s, d), BF16)]
    out_specs = [_nat(tile, d), _nat(tile, d), _nat(tile, LANES), _nat(tile, d)]
    if want_perm:
        seg_tiles = s // SUBLANES // tile
        out_shape.append(jax.ShapeDtypeStruct((s // SUBLANES, SUBLANES * d), F32))
        out_specs.append(_perm(tile, d, seg_tiles))
    res = pl.pallas_call(
        body, name=name, grid=(s // tile,), in_specs=specs, out_specs=out_specs, out_shape=out_shape,
        compiler_params=_params("parallel"),
    )(*ins)
    return res[0], res[1], res[2], res[3], (res[4].reshape(s, d) if want_perm else None)


def _ln_bwd(name, dparts, xhat, rstd, g, gate=None, tile=256):
    s, d = xhat.shape
    n_part = len(dparts)
    coefs = [c for _, _, c in dparts]
    has_gate = gate is not None

    def body(*refs):
        part_refs = refs[:n_part]
        xhat_ref, rstd_ref, g_ref = refs[n_part:n_part + 3]
        pos = n_part + 3
        if has_gate:
            val_ref, pre_ref = refs[pos], refs[pos + 1]
            pos += 2
        outs = list(refs[pos:])
        dz_ref = outs.pop(0)
        dzb_ref = outs.pop(0)
        dgate_ref = outs.pop(0) if has_gate else None
        dg_ref, db_ref = outs
        dh = coefs[0] * part_refs[0][...]
        for c, r in zip(coefs[1:], part_refs[1:]):
            dh = dh + c * r[...]
        xh = xhat_ref[...]
        dxh = dh * g_ref[...]
        m1 = jnp.mean(dxh, axis=-1, keepdims=True)
        m2 = jnp.mean(dxh * xh, axis=-1, keepdims=True)
        dz = rstd_ref[:, 0:1] * (dxh - m1 - xh * m2)
        dz_ref[...] = dz
        dzb_ref[...] = _bf(dz)
        if has_gate:
            sg = _sigmoid(pre_ref[...])
            dval = dz * sg
            dpre = dz * val_ref[...] * sg * (1.0 - sg)
            dgate_ref[...] = jnp.concatenate([_bf(dval), _bf(dpre)], axis=1)

        @pl.when(pl.program_id(0) == 0)
        def _():
            dg_ref[...] = jnp.zeros_like(dg_ref)
            db_ref[...] = jnp.zeros_like(db_ref)

        dg_ref[0:1, :] += jnp.sum(dh * xh, axis=0, keepdims=True)
        db_ref[0:1, :] += jnp.sum(dh, axis=0, keepdims=True)

    ins, specs = [], []
    for arr, layout, _ in dparts:
        x_, sp = _row_spec(arr, layout, tile, d)
        ins.append(x_)
        specs.append(sp)
    ins += [xhat, rstd, g.reshape(1, d)]
    specs += [_nat(tile, d), _nat(tile, LANES), _whole((1, d))]
    gate_layout = None
    if has_gate:
        gate_layout = gate[0]
        for arr, ncb, cb in gate[1:]:
            x_, sp = _row_spec(arr, gate_layout, tile, d, ncb=ncb, cb=cb)
            ins.append(x_)
            specs.append(sp)
    seg_tiles = s // SUBLANES // tile
    out_shape = [jax.ShapeDtypeStruct((s, d), F32), jax.ShapeDtypeStruct((s, d), BF16)]
    out_specs = [_nat(tile, d), _nat(tile, d)]
    if has_gate:
        if gate_layout == 'nat':
            out_shape.append(jax.ShapeDtypeStruct((s, 2 * d), BF16))
            out_specs.append(_nat(tile, 2 * d))
        else:
            out_shape.append(jax.ShapeDtypeStruct((s // SUBLANES, SUBLANES * 2 * d), BF16))
            out_specs.append(_perm(tile, 2 * d, seg_tiles))
    out_shape += [jax.ShapeDtypeStruct((SUBLANES, d), F32)] * 2
    out_specs += [_whole((SUBLANES, d))] * 2
    res = list(pl.pallas_call(
        body, name=name, grid=(s // tile,), in_specs=specs, out_specs=out_specs, out_shape=out_shape,
        compiler_params=_params("arbitrary"),
    )(*ins))
    out = [res.pop(0), res.pop(0)]
    if has_gate:
        out.append(res.pop(0).reshape(s, 2 * d))
    out += [res[0][0], res[1][0]]
    return out


def _loss_partial(h, target, tile=256):
    s, d = h.shape

    def body(h_ref, t_ref, o_ref):
        @pl.when(pl.program_id(0) == 0)
        def _():
            o_ref[...] = jnp.zeros_like(o_ref)

        e = h_ref[...] - t_ref[...]
        sq = e * e
        part = sq[:, 0:LANES]
        for k in range(1, d // LANES):
            part = part + sq[:, k * LANES:(k + 1) * LANES]
        o_ref[0:1, :] += jnp.sum(part, axis=0, keepdims=True) * (0.5 / d)

    return pl.pallas_call(
        body, name="loss_partial", grid=(s // tile,), in_specs=[_nat(tile, d), _nat(tile, d)],
        out_specs=_whole((SUBLANES, LANES)), out_shape=jax.ShapeDtypeStruct((SUBLANES, LANES), F32),
        compiler_params=_params("arbitrary"),
    )(h, target)


def _swiglu_fwd(name, gu, tile=256):
    s, f2 = gu.shape
    f = f2 // 2
    cw = _pick(f, 1408)
    ncb = f // cw

    def body(g_ref, u_ref, o_ref):
        gg = g_ref[...]
        o_ref[...] = _bf(gg * _sigmoid(gg) * u_ref[...])

    return pl.pallas_call(
        body, name=name, grid=(s // tile, ncb),
        in_specs=[pl.BlockSpec((tile, cw), lambda i, j: (i, j)), pl.BlockSpec((tile, cw), lambda i, j: (i, j + ncb))],
        out_specs=pl.BlockSpec((tile, cw), lambda i, j: (i, j)),
        out_shape=jax.ShapeDtypeStruct((s, f), BF16), compiler_params=_params("parallel", "parallel"),
    )(gu, gu)


def _swiglu_bwd(name, gu, dact, tile=128):
    s, f2 = gu.shape
    f = f2 // 2

    def body(g_ref, u_ref, da_ref, o_ref):
        gg = g_ref[...]
        sg = _sigmoid(gg)
        da = da_ref[...].astype(F32)
        silu = gg * sg
        o_ref[:, :f] = _bf(da * u_ref[...] * (sg + silu * (1.0 - sg)))
        o_ref[:, f:] = _bf(da * silu)

    return pl.pallas_call(
        body, name=name, grid=(s // tile,),
        in_specs=[_nat(tile, f, 0), _nat(tile, f, 1), _nat(tile, f)], out_specs=_nat(tile, f2),
        out_shape=jax.ShapeDtypeStruct((s, f2), BF16), compiler_params=_params("parallel"),
    )(gu, gu, dact)


def _rms_fwd(proj, ql, kvl, gq, gkv, tile=256):
    s = proj.shape[0]
    assert ql == kvl

    def body(q_ref, kv_ref, gq_ref, gkv_ref, o_ref):
        def nrm(x, gg):
            return x * lax.rsqrt(jnp.mean(x * x, axis=-1, keepdims=True) + RMS_EPS) * gg

        o_ref[...] = jnp.concatenate([_bf(nrm(q_ref[...], gq_ref[...])), _bf(nrm(kv_ref[...], gkv_ref[...]))], axis=1)

    return pl.pallas_call(
        body, name="mla_rms_fwd", grid=(s // tile,),
        in_specs=[_nat(tile, ql, 0), _nat(tile, kvl, 1), _whole((1, ql)), _whole((1, kvl))],
        out_specs=_nat(tile, ql + kvl), out_shape=jax.ShapeDtypeStruct((s, ql + kvl), BF16),
        compiler_params=_params("parallel"),
    )(proj, proj, gq.reshape(1, ql), gkv.reshape(1, kvl))


def _rope_coeffs(pos, invf):
    ang = pos * invf
    cs, sn = jnp.cos(ang), jnp.sin(ang)
    lane = lax.broadcasted_iota(jnp.int32, ang.shape, 1)
    half = ROPE // 2
    c = jnp.where(lane < ROPE, cs, 0.0)
    sa = jnp.where(lane < half, -sn, 0.0)
    sb = jnp.where((lane >= half) & (lane < ROPE), sn, 0.0)
    return c, sa, sb


def _rope_prep(q_raw, kv, proj, kpe_cb, pos, invf, heads, tile=256):
    s = q_raw.shape[0]
    half = ROPE // 2

    def body(q_ref, kv_ref, kpe_ref, pos_ref, invf_ref, qf_ref, kf_ref, v_ref):
        c, sa, sb = _rope_coeffs(pos_ref[...], invf_ref[...])

        def rope(t):
            return t * c + pltpu.roll(t, LANES - half, 1) * sa + pltpu.roll(t, half, 1) * sb

        kr = _bf(rope(kpe_ref[...]))
        for hh in range(heads):
            o = hh * QK_PAD
            qf_ref[:, o:o + NOPE] = _bf(q_ref[:, o:o + NOPE])
            qf_ref[:, o + NOPE:o + QK_PAD] = _bf(rope(q_ref[:, o + NOPE:o + QK_PAD]))
            kf_ref[:, o:o + NOPE] = _bf(kv_ref[:, o:o + NOPE])
            kf_ref[:, o + NOPE:o + QK_PAD] = kr
            v_ref[:, hh * VDIM:(hh + 1) * VDIM] = _bf(kv_ref[:, o + NOPE:o + QK_PAD])

    w = heads * QK_PAD
    return pl.pallas_call(
        body, name="mla_rope_prep", grid=(s // tile,),
        in_specs=[_nat(tile, w), _nat(tile, w), _nat(tile, LANES, kpe_cb), _nat(tile, 1), _whole((1, LANES))],
        out_specs=[_nat(tile, w), _nat(tile, w), _nat(tile, heads * VDIM)],
        out_shape=[jax.ShapeDtypeStruct((s, w), BF16), jax.ShapeDtypeStruct((s, w), BF16),
                   jax.ShapeDtypeStruct((s, heads * VDIM), BF16)],
        compiler_params=_params("parallel"),
    )(q_raw, kv, proj, pos, invf)


def _rope_unprep(dqf, dkf, dv, pos, invf, heads, tile=256):
    s = dqf.shape[0]
    half = ROPE // 2

    def body(dq_ref, dk_ref, dv_ref, pos_ref, invf_ref, dqr_ref, dkv_ref, dkpe_ref):
        c, sa, sb = _rope_coeffs(pos_ref[...], invf_ref[...])

        def unrope(gt):
            return gt * c + pltpu.roll(gt * sa, half, 1) + pltpu.roll(gt * sb, LANES - half, 1)

        dkpe = jnp.zeros((tile, LANES), F32)
        for hh in range(heads):
            o = hh * QK_PAD
            dqr_ref[:, o:o + NOPE] = _bf(dq_ref[:, o:o + NOPE])
            dqr_ref[:, o + NOPE:o + QK_PAD] = _bf(unrope(dq_ref[:, o + NOPE:o + QK_PAD]))
            dkv_ref[:, o:o + NOPE] = _bf(dk_ref[:, o:o + NOPE])
            dkv_ref[:, o + NOPE:o + QK_PAD] = _bf(dv_ref[:, hh * VDIM:(hh + 1) * VDIM])
            dkpe = dkpe + dk_ref[:, o + NOPE:o + QK_PAD]
        dkpe_ref[...] = unrope(dkpe)

    w = heads * QK_PAD
    return pl.pallas_call(
        body, name="mla_rope_unprep", grid=(s // tile,),
        in_specs=[_nat(tile, w), _nat(tile, w), _nat(tile, heads * VDIM), _nat(tile, 1), _whole((1, LANES))],
        out_specs=[_nat(tile, w), _nat(tile, w), _nat(tile, LANES)],
        out_shape=[jax.ShapeDtypeStruct((s, w), BF16), jax.ShapeDtypeStruct((s, w), BF16),
                   jax.ShapeDtypeStruct((s, LANES), F32)],
        compiler_params=_params("parallel"),
    )(dqf, dkf, dv, pos, invf)


LOG2E = 1.4426950408889634
MLA_SCALE = (NOPE + ROPE) ** -0.5


def _mla_scores_t(k, q, t, masked):
    sc = _dot(k, q, NT) * (MLA_SCALE * LOG2E)
    if masked:
        row = lax.broadcasted_iota(jnp.int32, (t, t), 0)
        col = lax.broadcasted_iota(jnp.int32, (t, t), 1)
        sc = jnp.where(row <= col, sc, NEG)
    return sc


def _mla_fwd(qf, kf, vt, heads, t=512):
    s = qf.shape[0]
    t = min(t, s)
    nq = s // t

    def body(q_ref, k_ref, vt_ref, o_ref, lse_ref, m_ref, l_ref, acc_ref):
        i = pl.program_id(1)
        m_ref[...] = jnp.full_like(m_ref, NEG)
        l_ref[...] = jnp.zeros_like(l_ref)
        acc_ref[...] = jnp.zeros_like(acc_ref)
        q = q_ref[...]

        def block(j, masked):
            r0 = pl.multiple_of(j * t, t)
            sc = _mla_scores_t(k_ref[pl.ds(r0, t), :], q, t, masked)
            m_prev = m_ref[0:1, :]
            m_new = jnp.maximum(m_prev, jnp.max(sc, axis=0, keepdims=True))
            corr = jnp.exp2(m_prev - m_new)
            p = jnp.exp2(sc - m_new)
            l_new = corr * l_ref[0:1, :] + jnp.sum(p, axis=0, keepdims=True)
            acc_ref[...] = corr * acc_ref[...] + _dot(vt_ref[:, pl.ds(r0, t)], _bf(p), NN)
            m_ref[...] = jnp.broadcast_to(m_new, (SUBLANES, t))
            l_ref[...] = jnp.broadcast_to(l_new, (SUBLANES, t))

        def unmasked(j, carry):
            block(j, False)
            return carry

        lax.fori_loop(0, i, unmasked, 0)
        block(i, True)
        o_ref[...] = (acc_ref[...] / l_ref[0:1, :]).T
        lse_ref[...] = m_ref[...] + jnp.log(l_ref[...]) * LOG2E

    return pl.pallas_call(
        body, name="mla_flash_fwd", grid=(heads, nq),
        in_specs=[pl.BlockSpec((t, QK_PAD), lambda h, i: (i, h)), pl.BlockSpec((s, QK_PAD), lambda h, i: (0, h)),
                  pl.BlockSpec((VDIM, s), lambda h, i: (h, 0))],
        out_specs=[pl.BlockSpec((t, VDIM), lambda h, i: (i, h)), pl.BlockSpec((SUBLANES, t), lambda h, i: (h, i))],
        out_shape=[jax.ShapeDtypeStruct((s, heads * VDIM), F32), jax.ShapeDtypeStruct((heads * SUBLANES, s), F32)],
        scratch_shapes=[pltpu.VMEM((SUBLANES, t), F32), pltpu.VMEM((SUBLANES, t), F32), pltpu.VMEM((VDIM, t), F32)],
        compiler_params=_params("parallel", "arbitrary"),
    )(qf, kf, vt)


def _mla_bwd(qf, kf, v, do, lse_t, delta_t, heads, do_cb0, t=512):
    s = qf.shape[0]
    t = min(t, s)
    nq = s // t

    def body(q_ref, k_ref, v_ref, do_ref, lse_ref, dl_ref, dq_ref, dk_ref, dv_ref, acc_ref):
        i = pl.program_id(1)

        @pl.when(i == 0)
        def _():
            dk_ref[...] = jnp.zeros_like(dk_ref)
            dv_ref[...] = jnp.zeros_like(dv_ref)

        acc_ref[...] = jnp.zeros_like(acc_ref)
        q, dob = q_ref[...], do_ref[...]
        lse, dl = lse_ref[0:1, :], dl_ref[0:1, :]

        def block(j, masked):
            r0 = pl.multiple_of(j * t, t)
            k = k_ref[pl.ds(r0, t), :]
            p = jnp.exp2(_mla_scores_t(k, q, t, masked) - lse)
            dp = _dot(v_ref[pl.ds(r0, t), :], dob, NT)
            ds = _bf(p * (dp - dl) * MLA_SCALE)
            acc_ref[...] += _dot(ds, k, TN)
            dk_ref[pl.ds(r0, t), :] += _dot(ds, q, NN)
            dv_ref[pl.ds(r0, t), :] += _dot(_bf(p), dob, NN)

        def unmasked(j, carry):
            block(j, False)
            return carry

        lax.fori_loop(0, i, unmasked, 0)
        block(i, True)
        dq_ref[...] = acc_ref[...]

    qs = lambda w, off=0: pl.BlockSpec((t, w), lambda h, i: (i, h + off))
    ks = lambda w: pl.BlockSpec((s, w), lambda h, i: (0, h))
    st = pl.BlockSpec((SUBLANES, t), lambda h, i: (h, i))
    return pl.pallas_call(
        body, name="mla_flash_bwd", grid=(heads, nq),
        in_specs=[qs(QK_PAD), ks(QK_PAD), ks(VDIM), qs(VDIM, do_cb0), st, st],
        out_specs=[qs(QK_PAD), ks(QK_PAD), ks(VDIM)],
        out_shape=[jax.ShapeDtypeStruct((s, heads * QK_PAD), F32), jax.ShapeDtypeStruct((s, heads * QK_PAD), F32),
                   jax.ShapeDtypeStruct((s, heads * VDIM), F32)],
        scratch_shapes=[pltpu.VMEM((t, QK_PAD), F32)],
        compiler_params=_params("parallel", "arbitrary"),
    )(qf, kf, v, do, lse_t, delta_t)


def _band_mask(tq, first_block):
    row = lax.broadcasted_iota(jnp.int32, (tq, DIL_STEPS + tq), 0)
    col = lax.broadcasted_iota(jnp.int32, (tq, DIL_STEPS + tq), 1)
    dist = row + DIL_STEPS - col
    valid = (dist >= 0) & (dist <= DIL_STEPS) & (jnp.logical_not(first_block) | (col >= DIL_STEPS))
    return dist, valid


def _dil_scores(q, kp, kc, slope, dil, tq, first_block):
    sc = jnp.concatenate([_dot(q, kp, NT), _dot(q, kc, NT)], axis=1) * (DHD ** -0.5)
    dist, valid = _band_mask(tq, first_block)
    return jnp.where(valid, sc - slope * (dil * dist).astype(F32), NEG)


def _dil_specs(proj_w, dh, tq):
    pwb = proj_w // LANES
    r_of = lambda cb: cb // dh
    h_of = lambda cb: cb % dh
    cur = lambda off: pl.BlockSpec((tq, DHD), lambda cb, i: (i, r_of(cb) * pwb + off + h_of(cb)))
    prev = lambda off: pl.BlockSpec(
        (DIL_STEPS, DHD), lambda cb, i: (jnp.maximum(i * (tq // DIL_STEPS) - 1, 0), r_of(cb) * pwb + off + h_of(cb)))
    return cur, prev


def _dil_fwd(name, proj, slopes, dil, dh, q_cb, tq=512):
    s, pw = proj.shape
    l = s // dil
    tq = min(tq, l)
    nb = l // tq
    k_cb, v_cb = q_cb + dh, q_cb + 2 * dh
    cur, prev = _dil_specs(pw, dh, tq)
    pv = proj.reshape(l, dil * pw)

    def body(q_ref, kc_ref, kp_ref, vc_ref, vp_ref, sl_ref, o_ref, lse_ref):
        i = pl.program_id(1)
        sc = _dil_scores(_bf(q_ref[...]), _bf(kp_ref[...]), _bf(kc_ref[...]), sl_ref[0:1, 0:1], dil, tq, i == 0)
        m = jnp.max(sc, axis=-1, keepdims=True)
        e = jnp.exp(sc - m)
        lsum = jnp.sum(e, axis=-1, keepdims=True)
        pn = e / lsum
        o_ref[...] = (_dot(_bf(pn[:, :DIL_STEPS]), _bf(vp_ref[...]), NN)
                      + _dot(_bf(pn[:, DIL_STEPS:]), _bf(vc_ref[...]), NN))
        lse_ref[...] = jnp.broadcast_to(m + jnp.log(lsum), (tq, LANES))

    ospec = pl.BlockSpec((tq, DHD), lambda cb, i: (i, cb))
    o, lse = pl.pallas_call(
        body, name=name, grid=(dil * dh, nb),
        in_specs=[cur(q_cb), cur(k_cb), prev(k_cb), cur(v_cb), prev(v_cb),
                  pl.BlockSpec((SUBLANES, LANES), lambda cb, i: (cb % dh, 0))],
        out_specs=[ospec, ospec],
        out_shape=[jax.ShapeDtypeStruct((l, dil * dh * DHD), F32)] * 2,
        compiler_params=_params("parallel", "parallel"),
    )(pv, pv, pv, pv, pv, slopes)
    return o.reshape(s, dh * DHD), lse.reshape(s, dh * DHD)


def _dil_bwd_dq(name, proj, slopes, do, lse, delta, dil, dh, q_cb, b_cb0, tq=512):
    s, pw = proj.shape
    mixw = do.shape[1]
    l = s // dil
    tq = min(tq, l)
    nb = l // tq
    k_cb, v_cb = q_cb + dh, q_cb + 2 * dh
    cur, prev = _dil_specs(pw, dh, tq)
    pv = proj.reshape(l, dil * pw)
    mb = mixw // LANES
    mspec = pl.BlockSpec((tq, DHD), lambda cb, i: (i, (cb // dh) * mb + b_cb0 + cb % dh))
    ospec = pl.BlockSpec((tq, DHD), lambda cb, i: (i, cb))

    def body(q_ref, kc_ref, kp_ref, vc_ref, vp_ref, sl_ref, do_ref, lse_ref, dl_ref, dq_ref):
        i = pl.program_id(1)
        kp, kc = _bf(kp_ref[...]), _bf(kc_ref[...])
        sc = _dil_scores(_bf(q_ref[...]), kp, kc, sl_ref[0:1, 0:1], dil, tq, i == 0)
        p = jnp.exp(sc - lse_ref[:, 0:1])
        dob = do_ref[...]
        dp = jnp.concatenate([_dot(dob, _bf(vp_ref[...]), NT), _dot(dob, _bf(vc_ref[...]), NT)], axis=1)
        ds = _bf(p * (dp - dl_ref[:, 0:1]) * (DHD ** -0.5))
        dq_ref[...] = _dot(ds[:, :DIL_STEPS], kp, NN) + _dot(ds[:, DIL_STEPS:], kc, NN)

    dq = pl.pallas_call(
        body, name=name, grid=(dil * dh, nb),
        in_specs=[cur(q_cb), cur(k_cb), prev(k_cb), cur(v_cb), prev(v_cb),
                  pl.BlockSpec((SUBLANES, LANES), lambda cb, i: (cb % dh, 0)), mspec, ospec, mspec],
        out_specs=ospec, out_shape=jax.ShapeDtypeStruct((l, dil * dh * DHD), F32),
        compiler_params=_params("parallel", "parallel"),
    )(pv, pv, pv, pv, pv, slopes, do.reshape(l, dil * mixw), lse.reshape(l, dil * dh * DHD), delta.reshape(l, dil * mixw))
    return dq.reshape(s, dh * DHD)


def _dil_bwd_dkv(name, proj, slopes, do, lse, delta, dil, dh, q_cb, b_cb0, tk=512):
    s, pw = proj.shape
    mixw = do.shape[1]
    l = s // dil
    tk = min(tk, l)
    nb = l // tk
    k_cb, v_cb = q_cb + dh, q_cb + 2 * dh
    pwb, mb = pw // LANES, mixw // LANES
    sub = tk // DIL_STEPS
    last128 = l // DIL_STEPS - 1
    pv = proj.reshape(l, dil * pw)

    def cur(width_blocks, off):
        return pl.BlockSpec((tk, DHD), lambda cb, j: (j, (cb // dh) * width_blocks + off + cb % dh))

    def nxt(width_blocks, off):
        return pl.BlockSpec((DIL_STEPS, DHD), lambda cb, j: (jnp.minimum((j + 1) * sub, last128),
                                                               (cb // dh) * width_blocks + off + cb % dh))

    ocur = pl.BlockSpec((tk, DHD), lambda cb, j: (j, cb))
    onxt = pl.BlockSpec((DIL_STEPS, DHD), lambda cb, j: (jnp.minimum((j + 1) * sub, last128), cb))

    def body(k_ref, v_ref, qc_ref, qn_ref, sl_ref, doc_ref, don_ref, lsec_ref, lsen_ref, dlc_ref, dln_ref,
             dk_ref, dv_ref):
        j = pl.program_id(1)
        slope = sl_ref[0:1, 0:1]
        scale = DHD ** -0.5
        k, v = _bf(k_ref[...]), _bf(v_ref[...])
        qc = _bf(qc_ref[...])
        row = lax.broadcasted_iota(jnp.int32, (tk, tk), 0)
        col = lax.broadcasted_iota(jnp.int32, (tk, tk), 1)
        dist = row - col
        valid = (dist >= 0) & (dist <= DIL_STEPS)
        sc = jnp.where(valid, _dot(qc, k, NT) * scale - slope * (dil * dist).astype(F32), NEG)
        p = jnp.exp(sc - lsec_ref[:, 0:1])
        doc = doc_ref[...]
        ds = _bf(p * (_dot(doc, v, NT) - dlc_ref[:, 0:1]) * scale)
        dv_ref[...] = _dot(_bf(p), doc, TN)
        dk_ref[...] = _dot(ds, qc, TN)
        kl, vl = k[tk - DIL_STEPS:, :], v[tk - DIL_STEPS:, :]
        qn = _bf(qn_ref[...])
        row = lax.broadcasted_iota(jnp.int32, (DIL_STEPS, DIL_STEPS), 0)
        col = lax.broadcasted_iota(jnp.int32, (DIL_STEPS, DIL_STEPS), 1)
        dist = DIL_STEPS + row - col
        valid = (dist <= DIL_STEPS) & (j < nb - 1)
        sc = jnp.where(valid, _dot(qn, kl, NT) * scale - slope * (dil * dist).astype(F32), NEG)
        p = jnp.exp(sc - lsen_ref[:, 0:1])
        don = don_ref[...]
        ds = _bf(p * (_dot(don, vl, NT) - dln_ref[:, 0:1]) * scale)
        dv_ref[tk - DIL_STEPS:, :] += _dot(_bf(p), don, TN)
        dk_ref[tk - DIL_STEPS:, :] += _dot(ds, qn, TN)

    dov = do.reshape(l, dil * mixw)
    dlv = delta.reshape(l, dil * mixw)
    lsv = lse.reshape(l, dil * dh * DHD)
    dk, dv = pl.pallas_call(
        body, name=name, grid=(dil * dh, nb),
        in_specs=[cur(pwb, k_cb), cur(pwb, v_cb), cur(pwb, q_cb), nxt(pwb, q_cb),
                  pl.BlockSpec((SUBLANES, LANES), lambda cb, j: (cb % dh, 0)),
                  cur(mb, b_cb0), nxt(mb, b_cb0), ocur, onxt, cur(mb, b_cb0), nxt(mb, b_cb0)],
        out_specs=[ocur, ocur], out_shape=[jax.ShapeDtypeStruct((l, dil * dh * DHD), F32)] * 2,
        compiler_params=_params("parallel", "parallel"),
    )(pv, pv, pv, pv, slopes, dov, dov, lsv, lsv, dlv, dlv)
    return dk.reshape(s, dh * DHD), dv.reshape(s, dh * DHD)


def _dil_merge(out_a, outs, lses, tile=256):
    s, wa = out_a.shape
    wb = outs[0].shape[1]
    nbr = len(outs)

    def body(*refs):
        a_ref = refs[0]
        o_refs, l_refs = refs[1:1 + nbr], refs[1 + nbr:1 + 2 * nbr]
        att_ref, ob_ref, lse_ref = refs[1 + 2 * nbr:]
        ls = [r[...] for r in l_refs]
        m = ls[0]
        for x_ in ls[1:]:
            m = jnp.maximum(m, x_)
        es = [jnp.exp(x_ - m) for x_ in ls]
        tot = es[0]
        for e in es[1:]:
            tot = tot + e
        ob = (es[0] / tot) * o_refs[0][...]
        for e, r in zip(es[1:], o_refs[1:]):
            ob = ob + (e / tot) * r[...]
        ob_ref[...] = ob
        lse_ref[...] = m + jnp.log(tot)
        att_ref[...] = jnp.concatenate([_bf(a_ref[...]), _bf(ob)], axis=1)

    return pl.pallas_call(
        body, name="dil_merge", grid=(s // tile,),
        in_specs=[_nat(tile, wa)] + [_nat(tile, wb)] * (2 * nbr),
        out_specs=[_nat(tile, wa + wb), _nat(tile, wb), _nat(tile, wb)],
        out_shape=[jax.ShapeDtypeStruct((s, wa + wb), BF16), jax.ShapeDtypeStruct((s, wb), F32),
                   jax.ShapeDtypeStruct((s, wb), F32)],
        compiler_params=_params("parallel"),
    )(out_a, *outs, *lses)


DIL_BLOCK = 2048


def _dil_unit_rows(u, dil, block):
    sub = u // dil
    return u % dil + (dil * DIL_STEPS) * sub, sub == 0


def _dil_unit_scores(q, kp, kc, slope, dil, no_prev):
    sc = jnp.concatenate([_dot(q, kp, NT), _dot(q, kc, NT)], axis=1) * (DHD ** -0.5)
    row = lax.broadcasted_iota(jnp.int32, (DIL_STEPS, 2 * DIL_STEPS), 0)
    col = lax.broadcasted_iota(jnp.int32, (DIL_STEPS, 2 * DIL_STEPS), 1)
    dist = row + DIL_STEPS - col
    valid = (dist >= 0) & (dist <= DIL_STEPS) & (jnp.logical_not(no_prev) | (col >= DIL_STEPS))
    return jnp.where(valid, sc - slope * (dil * dist).astype(F32), NEG)


def _dil_in_specs(pw, dh, q_cb, block, rev_nb=None):
    blk = (lambda i: i) if rev_nb is None else (lambda i: rev_nb - 1 - i)
    own = lambda off: pl.BlockSpec((block, DHD), lambda h, i: (blk(i), off + h))
    prev = lambda off: pl.BlockSpec((block, DHD), lambda h, i: (jnp.maximum(blk(i) - 1, 0), off + h))
    return [own(q_cb), own(q_cb + dh), prev(q_cb + dh), own(q_cb + 2 * dh), prev(q_cb + 2 * dh)]


def _dil_fused_fwd(proj, slopes, dh, q_cb):
    s, pw = proj.shape
    block = min(DIL_BLOCK, s)
    nb = s // block
    n_units = block // DIL_STEPS
    nbr = len(DIL_BRANCHES)
    assert block >= DIL_STEPS * max(d for _, d in DIL_BRANCHES)

    def body(q_ref, kc_ref, kp_ref, vc_ref, vp_ref, sl_ref, o_ref, lse_ref, kk, vv, *per_branch):
        og, mg, lg = per_branch[:nbr], per_branch[nbr:2 * nbr], per_branch[2 * nbr:]
        i = pl.program_id(1)
        kk[0:block, :] = kp_ref[...]
        kk[block:, :] = kc_ref[...]
        vv[0:block, :] = vp_ref[...]
        vv[block:, :] = vc_ref[...]
        slope = sl_ref[0:1, 0:1]
        for g, (_, dil) in enumerate(DIL_BRANCHES):
            def unit(u, carry, g=g, dil=dil):
                q0, first = _dil_unit_rows(u, dil, block)
                rows = lambda base: pl.ds(base, DIL_STEPS, stride=dil) if dil > 1 else pl.ds(base, DIL_STEPS)
                q = _bf(q_ref[rows(q0), :])
                kc, kp = _bf(kk[rows(block + q0), :]), _bf(kk[rows(block + q0 - dil * DIL_STEPS), :])
                vc, vp = _bf(vv[rows(block + q0), :]), _bf(vv[rows(block + q0 - dil * DIL_STEPS), :])
                sc = _dil_unit_scores(q, kp, kc, slope, dil, first & (i == 0))
                m = jnp.max(sc, axis=-1, keepdims=True)
                e = jnp.exp(sc - m)
                og[g][rows(q0), :] = _dot(_bf(e[:, :DIL_STEPS]), vp, NN) + _dot(_bf(e[:, DIL_STEPS:]), vc, NN)
                mg[g][rows(q0), :] = jnp.broadcast_to(m, (DIL_STEPS, LANES))
                lg[g][rows(q0), :] = jnp.broadcast_to(jnp.sum(e, axis=-1, keepdims=True), (DIL_STEPS, LANES))
                return carry

            lax.fori_loop(0, n_units, unit, 0, unroll=4)
        m_all = mg[0][...]
        for g in range(1, nbr):
            m_all = jnp.maximum(m_all, mg[g][...])
        tot = jnp.zeros((block, LANES), F32)
        acc = jnp.zeros((block, DHD), F32)
        for g in range(nbr):
            w = jnp.exp(mg[g][...] - m_all)
            tot = tot + w * lg[g][...]
            acc = acc + w * og[g][...]
        o_ref[...] = acc / tot
        lse_ref[...] = m_all + jnp.log(tot)

    ospec = pl.BlockSpec((block, DHD), lambda h, i: (i, h))
    return pl.pallas_call(
        body, name="dil_fused_fwd", grid=(dh, nb),
        in_specs=_dil_in_specs(pw, dh, q_cb, block) + [pl.BlockSpec((SUBLANES, LANES), lambda h, i: (h, 0))],
        out_specs=[ospec, ospec], out_shape=[jax.ShapeDtypeStruct((s, dh * DHD), F32)] * 2,
        scratch_shapes=[pltpu.VMEM((2 * block, DHD), F32), pltpu.VMEM((2 * block, DHD), F32)]
        + [pltpu.VMEM((block, DHD), F32)] * (3 * nbr),
        compiler_params=_params("parallel", "arbitrary"),
    )(proj, proj, proj, proj, proj, slopes)


def _dil_fused_bwd(proj, slopes, datt, lse, delta, dh, q_cb, b_cb0):
    s, pw = proj.shape
    block = min(DIL_BLOCK, s)
    nb = s // block
    n_units = block // DIL_STEPS
    scale = DHD ** -0.5

    def body(q_ref, kc_ref, kp_ref, vc_ref, vp_ref, sl_ref, do_ref, lse_ref, dl_ref, dq_ref, dk_ref, dv_ref,
             kk, vv, dkk, dvv, carry_k, carry_v):
        ii = pl.program_id(1)
        i = nb - 1 - ii

        @pl.when(ii == 0)
        def _():
            carry_k[...] = jnp.zeros_like(carry_k)
            carry_v[...] = jnp.zeros_like(carry_v)

        kk[0:block, :] = kp_ref[...]
        kk[block:, :] = kc_ref[...]
        vv[0:block, :] = vp_ref[...]
        vv[block:, :] = vc_ref[...]
        dkk[...] = jnp.zeros_like(dkk)
        dvv[...] = jnp.zeros_like(dvv)
        dq_ref[...] = jnp.zeros_like(dq_ref)
        slope = sl_ref[0:1, 0:1]
        for _, dil in DIL_BRANCHES:
            def unit(u, carry, dil=dil):
                q0, first = _dil_unit_rows(u, dil, block)
                rows = lambda base: pl.ds(base, DIL_STEPS, stride=dil) if dil > 1 else pl.ds(base, DIL_STEPS)
                cur, prev = rows(block + q0), rows(block + q0 - dil * DIL_STEPS)
                q = _bf(q_ref[rows(q0), :])
                kc, kp, vc, vp = _bf(kk[cur, :]), _bf(kk[prev, :]), _bf(vv[cur, :]), _bf(vv[prev, :])
                dob = _bf(do_ref[rows(q0), :])
                sc = _dil_unit_scores(q, kp, kc, slope, dil, first & (i == 0))
                p = jnp.exp(sc - lse_ref[rows(q0), 0:1])
                dp = jnp.concatenate([_dot(dob, vp, NT), _dot(dob, vc, NT)], axis=1)
                ds = _bf(p * (dp - dl_ref[rows(q0), 0:1]) * scale)
                pb = _bf(p)
                dq_ref[rows(q0), :] += _dot(ds[:, :DIL_STEPS], kp, NN) + _dot(ds[:, DIL_STEPS:], kc, NN)
                dkk[prev, :] += _dot(ds[:, :DIL_STEPS], q, TN)
                dkk[cur, :] += _dot(ds[:, DIL_STEPS:], q, TN)
                dvv[prev, :] += _dot(pb[:, :DIL_STEPS], dob, TN)
                dvv[cur, :] += _dot(pb[:, DIL_STEPS:], dob, TN)
                return carry

            lax.fori_loop(0, n_units, unit, 0, unroll=4)
        dk_ref[...] = dkk[block:, :] + carry_k[...]
        dv_ref[...] = dvv[block:, :] + carry_v[...]
        carry_k[...] = dkk[0:block, :]
        carry_v[...] = dvv[0:block, :]

    rev = lambda i: nb - 1 - i
    mspec = pl.BlockSpec((block, DHD), lambda h, i: (rev(i), b_cb0 + h))
    ospec = pl.BlockSpec((block, DHD), lambda h, i: (rev(i), h))
    big = lambda: pltpu.VMEM((2 * block, DHD), F32)
    return pl.pallas_call(
        body, name="dil_fused_bwd", grid=(dh, nb),
        in_specs=_dil_in_specs(pw, dh, q_cb, block, rev_nb=nb)
        + [pl.BlockSpec((SUBLANES, LANES), lambda h, i: (h, 0)), mspec, ospec, mspec],
        out_specs=[ospec, ospec, ospec], out_shape=[jax.ShapeDtypeStruct((s, dh * DHD), F32)] * 3,
        scratch_shapes=[big(), big(), big(), big(), pltpu.VMEM((block, DHD), F32), pltpu.VMEM((block, DHD), F32)],
        compiler_params=_params("parallel", "arbitrary"),
    )(proj, proj, proj, proj, proj, slopes, datt, lse, delta)


def _concat_bf16(name, a, b, tile=256):
    s, wa = a.shape
    wb = b.shape[1]

    def body(a_ref, b_ref, o_ref):
        o_ref[...] = jnp.concatenate([_bf(a_ref[...]), _bf(b_ref[...])], axis=1)

    return pl.pallas_call(
        body, name=name, grid=(s // tile,), in_specs=[_nat(tile, wa), _nat(tile, wb)], out_specs=_nat(tile, wa + wb),
        out_shape=jax.ShapeDtypeStruct((s, wa + wb), BF16), compiler_params=_params("parallel"),
    )(a, b)


def _attn_bwd_prep(datt, out_a, out_b, tile=256):
    s, mixw = datt.shape
    wa = out_a.shape[1]
    heads_a = wa // LANES

    def body(d_ref, a_ref, b_ref, do_ref, dl_ref, dlt_ref):
        d = d_ref[...]
        do_ref[...] = _bf(d)
        prod = d * jnp.concatenate([a_ref[...], b_ref[...]], axis=1)
        for hh in range(mixw // LANES):
            sl = slice(hh * LANES, (hh + 1) * LANES)
            dl = jnp.broadcast_to(jnp.sum(prod[:, sl], axis=-1, keepdims=True), (tile, LANES))
            dl_ref[:, sl] = dl
            if hh < heads_a:
                dlt_ref[hh * SUBLANES:(hh + 1) * SUBLANES, :] = dl.T[0:SUBLANES, :]

    return pl.pallas_call(
        body, name="attn_bwd_prep", grid=(s // tile,),
        in_specs=[_nat(tile, mixw), _nat(tile, wa), _nat(tile, mixw - wa)],
        out_specs=[_nat(tile, mixw), _nat(tile, mixw), pl.BlockSpec((heads_a * SUBLANES, tile), lambda i: (0, i))],
        out_shape=[jax.ShapeDtypeStruct((s, mixw), BF16), jax.ShapeDtypeStruct((s, mixw), F32),
                   jax.ShapeDtypeStruct((heads_a * SUBLANES, s), F32)],
        compiler_params=_params("parallel"),
    )(datt, out_a, out_b)


def _dproj_assemble(proj, dnq, dnkv, dkpe, dqs, dks, dvs, gq, gkv, ql, tile=256):
    s, pw = proj.shape
    dw = dqs[0].shape[1]
    nbr = len(dqs)

    def body(*refs):
        ql_ref, kvl_ref, dnq_ref, dnkv_ref, dkpe_ref = refs[:5]
        br = refs[5:5 + 3 * nbr]
        gq_ref, gkv_ref = refs[5 + 3 * nbr:7 + 3 * nbr]
        dp_ref, dgq_ref, dgkv_ref = refs[7 + 3 * nbr:]

        @pl.when(pl.program_id(0) == 0)
        def _():
            dgq_ref[...] = jnp.zeros_like(dgq_ref)
            dgkv_ref[...] = jnp.zeros_like(dgkv_ref)

        def rms_bwd(x, dy, gg, dg_ref):
            r = lax.rsqrt(jnp.mean(x * x, axis=-1, keepdims=True) + RMS_EPS)
            xh = x * r
            dxh = dy * gg
            dg_ref[0:1, :] += jnp.sum(dy * xh, axis=0, keepdims=True)
            return r * (dxh - xh * jnp.mean(dxh * xh, axis=-1, keepdims=True))

        pieces = [_bf(rms_bwd(ql_ref[...], dnq_ref[...], gq_ref[...], dgq_ref)),
                  _bf(rms_bwd(kvl_ref[...], dnkv_ref[...], gkv_ref[...], dgkv_ref)),
                  _bf(dkpe_ref[...])]
        for k in range(3):
            acc = br[k * nbr][...]
            for r in br[k * nbr + 1:(k + 1) * nbr]:
                acc = acc + r[...]
            pieces.append(_bf(acc))
        dp_ref[...] = jnp.concatenate(pieces, axis=1)

    res = pl.pallas_call(
        body, name="dproj_assemble", grid=(s // tile,),
        in_specs=[_nat(tile, ql, 0), _nat(tile, ql, 1), _nat(tile, ql), _nat(tile, ql), _nat(tile, LANES)]
        + [_nat(tile, dw)] * (3 * nbr) + [_whole((1, ql)), _whole((1, ql))],
        out_specs=[_nat(tile, pw), _whole((SUBLANES, ql)), _whole((SUBLANES, ql))],
        out_shape=[jax.ShapeDtypeStruct((s, pw), BF16), jax.ShapeDtypeStruct((SUBLANES, ql), F32),
                   jax.ShapeDtypeStruct((SUBLANES, ql), F32)],
        compiler_params=_params("arbitrary"),
    )(proj, proj, dnq, dnkv, dkpe, *dqs, *dks, *dvs, gq.reshape(1, ql), gkv.reshape(1, ql))
    return res[0], res[1][0], res[2][0]


def _axpy(name, alpha, a, b, tile=256):
    s, d = a.shape

    def body(a_ref, b_ref, o_ref):
        o_ref[...] = alpha * a_ref[...] + b_ref[...]

    return pl.pallas_call(
        body, name=name, grid=(s // tile,), in_specs=[_nat(tile, d), _nat(tile, d)], out_specs=_nat(tile, d),
        out_shape=jax.ShapeDtypeStruct((s, d), F32), compiler_params=_params("parallel"),
    )(a, b)


def _cmul(ar, ai, br, bi):
    return ar * br - ai * bi, ar * bi + ai * br


def _s5_discretise(a_re, a_im, log_dt, b_re, b_im, n_sq):
    shape = a_re.shape

    def body(ar_ref, ai_ref, ldt_ref, br_ref, bi_ref, abr_ref, abi_ref, apr_ref, api_ref, bbr_ref, bbi_ref):
        ar, ai = ar_ref[...], ai_ref[...]
        dt = jnp.exp(ldt_ref[...])
        e = jnp.exp(ar * dt)
        abr, abi = e * jnp.cos(ai * dt), e * jnp.sin(ai * dt)
        den = ar * ar + ai * ai
        qr = ((abr - 1.0) * ar + abi * ai) / den
        qi = (abi * ar - (abr - 1.0) * ai) / den
        bbr, bbi = _cmul(qr, qi, br_ref[...], bi_ref[...])
        abr_ref[...], abi_ref[...] = abr, abi
        bbr_ref[...], bbi_ref[...] = bbr, bbi
        pr, pi = abr, abi
        for _ in range(n_sq):
            pr, pi = _cmul(pr, pi, pr, pi)
        apr_ref[...], api_ref[...] = pr, pi

    return pl.pallas_call(
        body, name="s5_discretise", out_shape=[jax.ShapeDtypeStruct(shape, F32)] * 6,
        compiler_params=pltpu.CompilerParams(vmem_limit_bytes=VMEM_LIMIT),
    )(a_re, a_im, log_dt, b_re, b_im)


def _s5_discretise_bwd(a16, b16, ag, gab, gbb):
    rows, p = a16[0].shape
    g = rows // S5_GROUP

    def disc(ar, ai, ldt):
        dt = jnp.exp(ldt)
        e = jnp.exp(ar * dt)
        abr, abi = e * jnp.cos(ai * dt), e * jnp.sin(ai * dt)
        den = ar * ar + ai * ai
        inv_r, inv_i = ar / den, -ai / den
        qr, qi = _cmul(abr - 1.0, abi, inv_r, inv_i)
        return dt, abr, abi, inv_r, inv_i, qr, qi

    def body(ar16_ref, ai16_ref, ldt16_ref, br_ref, bi_ref, ar_ref, ai_ref, ldt_ref, gar_ref, gai_ref, gbr_ref, gbi_ref,
             dar_ref, dai_ref, dldt_ref, dbr_ref, dbi_ref):
        _, _, _, _, _, qr16, qi16 = disc(ar16_ref[...], ai16_ref[...], ldt16_ref[...])
        gbr, gbi = gbr_ref[...], gbi_ref[...]
        dbr_ref[...], dbi_ref[...] = _cmul(qr16, -qi16, gbr, gbi)
        cr, ci = _cmul(br_ref[...], -bi_ref[...], gbr, gbi)
        gqr = jnp.sum(cr.reshape(g, S5_GROUP, p), axis=1)
        gqi = jnp.sum(ci.reshape(g, S5_GROUP, p), axis=1)
        ar, ai = ar_ref[...], ai_ref[...]
        dt, abr, abi, inv_r, inv_i, qr, qi = disc(ar, ai, ldt_ref[...])
        t_r, t_i = _cmul(inv_r, -inv_i, gqr, gqi)
        gab_r = gar_ref[...] + t_r
        gab_i = gai_ref[...] + t_i
        qa_r, qa_i = _cmul(qr, qi, inv_r, inv_i)
        a1_r, a1_i = _cmul(qa_r, -qa_i, gqr, gqi)
        gl_r, gl_i = _cmul(abr, -abi, gab_r, gab_i)
        dar_ref[...] = dt * gl_r - a1_r
        dai_ref[...] = dt * gl_i - a1_i
        gdt = jnp.sum(ar * gl_r + ai * gl_i, axis=-1, keepdims=True)
        dldt_ref[...] = gdt * dt[:, 0:1]

    return pl.pallas_call(
        body, name="s5_discretise_bwd",
        out_shape=[jax.ShapeDtypeStruct((g, p), F32), jax.ShapeDtypeStruct((g, p), F32),
                   jax.ShapeDtypeStruct((g, 1), F32), jax.ShapeDtypeStruct((rows, p), F32),
                   jax.ShapeDtypeStruct((rows, p), F32)],
        compiler_params=pltpu.CompilerParams(vmem_limit_bytes=VMEM_LIMIT),
    )(*a16, *b16, *ag, *gab, *gbb)


def _slab_tile(re, im, nsl):
    row = jnp.concatenate([re.reshape(nsl, SLAB_COLS), im.reshape(nsl, SLAB_COLS)], axis=-1)
    return jnp.repeat(row, SUBLANES, axis=0)


def _slab_in_matrix(b_re, b_im, nsl):
    eye = jnp.eye(SLAB_GROUPS, dtype=F32)

    def blk(b):
        b = b.reshape(nsl, SLAB_GROUPS, S5_GROUP, S5_STATE)
        return jnp.einsum('sgcp,gh->sgchp', b, eye).reshape(nsl, LANES, SLAB_COLS)

    return jnp.concatenate([blk(b_re), blk(b_im)], axis=-1)


def _slab_in_extract(m, nsl):
    eye = jnp.eye(SLAB_GROUPS, dtype=F32)

    def ext(x_):
        x_ = x_.reshape(nsl, SLAB_GROUPS, S5_GROUP, SLAB_GROUPS, S5_STATE)
        return jnp.einsum('sgchp,gh->sgcp', x_, eye).reshape(nsl * LANES, S5_STATE)

    return ext(m[..., :SLAB_COLS]), ext(m[..., SLAB_COLS:])


def _slab_out_matrix(c_re, c_im, nsl):
    eye = jnp.eye(SLAB_GROUPS, dtype=F32)

    def blk(c):
        c = c.reshape(nsl, SLAB_GROUPS, S5_GROUP, S5_STATE)
        return jnp.einsum('sgcp,gh->sgphc', c, eye).reshape(nsl, SLAB_COLS, LANES)

    return jnp.concatenate([blk(c_re), -blk(c_im)], axis=1)


def _slab_out_extract(m, nsl):
    eye = jnp.eye(SLAB_GROUPS, dtype=F32)

    def ext(x_):
        x_ = x_.reshape(nsl, SLAB_GROUPS, S5_STATE, SLAB_GROUPS, S5_GROUP)
        return jnp.einsum('sgphc,gh->sgcp', x_, eye).reshape(nsl * SLAB_GROUPS, S5_GROUP, S5_STATE)

    return ext(m[:, :SLAB_COLS]), -ext(m[:, SLAB_COLS:])


def _gelu(y):
    t = jnp.tanh(0.7978845608028654 * (y + 0.044715 * y * y * y))
    return 0.5 * y * (1.0 + t)


def _gelu_grad(y):
    t = jnp.tanh(0.7978845608028654 * (y + 0.044715 * y * y * y))
    return 0.5 * (1.0 + t) + 0.5 * y * (1.0 - t * t) * 0.7978845608028654 * (1.0 + 3.0 * 0.044715 * y * y)


def _scan_rows(ref, n_steps, ar, ai, state, reverse, conj):
    sgn = -1.0 if conj else 1.0

    def step(k, carry):
        xr, xi = carry
        t = (n_steps - 1 - k) if reverse else k
        r0 = pl.multiple_of(t * SUBLANES, SUBLANES)
        nr = ar * xr - sgn * ai * xi + ref[pl.ds(r0, SUBLANES), :SLAB_COLS]
        ni = ar * xi + sgn * ai * xr + ref[pl.ds(r0, SUBLANES), SLAB_COLS:]
        ref[pl.ds(r0, SUBLANES), :SLAB_COLS] = nr
        ref[pl.ds(r0, SUBLANES), SLAB_COLS:] = ni
        return nr, ni

    return lax.fori_loop(0, n_steps, step, state, unroll=4)


def _s5_pass1(hp, bblk, ab_tile, rc=1024):
    s, d = hp.shape
    nsl = d // LANES
    rc = min(rc, s)
    nch = s // rc
    w = 2 * SLAB_COLS

    def body(u_ref, b_ref, ab_ref, x_ref, end_ref, st_ref):
        j = pl.program_id(1)

        @pl.when(j == 0)
        def _():
            st_ref[...] = jnp.zeros_like(st_ref)

        x_ref[...] = _dot(_bf(u_ref[...]), b_ref[0], NN)
        xr, xi = _scan_rows(x_ref, rc // SUBLANES, ab_ref[:, :SLAB_COLS], ab_ref[:, SLAB_COLS:],
                            (st_ref[:, :SLAB_COLS], st_ref[:, SLAB_COLS:]), False, False)
        st_ref[:, :SLAB_COLS] = xr
        st_ref[:, SLAB_COLS:] = xi

        @pl.when(j == nch - 1)
        def _():
            end_ref[...] = st_ref[...]

    return pl.pallas_call(
        body, name="s5_scan_local", grid=(nsl, nch),
        in_specs=[pl.BlockSpec((rc, LANES), lambda sl, j: (j, sl)), pl.BlockSpec((1, LANES, w), lambda sl, j: (sl, 0, 0)),
                  pl.BlockSpec((SUBLANES, w), lambda sl, j: (sl, 0))],
        out_specs=[pl.BlockSpec((rc, w), lambda sl, j: (j, sl)), pl.BlockSpec((SUBLANES, w), lambda sl, j: (sl, 0))],
        out_shape=[jax.ShapeDtypeStruct((s, nsl * w), F32), jax.ShapeDtypeStruct((nsl * SUBLANES, w), F32)],
        scratch_shapes=[pltpu.VMEM((SUBLANES, w), F32)],
        compiler_params=_params("parallel", "arbitrary"),
    )(hp, bblk, ab_tile)


def _s5_carry(name, ends, ap_tile, reverse):
    rows, w = ends.shape
    nsl = rows // SUBLANES
    sgn = -1.0 if reverse else 1.0

    def body(e_ref, ap_ref, c_ref):
        pr, pi = ap_ref[0:1, :SLAB_COLS], sgn * ap_ref[0:1, SLAB_COLS:]
        tr = jnp.zeros((1, SLAB_COLS), F32)
        ti = jnp.zeros((1, SLAB_COLS), F32)
        order = range(SUBLANES - 1, -1, -1) if reverse else range(SUBLANES)
        for seg in order:
            c_ref[seg:seg + 1, :SLAB_COLS] = tr
            c_ref[seg:seg + 1, SLAB_COLS:] = ti
            mr, mi = _cmul(pr, pi, tr, ti)
            tr = e_ref[seg:seg + 1, :SLAB_COLS] + mr
            ti = e_ref[seg:seg + 1, SLAB_COLS:] + mi

    spec = pl.BlockSpec((SUBLANES, w), lambda sl: (sl, 0))
    return pl.pallas_call(
        body, name=name, grid=(nsl,), in_specs=[spec, spec], out_specs=spec,
        out_shape=jax.ShapeDtypeStruct((rows, w), F32), compiler_params=_params("parallel"),
    )(ends, ap_tile)


def _s5_pass2(xloc, cin, ab_tile, cblk, hp, dvec, rc=1024):
    s, d = hp.shape
    nsl = d // LANES
    rc = min(rc, s)
    nch = s // rc
    w = 2 * SLAB_COLS

    def body(xl_ref, cin_ref, ab_ref, c_ref, h_ref, d_ref, x_ref, y_ref, z_ref, st_ref):
        j = pl.program_id(1)

        @pl.when(j == 0)
        def _():
            st_ref[...] = cin_ref[...]

        x_ref[...] = jnp.zeros_like(x_ref)
        zr, zi = _scan_rows(x_ref, rc // SUBLANES, ab_ref[:, :SLAB_COLS], ab_ref[:, SLAB_COLS:],
                            (st_ref[:, :SLAB_COLS], st_ref[:, SLAB_COLS:]), False, False)
        st_ref[:, :SLAB_COLS] = zr
        st_ref[:, SLAB_COLS:] = zi
        x = x_ref[...] + xl_ref[...]
        x_ref[...] = x
        y = _dot(_bf(x), c_ref[0], NN) + d_ref[...] * h_ref[...]
        y_ref[...] = y
        z_ref[...] = _bf(_gelu(y))

    tile = lambda wd: pl.BlockSpec((rc, wd), lambda sl, j: (j, sl))
    small = pl.BlockSpec((SUBLANES, w), lambda sl, j: (sl, 0))
    return pl.pallas_call(
        body, name="s5_scan_carry_out", grid=(nsl, nch),
        in_specs=[tile(w), small, small, pl.BlockSpec((1, w, LANES), lambda sl, j: (sl, 0, 0)), tile(LANES),
                  pl.BlockSpec((1, LANES), lambda sl, j: (0, sl))],
        out_specs=[tile(w), tile(LANES), tile(LANES)],
        out_shape=[jax.ShapeDtypeStruct((s, nsl * w), F32), jax.ShapeDtypeStruct((s, d), F32),
                   jax.ShapeDtypeStruct((s, d), BF16)],
        scratch_shapes=[pltpu.VMEM((SUBLANES, w), F32)],
        compiler_params=_params("parallel", "arbitrary"),
    )(xloc, cin, ab_tile, cblk, hp, dvec)


def _s5_bwd_pass1(dzg, ypre, cblk, ab_tile, hp, rc=1024):
    s, d = hp.shape
    nsl = d // LANES
    rc = min(rc, s)
    nch = s // rc
    w = 2 * SLAB_COLS

    def body(dz_ref, y_ref, c_ref, ab_ref, h_ref, lam_ref, st_out_ref, dy_ref, dd_ref, st_ref):
        j = pl.program_id(1)

        @pl.when(j == 0)
        def _():
            st_ref[...] = jnp.zeros_like(st_ref)
            dd_ref[...] = jnp.zeros_like(dd_ref)

        dy = dz_ref[...] * _gelu_grad(y_ref[...])
        dy_ref[...] = dy
        dd_ref[0:1, :] += jnp.sum(dy * h_ref[...], axis=0, keepdims=True)
        lam_ref[...] = _dot(_bf(dy), c_ref[0], NT)
        lr, li = _scan_rows(lam_ref, rc // SUBLANES, ab_ref[:, :SLAB_COLS], ab_ref[:, SLAB_COLS:],
                            (st_ref[:, :SLAB_COLS], st_ref[:, SLAB_COLS:]), True, True)
        st_ref[:, :SLAB_COLS] = lr
        st_ref[:, SLAB_COLS:] = li

        @pl.when(j == nch - 1)
        def _():
            st_out_ref[...] = st_ref[...]

    tile = lambda wd: pl.BlockSpec((rc, wd), lambda sl, j: (nch - 1 - j, sl))
    small = pl.BlockSpec((SUBLANES, w), lambda sl, j: (sl, 0))
    return pl.pallas_call(
        body, name="s5_adjoint_local", grid=(nsl, nch),
        in_specs=[tile(LANES), tile(LANES), pl.BlockSpec((1, w, LANES), lambda sl, j: (sl, 0, 0)), small, tile(LANES)],
        out_specs=[tile(w), small, tile(LANES), pl.BlockSpec((SUBLANES, LANES), lambda sl, j: (0, sl))],
        out_shape=[jax.ShapeDtypeStruct((s, nsl * w), F32), jax.ShapeDtypeStruct((nsl * SUBLANES, w), F32),
                   jax.ShapeDtypeStruct((s, d), F32), jax.ShapeDtypeStruct((SUBLANES, d), F32)],
        scratch_shapes=[pltpu.VMEM((SUBLANES, w), F32)],
        compiler_params=_params("parallel", "arbitrary"),
    )(dzg, ypre, cblk, ab_tile, hp)


def _s5_bwd_pass2(lamloc, cinl, ab_tile, xtrue, cinx, hp, dy, bblk, dvec, rc=1024):
    s, d = hp.shape
    nsl = d // LANES
    rc = min(rc, s)
    nch = s // rc
    w = 2 * SLAB_COLS
    n_steps = rc // SUBLANES

    def body(ll_ref, cl_ref, ab_ref, x_ref, xp_ref, cx_ref, h_ref, dy_ref, b_ref, d_ref,
             du_ref, db_ref, dc_ref, da_ref, st_ref, lam_ref, acc_ref):
        j = pl.program_id(1)

        @pl.when(j == 0)
        def _():
            st_ref[...] = cl_ref[...]
            acc_ref[...] = jnp.zeros_like(acc_ref)
            db_ref[...] = jnp.zeros_like(db_ref)
            dc_ref[...] = jnp.zeros_like(dc_ref)

        ar, ai = ab_ref[:, :SLAB_COLS], ab_ref[:, SLAB_COLS:]
        lam_ref[...] = jnp.zeros_like(lam_ref)
        zr, zi = _scan_rows(lam_ref, n_steps, ar, ai, (st_ref[:, :SLAB_COLS], st_ref[:, SLAB_COLS:]), True, True)
        st_ref[:, :SLAB_COLS] = zr
        st_ref[:, SLAB_COLS:] = zi
        lam_ref[...] = lam_ref[...] + ll_ref[...]

        def step(k, carry):
            dr, di = carry
            r0 = pl.multiple_of(k * SUBLANES, SUBLANES)
            r1 = pl.multiple_of((k + 1) * SUBLANES, SUBLANES)
            xr, xi = x_ref[pl.ds(r0, SUBLANES), :SLAB_COLS], x_ref[pl.ds(r0, SUBLANES), SLAB_COLS:]
            lr, li = lam_ref[pl.ds(r1, SUBLANES), :SLAB_COLS], lam_ref[pl.ds(r1, SUBLANES), SLAB_COLS:]
            return dr + xr * lr + xi * li, di + xr * li - xi * lr

        dr, di = lax.fori_loop(0, n_steps - 1, step, (acc_ref[:, :SLAB_COLS], acc_ref[:, SLAB_COLS:]), unroll=4)
        first_chunk = j == nch - 1
        xr = jnp.where(first_chunk, cx_ref[:, :SLAB_COLS], xp_ref[:, :SLAB_COLS])
        xi = jnp.where(first_chunk, cx_ref[:, SLAB_COLS:], xp_ref[:, SLAB_COLS:])
        lr, li = lam_ref[0:SUBLANES, :SLAB_COLS], lam_ref[0:SUBLANES, SLAB_COLS:]
        acc_ref[:, :SLAB_COLS] = dr + xr * lr + xi * li
        acc_ref[:, SLAB_COLS:] = di + xr * li - xi * lr

        lam_b = _bf(lam_ref[...])
        dyv = dy_ref[...]
        db_ref[0] += _dot(_bf(h_ref[...]), lam_b, TN)
        dc_ref[0] += _dot(_bf(x_ref[...]), _bf(dyv), TN)
        du_ref[...] = _dot(lam_b, b_ref[0], NT) + d_ref[...] * dyv

        @pl.when(j == nch - 1)
        def _():
            da_ref[...] = jnp.broadcast_to(jnp.sum(acc_ref[...], axis=0, keepdims=True), (SUBLANES, w))

    sub = rc // SUBLANES
    tile = lambda wd: pl.BlockSpec((rc, wd), lambda sl, j: (nch - 1 - j, sl))
    small = pl.BlockSpec((SUBLANES, w), lambda sl, j: (sl, 0))
    prev = pl.BlockSpec((SUBLANES, w), lambda sl, j: (jnp.maximum((nch - 1 - j) * sub - 1, 0), sl))
    return pl.pallas_call(
        body, name="s5_adjoint_carry_grads", grid=(nsl, nch),
        in_specs=[tile(w), small, small, tile(w), prev, small, tile(LANES), tile(LANES),
                  pl.BlockSpec((1, LANES, w), lambda sl, j: (sl, 0, 0)), pl.BlockSpec((1, LANES), lambda sl, j: (0, sl))],
        out_specs=[tile(LANES), pl.BlockSpec((1, LANES, w), lambda sl, j: (sl, 0, 0)),
                   pl.BlockSpec((1, w, LANES), lambda sl, j: (sl, 0, 0)), small],
        out_shape=[jax.ShapeDtypeStruct((s, d), F32), jax.ShapeDtypeStruct((nsl, LANES, w), F32),
                   jax.ShapeDtypeStruct((nsl, w, LANES), F32), jax.ShapeDtypeStruct((nsl * SUBLANES, w), F32)],
        scratch_shapes=[pltpu.VMEM((SUBLANES, w), F32), pltpu.VMEM((rc, w), F32), pltpu.VMEM((SUBLANES, w), F32)],
        compiler_params=_params("parallel", "arbitrary"),
    )(lamloc, cinl, ab_tile, xtrue, xtrue, cinx, hp, dy, bblk, dvec)


def _adamw(name, w, g, m, v):
    r, c = w.shape
    tile = r if r * c <= 512 * 1024 else _pick(r, max(SUBLANES, (512 * 1024 // c) // SUBLANES * SUBLANES), q=SUBLANES)
    c1 = 1.0 / (1.0 - ADAM_B1 ** ADAM_STEP)
    c2 = 1.0 / (1.0 - ADAM_B2 ** ADAM_STEP)

    def body(w_ref, g_ref, m_ref, v_ref, d_ref, nm_ref, nv_ref):
        gg = g_ref[...]
        nm = ADAM_B1 * m_ref[...] + (1.0 - ADAM_B1) * gg
        nv = ADAM_B2 * v_ref[...] + (1.0 - ADAM_B2) * gg * gg
        d_ref[...] = -ADAM_LR * ((nm * c1) / (jnp.sqrt(nv * c2) + ADAM_EPS) + ADAM_WD * w_ref[...])
        nm_ref[...] = nm
        nv_ref[...] = nv

    spec = _nat(tile, c)
    return pl.pallas_call(
        body, name=name, grid=(r // tile,), in_specs=[spec] * 4, out_specs=[spec] * 3,
        out_shape=[jax.ShapeDtypeStruct((r, c), F32)] * 3, compiler_params=_params("parallel"),
    )(w, g, m, v)


def _place():
    x, y, c = lax.axis_index("x"), lax.axis_index("y"), lax.axis_index("c")
    return x, y, c, [(1 - x, y), (x, 1 - y), (1 - x, 1 - y)]


_ANY = pl.BlockSpec(memory_space=pl.ANY)


def _gather_weights(shards):
    n = len(shards)

    def body(*refs):
        ins, outs = refs[:n], refs[n:2 * n]
        send_sems, recv_sems, local_sems = refs[2 * n:]
        x, y, c, chips = _place()
        me = 2 * x + y
        sibling = (x, y, 1 - c)
        started = []
        for a in range(n):
            local = pltpu.make_async_copy(ins[a], outs[a].at[me], local_sems.at[a])
            local.start()
            started.append(local)

        def half(a, chip, h):
            hw = ins[a].shape[1] // 2
            return outs[a].at[chip, :, pl.ds(pl.multiple_of(h * hw, LANES), hw)]

        def copy(a, k, src, chip, h, to):
            return pltpu.make_async_remote_copy(
                src_ref=src, dst_ref=half(a, chip, h), send_sem=send_sems.at[a, k], recv_sem=recv_sems.at[a, k],
                device_id=to, device_id_type=MESH)

        sends = []
        for a in range(n):
            hw = ins[a].shape[1] // 2
            mine = ins[a].at[:, pl.ds(pl.multiple_of(c * hw, LANES), hw)]
            for k, chip in enumerate(chips):
                cp = copy(a, k, mine, me, c, (*chip, c))
                cp.start()
                sends.append(cp)
        for a in range(n):
            for k, (cx, cy) in enumerate(chips):
                src_chip = 2 * cx + cy
                copy(a, k, half(a, src_chip, c), src_chip, c, (x, y, c)).wait_recv()
                fwd = copy(a, 3 + k, half(a, src_chip, c), src_chip, c, sibling)
                fwd.start()
                sends.append(fwd)
        for a in range(n):
            for k, (cx, cy) in enumerate(chips):
                src_chip = 2 * cx + cy
                copy(a, 3 + k, half(a, src_chip, 1 - c), src_chip, 1 - c, (x, y, c)).wait_recv()
        for cp in sends:
            cp.wait_send()
        for cp in started:
            cp.wait()

    return pl.pallas_call(
        body, name="gather_weights",
        in_specs=[_ANY] * n, out_specs=[_ANY] * n,
        out_shape=[jax.ShapeDtypeStruct((N_CHIPS,) + s_.shape, s_.dtype) for s_ in shards],
        scratch_shapes=[pltpu.SemaphoreType.DMA((n, 6)), pltpu.SemaphoreType.DMA((n, 6)), pltpu.SemaphoreType.DMA((n,))],

    )(*shards)


def _gather_weights_async(shards):
    n = len(shards)
    srcs = [jax.new_ref(s_, memory_space=pltpu.MemorySpace.HBM) for s_ in shards]
    outs = [jax.empty_ref(jax.ShapeDtypeStruct((N_CHIPS,) + s_.shape, s_.dtype), memory_space=pltpu.MemorySpace.HBM)
            for s_ in shards]

    @pl.kernel(mesh=plsc.ScalarSubcoreMesh(axis_name="seq", num_cores=1), name="gather_weights_async",
               scratch_types=(pltpu.SemaphoreType.DMA((n, 6)), pltpu.SemaphoreType.DMA((n, 6)),
                              pltpu.SemaphoreType.DMA((n,))),
               compiler_params=pltpu.CompilerParams(collective_id=1))
    def launch(send_sems, recv_sems, local_sems):
        x, y, c, chips = _place()
        me = 2 * x + y
        sibling = (x, y, 1 - c)
        barrier = pltpu.get_barrier_semaphore()
        for peer in [sibling] + [(*chip, c) for chip in chips]:
            pl.semaphore_signal(barrier, inc=1, device_id=peer, device_id_type=MESH)
        pl.semaphore_wait(barrier, 4)

        def half(a, chip, h):
            hw = srcs[a].shape[1] // 2
            return outs[a].at[chip, :, pl.ds(pl.multiple_of(h * hw, LANES), hw)]

        def copy(a, k, src, chip, h, to):
            return pltpu.make_async_remote_copy(
                src_ref=src, dst_ref=half(a, chip, h), send_sem=send_sems.at[a, k], recv_sem=recv_sems.at[a, k],
                device_id=to, device_id_type=MESH)

        locals_, sends = [], []
        for a in range(n):
            local = pltpu.make_async_copy(srcs[a], outs[a].at[me], local_sems.at[a])
            local.start()
            locals_.append(local)
            hw = srcs[a].shape[1] // 2
            mine = srcs[a].at[:, pl.ds(pl.multiple_of(c * hw, LANES), hw)]
            for k, chip in enumerate(chips):
                cp = copy(a, k, mine, me, c, (*chip, c))
                cp.start()
                sends.append(cp)
        for a in range(n):
            for k, (cx, cy) in enumerate(chips):
                src_chip = 2 * cx + cy
                copy(a, k, half(a, src_chip, c), src_chip, c, (x, y, c)).wait_recv()
                fwd = copy(a, 3 + k, half(a, src_chip, c), src_chip, c, sibling)
                fwd.start()
                sends.append(fwd)
        for a in range(n):
            for k, (cx, cy) in enumerate(chips):
                src_chip = 2 * cx + cy
                copy(a, 3 + k, half(a, src_chip, 1 - c), src_chip, 1 - c, (x, y, c)).wait_recv()
        for cp in sends:
            cp.wait_send()
        for cp in locals_:
            cp.wait()

    launch()
    return [o[...] for o in outs]


def _on_sequencer(name, cid, inputs, out_shapes, sem_types, peers, body):
    srcs = [jax.new_ref(a, memory_space=pltpu.MemorySpace.HBM) for a in inputs]
    outs = [jax.empty_ref(sd, memory_space=pltpu.MemorySpace.HBM) for sd in out_shapes]

    @pl.kernel(mesh=plsc.ScalarSubcoreMesh(axis_name="seq", num_cores=1), name=name, scratch_types=tuple(sem_types),
               compiler_params=pltpu.CompilerParams(collective_id=cid))
    def launch(*sems):
        x, y, c, chips = _place()
        barrier = pltpu.get_barrier_semaphore()
        ps = peers(x, y, c, chips)
        for peer in ps:
            pl.semaphore_signal(barrier, inc=1, device_id=peer, device_id_type=MESH)
        pl.semaphore_wait(barrier, len(ps))
        body(srcs, outs, *sems)

    launch()
    return [o[...] for o in outs]


def _sibling_only(x, y, c, chips):
    return [(x, y, 1 - c)]


def _same_core_of_other_chips(x, y, c, chips):
    return [(*chip, c) for chip in chips]


def _swap_halves_to_sibling(name, cid, grads):
    n = len(grads)

    def body(ins, outs, send_sems, recv_sems):
        x, y, c, _ = _place()
        cps = []
        for a in range(n):
            hw = ins[a].shape[2] // 2
            src = ins[a].at[:, :, pl.ds(pl.multiple_of((1 - c) * hw, LANES), hw)]
            cp = pltpu.make_async_remote_copy(src_ref=src, dst_ref=outs[a], send_sem=send_sems.at[a],
                                              recv_sem=recv_sems.at[a], device_id=(x, y, 1 - c), device_id_type=MESH)
            cp.start()
            cps.append(cp)
        for cp in cps:
            cp.wait()

    return _on_sequencer(
        name, cid, grads, [jax.ShapeDtypeStruct(g.shape[:2] + (g.shape[2] // 2,), g.dtype) for g in grads],
        [pltpu.SemaphoreType.DMA((n,)), pltpu.SemaphoreType.DMA((n,))], _sibling_only, body)


def _exchange_quarters(name, cid, parts):
    n = len(parts)

    def body(ins, outs, send_sems, recv_sems):
        x, y, c, chips = _place()
        cps = []
        for a in range(n):
            for k, (cx, cy) in enumerate(chips):
                cp = pltpu.make_async_remote_copy(
                    src_ref=ins[a].at[2 * cx + cy], dst_ref=outs[a].at[k], send_sem=send_sems.at[a, k],
                    recv_sem=recv_sems.at[a, k], device_id=(cx, cy, c), device_id_type=MESH)
                cp.start()
                cps.append(cp)
        for cp in cps:
            cp.wait()

    return _on_sequencer(
        name, cid, parts, [jax.ShapeDtypeStruct((3,) + p_.shape[1:], p_.dtype) for p_ in parts],
        [pltpu.SemaphoreType.DMA((n, 3)), pltpu.SemaphoreType.DMA((n, 3))], _same_core_of_other_chips, body)


def _swap_final_halves(name, cid, halves):
    n = len(halves)

    def body(ins, outs, send_sems, recv_sems):
        x, y, c, _ = _place()
        cps = []
        for a in range(n):
            cp = pltpu.make_async_remote_copy(src_ref=ins[a], dst_ref=outs[a], send_sem=send_sems.at[a],
                                              recv_sem=recv_sems.at[a], device_id=(x, y, 1 - c), device_id_type=MESH)
            cp.start()
            cps.append(cp)
        for cp in cps:
            cp.wait()

    return _on_sequencer(
        name, cid, halves, [jax.ShapeDtypeStruct(h.shape, h.dtype) for h in halves],
        [pltpu.SemaphoreType.DMA((n,)), pltpu.SemaphoreType.DMA((n,))], _sibling_only, body)


def _add_half(name, grad, recv):
    nchip, r, cfull = grad.shape
    hw = cfull // 2
    tile = _pick(r, max(BF16_ROWS, (256 * 1024 // hw) // BF16_ROWS * BF16_ROWS), q=BF16_ROWS)
    c = lax.axis_index("c")

    def body(c_ref, g_ref, r_ref, o_ref):
        o_ref[...] = _bf(g_ref[...] + r_ref[...])

    return pl.pallas_call(
        body, name=name,
        grid_spec=pltpu.PrefetchScalarGridSpec(
            num_scalar_prefetch=1, grid=(nchip, r // tile),
            in_specs=[pl.BlockSpec((1, tile, hw), lambda k, i, cr: (k, i, cr[0])),
                      pl.BlockSpec((1, tile, hw), lambda k, i, cr: (k, i, 0))],
            out_specs=pl.BlockSpec((1, tile, hw), lambda k, i, cr: (k, i, 0))),
        out_shape=jax.ShapeDtypeStruct((nchip, r, hw), BF16), compiler_params=_params("parallel", "parallel"),
    )(c.reshape(1).astype(jnp.int32), grad, recv)


def _add_quarters(name, part, recv):
    _, r, hw = part.shape
    tile = _pick(r, max(BF16_ROWS, (256 * 1024 // hw) // BF16_ROWS * BF16_ROWS), q=BF16_ROWS)
    me = 2 * lax.axis_index("x") + lax.axis_index("y")

    def body(me_ref, p_ref, r_ref, o_ref):
        f = lambda v: v.astype(F32)
        o_ref[...] = ((f(p_ref[0]) + f(r_ref[0])) + f(r_ref[1])) + f(r_ref[2])

    return pl.pallas_call(
        body, name=name,
        grid_spec=pltpu.PrefetchScalarGridSpec(
            num_scalar_prefetch=1, grid=(r // tile,),
            in_specs=[pl.BlockSpec((1, tile, hw), lambda i, mr: (mr[0], i, 0)),
                      pl.BlockSpec((3, tile, hw), lambda i, mr: (0, i, 0))],
            out_specs=pl.BlockSpec((tile, hw), lambda i, mr: (i, 0))),
        out_shape=jax.ShapeDtypeStruct((r, hw), F32), compiler_params=_params("parallel"),
    )(me.reshape(1).astype(jnp.int32), part, recv)


class _ReduceScatter:
    def __init__(self, tag, first_cid, grads):
        self.tag, self.cid = tag, first_cid
        self.stacks = [g.reshape(N_CHIPS, g.shape[0] // N_CHIPS, g.shape[1]) for g in grads]

    def start(self, anchor):
        self.stacks, anchor = lax.optimization_barrier((self.stacks, anchor))
        self.recv = _swap_halves_to_sibling(f"rs_swap_halves_{self.tag}", self.cid, self.stacks)
        return anchor

    def exchange(self, anchor):
        parts = [_add_half(f"rs_add_half_{self.tag}{a}", g, r) for a, (g, r) in enumerate(zip(self.stacks, self.recv))]
        self.parts, anchor = lax.optimization_barrier((parts, anchor))
        self.quarters = _exchange_quarters(f"rs_exchange_{self.tag}", self.cid + 1, self.parts)
        return anchor

    def join(self, anchor):
        halves = [_add_quarters(f"rs_add_quarters_{self.tag}{a}", p_, q_)
                  for a, (p_, q_) in enumerate(zip(self.parts, self.quarters))]
        self.halves, anchor = lax.optimization_barrier((halves, anchor))
        self.others = _swap_final_halves(f"rs_swap_final_{self.tag}", self.cid + 2, self.halves)
        return anchor

    def result(self):
        south = lax.axis_index("c") == 0
        return [jnp.concatenate([jnp.where(south, h, o), jnp.where(south, o, h)], axis=1)
                for h, o in zip(self.halves, self.others)]


def _allgather_small(pack):
    m_per, n = pack.shape

    def body(x_ref, out_ref, send_sems, recv_sems, local_sem):
        x, y, c, chips = _place()
        me, sibling = (x, y, c), (x, y, 1 - c)

        def rows(px, py, pc):
            return out_ref.at[pl.ds(pl.multiple_of((4 * px + 2 * py + pc) * m_per, SUBLANES), m_per), :]

        def copy(k, block, to, src=None):
            return pltpu.make_async_remote_copy(
                src_ref=rows(*block) if src is None else src, dst_ref=rows(*block),
                send_sem=send_sems.at[k], recv_sem=recv_sems.at[k], device_id=to, device_id_type=MESH)

        mine = pltpu.make_async_copy(x_ref, rows(*me), local_sem)
        mine.start()
        first = [copy(0, me, sibling, src=x_ref)]
        first += [copy(1 + j, me, (*chip, c), src=x_ref) for j, chip in enumerate(chips)]
        for cp in first:
            cp.start()
        passed = [copy(4 + j, (*chip, c), sibling) for j, chip in enumerate(chips)]
        for j, chip in enumerate(chips):
            copy(1 + j, (*chip, c), me).wait_recv()
            passed[j].start()
        copy(0, sibling, me).wait_recv()
        for j, chip in enumerate(chips):
            copy(4 + j, (*chip, 1 - c), me).wait_recv()
        for cp in first + passed:
            cp.wait_send()
        mine.wait()

    return pl.pallas_call(
        body, name="allgather_small_grads",
        out_shape=jax.ShapeDtypeStruct((N_DEV * m_per, n), pack.dtype),
        in_specs=[pl.BlockSpec(memory_space=pltpu.VMEM)], out_specs=pl.BlockSpec(memory_space=pltpu.VMEM),
        scratch_shapes=[pltpu.SemaphoreType.DMA((7,)), pltpu.SemaphoreType.DMA((7,)), pltpu.SemaphoreType.DMA],
        compiler_params=pltpu.CompilerParams(vmem_limit_bytes=VMEM_LIMIT),
    )(pack)


def _sum_devices(packs, m_per):
    tile = _pick(m_per, 512, q=SUBLANES)
    nt = m_per // tile

    def body(*refs):
        acc = refs[0][...]
        for r in refs[1:N_DEV]:
            acc = acc + r[...]
        refs[N_DEV][...] = acc

    return pl.pallas_call(
        body, name="sum_small_grads", grid=(nt,),
        in_specs=[pl.BlockSpec((tile, LANES), functools.partial(lambda i, k: (k * nt + i, 0), k=k)) for k in range(N_DEV)],
        out_specs=_nat(tile, LANES), out_shape=jax.ShapeDtypeStruct((m_per, LANES), F32),
        compiler_params=_params("parallel"),
    )(*([packs] * N_DEV))


def _tail_fwd(tag, alpha, h_in, adds, mix_gate, ln1, ln2, p_l, w, want_perm):
    h_mid, xh1, rs1, h_mid_b, _ = _ln_fwd(f"ln1_fwd_{tag}", alpha, h_in, adds, mix_gate, *ln1)
    gp = _matmul(f"ple_gate_fwd_{tag}", h_mid_b, w['wg'], 'nn')
    pw = _matmul(f"ple_proj_fwd_{tag}", p_l, w['plet'], 'nt')
    gu = _matmul(f"ffn_in_fwd_{tag}", h_mid_b, w['wit'], 'nt', tn=1408)
    act = _swiglu_fwd(f"swiglu_fwd_{tag}", gu)
    ffn = _matmul(f"ffn_out_fwd_{tag}", act, w['wo'], 'nn', tk=2816)
    h_out, xh2, rs2, _, h_perm = _ln_fwd(f"ln2_fwd_{tag}", alpha, h_mid, [(ffn, 'nat')],
                                         ('nat', (pw, 1, 0), (gp, 1, 0)), *ln2, want_perm=want_perm)
    saved = dict(h_mid_b=h_mid_b, xh1=xh1, rs1=rs1, gp=gp, pw=pw, gu=gu, act=act, xh2=xh2, rs2=rs2)
    return h_out, h_perm, saved


def _tail_bwd(tag, alpha, dparts, sv, ln1_g, ln2_g, p_l, w, mix_gate):
    d = sv['h_mid_b'].shape[1]
    dz2, dz2b, dgate, dg2, db2 = _ln_bwd(f"ln2_bwd_{tag}", dparts, sv['xh2'], sv['rs2'], ln2_g,
                                         gate=('nat', (sv['pw'], 1, 0), (sv['gp'], 1, 0)))
    grads = dict(ln2_g=dg2, ln2_b=db2)
    grads['plet'] = _matmul(f"ple_proj_dw_{tag}", dgate, p_l, 'tn', a_win=(0, d))
    grads['wg'] = _matmul(f"ple_gate_dw_{tag}", sv['h_mid_b'], dgate, 'tn', b_win=(d, d))
    dx_gate = _matmul(f"ple_gate_dx_{tag}", dgate, w['wg'], 'nt', a_win=(d, d))
    dact = _matmul(f"ffn_out_dx_{tag}", dz2b, w['wo'], 'nt', out_dtype=BF16, tn=1408)
    grads['wo'] = _matmul(f"ffn_out_dw_{tag}", sv['act'], dz2b, 'tn', tm=1408)
    dgu = _swiglu_bwd(f"swiglu_bwd_{tag}", sv['gu'], dact)
    grads['wit'] = _matmul(f"ffn_in_dw_{tag}", dgu, sv['h_mid_b'], 'tn')
    dx_ffn = _matmul(f"ffn_in_dx_{tag}", dgu, w['wit'], 'nn', tk=2816)
    res = _ln_bwd(f"ln1_bwd_{tag}", [(dz2, 'nat', alpha), (dx_gate, 'nat', 1.0), (dx_ffn, 'nat', 1.0)],
                  sv['xh1'], sv['rs1'], ln1_g, gate=mix_gate)
    grads['ln1_g'], grads['ln1_b'] = res[-2], res[-1]
    return res[:-2], grads


def kernel(x, p, positions, attn_w_in, mla_q_norm, mla_w_q_b, mla_kv_norm, mla_w_kv_b, attn_w_out, s5_a_re, s5_a_im, s5_log_dt, s5_b_re, s5_b_im, s5_c_re, s5_c_im, s5_d, s5_w_glu, ln1_g, ln1_b, ffn_w_in, ffn_w_out, ple_w, ple_gate_w, ln2_g, ln2_b, loss_target, m_attn_w_in, m_mla_q_norm, m_mla_w_q_b, m_mla_kv_norm, m_mla_w_kv_b, m_attn_w_out, m_s5_a_re, m_s5_a_im, m_s5_log_dt, m_s5_b_re, m_s5_b_im, m_s5_c_re, m_s5_c_im, m_s5_d, m_s5_w_glu, m_ln1_g, m_ln1_b, m_ffn_w_in, m_ffn_w_out, m_ple_w, m_ple_gate_w, m_ln2_g, m_ln2_b, v_attn_w_in, v_mla_q_norm, v_mla_w_q_b, v_mla_kv_norm, v_mla_w_kv_b, v_attn_w_out, v_s5_a_re, v_s5_a_im, v_s5_log_dt, v_s5_b_re, v_s5_b_im, v_s5_c_re, v_s5_c_im, v_s5_d, v_s5_w_glu, v_ln1_g, v_ln1_b, v_ffn_w_in, v_ffn_w_out, v_ple_w, v_ple_gate_w, v_ln2_g, v_ln2_b):
    weights = dict(attn_w_in=attn_w_in, mla_q_norm=mla_q_norm, mla_w_q_b=mla_w_q_b, mla_kv_norm=mla_kv_norm,
                   mla_w_kv_b=mla_w_kv_b, attn_w_out=attn_w_out, s5_a_re=s5_a_re, s5_a_im=s5_a_im, s5_log_dt=s5_log_dt,
                   s5_b_re=s5_b_re, s5_b_im=s5_b_im, s5_c_re=s5_c_re, s5_c_im=s5_c_im, s5_d=s5_d, s5_w_glu=s5_w_glu,
                   ln1_g=ln1_g, ln1_b=ln1_b, ffn_w_in=ffn_w_in, ffn_w_out=ffn_w_out, ple_w=ple_w, ple_gate_w=ple_gate_w,
                   ln2_g=ln2_g, ln2_b=ln2_b)
    m_in = dict(attn_w_in=m_attn_w_in, mla_q_norm=m_mla_q_norm, mla_w_q_b=m_mla_w_q_b, mla_kv_norm=m_mla_kv_norm,
                mla_w_kv_b=m_mla_w_kv_b, attn_w_out=m_attn_w_out, s5_a_re=m_s5_a_re, s5_a_im=m_s5_a_im,
                s5_log_dt=m_s5_log_dt, s5_b_re=m_s5_b_re, s5_b_im=m_s5_b_im, s5_c_re=m_s5_c_re, s5_c_im=m_s5_c_im,
                s5_d=m_s5_d, s5_w_glu=m_s5_w_glu, ln1_g=m_ln1_g, ln1_b=m_ln1_b, ffn_w_in=m_ffn_w_in,
                ffn_w_out=m_ffn_w_out, ple_w=m_ple_w, ple_gate_w=m_ple_gate_w, ln2_g=m_ln2_g, ln2_b=m_ln2_b)
    v_in = dict(attn_w_in=v_attn_w_in, mla_q_norm=v_mla_q_norm, mla_w_q_b=v_mla_w_q_b, mla_kv_norm=v_mla_kv_norm,
                mla_w_kv_b=v_mla_w_kv_b, attn_w_out=v_attn_w_out, s5_a_re=v_s5_a_re, s5_a_im=v_s5_a_im,
                s5_log_dt=v_s5_log_dt, s5_b_re=v_s5_b_re, s5_b_im=v_s5_b_im, s5_c_re=v_s5_c_re, s5_c_im=v_s5_c_im,
                s5_d=v_s5_d, s5_w_glu=v_s5_w_glu, ln1_g=v_ln1_g, ln1_b=v_ln1_b, ffn_w_in=v_ffn_w_in,
                ffn_w_out=v_ffn_w_out, ple_w=v_ple_w, ple_gate_w=v_ple_gate_w, ln2_g=v_ln2_g, ln2_b=v_ln2_b)
    names = list(weights)

    s, d = x.shape[1], x.shape[2]
    depth = ln1_g.shape[0]
    assert depth == 2
    alpha = (2.0 * depth) ** 0.25
    ql, kvl = mla_q_norm.shape[1], mla_kv_norm.shape[1]
    in_cols = N_CHIPS * attn_w_in.shape[2]
    heads = N_CHIPS * mla_w_q_b.shape[2] // (NOPE + ROPE)
    hps = heads // N_CHIPS
    dw = (in_cols - ql - kvl - ROPE) // 3
    dh = dw // DHD
    assert ql % LANES == 0 and kvl == ql and dw % DHD == 0 and heads % N_CHIPS == 0
    ngroups, nstate = s5_a_re.shape[1], s5_a_re.shape[2]
    assert nstate == S5_STATE and ngroups * S5_GROUP == d and d % LANES == 0
    nsl = d // LANES
    seg_len = s // SUBLANES
    n_sq = seg_len.bit_length() - 1
    assert 1 << n_sq == seg_len, "the segment length of the S5 scan must be a power of two"
    for window, dil in DIL_BRANCHES:
        assert window // dil == DIL_STEPS and (s // dil) % DIL_STEPS == 0
    me = 2 * lax.axis_index("x") + lax.axis_index("y")

    xb = x[0]
    target = loss_target[0]
    p_layers = [p[0, 0], p[1, 0]]
    pos = positions[0].astype(F32).reshape(s, 1)
    inv_freq = ROPE_THETA ** (-jnp.arange(ROPE // 2, dtype=F32) / (ROPE // 2))
    invf = jnp.concatenate([inv_freq, inv_freq, jnp.zeros((LANES - ROPE,), F32)]).reshape(1, LANES)
    slopes = 2.0 ** (-8.0 * jnp.arange(1, dh + 1, dtype=F32) / dh)
    slopes = jnp.broadcast_to(jnp.repeat(slopes, SUBLANES)[:, None], (dh * SUBLANES, LANES))

    wqb_t = mla_w_q_b[0].T.reshape(hps, NOPE + ROPE, ql)
    wqb_t = jnp.pad(wqb_t, ((0, 0), (0, QK_PAD - NOPE - ROPE), (0, 0))).reshape(hps * QK_PAD, ql)
    d_cols = max(d // N_CHIPS, 2 * LANES)
    d_pad = jnp.zeros((SUBLANES, d_cols), F32).at[0, :d // N_CHIPS].set(s5_d[0])
    shards = [_bf(attn_w_in[0].T), _bf(wqb_t), _bf(mla_w_kv_b[0].T), _bf(attn_w_out[0]), _bf(s5_w_glu[0].T)]
    for l in range(depth):
        shards += [_bf(ffn_w_in[l].T), _bf(ffn_w_out[l]), _bf(ple_w[l].T), _bf(ple_gate_w[l])]
    shards.append(d_pad)
    first, later = lax.optimization_barrier((list(_gather_weights(shards[:3])), shards[3:]))
    gathered = first + _gather_weights_async(later)
    full = [g.reshape(N_CHIPS * g.shape[1], g.shape[2]) for g in gathered]
    win_t, wqb_t_f, wkv_t, wout, wglu_t = full[:5]
    lw = [dict(wit=full[5 + 4 * l], wo=full[6 + 4 * l], plet=full[7 + 4 * l], wg=full[8 + 4 * l]) for l in range(depth)]
    dvec = full[-1].reshape(N_CHIPS, SUBLANES, d_cols)[:, 0, :d // N_CHIPS].reshape(1, d)
    lat = ql + kvl
    win_t = jnp.concatenate([win_t[:lat + ROPE], jnp.zeros((LANES - ROPE, d), BF16), win_t[lat + ROPE:]], axis=0)
    kpe_cb = lat // LANES
    q_cb = kpe_cb + 1
    a_cb = heads * VDIM // LANES

    xbb = _bf(xb)
    proj = _matmul("attn_in_fwd", xbb, win_t, 'nt', tn=1408)
    nrm = _rms_fwd(proj, ql, kvl, mla_q_norm[0], mla_kv_norm[0])
    q_raw = _matmul("mla_q_up_fwd", nrm, wqb_t_f, 'nt', a_win=(0, ql))
    kv = _matmul("mla_kv_up_fwd", nrm, wkv_t, 'nt', a_win=(ql, kvl))
    qf, kf, vv = _rope_prep(q_raw, kv, proj, kpe_cb, pos, invf, heads)
    out_a, lse_a = _mla_fwd(qf, kf, vv.T, heads)
    out_b, lse_b = _dil_fused_fwd(proj, slopes, dh, q_cb)
    att = _concat_bf16("attn_heads_concat", out_a, out_b)
    mix0 = _matmul("attn_out_fwd", att, wout, 'nn')
    h2, h2p, sv0 = _tail_fwd("l0", alpha, xb, [(mix0, 'nat')], None, (ln1_g[0], ln1_b[0]), (ln2_g[0], ln2_b[0]),
                             p_layers[0], lw[0], want_perm=True)

    rep = lambda a: jnp.repeat(a, S5_GROUP, axis=0)
    ag = (s5_a_re[0], s5_a_im[0], jnp.broadcast_to(s5_log_dt[0][:, None], (ngroups, nstate)))
    a16 = tuple(rep(a) for a in ag)
    b16 = tuple(b[0].transpose(0, 2, 1).reshape(ngroups * S5_GROUP, nstate) for b in (s5_b_re, s5_b_im))
    abr, abi, apr, api, bbr, bbi = _s5_discretise(*a16, *b16, n_sq)
    ab_tile = _slab_tile(abr[::S5_GROUP], abi[::S5_GROUP], nsl)
    ap_tile = _slab_tile(apr[::S5_GROUP], api[::S5_GROUP], nsl)
    bblk = _bf(_slab_in_matrix(bbr.reshape(ngroups, S5_GROUP, nstate), bbi.reshape(ngroups, S5_GROUP, nstate), nsl))
    cblk = _bf(_slab_out_matrix(s5_c_re[0], s5_c_im[0], nsl))
    xloc, ends = _s5_pass1(h2p, bblk, ab_tile)
    cinx = _s5_carry("s5_carry_fwd", ends, ap_tile, False)
    xtrue, ypre, zg = _s5_pass2(xloc, cinx, ab_tile, cblk, h2p, dvec)
    vg = _matmul("s5_glu_fwd", zg, wglu_t, 'nt')
    glu_gate = ('perm', (vg, 2, 0), (vg, 2, 1))
    h4, _, sv1 = _tail_fwd("l1", alpha, h2, [], glu_gate, (ln1_g[1], ln1_b[1]), (ln2_g[1], ln2_b[1]),
                           p_layers[1], lw[1], want_perm=False)
    loss = lax.psum(jnp.sum(_loss_partial(h4, target)), ("x", "y", "c"))

    (dz1_1, _, dvg), g1 = _tail_bwd("l1", alpha, [(h4, 'nat', 1.0 / d), (target, 'nat', -1.0 / d)], sv1, ln1_g[1],
                                    ln2_g[1], p_layers[1], lw[1], glu_gate)
    d_wglu_t = _matmul("s5_glu_dw", dvg, zg, 'tn')
    dzg = _matmul("s5_glu_dx", dvg, wglu_t, 'nn')
    rs_l1 = _ReduceScatter("l1", 2, [d_wglu_t, g1['wit'], g1['wo'], g1['plet'], g1['wg']])
    dzg = rs_l1.start(dzg)
    lamloc, starts, dy, dd = _s5_bwd_pass1(dzg, ypre, cblk, ab_tile, h2p)
    cinl = _s5_carry("s5_carry_bwd", starts, ap_tile, True)
    du_p, d_bblk, d_cblk, d_ab = _s5_bwd_pass2(lamloc, cinl, ab_tile, xtrue, cinx, h2p, dy, bblk, dvec)
    gbb = _slab_in_extract(d_bblk, nsl)
    g_c_re, g_c_im = _slab_out_extract(d_cblk, nsl)
    d_ab = d_ab[::SUBLANES]
    gab = (d_ab[:, :SLAB_COLS].reshape(ngroups, nstate), d_ab[:, SLAB_COLS:].reshape(ngroups, nstate))
    g_a_re, g_a_im, g_log_dt, g_b_re, g_b_im = _s5_discretise_bwd(a16, b16, ag, gab, gbb)
    unt = lambda b: b.reshape(ngroups, S5_GROUP, nstate).transpose(0, 2, 1)

    du_p = rs_l1.exchange(du_p)
    (dz1_0, dz1_0b), g0 = _tail_bwd("l0", alpha, [(dz1_1, 'nat', alpha), (du_p, 'perm', 1.0)], sv0, ln1_g[0], ln2_g[0],
                                    p_layers[0], lw[0], None)
    dz1_0b = rs_l1.join(dz1_0b)
    rs_l0 = _ReduceScatter("l0", 5, [g0['wit'], g0['wo'], g0['plet'], g0['wg']])
    dz1_0b = rs_l0.start(dz1_0b)
    datt = _matmul("attn_out_dx", dz1_0b, wout, 'nt')
    d_wout = _matmul("attn_out_dw", att, dz1_0b, 'tn')
    do, delta, delta_t = _attn_bwd_prep(datt, out_a, out_b)
    dqf, dkf, dvv = _mla_bwd(qf, kf, vv, do, lse_a, delta_t, heads, 0)
    dqf = rs_l0.exchange(dqf)
    dq_raw, dkv, dkpe = _rope_unprep(dqf, dkf, dvv, pos, invf, heads)
    d_wqb_t = _matmul("mla_q_up_dw", dq_raw, nrm, 'tn', b_win=(0, ql))
    d_wkv_t = _matmul("mla_kv_up_dw", dkv, nrm, 'tn', b_win=(ql, kvl))
    dnq = _matmul("mla_q_up_dx", dq_raw, wqb_t_f, 'nn')
    dnkv = _matmul("mla_kv_up_dx", dkv, wkv_t, 'nn')
    dqd, dkd, dvd = _dil_fused_bwd(proj, slopes, datt, lse_b, delta, dh, q_cb, a_cb)
    dkpe = rs_l0.join(dkpe)
    dproj, g_gq, g_gkv = _dproj_assemble(proj, dnq, dnkv, dkpe, [dqd], [dkd], [dvd], mla_q_norm[0], mla_kv_norm[0], ql)
    d_win_t = _matmul("attn_in_dw", dproj, xbb, 'tn', tm=1408)
    dx_attn = _matmul("attn_in_dx", dproj, win_t, 'nn')
    grad_x = _axpy("grad_x", alpha, dz1_0, dx_attn)

    d_win_t = jnp.concatenate([d_win_t[:lat + ROPE], d_win_t[lat + LANES:]], axis=0)
    rs_at = _ReduceScatter("attn", 8, [d_win_t, d_wqb_t, d_wkv_t, d_wout])
    grad_x = rs_at.start(grad_x)
    grad_x = rs_at.exchange(grad_x)
    grad_x = rs_at.join(grad_x)
    r_win, r_wqb, r_wkv, r_wout = rs_at.result()
    r_wglu, r_wit1, r_wo1, r_plet1, r_wg1 = rs_l1.result()
    r_wit0, r_wo0, r_plet0, r_wg0 = rs_l0.result()
    r_wqb = r_wqb.reshape(hps, QK_PAD, ql)[:, :NOPE + ROPE].reshape(hps * (NOPE + ROPE), ql)
    grads = dict(attn_w_in=r_win.T[None], mla_w_q_b=r_wqb.T[None], mla_w_kv_b=r_wkv.T[None], attn_w_out=r_wout[None],
                 s5_w_glu=r_wglu.T[None],
                 ffn_w_in=jnp.stack([r_wit0.T, r_wit1.T]), ffn_w_out=jnp.stack([r_wo0, r_wo1]),
                 ple_w=jnp.stack([r_plet0.T, r_plet1.T]), ple_gate_w=jnp.stack([r_wg0, r_wg1]))

    small = dict(mla_q_norm=g_gq, mla_kv_norm=g_gkv, s5_a_re=g_a_re, s5_a_im=g_a_im, s5_log_dt=g_log_dt,
                 s5_b_re=unt(g_b_re), s5_b_im=unt(g_b_im), s5_c_re=g_c_re, s5_c_im=g_c_im, s5_d=dd[0],
                 ln1_g=jnp.stack([g0['ln1_g'], g1['ln1_g']]), ln1_b=jnp.stack([g0['ln1_b'], g1['ln1_b']]),
                 ln2_g=jnp.stack([g0['ln2_g'], g1['ln2_g']]), ln2_b=jnp.stack([g0['ln2_b'], g1['ln2_b']]))
    flat = jnp.concatenate([v_.reshape(-1) for v_ in small.values()])
    m_per = -(-flat.shape[0] // (LANES * SUBLANES)) * SUBLANES
    pack = jnp.pad(flat, (0, m_per * LANES - flat.shape[0])).reshape(m_per, LANES)
    total = _sum_devices(_allgather_small(pack), m_per).reshape(-1)
    off = 0
    for k_, v_ in small.items():
        n_ = v_.size
        piece = total[off:off + n_]
        off += n_
        if k_ == 's5_d':
            grads[k_] = lax.dynamic_slice(piece, (me * (d // N_CHIPS),), (d // N_CHIPS,)).reshape(weights[k_].shape)
        else:
            grads[k_] = piece.reshape(weights[k_].shape)

    deltas, new_m, new_v = {}, {}, {}
    for k_ in names:
        w_ = weights[k_]
        shape = w_.shape
        if w_.ndim == 3 and w_.shape[-1] >= LANES:
            two_d = (shape[0] * shape[1], shape[2])
        elif w_.ndim == 4:
            two_d = (shape[0] * shape[1], shape[2] * shape[3])
        else:
            two_d = (1, w_.size) if w_.ndim == 2 and shape[0] == 1 else (shape[0], w_.size // shape[0])
        dl, nm, nv = _adamw(f"adamw_{k_}", w_.reshape(two_d), grads[k_].reshape(two_d), m_in[k_].reshape(two_d),
                            v_in[k_].reshape(two_d))
        deltas[k_], new_m[k_], new_v[k_] = dl.reshape(shape), nm.reshape(shape), nv.reshape(shape)

    return (loss, grad_x[None], *[grads[k_] for k_ in names], *[deltas[k_] for k_ in names],
            *[new_m[k_] for k_ in names], *[new_v[k_] for k_ in names])
```

```python
import functools
import math

import jax
import jax.numpy as jnp
from jax import lax
from jax.experimental import pallas as pl
from jax.experimental.pallas import tpu as pltpu
from jax.experimental.pallas import tpu_sc as plsc

F32 = jnp.float32
BF16 = jnp.bfloat16
MESH = pl.DeviceIdType.MESH

LANES = 128
SUBLANES = 8
BF16_ROWS = 16
VMEM_LIMIT = 48 * 2 ** 20
N_CHIPS = 4
N_DEV = 8

NOPE = 128
ROPE = 64
VDIM = 128
QK_PAD = 256
DHD = 128
DIL_STEPS = 128
DIL_BRANCHES = ((128, 1), (512, 4), (2048, 16))
ROPE_THETA = 10000.0
S5_GROUP = 16
S5_STATE = 64
SLAB_GROUPS = LANES // S5_GROUP
SLAB_COLS = SLAB_GROUPS * S5_STATE
NEG = -1e30
LN_EPS = 1e-5
RMS_EPS = 1e-6

ADAM_LR = 0.001
ADAM_B1 = 0.9
ADAM_B2 = 0.999
ADAM_EPS = 1e-08
ADAM_WD = 0.01
ADAM_STEP = 10

NN = ((1,), (0,))
NT = ((1,), (1,))
TN = ((0,), (0,))


def _dot(a, b, dims):
    return lax.dot_general(a, b, (dims, ((), ())), preferred_element_type=F32)


def _bf(v):
    return v.astype(BF16)


def _pick(n, target, q=LANES, also=0):
    g = math.gcd(n, also) if also else n
    if g <= target and g == n:
        return n
    best = None
    for t in range(q, min(g, target) + 1, q):
        if g % t == 0:
            best = t
    assert best is not None, (n, target, q, also)
    return best


def _params(*sem):
    return pltpu.CompilerParams(dimension_semantics=sem, vmem_limit_bytes=VMEM_LIMIT)


def _sigmoid(v):
    return 1.0 / (1.0 + jnp.exp(-v))


def _matmul(name, a, b, form, out_dtype=F32, a_win=None, b_win=None, tm=1024, tn=1024, tk=2048):
    c0, aw = a_win if a_win else (0, a.shape[1])
    if form == 'nt':
        assert b_win is None
        n, kdim = b.shape
        d0 = 0
    else:
        kdim = b.shape[0]
        d0, n = b_win if b_win else (0, b.shape[1])
    if form == 'tn':
        m = aw
        assert a.shape[0] == kdim, (name, a.shape, b.shape)
        tm = _pick(m, tm, also=c0)
        tk = _pick(kdim, tk)
        a_off = c0 // tm
    else:
        m = a.shape[0]
        assert aw == kdim, (name, a.shape, b.shape, a_win)
        tm = _pick(m, tm)
        tk = _pick(kdim, tk, also=c0)
        a_off = c0 // tk
    tn = _pick(n, tn, also=d0)
    b_off = d0 // tn
    nk = kdim // tk
    dims = {'nn': NN, 'nt': NT, 'tn': TN}[form]

    def body(a_ref, b_ref, o_ref, *acc):
        prod = _dot(_bf(a_ref[...]), _bf(b_ref[...]), dims)
        if nk == 1:
            o_ref[...] = prod.astype(o_ref.dtype)
            return
        acc_ref, = acc
        k = pl.program_id(2)

        @pl.when(k == 0)
        def _():
            acc_ref[...] = prod

        @pl.when((k > 0) & (k < nk - 1))
        def _():
            acc_ref[...] += prod

        @pl.when(k == nk - 1)
        def _():
            o_ref[...] = (acc_ref[...] + prod).astype(o_ref.dtype)

    if form == 'tn':
        a_spec = pl.BlockSpec((tk, tm), lambda i, j, k: (k, i + a_off))
    else:
        a_spec = pl.BlockSpec((tm, tk), lambda i, j, k: (i, k + a_off))
    if form == 'nt':
        b_spec = pl.BlockSpec((tn, tk), lambda i, j, k: (j, k))
    else:
        b_spec = pl.BlockSpec((tk, tn), lambda i, j, k: (k, j + b_off))
    return pl.pallas_call(
        body, name=name,
        grid=(m // tm, n // tn, nk),
        in_specs=[a_spec, b_spec],
        out_specs=pl.BlockSpec((tm, tn), lambda i, j, k: (i, j)),
        out_shape=jax.ShapeDtypeStruct((m, n), out_dtype),
        scratch_shapes=[pltpu.VMEM((tm, tn), F32)] if nk > 1 else [],
        compiler_params=_params("parallel", "parallel", "arbitrary"),
    )(a, b)


def _nat(tile, width, cb=0):
    return pl.BlockSpec((tile, width), lambda i: (i, cb))


def _perm(tile, width, seg_tiles, ncb=1, cb=0):
    return pl.BlockSpec((tile, width), lambda i: (i % seg_tiles, (i // seg_tiles) * ncb + cb))


def _whole(shape):
    return pl.BlockSpec(shape, lambda i: (0,) * len(shape))


def _perm_view(a):
    s, w = a.shape
    return a.reshape(s // SUBLANES, SUBLANES * w)


def _row_spec(a, layout, tile, width, ncb=1, cb=0):
    if layout == 'nat':
        return a, _nat(tile, width, cb)
    seg_tiles = a.shape[0] // SUBLANES // tile
    return _perm_view(a), _perm(tile, width, seg_tiles, ncb, cb)


def _ln_fwd(name, alpha, a, adds, gate, g, b, want_perm=False, tile=256):
    s, d = a.shape
    n_add = len(adds)
    has_gate = gate is not None

    def body(*refs):
        a_ref = refs[0]
        add_refs = refs[1:1 + n_add]
        pos = 1 + n_add
        if has_gate:
            val_ref, pre_ref = refs[pos], refs[pos + 1]
            pos += 2
        g_ref, b_ref = refs[pos], refs[pos + 1]
        outs = refs[pos + 2:]
        z = alpha * a_ref[...]
        for r in add_refs:
            z = z + r[...]
        if has_gate:
            z = z + val_ref[...] * _sigmoid(pre_ref[...])
        mu = jnp.mean(z, axis=-1, keepdims=True)
        zc = z - mu
        var = jnp.mean(zc * zc, axis=-1, keepdims=True)
        rstd = lax.rsqrt(var + LN_EPS)
        xhat = zc * rstd
        h = xhat * g_ref[...] + b_ref[...]
        outs[0][...] = h
        outs[1][...] = xhat
        outs[2][...] = jnp.broadcast_to(rstd, (tile, LANES))
        outs[3][...] = _bf(h)
        if want_perm:
            outs[4][...] = h

    ins, specs = [a], [_nat(tile, d)]
    for arr, layout in adds:
        x_, sp = _row_spec(arr, layout, tile, d)
        ins.append(x_)
        specs.append(sp)
    if has_gate:
        layout = gate[0]
        for arr, ncb, cb in gate[1:]:
            x_, sp = _row_spec(arr, layout, tile, d, ncb=ncb, cb=cb)
            ins.append(x_)
            specs.append(sp)
    ins += [g.reshape(1, d), b.reshape(1, d)]
    specs += [_whole((1, d)), _whole((1, d))]
    out_shape = [jax.ShapeDtypeStruct((s, d), F32), jax.ShapeDtypeStruct((s, d), F32),
                 jax.ShapeDtypeStruct((s, LANES), F32), jax.ShapeDtypeStruct((s, d), BF16)]
    out_specs = [_nat(tile, d), _nat(tile, d), _nat(tile, LANES), _nat(tile, d)]
    if want_perm:
        seg_tiles = s // SUBLANES // tile
        out_shape.append(jax.ShapeDtypeStruct((s // SUBLANES, SUBLANES * d), F32))
        out_specs.append(_perm(tile, d, seg_tiles))
    res = pl.pallas_call(
        body, name=name, grid=(s // tile,), in_specs=specs, out_specs=out_specs, out_shape=out_shape,
        compiler_params=_params("parallel"),
    )(*ins)
    return res[0], res[1], res[2], res[3], (res[4].reshape(s, d) if want_perm else None)


def _ln_bwd(name, dparts, xhat, rstd, g, gate=None, tile=256):
    s, d = xhat.shape
    n_part = len(dparts)
    coefs = [c for _, _, c in dparts]
    has_gate = gate is not None

    def body(*refs):
        part_refs = refs[:n_part]
        xhat_ref, rstd_ref, g_ref = refs[n_part:n_part + 3]
        pos = n_part + 3
        if has_gate:
            val_ref, pre_ref = refs[pos], refs[pos + 1]
            pos += 2
        outs = list(refs[pos:])
        dz_ref = outs.pop(0)
        dzb_ref = outs.pop(0)
        dgate_ref = outs.pop(0) if has_gate else None
        dg_ref, db_ref = outs
        dh = coefs[0] * part_refs[0][...]
        for c, r in zip(coefs[1:], part_refs[1:]):
            dh = dh + c * r[...]
        xh = xhat_ref[...]
        dxh = dh * g_ref[...]
        m1 = jnp.mean(dxh, axis=-1, keepdims=True)
        m2 = jnp.mean(dxh * xh, axis=-1, keepdims=True)
        dz = rstd_ref[:, 0:1] * (dxh - m1 - xh * m2)
        dz_ref[...] = dz
        dzb_ref[...] = _bf(dz)
        if has_gate:
            sg = _sigmoid(pre_ref[...])
            dval = dz * sg
            dpre = dz * val_ref[...] * sg * (1.0 - sg)
            dgate_ref[...] = jnp.concatenate([_bf(dval), _bf(dpre)], axis=1)

        @pl.when(pl.program_id(0) == 0)
        def _():
            dg_ref[...] = jnp.zeros_like(dg_ref)
            db_ref[...] = jnp.zeros_like(db_ref)

        dg_ref[0:1, :] += jnp.sum(dh * xh, axis=0, keepdims=True)
        db_ref[0:1, :] += jnp.sum(dh, axis=0, keepdims=True)

    ins, specs = [], []
    for arr, layout, _ in dparts:
        x_, sp = _row_spec(arr, layout, tile, d)
        ins.append(x_)
        specs.append(sp)
    ins += [xhat, rstd, g.reshape(1, d)]
    specs += [_nat(tile, d), _nat(tile, LANES), _whole((1, d))]
    gate_layout = None
    if has_gate:
        gate_layout = gate[0]
        for arr, ncb, cb in gate[1:]:
            x_, sp = _row_spec(arr, gate_layout, tile, d, ncb=ncb, cb=cb)
            ins.append(x_)
            specs.append(sp)
    seg_tiles = s // SUBLANES // tile
    out_shape = [jax.ShapeDtypeStruct((s, d), F32), jax.ShapeDtypeStruct((s, d), BF16)]
    out_specs = [_nat(tile, d), _nat(tile, d)]
    if has_gate:
        if gate_layout == 'nat':
            out_shape.append(jax.ShapeDtypeStruct((s, 2 * d), BF16))
            out_specs.append(_nat(tile, 2 * d))
        else:
            out_shape.append(jax.ShapeDtypeStruct((s // SUBLANES, SUBLANES * 2 * d), BF16))
            out_specs.append(_perm(tile, 2 * d, seg_tiles))
    out_shape += [jax.ShapeDtypeStruct((SUBLANES, d), F32)] * 2
    out_specs += [_whole((SUBLANES, d))] * 2
    res = list(pl.pallas_call(
        body, name=name, grid=(s // tile,), in_specs=specs, out_specs=out_specs, out_shape=out_shape,
        compiler_params=_params("arbitrary"),
    )(*ins))
    out = [res.pop(0), res.pop(0)]
    if has_gate:
        out.append(res.pop(0).reshape(s, 2 * d))
    out += [res[0][0], res[1][0]]
    return out


def _loss_partial(h, target, tile=256):
    s, d = h.shape

    def body(h_ref, t_ref, o_ref):
        @pl.when(pl.program_id(0) == 0)
        def _():
            o_ref[...] = jnp.zeros_like(o_ref)

        e = h_ref[...] - t_ref[...]
        sq = e * e
        part = sq[:, 0:LANES]
        for k in range(1, d // LANES):
            part = part + sq[:, k * LANES:(k + 1) * LANES]
        o_ref[0:1, :] += jnp.sum(part, axis=0, keepdims=True) * (0.5 / d)

    return pl.pallas_call(
        body, name="loss_partial", grid=(s // tile,), in_specs=[_nat(tile, d), _nat(tile, d)],
        out_specs=_whole((SUBLANES, LANES)), out_shape=jax.ShapeDtypeStruct((SUBLANES, LANES), F32),
        compiler_params=_params("arbitrary"),
    )(h, target)


def _swiglu_fwd(name, gu, tile=256):
    s, f2 = gu.shape
    f = f2 // 2
    cw = _pick(f, 1408)
    ncb = f // cw

    def body(g_ref, u_ref, o_ref):
        gg = g_ref[...]
        o_ref[...] = _bf(gg * _sigmoid(gg) * u_ref[...])

    return pl.pallas_call(
        body, name=name, grid=(s // tile, ncb),
        in_specs=[pl.BlockSpec((tile, cw), lambda i, j: (i, j)), pl.BlockSpec((tile, cw), lambda i, j: (i, j + ncb))],
        out_specs=pl.BlockSpec((tile, cw), lambda i, j: (i, j)),
        out_shape=jax.ShapeDtypeStruct((s, f), BF16), compiler_params=_params("parallel", "parallel"),
    )(gu, gu)


def _swiglu_bwd(name, gu, dact, tile=128):
    s, f2 = gu.shape
    f = f2 // 2

    def body(g_ref, u_ref, da_ref, o_ref):
        gg = g_ref[...]
        sg = _sigmoid(gg)
        da = da_ref[...].astype(F32)
        silu = gg * sg
        o_ref[:, :f] = _bf(da * u_ref[...] * (sg + silu * (1.0 - sg)))
        o_ref[:, f:] = _bf(da * silu)

    return pl.pallas_call(
        body, name=name, grid=(s // tile,),
        in_specs=[_nat(tile, f, 0), _nat(tile, f, 1), _nat(tile, f)], out_specs=_nat(tile, f2),
        out_shape=jax.ShapeDtypeStruct((s, f2), BF16), compiler_params=_params("parallel"),
    )(gu, gu, dact)


def _rms_fwd(proj, ql, kvl, gq, gkv, tile=256):
    s = proj.shape[0]
    assert ql == kvl

    def body(q_ref, kv_ref, gq_ref, gkv_ref, o_ref):
        def nrm(x, gg):
            return x * lax.rsqrt(jnp.mean(x * x, axis=-1, keepdims=True) + RMS_EPS) * gg

        o_ref[...] = jnp.concatenate([_bf(nrm(q_ref[...], gq_ref[...])), _bf(nrm(kv_ref[...], gkv_ref[...]))], axis=1)

    return pl.pallas_call(
        body, name="mla_rms_fwd", grid=(s // tile,),
        in_specs=[_nat(tile, ql, 0), _nat(tile, kvl, 1), _whole((1, ql)), _whole((1, kvl))],
        out_specs=_nat(tile, ql + kvl), out_shape=jax.ShapeDtypeStruct((s, ql + kvl), BF16),
        compiler_params=_params("parallel"),
    )(proj, proj, gq.reshape(1, ql), gkv.reshape(1, kvl))


def _rope_coeffs(pos, invf):
    ang = pos * invf
    cs, sn = jnp.cos(ang), jnp.sin(ang)
    lane = lax.broadcasted_iota(jnp.int32, ang.shape, 1)
    half = ROPE // 2
    c = jnp.where(lane < ROPE, cs, 0.0)
    sa = jnp.where(lane < half, -sn, 0.0)
    sb = jnp.where((lane >= half) & (lane < ROPE), sn, 0.0)
    return c, sa, sb


def _rope_prep(q_raw, kv, proj, kpe_cb, pos, invf, heads, tile=256):
    s = q_raw.shape[0]
    half = ROPE // 2

    def body(q_ref, kv_ref, kpe_ref, pos_ref, invf_ref, qf_ref, kf_ref, v_ref):
        c, sa, sb = _rope_coeffs(pos_ref[...], invf_ref[...])

        def rope(t):
            return t * c + pltpu.roll(t, LANES - half, 1) * sa + pltpu.roll(t, half, 1) * sb

        kr = _bf(rope(kpe_ref[...]))
        for hh in range(heads):
            o = hh * QK_PAD
            qf_ref[:, o:o + NOPE] = _bf(q_ref[:, o:o + NOPE])
            qf_ref[:, o + NOPE:o + QK_PAD] = _bf(rope(q_ref[:, o + NOPE:o + QK_PAD]))
            kf_ref[:, o:o + NOPE] = _bf(kv_ref[:, o:o + NOPE])
            kf_ref[:, o + NOPE:o + QK_PAD] = kr
            v_ref[:, hh * VDIM:(hh + 1) * VDIM] = _bf(kv_ref[:, o + NOPE:o + QK_PAD])

    w = heads * QK_PAD
    return pl.pallas_call(
        body, name="mla_rope_prep", grid=(s // tile,),
        in_specs=[_nat(tile, w), _nat(tile, w), _nat(tile, LANES, kpe_cb), _nat(tile, 1), _whole((1, LANES))],
        out_specs=[_nat(tile, w), _nat(tile, w), _nat(tile, heads * VDIM)],
        out_shape=[jax.ShapeDtypeStruct((s, w), BF16), jax.ShapeDtypeStruct((s, w), BF16),
                   jax.ShapeDtypeStruct((s, heads * VDIM), BF16)],
        compiler_params=_params("parallel"),
    )(q_raw, kv, proj, pos, invf)


def _rope_unprep(dqf, dkf, dv, pos, invf, heads, tile=256):
    s = dqf.shape[0]
    half = ROPE // 2

    def body(dq_ref, dk_ref, dv_ref, pos_ref, invf_ref, dqr_ref, dkv_ref, dkpe_ref):
        c, sa, sb = _rope_coeffs(pos_ref[...], invf_ref[...])

        def unrope(gt):
            return gt * c + pltpu.roll(gt * sa, half, 1) + pltpu.roll(gt * sb, LANES - half, 1)

        dkpe = jnp.zeros((tile, LANES), F32)
        for hh in range(heads):
            o = hh * QK_PAD
            dqr_ref[:, o:o + NOPE] = _bf(dq_ref[:, o:o + NOPE])
            dqr_ref[:, o + NOPE:o + QK_PAD] = _bf(unrope(dq_ref[:, o + NOPE:o + QK_PAD]))
            dkv_ref[:, o:o + NOPE] = _bf(dk_ref[:, o:o + NOPE])
            dkv_ref[:, o + NOPE:o + QK_PAD] = _bf(dv_ref[:, hh * VDIM:(hh + 1) * VDIM])
            dkpe = dkpe + dk_ref[:, o + NOPE:o + QK_PAD]
        dkpe_ref[...] = unrope(dkpe)

    w = heads * QK_PAD
    return pl.pallas_call(
        body, name="mla_rope_unprep", grid=(s // tile,),
        in_specs=[_nat(tile, w), _nat(tile, w), _nat(tile, heads * VDIM), _nat(tile, 1), _whole((1, LANES))],
        out_specs=[_nat(tile, w), _nat(tile, w), _nat(tile, LANES)],
        out_shape=[jax.ShapeDtypeStruct((s, w), BF16), jax.ShapeDtypeStruct((s, w), BF16),
                   jax.ShapeDtypeStruct((s, LANES), F32)],
        compiler_params=_params("parallel"),
    )(dqf, dkf, dv, pos, invf)


LOG2E = 1.4426950408889634
MLA_SCALE = (NOPE + ROPE) ** -0.5


def _mla_scores_t(k, q, t, masked):
    sc = _dot(k, q, NT) * (MLA_SCALE * LOG2E)
    if masked:
        row = lax.broadcasted_iota(jnp.int32, (t, t), 0)
        col = lax.broadcasted_iota(jnp.int32, (t, t), 1)
        sc = jnp.where(row <= col, sc, NEG)
    return sc


def _mla_fwd(qf, kf, vt, heads, t=512):
    s = qf.shape[0]
    t = min(t, s)
    nq = s // t

    def body(q_ref, k_ref, vt_ref, o_ref, lse_ref, m_ref, l_ref, acc_ref):
        i = pl.program_id(1)
        m_ref[...] = jnp.full_like(m_ref, NEG)
        l_ref[...] = jnp.zeros_like(l_ref)
        acc_ref[...] = jnp.zeros_like(acc_ref)
        q = q_ref[...]

        def block(j, masked):
            r0 = pl.multiple_of(j * t, t)
            sc = _mla_scores_t(k_ref[pl.ds(r0, t), :], q, t, masked)
            m_prev = m_ref[0:1, :]
            m_new = jnp.maximum(m_prev, jnp.max(sc, axis=0, keepdims=True))
            corr = jnp.exp2(m_prev - m_new)
            p = jnp.exp2(sc - m_new)
            l_new = corr * l_ref[0:1, :] + jnp.sum(p, axis=0, keepdims=True)
            acc_ref[...] = corr * acc_ref[...] + _dot(vt_ref[:, pl.ds(r0, t)], _bf(p), NN)
            m_ref[...] = jnp.broadcast_to(m_new, (SUBLANES, t))
            l_ref[...] = jnp.broadcast_to(l_new, (SUBLANES, t))

        def unmasked(j, carry):
            block(j, False)
            return carry

        lax.fori_loop(0, i, unmasked, 0)
        block(i, True)
        o_ref[...] = (acc_ref[...] / l_ref[0:1, :]).T
        lse_ref[...] = m_ref[...] + jnp.log(l_ref[...]) * LOG2E

    return pl.pallas_call(
        body, name="mla_flash_fwd", grid=(heads, nq),
        in_specs=[pl.BlockSpec((t, QK_PAD), lambda h, i: (i, h)), pl.BlockSpec((s, QK_PAD), lambda h, i: (0, h)),
                  pl.BlockSpec((VDIM, s), lambda h, i: (h, 0))],
        out_specs=[pl.BlockSpec((t, VDIM), lambda h, i: (i, h)), pl.BlockSpec((SUBLANES, t), lambda h, i: (h, i))],
        out_shape=[jax.ShapeDtypeStruct((s, heads * VDIM), F32), jax.ShapeDtypeStruct((heads * SUBLANES, s), F32)],
        scratch_shapes=[pltpu.VMEM((SUBLANES, t), F32), pltpu.VMEM((SUBLANES, t), F32), pltpu.VMEM((VDIM, t), F32)],
        compiler_params=_params("parallel", "arbitrary"),
    )(qf, kf, vt)


def _mla_bwd(qf, kf, v, do, lse_t, delta_t, heads, do_cb0, t=512):
    s = qf.shape[0]
    t = min(t, s)
    nq = s // t

    def body(q_ref, k_ref, v_ref, do_ref, lse_ref, dl_ref, dq_ref, dk_ref, dv_ref, acc_ref):
        i = pl.program_id(1)

        @pl.when(i == 0)
        def _():
            dk_ref[...] = jnp.zeros_like(dk_ref)
            dv_ref[...] = jnp.zeros_like(dv_ref)

        acc_ref[...] = jnp.zeros_like(acc_ref)
        q, dob = q_ref[...], do_ref[...]
        lse, dl = lse_ref[0:1, :], dl_ref[0:1, :]

        def block(j, masked):
            r0 = pl.multiple_of(j * t, t)
            k = k_ref[pl.ds(r0, t), :]
            p = jnp.exp2(_mla_scores_t(k, q, t, masked) - lse)
            dp = _dot(v_ref[pl.ds(r0, t), :], dob, NT)
            ds = _bf(p * (dp - dl) * MLA_SCALE)
            acc_ref[...] += _dot(ds, k, TN)
            dk_ref[pl.ds(r0, t), :] += _dot(ds, q, NN)
            dv_ref[pl.ds(r0, t), :] += _dot(_bf(p), dob, NN)

        def unmasked(j, carry):
            block(j, False)
            return carry

        lax.fori_loop(0, i, unmasked, 0)
        block(i, True)
        dq_ref[...] = acc_ref[...]

    qs = lambda w, off=0: pl.BlockSpec((t, w), lambda h, i: (i, h + off))
    ks = lambda w: pl.BlockSpec((s, w), lambda h, i: (0, h))
    st = pl.BlockSpec((SUBLANES, t), lambda h, i: (h, i))
    return pl.pallas_call(
        body, name="mla_flash_bwd", grid=(heads, nq),
        in_specs=[qs(QK_PAD), ks(QK_PAD), ks(VDIM), qs(VDIM, do_cb0), st, st],
        out_specs=[qs(QK_PAD), ks(QK_PAD), ks(VDIM)],
        out_shape=[jax.ShapeDtypeStruct((s, heads * QK_PAD), F32), jax.ShapeDtypeStruct((s, heads * QK_PAD), F32),
                   jax.ShapeDtypeStruct((s, heads * VDIM), F32)],
        scratch_shapes=[pltpu.VMEM((t, QK_PAD), F32)],
        compiler_params=_params("parallel", "arbitrary"),
    )(qf, kf, v, do, lse_t, delta_t)


def _band_mask(tq, first_block):
    row = lax.broadcasted_iota(jnp.int32, (tq, DIL_STEPS + tq), 0)
    col = lax.broadcasted_iota(jnp.int32, (tq, DIL_STEPS + tq), 1)
    dist = row + DIL_STEPS - col
    valid = (dist >= 0) & (dist <= DIL_STEPS) & (jnp.logical_not(first_block) | (col >= DIL_STEPS))
    return dist, valid


def _dil_scores(q, kp, kc, slope, dil, tq, first_block):
    sc = jnp.concatenate([_dot(q, kp, NT), _dot(q, kc, NT)], axis=1) * (DHD ** -0.5)
    dist, valid = _band_mask(tq, first_block)
    return jnp.where(valid, sc - slope * (dil * dist).astype(F32), NEG)


def _dil_specs(proj_w, dh, tq):
    pwb = proj_w // LANES
    r_of = lambda cb: cb // dh
    h_of = lambda cb: cb % dh
    cur = lambda off: pl.BlockSpec((tq, DHD), lambda cb, i: (i, r_of(cb) * pwb + off + h_of(cb)))
    prev = lambda off: pl.BlockSpec(
        (DIL_STEPS, DHD), lambda cb, i: (jnp.maximum(i * (tq // DIL_STEPS) - 1, 0), r_of(cb) * pwb + off + h_of(cb)))
    return cur, prev


def _dil_fwd(name, proj, slopes, dil, dh, q_cb, tq=512):
    s, pw = proj.shape
    l = s // dil
    tq = min(tq, l)
    nb = l // tq
    k_cb, v_cb = q_cb + dh, q_cb + 2 * dh
    cur, prev = _dil_specs(pw, dh, tq)
    pv = proj.reshape(l, dil * pw)

    def body(q_ref, kc_ref, kp_ref, vc_ref, vp_ref, sl_ref, o_ref, lse_ref):
        i = pl.program_id(1)
        sc = _dil_scores(_bf(q_ref[...]), _bf(kp_ref[...]), _bf(kc_ref[...]), sl_ref[0:1, 0:1], dil, tq, i == 0)
        m = jnp.max(sc, axis=-1, keepdims=True)
        e = jnp.exp(sc - m)
        lsum = jnp.sum(e, axis=-1, keepdims=True)
        pn = e / lsum
        o_ref[...] = (_dot(_bf(pn[:, :DIL_STEPS]), _bf(vp_ref[...]), NN)
                      + _dot(_bf(pn[:, DIL_STEPS:]), _bf(vc_ref[...]), NN))
        lse_ref[...] = jnp.broadcast_to(m + jnp.log(lsum), (tq, LANES))

    ospec = pl.BlockSpec((tq, DHD), lambda cb, i: (i, cb))
    o, lse = pl.pallas_call(
        body, name=name, grid=(dil * dh, nb),
        in_specs=[cur(q_cb), cur(k_cb), prev(k_cb), cur(v_cb), prev(v_cb),
                  pl.BlockSpec((SUBLANES, LANES), lambda cb, i: (cb % dh, 0))],
        out_specs=[ospec, ospec],
        out_shape=[jax.ShapeDtypeStruct((l, dil * dh * DHD), F32)] * 2,
        compiler_params=_params("parallel", "parallel"),
    )(pv, pv, pv, pv, pv, slopes)
    return o.reshape(s, dh * DHD), lse.reshape(s, dh * DHD)


def _dil_bwd_dq(name, proj, slopes, do, lse, delta, dil, dh, q_cb, b_cb0, tq=512):
    s, pw = proj.shape
    mixw = do.shape[1]
    l = s // dil
    tq = min(tq, l)
    nb = l // tq
    k_cb, v_cb = q_cb + dh, q_cb + 2 * dh
    cur, prev = _dil_specs(pw, dh, tq)
    pv = proj.reshape(l, dil * pw)
    mb = mixw // LANES
    mspec = pl.BlockSpec((tq, DHD), lambda cb, i: (i, (cb // dh) * mb + b_cb0 + cb % dh))
    ospec = pl.BlockSpec((tq, DHD), lambda cb, i: (i, cb))

    def body(q_ref, kc_ref, kp_ref, vc_ref, vp_ref, sl_ref, do_ref, lse_ref, dl_ref, dq_ref):
        i = pl.program_id(1)
        kp, kc = _bf(kp_ref[...]), _bf(kc_ref[...])
        sc = _dil_scores(_bf(q_ref[...]), kp, kc, sl_ref[0:1, 0:1], dil, tq, i == 0)
        p = jnp.exp(sc - lse_ref[:, 0:1])
        dob = do_ref[...]
        dp = jnp.concatenate([_dot(dob, _bf(vp_ref[...]), NT), _dot(dob, _bf(vc_ref[...]), NT)], axis=1)
        ds = _bf(p * (dp - dl_ref[:, 0:1]) * (DHD ** -0.5))
        dq_ref[...] = _dot(ds[:, :DIL_STEPS], kp, NN) + _dot(ds[:, DIL_STEPS:], kc, NN)

    dq = pl.pallas_call(
        body, name=name, grid=(dil * dh, nb),
        in_specs=[cur(q_cb), cur(k_cb), prev(k_cb), cur(v_cb), prev(v_cb),
                  pl.BlockSpec((SUBLANES, LANES), lambda cb, i: (cb % dh, 0)), mspec, ospec, mspec],
        out_specs=ospec, out_shape=jax.ShapeDtypeStruct((l, dil * dh * DHD), F32),
        compiler_params=_params("parallel", "parallel"),
    )(pv, pv, pv, pv, pv, slopes, do.reshape(l, dil * mixw), lse.reshape(l, dil * dh * DHD), delta.reshape(l, dil * mixw))
    return dq.reshape(s, dh * DHD)


def _dil_bwd_dkv(name, proj, slopes, do, lse, delta, dil, dh, q_cb, b_cb0, tk=512):
    s, pw = proj.shape
    mixw = do.shape[1]
    l = s // dil
    tk = min(tk, l)
    nb = l // tk
    k_cb, v_cb = q_cb + dh, q_cb + 2 * dh
    pwb, mb = pw // LANES, mixw // LANES
    sub = tk // DIL_STEPS
    last128 = l // DIL_STEPS - 1
    pv = proj.reshape(l, dil * pw)

    def cur(width_blocks, off):
        return pl.BlockSpec((tk, DHD), lambda cb, j: (j, (cb // dh) * width_blocks + off + cb % dh))

    def nxt(width_blocks, off):
        return pl.BlockSpec((DIL_STEPS, DHD), lambda cb, j: (jnp.minimum((j + 1) * sub, last128),
                                                               (cb // dh) * width_blocks + off + cb % dh))

    ocur = pl.BlockSpec((tk, DHD), lambda cb, j: (j, cb))
    onxt = pl.BlockSpec((DIL_STEPS, DHD), lambda cb, j: (jnp.minimum((j + 1) * sub, last128), cb))

    def body(k_ref, v_ref, qc_ref, qn_ref, sl_ref, doc_ref, don_ref, lsec_ref, lsen_ref, dlc_ref, dln_ref,
             dk_ref, dv_ref):
        j = pl.program_id(1)
        slope = sl_ref[0:1, 0:1]
        scale = DHD ** -0.5
        k, v = _bf(k_ref[...]), _bf(v_ref[...])
        qc = _bf(qc_ref[...])
        row = lax.broadcasted_iota(jnp.int32, (tk, tk), 0)
        col = lax.broadcasted_iota(jnp.int32, (tk, tk), 1)
        dist = row - col
        valid = (dist >= 0) & (dist <= DIL_STEPS)
        sc = jnp.where(valid, _dot(qc, k, NT) * scale - slope * (dil * dist).astype(F32), NEG)
        p = jnp.exp(sc - lsec_ref[:, 0:1])
        doc = doc_ref[...]
        ds = _bf(p * (_dot(doc, v, NT) - dlc_ref[:, 0:1]) * scale)
        dv_ref[...] = _dot(_bf(p), doc, TN)
        dk_ref[...] = _dot(ds, qc, TN)
        kl, vl = k[tk - DIL_STEPS:, :], v[tk - DIL_STEPS:, :]
        qn = _bf(qn_ref[...])
        row = lax.broadcasted_iota(jnp.int32, (DIL_STEPS, DIL_STEPS), 0)
        col = lax.broadcasted_iota(jnp.int32, (DIL_STEPS, DIL_STEPS), 1)
        dist = DIL_STEPS + row - col
        valid = (dist <= DIL_STEPS) & (j < nb - 1)
        sc = jnp.where(valid, _dot(qn, kl, NT) * scale - slope * (dil * dist).astype(F32), NEG)
        p = jnp.exp(sc - lsen_ref[:, 0:1])
        don = don_ref[...]
        ds = _bf(p * (_dot(don, vl, NT) - dln_ref[:, 0:1]) * scale)
        dv_ref[tk - DIL_STEPS:, :] += _dot(_bf(p), don, TN)
        dk_ref[tk - DIL_STEPS:, :] += _dot(ds, qn, TN)

    dov = do.reshape(l, dil * mixw)
    dlv = delta.reshape(l, dil * mixw)
    lsv = lse.reshape(l, dil * dh * DHD)
    dk, dv = pl.pallas_call(
        body, name=name, grid=(dil * dh, nb),
        in_specs=[cur(pwb, k_cb), cur(pwb, v_cb), cur(pwb, q_cb), nxt(pwb, q_cb),
                  pl.BlockSpec((SUBLANES, LANES), lambda cb, j: (cb % dh, 0)),
                  cur(mb, b_cb0), nxt(mb, b_cb0), ocur, onxt, cur(mb, b_cb0), nxt(mb, b_cb0)],
        out_specs=[ocur, ocur], out_shape=[jax.ShapeDtypeStruct((l, dil * dh * DHD), F32)] * 2,
        compiler_params=_params("parallel", "parallel"),
    )(pv, pv, pv, pv, slopes, dov, dov, lsv, lsv, dlv, dlv)
    return dk.reshape(s, dh * DHD), dv.reshape(s, dh * DHD)


def _dil_merge(out_a, outs, lses, tile=256):
    s, wa = out_a.shape
    wb = outs[0].shape[1]
    nbr = len(outs)

    def body(*refs):
        a_ref = refs[0]
        o_refs, l_refs = refs[1:1 + nbr], refs[1 + nbr:1 + 2 * nbr]
        att_ref, ob_ref, lse_ref = refs[1 + 2 * nbr:]
        ls = [r[...] for r in l_refs]
        m = ls[0]
        for x_ in ls[1:]:
            m = jnp.maximum(m, x_)
        es = [jnp.exp(x_ - m) for x_ in ls]
        tot = es[0]
        for e in es[1:]:
            tot = tot + e
        ob = (es[0] / tot) * o_refs[0][...]
        for e, r in zip(es[1:], o_refs[1:]):
            ob = ob + (e / tot) * r[...]
        ob_ref[...] = ob
        lse_ref[...] = m + jnp.log(tot)
        att_ref[...] = jnp.concatenate([_bf(a_ref[...]), _bf(ob)], axis=1)

    return pl.pallas_call(
        body, name="dil_merge", grid=(s // tile,),
        in_specs=[_nat(tile, wa)] + [_nat(tile, wb)] * (2 * nbr),
        out_specs=[_nat(tile, wa + wb), _nat(tile, wb), _nat(tile, wb)],
        out_shape=[jax.ShapeDtypeStruct((s, wa + wb), BF16), jax.ShapeDtypeStruct((s, wb), F32),
                   jax.ShapeDtypeStruct((s, wb), F32)],
        compiler_params=_params("parallel"),
    )(out_a, *outs, *lses)


DIL_BLOCK = 2048


def _dil_unit_rows(u, dil, block):
    sub = u // dil
    return u % dil + (dil * DIL_STEPS) * sub, sub == 0


def _dil_unit_scores(q, kp, kc, slope, dil, no_prev):
    sc = jnp.concatenate([_dot(q, kp, NT), _dot(q, kc, NT)], axis=1) * (DHD ** -0.5)
    row = lax.broadcasted_iota(jnp.int32, (DIL_STEPS, 2 * DIL_STEPS), 0)
    col = lax.broadcasted_iota(jnp.int32, (DIL_STEPS, 2 * DIL_STEPS), 1)
    dist = row + DIL_STEPS - col
    valid = (dist >= 0) & (dist <= DIL_STEPS) & (jnp.logical_not(no_prev) | (col >= DIL_STEPS))
    return jnp.where(valid, sc - slope * (dil * dist).astype(F32), NEG)


def _dil_in_specs(pw, dh, q_cb, block, rev_nb=None):
    blk = (lambda i: i) if rev_nb is None else (lambda i: rev_nb - 1 - i)
    own = lambda off: pl.BlockSpec((block, DHD), lambda h, i: (blk(i), off + h))
    prev = lambda off: pl.BlockSpec((block, DHD), lambda h, i: (jnp.maximum(blk(i) - 1, 0), off + h))
    return [own(q_cb), own(q_cb + dh), prev(q_cb + dh), own(q_cb + 2 * dh), prev(q_cb + 2 * dh)]


def _dil_fused_fwd(proj, slopes, dh, q_cb):
    s, pw = proj.shape
    block = min(DIL_BLOCK, s)
    nb = s // block
    n_units = block // DIL_STEPS
    nbr = len(DIL_BRANCHES)
    assert block >= DIL_STEPS * max(d for _, d in DIL_BRANCHES)

    def body(q_ref, kc_ref, kp_ref, vc_ref, vp_ref, sl_ref, o_ref, lse_ref, kk, vv, *per_branch):
        og, mg, lg = per_branch[:nbr], per_branch[nbr:2 * nbr], per_branch[2 * nbr:]
        i = pl.program_id(1)
        kk[0:block, :] = kp_ref[...]
        kk[block:, :] = kc_ref[...]
        vv[0:block, :] = vp_ref[...]
        vv[block:, :] = vc_ref[...]
        slope = sl_ref[0:1, 0:1]
        for g, (_, dil) in enumerate(DIL_BRANCHES):
            def unit(u, carry, g=g, dil=dil):
                q0, first = _dil_unit_rows(u, dil, block)
                rows = lambda base: pl.ds(base, DIL_STEPS, stride=dil) if dil > 1 else pl.ds(base, DIL_STEPS)
                q = _bf(q_ref[rows(q0), :])
                kc, kp = _bf(kk[rows(block + q0), :]), _bf(kk[rows(block + q0 - dil * DIL_STEPS), :])
                vc, vp = _bf(vv[rows(block + q0), :]), _bf(vv[rows(block + q0 - dil * DIL_STEPS), :])
                sc = _dil_unit_scores(q, kp, kc, slope, dil, first & (i == 0))
                m = jnp.max(sc, axis=-1, keepdims=True)
                e = jnp.exp(sc - m)
                og[g][rows(q0), :] = _dot(_bf(e[:, :DIL_STEPS]), vp, NN) + _dot(_bf(e[:, DIL_STEPS:]), vc, NN)
                mg[g][rows(q0), :] = jnp.broadcast_to(m, (DIL_STEPS, LANES))
                lg[g][rows(q0), :] = jnp.broadcast_to(jnp.sum(e, axis=-1, keepdims=True), (DIL_STEPS, LANES))
                return carry

            lax.fori_loop(0, n_units, unit, 0, unroll=4)
        m_all = mg[0][...]
        for g in range(1, nbr):
            m_all = jnp.maximum(m_all, mg[g][...])
        tot = jnp.zeros((block, LANES), F32)
        acc = jnp.zeros((block, DHD), F32)
        for g in range(nbr):
            w = jnp.exp(mg[g][...] - m_all)
            tot = tot + w * lg[g][...]
            acc = acc + w * og[g][...]
        o_ref[...] = acc / tot
        lse_ref[...] = m_all + jnp.log(tot)

    ospec = pl.BlockSpec((block, DHD), lambda h, i: (i, h))
    return pl.pallas_call(
        body, name="dil_fused_fwd", grid=(dh, nb),
        in_specs=_dil_in_specs(pw, dh, q_cb, block) + [pl.BlockSpec((SUBLANES, LANES), lambda h, i: (h, 0))],
        out_specs=[ospec, ospec], out_shape=[jax.ShapeDtypeStruct((s, dh * DHD), F32)] * 2,
        scratch_shapes=[pltpu.VMEM((2 * block, DHD), F32), pltpu.VMEM((2 * block, DHD), F32)]
        + [pltpu.VMEM((block, DHD), F32)] * (3 * nbr),
        compiler_params=_params("parallel", "arbitrary"),
    )(proj, proj, proj, proj, proj, slopes)


def _dil_fused_bwd(proj, slopes, datt, lse, delta, dh, q_cb, b_cb0):
    s, pw = proj.shape
    block = min(DIL_BLOCK, s)
    nb = s // block
    n_units = block // DIL_STEPS
    scale = DHD ** -0.5

    def body(q_ref, kc_ref, kp_ref, vc_ref, vp_ref, sl_ref, do_ref, lse_ref, dl_ref, dq_ref, dk_ref, dv_ref,
             kk, vv, dkk, dvv, carry_k, carry_v):
        ii = pl.program_id(1)
        i = nb - 1 - ii

        @pl.when(ii == 0)
        def _():
            carry_k[...] = jnp.zeros_like(carry_k)
            carry_v[...] = jnp.zeros_like(carry_v)

        kk[0:block, :] = kp_ref[...]
        kk[block:, :] = kc_ref[...]
        vv[0:block, :] = vp_ref[...]
        vv[block:, :] = vc_ref[...]
        dkk[...] = jnp.zeros_like(dkk)
        dvv[...] = jnp.zeros_like(dvv)
        dq_ref[...] = jnp.zeros_like(dq_ref)
        slope = sl_ref[0:1, 0:1]
        for _, dil in DIL_BRANCHES:
            def unit(u, carry, dil=dil):
                q0, first = _dil_unit_rows(u, dil, block)
                rows = lambda base: pl.ds(base, DIL_STEPS, stride=dil) if dil > 1 else pl.ds(base, DIL_STEPS)
                cur, prev = rows(block + q0), rows(block + q0 - dil * DIL_STEPS)
                q = _bf(q_ref[rows(q0), :])
                kc, kp, vc, vp = _bf(kk[cur, :]), _bf(kk[prev, :]), _bf(vv[cur, :]), _bf(vv[prev, :])
                dob = _bf(do_ref[rows(q0), :])
                sc = _dil_unit_scores(q, kp, kc, slope, dil, first & (i == 0))
                p = jnp.exp(sc - lse_ref[rows(q0), 0:1])
                dp = jnp.concatenate([_dot(dob, vp, NT), _dot(dob, vc, NT)], axis=1)
                ds = _bf(p * (dp - dl_ref[rows(q0), 0:1]) * scale)
                pb = _bf(p)
                dq_ref[rows(q0), :] += _dot(ds[:, :DIL_STEPS], kp, NN) + _dot(ds[:, DIL_STEPS:], kc, NN)
                dkk[prev, :] += _dot(ds[:, :DIL_STEPS], q, TN)
                dkk[cur, :] += _dot(ds[:, DIL_STEPS:], q, TN)
                dvv[prev, :] += _dot(pb[:, :DIL_STEPS], dob, TN)
                dvv[cur, :] += _dot(pb[:, DIL_STEPS:], dob, TN)
                return carry

            lax.fori_loop(0, n_units, unit, 0, unroll=4)
        dk_ref[...] = dkk[block:, :] + carry_k[...]
        dv_ref[...] = dvv[block:, :] + carry_v[...]
        carry_k[...] = dkk[0:block, :]
        carry_v[...] = dvv[0:block, :]

    rev = lambda i: nb - 1 - i
    mspec = pl.BlockSpec((block, DHD), lambda h, i: (rev(i), b_cb0 + h))
    ospec = pl.BlockSpec((block, DHD), lambda h, i: (rev(i), h))
    big = lambda: pltpu.VMEM((2 * block, DHD), F32)
    return pl.pallas_call(
        body, name="dil_fused_bwd", grid=(dh, nb),
        in_specs=_dil_in_specs(pw, dh, q_cb, block, rev_nb=nb)
        + [pl.BlockSpec((SUBLANES, LANES), lambda h, i: (h, 0)), mspec, ospec, mspec],
        out_specs=[ospec, ospec, ospec], out_shape=[jax.ShapeDtypeStruct((s, dh * DHD), F32)] * 3,
        scratch_shapes=[big(), big(), big(), big(), pltpu.VMEM((block, DHD), F32), pltpu.VMEM((block, DHD), F32)],
        compiler_params=_params("parallel", "arbitrary"),
    )(proj, proj, proj, proj, proj, slopes, datt, lse, delta)


def _concat_bf16(name, a, b, tile=256):
    s, wa = a.shape
    wb = b.shape[1]

    def body(a_ref, b_ref, o_ref):
        o_ref[...] = jnp.concatenate([_bf(a_ref[...]), _bf(b_ref[...])], axis=1)

    return pl.pallas_call(
        body, name=name, grid=(s // tile,), in_specs=[_nat(tile, wa), _nat(tile, wb)], out_specs=_nat(tile, wa + wb),
        out_shape=jax.ShapeDtypeStruct((s, wa + wb), BF16), compiler_params=_params("parallel"),
    )(a, b)


def _attn_bwd_prep(datt, out_a, out_b, tile=256):
    s, mixw = datt.shape
    wa = out_a.shape[1]
    heads_a = wa // LANES

    def body(d_ref, a_ref, b_ref, do_ref, dl_ref, dlt_ref):
        d = d_ref[...]
        do_ref[...] = _bf(d)
        prod = d * jnp.concatenate([a_ref[...], b_ref[...]], axis=1)
        for hh in range(mixw // LANES):
            sl = slice(hh * LANES, (hh + 1) * LANES)
            dl = jnp.broadcast_to(jnp.sum(prod[:, sl], axis=-1, keepdims=True), (tile, LANES))
            dl_ref[:, sl] = dl
            if hh < heads_a:
                dlt_ref[hh * SUBLANES:(hh + 1) * SUBLANES, :] = dl.T[0:SUBLANES, :]

    return pl.pallas_call(
        body, name="attn_bwd_prep", grid=(s // tile,),
        in_specs=[_nat(tile, mixw), _nat(tile, wa), _nat(tile, mixw - wa)],
        out_specs=[_nat(tile, mixw), _nat(tile, mixw), pl.BlockSpec((heads_a * SUBLANES, tile), lambda i: (0, i))],
        out_shape=[jax.ShapeDtypeStruct((s, mixw), BF16), jax.ShapeDtypeStruct((s, mixw), F32),
                   jax.ShapeDtypeStruct((heads_a * SUBLANES, s), F32)],
        compiler_params=_params("parallel"),
    )(datt, out_a, out_b)


def _dproj_assemble(proj, dnq, dnkv, dkpe, dqs, dks, dvs, gq, gkv, ql, tile=256):
    s, pw = proj.shape
    dw = dqs[0].shape[1]
    nbr = len(dqs)

    def body(*refs):
        ql_ref, kvl_ref, dnq_ref, dnkv_ref, dkpe_ref = refs[:5]
        br = refs[5:5 + 3 * nbr]
        gq_ref, gkv_ref = refs[5 + 3 * nbr:7 + 3 * nbr]
        dp_ref, dgq_ref, dgkv_ref = refs[7 + 3 * nbr:]

        @pl.when(pl.program_id(0) == 0)
        def _():
            dgq_ref[...] = jnp.zeros_like(dgq_ref)
            dgkv_ref[...] = jnp.zeros_like(dgkv_ref)

        def rms_bwd(x, dy, gg, dg_ref):
            r = lax.rsqrt(jnp.mean(x * x, axis=-1, keepdims=True) + RMS_EPS)
            xh = x * r
            dxh = dy * gg
            dg_ref[0:1, :] += jnp.sum(dy * xh, axis=0, keepdims=True)
            return r * (dxh - xh * jnp.mean(dxh * xh, axis=-1, keepdims=True))

        pieces = [_bf(rms_bwd(ql_ref[...], dnq_ref[...], gq_ref[...], dgq_ref)),
                  _bf(rms_bwd(kvl_ref[...], dnkv_ref[...], gkv_ref[...], dgkv_ref)),
                  _bf(dkpe_ref[...])]
        for k in range(3):
            acc = br[k * nbr][...]
            for r in br[k * nbr + 1:(k + 1) * nbr]:
                acc = acc + r[...]
            pieces.append(_bf(acc))
        dp_ref[...] = jnp.concatenate(pieces, axis=1)

    res = pl.pallas_call(
        body, name="dproj_assemble", grid=(s // tile,),
        in_specs=[_nat(tile, ql, 0), _nat(tile, ql, 1), _nat(tile, ql), _nat(tile, ql), _nat(tile, LANES)]
        + [_nat(tile, dw)] * (3 * nbr) + [_whole((1, ql)), _whole((1, ql))],
        out_specs=[_nat(tile, pw), _whole((SUBLANES, ql)), _whole((SUBLANES, ql))],
        out_shape=[jax.ShapeDtypeStruct((s, pw), BF16), jax.ShapeDtypeStruct((SUBLANES, ql), F32),
                   jax.ShapeDtypeStruct((SUBLANES, ql), F32)],
        compiler_params=_params("arbitrary"),
    )(proj, proj, dnq, dnkv, dkpe, *dqs, *dks, *dvs, gq.reshape(1, ql), gkv.reshape(1, ql))
    return res[0], res[1][0], res[2][0]


def _axpy(name, alpha, a, b, tile=256):
    s, d = a.shape

    def body(a_ref, b_ref, o_ref):
        o_ref[...] = alpha * a_ref[...] + b_ref[...]

    return pl.pallas_call(
        body, name=name, grid=(s // tile,), in_specs=[_nat(tile, d), _nat(tile, d)], out_specs=_nat(tile, d),
        out_shape=jax.ShapeDtypeStruct((s, d), F32), compiler_params=_params("parallel"),
    )(a, b)


def _cmul(ar, ai, br, bi):
    return ar * br - ai * bi, ar * bi + ai * br


def _s5_discretise(a_re, a_im, log_dt, b_re, b_im, n_sq):
    shape = a_re.shape

    def body(ar_ref, ai_ref, ldt_ref, br_ref, bi_ref, abr_ref, abi_ref, apr_ref, api_ref, bbr_ref, bbi_ref):
        ar, ai = ar_ref[...], ai_ref[...]
        dt = jnp.exp(ldt_ref[...])
        e = jnp.exp(ar * dt)
        abr, abi = e * jnp.cos(ai * dt), e * jnp.sin(ai * dt)
        den = ar * ar + ai * ai
        qr = ((abr - 1.0) * ar + abi * ai) / den
        qi = (abi * ar - (abr - 1.0) * ai) / den
        bbr, bbi = _cmul(qr, qi, br_ref[...], bi_ref[...])
        abr_ref[...], abi_ref[...] = abr, abi
        bbr_ref[...], bbi_ref[...] = bbr, bbi
        pr, pi = abr, abi
        for _ in range(n_sq):
            pr, pi = _cmul(pr, pi, pr, pi)
        apr_ref[...], api_ref[...] = pr, pi

    return pl.pallas_call(
        body, name="s5_discretise", out_shape=[jax.ShapeDtypeStruct(shape, F32)] * 6,
        compiler_params=pltpu.CompilerParams(vmem_limit_bytes=VMEM_LIMIT),
    )(a_re, a_im, log_dt, b_re, b_im)


def _s5_discretise_bwd(a16, b16, ag, gab, gbb):
    rows, p = a16[0].shape
    g = rows // S5_GROUP

    def disc(ar, ai, ldt):
        dt = jnp.exp(ldt)
        e = jnp.exp(ar * dt)
        abr, abi = e * jnp.cos(ai * dt), e * jnp.sin(ai * dt)
        den = ar * ar + ai * ai
        inv_r, inv_i = ar / den, -ai / den
        qr, qi = _cmul(abr - 1.0, abi, inv_r, inv_i)
        return dt, abr, abi, inv_r, inv_i, qr, qi

    def body(ar16_ref, ai16_ref, ldt16_ref, br_ref, bi_ref, ar_ref, ai_ref, ldt_ref, gar_ref, gai_ref, gbr_ref, gbi_ref,
             dar_ref, dai_ref, dldt_ref, dbr_ref, dbi_ref):
        _, _, _, _, _, qr16, qi16 = disc(ar16_ref[...], ai16_ref[...], ldt16_ref[...])
        gbr, gbi = gbr_ref[...], gbi_ref[...]
        dbr_ref[...], dbi_ref[...] = _cmul(qr16, -qi16, gbr, gbi)
        cr, ci = _cmul(br_ref[...], -bi_ref[...], gbr, gbi)
        gqr = jnp.sum(cr.reshape(g, S5_GROUP, p), axis=1)
        gqi = jnp.sum(ci.reshape(g, S5_GROUP, p), axis=1)
        ar, ai = ar_ref[...], ai_ref[...]
        dt, abr, abi, inv_r, inv_i, qr, qi = disc(ar, ai, ldt_ref[...])
        t_r, t_i = _cmul(inv_r, -inv_i, gqr, gqi)
        gab_r = gar_ref[...] + t_r
        gab_i = gai_ref[...] + t_i
        qa_r, qa_i = _cmul(qr, qi, inv_r, inv_i)
        a1_r, a1_i = _cmul(qa_r, -qa_i, gqr, gqi)
        gl_r, gl_i = _cmul(abr, -abi, gab_r, gab_i)
        dar_ref[...] = dt * gl_r - a1_r
        dai_ref[...] = dt * gl_i - a1_i
        gdt = jnp.sum(ar * gl_r + ai * gl_i, axis=-1, keepdims=True)
        dldt_ref[...] = gdt * dt[:, 0:1]

    return pl.pallas_call(
        body, name="s5_discretise_bwd",
        out_shape=[jax.ShapeDtypeStruct((g, p), F32), jax.ShapeDtypeStruct((g, p), F32),
                   jax.ShapeDtypeStruct((g, 1), F32), jax.ShapeDtypeStruct((rows, p), F32),
                   jax.ShapeDtypeStruct((rows, p), F32)],
        compiler_params=pltpu.CompilerParams(vmem_limit_bytes=VMEM_LIMIT),
    )(*a16, *b16, *ag, *gab, *gbb)


def _slab_tile(re, im, nsl):
    row = jnp.concatenate([re.reshape(nsl, SLAB_COLS), im.reshape(nsl, SLAB_COLS)], axis=-1)
    return jnp.repeat(row, SUBLANES, axis=0)


def _slab_in_matrix(b_re, b_im, nsl):
    eye = jnp.eye(SLAB_GROUPS, dtype=F32)

    def blk(b):
        b = b.reshape(nsl, SLAB_GROUPS, S5_GROUP, S5_STATE)
        return jnp.einsum('sgcp,gh->sgchp', b, eye).reshape(nsl, LANES, SLAB_COLS)

    return jnp.concatenate([blk(b_re), blk(b_im)], axis=-1)


def _slab_in_extract(m, nsl):
    eye = jnp.eye(SLAB_GROUPS, dtype=F32)

    def ext(x_):
        x_ = x_.reshape(nsl, SLAB_GROUPS, S5_GROUP, SLAB_GROUPS, S5_STATE)
        return jnp.einsum('sgchp,gh->sgcp', x_, eye).reshape(nsl * LANES, S5_STATE)

    return ext(m[..., :SLAB_COLS]), ext(m[..., SLAB_COLS:])


def _slab_out_matrix(c_re, c_im, nsl):
    eye = jnp.eye(SLAB_GROUPS, dtype=F32)

    def blk(c):
        c = c.reshape(nsl, SLAB_GROUPS, S5_GROUP, S5_STATE)
        return jnp.einsum('sgcp,gh->sgphc', c, eye).reshape(nsl, SLAB_COLS, LANES)

    return jnp.concatenate([blk(c_re), -blk(c_im)], axis=1)


def _slab_out_extract(m, nsl):
    eye = jnp.eye(SLAB_GROUPS, dtype=F32)

    def ext(x_):
        x_ = x_.reshape(nsl, SLAB_GROUPS, S5_STATE, SLAB_GROUPS, S5_GROUP)
        return jnp.einsum('sgphc,gh->sgcp', x_, eye).reshape(nsl * SLAB_GROUPS, S5_GROUP, S5_STATE)

    return ext(m[:, :SLAB_COLS]), -ext(m[:, SLAB_COLS:])


def _gelu(y):
    t = jnp.tanh(0.7978845608028654 * (y + 0.044715 * y * y * y))
    return 0.5 * y * (1.0 + t)


def _gelu_grad(y):
    t = jnp.tanh(0.7978845608028654 * (y + 0.044715 * y * y * y))
    return 0.5 * (1.0 + t) + 0.5 * y * (1.0 - t * t) * 0.7978845608028654 * (1.0 + 3.0 * 0.044715 * y * y)


def _scan_rows(ref, n_steps, ar, ai, state, reverse, conj, keep=True):
    sgn = -1.0 if conj else 1.0

    def step(k, carry):
        xr, xi = carry
        t = (n_steps - 1 - k) if reverse else k
        r0 = pl.multiple_of(t * SUBLANES, SUBLANES)
        nr = ar * xr - sgn * ai * xi + ref[pl.ds(r0, SUBLANES), :SLAB_COLS]
        ni = ar * xi + sgn * ai * xr + ref[pl.ds(r0, SUBLANES), SLAB_COLS:]
        if keep:
            ref[pl.ds(r0, SUBLANES), :SLAB_COLS] = nr
            ref[pl.ds(r0, SUBLANES), SLAB_COLS:] = ni
        return nr, ni

    return lax.fori_loop(0, n_steps, step, state, unroll=4)


def _s5_pass1(hp, bblk, ab_tile, rc=1024):
    s, d = hp.shape
    nsl = d // LANES
    rc = min(rc, s)
    nch = s // rc
    w = 2 * SLAB_COLS

    def body(u_ref, b_ref, ab_ref, end_ref, st_ref, x_ref):
        j = pl.program_id(1)

        @pl.when(j == 0)
        def _():
            st_ref[...] = jnp.zeros_like(st_ref)

        x_ref[...] = _dot(_bf(u_ref[...]), b_ref[0], NN)
        xr, xi = _scan_rows(x_ref, rc // SUBLANES, ab_ref[:, :SLAB_COLS], ab_ref[:, SLAB_COLS:],
                            (st_ref[:, :SLAB_COLS], st_ref[:, SLAB_COLS:]), False, False, keep=False)
        st_ref[:, :SLAB_COLS] = xr
        st_ref[:, SLAB_COLS:] = xi

        @pl.when(j == nch - 1)
        def _():
            end_ref[...] = st_ref[...]

    return pl.pallas_call(
        body, name="s5_scan_local", grid=(nsl, nch),
        in_specs=[pl.BlockSpec((rc, LANES), lambda sl, j: (j, sl)), pl.BlockSpec((1, LANES, w), lambda sl, j: (sl, 0, 0)),
                  pl.BlockSpec((SUBLANES, w), lambda sl, j: (sl, 0))],
        out_specs=pl.BlockSpec((SUBLANES, w), lambda sl, j: (sl, 0)),
        out_shape=jax.ShapeDtypeStruct((nsl * SUBLANES, w), F32),
        scratch_shapes=[pltpu.VMEM((SUBLANES, w), F32), pltpu.VMEM((rc, w), F32)],
        compiler_params=_params("parallel", "arbitrary"),
    )(hp, bblk, ab_tile)


def _s5_carry(name, ends, ap_tile, reverse):
    rows, w = ends.shape
    nsl = rows // SUBLANES
    sgn = -1.0 if reverse else 1.0

    def body(e_ref, ap_ref, c_ref):
        pr, pi = ap_ref[0:1, :SLAB_COLS], sgn * ap_ref[0:1, SLAB_COLS:]
        tr = jnp.zeros((1, SLAB_COLS), F32)
        ti = jnp.zeros((1, SLAB_COLS), F32)
        order = range(SUBLANES - 1, -1, -1) if reverse else range(SUBLANES)
        for seg in order:
            c_ref[seg:seg + 1, :SLAB_COLS] = tr
            c_ref[seg:seg + 1, SLAB_COLS:] = ti
            mr, mi = _cmul(pr, pi, tr, ti)
            tr = e_ref[seg:seg + 1, :SLAB_COLS] + mr
            ti = e_ref[seg:seg + 1, SLAB_COLS:] + mi

    spec = pl.BlockSpec((SUBLANES, w), lambda sl: (sl, 0))
    return pl.pallas_call(
        body, name=name, grid=(nsl,), in_specs=[spec, spec], out_specs=spec,
        out_shape=jax.ShapeDtypeStruct((rows, w), F32), compiler_params=_params("parallel"),
    )(ends, ap_tile)


def _s5_pass2(hp, bblk, cin, ab_tile, cblk, dvec, rc=1024):
    s, d = hp.shape
    nsl = d // LANES
    rc = min(rc, s)
    nch = s // rc
    w = 2 * SLAB_COLS

    def body(h_ref, b_ref, cin_ref, ab_ref, c_ref, d_ref, x_ref, y_ref, z_ref, st_ref):
        j = pl.program_id(1)

        @pl.when(j == 0)
        def _():
            st_ref[...] = cin_ref[...]

        hv = h_ref[...]
        x_ref[...] = _dot(_bf(hv), b_ref[0], NN)
        xr, xi = _scan_rows(x_ref, rc // SUBLANES, ab_ref[:, :SLAB_COLS], ab_ref[:, SLAB_COLS:],
                            (st_ref[:, :SLAB_COLS], st_ref[:, SLAB_COLS:]), False, False)
        st_ref[:, :SLAB_COLS] = xr
        st_ref[:, SLAB_COLS:] = xi
        y = _dot(_bf(x_ref[...]), c_ref[0], NN) + d_ref[...] * hv
        y_ref[...] = y
        z_ref[...] = _bf(_gelu(y))

    tile = lambda wd: pl.BlockSpec((rc, wd), lambda sl, j: (j, sl))
    small = pl.BlockSpec((SUBLANES, w), lambda sl, j: (sl, 0))
    return pl.pallas_call(
        body, name="s5_scan_carry_out", grid=(nsl, nch),
        in_specs=[tile(LANES), pl.BlockSpec((1, LANES, w), lambda sl, j: (sl, 0, 0)), small, small,
                  pl.BlockSpec((1, w, LANES), lambda sl, j: (sl, 0, 0)), pl.BlockSpec((1, LANES), lambda sl, j: (0, sl))],
        out_specs=[tile(w), tile(LANES), tile(LANES)],
        out_shape=[jax.ShapeDtypeStruct((s, nsl * w), F32), jax.ShapeDtypeStruct((s, d), F32),
                   jax.ShapeDtypeStruct((s, d), BF16)],
        scratch_shapes=[pltpu.VMEM((SUBLANES, w), F32)],
        compiler_params=_params("parallel", "arbitrary"),
    )(hp, bblk, cin, ab_tile, cblk, dvec)


def _s5_bwd_pass1(dzg, ypre, cblk, ab_tile, hp, rc=1024):
    s, d = hp.shape
    nsl = d // LANES
    rc = min(rc, s)
    nch = s // rc
    w = 2 * SLAB_COLS

    def body(dz_ref, y_ref, c_ref, ab_ref, h_ref, st_out_ref, dy_ref, dd_ref, st_ref, lam_ref):
        j = pl.program_id(1)

        @pl.when(j == 0)
        def _():
            st_ref[...] = jnp.zeros_like(st_ref)
            dd_ref[...] = jnp.zeros_like(dd_ref)

        dy = dz_ref[...] * _gelu_grad(y_ref[...])
        dy_ref[...] = dy
        dd_ref[0:1, :] += jnp.sum(dy * h_ref[...], axis=0, keepdims=True)
        lam_ref[...] = _dot(_bf(dy), c_ref[0], NT)
        lr, li = _scan_rows(lam_ref, rc // SUBLANES, ab_ref[:, :SLAB_COLS], ab_ref[:, SLAB_COLS:],
                            (st_ref[:, :SLAB_COLS], st_ref[:, SLAB_COLS:]), True, True, keep=False)
        st_ref[:, :SLAB_COLS] = lr
        st_ref[:, SLAB_COLS:] = li

        @pl.when(j == nch - 1)
        def _():
            st_out_ref[...] = st_ref[...]

    tile = lambda wd: pl.BlockSpec((rc, wd), lambda sl, j: (nch - 1 - j, sl))
    small = pl.BlockSpec((SUBLANES, w), lambda sl, j: (sl, 0))
    return pl.pallas_call(
        body, name="s5_adjoint_local", grid=(nsl, nch),
        in_specs=[tile(LANES), tile(LANES), pl.BlockSpec((1, w, LANES), lambda sl, j: (sl, 0, 0)), small, tile(LANES)],
        out_specs=[small, tile(LANES), pl.BlockSpec((SUBLANES, LANES), lambda sl, j: (0, sl))],
        out_shape=[jax.ShapeDtypeStruct((nsl * SUBLANES, w), F32),
                   jax.ShapeDtypeStruct((s, d), F32), jax.ShapeDtypeStruct((SUBLANES, d), F32)],
        scratch_shapes=[pltpu.VMEM((SUBLANES, w), F32), pltpu.VMEM((rc, w), F32)],
        compiler_params=_params("parallel", "arbitrary"),
    )(dzg, ypre, cblk, ab_tile, hp)


def _s5_bwd_pass2(dy, cblk, cinl, ab_tile, xtrue, cinx, hp, bblk, dvec, rc=1024):
    s, d = hp.shape
    nsl = d // LANES
    rc = min(rc, s)
    nch = s // rc
    w = 2 * SLAB_COLS
    n_steps = rc // SUBLANES

    def body(dy_ref, c_ref, cl_ref, ab_ref, x_ref, xp_ref, cx_ref, h_ref, b_ref, d_ref,
             du_ref, db_ref, dc_ref, da_ref, st_ref, lam_ref, acc_ref):
        j = pl.program_id(1)

        @pl.when(j == 0)
        def _():
            st_ref[...] = cl_ref[...]
            acc_ref[...] = jnp.zeros_like(acc_ref)
            db_ref[...] = jnp.zeros_like(db_ref)
            dc_ref[...] = jnp.zeros_like(dc_ref)

        ar, ai = ab_ref[:, :SLAB_COLS], ab_ref[:, SLAB_COLS:]
        lam_ref[...] = _dot(_bf(dy_ref[...]), c_ref[0], NT)
        zr, zi = _scan_rows(lam_ref, n_steps, ar, ai, (st_ref[:, :SLAB_COLS], st_ref[:, SLAB_COLS:]), True, True)
        st_ref[:, :SLAB_COLS] = zr
        st_ref[:, SLAB_COLS:] = zi

        def step(k, carry):
            dr, di = carry
            r0 = pl.multiple_of(k * SUBLANES, SUBLANES)
            r1 = pl.multiple_of((k + 1) * SUBLANES, SUBLANES)
            xr, xi = x_ref[pl.ds(r0, SUBLANES), :SLAB_COLS], x_ref[pl.ds(r0, SUBLANES), SLAB_COLS:]
            lr, li = lam_ref[pl.ds(r1, SUBLANES), :SLAB_COLS], lam_ref[pl.ds(r1, SUBLANES), SLAB_COLS:]
            return dr + xr * lr + xi * li, di + xr * li - xi * lr

        dr, di = lax.fori_loop(0, n_steps - 1, step, (acc_ref[:, :SLAB_COLS], acc_ref[:, SLAB_COLS:]), unroll=4)
        first_chunk = j == nch - 1
        xr = jnp.where(first_chunk, cx_ref[:, :SLAB_COLS], xp_ref[:, :SLAB_COLS])
        xi = jnp.where(first_chunk, cx_ref[:, SLAB_COLS:], xp_ref[:, SLAB_COLS:])
        lr, li = lam_ref[0:SUBLANES, :SLAB_COLS], lam_ref[0:SUBLANES, SLAB_COLS:]
        acc_ref[:, :SLAB_COLS] = dr + xr * lr + xi * li
        acc_ref[:, SLAB_COLS:] = di + xr * li - xi * lr

        lam_b = _bf(lam_ref[...])
        dyv = dy_ref[...]
        db_ref[0] += _dot(_bf(h_ref[...]), lam_b, TN)
        dc_ref[0] += _dot(_bf(x_ref[...]), _bf(dyv), TN)
        du_ref[...] = _dot(lam_b, b_ref[0], NT) + d_ref[...] * dyv

        @pl.when(j == nch - 1)
        def _():
            da_ref[...] = jnp.broadcast_to(jnp.sum(acc_ref[...], axis=0, keepdims=True), (SUBLANES, w))

    sub = rc // SUBLANES
    tile = lambda wd: pl.BlockSpec((rc, wd), lambda sl, j: (nch - 1 - j, sl))
    small = pl.BlockSpec((SUBLANES, w), lambda sl, j: (sl, 0))
    prev = pl.BlockSpec((SUBLANES, w), lambda sl, j: (jnp.maximum((nch - 1 - j) * sub - 1, 0), sl))
    return pl.pallas_call(
        body, name="s5_adjoint_carry_grads", grid=(nsl, nch),
        in_specs=[tile(LANES), pl.BlockSpec((1, w, LANES), lambda sl, j: (sl, 0, 0)), small, small, tile(w), prev, small,
                  tile(LANES), pl.BlockSpec((1, LANES, w), lambda sl, j: (sl, 0, 0)),
                  pl.BlockSpec((1, LANES), lambda sl, j: (0, sl))],
        out_specs=[tile(LANES), pl.BlockSpec((1, LANES, w), lambda sl, j: (sl, 0, 0)),
                   pl.BlockSpec((1, w, LANES), lambda sl, j: (sl, 0, 0)), small],
        out_shape=[jax.ShapeDtypeStruct((s, d), F32), jax.ShapeDtypeStruct((nsl, LANES, w), F32),
                   jax.ShapeDtypeStruct((nsl, w, LANES), F32), jax.ShapeDtypeStruct((nsl * SUBLANES, w), F32)],
        scratch_shapes=[pltpu.VMEM((SUBLANES, w), F32), pltpu.VMEM((rc, w), F32), pltpu.VMEM((SUBLANES, w), F32)],
        compiler_params=_params("parallel", "arbitrary"),
    )(dy, cblk, cinl, ab_tile, xtrue, xtrue, cinx, hp, bblk, dvec)


def _adamw(name, w, g, m, v):
    r, c = w.shape
    tile = r if r * c <= 512 * 1024 else _pick(r, max(SUBLANES, (512 * 1024 // c) // SUBLANES * SUBLANES), q=SUBLANES)
    c1 = 1.0 / (1.0 - ADAM_B1 ** ADAM_STEP)
    c2 = 1.0 / (1.0 - ADAM_B2 ** ADAM_STEP)

    def body(w_ref, g_ref, m_ref, v_ref, d_ref, nm_ref, nv_ref):
        gg = g_ref[...]
        nm = ADAM_B1 * m_ref[...] + (1.0 - ADAM_B1) * gg
        nv = ADAM_B2 * v_ref[...] + (1.0 - ADAM_B2) * gg * gg
        d_ref[...] = -ADAM_LR * ((nm * c1) / (jnp.sqrt(nv * c2) + ADAM_EPS) + ADAM_WD * w_ref[...])
        nm_ref[...] = nm
        nv_ref[...] = nv

    spec = _nat(tile, c)
    return pl.pallas_call(
        body, name=name, grid=(r // tile,), in_specs=[spec] * 4, out_specs=[spec] * 3,
        out_shape=[jax.ShapeDtypeStruct((r, c), F32)] * 3, compiler_params=_params("parallel"),
    )(w, g, m, v)


def _place():
    x, y, c = lax.axis_index("x"), lax.axis_index("y"), lax.axis_index("c")
    return x, y, c, [(1 - x, y), (x, 1 - y), (1 - x, 1 - y)]


_ANY = pl.BlockSpec(memory_space=pl.ANY)


def _gather_weights(shards):
    n = len(shards)

    def body(*refs):
        ins, outs = refs[:n], refs[n:2 * n]
        send_sems, recv_sems, local_sems = refs[2 * n:]
        x, y, c, chips = _place()
        me = 2 * x + y
        sibling = (x, y, 1 - c)
        started = []
        for a in range(n):
            local = pltpu.make_async_copy(ins[a], outs[a].at[me], local_sems.at[a])
            local.start()
            started.append(local)

        def half(a, chip, h):
            hw = ins[a].shape[1] // 2
            return outs[a].at[chip, :, pl.ds(pl.multiple_of(h * hw, LANES), hw)]

        def copy(a, k, src, chip, h, to):
            return pltpu.make_async_remote_copy(
                src_ref=src, dst_ref=half(a, chip, h), send_sem=send_sems.at[a, k], recv_sem=recv_sems.at[a, k],
                device_id=to, device_id_type=MESH)

        sends = []
        for a in range(n):
            hw = ins[a].shape[1] // 2
            mine = ins[a].at[:, pl.ds(pl.multiple_of(c * hw, LANES), hw)]
            for k, chip in enumerate(chips):
                cp = copy(a, k, mine, me, c, (*chip, c))
                cp.start()
                sends.append(cp)
        for a in range(n):
            for k, (cx, cy) in enumerate(chips):
                src_chip = 2 * cx + cy
                copy(a, k, half(a, src_chip, c), src_chip, c, (x, y, c)).wait_recv()
                fwd = copy(a, 3 + k, half(a, src_chip, c), src_chip, c, sibling)
                fwd.start()
                sends.append(fwd)
        for a in range(n):
            for k, (cx, cy) in enumerate(chips):
                src_chip = 2 * cx + cy
                copy(a, 3 + k, half(a, src_chip, 1 - c), src_chip, 1 - c, (x, y, c)).wait_recv()
        for cp in sends:
            cp.wait_send()
        for cp in started:
            cp.wait()

    return pl.pallas_call(
        body, name="gather_weights",
        in_specs=[_ANY] * n, out_specs=[_ANY] * n,
        out_shape=[jax.ShapeDtypeStruct((N_CHIPS,) + s_.shape, s_.dtype) for s_ in shards],
        scratch_shapes=[pltpu.SemaphoreType.DMA((n, 6)), pltpu.SemaphoreType.DMA((n, 6)), pltpu.SemaphoreType.DMA((n,))],

    )(*shards)


def _gather_weights_async(shards):
    n = len(shards)
    srcs = [jax.new_ref(s_, memory_space=pltpu.MemorySpace.HBM) for s_ in shards]
    outs = [jax.empty_ref(jax.ShapeDtypeStruct((N_CHIPS,) + s_.shape, s_.dtype), memory_space=pltpu.MemorySpace.HBM)
            for s_ in shards]

    @pl.kernel(mesh=plsc.ScalarSubcoreMesh(axis_name="seq", num_cores=1), name="gather_weights_async",
               scratch_types=(pltpu.SemaphoreType.DMA((n, 6)), pltpu.SemaphoreType.DMA((n, 6)),
                              pltpu.SemaphoreType.DMA((n,))),
               compiler_params=pltpu.CompilerParams(collective_id=1))
    def launch(send_sems, recv_sems, local_sems):
        x, y, c, chips = _place()
        me = 2 * x + y
        sibling = (x, y, 1 - c)
        barrier = pltpu.get_barrier_semaphore()
        for peer in [sibling] + [(*chip, c) for chip in chips]:
            pl.semaphore_signal(barrier, inc=1, device_id=peer, device_id_type=MESH)
        pl.semaphore_wait(barrier, 4)

        def half(a, chip, h):
            hw = srcs[a].shape[1] // 2
            return outs[a].at[chip, :, pl.ds(pl.multiple_of(h * hw, LANES), hw)]

        def copy(a, k, src, chip, h, to):
            return pltpu.make_async_remote_copy(
                src_ref=src, dst_ref=half(a, chip, h), send_sem=send_sems.at[a, k], recv_sem=recv_sems.at[a, k],
                device_id=to, device_id_type=MESH)

        locals_, sends = [], []
        for a in range(n):
            local = pltpu.make_async_copy(srcs[a], outs[a].at[me], local_sems.at[a])
            local.start()
            locals_.append(local)
            hw = srcs[a].shape[1] // 2
            mine = srcs[a].at[:, pl.ds(pl.multiple_of(c * hw, LANES), hw)]
            for k, chip in enumerate(chips):
                cp = copy(a, k, mine, me, c, (*chip, c))
                cp.start()
                sends.append(cp)
        for a in range(n):
            for k, (cx, cy) in enumerate(chips):
                src_chip = 2 * cx + cy
                copy(a, k, half(a, src_chip, c), src_chip, c, (x, y, c)).wait_recv()
                fwd = copy(a, 3 + k, half(a, src_chip, c), src_chip, c, sibling)
                fwd.start()
                sends.append(fwd)
        for a in range(n):
            for k, (cx, cy) in enumerate(chips):
                src_chip = 2 * cx + cy
                copy(a, 3 + k, half(a, src_chip, 1 - c), src_chip, 1 - c, (x, y, c)).wait_recv()
        for cp in sends:
            cp.wait_send()
        for cp in locals_:
            cp.wait()

    launch()
    return [o[...] for o in outs]


def _on_sequencer(name, cid, inputs, out_shapes, sem_types, peers, body):
    srcs = [jax.new_ref(a, memory_space=pltpu.MemorySpace.HBM) for a in inputs]
    outs = [jax.empty_ref(sd, memory_space=pltpu.MemorySpace.HBM) for sd in out_shapes]

    @pl.kernel(mesh=plsc.ScalarSubcoreMesh(axis_name="seq", num_cores=1), name=name, scratch_types=tuple(sem_types),
               compiler_params=pltpu.CompilerParams(collective_id=cid))
    def launch(*sems):
        x, y, c, chips = _place()
        barrier = pltpu.get_barrier_semaphore()
        ps = peers(x, y, c, chips)
        for peer in ps:
            pl.semaphore_signal(barrier, inc=1, device_id=peer, device_id_type=MESH)
        pl.semaphore_wait(barrier, len(ps))
        body(srcs, outs, *sems)

    launch()
    return [o[...] for o in outs]


def _sibling_only(x, y, c, chips):
    return [(x, y, 1 - c)]


def _same_core_of_other_chips(x, y, c, chips):
    return [(*chip, c) for chip in chips]


def _swap_halves_to_sibling(name, cid, grads):
    n = len(grads)

    def body(ins, outs, send_sems, recv_sems):
        x, y, c, _ = _place()
        cps = []
        for a in range(n):
            hw = ins[a].shape[2] // 2
            src = ins[a].at[:, :, pl.ds(pl.multiple_of((1 - c) * hw, LANES), hw)]
            cp = pltpu.make_async_remote_copy(src_ref=src, dst_ref=outs[a], send_sem=send_sems.at[a],
                                              recv_sem=recv_sems.at[a], device_id=(x, y, 1 - c), device_id_type=MESH)
            cp.start()
            cps.append(cp)
        for cp in cps:
            cp.wait()

    return _on_sequencer(
        name, cid, grads, [jax.ShapeDtypeStruct(g.shape[:2] + (g.shape[2] // 2,), g.dtype) for g in grads],
        [pltpu.SemaphoreType.DMA((n,)), pltpu.SemaphoreType.DMA((n,))], _sibling_only, body)


def _exchange_quarters(name, cid, parts):
    n = len(parts)

    def body(ins, outs, send_sems, recv_sems):
        x, y, c, chips = _place()
        cps = []
        for a in range(n):
            for k, (cx, cy) in enumerate(chips):
                cp = pltpu.make_async_remote_copy(
                    src_ref=ins[a].at[2 * cx + cy], dst_ref=outs[a].at[k], send_sem=send_sems.at[a, k],
                    recv_sem=recv_sems.at[a, k], device_id=(cx, cy, c), device_id_type=MESH)
                cp.start()
                cps.append(cp)
        for cp in cps:
            cp.wait()

    return _on_sequencer(
        name, cid, parts, [jax.ShapeDtypeStruct((3,) + p_.shape[1:], p_.dtype) for p_ in parts],
        [pltpu.SemaphoreType.DMA((n, 3)), pltpu.SemaphoreType.DMA((n, 3))], _same_core_of_other_chips, body)


def _swap_final_halves(name, cid, halves):
    n = len(halves)

    def body(ins, outs, send_sems, recv_sems):
        x, y, c, _ = _place()
        cps = []
        for a in range(n):
            cp = pltpu.make_async_remote_copy(src_ref=ins[a], dst_ref=outs[a], send_sem=send_sems.at[a],
                                              recv_sem=recv_sems.at[a], device_id=(x, y, 1 - c), device_id_type=MESH)
            cp.start()
            cps.append(cp)
        for cp in cps:
            cp.wait()

    return _on_sequencer(
        name, cid, halves, [jax.ShapeDtypeStruct(h.shape, h.dtype) for h in halves],
        [pltpu.SemaphoreType.DMA((n,)), pltpu.SemaphoreType.DMA((n,))], _sibling_only, body)


def _add_half(name, grad, recv):
    nchip, r, cfull = grad.shape
    hw = cfull // 2
    tile = _pick(r, max(BF16_ROWS, (256 * 1024 // hw) // BF16_ROWS * BF16_ROWS), q=BF16_ROWS)
    c = lax.axis_index("c")

    def body(c_ref, g_ref, r_ref, o_ref):
        o_ref[...] = _bf(g_ref[...] + r_ref[...])

    return pl.pallas_call(
        body, name=name,
        grid_spec=pltpu.PrefetchScalarGridSpec(
            num_scalar_prefetch=1, grid=(nchip, r // tile),
            in_specs=[pl.BlockSpec((1, tile, hw), lambda k, i, cr: (k, i, cr[0])),
                      pl.BlockSpec((1, tile, hw), lambda k, i, cr: (k, i, 0))],
            out_specs=pl.BlockSpec((1, tile, hw), lambda k, i, cr: (k, i, 0))),
        out_shape=jax.ShapeDtypeStruct((nchip, r, hw), BF16), compiler_params=_params("parallel", "parallel"),
    )(c.reshape(1).astype(jnp.int32), grad, recv)


def _add_quarters(name, part, recv):
    _, r, hw = part.shape
    tile = _pick(r, max(BF16_ROWS, (256 * 1024 // hw) // BF16_ROWS * BF16_ROWS), q=BF16_ROWS)
    me = 2 * lax.axis_index("x") + lax.axis_index("y")

    def body(me_ref, p_ref, r_ref, o_ref):
        f = lambda v: v.astype(F32)
        o_ref[...] = ((f(p_ref[0]) + f(r_ref[0])) + f(r_ref[1])) + f(r_ref[2])

    return pl.pallas_call(
        body, name=name,
        grid_spec=pltpu.PrefetchScalarGridSpec(
            num_scalar_prefetch=1, grid=(r // tile,),
            in_specs=[pl.BlockSpec((1, tile, hw), lambda i, mr: (mr[0], i, 0)),
                      pl.BlockSpec((3, tile, hw), lambda i, mr: (0, i, 0))],
            out_specs=pl.BlockSpec((tile, hw), lambda i, mr: (i, 0))),
        out_shape=jax.ShapeDtypeStruct((r, hw), F32), compiler_params=_params("parallel"),
    )(me.reshape(1).astype(jnp.int32), part, recv)


class _ReduceScatter:
    def __init__(self, tag, first_cid, grads):
        self.tag, self.cid = tag, first_cid
        self.stacks = [g.reshape(N_CHIPS, g.shape[0] // N_CHIPS, g.shape[1]) for g in grads]

    def start(self, anchor):
        self.stacks, anchor = lax.optimization_barrier((self.stacks, anchor))
        self.recv = _swap_halves_to_sibling(f"rs_swap_halves_{self.tag}", self.cid, self.stacks)
        return anchor

    def exchange(self, anchor):
        parts = [_add_half(f"rs_add_half_{self.tag}{a}", g, r) for a, (g, r) in enumerate(zip(self.stacks, self.recv))]
        self.parts, anchor = lax.optimization_barrier((parts, anchor))
        self.quarters = _exchange_quarters(f"rs_exchange_{self.tag}", self.cid + 1, self.parts)
        return anchor

    def join(self, anchor):
        halves = [_add_quarters(f"rs_add_quarters_{self.tag}{a}", p_, q_)
                  for a, (p_, q_) in enumerate(zip(self.parts, self.quarters))]
        self.halves, anchor = lax.optimization_barrier((halves, anchor))
        self.others = _swap_final_halves(f"rs_swap_final_{self.tag}", self.cid + 2, self.halves)
        return anchor

    def result(self):
        south = lax.axis_index("c") == 0
        return [jnp.concatenate([jnp.where(south, h, o), jnp.where(south, o, h)], axis=1)
                for h, o in zip(self.halves, self.others)]


def _allgather_small(pack):
    m_per, n = pack.shape

    def body(x_ref, out_ref, send_sems, recv_sems, local_sem):
        x, y, c, chips = _place()
        me, sibling = (x, y, c), (x, y, 1 - c)

        def rows(px, py, pc):
            return out_ref.at[pl.ds(pl.multiple_of((4 * px + 2 * py + pc) * m_per, SUBLANES), m_per), :]

        def copy(k, block, to, src=None):
            return pltpu.make_async_remote_copy(
                src_ref=rows(*block) if src is None else src, dst_ref=rows(*block),
                send_sem=send_sems.at[k], recv_sem=recv_sems.at[k], device_id=to, device_id_type=MESH)

        mine = pltpu.make_async_copy(x_ref, rows(*me), local_sem)
        mine.start()
        first = [copy(0, me, sibling, src=x_ref)]
        first += [copy(1 + j, me, (*chip, c), src=x_ref) for j, chip in enumerate(chips)]
        for cp in first:
            cp.start()
        passed = [copy(4 + j, (*chip, c), sibling) for j, chip in enumerate(chips)]
        for j, chip in enumerate(chips):
            copy(1 + j, (*chip, c), me).wait_recv()
            passed[j].start()
        copy(0, sibling, me).wait_recv()
        for j, chip in enumerate(chips):
            copy(4 + j, (*chip, 1 - c), me).wait_recv()
        for cp in first + passed:
            cp.wait_send()
        mine.wait()

    return pl.pallas_call(
        body, name="allgather_small_grads",
        out_shape=jax.ShapeDtypeStruct((N_DEV * m_per, n), pack.dtype),
        in_specs=[pl.BlockSpec(memory_space=pltpu.VMEM)], out_specs=pl.BlockSpec(memory_space=pltpu.VMEM),
        scratch_shapes=[pltpu.SemaphoreType.DMA((7,)), pltpu.SemaphoreType.DMA((7,)), pltpu.SemaphoreType.DMA],
        compiler_params=pltpu.CompilerParams(vmem_limit_bytes=VMEM_LIMIT),
    )(pack)


def _sum_devices(packs, m_per):
    tile = _pick(m_per, 512, q=SUBLANES)
    nt = m_per // tile

    def body(*refs):
        acc = refs[0][...]
        for r in refs[1:N_DEV]:
            acc = acc + r[...]
        refs[N_DEV][...] = acc

    return pl.pallas_call(
        body, name="sum_small_grads", grid=(nt,),
        in_specs=[pl.BlockSpec((tile, LANES), functools.partial(lambda i, k: (k * nt + i, 0), k=k)) for k in range(N_DEV)],
        out_specs=_nat(tile, LANES), out_shape=jax.ShapeDtypeStruct((m_per, LANES), F32),
        compiler_params=_params("parallel"),
    )(*([packs] * N_DEV))


def _tail_fwd(tag, alpha, h_in, adds, mix_gate, ln1, ln2, p_l, w, want_perm):
    h_mid, xh1, rs1, h_mid_b, _ = _ln_fwd(f"ln1_fwd_{tag}", alpha, h_in, adds, mix_gate, *ln1)
    gp = _matmul(f"ple_gate_fwd_{tag}", h_mid_b, w['wg'], 'nn')
    pw = _matmul(f"ple_proj_fwd_{tag}", p_l, w['plet'], 'nt')
    gu = _matmul(f"ffn_in_fwd_{tag}", h_mid_b, w['wit'], 'nt', tn=1408)
    act = _swiglu_fwd(f"swiglu_fwd_{tag}", gu)
    ffn = _matmul(f"ffn_out_fwd_{tag}", act, w['wo'], 'nn', tk=2816)
    h_out, xh2, rs2, _, h_perm = _ln_fwd(f"ln2_fwd_{tag}", alpha, h_mid, [(ffn, 'nat')],
                                         ('nat', (pw, 1, 0), (gp, 1, 0)), *ln2, want_perm=want_perm)
    saved = dict(h_mid_b=h_mid_b, xh1=xh1, rs1=rs1, gp=gp, pw=pw, gu=gu, act=act, xh2=xh2, rs2=rs2)
    return h_out, h_perm, saved


def _tail_bwd(tag, alpha, dparts, sv, ln1_g, ln2_g, p_l, w, mix_gate):
    d = sv['h_mid_b'].shape[1]
    dz2, dz2b, dgate, dg2, db2 = _ln_bwd(f"ln2_bwd_{tag}", dparts, sv['xh2'], sv['rs2'], ln2_g,
                                         gate=('nat', (sv['pw'], 1, 0), (sv['gp'], 1, 0)))
    grads = dict(ln2_g=dg2, ln2_b=db2)
    grads['plet'] = _matmul(f"ple_proj_dw_{tag}", dgate, p_l, 'tn', a_win=(0, d))
    grads['wg'] = _matmul(f"ple_gate_dw_{tag}", sv['h_mid_b'], dgate, 'tn', b_win=(d, d))
    dx_gate = _matmul(f"ple_gate_dx_{tag}", dgate, w['wg'], 'nt', a_win=(d, d))
    dact = _matmul(f"ffn_out_dx_{tag}", dz2b, w['wo'], 'nt', out_dtype=BF16, tn=1408)
    grads['wo'] = _matmul(f"ffn_out_dw_{tag}", sv['act'], dz2b, 'tn', tm=1408)
    dgu = _swiglu_bwd(f"swiglu_bwd_{tag}", sv['gu'], dact)
    grads['wit'] = _matmul(f"ffn_in_dw_{tag}", dgu, sv['h_mid_b'], 'tn')
    dx_ffn = _matmul(f"ffn_in_dx_{tag}", dgu, w['wit'], 'nn', tk=2816)
    res = _ln_bwd(f"ln1_bwd_{tag}", [(dz2, 'nat', alpha), (dx_gate, 'nat', 1.0), (dx_ffn, 'nat', 1.0)],
                  sv['xh1'], sv['rs1'], ln1_g, gate=mix_gate)
    grads['ln1_g'], grads['ln1_b'] = res[-2], res[-1]
    return res[:-2], grads


def kernel(x, p, positions, attn_w_in, mla_q_norm, mla_w_q_b, mla_kv_norm, mla_w_kv_b, attn_w_out, s5_a_re, s5_a_im, s5_log_dt, s5_b_re, s5_b_im, s5_c_re, s5_c_im, s5_d, s5_w_glu, ln1_g, ln1_b, ffn_w_in, ffn_w_out, ple_w, ple_gate_w, ln2_g, ln2_b, loss_target, m_attn_w_in, m_mla_q_norm, m_mla_w_q_b, m_mla_kv_norm, m_mla_w_kv_b, m_attn_w_out, m_s5_a_re, m_s5_a_im, m_s5_log_dt, m_s5_b_re, m_s5_b_im, m_s5_c_re, m_s5_c_im, m_s5_d, m_s5_w_glu, m_ln1_g, m_ln1_b, m_ffn_w_in, m_ffn_w_out, m_ple_w, m_ple_gate_w, m_ln2_g, m_ln2_b, v_attn_w_in, v_mla_q_norm, v_mla_w_q_b, v_mla_kv_norm, v_mla_w_kv_b, v_attn_w_out, v_s5_a_re, v_s5_a_im, v_s5_log_dt, v_s5_b_re, v_s5_b_im, v_s5_c_re, v_s5_c_im, v_s5_d, v_s5_w_glu, v_ln1_g, v_ln1_b, v_ffn_w_in, v_ffn_w_out, v_ple_w, v_ple_gate_w, v_ln2_g, v_ln2_b):
    weights = dict(attn_w_in=attn_w_in, mla_q_norm=mla_q_norm, mla_w_q_b=mla_w_q_b, mla_kv_norm=mla_kv_norm,
                   mla_w_kv_b=mla_w_kv_b, attn_w_out=attn_w_out, s5_a_re=s5_a_re, s5_a_im=s5_a_im, s5_log_dt=s5_log_dt,
                   s5_b_re=s5_b_re, s5_b_im=s5_b_im, s5_c_re=s5_c_re, s5_c_im=s5_c_im, s5_d=s5_d, s5_w_glu=s5_w_glu,
                   ln1_g=ln1_g, ln1_b=ln1_b, ffn_w_in=ffn_w_in, ffn_w_out=ffn_w_out, ple_w=ple_w, ple_gate_w=ple_gate_w,
                   ln2_g=ln2_g, ln2_b=ln2_b)
    m_in = dict(attn_w_in=m_attn_w_in, mla_q_norm=m_mla_q_norm, mla_w_q_b=m_mla_w_q_b, mla_kv_norm=m_mla_kv_norm,
                mla_w_kv_b=m_mla_w_kv_b, attn_w_out=m_attn_w_out, s5_a_re=m_s5_a_re, s5_a_im=m_s5_a_im,
                s5_log_dt=m_s5_log_dt, s5_b_re=m_s5_b_re, s5_b_im=m_s5_b_im, s5_c_re=m_s5_c_re, s5_c_im=m_s5_c_im,
                s5_d=m_s5_d, s5_w_glu=m_s5_w_glu, ln1_g=m_ln1_g, ln1_b=m_ln1_b, ffn_w_in=m_ffn_w_in,
                ffn_w_out=m_ffn_w_out, ple_w=m_ple_w, ple_gate_w=m_ple_gate_w, ln2_g=m_ln2_g, ln2_b=m_ln2_b)
    v_in = dict(attn_w_in=v_attn_w_in, mla_q_norm=v_mla_q_norm, mla_w_q_b=v_mla_w_q_b, mla_kv_norm=v_mla_kv_norm,
                mla_w_kv_b=v_mla_w_kv_b, attn_w_out=v_attn_w_out, s5_a_re=v_s5_a_re, s5_a_im=v_s5_a_im,
                s5_log_dt=v_s5_log_dt, s5_b_re=v_s5_b_re, s5_b_im=v_s5_b_im, s5_c_re=v_s5_c_re, s5_c_im=v_s5_c_im,
                s5_d=v_s5_d, s5_w_glu=v_s5_w_glu, ln1_g=v_ln1_g, ln1_b=v_ln1_b, ffn_w_in=v_ffn_w_in,
                ffn_w_out=v_ffn_w_out, ple_w=v_ple_w, ple_gate_w=v_ple_gate_w, ln2_g=v_ln2_g, ln2_b=v_ln2_b)
    names = list(weights)

    s, d = x.shape[1], x.shape[2]
    depth = ln1_g.shape[0]
    assert depth == 2
    alpha = (2.0 * depth) ** 0.25
    ql, kvl = mla_q_norm.shape[1], mla_kv_norm.shape[1]
    in_cols = N_CHIPS * attn_w_in.shape[2]
    heads = N_CHIPS * mla_w_q_b.shape[2] // (NOPE + ROPE)
    hps = heads // N_CHIPS
    dw = (in_cols - ql - kvl - ROPE) // 3
    dh = dw // DHD
    assert ql % LANES == 0 and kvl == ql and dw % DHD == 0 and heads % N_CHIPS == 0
    ngroups, nstate = s5_a_re.shape[1], s5_a_re.shape[2]
    assert nstate == S5_STATE and ngroups * S5_GROUP == d and d % LANES == 0
    nsl = d // LANES
    seg_len = s // SUBLANES
    n_sq = seg_len.bit_length() - 1
    assert 1 << n_sq == seg_len, "the segment length of the S5 scan must be a power of two"
    for window, dil in DIL_BRANCHES:
        assert window // dil == DIL_STEPS and (s // dil) % DIL_STEPS == 0
    me = 2 * lax.axis_index("x") + lax.axis_index("y")

    xb = x[0]
    target = loss_target[0]
    p_layers = [p[0, 0], p[1, 0]]
    pos = positions[0].astype(F32).reshape(s, 1)
    inv_freq = ROPE_THETA ** (-jnp.arange(ROPE // 2, dtype=F32) / (ROPE // 2))
    invf = jnp.concatenate([inv_freq, inv_freq, jnp.zeros((LANES - ROPE,), F32)]).reshape(1, LANES)
    slopes = 2.0 ** (-8.0 * jnp.arange(1, dh + 1, dtype=F32) / dh)
    slopes = jnp.broadcast_to(jnp.repeat(slopes, SUBLANES)[:, None], (dh * SUBLANES, LANES))

    wqb_t = mla_w_q_b[0].T.reshape(hps, NOPE + ROPE, ql)
    wqb_t = jnp.pad(wqb_t, ((0, 0), (0, QK_PAD - NOPE - ROPE), (0, 0))).reshape(hps * QK_PAD, ql)
    d_cols = max(d // N_CHIPS, 2 * LANES)
    d_pad = jnp.zeros((SUBLANES, d_cols), F32).at[0, :d // N_CHIPS].set(s5_d[0])
    shards = [_bf(attn_w_in[0].T), _bf(wqb_t), _bf(mla_w_kv_b[0].T), _bf(attn_w_out[0]), _bf(s5_w_glu[0].T)]
    for l in range(depth):
        shards += [_bf(ffn_w_in[l].T), _bf(ffn_w_out[l]), _bf(ple_w[l].T), _bf(ple_gate_w[l])]
    shards.append(d_pad)
    first, later = lax.optimization_barrier((list(_gather_weights(shards[:3])), shards[3:]))
    gathered = first + _gather_weights_async(later)
    full = [g.reshape(N_CHIPS * g.shape[1], g.shape[2]) for g in gathered]
    win_t, wqb_t_f, wkv_t, wout, wglu_t = full[:5]
    lw = [dict(wit=full[5 + 4 * l], wo=full[6 + 4 * l], plet=full[7 + 4 * l], wg=full[8 + 4 * l]) for l in range(depth)]
    dvec = full[-1].reshape(N_CHIPS, SUBLANES, d_cols)[:, 0, :d // N_CHIPS].reshape(1, d)
    lat = ql + kvl
    win_t = jnp.concatenate([win_t[:lat + ROPE], jnp.zeros((LANES - ROPE, d), BF16), win_t[lat + ROPE:]], axis=0)
    kpe_cb = lat // LANES
    q_cb = kpe_cb + 1
    a_cb = heads * VDIM // LANES

    xbb = _bf(xb)
    proj = _matmul("attn_in_fwd", xbb, win_t, 'nt', tn=1408)
    nrm = _rms_fwd(proj, ql, kvl, mla_q_norm[0], mla_kv_norm[0])
    q_raw = _matmul("mla_q_up_fwd", nrm, wqb_t_f, 'nt', a_win=(0, ql))
    kv = _matmul("mla_kv_up_fwd", nrm, wkv_t, 'nt', a_win=(ql, kvl))
    qf, kf, vv = _rope_prep(q_raw, kv, proj, kpe_cb, pos, invf, heads)
    out_a, lse_a = _mla_fwd(qf, kf, vv.T, heads)
    out_b, lse_b = _dil_fused_fwd(proj, slopes, dh, q_cb)
    att = _concat_bf16("attn_heads_concat", out_a, out_b)
    mix0 = _matmul("attn_out_fwd", att, wout, 'nn')
    h2, h2p, sv0 = _tail_fwd("l0", alpha, xb, [(mix0, 'nat')], None, (ln1_g[0], ln1_b[0]), (ln2_g[0], ln2_b[0]),
                             p_layers[0], lw[0], want_perm=True)

    rep = lambda a: jnp.repeat(a, S5_GROUP, axis=0)
    ag = (s5_a_re[0], s5_a_im[0], jnp.broadcast_to(s5_log_dt[0][:, None], (ngroups, nstate)))
    a16 = tuple(rep(a) for a in ag)
    b16 = tuple(b[0].transpose(0, 2, 1).reshape(ngroups * S5_GROUP, nstate) for b in (s5_b_re, s5_b_im))
    abr, abi, apr, api, bbr, bbi = _s5_discretise(*a16, *b16, n_sq)
    ab_tile = _slab_tile(abr[::S5_GROUP], abi[::S5_GROUP], nsl)
    ap_tile = _slab_tile(apr[::S5_GROUP], api[::S5_GROUP], nsl)
    bblk = _bf(_slab_in_matrix(bbr.reshape(ngroups, S5_GROUP, nstate), bbi.reshape(ngroups, S5_GROUP, nstate), nsl))
    cblk = _bf(_slab_out_matrix(s5_c_re[0], s5_c_im[0], nsl))
    ends = _s5_pass1(h2p, bblk, ab_tile)
    cinx = _s5_carry("s5_carry_fwd", ends, ap_tile, False)
    xtrue, ypre, zg = _s5_pass2(h2p, bblk, cinx, ab_tile, cblk, dvec)
    vg = _matmul("s5_glu_fwd", zg, wglu_t, 'nt')
    glu_gate = ('perm', (vg, 2, 0), (vg, 2, 1))
    h4, _, sv1 = _tail_fwd("l1", alpha, h2, [], glu_gate, (ln1_g[1], ln1_b[1]), (ln2_g[1], ln2_b[1]),
                           p_layers[1], lw[1], want_perm=False)
    loss = lax.psum(jnp.sum(_loss_partial(h4, target)), ("x", "y", "c"))

    (dz1_1, _, dvg), g1 = _tail_bwd("l1", alpha, [(h4, 'nat', 1.0 / d), (target, 'nat', -1.0 / d)], sv1, ln1_g[1],
                                    ln2_g[1], p_layers[1], lw[1], glu_gate)
    d_wglu_t = _matmul("s5_glu_dw", dvg, zg, 'tn')
    dzg = _matmul("s5_glu_dx", dvg, wglu_t, 'nn')
    rs_l1 = _ReduceScatter("l1", 2, [d_wglu_t, g1['wit'], g1['wo'], g1['plet'], g1['wg']])
    dzg = rs_l1.start(dzg)
    starts, dy, dd = _s5_bwd_pass1(dzg, ypre, cblk, ab_tile, h2p)
    cinl = _s5_carry("s5_carry_bwd", starts, ap_tile, True)
    du_p, d_bblk, d_cblk, d_ab = _s5_bwd_pass2(dy, cblk, cinl, ab_tile, xtrue, cinx, h2p, bblk, dvec)
    gbb = _slab_in_extract(d_bblk, nsl)
    g_c_re, g_c_im = _slab_out_extract(d_cblk, nsl)
    d_ab = d_ab[::SUBLANES]
    gab = (d_ab[:, :SLAB_COLS].reshape(ngroups, nstate), d_ab[:, SLAB_COLS:].reshape(ngroups, nstate))
    g_a_re, g_a_im, g_log_dt, g_b_re, g_b_im = _s5_discretise_bwd(a16, b16, ag, gab, gbb)
    unt = lambda b: b.reshape(ngroups, S5_GROUP, nstate).transpose(0, 2, 1)

    du_p = rs_l1.exchange(du_p)
    (dz1_0, dz1_0b), g0 = _tail_bwd("l0", alpha, [(dz1_1, 'nat', alpha), (du_p, 'perm', 1.0)], sv0, ln1_g[0], ln2_g[0],
                                    p_layers[0], lw[0], None)
    dz1_0b = rs_l1.join(dz1_0b)
    rs_l0 = _ReduceScatter("l0", 5, [g0['wit'], g0['wo'], g0['plet'], g0['wg']])
    dz1_0b = rs_l0.start(dz1_0b)
    datt = _matmul("attn_out_dx", dz1_0b, wout, 'nt')
    d_wout = _matmul("attn_out_dw", att, dz1_0b, 'tn')
    do, delta, delta_t = _attn_bwd_prep(datt, out_a, out_b)
    dqf, dkf, dvv = _mla_bwd(qf, kf, vv, do, lse_a, delta_t, heads, 0)
    dqf = rs_l0.exchange(dqf)
    dq_raw, dkv, dkpe = _rope_unprep(dqf, dkf, dvv, pos, invf, heads)
    d_wqb_t = _matmul("mla_q_up_dw", dq_raw, nrm, 'tn', b_win=(0, ql))
    d_wkv_t = _matmul("mla_kv_up_dw", dkv, nrm, 'tn', b_win=(ql, kvl))
    dnq = _matmul("mla_q_up_dx", dq_raw, wqb_t_f, 'nn')
    dnkv = _matmul("mla_kv_up_dx", dkv, wkv_t, 'nn')
    dqd, dkd, dvd = _dil_fused_bwd(proj, slopes, datt, lse_b, delta, dh, q_cb, a_cb)
    dkpe = rs_l0.join(dkpe)
    dproj, g_gq, g_gkv = _dproj_assemble(proj, dnq, dnkv, dkpe, [dqd], [dkd], [dvd], mla_q_norm[0], mla_kv_norm[0], ql)
    d_win_t = _matmul("attn_in_dw", dproj, xbb, 'tn', tm=1408)
    dx_attn = _matmul("attn_in_dx", dproj, win_t, 'nn')
    grad_x = _axpy("grad_x", alpha, dz1_0, dx_attn)

    d_win_t = jnp.concatenate([d_win_t[:lat + ROPE], d_win_t[lat + LANES:]], axis=0)
    rs_at = _ReduceScatter("attn", 8, [d_win_t, d_wqb_t, d_wkv_t, d_wout])
    grad_x = rs_at.start(grad_x)
    grad_x = rs_at.exchange(grad_x)
    grad_x = rs_at.join(grad_x)
    r_win, r_wqb, r_wkv, r_wout = rs_at.result()
    r_wglu, r_wit1, r_wo1, r_plet1, r_wg1 = rs_l1.result()
    r_wit0, r_wo0, r_plet0, r_wg0 = rs_l0.result()
    r_wqb = r_wqb.reshape(hps, QK_PAD, ql)[:, :NOPE + ROPE].reshape(hps * (NOPE + ROPE), ql)
    grads = dict(attn_w_in=r_win.T[None], mla_w_q_b=r_wqb.T[None], mla_w_kv_b=r_wkv.T[None], attn_w_out=r_wout[None],
                 s5_w_glu=r_wglu.T[None],
                 ffn_w_in=jnp.stack([r_wit0.T, r_wit1.T]), ffn_w_out=jnp.stack([r_wo0, r_wo1]),
                 ple_w=jnp.stack([r_plet0.T, r_plet1.T]), ple_gate_w=jnp.stack([r_wg0, r_wg1]))

    small = dict(mla_q_norm=g_gq, mla_kv_norm=g_gkv, s5_a_re=g_a_re, s5_a_im=g_a_im, s5_log_dt=g_log_dt,
                 s5_b_re=unt(g_b_re), s5_b_im=unt(g_b_im), s5_c_re=g_c_re, s5_c_im=g_c_im, s5_d=dd[0],
                 ln1_g=jnp.stack([g0['ln1_g'], g1['ln1_g']]), ln1_b=jnp.stack([g0['ln1_b'], g1['ln1_b']]),
                 ln2_g=jnp.stack([g0['ln2_g'], g1['ln2_g']]), ln2_b=jnp.stack([g0['ln2_b'], g1['ln2_b']]))
    flat = jnp.concatenate([v_.reshape(-1) for v_ in small.values()])
    m_per = -(-flat.shape[0] // (LANES * SUBLANES)) * SUBLANES
    pack = jnp.pad(flat, (0, m_per * LANES - flat.shape[0])).reshape(m_per, LANES)
    total = _sum_devices(_allgather_small(pack), m_per).reshape(-1)
    off = 0
    for k_, v_ in small.items():
        n_ = v_.size
        piece = total[off:off + n_]
        off += n_
        if k_ == 's5_d':
            grads[k_] = lax.dynamic_slice(piece, (me * (d // N_CHIPS),), (d // N_CHIPS,)).reshape(weights[k_].shape)
        else:
            grads[k_] = piece.reshape(weights[k_].shape)

    deltas, new_m, new_v = {}, {}, {}
    for k_ in names:
        w_ = weights[k_]
        shape = w_.shape
        if w_.ndim == 3 and w_.shape[-1] >= LANES:
            two_d = (shape[0] * shape[1], shape[2])
        elif w_.ndim == 4:
            two_d = (shape[0] * shape[1], shape[2] * shape[3])
        else:
            two_d = (1, w_.size) if w_.ndim == 2 and shape[0] == 1 else (shape[0], w_.size // shape[0])
        dl, nm, nv = _adamw(f"adamw_{k_}", w_.reshape(two_d), grads[k_].reshape(two_d), m_in[k_].reshape(two_d),
                            v_in[k_].reshape(two_d))
        deltas[k_], new_m[k_], new_v[k_] = dl.reshape(shape), nm.reshape(shape), nv.reshape(shape)

    return (loss, grad_x[None], *[grads[k_] for k_ in names], *[deltas[k_] for k_ in names],
            *[new_m[k_] for k_ in names], *[new_v[k_] for k_ in names])
```

```python
import functools
import math

import jax
import jax.numpy as jnp
from jax import lax
from jax.experimental import pallas as pl
from jax.experimental.pallas import tpu as pltpu
from jax.experimental.pallas import tpu_sc as plsc

F32 = jnp.float32
BF16 = jnp.bfloat16
MESH = pl.DeviceIdType.MESH

LANES = 128
SUBLANES = 8
BF16_ROWS = 16
VMEM_LIMIT = 48 * 2 ** 20
N_CHIPS = 4
N_DEV = 8

NOPE = 128
ROPE = 64
VDIM = 128
QK_PAD = 256
DHD = 128
DIL_STEPS = 128
DIL_BRANCHES = ((128, 1), (512, 4), (2048, 16))
ROPE_THETA = 10000.0
S5_GROUP = 16
S5_STATE = 64
SLAB_GROUPS = LANES // S5_GROUP
SLAB_COLS = SLAB_GROUPS * S5_STATE
NEG = -1e30
LN_EPS = 1e-5
RMS_EPS = 1e-6

ADAM_LR = 0.001
ADAM_B1 = 0.9
ADAM_B2 = 0.999
ADAM_EPS = 1e-08
ADAM_WD = 0.01
ADAM_STEP = 10

NN = ((1,), (0,))
NT = ((1,), (1,))
TN = ((0,), (0,))


def _dot(a, b, dims):
    return lax.dot_general(a, b, (dims, ((), ())), preferred_element_type=F32)


def _bf(v):
    return v.astype(BF16)


def _pick(n, target, q=LANES, also=0):
    g = math.gcd(n, also) if also else n
    if g <= target and g == n:
        return n
    best = None
    for t in range(q, min(g, target) + 1, q):
        if g % t == 0:
            best = t
    assert best is not None, (n, target, q, also)
    return best


def _params(*sem):
    return pltpu.CompilerParams(dimension_semantics=sem, vmem_limit_bytes=VMEM_LIMIT)


def _sigmoid(v):
    return 1.0 / (1.0 + jnp.exp(-v))


def _matmul(name, a, b, form, out_dtype=F32, a_win=None, b_win=None, tm=1024, tn=1024, tk=2048):
    c0, aw = a_win if a_win else (0, a.shape[1])
    if form == 'nt':
        assert b_win is None
        n, kdim = b.shape
        d0 = 0
    else:
        kdim = b.shape[0]
        d0, n = b_win if b_win else (0, b.shape[1])
    if form == 'tn':
        m = aw
        assert a.shape[0] == kdim, (name, a.shape, b.shape)
        tm = _pick(m, tm, also=c0)
        tk = _pick(kdim, tk)
        a_off = c0 // tm
    else:
        m = a.shape[0]
        assert aw == kdim, (name, a.shape, b.shape, a_win)
        tm = _pick(m, tm)
        tk = _pick(kdim, tk, also=c0)
        a_off = c0 // tk
    tn = _pick(n, tn, also=d0)
    b_off = d0 // tn
    nk = kdim // tk
    dims = {'nn': NN, 'nt': NT, 'tn': TN}[form]

    def body(a_ref, b_ref, o_ref, *acc):
        prod = _dot(_bf(a_ref[...]), _bf(b_ref[...]), dims)
        if nk == 1:
            o_ref[...] = prod.astype(o_ref.dtype)
            return
        acc_ref, = acc
        k = pl.program_id(2)

        @pl.when(k == 0)
        def _():
            acc_ref[...] = prod

        @pl.when((k > 0) & (k < nk - 1))
        def _():
            acc_ref[...] += prod

        @pl.when(k == nk - 1)
        def _():
            o_ref[...] = (acc_ref[...] + prod).astype(o_ref.dtype)

    if form == 'tn':
        a_spec = pl.BlockSpec((tk, tm), lambda i, j, k: (k, i + a_off))
    else:
        a_spec = pl.BlockSpec((tm, tk), lambda i, j, k: (i, k + a_off))
    if form == 'nt':
        b_spec = pl.BlockSpec((tn, tk), lambda i, j, k: (j, k))
    else:
        b_spec = pl.BlockSpec((tk, tn), lambda i, j, k: (k, j + b_off))
    return pl.pallas_call(
        body, name=name,
        grid=(m // tm, n // tn, nk),
        in_specs=[a_spec, b_spec],
        out_specs=pl.BlockSpec((tm, tn), lambda i, j, k: (i, j)),
        out_shape=jax.ShapeDtypeStruct((m, n), out_dtype),
        scratch_shapes=[pltpu.VMEM((tm, tn), F32)] if nk > 1 else [],
        compiler_params=_params("parallel", "parallel", "arbitrary"),
    )(a, b)


def _nat(tile, width, cb=0):
    return pl.BlockSpec((tile, width), lambda i: (i, cb))


def _perm(tile, width, seg_tiles, ncb=1, cb=0):
    return pl.BlockSpec((tile, width), lambda i: (i % seg_tiles, (i // seg_tiles) * ncb + cb))


def _whole(shape):
    return pl.BlockSpec(shape, lambda i: (0,) * len(shape))


def _perm_view(a):
    s, w = a.shape
    return a.reshape(s // SUBLANES, SUBLANES * w)


def _row_spec(a, layout, tile, width, ncb=1, cb=0):
    if layout == 'nat':
        return a, _nat(tile, width, cb)
    seg_tiles = a.shape[0] // SUBLANES // tile
    return _perm_view(a), _perm(tile, width, seg_tiles, ncb, cb)


def _ln_fwd(name, alpha, a, adds, gate, g, b, want_perm=False, tile=256):
    s, d = a.shape
    n_add = len(adds)
    has_gate = gate is not None

    def body(*refs):
        a_ref = refs[0]
        add_refs = refs[1:1 + n_add]
        pos = 1 + n_add
        if has_gate:
            val_ref, pre_ref = refs[pos], refs[pos + 1]
            pos += 2
        g_ref, b_ref = refs[pos], refs[pos + 1]
        outs = refs[pos + 2:]
        z = alpha * a_ref[...]
        for r in add_refs:
            z = z + r[...]
        if has_gate:
            z = z + val_ref[...] * _sigmoid(pre_ref[...])
        mu = jnp.mean(z, axis=-1, keepdims=True)
        zc = z - mu
        var = jnp.mean(zc * zc, axis=-1, keepdims=True)
        rstd = lax.rsqrt(var + LN_EPS)
        xhat = zc * rstd
        h = xhat * g_ref[...] + b_ref[...]
        outs[0][...] = h
        outs[1][...] = xhat
        outs[2][...] = jnp.broadcast_to(rstd, (tile, LANES))
        outs[3][...] = _bf(h)
        if want_perm:
            outs[4][...] = h

    ins, specs = [a], [_nat(tile, d)]
    for arr, layout in adds:
        x_, sp = _row_spec(arr, layout, tile, d)
        ins.append(x_)
        specs.append(sp)
    if has_gate:
        layout = gate[0]
        for arr, ncb, cb in gate[1:]:
            x_, sp = _row_spec(arr, layout, tile, d, ncb=ncb, cb=cb)
            ins.append(x_)
            specs.append(sp)
    ins += [g.reshape(1, d), b.reshape(1, d)]
    specs += [_whole((1, d)), _whole((1, d))]
    out_shape = [jax.ShapeDtypeStruct((s, d), F32), jax.ShapeDtypeStruct((s, d), F32),
                 jax.ShapeDtypeStruct((s, LANES), F32), jax.ShapeDtypeStruct((s, d), BF16)]
    out_specs = [_nat(tile, d), _nat(tile, d), _nat(tile, LANES), _nat(tile, d)]
    if want_perm:
        seg_tiles = s // SUBLANES // tile
        out_shape.append(jax.ShapeDtypeStruct((s // SUBLANES, SUBLANES * d), F32))
        out_specs.append(_perm(tile, d, seg_tiles))
    res = pl.pallas_call(
        body, name=name, grid=(s // tile,), in_specs=specs, out_specs=out_specs, out_shape=out_shape,
        compiler_params=_params("parallel"),
    )(*ins)
    return res[0], res[1], res[2], res[3], (res[4].reshape(s, d) if want_perm else None)


def _ln_bwd(name, dparts, xhat, rstd, g, gate=None, tile=256):
    s, d = xhat.shape
    n_part = len(dparts)
    coefs = [c for _, _, c in dparts]
    has_gate = gate is not None

    def body(*refs):
        part_refs = refs[:n_part]
        xhat_ref, rstd_ref, g_ref = refs[n_part:n_part + 3]
        pos = n_part + 3
        if has_gate:
            val_ref, pre_ref = refs[pos], refs[pos + 1]
            pos += 2
        outs = list(refs[pos:])
        dz_ref = outs.pop(0)
        dzb_ref = outs.pop(0)
        dgate_ref = outs.pop(0) if has_gate else None
        dg_ref, db_ref = outs
        dh = coefs[0] * part_refs[0][...]
        for c, r in zip(coefs[1:], part_refs[1:]):
            dh = dh + c * r[...]
        xh = xhat_ref[...]
        dxh = dh * g_ref[...]
        m1 = jnp.mean(dxh, axis=-1, keepdims=True)
        m2 = jnp.mean(dxh * xh, axis=-1, keepdims=True)
        dz = rstd_ref[:, 0:1] * (dxh - m1 - xh * m2)
        dz_ref[...] = dz
        dzb_ref[...] = _bf(dz)
        if has_gate:
            sg = _sigmoid(pre_ref[...])
            dval = dz * sg
            dpre = dz * val_ref[...] * sg * (1.0 - sg)
            dgate_ref[...] = jnp.concatenate([_bf(dval), _bf(dpre)], axis=1)

        @pl.when(pl.program_id(0) == 0)
        def _():
            dg_ref[...] = jnp.zeros_like(dg_ref)
            db_ref[...] = jnp.zeros_like(db_ref)

        dg_ref[0:1, :] += jnp.sum(dh * xh, axis=0, keepdims=True)
        db_ref[0:1, :] += jnp.sum(dh, axis=0, keepdims=True)

    ins, specs = [], []
    for arr, layout, _ in dparts:
        x_, sp = _row_spec(arr, layout, tile, d)
        ins.append(x_)
        specs.append(sp)
    ins += [xhat, rstd, g.reshape(1, d)]
    specs += [_nat(tile, d), _nat(tile, LANES), _whole((1, d))]
    gate_layout = None
    if has_gate:
        gate_layout = gate[0]
        for arr, ncb, cb in gate[1:]:
            x_, sp = _row_spec(arr, gate_layout, tile, d, ncb=ncb, cb=cb)
            ins.append(x_)
            specs.append(sp)
    seg_tiles = s // SUBLANES // tile
    out_shape = [jax.ShapeDtypeStruct((s, d), F32), jax.ShapeDtypeStruct((s, d), BF16)]
    out_specs = [_nat(tile, d), _nat(tile, d)]
    if has_gate:
        if gate_layout == 'nat':
            out_shape.append(jax.ShapeDtypeStruct((s, 2 * d), BF16))
            out_specs.append(_nat(tile, 2 * d))
        else:
            out_shape.append(jax.ShapeDtypeStruct((s // SUBLANES, SUBLANES * 2 * d), BF16))
            out_specs.append(_perm(tile, 2 * d, seg_tiles))
    out_shape += [jax.ShapeDtypeStruct((SUBLANES, d), F32)] * 2
    out_specs += [_whole((SUBLANES, d))] * 2
    res = list(pl.pallas_call(
        body, name=name, grid=(s // tile,), in_specs=specs, out_specs=out_specs, out_shape=out_shape,
        compiler_params=_params("arbitrary"),
    )(*ins))
    out = [res.pop(0), res.pop(0)]
    if has_gate:
        out.append(res.pop(0).reshape(s, 2 * d))
    out += [res[0][0], res[1][0]]
    return out


def _loss_partial(h, target, tile=256):
    s, d = h.shape

    def body(h_ref, t_ref, o_ref):
        @pl.when(pl.program_id(0) == 0)
        def _():
            o_ref[...] = jnp.zeros_like(o_ref)

        e = h_ref[...] - t_ref[...]
        sq = e * e
        part = sq[:, 0:LANES]
        for k in range(1, d // LANES):
            part = part + sq[:, k * LANES:(k + 1) * LANES]
        o_ref[0:1, :] += jnp.sum(part, axis=0, keepdims=True) * (0.5 / d)

    return pl.pallas_call(
        body, name="loss_partial", grid=(s // tile,), in_specs=[_nat(tile, d), _nat(tile, d)],
        out_specs=_whole((SUBLANES, LANES)), out_shape=jax.ShapeDtypeStruct((SUBLANES, LANES), F32),
        compiler_params=_params("arbitrary"),
    )(h, target)


def _swiglu_fwd(name, gu, tile=256):
    s, f2 = gu.shape
    f = f2 // 2
    cw = _pick(f, 1408)
    ncb = f // cw

    def body(g_ref, u_ref, o_ref):
        gg = g_ref[...]
        o_ref[...] = _bf(gg * _sigmoid(gg) * u_ref[...])

    return pl.pallas_call(
        body, name=name, grid=(s // tile, ncb),
        in_specs=[pl.BlockSpec((tile, cw), lambda i, j: (i, j)), pl.BlockSpec((tile, cw), lambda i, j: (i, j + ncb))],
        out_specs=pl.BlockSpec((tile, cw), lambda i, j: (i, j)),
        out_shape=jax.ShapeDtypeStruct((s, f), BF16), compiler_params=_params("parallel", "parallel"),
    )(gu, gu)


def _swiglu_bwd(name, gu, dact, tile=128):
    s, f2 = gu.shape
    f = f2 // 2

    def body(g_ref, u_ref, da_ref, o_ref):
        gg = g_ref[...]
        sg = _sigmoid(gg)
        da = da_ref[...].astype(F32)
        silu = gg * sg
        o_ref[:, :f] = _bf(da * u_ref[...] * (sg + silu * (1.0 - sg)))
        o_ref[:, f:] = _bf(da * silu)

    return pl.pallas_call(
        body, name=name, grid=(s // tile,),
        in_specs=[_nat(tile, f, 0), _nat(tile, f, 1), _nat(tile, f)], out_specs=_nat(tile, f2),
        out_shape=jax.ShapeDtypeStruct((s, f2), BF16), compiler_params=_params("parallel"),
    )(gu, gu, dact)


def _rms_fwd(proj, ql, kvl, gq, gkv, tile=256):
    s = proj.shape[0]
    assert ql == kvl

    def body(q_ref, kv_ref, gq_ref, gkv_ref, o_ref):
        def nrm(x, gg):
            return x * lax.rsqrt(jnp.mean(x * x, axis=-1, keepdims=True) + RMS_EPS) * gg

        o_ref[...] = jnp.concatenate([_bf(nrm(q_ref[...], gq_ref[...])), _bf(nrm(kv_ref[...], gkv_ref[...]))], axis=1)

    return pl.pallas_call(
        body, name="mla_rms_fwd", grid=(s // tile,),
        in_specs=[_nat(tile, ql, 0), _nat(tile, kvl, 1), _whole((1, ql)), _whole((1, kvl))],
        out_specs=_nat(tile, ql + kvl), out_shape=jax.ShapeDtypeStruct((s, ql + kvl), BF16),
        compiler_params=_params("parallel"),
    )(proj, proj, gq.reshape(1, ql), gkv.reshape(1, kvl))


def _rope_coeffs(pos, invf):
    ang = pos * invf
    cs, sn = jnp.cos(ang), jnp.sin(ang)
    lane = lax.broadcasted_iota(jnp.int32, ang.shape, 1)
    half = ROPE // 2
    c = jnp.where(lane < ROPE, cs, 0.0)
    sa = jnp.where(lane < half, -sn, 0.0)
    sb = jnp.where((lane >= half) & (lane < ROPE), sn, 0.0)
    return c, sa, sb


def _rope_prep(q_raw, kv, proj, kpe_cb, pos, invf, heads, tile=256):
    s = q_raw.shape[0]
    half = ROPE // 2

    def body(q_ref, kv_ref, kpe_ref, pos_ref, invf_ref, qf_ref, kf_ref, v_ref):
        c, sa, sb = _rope_coeffs(pos_ref[...], invf_ref[...])

        def rope(t):
            return t * c + pltpu.roll(t, LANES - half, 1) * sa + pltpu.roll(t, half, 1) * sb

        kr = _bf(rope(kpe_ref[...]))
        for hh in range(heads):
            o = hh * QK_PAD
            qf_ref[:, o:o + NOPE] = _bf(q_ref[:, o:o + NOPE])
            qf_ref[:, o + NOPE:o + QK_PAD] = _bf(rope(q_ref[:, o + NOPE:o + QK_PAD]))
            kf_ref[:, o:o + NOPE] = _bf(kv_ref[:, o:o + NOPE])
            kf_ref[:, o + NOPE:o + QK_PAD] = kr
            v_ref[:, hh * VDIM:(hh + 1) * VDIM] = _bf(kv_ref[:, o + NOPE:o + QK_PAD])

    w = heads * QK_PAD
    return pl.pallas_call(
        body, name="mla_rope_prep", grid=(s // tile,),
        in_specs=[_nat(tile, w), _nat(tile, w), _nat(tile, LANES, kpe_cb), _nat(tile, 1), _whole((1, LANES))],
        out_specs=[_nat(tile, w), _nat(tile, w), _nat(tile, heads * VDIM)],
        out_shape=[jax.ShapeDtypeStruct((s, w), BF16), jax.ShapeDtypeStruct((s, w), BF16),
                   jax.ShapeDtypeStruct((s, heads * VDIM), BF16)],
        compiler_params=_params("parallel"),
    )(q_raw, kv, proj, pos, invf)


def _rope_unprep(dqf, dkf, dv, pos, invf, heads, tile=256):
    s = dqf.shape[0]
    half = ROPE // 2

    def body(dq_ref, dk_ref, dv_ref, pos_ref, invf_ref, dqr_ref, dkv_ref, dkpe_ref):
        c, sa, sb = _rope_coeffs(pos_ref[...], invf_ref[...])

        def unrope(gt):
            return gt * c + pltpu.roll(gt * sa, half, 1) + pltpu.roll(gt * sb, LANES - half, 1)

        dkpe = jnp.zeros((tile, LANES), F32)
        for hh in range(heads):
            o = hh * QK_PAD
            dqr_ref[:, o:o + NOPE] = _bf(dq_ref[:, o:o + NOPE])
            dqr_ref[:, o + NOPE:o + QK_PAD] = _bf(unrope(dq_ref[:, o + NOPE:o + QK_PAD]))
            dkv_ref[:, o:o + NOPE] = _bf(dk_ref[:, o:o + NOPE])
            dkv_ref[:, o + NOPE:o + QK_PAD] = _bf(dv_ref[:, hh * VDIM:(hh + 1) * VDIM])
            dkpe = dkpe + dk_ref[:, o + NOPE:o + QK_PAD]
        dkpe_ref[...] = unrope(dkpe)

    w = heads * QK_PAD
    return pl.pallas_call(
        body, name="mla_rope_unprep", grid=(s // tile,),
        in_specs=[_nat(tile, w), _nat(tile, w), _nat(tile, heads * VDIM), _nat(tile, 1), _whole((1, LANES))],
        out_specs=[_nat(tile, w), _nat(tile, w), _nat(tile, LANES)],
        out_shape=[jax.ShapeDtypeStruct((s, w), BF16), jax.ShapeDtypeStruct((s, w), BF16),
                   jax.ShapeDtypeStruct((s, LANES), F32)],
        compiler_params=_params("parallel"),
    )(dqf, dkf, dv, pos, invf)


LOG2E = 1.4426950408889634
MLA_SCALE = (NOPE + ROPE) ** -0.5


def _mla_scores_t(k, q, t, masked):
    sc = _dot(k, q, NT) * (MLA_SCALE * LOG2E)
    if masked:
        row = lax.broadcasted_iota(jnp.int32, (t, t), 0)
        col = lax.broadcasted_iota(jnp.int32, (t, t), 1)
        sc = jnp.where(row <= col, sc, NEG)
    return sc


def _mla_fwd(qf, kf, vt, heads, t=512):
    s = qf.shape[0]
    t = min(t, s)
    nq = s // t

    def body(q_ref, k_ref, vt_ref, o_ref, lse_ref, m_ref, l_ref, acc_ref):
        i = pl.program_id(1)
        m_ref[...] = jnp.full_like(m_ref, NEG)
        l_ref[...] = jnp.zeros_like(l_ref)
        acc_ref[...] = jnp.zeros_like(acc_ref)
        q = q_ref[...]

        def block(j, masked):
            r0 = pl.multiple_of(j * t, t)
            sc = _mla_scores_t(k_ref[pl.ds(r0, t), :], q, t, masked)
            m_prev = m_ref[0:1, :]
            m_new = jnp.maximum(m_prev, jnp.max(sc, axis=0, keepdims=True))
            corr = jnp.exp2(m_prev - m_new)
            p = jnp.exp2(sc - m_new)
            l_new = corr * l_ref[0:1, :] + jnp.sum(p, axis=0, keepdims=True)
            acc_ref[...] = corr * acc_ref[...] + _dot(vt_ref[:, pl.ds(r0, t)], _bf(p), NN)
            m_ref[...] = jnp.broadcast_to(m_new, (SUBLANES, t))
            l_ref[...] = jnp.broadcast_to(l_new, (SUBLANES, t))

        def unmasked(j, carry):
            block(j, False)
            return carry

        lax.fori_loop(0, i, unmasked, 0)
        block(i, True)
        o_ref[...] = (acc_ref[...] / l_ref[0:1, :]).T
        lse_ref[...] = m_ref[...] + jnp.log(l_ref[...]) * LOG2E

    return pl.pallas_call(
        body, name="mla_flash_fwd", grid=(heads, nq),
        in_specs=[pl.BlockSpec((t, QK_PAD), lambda h, i: (i, h)), pl.BlockSpec((s, QK_PAD), lambda h, i: (0, h)),
                  pl.BlockSpec((VDIM, s), lambda h, i: (h, 0))],
        out_specs=[pl.BlockSpec((t, VDIM), lambda h, i: (i, h)), pl.BlockSpec((SUBLANES, t), lambda h, i: (h, i))],
        out_shape=[jax.ShapeDtypeStruct((s, heads * VDIM), F32), jax.ShapeDtypeStruct((heads * SUBLANES, s), F32)],
        scratch_shapes=[pltpu.VMEM((SUBLANES, t), F32), pltpu.VMEM((SUBLANES, t), F32), pltpu.VMEM((VDIM, t), F32)],
        compiler_params=_params("parallel", "arbitrary"),
    )(qf, kf, vt)


def _mla_bwd(qf, kf, v, do, lse_t, delta_t, heads, do_cb0, t=512):
    s = qf.shape[0]
    t = min(t, s)
    nq = s // t

    def body(q_ref, k_ref, v_ref, do_ref, lse_ref, dl_ref, dq_ref, dk_ref, dv_ref, acc_ref):
        i = pl.program_id(1)

        @pl.when(i == 0)
        def _():
            dk_ref[...] = jnp.zeros_like(dk_ref)
            dv_ref[...] = jnp.zeros_like(dv_ref)

        acc_ref[...] = jnp.zeros_like(acc_ref)
        q, dob = q_ref[...], do_ref[...]
        lse, dl = lse_ref[0:1, :], dl_ref[0:1, :]

        def block(j, masked):
            r0 = pl.multiple_of(j * t, t)
            k = k_ref[pl.ds(r0, t), :]
            p = jnp.exp2(_mla_scores_t(k, q, t, masked) - lse)
            dp = _dot(v_ref[pl.ds(r0, t), :], dob, NT)
            ds = _bf(p * (dp - dl) * MLA_SCALE)
            acc_ref[...] += _dot(ds, k, TN)
            dk_ref[pl.ds(r0, t), :] += _dot(ds, q, NN)
            dv_ref[pl.ds(r0, t), :] += _dot(_bf(p), dob, NN)

        def unmasked(j, carry):
            block(j, False)
            return carry

        lax.fori_loop(0, i, unmasked, 0)
        block(i, True)
        dq_ref[...] = acc_ref[...]

    qs = lambda w, off=0: pl.BlockSpec((t, w), lambda h, i: (i, h + off))
    ks = lambda w: pl.BlockSpec((s, w), lambda h, i: (0, h))
    st = pl.BlockSpec((SUBLANES, t), lambda h, i: (h, i))
    return pl.pallas_call(
        body, name="mla_flash_bwd", grid=(heads, nq),
        in_specs=[qs(QK_PAD), ks(QK_PAD), ks(VDIM), qs(VDIM, do_cb0), st, st],
        out_specs=[qs(QK_PAD), ks(QK_PAD), ks(VDIM)],
        out_shape=[jax.ShapeDtypeStruct((s, heads * QK_PAD), F32), jax.ShapeDtypeStruct((s, heads * QK_PAD), F32),
                   jax.ShapeDtypeStruct((s, heads * VDIM), F32)],
        scratch_shapes=[pltpu.VMEM((t, QK_PAD), F32)],
        compiler_params=_params("parallel", "arbitrary"),
    )(qf, kf, v, do, lse_t, delta_t)


def _band_mask(tq, first_block):
    row = lax.broadcasted_iota(jnp.int32, (tq, DIL_STEPS + tq), 0)
    col = lax.broadcasted_iota(jnp.int32, (tq, DIL_STEPS + tq), 1)
    dist = row + DIL_STEPS - col
    valid = (dist >= 0) & (dist <= DIL_STEPS) & (jnp.logical_not(first_block) | (col >= DIL_STEPS))
    return dist, valid


def _dil_scores(q, kp, kc, slope, dil, tq, first_block):
    sc = jnp.concatenate([_dot(q, kp, NT), _dot(q, kc, NT)], axis=1) * (DHD ** -0.5)
    dist, valid = _band_mask(tq, first_block)
    return jnp.where(valid, sc - slope * (dil * dist).astype(F32), NEG)


def _dil_specs(proj_w, dh, tq):
    pwb = proj_w // LANES
    r_of = lambda cb: cb // dh
    h_of = lambda cb: cb % dh
    cur = lambda off: pl.BlockSpec((tq, DHD), lambda cb, i: (i, r_of(cb) * pwb + off + h_of(cb)))
    prev = lambda off: pl.BlockSpec(
        (DIL_STEPS, DHD), lambda cb, i: (jnp.maximum(i * (tq // DIL_STEPS) - 1, 0), r_of(cb) * pwb + off + h_of(cb)))
    return cur, prev


def _dil_fwd(name, proj, slopes, dil, dh, q_cb, tq=512):
    s, pw = proj.shape
    l = s // dil
    tq = min(tq, l)
    nb = l // tq
    k_cb, v_cb = q_cb + dh, q_cb + 2 * dh
    cur, prev = _dil_specs(pw, dh, tq)
    pv = proj.reshape(l, dil * pw)

    def body(q_ref, kc_ref, kp_ref, vc_ref, vp_ref, sl_ref, o_ref, lse_ref):
        i = pl.program_id(1)
        sc = _dil_scores(_bf(q_ref[...]), _bf(kp_ref[...]), _bf(kc_ref[...]), sl_ref[0:1, 0:1], dil, tq, i == 0)
        m = jnp.max(sc, axis=-1, keepdims=True)
        e = jnp.exp(sc - m)
        lsum = jnp.sum(e, axis=-1, keepdims=True)
        pn = e / lsum
        o_ref[...] = (_dot(_bf(pn[:, :DIL_STEPS]), _bf(vp_ref[...]), NN)
                      + _dot(_bf(pn[:, DIL_STEPS:]), _bf(vc_ref[...]), NN))
        lse_ref[...] = jnp.broadcast_to(m + jnp.log(lsum), (tq, LANES))

    ospec = pl.BlockSpec((tq, DHD), lambda cb, i: (i, cb))
    o, lse = pl.pallas_call(
        body, name=name, grid=(dil * dh, nb),
        in_specs=[cur(q_cb), cur(k_cb), prev(k_cb), cur(v_cb), prev(v_cb),
                  pl.BlockSpec((SUBLANES, LANES), lambda cb, i: (cb % dh, 0))],
        out_specs=[ospec, ospec],
        out_shape=[jax.ShapeDtypeStruct((l, dil * dh * DHD), F32)] * 2,
        compiler_params=_params("parallel", "parallel"),
    )(pv, pv, pv, pv, pv, slopes)
    return o.reshape(s, dh * DHD), lse.reshape(s, dh * DHD)


def _dil_bwd_dq(name, proj, slopes, do, lse, delta, dil, dh, q_cb, b_cb0, tq=512):
    s, pw = proj.shape
    mixw = do.shape[1]
    l = s // dil
    tq = min(tq, l)
    nb = l // tq
    k_cb, v_cb = q_cb + dh, q_cb + 2 * dh
    cur, prev = _dil_specs(pw, dh, tq)
    pv = proj.reshape(l, dil * pw)
    mb = mixw // LANES
    mspec = pl.BlockSpec((tq, DHD), lambda cb, i: (i, (cb // dh) * mb + b_cb0 + cb % dh))
    ospec = pl.BlockSpec((tq, DHD), lambda cb, i: (i, cb))

    def body(q_ref, kc_ref, kp_ref, vc_ref, vp_ref, sl_ref, do_ref, lse_ref, dl_ref, dq_ref):
        i = pl.program_id(1)
        kp, kc = _bf(kp_ref[...]), _bf(kc_ref[...])
        sc = _dil_scores(_bf(q_ref[...]), kp, kc, sl_ref[0:1, 0:1], dil, tq, i == 0)
        p = jnp.exp(sc - lse_ref[:, 0:1])
        dob = do_ref[...]
        dp = jnp.concatenate([_dot(dob, _bf(vp_ref[...]), NT), _dot(dob, _bf(vc_ref[...]), NT)], axis=1)
        ds = _bf(p * (dp - dl_ref[:, 0:1]) * (DHD ** -0.5))
        dq_ref[...] = _dot(ds[:, :DIL_STEPS], kp, NN) + _dot(ds[:, DIL_STEPS:], kc, NN)

    dq = pl.pallas_call(
        body, name=name, grid=(dil * dh, nb),
        in_specs=[cur(q_cb), cur(k_cb), prev(k_cb), cur(v_cb), prev(v_cb),
                  pl.BlockSpec((SUBLANES, LANES), lambda cb, i: (cb % dh, 0)), mspec, ospec, mspec],
        out_specs=ospec, out_shape=jax.ShapeDtypeStruct((l, dil * dh * DHD), F32),
        compiler_params=_params("parallel", "parallel"),
    )(pv, pv, pv, pv, pv, slopes, do.reshape(l, dil * mixw), lse.reshape(l, dil * dh * DHD), delta.reshape(l, dil * mixw))
    return dq.reshape(s, dh * DHD)


def _dil_bwd_dkv(name, proj, slopes, do, lse, delta, dil, dh, q_cb, b_cb0, tk=512):
    s, pw = proj.shape
    mixw = do.shape[1]
    l = s // dil
    tk = min(tk, l)
    nb = l // tk
    k_cb, v_cb = q_cb + dh, q_cb + 2 * dh
    pwb, mb = pw // LANES, mixw // LANES
    sub = tk // DIL_STEPS
    last128 = l // DIL_STEPS - 1
    pv = proj.reshape(l, dil * pw)

    def cur(width_blocks, off):
        return pl.BlockSpec((tk, DHD), lambda cb, j: (j, (cb // dh) * width_blocks + off + cb % dh))

    def nxt(width_blocks, off):
        return pl.BlockSpec((DIL_STEPS, DHD), lambda cb, j: (jnp.minimum((j + 1) * sub, last128),
                                                               (cb // dh) * width_blocks + off + cb % dh))

    ocur = pl.BlockSpec((tk, DHD), lambda cb, j: (j, cb))
    onxt = pl.BlockSpec((DIL_STEPS, DHD), lambda cb, j: (jnp.minimum((j + 1) * sub, last128), cb))

    def body(k_ref, v_ref, qc_ref, qn_ref, sl_ref, doc_ref, don_ref, lsec_ref, lsen_ref, dlc_ref, dln_ref,
             dk_ref, dv_ref):
        j = pl.program_id(1)
        slope = sl_ref[0:1, 0:1]
        scale = DHD ** -0.5
        k, v = _bf(k_ref[...]), _bf(v_ref[...])
        qc = _bf(qc_ref[...])
        row = lax.broadcasted_iota(jnp.int32, (tk, tk), 0)
        col = lax.broadcasted_iota(jnp.int32, (tk, tk), 1)
        dist = row - col
        valid = (dist >= 0) & (dist <= DIL_STEPS)
        sc = jnp.where(valid, _dot(qc, k, NT) * scale - slope * (dil * dist).astype(F32), NEG)
        p = jnp.exp(sc - lsec_ref[:, 0:1])
        doc = doc_ref[...]
        ds = _bf(p * (_dot(doc, v, NT) - dlc_ref[:, 0:1]) * scale)
        dv_ref[...] = _dot(_bf(p), doc, TN)
        dk_ref[...] = _dot(ds, qc, TN)
        kl, vl = k[tk - DIL_STEPS:, :], v[tk - DIL_STEPS:, :]
        qn = _bf(qn_ref[...])
        row = lax.broadcasted_iota(jnp.int32, (DIL_STEPS, DIL_STEPS), 0)
        col = lax.broadcasted_iota(jnp.int32, (DIL_STEPS, DIL_STEPS), 1)
        dist = DIL_STEPS + row - col
        valid = (dist <= DIL_STEPS) & (j < nb - 1)
        sc = jnp.where(valid, _dot(qn, kl, NT) * scale - slope * (dil * dist).astype(F32), NEG)
        p = jnp.exp(sc - lsen_ref[:, 0:1])
        don = don_ref[...]
        ds = _bf(p * (_dot(don, vl, NT) - dln_ref[:, 0:1]) * scale)
        dv_ref[tk - DIL_STEPS:, :] += _dot(_bf(p), don, TN)
        dk_ref[tk - DIL_STEPS:, :] += _dot(ds, qn, TN)

    dov = do.reshape(l, dil * mixw)
    dlv = delta.reshape(l, dil * mixw)
    lsv = lse.reshape(l, dil * dh * DHD)
    dk, dv = pl.pallas_call(
        body, name=name, grid=(dil * dh, nb),
        in_specs=[cur(pwb, k_cb), cur(pwb, v_cb), cur(pwb, q_cb), nxt(pwb, q_cb),
                  pl.BlockSpec((SUBLANES, LANES), lambda cb, j: (cb % dh, 0)),
                  cur(mb, b_cb0), nxt(mb, b_cb0), ocur, onxt, cur(mb, b_cb0), nxt(mb, b_cb0)],
        out_specs=[ocur, ocur], out_shape=[jax.ShapeDtypeStruct((l, dil * dh * DHD), F32)] * 2,
        compiler_params=_params("parallel", "parallel"),
    )(pv, pv, pv, pv, slopes, dov, dov, lsv, lsv, dlv, dlv)
    return dk.reshape(s, dh * DHD), dv.reshape(s, dh * DHD)


def _dil_merge(out_a, outs, lses, tile=256):
    s, wa = out_a.shape
    wb = outs[0].shape[1]
    nbr = len(outs)

    def body(*refs):
        a_ref = refs[0]
        o_refs, l_refs = refs[1:1 + nbr], refs[1 + nbr:1 + 2 * nbr]
        att_ref, ob_ref, lse_ref = refs[1 + 2 * nbr:]
        ls = [r[...] for r in l_refs]
        m = ls[0]
        for x_ in ls[1:]:
            m = jnp.maximum(m, x_)
        es = [jnp.exp(x_ - m) for x_ in ls]
        tot = es[0]
        for e in es[1:]:
            tot = tot + e
        ob = (es[0] / tot) * o_refs[0][...]
        for e, r in zip(es[1:], o_refs[1:]):
            ob = ob + (e / tot) * r[...]
        ob_ref[...] = ob
        lse_ref[...] = m + jnp.log(tot)
        att_ref[...] = jnp.concatenate([_bf(a_ref[...]), _bf(ob)], axis=1)

    return pl.pallas_call(
        body, name="dil_merge", grid=(s // tile,),
        in_specs=[_nat(tile, wa)] + [_nat(tile, wb)] * (2 * nbr),
        out_specs=[_nat(tile, wa + wb), _nat(tile, wb), _nat(tile, wb)],
        out_shape=[jax.ShapeDtypeStruct((s, wa + wb), BF16), jax.ShapeDtypeStruct((s, wb), F32),
                   jax.ShapeDtypeStruct((s, wb), F32)],
        compiler_params=_params("parallel"),
    )(out_a, *outs, *lses)


DIL_BLOCK = 2048


def _dil_unit_rows(u, dil, block):
    sub = u // dil
    return u % dil + (dil * DIL_STEPS) * sub, sub == 0


def _dil_unit_scores(q, kp, kc, slope, dil, no_prev):
    sc = jnp.concatenate([_dot(q, kp, NT), _dot(q, kc, NT)], axis=1) * (DHD ** -0.5)
    row = lax.broadcasted_iota(jnp.int32, (DIL_STEPS, 2 * DIL_STEPS), 0)
    col = lax.broadcasted_iota(jnp.int32, (DIL_STEPS, 2 * DIL_STEPS), 1)
    dist = row + DIL_STEPS - col
    valid = (dist >= 0) & (dist <= DIL_STEPS) & (jnp.logical_not(no_prev) | (col >= DIL_STEPS))
    return jnp.where(valid, sc - slope * (dil * dist).astype(F32), NEG)


def _dil_in_specs(pw, dh, q_cb, block, rev_nb=None):
    blk = (lambda i: i) if rev_nb is None else (lambda i: rev_nb - 1 - i)
    own = lambda off: pl.BlockSpec((block, DHD), lambda h, i: (blk(i), off + h))
    prev = lambda off: pl.BlockSpec((block, DHD), lambda h, i: (jnp.maximum(blk(i) - 1, 0), off + h))
    return [own(q_cb), own(q_cb + dh), prev(q_cb + dh), own(q_cb + 2 * dh), prev(q_cb + 2 * dh)]


def _dil_fused_fwd(proj, slopes, dh, q_cb):
    s, pw = proj.shape
    block = min(DIL_BLOCK, s)
    nb = s // block
    n_units = block // DIL_STEPS
    nbr = len(DIL_BRANCHES)
    assert block >= DIL_STEPS * max(d for _, d in DIL_BRANCHES)

    def body(q_ref, kc_ref, kp_ref, vc_ref, vp_ref, sl_ref, o_ref, lse_ref, kk, vv, *per_branch):
        og, mg, lg = per_branch[:nbr], per_branch[nbr:2 * nbr], per_branch[2 * nbr:]
        i = pl.program_id(1)
        kk[0:block, :] = kp_ref[...]
        kk[block:, :] = kc_ref[...]
        vv[0:block, :] = vp_ref[...]
        vv[block:, :] = vc_ref[...]
        slope = sl_ref[0:1, 0:1]
        for g, (_, dil) in enumerate(DIL_BRANCHES):
            def unit(u, carry, g=g, dil=dil):
                q0, first = _dil_unit_rows(u, dil, block)
                rows = lambda base: pl.ds(base, DIL_STEPS, stride=dil) if dil > 1 else pl.ds(base, DIL_STEPS)
                q = _bf(q_ref[rows(q0), :])
                kc, kp = _bf(kk[rows(block + q0), :]), _bf(kk[rows(block + q0 - dil * DIL_STEPS), :])
                vc, vp = _bf(vv[rows(block + q0), :]), _bf(vv[rows(block + q0 - dil * DIL_STEPS), :])
                sc = _dil_unit_scores(q, kp, kc, slope, dil, first & (i == 0))
                m = jnp.max(sc, axis=-1, keepdims=True)
                e = jnp.exp(sc - m)
                og[g][rows(q0), :] = _dot(_bf(e[:, :DIL_STEPS]), vp, NN) + _dot(_bf(e[:, DIL_STEPS:]), vc, NN)
                mg[g][rows(q0), :] = jnp.broadcast_to(m, (DIL_STEPS, LANES))
                lg[g][rows(q0), :] = jnp.broadcast_to(jnp.sum(e, axis=-1, keepdims=True), (DIL_STEPS, LANES))
                return carry

            lax.fori_loop(0, n_units, unit, 0, unroll=4)
        m_all = mg[0][...]
        for g in range(1, nbr):
            m_all = jnp.maximum(m_all, mg[g][...])
        tot = jnp.zeros((block, LANES), F32)
        acc = jnp.zeros((block, DHD), F32)
        for g in range(nbr):
            w = jnp.exp(mg[g][...] - m_all)
            tot = tot + w * lg[g][...]
            acc = acc + w * og[g][...]
        o_ref[...] = acc / tot
        lse_ref[...] = m_all + jnp.log(tot)

    ospec = pl.BlockSpec((block, DHD), lambda h, i: (i, h))
    return pl.pallas_call(
        body, name="dil_fused_fwd", grid=(dh, nb),
        in_specs=_dil_in_specs(pw, dh, q_cb, block) + [pl.BlockSpec((SUBLANES, LANES), lambda h, i: (h, 0))],
        out_specs=[ospec, ospec], out_shape=[jax.ShapeDtypeStruct((s, dh * DHD), F32)] * 2,
        scratch_shapes=[pltpu.VMEM((2 * block, DHD), F32), pltpu.VMEM((2 * block, DHD), F32)]
        + [pltpu.VMEM((block, DHD), F32)] * (3 * nbr),
        compiler_params=_params("parallel", "arbitrary"),
    )(proj, proj, proj, proj, proj, slopes)


def _dil_fused_bwd(proj, slopes, datt, lse, delta, dh, q_cb, b_cb0):
    s, pw = proj.shape
    block = min(DIL_BLOCK, s)
    nb = s // block
    n_units = block // DIL_STEPS
    scale = DHD ** -0.5

    def body(q_ref, kc_ref, kp_ref, vc_ref, vp_ref, sl_ref, do_ref, lse_ref, dl_ref, dq_ref, dk_ref, dv_ref,
             kk, vv, dkk, dvv, carry_k, carry_v):
        ii = pl.program_id(1)
        i = nb - 1 - ii

        @pl.when(ii == 0)
        def _():
            carry_k[...] = jnp.zeros_like(carry_k)
            carry_v[...] = jnp.zeros_like(carry_v)

        kk[0:block, :] = kp_ref[...]
        kk[block:, :] = kc_ref[...]
        vv[0:block, :] = vp_ref[...]
        vv[block:, :] = vc_ref[...]
        dkk[...] = jnp.zeros_like(dkk)
        dvv[...] = jnp.zeros_like(dvv)
        dq_ref[...] = jnp.zeros_like(dq_ref)
        slope = sl_ref[0:1, 0:1]
        for _, dil in DIL_BRANCHES:
            def unit(u, carry, dil=dil):
                q0, first = _dil_unit_rows(u, dil, block)
                rows = lambda base: pl.ds(base, DIL_STEPS, stride=dil) if dil > 1 else pl.ds(base, DIL_STEPS)
                cur, prev = rows(block + q0), rows(block + q0 - dil * DIL_STEPS)
                q = _bf(q_ref[rows(q0), :])
                kc, kp, vc, vp = _bf(kk[cur, :]), _bf(kk[prev, :]), _bf(vv[cur, :]), _bf(vv[prev, :])
                dob = _bf(do_ref[rows(q0), :])
                sc = _dil_unit_scores(q, kp, kc, slope, dil, first & (i == 0))
                p = jnp.exp(sc - lse_ref[rows(q0), 0:1])
                dp = jnp.concatenate([_dot(dob, vp, NT), _dot(dob, vc, NT)], axis=1)
                ds = _bf(p * (dp - dl_ref[rows(q0), 0:1]) * scale)
                pb = _bf(p)
                dq_ref[rows(q0), :] += _dot(ds[:, :DIL_STEPS], kp, NN) + _dot(ds[:, DIL_STEPS:], kc, NN)
                dkk[prev, :] += _dot(ds[:, :DIL_STEPS], q, TN)
                dkk[cur, :] += _dot(ds[:, DIL_STEPS:], q, TN)
                dvv[prev, :] += _dot(pb[:, :DIL_STEPS], dob, TN)
                dvv[cur, :] += _dot(pb[:, DIL_STEPS:], dob, TN)
                return carry

            lax.fori_loop(0, n_units, unit, 0, unroll=4)
        dk_ref[...] = dkk[block:, :] + carry_k[...]
        dv_ref[...] = dvv[block:, :] + carry_v[...]
        carry_k[...] = dkk[0:block, :]
        carry_v[...] = dvv[0:block, :]

    rev = lambda i: nb - 1 - i
    mspec = pl.BlockSpec((block, DHD), lambda h, i: (rev(i), b_cb0 + h))
    ospec = pl.BlockSpec((block, DHD), lambda h, i: (rev(i), h))
    big = lambda: pltpu.VMEM((2 * block, DHD), F32)
    return pl.pallas_call(
        body, name="dil_fused_bwd", grid=(dh, nb),
        in_specs=_dil_in_specs(pw, dh, q_cb, block, rev_nb=nb)
        + [pl.BlockSpec((SUBLANES, LANES), lambda h, i: (h, 0)), mspec, ospec, mspec],
        out_specs=[ospec, ospec, ospec], out_shape=[jax.ShapeDtypeStruct((s, dh * DHD), F32)] * 3,
        scratch_shapes=[big(), big(), big(), big(), pltpu.VMEM((block, DHD), F32), pltpu.VMEM((block, DHD), F32)],
        compiler_params=_params("parallel", "arbitrary"),
    )(proj, proj, proj, proj, proj, slopes, datt, lse, delta)


def _concat_bf16(name, a, b, tile=256):
    s, wa = a.shape
    wb = b.shape[1]

    def body(a_ref, b_ref, o_ref):
        o_ref[...] = jnp.concatenate([_bf(a_ref[...]), _bf(b_ref[...])], axis=1)

    return pl.pallas_call(
        body, name=name, grid=(s // tile,), in_specs=[_nat(tile, wa), _nat(tile, wb)], out_specs=_nat(tile, wa + wb),
        out_shape=jax.ShapeDtypeStruct((s, wa + wb), BF16), compiler_params=_params("parallel"),
    )(a, b)


def _attn_bwd_prep(datt, out_a, out_b, tile=256):
    s, mixw = datt.shape
    wa = out_a.shape[1]
    heads_a = wa // LANES

    def body(d_ref, a_ref, b_ref, do_ref, dl_ref, dlt_ref):
        d = d_ref[...]
        do_ref[...] = _bf(d)
        prod = d * jnp.concatenate([a_ref[...], b_ref[...]], axis=1)
        for hh in range(mixw // LANES):
            sl = slice(hh * LANES, (hh + 1) * LANES)
            dl = jnp.broadcast_to(jnp.sum(prod[:, sl], axis=-1, keepdims=True), (tile, LANES))
            dl_ref[:, sl] = dl
            if hh < heads_a:
                dlt_ref[hh * SUBLANES:(hh + 1) * SUBLANES, :] = dl.T[0:SUBLANES, :]

    return pl.pallas_call(
        body, name="attn_bwd_prep", grid=(s // tile,),
        in_specs=[_nat(tile, mixw), _nat(tile, wa), _nat(tile, mixw - wa)],
        out_specs=[_nat(tile, mixw), _nat(tile, mixw), pl.BlockSpec((heads_a * SUBLANES, tile), lambda i: (0, i))],
        out_shape=[jax.ShapeDtypeStruct((s, mixw), BF16), jax.ShapeDtypeStruct((s, mixw), F32),
                   jax.ShapeDtypeStruct((heads_a * SUBLANES, s), F32)],
        compiler_params=_params("parallel"),
    )(datt, out_a, out_b)


def _dproj_assemble(proj, dnq, dnkv, dkpe, dqs, dks, dvs, gq, gkv, ql, tile=256):
    s, pw = proj.shape
    dw = dqs[0].shape[1]
    nbr = len(dqs)

    def body(*refs):
        ql_ref, kvl_ref, dnq_ref, dnkv_ref, dkpe_ref = refs[:5]
        br = refs[5:5 + 3 * nbr]
        gq_ref, gkv_ref = refs[5 + 3 * nbr:7 + 3 * nbr]
        dp_ref, dgq_ref, dgkv_ref = refs[7 + 3 * nbr:]

        @pl.when(pl.program_id(0) == 0)
        def _():
            dgq_ref[...] = jnp.zeros_like(dgq_ref)
            dgkv_ref[...] = jnp.zeros_like(dgkv_ref)

        def rms_bwd(x, dy, gg, dg_ref):
            r = lax.rsqrt(jnp.mean(x * x, axis=-1, keepdims=True) + RMS_EPS)
            xh = x * r
            dxh = dy * gg
            dg_ref[0:1, :] += jnp.sum(dy * xh, axis=0, keepdims=True)
            return r * (dxh - xh * jnp.mean(dxh * xh, axis=-1, keepdims=True))

        pieces = [_bf(rms_bwd(ql_ref[...], dnq_ref[...], gq_ref[...], dgq_ref)),
                  _bf(rms_bwd(kvl_ref[...], dnkv_ref[...], gkv_ref[...], dgkv_ref)),
                  _bf(dkpe_ref[...])]
        for k in range(3):
            acc = br[k * nbr][...]
            for r in br[k * nbr + 1:(k + 1) * nbr]:
                acc = acc + r[...]
            pieces.append(_bf(acc))
        dp_ref[...] = jnp.concatenate(pieces, axis=1)

    res = pl.pallas_call(
        body, name="dproj_assemble", grid=(s // tile,),
        in_specs=[_nat(tile, ql, 0), _nat(tile, ql, 1), _nat(tile, ql), _nat(tile, ql), _nat(tile, LANES)]
        + [_nat(tile, dw)] * (3 * nbr) + [_whole((1, ql)), _whole((1, ql))],
        out_specs=[_nat(tile, pw), _whole((SUBLANES, ql)), _whole((SUBLANES, ql))],
        out_shape=[jax.ShapeDtypeStruct((s, pw), BF16), jax.ShapeDtypeStruct((SUBLANES, ql), F32),
                   jax.ShapeDtypeStruct((SUBLANES, ql), F32)],
        compiler_params=_params("arbitrary"),
    )(proj, proj, dnq, dnkv, dkpe, *dqs, *dks, *dvs, gq.reshape(1, ql), gkv.reshape(1, ql))
    return res[0], res[1][0], res[2][0]


def _axpy(name, alpha, a, b, tile=256):
    s, d = a.shape

    def body(a_ref, b_ref, o_ref):
        o_ref[...] = alpha * a_ref[...] + b_ref[...]

    return pl.pallas_call(
        body, name=name, grid=(s // tile,), in_specs=[_nat(tile, d), _nat(tile, d)], out_specs=_nat(tile, d),
        out_shape=jax.ShapeDtypeStruct((s, d), F32), compiler_params=_params("parallel"),
    )(a, b)


def _cmul(ar, ai, br, bi):
    return ar * br - ai * bi, ar * bi + ai * br


def _s5_discretise(a_re, a_im, log_dt, b_re, b_im, n_sq):
    shape = a_re.shape

    def body(ar_ref, ai_ref, ldt_ref, br_ref, bi_ref, abr_ref, abi_ref, apr_ref, api_ref, bbr_ref, bbi_ref):
        ar, ai = ar_ref[...], ai_ref[...]
        dt = jnp.exp(ldt_ref[...])
        e = jnp.exp(ar * dt)
        abr, abi = e * jnp.cos(ai * dt), e * jnp.sin(ai * dt)
        den = ar * ar + ai * ai
        qr = ((abr - 1.0) * ar + abi * ai) / den
        qi = (abi * ar - (abr - 1.0) * ai) / den
        bbr, bbi = _cmul(qr, qi, br_ref[...], bi_ref[...])
        abr_ref[...], abi_ref[...] = abr, abi
        bbr_ref[...], bbi_ref[...] = bbr, bbi
        pr, pi = abr, abi
        for _ in range(n_sq):
            pr, pi = _cmul(pr, pi, pr, pi)
        apr_ref[...], api_ref[...] = pr, pi

    return pl.pallas_call(
        body, name="s5_discretise", out_shape=[jax.ShapeDtypeStruct(shape, F32)] * 6,
        compiler_params=pltpu.CompilerParams(vmem_limit_bytes=VMEM_LIMIT),
    )(a_re, a_im, log_dt, b_re, b_im)


def _s5_discretise_bwd(a16, b16, ag, gab, gbb):
    rows, p = a16[0].shape
    g = rows // S5_GROUP

    def disc(ar, ai, ldt):
        dt = jnp.exp(ldt)
        e = jnp.exp(ar * dt)
        abr, abi = e * jnp.cos(ai * dt), e * jnp.sin(ai * dt)
        den = ar * ar + ai * ai
        inv_r, inv_i = ar / den, -ai / den
        qr, qi = _cmul(abr - 1.0, abi, inv_r, inv_i)
        return dt, abr, abi, inv_r, inv_i, qr, qi

    def body(ar16_ref, ai16_ref, ldt16_ref, br_ref, bi_ref, ar_ref, ai_ref, ldt_ref, gar_ref, gai_ref, gbr_ref, gbi_ref,
             dar_ref, dai_ref, dldt_ref, dbr_ref, dbi_ref):
        _, _, _, _, _, qr16, qi16 = disc(ar16_ref[...], ai16_ref[...], ldt16_ref[...])
        gbr, gbi = gbr_ref[...], gbi_ref[...]
        dbr_ref[...], dbi_ref[...] = _cmul(qr16, -qi16, gbr, gbi)
        cr, ci = _cmul(br_ref[...], -bi_ref[...], gbr, gbi)
        gqr = jnp.sum(cr.reshape(g, S5_GROUP, p), axis=1)
        gqi = jnp.sum(ci.reshape(g, S5_GROUP, p), axis=1)
        ar, ai = ar_ref[...], ai_ref[...]
        dt, abr, abi, inv_r, inv_i, qr, qi = disc(ar, ai, ldt_ref[...])
        t_r, t_i = _cmul(inv_r, -inv_i, gqr, gqi)
        gab_r = gar_ref[...] + t_r
        gab_i = gai_ref[...] + t_i
        qa_r, qa_i = _cmul(qr, qi, inv_r, inv_i)
        a1_r, a1_i = _cmul(qa_r, -qa_i, gqr, gqi)
        gl_r, gl_i = _cmul(abr, -abi, gab_r, gab_i)
        dar_ref[...] = dt * gl_r - a1_r
        dai_ref[...] = dt * gl_i - a1_i
        gdt = jnp.sum(ar * gl_r + ai * gl_i, axis=-1, keepdims=True)
        dldt_ref[...] = gdt * dt[:, 0:1]

    return pl.pallas_call(
        body, name="s5_discretise_bwd",
        out_shape=[jax.ShapeDtypeStruct((g, p), F32), jax.ShapeDtypeStruct((g, p), F32),
                   jax.ShapeDtypeStruct((g, 1), F32), jax.ShapeDtypeStruct((rows, p), F32),
                   jax.ShapeDtypeStruct((rows, p), F32)],
        compiler_params=pltpu.CompilerParams(vmem_limit_bytes=VMEM_LIMIT),
    )(*a16, *b16, *ag, *gab, *gbb)


def _slab_tile(re, im, nsl):
    row = jnp.concatenate([re.reshape(nsl, SLAB_COLS), im.reshape(nsl, SLAB_COLS)], axis=-1)
    return jnp.repeat(row, SUBLANES, axis=0)


def _slab_in_matrix(b_re, b_im, nsl):
    eye = jnp.eye(SLAB_GROUPS, dtype=F32)

    def blk(b):
        b = b.reshape(nsl, SLAB_GROUPS, S5_GROUP, S5_STATE)
        return jnp.einsum('sgcp,gh->sgchp', b, eye).reshape(nsl, LANES, SLAB_COLS)

    return jnp.concatenate([blk(b_re), blk(b_im)], axis=-1)


def _slab_in_extract(m, nsl):
    eye = jnp.eye(SLAB_GROUPS, dtype=F32)

    def ext(x_):
        x_ = x_.reshape(nsl, SLAB_GROUPS, S5_GROUP, SLAB_GROUPS, S5_STATE)
        return jnp.einsum('sgchp,gh->sgcp', x_, eye).reshape(nsl * LANES, S5_STATE)

    return ext(m[..., :SLAB_COLS]), ext(m[..., SLAB_COLS:])


def _slab_out_matrix(c_re, c_im, nsl):
    eye = jnp.eye(SLAB_GROUPS, dtype=F32)

    def blk(c):
        c = c.reshape(nsl, SLAB_GROUPS, S5_GROUP, S5_STATE)
        return jnp.einsum('sgcp,gh->sgphc', c, eye).reshape(nsl, SLAB_COLS, LANES)

    return jnp.concatenate([blk(c_re), -blk(c_im)], axis=1)


def _slab_out_extract(m, nsl):
    eye = jnp.eye(SLAB_GROUPS, dtype=F32)

    def ext(x_):
        x_ = x_.reshape(nsl, SLAB_GROUPS, S5_STATE, SLAB_GROUPS, S5_GROUP)
        return jnp.einsum('sgphc,gh->sgcp', x_, eye).reshape(nsl * SLAB_GROUPS, S5_GROUP, S5_STATE)

    return ext(m[:, :SLAB_COLS]), -ext(m[:, SLAB_COLS:])


def _gelu(y):
    t = jnp.tanh(0.7978845608028654 * (y + 0.044715 * y * y * y))
    return 0.5 * y * (1.0 + t)


def _gelu_grad(y):
    t = jnp.tanh(0.7978845608028654 * (y + 0.044715 * y * y * y))
    return 0.5 * (1.0 + t) + 0.5 * y * (1.0 - t * t) * 0.7978845608028654 * (1.0 + 3.0 * 0.044715 * y * y)


def _scan_rows(ref, n_steps, ar, ai, state, reverse, conj, keep=True):
    sgn = -1.0 if conj else 1.0

    def step(k, carry):
        xr, xi = carry
        t = (n_steps - 1 - k) if reverse else k
        r0 = pl.multiple_of(t * SUBLANES, SUBLANES)
        nr = ar * xr - sgn * ai * xi + ref[pl.ds(r0, SUBLANES), :SLAB_COLS]
        ni = ar * xi + sgn * ai * xr + ref[pl.ds(r0, SUBLANES), SLAB_COLS:]
        if keep:
            ref[pl.ds(r0, SUBLANES), :SLAB_COLS] = nr
            ref[pl.ds(r0, SUBLANES), SLAB_COLS:] = ni
        return nr, ni

    return lax.fori_loop(0, n_steps, step, state, unroll=4)


def _s5_pass1(hp, bblk, ab_tile, rc=1024):
    s, d = hp.shape
    nsl = d // LANES
    rc = min(rc, s)
    nch = s // rc
    w = 2 * SLAB_COLS

    def body(u_ref, b_ref, ab_ref, end_ref, st_ref, x_ref):
        j = pl.program_id(1)

        @pl.when(j == 0)
        def _():
            st_ref[...] = jnp.zeros_like(st_ref)

        x_ref[...] = _dot(_bf(u_ref[...]), b_ref[0], NN)
        xr, xi = _scan_rows(x_ref, rc // SUBLANES, ab_ref[:, :SLAB_COLS], ab_ref[:, SLAB_COLS:],
                            (st_ref[:, :SLAB_COLS], st_ref[:, SLAB_COLS:]), False, False, keep=False)
        st_ref[:, :SLAB_COLS] = xr
        st_ref[:, SLAB_COLS:] = xi

        @pl.when(j == nch - 1)
        def _():
            end_ref[...] = st_ref[...]

    return pl.pallas_call(
        body, name="s5_scan_local", grid=(nsl, nch),
        in_specs=[pl.BlockSpec((rc, LANES), lambda sl, j: (j, sl)), pl.BlockSpec((1, LANES, w), lambda sl, j: (sl, 0, 0)),
                  pl.BlockSpec((SUBLANES, w), lambda sl, j: (sl, 0))],
        out_specs=pl.BlockSpec((SUBLANES, w), lambda sl, j: (sl, 0)),
        out_shape=jax.ShapeDtypeStruct((nsl * SUBLANES, w), F32),
        scratch_shapes=[pltpu.VMEM((SUBLANES, w), F32), pltpu.VMEM((rc, w), F32)],
        compiler_params=_params("parallel", "arbitrary"),
    )(hp, bblk, ab_tile)


def _s5_carry(name, ends, ap_tile, reverse):
    rows, w = ends.shape
    nsl = rows // SUBLANES
    sgn = -1.0 if reverse else 1.0

    def body(e_ref, ap_ref, c_ref):
        pr, pi = ap_ref[0:1, :SLAB_COLS], sgn * ap_ref[0:1, SLAB_COLS:]
        tr = jnp.zeros((1, SLAB_COLS), F32)
        ti = jnp.zeros((1, SLAB_COLS), F32)
        order = range(SUBLANES - 1, -1, -1) if reverse else range(SUBLANES)
        for seg in order:
            c_ref[seg:seg + 1, :SLAB_COLS] = tr
            c_ref[seg:seg + 1, SLAB_COLS:] = ti
            mr, mi = _cmul(pr, pi, tr, ti)
            tr = e_ref[seg:seg + 1, :SLAB_COLS] + mr
            ti = e_ref[seg:seg + 1, SLAB_COLS:] + mi

    spec = pl.BlockSpec((SUBLANES, w), lambda sl: (sl, 0))
    return pl.pallas_call(
        body, name=name, grid=(nsl,), in_specs=[spec, spec], out_specs=spec,
        out_shape=jax.ShapeDtypeStruct((rows, w), F32), compiler_params=_params("parallel"),
    )(ends, ap_tile)


def _s5_pass2(hp, bblk, cin, ab_tile, cblk, dvec, rc=1024):
    s, d = hp.shape
    nsl = d // LANES
    rc = min(rc, s)
    nch = s // rc
    w = 2 * SLAB_COLS

    def body(h_ref, b_ref, cin_ref, ab_ref, c_ref, d_ref, x_ref, y_ref, z_ref, st_ref):
        j = pl.program_id(1)

        @pl.when(j == 0)
        def _():
            st_ref[...] = cin_ref[...]

        hv = h_ref[...]
        x_ref[...] = _dot(_bf(hv), b_ref[0], NN)
        xr, xi = _scan_rows(x_ref, rc // SUBLANES, ab_ref[:, :SLAB_COLS], ab_ref[:, SLAB_COLS:],
                            (st_ref[:, :SLAB_COLS], st_ref[:, SLAB_COLS:]), False, False)
        st_ref[:, :SLAB_COLS] = xr
        st_ref[:, SLAB_COLS:] = xi
        y = _dot(_bf(x_ref[...]), c_ref[0], NN) + d_ref[...] * hv
        y_ref[...] = y
        z_ref[...] = _bf(_gelu(y))

    tile = lambda wd: pl.BlockSpec((rc, wd), lambda sl, j: (j, sl))
    small = pl.BlockSpec((SUBLANES, w), lambda sl, j: (sl, 0))
    return pl.pallas_call(
        body, name="s5_scan_carry_out", grid=(nsl, nch),
        in_specs=[tile(LANES), pl.BlockSpec((1, LANES, w), lambda sl, j: (sl, 0, 0)), small, small,
                  pl.BlockSpec((1, w, LANES), lambda sl, j: (sl, 0, 0)), pl.BlockSpec((1, LANES), lambda sl, j: (0, sl))],
        out_specs=[tile(w), tile(LANES), tile(LANES)],
        out_shape=[jax.ShapeDtypeStruct((s, nsl * w), F32), jax.ShapeDtypeStruct((s, d), F32),
                   jax.ShapeDtypeStruct((s, d), BF16)],
        scratch_shapes=[pltpu.VMEM((SUBLANES, w), F32)],
        compiler_params=_params("parallel", "arbitrary"),
    )(hp, bblk, cin, ab_tile, cblk, dvec)


def _s5_bwd_pass1(dzg, ypre, cblk, ab_tile, hp, rc=1024):
    s, d = hp.shape
    nsl = d // LANES
    rc = min(rc, s)
    nch = s // rc
    w = 2 * SLAB_COLS

    def body(dz_ref, y_ref, c_ref, ab_ref, h_ref, st_out_ref, dy_ref, dd_ref, st_ref, lam_ref):
        j = pl.program_id(1)

        @pl.when(j == 0)
        def _():
            st_ref[...] = jnp.zeros_like(st_ref)
            dd_ref[...] = jnp.zeros_like(dd_ref)

        dy = dz_ref[...] * _gelu_grad(y_ref[...])
        dy_ref[...] = dy
        dd_ref[0:1, :] += jnp.sum(dy * h_ref[...], axis=0, keepdims=True)
        lam_ref[...] = _dot(_bf(dy), c_ref[0], NT)
        lr, li = _scan_rows(lam_ref, rc // SUBLANES, ab_ref[:, :SLAB_COLS], ab_ref[:, SLAB_COLS:],
                            (st_ref[:, :SLAB_COLS], st_ref[:, SLAB_COLS:]), True, True, keep=False)
        st_ref[:, :SLAB_COLS] = lr
        st_ref[:, SLAB_COLS:] = li

        @pl.when(j == nch - 1)
        def _():
            st_out_ref[...] = st_ref[...]

    tile = lambda wd: pl.BlockSpec((rc, wd), lambda sl, j: (nch - 1 - j, sl))
    small = pl.BlockSpec((SUBLANES, w), lambda sl, j: (sl, 0))
    return pl.pallas_call(
        body, name="s5_adjoint_local", grid=(nsl, nch),
        in_specs=[tile(LANES), tile(LANES), pl.BlockSpec((1, w, LANES), lambda sl, j: (sl, 0, 0)), small, tile(LANES)],
        out_specs=[small, tile(LANES), pl.BlockSpec((SUBLANES, LANES), lambda sl, j: (0, sl))],
        out_shape=[jax.ShapeDtypeStruct((nsl * SUBLANES, w), F32),
                   jax.ShapeDtypeStruct((s, d), F32), jax.ShapeDtypeStruct((SUBLANES, d), F32)],
        scratch_shapes=[pltpu.VMEM((SUBLANES, w), F32), pltpu.VMEM((rc, w), F32)],
        compiler_params=_params("parallel", "arbitrary"),
    )(dzg, ypre, cblk, ab_tile, hp)


def _s5_bwd_pass2(dy, cblk, cinl, ab_tile, xtrue, cinx, hp, bblk, dvec, rc=1024):
    s, d = hp.shape
    nsl = d // LANES
    rc = min(rc, s)
    nch = s // rc
    w = 2 * SLAB_COLS
    n_steps = rc // SUBLANES

    def body(dy_ref, c_ref, cl_ref, ab_ref, x_ref, xp_ref, cx_ref, h_ref, b_ref, d_ref,
             du_ref, db_ref, dc_ref, da_ref, st_ref, lam_ref, acc_ref):
        j = pl.program_id(1)

        @pl.when(j == 0)
        def _():
            st_ref[...] = cl_ref[...]
            acc_ref[...] = jnp.zeros_like(acc_ref)
            db_ref[...] = jnp.zeros_like(db_ref)
            dc_ref[...] = jnp.zeros_like(dc_ref)

        ar, ai = ab_ref[:, :SLAB_COLS], ab_ref[:, SLAB_COLS:]
        lam_ref[...] = _dot(_bf(dy_ref[...]), c_ref[0], NT)

        def advance(lr, li, r0):
            nr = ar * lr + ai * li + lam_ref[pl.ds(r0, SUBLANES), :SLAB_COLS]
            ni = ar * li - ai * lr + lam_ref[pl.ds(r0, SUBLANES), SLAB_COLS:]
            lam_ref[pl.ds(r0, SUBLANES), :SLAB_COLS] = nr
            lam_ref[pl.ds(r0, SUBLANES), SLAB_COLS:] = ni
            return nr, ni

        def step(k, carry):
            lr, li, dr, di = carry
            t = n_steps - 1 - k
            nr, ni = advance(lr, li, pl.multiple_of(t * SUBLANES, SUBLANES))
            rx = pl.multiple_of((t - 1) * SUBLANES, SUBLANES)
            xr, xi = x_ref[pl.ds(rx, SUBLANES), :SLAB_COLS], x_ref[pl.ds(rx, SUBLANES), SLAB_COLS:]
            return nr, ni, dr + xr * nr + xi * ni, di + xr * ni - xi * nr

        lr, li, dr, di = lax.fori_loop(
            0, n_steps - 1, step,
            (st_ref[:, :SLAB_COLS], st_ref[:, SLAB_COLS:], acc_ref[:, :SLAB_COLS], acc_ref[:, SLAB_COLS:]), unroll=4)
        lr, li = advance(lr, li, 0)
        st_ref[:, :SLAB_COLS] = lr
        st_ref[:, SLAB_COLS:] = li
        first_chunk = j == nch - 1
        xr = jnp.where(first_chunk, cx_ref[:, :SLAB_COLS], xp_ref[:, :SLAB_COLS])
        xi = jnp.where(first_chunk, cx_ref[:, SLAB_COLS:], xp_ref[:, SLAB_COLS:])
        acc_ref[:, :SLAB_COLS] = dr + xr * lr + xi * li
        acc_ref[:, SLAB_COLS:] = di + xr * li - xi * lr

        lam_b = _bf(lam_ref[...])
        dyv = dy_ref[...]
        db_ref[0] += _dot(_bf(h_ref[...]), lam_b, TN)
        dc_ref[0] += _dot(_bf(dyv), _bf(x_ref[...]), TN)
        du_ref[...] = _dot(lam_b, b_ref[0], NT) + d_ref[...] * dyv

        @pl.when(j == nch - 1)
        def _():
            da_ref[...] = jnp.broadcast_to(jnp.sum(acc_ref[...], axis=0, keepdims=True), (SUBLANES, w))

    sub = rc // SUBLANES
    tile = lambda wd: pl.BlockSpec((rc, wd), lambda sl, j: (nch - 1 - j, sl))
    small = pl.BlockSpec((SUBLANES, w), lambda sl, j: (sl, 0))
    prev = pl.BlockSpec((SUBLANES, w), lambda sl, j: (jnp.maximum((nch - 1 - j) * sub - 1, 0), sl))
    return pl.pallas_call(
        body, name="s5_adjoint_carry_grads", grid=(nsl, nch),
        in_specs=[tile(LANES), pl.BlockSpec((1, w, LANES), lambda sl, j: (sl, 0, 0)), small, small, tile(w), prev, small,
                  tile(LANES), pl.BlockSpec((1, LANES, w), lambda sl, j: (sl, 0, 0)),
                  pl.BlockSpec((1, LANES), lambda sl, j: (0, sl))],
        out_specs=[tile(LANES), pl.BlockSpec((1, LANES, w), lambda sl, j: (sl, 0, 0)),
                   pl.BlockSpec((1, LANES, w), lambda sl, j: (sl, 0, 0)), small],
        out_shape=[jax.ShapeDtypeStruct((s, d), F32), jax.ShapeDtypeStruct((nsl, LANES, w), F32),
                   jax.ShapeDtypeStruct((nsl, LANES, w), F32), jax.ShapeDtypeStruct((nsl * SUBLANES, w), F32)],
        scratch_shapes=[pltpu.VMEM((SUBLANES, w), F32), pltpu.VMEM((rc, w), F32), pltpu.VMEM((SUBLANES, w), F32)],
        compiler_params=_params("parallel", "arbitrary"),
    )(dy, cblk, cinl, ab_tile, xtrue, xtrue, cinx, hp, bblk, dvec)


def _adamw(name, w, g, m, v):
    r, c = w.shape
    tile = r if r * c <= 512 * 1024 else _pick(r, max(SUBLANES, (512 * 1024 // c) // SUBLANES * SUBLANES), q=SUBLANES)
    c1 = 1.0 / (1.0 - ADAM_B1 ** ADAM_STEP)
    c2 = 1.0 / (1.0 - ADAM_B2 ** ADAM_STEP)

    def body(w_ref, g_ref, m_ref, v_ref, d_ref, nm_ref, nv_ref):
        gg = g_ref[...]
        nm = ADAM_B1 * m_ref[...] + (1.0 - ADAM_B1) * gg
        nv = ADAM_B2 * v_ref[...] + (1.0 - ADAM_B2) * gg * gg
        d_ref[...] = -ADAM_LR * ((nm * c1) / (jnp.sqrt(nv * c2) + ADAM_EPS) + ADAM_WD * w_ref[...])
        nm_ref[...] = nm
        nv_ref[...] = nv

    spec = _nat(tile, c)
    return pl.pallas_call(
        body, name=name, grid=(r // tile,), in_specs=[spec] * 4, out_specs=[spec] * 3,
        out_shape=[jax.ShapeDtypeStruct((r, c), F32)] * 3, compiler_params=_params("parallel"),
    )(w, g, m, v)


def _place():
    x, y, c = lax.axis_index("x"), lax.axis_index("y"), lax.axis_index("c")
    return x, y, c, [(1 - x, y), (x, 1 - y), (1 - x, 1 - y)]


_ANY = pl.BlockSpec(memory_space=pl.ANY)


def _gather_weights(shards):
    n = len(shards)

    def body(*refs):
        ins, outs = refs[:n], refs[n:2 * n]
        send_sems, recv_sems, local_sems = refs[2 * n:]
        x, y, c, chips = _place()
        me = 2 * x + y
        sibling = (x, y, 1 - c)
        started = []
        for a in range(n):
            local = pltpu.make_async_copy(ins[a], outs[a].at[me], local_sems.at[a])
            local.start()
            started.append(local)

        def half(a, chip, h):
            hw = ins[a].shape[1] // 2
            return outs[a].at[chip, :, pl.ds(pl.multiple_of(h * hw, LANES), hw)]

        def copy(a, k, src, chip, h, to):
            return pltpu.make_async_remote_copy(
                src_ref=src, dst_ref=half(a, chip, h), send_sem=send_sems.at[a, k], recv_sem=recv_sems.at[a, k],
                device_id=to, device_id_type=MESH)

        sends = []
        for a in range(n):
            hw = ins[a].shape[1] // 2
            mine = ins[a].at[:, pl.ds(pl.multiple_of(c * hw, LANES), hw)]
            for k, chip in enumerate(chips):
                cp = copy(a, k, mine, me, c, (*chip, c))
                cp.start()
                sends.append(cp)
        for a in range(n):
            for k, (cx, cy) in enumerate(chips):
                src_chip = 2 * cx + cy
                copy(a, k, half(a, src_chip, c), src_chip, c, (x, y, c)).wait_recv()
                fwd = copy(a, 3 + k, half(a, src_chip, c), src_chip, c, sibling)
                fwd.start()
                sends.append(fwd)
        for a in range(n):
            for k, (cx, cy) in enumerate(chips):
                src_chip = 2 * cx + cy
                copy(a, 3 + k, half(a, src_chip, 1 - c), src_chip, 1 - c, (x, y, c)).wait_recv()
        for cp in sends:
            cp.wait_send()
        for cp in started:
            cp.wait()

    return pl.pallas_call(
        body, name="gather_weights",
        in_specs=[_ANY] * n, out_specs=[_ANY] * n,
        out_shape=[jax.ShapeDtypeStruct((N_CHIPS,) + s_.shape, s_.dtype) for s_ in shards],
        scratch_shapes=[pltpu.SemaphoreType.DMA((n, 6)), pltpu.SemaphoreType.DMA((n, 6)), pltpu.SemaphoreType.DMA((n,))],

    )(*shards)


def _gather_weights_async(shards):
    n = len(shards)
    srcs = [jax.new_ref(s_, memory_space=pltpu.MemorySpace.HBM) for s_ in shards]
    outs = [jax.empty_ref(jax.ShapeDtypeStruct((N_CHIPS,) + s_.shape, s_.dtype), memory_space=pltpu.MemorySpace.HBM)
            for s_ in shards]

    @pl.kernel(mesh=plsc.ScalarSubcoreMesh(axis_name="seq", num_cores=1), name="gather_weights_async",
               scratch_types=(pltpu.SemaphoreType.DMA((n, 6)), pltpu.SemaphoreType.DMA((n, 6)),
                              pltpu.SemaphoreType.DMA((n,))),
               compiler_params=pltpu.CompilerParams(collective_id=1))
    def launch(send_sems, recv_sems, local_sems):
        x, y, c, chips = _place()
        me = 2 * x + y
        sibling = (x, y, 1 - c)
        barrier = pltpu.get_barrier_semaphore()
        for peer in [sibling] + [(*chip, c) for chip in chips]:
            pl.semaphore_signal(barrier, inc=1, device_id=peer, device_id_type=MESH)
        pl.semaphore_wait(barrier, 4)

        def half(a, chip, h):
            hw = srcs[a].shape[1] // 2
            return outs[a].at[chip, :, pl.ds(pl.multiple_of(h * hw, LANES), hw)]

        def copy(a, k, src, chip, h, to):
            return pltpu.make_async_remote_copy(
                src_ref=src, dst_ref=half(a, chip, h), send_sem=send_sems.at[a, k], recv_sem=recv_sems.at[a, k],
                device_id=to, device_id_type=MESH)

        locals_, sends = [], []
        for a in range(n):
            local = pltpu.make_async_copy(srcs[a], outs[a].at[me], local_sems.at[a])
            local.start()
            locals_.append(local)
            hw = srcs[a].shape[1] // 2
            mine = srcs[a].at[:, pl.ds(pl.multiple_of(c * hw, LANES), hw)]
            for k, chip in enumerate(chips):
                cp = copy(a, k, mine, me, c, (*chip, c))
                cp.start()
                sends.append(cp)
        for a in range(n):
            for k, (cx, cy) in enumerate(chips):
                src_chip = 2 * cx + cy
                copy(a, k, half(a, src_chip, c), src_chip, c, (x, y, c)).wait_recv()
                fwd = copy(a, 3 + k, half(a, src_chip, c), src_chip, c, sibling)
                fwd.start()
                sends.append(fwd)
        for a in range(n):
            for k, (cx, cy) in enumerate(chips):
                src_chip = 2 * cx + cy
                copy(a, 3 + k, half(a, src_chip, 1 - c), src_chip, 1 - c, (x, y, c)).wait_recv()
        for cp in sends:
            cp.wait_send()
        for cp in locals_:
            cp.wait()

    launch()
    return [o[...] for o in outs]


def _on_sequencer(name, cid, inputs, out_shapes, sem_types, peers, body):
    srcs = [jax.new_ref(a, memory_space=pltpu.MemorySpace.HBM) for a in inputs]
    outs = [jax.empty_ref(sd, memory_space=pltpu.MemorySpace.HBM) for sd in out_shapes]

    @pl.kernel(mesh=plsc.ScalarSubcoreMesh(axis_name="seq", num_cores=1), name=name, scratch_types=tuple(sem_types),
               compiler_params=pltpu.CompilerParams(collective_id=cid))
    def launch(*sems):
        x, y, c, chips = _place()
        barrier = pltpu.get_barrier_semaphore()
        ps = peers(x, y, c, chips)
        for peer in ps:
            pl.semaphore_signal(barrier, inc=1, device_id=peer, device_id_type=MESH)
        pl.semaphore_wait(barrier, len(ps))
        body(srcs, outs, *sems)

    launch()
    return [o[...] for o in outs]


def _sibling_only(x, y, c, chips):
    return [(x, y, 1 - c)]


def _same_core_of_other_chips(x, y, c, chips):
    return [(*chip, c) for chip in chips]


def _swap_halves_to_sibling(name, cid, grads):
    n = len(grads)

    def body(ins, outs, send_sems, recv_sems):
        x, y, c, _ = _place()
        cps = []
        for a in range(n):
            hw = ins[a].shape[2] // 2
            src = ins[a].at[:, :, pl.ds(pl.multiple_of((1 - c) * hw, LANES), hw)]
            cp = pltpu.make_async_remote_copy(src_ref=src, dst_ref=outs[a], send_sem=send_sems.at[a],
                                              recv_sem=recv_sems.at[a], device_id=(x, y, 1 - c), device_id_type=MESH)
            cp.start()
            cps.append(cp)
        for cp in cps:
            cp.wait()

    return _on_sequencer(
        name, cid, grads, [jax.ShapeDtypeStruct(g.shape[:2] + (g.shape[2] // 2,), g.dtype) for g in grads],
        [pltpu.SemaphoreType.DMA((n,)), pltpu.SemaphoreType.DMA((n,))], _sibling_only, body)


def _exchange_quarters(name, cid, parts):
    n = len(parts)

    def body(ins, outs, send_sems, recv_sems):
        x, y, c, chips = _place()
        cps = []
        for a in range(n):
            for k, (cx, cy) in enumerate(chips):
                cp = pltpu.make_async_remote_copy(
                    src_ref=ins[a].at[2 * cx + cy], dst_ref=outs[a].at[k], send_sem=send_sems.at[a, k],
                    recv_sem=recv_sems.at[a, k], device_id=(cx, cy, c), device_id_type=MESH)
                cp.start()
                cps.append(cp)
        for cp in cps:
            cp.wait()

    return _on_sequencer(
        name, cid, parts, [jax.ShapeDtypeStruct((3,) + p_.shape[1:], p_.dtype) for p_ in parts],
        [pltpu.SemaphoreType.DMA((n, 3)), pltpu.SemaphoreType.DMA((n, 3))], _same_core_of_other_chips, body)


def _swap_final_halves(name, cid, halves):
    n = len(halves)

    def body(ins, outs, send_sems, recv_sems):
        x, y, c, _ = _place()
        cps = []
        for a in range(n):
            cp = pltpu.make_async_remote_copy(src_ref=ins[a], dst_ref=outs[a], send_sem=send_sems.at[a],
                                              recv_sem=recv_sems.at[a], device_id=(x, y, 1 - c), device_id_type=MESH)
            cp.start()
            cps.append(cp)
        for cp in cps:
            cp.wait()

    return _on_sequencer(
        name, cid, halves, [jax.ShapeDtypeStruct(h.shape, h.dtype) for h in halves],
        [pltpu.SemaphoreType.DMA((n,)), pltpu.SemaphoreType.DMA((n,))], _sibling_only, body)


def _add_half(name, grad, recv):
    nchip, r, cfull = grad.shape
    hw = cfull // 2
    tile = _pick(r, max(BF16_ROWS, (256 * 1024 // hw) // BF16_ROWS * BF16_ROWS), q=BF16_ROWS)
    c = lax.axis_index("c")

    def body(c_ref, g_ref, r_ref, o_ref):
        o_ref[...] = _bf(g_ref[...] + r_ref[...])

    return pl.pallas_call(
        body, name=name,
        grid_spec=pltpu.PrefetchScalarGridSpec(
            num_scalar_prefetch=1, grid=(nchip, r // tile),
            in_specs=[pl.BlockSpec((1, tile, hw), lambda k, i, cr: (k, i, cr[0])),
                      pl.BlockSpec((1, tile, hw), lambda k, i, cr: (k, i, 0))],
            out_specs=pl.BlockSpec((1, tile, hw), lambda k, i, cr: (k, i, 0))),
        out_shape=jax.ShapeDtypeStruct((nchip, r, hw), BF16), compiler_params=_params("parallel", "parallel"),
    )(c.reshape(1).astype(jnp.int32), grad, recv)


def _add_quarters(name, part, recv):
    _, r, hw = part.shape
    tile = _pick(r, max(BF16_ROWS, (256 * 1024 // hw) // BF16_ROWS * BF16_ROWS), q=BF16_ROWS)
    me = 2 * lax.axis_index("x") + lax.axis_index("y")

    def body(me_ref, p_ref, r_ref, o_ref):
        f = lambda v: v.astype(F32)
        o_ref[...] = ((f(p_ref[0]) + f(r_ref[0])) + f(r_ref[1])) + f(r_ref[2])

    return pl.pallas_call(
        body, name=name,
        grid_spec=pltpu.PrefetchScalarGridSpec(
            num_scalar_prefetch=1, grid=(r // tile,),
            in_specs=[pl.BlockSpec((1, tile, hw), lambda i, mr: (mr[0], i, 0)),
                      pl.BlockSpec((3, tile, hw), lambda i, mr: (0, i, 0))],
            out_specs=pl.BlockSpec((tile, hw), lambda i, mr: (i, 0))),
        out_shape=jax.ShapeDtypeStruct((r, hw), F32), compiler_params=_params("parallel"),
    )(me.reshape(1).astype(jnp.int32), part, recv)


class _ReduceScatter:
    def __init__(self, tag, first_cid, grads):
        self.tag, self.cid = tag, first_cid
        self.stacks = [g.reshape(N_CHIPS, g.shape[0] // N_CHIPS, g.shape[1]) for g in grads]

    def start(self, anchor):
        self.stacks, anchor = lax.optimization_barrier((self.stacks, anchor))
        self.recv = _swap_halves_to_sibling(f"rs_swap_halves_{self.tag}", self.cid, self.stacks)
        return anchor

    def exchange(self, anchor):
        parts = [_add_half(f"rs_add_half_{self.tag}{a}", g, r) for a, (g, r) in enumerate(zip(self.stacks, self.recv))]
        self.parts, anchor = lax.optimization_barrier((parts, anchor))
        self.quarters = _exchange_quarters(f"rs_exchange_{self.tag}", self.cid + 1, self.parts)
        return anchor

    def join(self, anchor):
        halves = [_add_quarters(f"rs_add_quarters_{self.tag}{a}", p_, q_)
                  for a, (p_, q_) in enumerate(zip(self.parts, self.quarters))]
        self.halves, anchor = lax.optimization_barrier((halves, anchor))
        self.others = _swap_final_halves(f"rs_swap_final_{self.tag}", self.cid + 2, self.halves)
        return anchor

    def result(self):
        south = lax.axis_index("c") == 0
        return [jnp.concatenate([jnp.where(south, h, o), jnp.where(south, o, h)], axis=1)
                for h, o in zip(self.halves, self.others)]


def _allgather_small(pack):
    m_per, n = pack.shape

    def body(x_ref, out_ref, send_sems, recv_sems, local_sem):
        x, y, c, chips = _place()
        me, sibling = (x, y, c), (x, y, 1 - c)

        def rows(px, py, pc):
            return out_ref.at[pl.ds(pl.multiple_of((4 * px + 2 * py + pc) * m_per, SUBLANES), m_per), :]

        def copy(k, block, to, src=None):
            return pltpu.make_async_remote_copy(
                src_ref=rows(*block) if src is None else src, dst_ref=rows(*block),
                send_sem=send_sems.at[k], recv_sem=recv_sems.at[k], device_id=to, device_id_type=MESH)

        mine = pltpu.make_async_copy(x_ref, rows(*me), local_sem)
        mine.start()
        first = [copy(0, me, sibling, src=x_ref)]
        first += [copy(1 + j, me, (*chip, c), src=x_ref) for j, chip in enumerate(chips)]
        for cp in first:
            cp.start()
        passed = [copy(4 + j, (*chip, c), sibling) for j, chip in enumerate(chips)]
        for j, chip in enumerate(chips):
            copy(1 + j, (*chip, c), me).wait_recv()
            passed[j].start()
        copy(0, sibling, me).wait_recv()
        for j, chip in enumerate(chips):
            copy(4 + j, (*chip, 1 - c), me).wait_recv()
        for cp in first + passed:
            cp.wait_send()
        mine.wait()

    return pl.pallas_call(
        body, name="allgather_small_grads",
        out_shape=jax.ShapeDtypeStruct((N_DEV * m_per, n), pack.dtype),
        in_specs=[pl.BlockSpec(memory_space=pltpu.VMEM)], out_specs=pl.BlockSpec(memory_space=pltpu.VMEM),
        scratch_shapes=[pltpu.SemaphoreType.DMA((7,)), pltpu.SemaphoreType.DMA((7,)), pltpu.SemaphoreType.DMA],
        compiler_params=pltpu.CompilerParams(vmem_limit_bytes=VMEM_LIMIT),
    )(pack)


def _sum_devices(packs, m_per):
    tile = _pick(m_per, 512, q=SUBLANES)
    nt = m_per // tile

    def body(*refs):
        acc = refs[0][...]
        for r in refs[1:N_DEV]:
            acc = acc + r[...]
        refs[N_DEV][...] = acc

    return pl.pallas_call(
        body, name="sum_small_grads", grid=(nt,),
        in_specs=[pl.BlockSpec((tile, LANES), functools.partial(lambda i, k: (k * nt + i, 0), k=k)) for k in range(N_DEV)],
        out_specs=_nat(tile, LANES), out_shape=jax.ShapeDtypeStruct((m_per, LANES), F32),
        compiler_params=_params("parallel"),
    )(*([packs] * N_DEV))


def _tail_fwd(tag, alpha, h_in, adds, mix_gate, ln1, ln2, p_l, w, want_perm):
    h_mid, xh1, rs1, h_mid_b, _ = _ln_fwd(f"ln1_fwd_{tag}", alpha, h_in, adds, mix_gate, *ln1)
    gp = _matmul(f"ple_gate_fwd_{tag}", h_mid_b, w['wg'], 'nn')
    pw = _matmul(f"ple_proj_fwd_{tag}", p_l, w['plet'], 'nt')
    gu = _matmul(f"ffn_in_fwd_{tag}", h_mid_b, w['wit'], 'nt', tn=1408)
    act = _swiglu_fwd(f"swiglu_fwd_{tag}", gu)
    ffn = _matmul(f"ffn_out_fwd_{tag}", act, w['wo'], 'nn', tk=2816)
    h_out, xh2, rs2, _, h_perm = _ln_fwd(f"ln2_fwd_{tag}", alpha, h_mid, [(ffn, 'nat')],
                                         ('nat', (pw, 1, 0), (gp, 1, 0)), *ln2, want_perm=want_perm)
    saved = dict(h_mid_b=h_mid_b, xh1=xh1, rs1=rs1, gp=gp, pw=pw, gu=gu, act=act, xh2=xh2, rs2=rs2)
    return h_out, h_perm, saved


def _tail_bwd(tag, alpha, dparts, sv, ln1_g, ln2_g, p_l, w, mix_gate):
    d = sv['h_mid_b'].shape[1]
    dz2, dz2b, dgate, dg2, db2 = _ln_bwd(f"ln2_bwd_{tag}", dparts, sv['xh2'], sv['rs2'], ln2_g,
                                         gate=('nat', (sv['pw'], 1, 0), (sv['gp'], 1, 0)))
    grads = dict(ln2_g=dg2, ln2_b=db2)
    grads['plet'] = _matmul(f"ple_proj_dw_{tag}", dgate, p_l, 'tn', a_win=(0, d))
    grads['wg'] = _matmul(f"ple_gate_dw_{tag}", sv['h_mid_b'], dgate, 'tn', b_win=(d, d))
    dx_gate = _matmul(f"ple_gate_dx_{tag}", dgate, w['wg'], 'nt', a_win=(d, d))
    dact = _matmul(f"ffn_out_dx_{tag}", dz2b, w['wo'], 'nt', out_dtype=BF16, tn=1408)
    grads['wo'] = _matmul(f"ffn_out_dw_{tag}", sv['act'], dz2b, 'tn', tm=1408)
    dgu = _swiglu_bwd(f"swiglu_bwd_{tag}", sv['gu'], dact)
    grads['wit'] = _matmul(f"ffn_in_dw_{tag}", dgu, sv['h_mid_b'], 'tn')
    dx_ffn = _matmul(f"ffn_in_dx_{tag}", dgu, w['wit'], 'nn', tk=2816)
    res = _ln_bwd(f"ln1_bwd_{tag}", [(dz2, 'nat', alpha), (dx_gate, 'nat', 1.0), (dx_ffn, 'nat', 1.0)],
                  sv['xh1'], sv['rs1'], ln1_g, gate=mix_gate)
    grads['ln1_g'], grads['ln1_b'] = res[-2], res[-1]
    return res[:-2], grads


def kernel(x, p, positions, attn_w_in, mla_q_norm, mla_w_q_b, mla_kv_norm, mla_w_kv_b, attn_w_out, s5_a_re, s5_a_im, s5_log_dt, s5_b_re, s5_b_im, s5_c_re, s5_c_im, s5_d, s5_w_glu, ln1_g, ln1_b, ffn_w_in, ffn_w_out, ple_w, ple_gate_w, ln2_g, ln2_b, loss_target, m_attn_w_in, m_mla_q_norm, m_mla_w_q_b, m_mla_kv_norm, m_mla_w_kv_b, m_attn_w_out, m_s5_a_re, m_s5_a_im, m_s5_log_dt, m_s5_b_re, m_s5_b_im, m_s5_c_re, m_s5_c_im, m_s5_d, m_s5_w_glu, m_ln1_g, m_ln1_b, m_ffn_w_in, m_ffn_w_out, m_ple_w, m_ple_gate_w, m_ln2_g, m_ln2_b, v_attn_w_in, v_mla_q_norm, v_mla_w_q_b, v_mla_kv_norm, v_mla_w_kv_b, v_attn_w_out, v_s5_a_re, v_s5_a_im, v_s5_log_dt, v_s5_b_re, v_s5_b_im, v_s5_c_re, v_s5_c_im, v_s5_d, v_s5_w_glu, v_ln1_g, v_ln1_b, v_ffn_w_in, v_ffn_w_out, v_ple_w, v_ple_gate_w, v_ln2_g, v_ln2_b):
    weights = dict(attn_w_in=attn_w_in, mla_q_norm=mla_q_norm, mla_w_q_b=mla_w_q_b, mla_kv_norm=mla_kv_norm,
                   mla_w_kv_b=mla_w_kv_b, attn_w_out=attn_w_out, s5_a_re=s5_a_re, s5_a_im=s5_a_im, s5_log_dt=s5_log_dt,
                   s5_b_re=s5_b_re, s5_b_im=s5_b_im, s5_c_re=s5_c_re, s5_c_im=s5_c_im, s5_d=s5_d, s5_w_glu=s5_w_glu,
                   ln1_g=ln1_g, ln1_b=ln1_b, ffn_w_in=ffn_w_in, ffn_w_out=ffn_w_out, ple_w=ple_w, ple_gate_w=ple_gate_w,
                   ln2_g=ln2_g, ln2_b=ln2_b)
    m_in = dict(attn_w_in=m_attn_w_in, mla_q_norm=m_mla_q_norm, mla_w_q_b=m_mla_w_q_b, mla_kv_norm=m_mla_kv_norm,
                mla_w_kv_b=m_mla_w_kv_b, attn_w_out=m_attn_w_out, s5_a_re=m_s5_a_re, s5_a_im=m_s5_a_im,
                s5_log_dt=m_s5_log_dt, s5_b_re=m_s5_b_re, s5_b_im=m_s5_b_im, s5_c_re=m_s5_c_re, s5_c_im=m_s5_c_im,
                s5_d=m_s5_d, s5_w_glu=m_s5_w_glu, ln1_g=m_ln1_g, ln1_b=m_ln1_b, ffn_w_in=m_ffn_w_in,
                ffn_w_out=m_ffn_w_out, ple_w=m_ple_w, ple_gate_w=m_ple_gate_w, ln2_g=m_ln2_g, ln2_b=m_ln2_b)
    v_in = dict(attn_w_in=v_attn_w_in, mla_q_norm=v_mla_q_norm, mla_w_q_b=v_mla_w_q_b, mla_kv_norm=v_mla_kv_norm,
                mla_w_kv_b=v_mla_w_kv_b, attn_w_out=v_attn_w_out, s5_a_re=v_s5_a_re, s5_a_im=v_s5_a_im,
                s5_log_dt=v_s5_log_dt, s5_b_re=v_s5_b_re, s5_b_im=v_s5_b_im, s5_c_re=v_s5_c_re, s5_c_im=v_s5_c_im,
                s5_d=v_s5_d, s5_w_glu=v_s5_w_glu, ln1_g=v_ln1_g, ln1_b=v_ln1_b, ffn_w_in=v_ffn_w_in,
                ffn_w_out=v_ffn_w_out, ple_w=v_ple_w, ple_gate_w=v_ple_gate_w, ln2_g=v_ln2_g, ln2_b=v_ln2_b)
    names = list(weights)

    s, d = x.shape[1], x.shape[2]
    depth = ln1_g.shape[0]
    assert depth == 2
    alpha = (2.0 * depth) ** 0.25
    ql, kvl = mla_q_norm.shape[1], mla_kv_norm.shape[1]
    in_cols = N_CHIPS * attn_w_in.shape[2]
    heads = N_CHIPS * mla_w_q_b.shape[2] // (NOPE + ROPE)
    hps = heads // N_CHIPS
    dw = (in_cols - ql - kvl - ROPE) // 3
    dh = dw // DHD
    assert ql % LANES == 0 and kvl == ql and dw % DHD == 0 and heads % N_CHIPS == 0
    ngroups, nstate = s5_a_re.shape[1], s5_a_re.shape[2]
    assert nstate == S5_STATE and ngroups * S5_GROUP == d and d % LANES == 0
    nsl = d // LANES
    seg_len = s // SUBLANES
    n_sq = seg_len.bit_length() - 1
    assert 1 << n_sq == seg_len, "the segment length of the S5 scan must be a power of two"
    for window, dil in DIL_BRANCHES:
        assert window // dil == DIL_STEPS and (s // dil) % DIL_STEPS == 0
    me = 2 * lax.axis_index("x") + lax.axis_index("y")

    xb = x[0]
    target = loss_target[0]
    p_layers = [p[0, 0], p[1, 0]]
    pos = positions[0].astype(F32).reshape(s, 1)
    inv_freq = ROPE_THETA ** (-jnp.arange(ROPE // 2, dtype=F32) / (ROPE // 2))
    invf = jnp.concatenate([inv_freq, inv_freq, jnp.zeros((LANES - ROPE,), F32)]).reshape(1, LANES)
    slopes = 2.0 ** (-8.0 * jnp.arange(1, dh + 1, dtype=F32) / dh)
    slopes = jnp.broadcast_to(jnp.repeat(slopes, SUBLANES)[:, None], (dh * SUBLANES, LANES))

    wqb_t = mla_w_q_b[0].T.reshape(hps, NOPE + ROPE, ql)
    wqb_t = jnp.pad(wqb_t, ((0, 0), (0, QK_PAD - NOPE - ROPE), (0, 0))).reshape(hps * QK_PAD, ql)
    d_cols = max(d // N_CHIPS, 2 * LANES)
    d_pad = jnp.zeros((SUBLANES, d_cols), F32).at[0, :d // N_CHIPS].set(s5_d[0])
    shards = [_bf(attn_w_in[0].T), _bf(wqb_t), _bf(mla_w_kv_b[0].T), _bf(attn_w_out[0]), _bf(s5_w_glu[0].T)]
    for l in range(depth):
        shards += [_bf(ffn_w_in[l].T), _bf(ffn_w_out[l]), _bf(ple_w[l].T), _bf(ple_gate_w[l])]
    shards.append(d_pad)
    first, later = lax.optimization_barrier((list(_gather_weights(shards[:3])), shards[3:]))
    gathered = first + _gather_weights_async(later)
    full = [g.reshape(N_CHIPS * g.shape[1], g.shape[2]) for g in gathered]
    win_t, wqb_t_f, wkv_t, wout, wglu_t = full[:5]
    lw = [dict(wit=full[5 + 4 * l], wo=full[6 + 4 * l], plet=full[7 + 4 * l], wg=full[8 + 4 * l]) for l in range(depth)]
    dvec = full[-1].reshape(N_CHIPS, SUBLANES, d_cols)[:, 0, :d // N_CHIPS].reshape(1, d)
    lat = ql + kvl
    win_t = jnp.concatenate([win_t[:lat + ROPE], jnp.zeros((LANES - ROPE, d), BF16), win_t[lat + ROPE:]], axis=0)
    kpe_cb = lat // LANES
    q_cb = kpe_cb + 1
    a_cb = heads * VDIM // LANES

    xbb = _bf(xb)
    proj = _matmul("attn_in_fwd", xbb, win_t, 'nt', tn=1408)
    nrm = _rms_fwd(proj, ql, kvl, mla_q_norm[0], mla_kv_norm[0])
    q_raw = _matmul("mla_q_up_fwd", nrm, wqb_t_f, 'nt', a_win=(0, ql))
    kv = _matmul("mla_kv_up_fwd", nrm, wkv_t, 'nt', a_win=(ql, kvl))
    qf, kf, vv = _rope_prep(q_raw, kv, proj, kpe_cb, pos, invf, heads)
    out_a, lse_a = _mla_fwd(qf, kf, vv.T, heads)
    out_b, lse_b = _dil_fused_fwd(proj, slopes, dh, q_cb)
    att = _concat_bf16("attn_heads_concat", out_a, out_b)
    mix0 = _matmul("attn_out_fwd", att, wout, 'nn')
    h2, h2p, sv0 = _tail_fwd("l0", alpha, xb, [(mix0, 'nat')], None, (ln1_g[0], ln1_b[0]), (ln2_g[0], ln2_b[0]),
                             p_layers[0], lw[0], want_perm=True)

    rep = lambda a: jnp.repeat(a, S5_GROUP, axis=0)
    ag = (s5_a_re[0], s5_a_im[0], jnp.broadcast_to(s5_log_dt[0][:, None], (ngroups, nstate)))
    a16 = tuple(rep(a) for a in ag)
    b16 = tuple(b[0].transpose(0, 2, 1).reshape(ngroups * S5_GROUP, nstate) for b in (s5_b_re, s5_b_im))
    abr, abi, apr, api, bbr, bbi = _s5_discretise(*a16, *b16, n_sq)
    ab_tile = _slab_tile(abr[::S5_GROUP], abi[::S5_GROUP], nsl)
    ap_tile = _slab_tile(apr[::S5_GROUP], api[::S5_GROUP], nsl)
    bblk = _bf(_slab_in_matrix(bbr.reshape(ngroups, S5_GROUP, nstate), bbi.reshape(ngroups, S5_GROUP, nstate), nsl))
    cblk = _bf(_slab_out_matrix(s5_c_re[0], s5_c_im[0], nsl))
    ends = _s5_pass1(h2p, bblk, ab_tile)
    cinx = _s5_carry("s5_carry_fwd", ends, ap_tile, False)
    xtrue, ypre, zg = _s5_pass2(h2p, bblk, cinx, ab_tile, cblk, dvec)
    vg = _matmul("s5_glu_fwd", zg, wglu_t, 'nt')
    glu_gate = ('perm', (vg, 2, 0), (vg, 2, 1))
    h4, _, sv1 = _tail_fwd("l1", alpha, h2, [], glu_gate, (ln1_g[1], ln1_b[1]), (ln2_g[1], ln2_b[1]),
                           p_layers[1], lw[1], want_perm=False)
    loss = lax.psum(jnp.sum(_loss_partial(h4, target)), ("x", "y", "c"))

    (dz1_1, _, dvg), g1 = _tail_bwd("l1", alpha, [(h4, 'nat', 1.0 / d), (target, 'nat', -1.0 / d)], sv1, ln1_g[1],
                                    ln2_g[1], p_layers[1], lw[1], glu_gate)
    d_wglu_t = _matmul("s5_glu_dw", dvg, zg, 'tn')
    dzg = _matmul("s5_glu_dx", dvg, wglu_t, 'nn')
    rs_l1 = _ReduceScatter("l1", 2, [d_wglu_t, g1['wit'], g1['wo'], g1['plet'], g1['wg']])
    dzg = rs_l1.start(dzg)
    starts, dy, dd = _s5_bwd_pass1(dzg, ypre, cblk, ab_tile, h2p)
    cinl = _s5_carry("s5_carry_bwd", starts, ap_tile, True)
    du_p, d_bblk, d_cblk, d_ab = _s5_bwd_pass2(dy, cblk, cinl, ab_tile, xtrue, cinx, h2p, bblk, dvec)
    gbb = _slab_in_extract(d_bblk, nsl)
    g_c_re, g_c_im = _slab_out_extract(jnp.swapaxes(d_cblk, 1, 2), nsl)
    d_ab = d_ab[::SUBLANES]
    gab = (d_ab[:, :SLAB_COLS].reshape(ngroups, nstate), d_ab[:, SLAB_COLS:].reshape(ngroups, nstate))
    g_a_re, g_a_im, g_log_dt, g_b_re, g_b_im = _s5_discretise_bwd(a16, b16, ag, gab, gbb)
    unt = lambda b: b.reshape(ngroups, S5_GROUP, nstate).transpose(0, 2, 1)

    du_p = rs_l1.exchange(du_p)
    (dz1_0, dz1_0b), g0 = _tail_bwd("l0", alpha, [(dz1_1, 'nat', alpha), (du_p, 'perm', 1.0)], sv0, ln1_g[0], ln2_g[0],
                                    p_layers[0], lw[0], None)
    dz1_0b = rs_l1.join(dz1_0b)
    rs_l0 = _ReduceScatter("l0", 5, [g0['wit'], g0['wo'], g0['plet'], g0['wg']])
    dz1_0b = rs_l0.start(dz1_0b)
    datt = _matmul("attn_out_dx", dz1_0b, wout, 'nt')
    d_wout = _matmul("attn_out_dw", att, dz1_0b, 'tn')
    do, delta, delta_t = _attn_bwd_prep(datt, out_a, out_b)
    dqf, dkf, dvv = _mla_bwd(qf, kf, vv, do, lse_a, delta_t, heads, 0)
    dqf = rs_l0.exchange(dqf)
    dq_raw, dkv, dkpe = _rope_unprep(dqf, dkf, dvv, pos, invf, heads)
    d_wqb_t = _matmul("mla_q_up_dw", dq_raw, nrm, 'tn', b_win=(0, ql))
    d_wkv_t = _matmul("mla_kv_up_dw", dkv, nrm, 'tn', b_win=(ql, kvl))
    dnq = _matmul("mla_q_up_dx", dq_raw, wqb_t_f, 'nn')
    dnkv = _matmul("mla_kv_up_dx", dkv, wkv_t, 'nn')
    dqd, dkd, dvd = _dil_fused_bwd(proj, slopes, datt, lse_b, delta, dh, q_cb, a_cb)
    dkpe = rs_l0.join(dkpe)
    dproj, g_gq, g_gkv = _dproj_assemble(proj, dnq, dnkv, dkpe, [dqd], [dkd], [dvd], mla_q_norm[0], mla_kv_norm[0], ql)
    d_win_t = _matmul("attn_in_dw", dproj, xbb, 'tn', tm=1408)
    dx_attn = _matmul("attn_in_dx", dproj, win_t, 'nn')
    grad_x = _axpy("grad_x", alpha, dz1_0, dx_attn)

    d_win_t = jnp.concatenate([d_win_t[:lat + ROPE], d_win_t[lat + LANES:]], axis=0)
    rs_at = _ReduceScatter("attn", 8, [d_win_t, d_wqb_t, d_wkv_t, d_wout])
    r_wglu, r_wit1, r_wo1, r_plet1, r_wg1 = rs_l1.result()
    r_wit0, r_wo0, r_plet0, r_wg0 = rs_l0.result()
    r_wit0 = rs_at.start(r_wit0)
    r_wit1 = rs_at.exchange(r_wit1)
    r_wo0 = rs_at.join(r_wo0)
    r_win, r_wqb, r_wkv, r_wout = rs_at.result()
    r_wqb = r_wqb.reshape(hps, QK_PAD, ql)[:, :NOPE + ROPE].reshape(hps * (NOPE + ROPE), ql)
    grads = dict(attn_w_in=r_win.T[None], mla_w_q_b=r_wqb.T[None], mla_w_kv_b=r_wkv.T[None], attn_w_out=r_wout[None],
                 s5_w_glu=r_wglu.T[None],
                 ffn_w_in=jnp.stack([r_wit0.T, r_wit1.T]), ffn_w_out=jnp.stack([r_wo0, r_wo1]),
                 ple_w=jnp.stack([r_plet0.T, r_plet1.T]), ple_gate_w=jnp.stack([r_wg0, r_wg1]))

    small = dict(mla_q_norm=g_gq, mla_kv_norm=g_gkv, s5_a_re=g_a_re, s5_a_im=g_a_im, s5_log_dt=g_log_dt,
                 s5_b_re=unt(g_b_re), s5_b_im=unt(g_b_im), s5_c_re=g_c_re, s5_c_im=g_c_im, s5_d=dd[0],
                 ln1_g=jnp.stack([g0['ln1_g'], g1['ln1_g']]), ln1_b=jnp.stack([g0['ln1_b'], g1['ln1_b']]),
                 ln2_g=jnp.stack([g0['ln2_g'], g1['ln2_g']]), ln2_b=jnp.stack([g0['ln2_b'], g1['ln2_b']]))
    flat = jnp.concatenate([v_.reshape(-1) for v_ in small.values()])
    m_per = -(-flat.shape[0] // (LANES * SUBLANES)) * SUBLANES
    pack = jnp.pad(flat, (0, m_per * LANES - flat.shape[0])).reshape(m_per, LANES)
    total = _sum_devices(_allgather_small(pack), m_per).reshape(-1)
    off = 0
    for k_, v_ in small.items():
        n_ = v_.size
        piece = total[off:off + n_]
        off += n_
        if k_ == 's5_d':
            grads[k_] = lax.dynamic_slice(piece, (me * (d // N_CHIPS),), (d // N_CHIPS,)).reshape(weights[k_].shape)
        else:
            grads[k_] = piece.reshape(weights[k_].shape)

    deltas, new_m, new_v = {}, {}, {}
    for k_ in names:
        w_ = weights[k_]
        shape = w_.shape
        if w_.ndim == 3 and w_.shape[-1] >= LANES:
            two_d = (shape[0] * shape[1], shape[2])
        elif w_.ndim == 4:
            two_d = (shape[0] * shape[1], shape[2] * shape[3])
        else:
            two_d = (1, w_.size) if w_.ndim == 2 and shape[0] == 1 else (shape[0], w_.size // shape[0])
        dl, nm, nv = _adamw(f"adamw_{k_}", w_.reshape(two_d), grads[k_].reshape(two_d), m_in[k_].reshape(two_d),
                            v_in[k_].reshape(two_d))
        deltas[k_], new_m[k_], new_v[k_] = dl.reshape(shape), nm.reshape(shape), nv.reshape(shape)

    return (loss, grad_x[None], *[grads[k_] for k_ in names], *[deltas[k_] for k_ in names],
            *[new_m[k_] for k_ in names], *[new_v[k_] for k_ in names])
```

```python
import functools
import math

import jax
import jax.numpy as jnp
from jax import lax
from jax.experimental import pallas as pl
from jax.experimental.pallas import tpu as pltpu
from jax.experimental.pallas import tpu_sc as plsc

F32 = jnp.float32
BF16 = jnp.bfloat16
MESH = pl.DeviceIdType.MESH

LANES = 128
SUBLANES = 8
BF16_ROWS = 16
VMEM_LIMIT = 48 * 2 ** 20
N_CHIPS = 4
N_DEV = 8

NOPE = 128
ROPE = 64
VDIM = 128
QK_PAD = 256
DHD = 128
DIL_STEPS = 128
DIL_BRANCHES = ((128, 1), (512, 4), (2048, 16))
ROPE_THETA = 10000.0
S5_GROUP = 16
S5_STATE = 64
SLAB_GROUPS = LANES // S5_GROUP
SLAB_COLS = SLAB_GROUPS * S5_STATE
NEG = -1e30
LN_EPS = 1e-5
RMS_EPS = 1e-6

ADAM_LR = 0.001
ADAM_B1 = 0.9
ADAM_B2 = 0.999
ADAM_EPS = 1e-08
ADAM_WD = 0.01
ADAM_STEP = 10

NN = ((1,), (0,))
NT = ((1,), (1,))
TN = ((0,), (0,))


def _dot(a, b, dims):
    return lax.dot_general(a, b, (dims, ((), ())), preferred_element_type=F32)


def _bf(v):
    return v.astype(BF16)


def _pick(n, target, q=LANES, also=0):
    g = math.gcd(n, also) if also else n
    if g <= target and g == n:
        return n
    best = None
    for t in range(q, min(g, target) + 1, q):
        if g % t == 0:
            best = t
    assert best is not None, (n, target, q, also)
    return best


def _params(*sem):
    return pltpu.CompilerParams(dimension_semantics=sem, vmem_limit_bytes=VMEM_LIMIT)


def _sigmoid(v):
    return 1.0 / (1.0 + jnp.exp(-v))


def _matmul(name, a, b, form, out_dtype=F32, a_win=None, b_win=None, tm=1024, tn=1024, tk=2048):
    c0, aw = a_win if a_win else (0, a.shape[1])
    if form == 'nt':
        assert b_win is None
        n, kdim = b.shape
        d0 = 0
    else:
        kdim = b.shape[0]
        d0, n = b_win if b_win else (0, b.shape[1])
    if form == 'tn':
        m = aw
        assert a.shape[0] == kdim, (name, a.shape, b.shape)
        tm = _pick(m, tm, also=c0)
        tk = _pick(kdim, tk)
        a_off = c0 // tm
    else:
        m = a.shape[0]
        assert aw == kdim, (name, a.shape, b.shape, a_win)
        tm = _pick(m, tm)
        tk = _pick(kdim, tk, also=c0)
        a_off = c0 // tk
    tn = _pick(n, tn, also=d0)
    b_off = d0 // tn
    nk = kdim // tk
    dims = {'nn': NN, 'nt': NT, 'tn': TN}[form]

    def body(a_ref, b_ref, o_ref, *acc):
        prod = _dot(_bf(a_ref[...]), _bf(b_ref[...]), dims)
        if nk == 1:
            o_ref[...] = prod.astype(o_ref.dtype)
            return
        acc_ref, = acc
        k = pl.program_id(2)

        @pl.when(k == 0)
        def _():
            acc_ref[...] = prod

        @pl.when((k > 0) & (k < nk - 1))
        def _():
            acc_ref[...] += prod

        @pl.when(k == nk - 1)
        def _():
            o_ref[...] = (acc_ref[...] + prod).astype(o_ref.dtype)

    if form == 'tn':
        a_spec = pl.BlockSpec((tk, tm), lambda i, j, k: (k, i + a_off))
    else:
        a_spec = pl.BlockSpec((tm, tk), lambda i, j, k: (i, k + a_off))
    if form == 'nt':
        b_spec = pl.BlockSpec((tn, tk), lambda i, j, k: (j, k))
    else:
        b_spec = pl.BlockSpec((tk, tn), lambda i, j, k: (k, j + b_off))
    return pl.pallas_call(
        body, name=name,
        grid=(m // tm, n // tn, nk),
        in_specs=[a_spec, b_spec],
        out_specs=pl.BlockSpec((tm, tn), lambda i, j, k: (i, j)),
        out_shape=jax.ShapeDtypeStruct((m, n), out_dtype),
        scratch_shapes=[pltpu.VMEM((tm, tn), F32)] if nk > 1 else [],
        compiler_params=_params("parallel", "parallel", "arbitrary"),
    )(a, b)


def _nat(tile, width, cb=0):
    return pl.BlockSpec((tile, width), lambda i: (i, cb))


def _perm(tile, width, seg_tiles, ncb=1, cb=0):
    return pl.BlockSpec((tile, width), lambda i: (i % seg_tiles, (i // seg_tiles) * ncb + cb))


def _whole(shape):
    return pl.BlockSpec(shape, lambda i: (0,) * len(shape))


def _perm_view(a):
    s, w = a.shape
    return a.reshape(s // SUBLANES, SUBLANES * w)


def _row_spec(a, layout, tile, width, ncb=1, cb=0):
    if layout == 'nat':
        return a, _nat(tile, width, cb)
    seg_tiles = a.shape[0] // SUBLANES // tile
    return _perm_view(a), _perm(tile, width, seg_tiles, ncb, cb)


def _ln_fwd(name, alpha, a, adds, gate, g, b, want_perm=False, tile=256):
    s, d = a.shape
    n_add = len(adds)
    has_gate = gate is not None

    def body(*refs):
        a_ref = refs[0]
        add_refs = refs[1:1 + n_add]
        pos = 1 + n_add
        if has_gate:
            val_ref, pre_ref = refs[pos], refs[pos + 1]
            pos += 2
        g_ref, b_ref = refs[pos], refs[pos + 1]
        outs = refs[pos + 2:]
        z = alpha * a_ref[...]
        for r in add_refs:
            z = z + r[...]
        if has_gate:
            z = z + val_ref[...] * _sigmoid(pre_ref[...])
        mu = jnp.mean(z, axis=-1, keepdims=True)
        zc = z - mu
        var = jnp.mean(zc * zc, axis=-1, keepdims=True)
        rstd = lax.rsqrt(var + LN_EPS)
        xhat = zc * rstd
        h = xhat * g_ref[...] + b_ref[...]
        outs[0][...] = h
        outs[1][...] = xhat
        outs[2][...] = jnp.broadcast_to(rstd, (tile, LANES))
        outs[3][...] = _bf(h)
        if want_perm:
            outs[4][...] = h

    ins, specs = [a], [_nat(tile, d)]
    for arr, layout in adds:
        x_, sp = _row_spec(arr, layout, tile, d)
        ins.append(x_)
        specs.append(sp)
    if has_gate:
        layout = gate[0]
        for arr, ncb, cb in gate[1:]:
            x_, sp = _row_spec(arr, layout, tile, d, ncb=ncb, cb=cb)
            ins.append(x_)
            specs.append(sp)
    ins += [g.reshape(1, d), b.reshape(1, d)]
    specs += [_whole((1, d)), _whole((1, d))]
    out_shape = [jax.ShapeDtypeStruct((s, d), F32), jax.ShapeDtypeStruct((s, d), F32),
                 jax.ShapeDtypeStruct((s, LANES), F32), jax.ShapeDtypeStruct((s, d), BF16)]
    out_specs = [_nat(tile, d), _nat(tile, d), _nat(tile, LANES), _nat(tile, d)]
    if want_perm:
        seg_tiles = s // SUBLANES // tile
        out_shape.append(jax.ShapeDtypeStruct((s // SUBLANES, SUBLANES * d), F32))
        out_specs.append(_perm(tile, d, seg_tiles))
    res = pl.pallas_call(
        body, name=name, grid=(s // tile,), in_specs=specs, out_specs=out_specs, out_shape=out_shape,
        compiler_params=_params("parallel"),
    )(*ins)
    return res[0], res[1], res[2], res[3], (res[4].reshape(s, d) if want_perm else None)


def _ln_bwd(name, dparts, xhat, rstd, g, gate=None, tile=256):
    s, d = xhat.shape
    n_part = len(dparts)
    coefs = [c for _, _, c in dparts]
    has_gate = gate is not None

    def body(*refs):
        part_refs = refs[:n_part]
        xhat_ref, rstd_ref, g_ref = refs[n_part:n_part + 3]
        pos = n_part + 3
        if has_gate:
            val_ref, pre_ref = refs[pos], refs[pos + 1]
            pos += 2
        outs = list(refs[pos:])
        dz_ref = outs.pop(0)
        dzb_ref = outs.pop(0)
        dgate_ref = outs.pop(0) if has_gate else None
        dg_ref, db_ref = outs
        dh = coefs[0] * part_refs[0][...]
        for c, r in zip(coefs[1:], part_refs[1:]):
            dh = dh + c * r[...]
        xh = xhat_ref[...]
        dxh = dh * g_ref[...]
        m1 = jnp.mean(dxh, axis=-1, keepdims=True)
        m2 = jnp.mean(dxh * xh, axis=-1, keepdims=True)
        dz = rstd_ref[:, 0:1] * (dxh - m1 - xh * m2)
        dz_ref[...] = dz
        dzb_ref[...] = _bf(dz)
        if has_gate:
            sg = _sigmoid(pre_ref[...])
            dval = dz * sg
            dpre = dz * val_ref[...] * sg * (1.0 - sg)
            dgate_ref[...] = jnp.concatenate([_bf(dval), _bf(dpre)], axis=1)

        @pl.when(pl.program_id(0) == 0)
        def _():
            dg_ref[...] = jnp.zeros_like(dg_ref)
            db_ref[...] = jnp.zeros_like(db_ref)

        dg_ref[0:1, :] += jnp.sum(dh * xh, axis=0, keepdims=True)
        db_ref[0:1, :] += jnp.sum(dh, axis=0, keepdims=True)

    ins, specs = [], []
    for arr, layout, _ in dparts:
        x_, sp = _row_spec(arr, layout, tile, d)
        ins.append(x_)
        specs.append(sp)
    ins += [xhat, rstd, g.reshape(1, d)]
    specs += [_nat(tile, d), _nat(tile, LANES), _whole((1, d))]
    gate_layout = None
    if has_gate:
        gate_layout = gate[0]
        for arr, ncb, cb in gate[1:]:
            x_, sp = _row_spec(arr, gate_layout, tile, d, ncb=ncb, cb=cb)
            ins.append(x_)
            specs.append(sp)
    seg_tiles = s // SUBLANES // tile
    out_shape = [jax.ShapeDtypeStruct((s, d), F32), jax.ShapeDtypeStruct((s, d), BF16)]
    out_specs = [_nat(tile, d), _nat(tile, d)]
    if has_gate:
        if gate_layout == 'nat':
            out_shape.append(jax.ShapeDtypeStruct((s, 2 * d), BF16))
            out_specs.append(_nat(tile, 2 * d))
        else:
            out_shape.append(jax.ShapeDtypeStruct((s // SUBLANES, SUBLANES * 2 * d), BF16))
            out_specs.append(_perm(tile, 2 * d, seg_tiles))
    out_shape += [jax.ShapeDtypeStruct((SUBLANES, d), F32)] * 2
    out_specs += [_whole((SUBLANES, d))] * 2
    res = list(pl.pallas_call(
        body, name=name, grid=(s // tile,), in_specs=specs, out_specs=out_specs, out_shape=out_shape,
        compiler_params=_params("arbitrary"),
    )(*ins))
    out = [res.pop(0), res.pop(0)]
    if has_gate:
        out.append(res.pop(0).reshape(s, 2 * d))
    out += [res[0][0], res[1][0]]
    return out


def _loss_partial(h, target, tile=256):
    s, d = h.shape

    def body(h_ref, t_ref, o_ref):
        @pl.when(pl.program_id(0) == 0)
        def _():
            o_ref[...] = jnp.zeros_like(o_ref)

        e = h_ref[...] - t_ref[...]
        sq = e * e
        part = sq[:, 0:LANES]
        for k in range(1, d // LANES):
            part = part + sq[:, k * LANES:(k + 1) * LANES]
        o_ref[0:1, :] += jnp.sum(part, axis=0, keepdims=True) * (0.5 / d)

    return pl.pallas_call(
        body, name="loss_partial", grid=(s // tile,), in_specs=[_nat(tile, d), _nat(tile, d)],
        out_specs=_whole((SUBLANES, LANES)), out_shape=jax.ShapeDtypeStruct((SUBLANES, LANES), F32),
        compiler_params=_params("arbitrary"),
    )(h, target)


def _swiglu_fwd(name, gu, tile=256):
    s, f2 = gu.shape
    f = f2 // 2
    cw = _pick(f, 1408)
    ncb = f // cw

    def body(g_ref, u_ref, o_ref):
        gg = g_ref[...]
        o_ref[...] = _bf(gg * _sigmoid(gg) * u_ref[...])

    return pl.pallas_call(
        body, name=name, grid=(s // tile, ncb),
        in_specs=[pl.BlockSpec((tile, cw), lambda i, j: (i, j)), pl.BlockSpec((tile, cw), lambda i, j: (i, j + ncb))],
        out_specs=pl.BlockSpec((tile, cw), lambda i, j: (i, j)),
        out_shape=jax.ShapeDtypeStruct((s, f), BF16), compiler_params=_params("parallel", "parallel"),
    )(gu, gu)


def _swiglu_bwd(name, gu, dact, tile=128):
    s, f2 = gu.shape
    f = f2 // 2

    def body(g_ref, u_ref, da_ref, o_ref):
        gg = g_ref[...]
        sg = _sigmoid(gg)
        da = da_ref[...].astype(F32)
        silu = gg * sg
        o_ref[:, :f] = _bf(da * u_ref[...] * (sg + silu * (1.0 - sg)))
        o_ref[:, f:] = _bf(da * silu)

    return pl.pallas_call(
        body, name=name, grid=(s // tile,),
        in_specs=[_nat(tile, f, 0), _nat(tile, f, 1), _nat(tile, f)], out_specs=_nat(tile, f2),
        out_shape=jax.ShapeDtypeStruct((s, f2), BF16), compiler_params=_params("parallel"),
    )(gu, gu, dact)


def _rms_fwd(proj, ql, kvl, gq, gkv, tile=256):
    s = proj.shape[0]
    assert ql == kvl

    def body(q_ref, kv_ref, gq_ref, gkv_ref, o_ref):
        def nrm(x, gg):
            return x * lax.rsqrt(jnp.mean(x * x, axis=-1, keepdims=True) + RMS_EPS) * gg

        o_ref[...] = jnp.concatenate([_bf(nrm(q_ref[...], gq_ref[...])), _bf(nrm(kv_ref[...], gkv_ref[...]))], axis=1)

    return pl.pallas_call(
        body, name="mla_rms_fwd", grid=(s // tile,),
        in_specs=[_nat(tile, ql, 0), _nat(tile, kvl, 1), _whole((1, ql)), _whole((1, kvl))],
        out_specs=_nat(tile, ql + kvl), out_shape=jax.ShapeDtypeStruct((s, ql + kvl), BF16),
        compiler_params=_params("parallel"),
    )(proj, proj, gq.reshape(1, ql), gkv.reshape(1, kvl))


def _rope_coeffs(pos, invf):
    ang = pos * invf
    cs, sn = jnp.cos(ang), jnp.sin(ang)
    lane = lax.broadcasted_iota(jnp.int32, ang.shape, 1)
    half = ROPE // 2
    c = jnp.where(lane < ROPE, cs, 0.0)
    sa = jnp.where(lane < half, -sn, 0.0)
    sb = jnp.where((lane >= half) & (lane < ROPE), sn, 0.0)
    return c, sa, sb


def _rope_prep(q_raw, kv, proj, kpe_cb, pos, invf, heads, tile=256):
    s = q_raw.shape[0]
    half = ROPE // 2

    def body(q_ref, kv_ref, kpe_ref, pos_ref, invf_ref, qf_ref, kf_ref, v_ref):
        c, sa, sb = _rope_coeffs(pos_ref[...], invf_ref[...])

        def rope(t):
            return t * c + pltpu.roll(t, LANES - half, 1) * sa + pltpu.roll(t, half, 1) * sb

        kr = _bf(rope(kpe_ref[...]))
        for hh in range(heads):
            o = hh * QK_PAD
            qf_ref[:, o:o + NOPE] = _bf(q_ref[:, o:o + NOPE])
            qf_ref[:, o + NOPE:o + QK_PAD] = _bf(rope(q_ref[:, o + NOPE:o + QK_PAD]))
            kf_ref[:, o:o + NOPE] = _bf(kv_ref[:, o:o + NOPE])
            kf_ref[:, o + NOPE:o + QK_PAD] = kr
            v_ref[:, hh * VDIM:(hh + 1) * VDIM] = _bf(kv_ref[:, o + NOPE:o + QK_PAD])

    w = heads * QK_PAD
    return pl.pallas_call(
        body, name="mla_rope_prep", grid=(s // tile,),
        in_specs=[_nat(tile, w), _nat(tile, w), _nat(tile, LANES, kpe_cb), _nat(tile, 1), _whole((1, LANES))],
        out_specs=[_nat(tile, w), _nat(tile, w), _nat(tile, heads * VDIM)],
        out_shape=[jax.ShapeDtypeStruct((s, w), BF16), jax.ShapeDtypeStruct((s, w), BF16),
                   jax.ShapeDtypeStruct((s, heads * VDIM), BF16)],
        compiler_params=_params("parallel"),
    )(q_raw, kv, proj, pos, invf)


def _rope_unprep(dqf, dkf, dv, pos, invf, heads, tile=256):
    s = dqf.shape[0]
    half = ROPE // 2

    def body(dq_ref, dk_ref, dv_ref, pos_ref, invf_ref, dqr_ref, dkv_ref, dkpe_ref):
        c, sa, sb = _rope_coeffs(pos_ref[...], invf_ref[...])

        def unrope(gt):
            return gt * c + pltpu.roll(gt * sa, half, 1) + pltpu.roll(gt * sb, LANES - half, 1)

        dkpe = jnp.zeros((tile, LANES), F32)
        for hh in range(heads):
            o = hh * QK_PAD
            dqr_ref[:, o:o + NOPE] = _bf(dq_ref[:, o:o + NOPE])
            dqr_ref[:, o + NOPE:o + QK_PAD] = _bf(unrope(dq_ref[:, o + NOPE:o + QK_PAD]))
            dkv_ref[:, o:o + NOPE] = _bf(dk_ref[:, o:o + NOPE])
            dkv_ref[:, o + NOPE:o + QK_PAD] = _bf(dv_ref[:, hh * VDIM:(hh + 1) * VDIM])
            dkpe = dkpe + dk_ref[:, o + NOPE:o + QK_PAD]
        dkpe_ref[...] = unrope(dkpe)

    w = heads * QK_PAD
    return pl.pallas_call(
        body, name="mla_rope_unprep", grid=(s // tile,),
        in_specs=[_nat(tile, w), _nat(tile, w), _nat(tile, heads * VDIM), _nat(tile, 1), _whole((1, LANES))],
        out_specs=[_nat(tile, w), _nat(tile, w), _nat(tile, LANES)],
        out_shape=[jax.ShapeDtypeStruct((s, w), BF16), jax.ShapeDtypeStruct((s, w), BF16),
                   jax.ShapeDtypeStruct((s, LANES), F32)],
        compiler_params=_params("parallel"),
    )(dqf, dkf, dv, pos, invf)


LOG2E = 1.4426950408889634
MLA_SCALE = (NOPE + ROPE) ** -0.5


def _mla_scores_t(k, q, t, masked):
    sc = _dot(k, q, NT) * (MLA_SCALE * LOG2E)
    if masked:
        row = lax.broadcasted_iota(jnp.int32, (t, t), 0)
        col = lax.broadcasted_iota(jnp.int32, (t, t), 1)
        sc = jnp.where(row <= col, sc, NEG)
    return sc


def _mla_fwd(qf, kf, vt, heads, t=512):
    s = qf.shape[0]
    t = min(t, s)
    nq = s // t

    def body(q_ref, k_ref, vt_ref, o_ref, lse_ref, m_ref, l_ref, acc_ref):
        i = pl.program_id(1)
        m_ref[...] = jnp.full_like(m_ref, NEG)
        l_ref[...] = jnp.zeros_like(l_ref)
        acc_ref[...] = jnp.zeros_like(acc_ref)
        q = q_ref[...]

        def block(j, masked):
            r0 = pl.multiple_of(j * t, t)
            sc = _mla_scores_t(k_ref[pl.ds(r0, t), :], q, t, masked)
            m_prev = m_ref[0:1, :]
            m_new = jnp.maximum(m_prev, jnp.max(sc, axis=0, keepdims=True))
            corr = jnp.exp2(m_prev - m_new)
            p = jnp.exp2(sc - m_new)
            l_new = corr * l_ref[0:1, :] + jnp.sum(p, axis=0, keepdims=True)
            acc_ref[...] = corr * acc_ref[...] + _dot(vt_ref[:, pl.ds(r0, t)], _bf(p), NN)
            m_ref[...] = jnp.broadcast_to(m_new, (SUBLANES, t))
            l_ref[...] = jnp.broadcast_to(l_new, (SUBLANES, t))

        def unmasked(j, carry):
            block(j, False)
            return carry

        lax.fori_loop(0, i, unmasked, 0)
        block(i, True)
        o_ref[...] = (acc_ref[...] / l_ref[0:1, :]).T
        lse_ref[...] = m_ref[...] + jnp.log(l_ref[...]) * LOG2E

    return pl.pallas_call(
        body, name="mla_flash_fwd", grid=(heads, nq),
        in_specs=[pl.BlockSpec((t, QK_PAD), lambda h, i: (i, h)), pl.BlockSpec((s, QK_PAD), lambda h, i: (0, h)),
                  pl.BlockSpec((VDIM, s), lambda h, i: (h, 0))],
        out_specs=[pl.BlockSpec((t, VDIM), lambda h, i: (i, h)), pl.BlockSpec((SUBLANES, t), lambda h, i: (h, i))],
        out_shape=[jax.ShapeDtypeStruct((s, heads * VDIM), F32), jax.ShapeDtypeStruct((heads * SUBLANES, s), F32)],
        scratch_shapes=[pltpu.VMEM((SUBLANES, t), F32), pltpu.VMEM((SUBLANES, t), F32), pltpu.VMEM((VDIM, t), F32)],
        compiler_params=_params("parallel", "arbitrary"),
    )(qf, kf, vt)


def _mla_bwd(qf, kf, v, do, lse_t, delta_t, heads, do_cb0, t=512):
    s = qf.shape[0]
    t = min(t, s)
    nq = s // t

    def body(q_ref, k_ref, v_ref, do_ref, lse_ref, dl_ref, dq_ref, dk_ref, dv_ref, acc_ref):
        i = pl.program_id(1)

        @pl.when(i == 0)
        def _():
            dk_ref[...] = jnp.zeros_like(dk_ref)
            dv_ref[...] = jnp.zeros_like(dv_ref)

        acc_ref[...] = jnp.zeros_like(acc_ref)
        q, dob = q_ref[...], do_ref[...]
        lse, dl = lse_ref[0:1, :], dl_ref[0:1, :]

        def block(j, masked):
            r0 = pl.multiple_of(j * t, t)
            k = k_ref[pl.ds(r0, t), :]
            p = jnp.exp2(_mla_scores_t(k, q, t, masked) - lse)
            dp = _dot(v_ref[pl.ds(r0, t), :], dob, NT)
            ds = _bf(p * (dp - dl) * MLA_SCALE)
            acc_ref[...] += _dot(ds, k, TN)
            dk_ref[pl.ds(r0, t), :] += _dot(ds, q, NN)
            dv_ref[pl.ds(r0, t), :] += _dot(_bf(p), dob, NN)

        def unmasked(j, carry):
            block(j, False)
            return carry

        lax.fori_loop(0, i, unmasked, 0)
        block(i, True)
        dq_ref[...] = acc_ref[...]

    qs = lambda w, off=0: pl.BlockSpec((t, w), lambda h, i: (i, h + off))
    ks = lambda w: pl.BlockSpec((s, w), lambda h, i: (0, h))
    st = pl.BlockSpec((SUBLANES, t), lambda h, i: (h, i))
    return pl.pallas_call(
        body, name="mla_flash_bwd", grid=(heads, nq),
        in_specs=[qs(QK_PAD), ks(QK_PAD), ks(VDIM), qs(VDIM, do_cb0), st, st],
        out_specs=[qs(QK_PAD), ks(QK_PAD), ks(VDIM)],
        out_shape=[jax.ShapeDtypeStruct((s, heads * QK_PAD), F32), jax.ShapeDtypeStruct((s, heads * QK_PAD), F32),
                   jax.ShapeDtypeStruct((s, heads * VDIM), F32)],
        scratch_shapes=[pltpu.VMEM((t, QK_PAD), F32)],
        compiler_params=_params("parallel", "arbitrary"),
    )(qf, kf, v, do, lse_t, delta_t)


def _band_mask(tq, first_block):
    row = lax.broadcasted_iota(jnp.int32, (tq, DIL_STEPS + tq), 0)
    col = lax.broadcasted_iota(jnp.int32, (tq, DIL_STEPS + tq), 1)
    dist = row + DIL_STEPS - col
    valid = (dist >= 0) & (dist <= DIL_STEPS) & (jnp.logical_not(first_block) | (col >= DIL_STEPS))
    return dist, valid


def _dil_scores(q, kp, kc, slope, dil, tq, first_block):
    sc = jnp.concatenate([_dot(q, kp, NT), _dot(q, kc, NT)], axis=1) * (DHD ** -0.5)
    dist, valid = _band_mask(tq, first_block)
    return jnp.where(valid, sc - slope * (dil * dist).astype(F32), NEG)


def _dil_specs(proj_w, dh, tq):
    pwb = proj_w // LANES
    r_of = lambda cb: cb // dh
    h_of = lambda cb: cb % dh
    cur = lambda off: pl.BlockSpec((tq, DHD), lambda cb, i: (i, r_of(cb) * pwb + off + h_of(cb)))
    prev = lambda off: pl.BlockSpec(
        (DIL_STEPS, DHD), lambda cb, i: (jnp.maximum(i * (tq // DIL_STEPS) - 1, 0), r_of(cb) * pwb + off + h_of(cb)))
    return cur, prev


def _dil_fwd(name, proj, slopes, dil, dh, q_cb, tq=512):
    s, pw = proj.shape
    l = s // dil
    tq = min(tq, l)
    nb = l // tq
    k_cb, v_cb = q_cb + dh, q_cb + 2 * dh
    cur, prev = _dil_specs(pw, dh, tq)
    pv = proj.reshape(l, dil * pw)

    def body(q_ref, kc_ref, kp_ref, vc_ref, vp_ref, sl_ref, o_ref, lse_ref):
        i = pl.program_id(1)
        sc = _dil_scores(_bf(q_ref[...]), _bf(kp_ref[...]), _bf(kc_ref[...]), sl_ref[0:1, 0:1], dil, tq, i == 0)
        m = jnp.max(sc, axis=-1, keepdims=True)
        e = jnp.exp(sc - m)
        lsum = jnp.sum(e, axis=-1, keepdims=True)
        pn = e / lsum
        o_ref[...] = (_dot(_bf(pn[:, :DIL_STEPS]), _bf(vp_ref[...]), NN)
                      + _dot(_bf(pn[:, DIL_STEPS:]), _bf(vc_ref[...]), NN))
        lse_ref[...] = jnp.broadcast_to(m + jnp.log(lsum), (tq, LANES))

    ospec = pl.BlockSpec((tq, DHD), lambda cb, i: (i, cb))
    o, lse = pl.pallas_call(
        body, name=name, grid=(dil * dh, nb),
        in_specs=[cur(q_cb), cur(k_cb), prev(k_cb), cur(v_cb), prev(v_cb),
                  pl.BlockSpec((SUBLANES, LANES), lambda cb, i: (cb % dh, 0))],
        out_specs=[ospec, ospec],
        out_shape=[jax.ShapeDtypeStruct((l, dil * dh * DHD), F32)] * 2,
        compiler_params=_params("parallel", "parallel"),
    )(pv, pv, pv, pv, pv, slopes)
    return o.reshape(s, dh * DHD), lse.reshape(s, dh * DHD)


def _dil_bwd_dq(name, proj, slopes, do, lse, delta, dil, dh, q_cb, b_cb0, tq=512):
    s, pw = proj.shape
    mixw = do.shape[1]
    l = s // dil
    tq = min(tq, l)
    nb = l // tq
    k_cb, v_cb = q_cb + dh, q_cb + 2 * dh
    cur, prev = _dil_specs(pw, dh, tq)
    pv = proj.reshape(l, dil * pw)
    mb = mixw // LANES
    mspec = pl.BlockSpec((tq, DHD), lambda cb, i: (i, (cb // dh) * mb + b_cb0 + cb % dh))
    ospec = pl.BlockSpec((tq, DHD), lambda cb, i: (i, cb))

    def body(q_ref, kc_ref, kp_ref, vc_ref, vp_ref, sl_ref, do_ref, lse_ref, dl_ref, dq_ref):
        i = pl.program_id(1)
        kp, kc = _bf(kp_ref[...]), _bf(kc_ref[...])
        sc = _dil_scores(_bf(q_ref[...]), kp, kc, sl_ref[0:1, 0:1], dil, tq, i == 0)
        p = jnp.exp(sc - lse_ref[:, 0:1])
        dob = do_ref[...]
        dp = jnp.concatenate([_dot(dob, _bf(vp_ref[...]), NT), _dot(dob, _bf(vc_ref[...]), NT)], axis=1)
        ds = _bf(p * (dp - dl_ref[:, 0:1]) * (DHD ** -0.5))
        dq_ref[...] = _dot(ds[:, :DIL_STEPS], kp, NN) + _dot(ds[:, DIL_STEPS:], kc, NN)

    dq = pl.pallas_call(
        body, name=name, grid=(dil * dh, nb),
        in_specs=[cur(q_cb), cur(k_cb), prev(k_cb), cur(v_cb), prev(v_cb),
                  pl.BlockSpec((SUBLANES, LANES), lambda cb, i: (cb % dh, 0)), mspec, ospec, mspec],
        out_specs=ospec, out_shape=jax.ShapeDtypeStruct((l, dil * dh * DHD), F32),
        compiler_params=_params("parallel", "parallel"),
    )(pv, pv, pv, pv, pv, slopes, do.reshape(l, dil * mixw), lse.reshape(l, dil * dh * DHD), delta.reshape(l, dil * mixw))
    return dq.reshape(s, dh * DHD)


def _dil_bwd_dkv(name, proj, slopes, do, lse, delta, dil, dh, q_cb, b_cb0, tk=512):
    s, pw = proj.shape
    mixw = do.shape[1]
    l = s // dil
    tk = min(tk, l)
    nb = l // tk
    k_cb, v_cb = q_cb + dh, q_cb + 2 * dh
    pwb, mb = pw // LANES, mixw // LANES
    sub = tk // DIL_STEPS
    last128 = l // DIL_STEPS - 1
    pv = proj.reshape(l, dil * pw)

    def cur(width_blocks, off):
        return pl.BlockSpec((tk, DHD), lambda cb, j: (j, (cb // dh) * width_blocks + off + cb % dh))

    def nxt(width_blocks, off):
        return pl.BlockSpec((DIL_STEPS, DHD), lambda cb, j: (jnp.minimum((j + 1) * sub, last128),
                                                               (cb // dh) * width_blocks + off + cb % dh))

    ocur = pl.BlockSpec((tk, DHD), lambda cb, j: (j, cb))
    onxt = pl.BlockSpec((DIL_STEPS, DHD), lambda cb, j: (jnp.minimum((j + 1) * sub, last128), cb))

    def body(k_ref, v_ref, qc_ref, qn_ref, sl_ref, doc_ref, don_ref, lsec_ref, lsen_ref, dlc_ref, dln_ref,
             dk_ref, dv_ref):
        j = pl.program_id(1)
        slope = sl_ref[0:1, 0:1]
        scale = DHD ** -0.5
        k, v = _bf(k_ref[...]), _bf(v_ref[...])
        qc = _bf(qc_ref[...])
        row = lax.broadcasted_iota(jnp.int32, (tk, tk), 0)
        col = lax.broadcasted_iota(jnp.int32, (tk, tk), 1)
        dist = row - col
        valid = (dist >= 0) & (dist <= DIL_STEPS)
        sc = jnp.where(valid, _dot(qc, k, NT) * scale - slope * (dil * dist).astype(F32), NEG)
        p = jnp.exp(sc - lsec_ref[:, 0:1])
        doc = doc_ref[...]
        ds = _bf(p * (_dot(doc, v, NT) - dlc_ref[:, 0:1]) * scale)
        dv_ref[...] = _dot(_bf(p), doc, TN)
        dk_ref[...] = _dot(ds, qc, TN)
        kl, vl = k[tk - DIL_STEPS:, :], v[tk - DIL_STEPS:, :]
        qn = _bf(qn_ref[...])
        row = lax.broadcasted_iota(jnp.int32, (DIL_STEPS, DIL_STEPS), 0)
        col = lax.broadcasted_iota(jnp.int32, (DIL_STEPS, DIL_STEPS), 1)
        dist = DIL_STEPS + row - col
        valid = (dist <= DIL_STEPS) & (j < nb - 1)
        sc = jnp.where(valid, _dot(qn, kl, NT) * scale - slope * (dil * dist).astype(F32), NEG)
        p = jnp.exp(sc - lsen_ref[:, 0:1])
        don = don_ref[...]
        ds = _bf(p * (_dot(don, vl, NT) - dln_ref[:, 0:1]) * scale)
        dv_ref[tk - DIL_STEPS:, :] += _dot(_bf(p), don, TN)
        dk_ref[tk - DIL_STEPS:, :] += _dot(ds, qn, TN)

    dov = do.reshape(l, dil * mixw)
    dlv = delta.reshape(l, dil * mixw)
    lsv = lse.reshape(l, dil * dh * DHD)
    dk, dv = pl.pallas_call(
        body, name=name, grid=(dil * dh, nb),
        in_specs=[cur(pwb, k_cb), cur(pwb, v_cb), cur(pwb, q_cb), nxt(pwb, q_cb),
                  pl.BlockSpec((SUBLANES, LANES), lambda cb, j: (cb % dh, 0)),
                  cur(mb, b_cb0), nxt(mb, b_cb0), ocur, onxt, cur(mb, b_cb0), nxt(mb, b_cb0)],
        out_specs=[ocur, ocur], out_shape=[jax.ShapeDtypeStruct((l, dil * dh * DHD), F32)] * 2,
        compiler_params=_params("parallel", "parallel"),
    )(pv, pv, pv, pv, slopes, dov, dov, lsv, lsv, dlv, dlv)
    return dk.reshape(s, dh * DHD), dv.reshape(s, dh * DHD)


def _dil_merge(out_a, outs, lses, tile=256):
    s, wa = out_a.shape
    wb = outs[0].shape[1]
    nbr = len(outs)

    def body(*refs):
        a_ref = refs[0]
        o_refs, l_refs = refs[1:1 + nbr], refs[1 + nbr:1 + 2 * nbr]
        att_ref, ob_ref, lse_ref = refs[1 + 2 * nbr:]
        ls = [r[...] for r in l_refs]
        m = ls[0]
        for x_ in ls[1:]:
            m = jnp.maximum(m, x_)
        es = [jnp.exp(x_ - m) for x_ in ls]
        tot = es[0]
        for e in es[1:]:
            tot = tot + e
        ob = (es[0] / tot) * o_refs[0][...]
        for e, r in zip(es[1:], o_refs[1:]):
            ob = ob + (e / tot) * r[...]
        ob_ref[...] = ob
        lse_ref[...] = m + jnp.log(tot)
        att_ref[...] = jnp.concatenate([_bf(a_ref[...]), _bf(ob)], axis=1)

    return pl.pallas_call(
        body, name="dil_merge", grid=(s // tile,),
        in_specs=[_nat(tile, wa)] + [_nat(tile, wb)] * (2 * nbr),
        out_specs=[_nat(tile, wa + wb), _nat(tile, wb), _nat(tile, wb)],
        out_shape=[jax.ShapeDtypeStruct((s, wa + wb), BF16), jax.ShapeDtypeStruct((s, wb), F32),
                   jax.ShapeDtypeStruct((s, wb), F32)],
        compiler_params=_params("parallel"),
    )(out_a, *outs, *lses)


DIL_BLOCK = 2048


def _dil_unit_rows(u, dil, block):
    sub = u // dil
    return u % dil + (dil * DIL_STEPS) * sub, sub == 0


def _dil_rows(base, dil):
    return pl.ds(base, DIL_STEPS, stride=dil) if dil > 1 else pl.ds(base, DIL_STEPS)


def _dil_unit_scores(q, kp, kc, slope, dil, no_prev):
    sc = jnp.concatenate([_dot(q, kp, NT), _dot(q, kc, NT)], axis=1) * (DHD ** -0.5)
    row = lax.broadcasted_iota(jnp.int32, (DIL_STEPS, 2 * DIL_STEPS), 0)
    col = lax.broadcasted_iota(jnp.int32, (DIL_STEPS, 2 * DIL_STEPS), 1)
    dist = row + DIL_STEPS - col
    valid = (dist >= 0) & (dist <= DIL_STEPS) & (jnp.logical_not(no_prev) | (col >= DIL_STEPS))
    return jnp.where(valid, sc - slope * (dil * dist).astype(F32), NEG)


def _dil_in_specs(pw, dh, q_cb, block, rev_nb=None):
    blk = (lambda i: i) if rev_nb is None else (lambda i: rev_nb - 1 - i)
    own = lambda off: pl.BlockSpec((block, DHD), lambda h, i: (blk(i), off + h))
    prev = lambda off: pl.BlockSpec((block, DHD), lambda h, i: (jnp.maximum(blk(i) - 1, 0), off + h))
    return [own(q_cb), own(q_cb + dh), prev(q_cb + dh), own(q_cb + 2 * dh), prev(q_cb + 2 * dh)]


def _dil_fused_fwd(proj, slopes, dh, q_cb):
    s, pw = proj.shape
    block = min(DIL_BLOCK, s)
    nb = s // block
    n_units = block // DIL_STEPS
    nbr = len(DIL_BRANCHES)
    assert block >= DIL_STEPS * max(d for _, d in DIL_BRANCHES)

    def body(q_ref, kc_ref, kp_ref, vc_ref, vp_ref, sl_ref, o_ref, lse_ref, kk, vv, *per_branch):
        og, mg, lg = per_branch[:nbr], per_branch[nbr:2 * nbr], per_branch[2 * nbr:]
        i = pl.program_id(1)
        kk[0:block, :] = kp_ref[...]
        kk[block:, :] = kc_ref[...]
        vv[0:block, :] = vp_ref[...]
        vv[block:, :] = vc_ref[...]
        slope = sl_ref[0:1, 0:1]
        for g, (_, dil) in enumerate(DIL_BRANCHES):
            rows = functools.partial(_dil_rows, dil=dil)

            def unit(u, dil=dil, rows=rows):
                q0, first = _dil_unit_rows(u, dil, block)
                q = _bf(q_ref[rows(q0), :])
                kc, kp = _bf(kk[rows(block + q0), :]), _bf(kk[rows(block + q0 - dil * DIL_STEPS), :])
                vc, vp = _bf(vv[rows(block + q0), :]), _bf(vv[rows(block + q0 - dil * DIL_STEPS), :])
                sc = _dil_unit_scores(q, kp, kc, slope, dil, first & (i == 0))
                m = jnp.max(sc, axis=-1, keepdims=True)
                e = jnp.exp(sc - m)
                o = _dot(_bf(e[:, :DIL_STEPS]), vp, NN) + _dot(_bf(e[:, DIL_STEPS:]), vc, NN)
                return q0, o, m, jnp.sum(e, axis=-1, keepdims=True)

            def pair(u, carry, g=g, rows=rows, unit=unit):
                for q0, o, m, lsum in (unit(u), unit(u + n_units // 2)):
                    og[g][rows(q0), :] = o
                    mg[g][rows(q0), :] = jnp.broadcast_to(m, (DIL_STEPS, LANES))
                    lg[g][rows(q0), :] = jnp.broadcast_to(lsum, (DIL_STEPS, LANES))
                return carry

            lax.fori_loop(0, n_units // 2, pair, 0, unroll=2)
        m_all = mg[0][...]
        for g in range(1, nbr):
            m_all = jnp.maximum(m_all, mg[g][...])
        tot = jnp.zeros((block, LANES), F32)
        acc = jnp.zeros((block, DHD), F32)
        for g in range(nbr):
            w = jnp.exp(mg[g][...] - m_all)
            tot = tot + w * lg[g][...]
            acc = acc + w * og[g][...]
        o_ref[...] = acc / tot
        lse_ref[...] = m_all + jnp.log(tot)

    ospec = pl.BlockSpec((block, DHD), lambda h, i: (i, h))
    return pl.pallas_call(
        body, name="dil_fused_fwd", grid=(dh, nb),
        in_specs=_dil_in_specs(pw, dh, q_cb, block) + [pl.BlockSpec((SUBLANES, LANES), lambda h, i: (h, 0))],
        out_specs=[ospec, ospec], out_shape=[jax.ShapeDtypeStruct((s, dh * DHD), F32)] * 2,
        scratch_shapes=[pltpu.VMEM((2 * block, DHD), F32), pltpu.VMEM((2 * block, DHD), F32)]
        + [pltpu.VMEM((block, DHD), F32)] * (3 * nbr),
        compiler_params=_params("parallel", "arbitrary"),
    )(proj, proj, proj, proj, proj, slopes)


def _dil_fused_bwd(proj, slopes, datt, lse, delta, dh, q_cb, b_cb0):
    s, pw = proj.shape
    block = min(DIL_BLOCK, s)
    nb = s // block
    n_units = block // DIL_STEPS
    scale = DHD ** -0.5

    def body(q_ref, kc_ref, kp_ref, vc_ref, vp_ref, sl_ref, do_ref, lse_ref, dl_ref, dq_ref, dk_ref, dv_ref,
             kk, vv, dkk, dvv, carry_k, carry_v):
        ii = pl.program_id(1)
        i = nb - 1 - ii

        @pl.when(ii == 0)
        def _():
            carry_k[...] = jnp.zeros_like(carry_k)
            carry_v[...] = jnp.zeros_like(carry_v)

        kk[0:block, :] = kp_ref[...]
        kk[block:, :] = kc_ref[...]
        vv[0:block, :] = vp_ref[...]
        vv[block:, :] = vc_ref[...]
        dkk[...] = jnp.zeros_like(dkk)
        dvv[...] = jnp.zeros_like(dvv)
        dq_ref[...] = jnp.zeros_like(dq_ref)
        slope = sl_ref[0:1, 0:1]
        for _, dil in DIL_BRANCHES:
            rows = functools.partial(_dil_rows, dil=dil)

            def unit(u, dil=dil, rows=rows):
                q0, first = _dil_unit_rows(u, dil, block)
                cur, prev = rows(block + q0), rows(block + q0 - dil * DIL_STEPS)
                q = _bf(q_ref[rows(q0), :])
                kc, kp, vc, vp = _bf(kk[cur, :]), _bf(kk[prev, :]), _bf(vv[cur, :]), _bf(vv[prev, :])
                dob = _bf(do_ref[rows(q0), :])
                sc = _dil_unit_scores(q, kp, kc, slope, dil, first & (i == 0))
                p = jnp.exp(sc - lse_ref[rows(q0), 0:1])
                dp = jnp.concatenate([_dot(dob, vp, NT), _dot(dob, vc, NT)], axis=1)
                ds = _bf(p * (dp - dl_ref[rows(q0), 0:1]) * scale)
                pb = _bf(p)
                return (rows(q0), cur, prev, _dot(ds[:, :DIL_STEPS], kp, NN) + _dot(ds[:, DIL_STEPS:], kc, NN),
                        _dot(ds[:, :DIL_STEPS], q, TN), _dot(ds[:, DIL_STEPS:], q, TN),
                        _dot(pb[:, :DIL_STEPS], dob, TN), _dot(pb[:, DIL_STEPS:], dob, TN))

            def pair(u, carry, unit=unit):
                for qrows, cur, prev, dq, dkp, dkc, dvp, dvc in (unit(u), unit(u + n_units // 2)):
                    dq_ref[qrows, :] += dq
                    dkk[prev, :] += dkp
                    dkk[cur, :] += dkc
                    dvv[prev, :] += dvp
                    dvv[cur, :] += dvc
                return carry

            lax.fori_loop(0, n_units // 2, pair, 0, unroll=2)
        dk_ref[...] = dkk[block:, :] + carry_k[...]
        dv_ref[...] = dvv[block:, :] + carry_v[...]
        carry_k[...] = dkk[0:block, :]
        carry_v[...] = dvv[0:block, :]

    rev = lambda i: nb - 1 - i
    mspec = pl.BlockSpec((block, DHD), lambda h, i: (rev(i), b_cb0 + h))
    ospec = pl.BlockSpec((block, DHD), lambda h, i: (rev(i), h))
    big = lambda: pltpu.VMEM((2 * block, DHD), F32)
    return pl.pallas_call(
        body, name="dil_fused_bwd", grid=(dh, nb),
        in_specs=_dil_in_specs(pw, dh, q_cb, block, rev_nb=nb)
        + [pl.BlockSpec((SUBLANES, LANES), lambda h, i: (h, 0)), mspec, ospec, mspec],
        out_specs=[ospec, ospec, ospec], out_shape=[jax.ShapeDtypeStruct((s, dh * DHD), F32)] * 3,
        scratch_shapes=[big(), big(), big(), big(), pltpu.VMEM((block, DHD), F32), pltpu.VMEM((block, DHD), F32)],
        compiler_params=_params("parallel", "arbitrary"),
    )(proj, proj, proj, proj, proj, slopes, datt, lse, delta)


def _concat_bf16(name, a, b, tile=256):
    s, wa = a.shape
    wb = b.shape[1]

    def body(a_ref, b_ref, o_ref):
        o_ref[...] = jnp.concatenate([_bf(a_ref[...]), _bf(b_ref[...])], axis=1)

    return pl.pallas_call(
        body, name=name, grid=(s // tile,), in_specs=[_nat(tile, wa), _nat(tile, wb)], out_specs=_nat(tile, wa + wb),
        out_shape=jax.ShapeDtypeStruct((s, wa + wb), BF16), compiler_params=_params("parallel"),
    )(a, b)


def _attn_bwd_prep(datt, out_a, out_b, tile=256):
    s, mixw = datt.shape
    wa = out_a.shape[1]
    heads_a = wa // LANES

    def body(d_ref, a_ref, b_ref, do_ref, dl_ref, dlt_ref):
        d = d_ref[...]
        do_ref[...] = _bf(d)
        prod = d * jnp.concatenate([a_ref[...], b_ref[...]], axis=1)
        for hh in range(mixw // LANES):
            sl = slice(hh * LANES, (hh + 1) * LANES)
            dl = jnp.broadcast_to(jnp.sum(prod[:, sl], axis=-1, keepdims=True), (tile, LANES))
            dl_ref[:, sl] = dl
            if hh < heads_a:
                dlt_ref[hh * SUBLANES:(hh + 1) * SUBLANES, :] = dl.T[0:SUBLANES, :]

    return pl.pallas_call(
        body, name="attn_bwd_prep", grid=(s // tile,),
        in_specs=[_nat(tile, mixw), _nat(tile, wa), _nat(tile, mixw - wa)],
        out_specs=[_nat(tile, mixw), _nat(tile, mixw), pl.BlockSpec((heads_a * SUBLANES, tile), lambda i: (0, i))],
        out_shape=[jax.ShapeDtypeStruct((s, mixw), BF16), jax.ShapeDtypeStruct((s, mixw), F32),
                   jax.ShapeDtypeStruct((heads_a * SUBLANES, s), F32)],
        compiler_params=_params("parallel"),
    )(datt, out_a, out_b)


def _dproj_assemble(proj, dnq, dnkv, dkpe, dqs, dks, dvs, gq, gkv, ql, tile=256):
    s, pw = proj.shape
    dw = dqs[0].shape[1]
    nbr = len(dqs)

    def body(*refs):
        ql_ref, kvl_ref, dnq_ref, dnkv_ref, dkpe_ref = refs[:5]
        br = refs[5:5 + 3 * nbr]
        gq_ref, gkv_ref = refs[5 + 3 * nbr:7 + 3 * nbr]
        dp_ref, dgq_ref, dgkv_ref = refs[7 + 3 * nbr:]

        @pl.when(pl.program_id(0) == 0)
        def _():
            dgq_ref[...] = jnp.zeros_like(dgq_ref)
            dgkv_ref[...] = jnp.zeros_like(dgkv_ref)

        def rms_bwd(x, dy, gg, dg_ref):
            r = lax.rsqrt(jnp.mean(x * x, axis=-1, keepdims=True) + RMS_EPS)
            xh = x * r
            dxh = dy * gg
            dg_ref[0:1, :] += jnp.sum(dy * xh, axis=0, keepdims=True)
            return r * (dxh - xh * jnp.mean(dxh * xh, axis=-1, keepdims=True))

        pieces = [_bf(rms_bwd(ql_ref[...], dnq_ref[...], gq_ref[...], dgq_ref)),
                  _bf(rms_bwd(kvl_ref[...], dnkv_ref[...], gkv_ref[...], dgkv_ref)),
                  _bf(dkpe_ref[...])]
        for k in range(3):
            acc = br[k * nbr][...]
            for r in br[k * nbr + 1:(k + 1) * nbr]:
                acc = acc + r[...]
            pieces.append(_bf(acc))
        dp_ref[...] = jnp.concatenate(pieces, axis=1)

    res = pl.pallas_call(
        body, name="dproj_assemble", grid=(s // tile,),
        in_specs=[_nat(tile, ql, 0), _nat(tile, ql, 1), _nat(tile, ql), _nat(tile, ql), _nat(tile, LANES)]
        + [_nat(tile, dw)] * (3 * nbr) + [_whole((1, ql)), _whole((1, ql))],
        out_specs=[_nat(tile, pw), _whole((SUBLANES, ql)), _whole((SUBLANES, ql))],
        out_shape=[jax.ShapeDtypeStruct((s, pw), BF16), jax.ShapeDtypeStruct((SUBLANES, ql), F32),
                   jax.ShapeDtypeStruct((SUBLANES, ql), F32)],
        compiler_params=_params("arbitrary"),
    )(proj, proj, dnq, dnkv, dkpe, *dqs, *dks, *dvs, gq.reshape(1, ql), gkv.reshape(1, ql))
    return res[0], res[1][0], res[2][0]


def _axpy(name, alpha, a, b, tile=256):
    s, d = a.shape

    def body(a_ref, b_ref, o_ref):
        o_ref[...] = alpha * a_ref[...] + b_ref[...]

    return pl.pallas_call(
        body, name=name, grid=(s // tile,), in_specs=[_nat(tile, d), _nat(tile, d)], out_specs=_nat(tile, d),
        out_shape=jax.ShapeDtypeStruct((s, d), F32), compiler_params=_params("parallel"),
    )(a, b)


def _cmul(ar, ai, br, bi):
    return ar * br - ai * bi, ar * bi + ai * br


def _s5_discretise(a_re, a_im, log_dt, b_re, b_im, n_sq):
    shape = a_re.shape

    def body(ar_ref, ai_ref, ldt_ref, br_ref, bi_ref, abr_ref, abi_ref, apr_ref, api_ref, bbr_ref, bbi_ref):
        ar, ai = ar_ref[...], ai_ref[...]
        dt = jnp.exp(ldt_ref[...])
        e = jnp.exp(ar * dt)
        abr, abi = e * jnp.cos(ai * dt), e * jnp.sin(ai * dt)
        den = ar * ar + ai * ai
        qr = ((abr - 1.0) * ar + abi * ai) / den
        qi = (abi * ar - (abr - 1.0) * ai) / den
        bbr, bbi = _cmul(qr, qi, br_ref[...], bi_ref[...])
        abr_ref[...], abi_ref[...] = abr, abi
        bbr_ref[...], bbi_ref[...] = bbr, bbi
        pr, pi = abr, abi
        for _ in range(n_sq):
            pr, pi = _cmul(pr, pi, pr, pi)
        apr_ref[...], api_ref[...] = pr, pi

    return pl.pallas_call(
        body, name="s5_discretise", out_shape=[jax.ShapeDtypeStruct(shape, F32)] * 6,
        compiler_params=pltpu.CompilerParams(vmem_limit_bytes=VMEM_LIMIT),
    )(a_re, a_im, log_dt, b_re, b_im)


def _s5_discretise_bwd(a16, b16, ag, gab, gbb):
    rows, p = a16[0].shape
    g = rows // S5_GROUP

    def disc(ar, ai, ldt):
        dt = jnp.exp(ldt)
        e = jnp.exp(ar * dt)
        abr, abi = e * jnp.cos(ai * dt), e * jnp.sin(ai * dt)
        den = ar * ar + ai * ai
        inv_r, inv_i = ar / den, -ai / den
        qr, qi = _cmul(abr - 1.0, abi, inv_r, inv_i)
        return dt, abr, abi, inv_r, inv_i, qr, qi

    def body(ar16_ref, ai16_ref, ldt16_ref, br_ref, bi_ref, ar_ref, ai_ref, ldt_ref, gar_ref, gai_ref, gbr_ref, gbi_ref,
             dar_ref, dai_ref, dldt_ref, dbr_ref, dbi_ref):
        _, _, _, _, _, qr16, qi16 = disc(ar16_ref[...], ai16_ref[...], ldt16_ref[...])
        gbr, gbi = gbr_ref[...], gbi_ref[...]
        dbr_ref[...], dbi_ref[...] = _cmul(qr16, -qi16, gbr, gbi)
        cr, ci = _cmul(br_ref[...], -bi_ref[...], gbr, gbi)
        gqr = jnp.sum(cr.reshape(g, S5_GROUP, p), axis=1)
        gqi = jnp.sum(ci.reshape(g, S5_GROUP, p), axis=1)
        ar, ai = ar_ref[...], ai_ref[...]
        dt, abr, abi, inv_r, inv_i, qr, qi = disc(ar, ai, ldt_ref[...])
        t_r, t_i = _cmul(inv_r, -inv_i, gqr, gqi)
        gab_r = gar_ref[...] + t_r
        gab_i = gai_ref[...] + t_i
        qa_r, qa_i = _cmul(qr, qi, inv_r, inv_i)
        a1_r, a1_i = _cmul(qa_r, -qa_i, gqr, gqi)
        gl_r, gl_i = _cmul(abr, -abi, gab_r, gab_i)
        dar_ref[...] = dt * gl_r - a1_r
        dai_ref[...] = dt * gl_i - a1_i
        gdt = jnp.sum(ar * gl_r + ai * gl_i, axis=-1, keepdims=True)
        dldt_ref[...] = gdt * dt[:, 0:1]

    return pl.pallas_call(
        body, name="s5_discretise_bwd",
        out_shape=[jax.ShapeDtypeStruct((g, p), F32), jax.ShapeDtypeStruct((g, p), F32),
                   jax.ShapeDtypeStruct((g, 1), F32), jax.ShapeDtypeStruct((rows, p), F32),
                   jax.ShapeDtypeStruct((rows, p), F32)],
        compiler_params=pltpu.CompilerParams(vmem_limit_bytes=VMEM_LIMIT),
    )(*a16, *b16, *ag, *gab, *gbb)


def _slab_tile(re, im, nsl):
    row = jnp.concatenate([re.reshape(nsl, SLAB_COLS), im.reshape(nsl, SLAB_COLS)], axis=-1)
    return jnp.repeat(row, SUBLANES, axis=0)


def _slab_in_matrix(b_re, b_im, nsl):
    eye = jnp.eye(SLAB_GROUPS, dtype=F32)

    def blk(b):
        b = b.reshape(nsl, SLAB_GROUPS, S5_GROUP, S5_STATE)
        return jnp.einsum('sgcp,gh->sgchp', b, eye).reshape(nsl, LANES, SLAB_COLS)

    return jnp.concatenate([blk(b_re), blk(b_im)], axis=-1)


def _slab_in_extract(m, nsl):
    eye = jnp.eye(SLAB_GROUPS, dtype=F32)

    def ext(x_):
        x_ = x_.reshape(nsl, SLAB_GROUPS, S5_GROUP, SLAB_GROUPS, S5_STATE)
        return jnp.einsum('sgchp,gh->sgcp', x_, eye).reshape(nsl * LANES, S5_STATE)

    return ext(m[..., :SLAB_COLS]), ext(m[..., SLAB_COLS:])


def _slab_out_matrix(c_re, c_im, nsl):
    eye = jnp.eye(SLAB_GROUPS, dtype=F32)

    def blk(c):
        c = c.reshape(nsl, SLAB_GROUPS, S5_GROUP, S5_STATE)
        return jnp.einsum('sgcp,gh->sgphc', c, eye).reshape(nsl, SLAB_COLS, LANES)

    return jnp.concatenate([blk(c_re), -blk(c_im)], axis=1)


def _slab_out_extract(m, nsl):
    eye = jnp.eye(SLAB_GROUPS, dtype=F32)

    def ext(x_):
        x_ = x_.reshape(nsl, SLAB_GROUPS, S5_STATE, SLAB_GROUPS, S5_GROUP)
        return jnp.einsum('sgphc,gh->sgcp', x_, eye).reshape(nsl * SLAB_GROUPS, S5_GROUP, S5_STATE)

    return ext(m[:, :SLAB_COLS]), -ext(m[:, SLAB_COLS:])


def _gelu(y):
    t = jnp.tanh(0.7978845608028654 * (y + 0.044715 * y * y * y))
    return 0.5 * y * (1.0 + t)


def _gelu_grad(y):
    t = jnp.tanh(0.7978845608028654 * (y + 0.044715 * y * y * y))
    return 0.5 * (1.0 + t) + 0.5 * y * (1.0 - t * t) * 0.7978845608028654 * (1.0 + 3.0 * 0.044715 * y * y)


def _scan_rows(ref, n_steps, ar, ai, state, reverse, conj, keep=True):
    sgn = -1.0 if conj else 1.0

    def step(k, carry):
        xr, xi = carry
        t = (n_steps - 1 - k) if reverse else k
        r0 = pl.multiple_of(t * SUBLANES, SUBLANES)
        nr = ar * xr - sgn * ai * xi + ref[pl.ds(r0, SUBLANES), :SLAB_COLS]
        ni = ar * xi + sgn * ai * xr + ref[pl.ds(r0, SUBLANES), SLAB_COLS:]
        if keep:
            ref[pl.ds(r0, SUBLANES), :SLAB_COLS] = nr
            ref[pl.ds(r0, SUBLANES), SLAB_COLS:] = ni
        return nr, ni

    return lax.fori_loop(0, n_steps, step, state, unroll=4)


def _s5_pass1(hp, bblk, ab_tile, rc=1024):
    s, d = hp.shape
    nsl = d // LANES
    rc = min(rc, s)
    nch = s // rc
    w = 2 * SLAB_COLS

    def body(u_ref, b_ref, ab_ref, end_ref, st_ref, x_ref):
        j = pl.program_id(1)

        @pl.when(j == 0)
        def _():
            st_ref[...] = jnp.zeros_like(st_ref)

        x_ref[...] = _dot(_bf(u_ref[...]), b_ref[0], NN)
        xr, xi = _scan_rows(x_ref, rc // SUBLANES, ab_ref[:, :SLAB_COLS], ab_ref[:, SLAB_COLS:],
                            (st_ref[:, :SLAB_COLS], st_ref[:, SLAB_COLS:]), False, False, keep=False)
        st_ref[:, :SLAB_COLS] = xr
        st_ref[:, SLAB_COLS:] = xi

        @pl.when(j == nch - 1)
        def _():
            end_ref[...] = st_ref[...]

    return pl.pallas_call(
        body, name="s5_scan_local", grid=(nsl, nch),
        in_specs=[pl.BlockSpec((rc, LANES), lambda sl, j: (j, sl)), pl.BlockSpec((1, LANES, w), lambda sl, j: (sl, 0, 0)),
                  pl.BlockSpec((SUBLANES, w), lambda sl, j: (sl, 0))],
        out_specs=pl.BlockSpec((SUBLANES, w), lambda sl, j: (sl, 0)),
        out_shape=jax.ShapeDtypeStruct((nsl * SUBLANES, w), F32),
        scratch_shapes=[pltpu.VMEM((SUBLANES, w), F32), pltpu.VMEM((rc, w), F32)],
        compiler_params=_params("parallel", "arbitrary"),
    )(hp, bblk, ab_tile)


def _s5_carry(name, ends, ap_tile, reverse):
    rows, w = ends.shape
    nsl = rows // SUBLANES
    sgn = -1.0 if reverse else 1.0

    def body(e_ref, ap_ref, c_ref):
        pr, pi = ap_ref[0:1, :SLAB_COLS], sgn * ap_ref[0:1, SLAB_COLS:]
        tr = jnp.zeros((1, SLAB_COLS), F32)
        ti = jnp.zeros((1, SLAB_COLS), F32)
        order = range(SUBLANES - 1, -1, -1) if reverse else range(SUBLANES)
        for seg in order:
            c_ref[seg:seg + 1, :SLAB_COLS] = tr
            c_ref[seg:seg + 1, SLAB_COLS:] = ti
            mr, mi = _cmul(pr, pi, tr, ti)
            tr = e_ref[seg:seg + 1, :SLAB_COLS] + mr
            ti = e_ref[seg:seg + 1, SLAB_COLS:] + mi

    spec = pl.BlockSpec((SUBLANES, w), lambda sl: (sl, 0))
    return pl.pallas_call(
        body, name=name, grid=(nsl,), in_specs=[spec, spec], out_specs=spec,
        out_shape=jax.ShapeDtypeStruct((rows, w), F32), compiler_params=_params("parallel"),
    )(ends, ap_tile)


def _s5_pass2(hp, bblk, cin, ab_tile, cblk, dvec, rc=1024):
    s, d = hp.shape
    nsl = d // LANES
    rc = min(rc, s)
    nch = s // rc
    w = 2 * SLAB_COLS

    def body(h_ref, b_ref, cin_ref, ab_ref, c_ref, d_ref, x_ref, y_ref, z_ref, st_ref):
        j = pl.program_id(1)

        @pl.when(j == 0)
        def _():
            st_ref[...] = cin_ref[...]

        hv = h_ref[...]
        x_ref[...] = _dot(_bf(hv), b_ref[0], NN)
        xr, xi = _scan_rows(x_ref, rc // SUBLANES, ab_ref[:, :SLAB_COLS], ab_ref[:, SLAB_COLS:],
                            (st_ref[:, :SLAB_COLS], st_ref[:, SLAB_COLS:]), False, False)
        st_ref[:, :SLAB_COLS] = xr
        st_ref[:, SLAB_COLS:] = xi
        y = _dot(_bf(x_ref[...]), c_ref[0], NN) + d_ref[...] * hv
        y_ref[...] = y
        z_ref[...] = _bf(_gelu(y))

    tile = lambda wd: pl.BlockSpec((rc, wd), lambda sl, j: (j, sl))
    small = pl.BlockSpec((SUBLANES, w), lambda sl, j: (sl, 0))
    return pl.pallas_call(
        body, name="s5_scan_carry_out", grid=(nsl, nch),
        in_specs=[tile(LANES), pl.BlockSpec((1, LANES, w), lambda sl, j: (sl, 0, 0)), small, small,
                  pl.BlockSpec((1, w, LANES), lambda sl, j: (sl, 0, 0)), pl.BlockSpec((1, LANES), lambda sl, j: (0, sl))],
        out_specs=[tile(w), tile(LANES), tile(LANES)],
        out_shape=[jax.ShapeDtypeStruct((s, nsl * w), F32), jax.ShapeDtypeStruct((s, d), F32),
                   jax.ShapeDtypeStruct((s, d), BF16)],
        scratch_shapes=[pltpu.VMEM((SUBLANES, w), F32)],
        compiler_params=_params("parallel", "arbitrary"),
    )(hp, bblk, cin, ab_tile, cblk, dvec)


def _s5_bwd_pass1(dzg, ypre, cblk, ab_tile, hp, rc=1024):
    s, d = hp.shape
    nsl = d // LANES
    rc = min(rc, s)
    nch = s // rc
    w = 2 * SLAB_COLS

    def body(dz_ref, y_ref, c_ref, ab_ref, h_ref, st_out_ref, dy_ref, dd_ref, st_ref, lam_ref):
        j = pl.program_id(1)

        @pl.when(j == 0)
        def _():
            st_ref[...] = jnp.zeros_like(st_ref)
            dd_ref[...] = jnp.zeros_like(dd_ref)

        dy = dz_ref[...] * _gelu_grad(y_ref[...])
        dy_ref[...] = dy
        dd_ref[0:1, :] += jnp.sum(dy * h_ref[...], axis=0, keepdims=True)
        lam_ref[...] = _dot(_bf(dy), c_ref[0], NT)
        lr, li = _scan_rows(lam_ref, rc // SUBLANES, ab_ref[:, :SLAB_COLS], ab_ref[:, SLAB_COLS:],
                            (st_ref[:, :SLAB_COLS], st_ref[:, SLAB_COLS:]), True, True, keep=False)
        st_ref[:, :SLAB_COLS] = lr
        st_ref[:, SLAB_COLS:] = li

        @pl.when(j == nch - 1)
        def _():
            st_out_ref[...] = st_ref[...]

    tile = lambda wd: pl.BlockSpec((rc, wd), lambda sl, j: (nch - 1 - j, sl))
    small = pl.BlockSpec((SUBLANES, w), lambda sl, j: (sl, 0))
    return pl.pallas_call(
        body, name="s5_adjoint_local", grid=(nsl, nch),
        in_specs=[tile(LANES), tile(LANES), pl.BlockSpec((1, w, LANES), lambda sl, j: (sl, 0, 0)), small, tile(LANES)],
        out_specs=[small, tile(LANES), pl.BlockSpec((SUBLANES, LANES), lambda sl, j: (0, sl))],
        out_shape=[jax.ShapeDtypeStruct((nsl * SUBLANES, w), F32),
                   jax.ShapeDtypeStruct((s, d), F32), jax.ShapeDtypeStruct((SUBLANES, d), F32)],
        scratch_shapes=[pltpu.VMEM((SUBLANES, w), F32), pltpu.VMEM((rc, w), F32)],
        compiler_params=_params("parallel", "arbitrary"),
    )(dzg, ypre, cblk, ab_tile, hp)


def _s5_bwd_pass2(dy, cblk, cinl, ab_tile, xtrue, cinx, hp, bblk, dvec, rc=1024):
    s, d = hp.shape
    nsl = d // LANES
    rc = min(rc, s)
    nch = s // rc
    w = 2 * SLAB_COLS
    n_steps = rc // SUBLANES

    def body(dy_ref, c_ref, cl_ref, ab_ref, x_ref, xp_ref, cx_ref, h_ref, b_ref, d_ref,
             du_ref, db_ref, dc_ref, da_ref, st_ref, lam_ref, acc_ref):
        j = pl.program_id(1)

        @pl.when(j == 0)
        def _():
            st_ref[...] = cl_ref[...]
            acc_ref[...] = jnp.zeros_like(acc_ref)
            db_ref[...] = jnp.zeros_like(db_ref)
            dc_ref[...] = jnp.zeros_like(dc_ref)

        ar, ai = ab_ref[:, :SLAB_COLS], ab_ref[:, SLAB_COLS:]
        lam_ref[...] = _dot(_bf(dy_ref[...]), c_ref[0], NT)

        def advance(lr, li, r0):
            nr = ar * lr + ai * li + lam_ref[pl.ds(r0, SUBLANES), :SLAB_COLS]
            ni = ar * li - ai * lr + lam_ref[pl.ds(r0, SUBLANES), SLAB_COLS:]
            lam_ref[pl.ds(r0, SUBLANES), :SLAB_COLS] = nr
            lam_ref[pl.ds(r0, SUBLANES), SLAB_COLS:] = ni
            return nr, ni

        def step(k, carry):
            lr, li, dr, di = carry
            t = n_steps - 1 - k
            nr, ni = advance(lr, li, pl.multiple_of(t * SUBLANES, SUBLANES))
            rx = pl.multiple_of((t - 1) * SUBLANES, SUBLANES)
            xr, xi = x_ref[pl.ds(rx, SUBLANES), :SLAB_COLS], x_ref[pl.ds(rx, SUBLANES), SLAB_COLS:]
            return nr, ni, dr + xr * nr + xi * ni, di + xr * ni - xi * nr

        lr, li, dr, di = lax.fori_loop(
            0, n_steps - 1, step,
            (st_ref[:, :SLAB_COLS], st_ref[:, SLAB_COLS:], acc_ref[:, :SLAB_COLS], acc_ref[:, SLAB_COLS:]), unroll=4)
        lr, li = advance(lr, li, 0)
        st_ref[:, :SLAB_COLS] = lr
        st_ref[:, SLAB_COLS:] = li
        first_chunk = j == nch - 1
        xr = jnp.where(first_chunk, cx_ref[:, :SLAB_COLS], xp_ref[:, :SLAB_COLS])
        xi = jnp.where(first_chunk, cx_ref[:, SLAB_COLS:], xp_ref[:, SLAB_COLS:])
        acc_ref[:, :SLAB_COLS] = dr + xr * lr + xi * li
        acc_ref[:, SLAB_COLS:] = di + xr * li - xi * lr

        lam_b = _bf(lam_ref[...])
        dyv = dy_ref[...]
        db_ref[0] += _dot(_bf(h_ref[...]), lam_b, TN)
        dc_ref[0] += _dot(_bf(dyv), _bf(x_ref[...]), TN)
        du_ref[...] = _dot(lam_b, b_ref[0], NT) + d_ref[...] * dyv

        @pl.when(j == nch - 1)
        def _():
            da_ref[...] = jnp.broadcast_to(jnp.sum(acc_ref[...], axis=0, keepdims=True), (SUBLANES, w))

    sub = rc // SUBLANES
    tile = lambda wd: pl.BlockSpec((rc, wd), lambda sl, j: (nch - 1 - j, sl))
    small = pl.BlockSpec((SUBLANES, w), lambda sl, j: (sl, 0))
    prev = pl.BlockSpec((SUBLANES, w), lambda sl, j: (jnp.maximum((nch - 1 - j) * sub - 1, 0), sl))
    return pl.pallas_call(
        body, name="s5_adjoint_carry_grads", grid=(nsl, nch),
        in_specs=[tile(LANES), pl.BlockSpec((1, w, LANES), lambda sl, j: (sl, 0, 0)), small, small, tile(w), prev, small,
                  tile(LANES), pl.BlockSpec((1, LANES, w), lambda sl, j: (sl, 0, 0)),
                  pl.BlockSpec((1, LANES), lambda sl, j: (0, sl))],
        out_specs=[tile(LANES), pl.BlockSpec((1, LANES, w), lambda sl, j: (sl, 0, 0)),
                   pl.BlockSpec((1, LANES, w), lambda sl, j: (sl, 0, 0)), small],
        out_shape=[jax.ShapeDtypeStruct((s, d), F32), jax.ShapeDtypeStruct((nsl, LANES, w), F32),
                   jax.ShapeDtypeStruct((nsl, LANES, w), F32), jax.ShapeDtypeStruct((nsl * SUBLANES, w), F32)],
        scratch_shapes=[pltpu.VMEM((SUBLANES, w), F32), pltpu.VMEM((rc, w), F32), pltpu.VMEM((SUBLANES, w), F32)],
        compiler_params=_params("parallel", "arbitrary"),
    )(dy, cblk, cinl, ab_tile, xtrue, xtrue, cinx, hp, bblk, dvec)


def _adamw(name, w, g, m, v):
    r, c = w.shape
    tile = r if r * c <= 512 * 1024 else _pick(r, max(SUBLANES, (512 * 1024 // c) // SUBLANES * SUBLANES), q=SUBLANES)
    c1 = 1.0 / (1.0 - ADAM_B1 ** ADAM_STEP)
    c2 = 1.0 / (1.0 - ADAM_B2 ** ADAM_STEP)

    def body(w_ref, g_ref, m_ref, v_ref, d_ref, nm_ref, nv_ref):
        gg = g_ref[...]
        nm = ADAM_B1 * m_ref[...] + (1.0 - ADAM_B1) * gg
        nv = ADAM_B2 * v_ref[...] + (1.0 - ADAM_B2) * gg * gg
        d_ref[...] = -ADAM_LR * ((nm * c1) / (jnp.sqrt(nv * c2) + ADAM_EPS) + ADAM_WD * w_ref[...])
        nm_ref[...] = nm
        nv_ref[...] = nv

    spec = _nat(tile, c)
    return pl.pallas_call(
        body, name=name, grid=(r // tile,), in_specs=[spec] * 4, out_specs=[spec] * 3,
        out_shape=[jax.ShapeDtypeStruct((r, c), F32)] * 3, compiler_params=_params("parallel"),
    )(w, g, m, v)


def _place():
    x, y, c = lax.axis_index("x"), lax.axis_index("y"), lax.axis_index("c")
    return x, y, c, [(1 - x, y), (x, 1 - y), (1 - x, 1 - y)]


_ANY = pl.BlockSpec(memory_space=pl.ANY)


def _gather_weights(shards):
    n = len(shards)

    def body(*refs):
        ins, outs = refs[:n], refs[n:2 * n]
        send_sems, recv_sems, local_sems = refs[2 * n:]
        x, y, c, chips = _place()
        me = 2 * x + y
        sibling = (x, y, 1 - c)
        started = []
        for a in range(n):
            local = pltpu.make_async_copy(ins[a], outs[a].at[me], local_sems.at[a])
            local.start()
            started.append(local)

        def half(a, chip, h):
            hw = ins[a].shape[1] // 2
            return outs[a].at[chip, :, pl.ds(pl.multiple_of(h * hw, LANES), hw)]

        def copy(a, k, src, chip, h, to):
            return pltpu.make_async_remote_copy(
                src_ref=src, dst_ref=half(a, chip, h), send_sem=send_sems.at[a, k], recv_sem=recv_sems.at[a, k],
                device_id=to, device_id_type=MESH)

        sends = []
        for a in range(n):
            hw = ins[a].shape[1] // 2
            mine = ins[a].at[:, pl.ds(pl.multiple_of(c * hw, LANES), hw)]
            for k, chip in enumerate(chips):
                cp = copy(a, k, mine, me, c, (*chip, c))
                cp.start()
                sends.append(cp)
        for a in range(n):
            for k, (cx, cy) in enumerate(chips):
                src_chip = 2 * cx + cy
                copy(a, k, half(a, src_chip, c), src_chip, c, (x, y, c)).wait_recv()
                fwd = copy(a, 3 + k, half(a, src_chip, c), src_chip, c, sibling)
                fwd.start()
                sends.append(fwd)
        for a in range(n):
            for k, (cx, cy) in enumerate(chips):
                src_chip = 2 * cx + cy
                copy(a, 3 + k, half(a, src_chip, 1 - c), src_chip, 1 - c, (x, y, c)).wait_recv()
        for cp in sends:
            cp.wait_send()
        for cp in started:
            cp.wait()

    return pl.pallas_call(
        body, name="gather_weights",
        in_specs=[_ANY] * n, out_specs=[_ANY] * n,
        out_shape=[jax.ShapeDtypeStruct((N_CHIPS,) + s_.shape, s_.dtype) for s_ in shards],
        scratch_shapes=[pltpu.SemaphoreType.DMA((n, 6)), pltpu.SemaphoreType.DMA((n, 6)), pltpu.SemaphoreType.DMA((n,))],

    )(*shards)


def _gather_weights_async(shards):
    n = len(shards)
    srcs = [jax.new_ref(s_, memory_space=pltpu.MemorySpace.HBM) for s_ in shards]
    outs = [jax.empty_ref(jax.ShapeDtypeStruct((N_CHIPS,) + s_.shape, s_.dtype), memory_space=pltpu.MemorySpace.HBM)
            for s_ in shards]

    @pl.kernel(mesh=plsc.ScalarSubcoreMesh(axis_name="seq", num_cores=1), name="gather_weights_async",
               scratch_types=(pltpu.SemaphoreType.DMA((n, 6)), pltpu.SemaphoreType.DMA((n, 6)),
                              pltpu.SemaphoreType.DMA((n,))),
               compiler_params=pltpu.CompilerParams(collective_id=1))
    def launch(send_sems, recv_sems, local_sems):
        x, y, c, chips = _place()
        me = 2 * x + y
        sibling = (x, y, 1 - c)
        barrier = pltpu.get_barrier_semaphore()
        for peer in [sibling] + [(*chip, c) for chip in chips]:
            pl.semaphore_signal(barrier, inc=1, device_id=peer, device_id_type=MESH)
        pl.semaphore_wait(barrier, 4)

        def half(a, chip, h):
            hw = srcs[a].shape[1] // 2
            return outs[a].at[chip, :, pl.ds(pl.multiple_of(h * hw, LANES), hw)]

        def copy(a, k, src, chip, h, to):
            return pltpu.make_async_remote_copy(
                src_ref=src, dst_ref=half(a, chip, h), send_sem=send_sems.at[a, k], recv_sem=recv_sems.at[a, k],
                device_id=to, device_id_type=MESH)

        locals_, sends = [], []
        for a in range(n):
            local = pltpu.make_async_copy(srcs[a], outs[a].at[me], local_sems.at[a])
            local.start()
            locals_.append(local)
            hw = srcs[a].shape[1] // 2
            mine = srcs[a].at[:, pl.ds(pl.multiple_of(c * hw, LANES), hw)]
            for k, chip in enumerate(chips):
                cp = copy(a, k, mine, me, c, (*chip, c))
                cp.start()
                sends.append(cp)
        for a in range(n):
            for k, (cx, cy) in enumerate(chips):
                src_chip = 2 * cx + cy
                copy(a, k, half(a, src_chip, c), src_chip, c, (x, y, c)).wait_recv()
                fwd = copy(a, 3 + k, half(a, src_chip, c), src_chip, c, sibling)
                fwd.start()
                sends.append(fwd)
        for a in range(n):
            for k, (cx, cy) in enumerate(chips):
                src_chip = 2 * cx + cy
                copy(a, 3 + k, half(a, src_chip, 1 - c), src_chip, 1 - c, (x, y, c)).wait_recv()
        for cp in sends:
            cp.wait_send()
        for cp in locals_:
            cp.wait()

    launch()
    return [o[...] for o in outs]


def _on_sequencer(name, cid, inputs, out_shapes, sem_types, peers, body):
    srcs = [jax.new_ref(a, memory_space=pltpu.MemorySpace.HBM) for a in inputs]
    outs = [jax.empty_ref(sd, memory_space=pltpu.MemorySpace.HBM) for sd in out_shapes]

    @pl.kernel(mesh=plsc.ScalarSubcoreMesh(axis_name="seq", num_cores=1), name=name, scratch_types=tuple(sem_types),
               compiler_params=pltpu.CompilerParams(collective_id=cid))
    def launch(*sems):
        x, y, c, chips = _place()
        barrier = pltpu.get_barrier_semaphore()
        ps = peers(x, y, c, chips)
        for peer in ps:
            pl.semaphore_signal(barrier, inc=1, device_id=peer, device_id_type=MESH)
        pl.semaphore_wait(barrier, len(ps))
        body(srcs, outs, *sems)

    launch()
    return [o[...] for o in outs]


def _sibling_only(x, y, c, chips):
    return [(x, y, 1 - c)]


def _same_core_of_other_chips(x, y, c, chips):
    return [(*chip, c) for chip in chips]


def _swap_halves_to_sibling(name, cid, grads):
    n = len(grads)

    def body(ins, outs, send_sems, recv_sems):
        x, y, c, _ = _place()
        cps = []
        for a in range(n):
            hw = ins[a].shape[2] // 2
            src = ins[a].at[:, :, pl.ds(pl.multiple_of((1 - c) * hw, LANES), hw)]
            cp = pltpu.make_async_remote_copy(src_ref=src, dst_ref=outs[a], send_sem=send_sems.at[a],
                                              recv_sem=recv_sems.at[a], device_id=(x, y, 1 - c), device_id_type=MESH)
            cp.start()
            cps.append(cp)
        for cp in cps:
            cp.wait()

    return _on_sequencer(
        name, cid, grads, [jax.ShapeDtypeStruct(g.shape[:2] + (g.shape[2] // 2,), g.dtype) for g in grads],
        [pltpu.SemaphoreType.DMA((n,)), pltpu.SemaphoreType.DMA((n,))], _sibling_only, body)


def _exchange_quarters(name, cid, parts):
    n = len(parts)

    def body(ins, outs, send_sems, recv_sems):
        x, y, c, chips = _place()
        cps = []
        for a in range(n):
            for k, (cx, cy) in enumerate(chips):
                cp = pltpu.make_async_remote_copy(
                    src_ref=ins[a].at[2 * cx + cy], dst_ref=outs[a].at[k], send_sem=send_sems.at[a, k],
                    recv_sem=recv_sems.at[a, k], device_id=(cx, cy, c), device_id_type=MESH)
                cp.start()
                cps.append(cp)
        for cp in cps:
            cp.wait()

    return _on_sequencer(
        name, cid, parts, [jax.ShapeDtypeStruct((3,) + p_.shape[1:], p_.dtype) for p_ in parts],
        [pltpu.SemaphoreType.DMA((n, 3)), pltpu.SemaphoreType.DMA((n, 3))], _same_core_of_other_chips, body)


def _swap_final_halves(name, cid, halves):
    n = len(halves)

    def body(ins, outs, send_sems, recv_sems):
        x, y, c, _ = _place()
        cps = []
        for a in range(n):
            cp = pltpu.make_async_remote_copy(src_ref=ins[a], dst_ref=outs[a], send_sem=send_sems.at[a],
                                              recv_sem=recv_sems.at[a], device_id=(x, y, 1 - c), device_id_type=MESH)
            cp.start()
            cps.append(cp)
        for cp in cps:
            cp.wait()

    return _on_sequencer(
        name, cid, halves, [jax.ShapeDtypeStruct(h.shape, h.dtype) for h in halves],
        [pltpu.SemaphoreType.DMA((n,)), pltpu.SemaphoreType.DMA((n,))], _sibling_only, body)


def _add_half(name, grad, recv):
    nchip, r, cfull = grad.shape
    hw = cfull // 2
    tile = _pick(r, max(BF16_ROWS, (256 * 1024 // hw) // BF16_ROWS * BF16_ROWS), q=BF16_ROWS)
    c = lax.axis_index("c")

    def body(c_ref, g_ref, r_ref, o_ref):
        o_ref[...] = _bf(g_ref[...] + r_ref[...])

    return pl.pallas_call(
        body, name=name,
        grid_spec=pltpu.PrefetchScalarGridSpec(
            num_scalar_prefetch=1, grid=(nchip, r // tile),
            in_specs=[pl.BlockSpec((1, tile, hw), lambda k, i, cr: (k, i, cr[0])),
                      pl.BlockSpec((1, tile, hw), lambda k, i, cr: (k, i, 0))],
            out_specs=pl.BlockSpec((1, tile, hw), lambda k, i, cr: (k, i, 0))),
        out_shape=jax.ShapeDtypeStruct((nchip, r, hw), BF16), compiler_params=_params("parallel", "parallel"),
    )(c.reshape(1).astype(jnp.int32), grad, recv)


def _add_quarters(name, part, recv):
    _, r, hw = part.shape
    tile = _pick(r, max(BF16_ROWS, (256 * 1024 // hw) // BF16_ROWS * BF16_ROWS), q=BF16_ROWS)
    me = 2 * lax.axis_index("x") + lax.axis_index("y")

    def body(me_ref, p_ref, r_ref, o_ref):
        f = lambda v: v.astype(F32)
        o_ref[...] = ((f(p_ref[0]) + f(r_ref[0])) + f(r_ref[1])) + f(r_ref[2])

    return pl.pallas_call(
        body, name=name,
        grid_spec=pltpu.PrefetchScalarGridSpec(
            num_scalar_prefetch=1, grid=(r // tile,),
            in_specs=[pl.BlockSpec((1, tile, hw), lambda i, mr: (mr[0], i, 0)),
                      pl.BlockSpec((3, tile, hw), lambda i, mr: (0, i, 0))],
            out_specs=pl.BlockSpec((tile, hw), lambda i, mr: (i, 0))),
        out_shape=jax.ShapeDtypeStruct((r, hw), F32), compiler_params=_params("parallel"),
    )(me.reshape(1).astype(jnp.int32), part, recv)


class _ReduceScatter:
    def __init__(self, tag, first_cid, grads):
        self.tag, self.cid = tag, first_cid
        self.stacks = [g.reshape(N_CHIPS, g.shape[0] // N_CHIPS, g.shape[1]) for g in grads]

    def start(self, anchor):
        self.stacks, anchor = lax.optimization_barrier((self.stacks, anchor))
        self.recv = _swap_halves_to_sibling(f"rs_swap_halves_{self.tag}", self.cid, self.stacks)
        return anchor

    def exchange(self, anchor):
        parts = [_add_half(f"rs_add_half_{self.tag}{a}", g, r) for a, (g, r) in enumerate(zip(self.stacks, self.recv))]
        self.parts, anchor = lax.optimization_barrier((parts, anchor))
        self.quarters = _exchange_quarters(f"rs_exchange_{self.tag}", self.cid + 1, self.parts)
        return anchor

    def join(self, anchor):
        halves = [_add_quarters(f"rs_add_quarters_{self.tag}{a}", p_, q_)
                  for a, (p_, q_) in enumerate(zip(self.parts, self.quarters))]
        self.halves, anchor = lax.optimization_barrier((halves, anchor))
        self.others = _swap_final_halves(f"rs_swap_final_{self.tag}", self.cid + 2, self.halves)
        return anchor

    def result(self):
        south = lax.axis_index("c") == 0
        return [jnp.concatenate([jnp.where(south, h, o), jnp.where(south, o, h)], axis=1)
                for h, o in zip(self.halves, self.others)]


def _allgather_small(pack):
    m_per, n = pack.shape

    def body(x_ref, out_ref, send_sems, recv_sems, local_sem):
        x, y, c, chips = _place()
        me, sibling = (x, y, c), (x, y, 1 - c)

        def rows(px, py, pc):
            return out_ref.at[pl.ds(pl.multiple_of((4 * px + 2 * py + pc) * m_per, SUBLANES), m_per), :]

        def copy(k, block, to, src=None):
            return pltpu.make_async_remote_copy(
                src_ref=rows(*block) if src is None else src, dst_ref=rows(*block),
                send_sem=send_sems.at[k], recv_sem=recv_sems.at[k], device_id=to, device_id_type=MESH)

        mine = pltpu.make_async_copy(x_ref, rows(*me), local_sem)
        mine.start()
        first = [copy(0, me, sibling, src=x_ref)]
        first += [copy(1 + j, me, (*chip, c), src=x_ref) for j, chip in enumerate(chips)]
        for cp in first:
            cp.start()
        passed = [copy(4 + j, (*chip, c), sibling) for j, chip in enumerate(chips)]
        for j, chip in enumerate(chips):
            copy(1 + j, (*chip, c), me).wait_recv()
            passed[j].start()
        copy(0, sibling, me).wait_recv()
        for j, chip in enumerate(chips):
            copy(4 + j, (*chip, 1 - c), me).wait_recv()
        for cp in first + passed:
            cp.wait_send()
        mine.wait()

    return pl.pallas_call(
        body, name="allgather_small_grads",
        out_shape=jax.ShapeDtypeStruct((N_DEV * m_per, n), pack.dtype),
        in_specs=[pl.BlockSpec(memory_space=pltpu.VMEM)], out_specs=pl.BlockSpec(memory_space=pltpu.VMEM),
        scratch_shapes=[pltpu.SemaphoreType.DMA((7,)), pltpu.SemaphoreType.DMA((7,)), pltpu.SemaphoreType.DMA],
        compiler_params=pltpu.CompilerParams(vmem_limit_bytes=VMEM_LIMIT),
    )(pack)


def _sum_devices(packs, m_per):
    tile = _pick(m_per, 512, q=SUBLANES)
    nt = m_per // tile

    def body(*refs):
        acc = refs[0][...]
        for r in refs[1:N_DEV]:
            acc = acc + r[...]
        refs[N_DEV][...] = acc

    return pl.pallas_call(
        body, name="sum_small_grads", grid=(nt,),
        in_specs=[pl.BlockSpec((tile, LANES), functools.partial(lambda i, k: (k * nt + i, 0), k=k)) for k in range(N_DEV)],
        out_specs=_nat(tile, LANES), out_shape=jax.ShapeDtypeStruct((m_per, LANES), F32),
        compiler_params=_params("parallel"),
    )(*([packs] * N_DEV))


def _tail_fwd(tag, alpha, h_in, adds, mix_gate, ln1, ln2, p_l, w, want_perm):
    h_mid, xh1, rs1, h_mid_b, _ = _ln_fwd(f"ln1_fwd_{tag}", alpha, h_in, adds, mix_gate, *ln1)
    gp = _matmul(f"ple_gate_fwd_{tag}", h_mid_b, w['wg'], 'nn')
    pw = _matmul(f"ple_proj_fwd_{tag}", p_l, w['plet'], 'nt')
    gu = _matmul(f"ffn_in_fwd_{tag}", h_mid_b, w['wit'], 'nt', tn=1408)
    act = _swiglu_fwd(f"swiglu_fwd_{tag}", gu)
    ffn = _matmul(f"ffn_out_fwd_{tag}", act, w['wo'], 'nn', tk=2816)
    h_out, xh2, rs2, _, h_perm = _ln_fwd(f"ln2_fwd_{tag}", alpha, h_mid, [(ffn, 'nat')],
                                         ('nat', (pw, 1, 0), (gp, 1, 0)), *ln2, want_perm=want_perm)
    saved = dict(h_mid_b=h_mid_b, xh1=xh1, rs1=rs1, gp=gp, pw=pw, gu=gu, act=act, xh2=xh2, rs2=rs2)
    return h_out, h_perm, saved


def _tail_bwd(tag, alpha, dparts, sv, ln1_g, ln2_g, p_l, w, mix_gate):
    d = sv['h_mid_b'].shape[1]
    dz2, dz2b, dgate, dg2, db2 = _ln_bwd(f"ln2_bwd_{tag}", dparts, sv['xh2'], sv['rs2'], ln2_g,
                                         gate=('nat', (sv['pw'], 1, 0), (sv['gp'], 1, 0)))
    grads = dict(ln2_g=dg2, ln2_b=db2)
    grads['plet'] = _matmul(f"ple_proj_dw_{tag}", dgate, p_l, 'tn', a_win=(0, d))
    grads['wg'] = _matmul(f"ple_gate_dw_{tag}", sv['h_mid_b'], dgate, 'tn', b_win=(d, d))
    dx_gate = _matmul(f"ple_gate_dx_{tag}", dgate, w['wg'], 'nt', a_win=(d, d))
    dact = _matmul(f"ffn_out_dx_{tag}", dz2b, w['wo'], 'nt', out_dtype=BF16, tn=1408)
    grads['wo'] = _matmul(f"ffn_out_dw_{tag}", sv['act'], dz2b, 'tn', tm=1408)
    dgu = _swiglu_bwd(f"swiglu_bwd_{tag}", sv['gu'], dact)
    grads['wit'] = _matmul(f"ffn_in_dw_{tag}", dgu, sv['h_mid_b'], 'tn')
    dx_ffn = _matmul(f"ffn_in_dx_{tag}", dgu, w['wit'], 'nn', tk=2816)
    res = _ln_bwd(f"ln1_bwd_{tag}", [(dz2, 'nat', alpha), (dx_gate, 'nat', 1.0), (dx_ffn, 'nat', 1.0)],
                  sv['xh1'], sv['rs1'], ln1_g, gate=mix_gate)
    grads['ln1_g'], grads['ln1_b'] = res[-2], res[-1]
    return res[:-2], grads


def kernel(x, p, positions, attn_w_in, mla_q_norm, mla_w_q_b, mla_kv_norm, mla_w_kv_b, attn_w_out, s5_a_re, s5_a_im, s5_log_dt, s5_b_re, s5_b_im, s5_c_re, s5_c_im, s5_d, s5_w_glu, ln1_g, ln1_b, ffn_w_in, ffn_w_out, ple_w, ple_gate_w, ln2_g, ln2_b, loss_target, m_attn_w_in, m_mla_q_norm, m_mla_w_q_b, m_mla_kv_norm, m_mla_w_kv_b, m_attn_w_out, m_s5_a_re, m_s5_a_im, m_s5_log_dt, m_s5_b_re, m_s5_b_im, m_s5_c_re, m_s5_c_im, m_s5_d, m_s5_w_glu, m_ln1_g, m_ln1_b, m_ffn_w_in, m_ffn_w_out, m_ple_w, m_ple_gate_w, m_ln2_g, m_ln2_b, v_attn_w_in, v_mla_q_norm, v_mla_w_q_b, v_mla_kv_norm, v_mla_w_kv_b, v_attn_w_out, v_s5_a_re, v_s5_a_im, v_s5_log_dt, v_s5_b_re, v_s5_b_im, v_s5_c_re, v_s5_c_im, v_s5_d, v_s5_w_glu, v_ln1_g, v_ln1_b, v_ffn_w_in, v_ffn_w_out, v_ple_w, v_ple_gate_w, v_ln2_g, v_ln2_b):
    weights = dict(attn_w_in=attn_w_in, mla_q_norm=mla_q_norm, mla_w_q_b=mla_w_q_b, mla_kv_norm=mla_kv_norm,
                   mla_w_kv_b=mla_w_kv_b, attn_w_out=attn_w_out, s5_a_re=s5_a_re, s5_a_im=s5_a_im, s5_log_dt=s5_log_dt,
                   s5_b_re=s5_b_re, s5_b_im=s5_b_im, s5_c_re=s5_c_re, s5_c_im=s5_c_im, s5_d=s5_d, s5_w_glu=s5_w_glu,
                   ln1_g=ln1_g, ln1_b=ln1_b, ffn_w_in=ffn_w_in, ffn_w_out=ffn_w_out, ple_w=ple_w, ple_gate_w=ple_gate_w,
                   ln2_g=ln2_g, ln2_b=ln2_b)
    m_in = dict(attn_w_in=m_attn_w_in, mla_q_norm=m_mla_q_norm, mla_w_q_b=m_mla_w_q_b, mla_kv_norm=m_mla_kv_norm,
                mla_w_kv_b=m_mla_w_kv_b, attn_w_out=m_attn_w_out, s5_a_re=m_s5_a_re, s5_a_im=m_s5_a_im,
                s5_log_dt=m_s5_log_dt, s5_b_re=m_s5_b_re, s5_b_im=m_s5_b_im, s5_c_re=m_s5_c_re, s5_c_im=m_s5_c_im,
                s5_d=m_s5_d, s5_w_glu=m_s5_w_glu, ln1_g=m_ln1_g, ln1_b=m_ln1_b, ffn_w_in=m_ffn_w_in,
                ffn_w_out=m_ffn_w_out, ple_w=m_ple_w, ple_gate_w=m_ple_gate_w, ln2_g=m_ln2_g, ln2_b=m_ln2_b)
    v_in = dict(attn_w_in=v_attn_w_in, mla_q_norm=v_mla_q_norm, mla_w_q_b=v_mla_w_q_b, mla_kv_norm=v_mla_kv_norm,
                mla_w_kv_b=v_mla_w_kv_b, attn_w_out=v_attn_w_out, s5_a_re=v_s5_a_re, s5_a_im=v_s5_a_im,
                s5_log_dt=v_s5_log_dt, s5_b_re=v_s5_b_re, s5_b_im=v_s5_b_im, s5_c_re=v_s5_c_re, s5_c_im=v_s5_c_im,
                s5_d=v_s5_d, s5_w_glu=v_s5_w_glu, ln1_g=v_ln1_g, ln1_b=v_ln1_b, ffn_w_in=v_ffn_w_in,
                ffn_w_out=v_ffn_w_out, ple_w=v_ple_w, ple_gate_w=v_ple_gate_w, ln2_g=v_ln2_g, ln2_b=v_ln2_b)
    names = list(weights)

    s, d = x.shape[1], x.shape[2]
    depth = ln1_g.shape[0]
    assert depth == 2
    alpha = (2.0 * depth) ** 0.25
    ql, kvl = mla_q_norm.shape[1], mla_kv_norm.shape[1]
    in_cols = N_CHIPS * attn_w_in.shape[2]
    heads = N_CHIPS * mla_w_q_b.shape[2] // (NOPE + ROPE)
    hps = heads // N_CHIPS
    dw = (in_cols - ql - kvl - ROPE) // 3
    dh = dw // DHD
    assert ql % LANES == 0 and kvl == ql and dw % DHD == 0 and heads % N_CHIPS == 0
    ngroups, nstate = s5_a_re.shape[1], s5_a_re.shape[2]
    assert nstate == S5_STATE and ngroups * S5_GROUP == d and d % LANES == 0
    nsl = d // LANES
    seg_len = s // SUBLANES
    n_sq = seg_len.bit_length() - 1
    assert 1 << n_sq == seg_len, "the segment length of the S5 scan must be a power of two"
    for window, dil in DIL_BRANCHES:
        assert window // dil == DIL_STEPS and (s // dil) % DIL_STEPS == 0
    me = 2 * lax.axis_index("x") + lax.axis_index("y")

    xb = x[0]
    target = loss_target[0]
    p_layers = [p[0, 0], p[1, 0]]
    pos = positions[0].astype(F32).reshape(s, 1)
    inv_freq = ROPE_THETA ** (-jnp.arange(ROPE // 2, dtype=F32) / (ROPE // 2))
    invf = jnp.concatenate([inv_freq, inv_freq, jnp.zeros((LANES - ROPE,), F32)]).reshape(1, LANES)
    slopes = 2.0 ** (-8.0 * jnp.arange(1, dh + 1, dtype=F32) / dh)
    slopes = jnp.broadcast_to(jnp.repeat(slopes, SUBLANES)[:, None], (dh * SUBLANES, LANES))

    wqb_t = mla_w_q_b[0].T.reshape(hps, NOPE + ROPE, ql)
    wqb_t = jnp.pad(wqb_t, ((0, 0), (0, QK_PAD - NOPE - ROPE), (0, 0))).reshape(hps * QK_PAD, ql)
    d_cols = max(d // N_CHIPS, 2 * LANES)
    d_pad = jnp.zeros((SUBLANES, d_cols), F32).at[0, :d // N_CHIPS].set(s5_d[0])
    shards = [_bf(attn_w_in[0].T), _bf(wqb_t), _bf(mla_w_kv_b[0].T), _bf(attn_w_out[0]), _bf(s5_w_glu[0].T)]
    for l in range(depth):
        shards += [_bf(ffn_w_in[l].T), _bf(ffn_w_out[l]), _bf(ple_w[l].T), _bf(ple_gate_w[l])]
    shards.append(d_pad)
    first, later = lax.optimization_barrier((list(_gather_weights(shards[:3])), shards[3:]))
    gathered = first + _gather_weights_async(later)
    full = [g.reshape(N_CHIPS * g.shape[1], g.shape[2]) for g in gathered]
    win_t, wqb_t_f, wkv_t, wout, wglu_t = full[:5]
    lw = [dict(wit=full[5 + 4 * l], wo=full[6 + 4 * l], plet=full[7 + 4 * l], wg=full[8 + 4 * l]) for l in range(depth)]
    dvec = full[-1].reshape(N_CHIPS, SUBLANES, d_cols)[:, 0, :d // N_CHIPS].reshape(1, d)
    lat = ql + kvl
    win_t = jnp.concatenate([win_t[:lat + ROPE], jnp.zeros((LANES - ROPE, d), BF16), win_t[lat + ROPE:]], axis=0)
    kpe_cb = lat // LANES
    q_cb = kpe_cb + 1
    a_cb = heads * VDIM // LANES

    xbb = _bf(xb)
    proj = _matmul("attn_in_fwd", xbb, win_t, 'nt', tn=1408)
    nrm = _rms_fwd(proj, ql, kvl, mla_q_norm[0], mla_kv_norm[0])
    q_raw = _matmul("mla_q_up_fwd", nrm, wqb_t_f, 'nt', a_win=(0, ql))
    kv = _matmul("mla_kv_up_fwd", nrm, wkv_t, 'nt', a_win=(ql, kvl))
    qf, kf, vv = _rope_prep(q_raw, kv, proj, kpe_cb, pos, invf, heads)
    out_a, lse_a = _mla_fwd(qf, kf, vv.T, heads)
    out_b, lse_b = _dil_fused_fwd(proj, slopes, dh, q_cb)
    att = _concat_bf16("attn_heads_concat", out_a, out_b)
    mix0 = _matmul("attn_out_fwd", att, wout, 'nn')
    h2, h2p, sv0 = _tail_fwd("l0", alpha, xb, [(mix0, 'nat')], None, (ln1_g[0], ln1_b[0]), (ln2_g[0], ln2_b[0]),
                             p_layers[0], lw[0], want_perm=True)

    rep = lambda a: jnp.repeat(a, S5_GROUP, axis=0)
    ag = (s5_a_re[0], s5_a_im[0], jnp.broadcast_to(s5_log_dt[0][:, None], (ngroups, nstate)))
    a16 = tuple(rep(a) for a in ag)
    b16 = tuple(b[0].transpose(0, 2, 1).reshape(ngroups * S5_GROUP, nstate) for b in (s5_b_re, s5_b_im))
    abr, abi, apr, api, bbr, bbi = _s5_discretise(*a16, *b16, n_sq)
    ab_tile = _slab_tile(abr[::S5_GROUP], abi[::S5_GROUP], nsl)
    ap_tile = _slab_tile(apr[::S5_GROUP], api[::S5_GROUP], nsl)
    bblk = _bf(_slab_in_matrix(bbr.reshape(ngroups, S5_GROUP, nstate), bbi.reshape(ngroups, S5_GROUP, nstate), nsl))
    cblk = _bf(_slab_out_matrix(s5_c_re[0], s5_c_im[0], nsl))
    ends = _s5_pass1(h2p, bblk, ab_tile)
    cinx = _s5_carry("s5_carry_fwd", ends, ap_tile, False)
    xtrue, ypre, zg = _s5_pass2(h2p, bblk, cinx, ab_tile, cblk, dvec)
    vg = _matmul("s5_glu_fwd", zg, wglu_t, 'nt')
    glu_gate = ('perm', (vg, 2, 0), (vg, 2, 1))
    h4, _, sv1 = _tail_fwd("l1", alpha, h2, [], glu_gate, (ln1_g[1], ln1_b[1]), (ln2_g[1], ln2_b[1]),
                           p_layers[1], lw[1], want_perm=False)
    loss = lax.psum(jnp.sum(_loss_partial(h4, target)), ("x", "y", "c"))

    (dz1_1, _, dvg), g1 = _tail_bwd("l1", alpha, [(h4, 'nat', 1.0 / d), (target, 'nat', -1.0 / d)], sv1, ln1_g[1],
                                    ln2_g[1], p_layers[1], lw[1], glu_gate)
    d_wglu_t = _matmul("s5_glu_dw", dvg, zg, 'tn')
    dzg = _matmul("s5_glu_dx", dvg, wglu_t, 'nn')
    rs_l1 = _ReduceScatter("l1", 2, [d_wglu_t, g1['wit'], g1['wo'], g1['plet'], g1['wg']])
    dzg = rs_l1.start(dzg)
    starts, dy, dd = _s5_bwd_pass1(dzg, ypre, cblk, ab_tile, h2p)
    cinl = _s5_carry("s5_carry_bwd", starts, ap_tile, True)
    du_p, d_bblk, d_cblk, d_ab = _s5_bwd_pass2(dy, cblk, cinl, ab_tile, xtrue, cinx, h2p, bblk, dvec)
    gbb = _slab_in_extract(d_bblk, nsl)
    g_c_re, g_c_im = _slab_out_extract(jnp.swapaxes(d_cblk, 1, 2), nsl)
    d_ab = d_ab[::SUBLANES]
    gab = (d_ab[:, :SLAB_COLS].reshape(ngroups, nstate), d_ab[:, SLAB_COLS:].reshape(ngroups, nstate))
    g_a_re, g_a_im, g_log_dt, g_b_re, g_b_im = _s5_discretise_bwd(a16, b16, ag, gab, gbb)
    unt = lambda b: b.reshape(ngroups, S5_GROUP, nstate).transpose(0, 2, 1)

    du_p = rs_l1.exchange(du_p)
    (dz1_0, dz1_0b), g0 = _tail_bwd("l0", alpha, [(dz1_1, 'nat', alpha), (du_p, 'perm', 1.0)], sv0, ln1_g[0], ln2_g[0],
                                    p_layers[0], lw[0], None)
    dz1_0b = rs_l1.join(dz1_0b)
    rs_l0 = _ReduceScatter("l0", 5, [g0['wit'], g0['wo'], g0['plet'], g0['wg']])
    dz1_0b = rs_l0.start(dz1_0b)
    datt = _matmul("attn_out_dx", dz1_0b, wout, 'nt')
    d_wout = _matmul("attn_out_dw", att, dz1_0b, 'tn')
    do, delta, delta_t = _attn_bwd_prep(datt, out_a, out_b)
    dqf, dkf, dvv = _mla_bwd(qf, kf, vv, do, lse_a, delta_t, heads, 0)
    dqf = rs_l0.exchange(dqf)
    dq_raw, dkv, dkpe = _rope_unprep(dqf, dkf, dvv, pos, invf, heads)
    d_wqb_t = _matmul("mla_q_up_dw", dq_raw, nrm, 'tn', b_win=(0, ql))
    d_wkv_t = _matmul("mla_kv_up_dw", dkv, nrm, 'tn', b_win=(ql, kvl))
    dnq = _matmul("mla_q_up_dx", dq_raw, wqb_t_f, 'nn')
    dnkv = _matmul("mla_kv_up_dx", dkv, wkv_t, 'nn')
    dqd, dkd, dvd = _dil_fused_bwd(proj, slopes, datt, lse_b, delta, dh, q_cb, a_cb)
    dkpe = rs_l0.join(dkpe)
    dproj, g_gq, g_gkv = _dproj_assemble(proj, dnq, dnkv, dkpe, [dqd], [dkd], [dvd], mla_q_norm[0], mla_kv_norm[0], ql)
    d_win_t = _matmul("attn_in_dw", dproj, xbb, 'tn', tm=1408)
    dx_attn = _matmul("attn_in_dx", dproj, win_t, 'nn')
    grad_x = _axpy("grad_x", alpha, dz1_0, dx_attn)

    d_win_t = jnp.concatenate([d_win_t[:lat + ROPE], d_win_t[lat + LANES:]], axis=0)
    rs_at = _ReduceScatter("attn", 8, [d_win_t, d_wqb_t, d_wkv_t, d_wout])
    r_wglu, r_wit1, r_wo1, r_plet1, r_wg1 = rs_l1.result()
    r_wit0, r_wo0, r_plet0, r_wg0 = rs_l0.result()
    grad_x = rs_at.start(grad_x)
    r_wit0, r_wit1, r_wo0, r_wo1 = rs_at.exchange((r_wit0, r_wit1, r_wo0, r_wo1))
    grad_x = rs_at.join(grad_x)
    r_win, r_wqb, r_wkv, r_wout = rs_at.result()
    r_wqb = r_wqb.reshape(hps, QK_PAD, ql)[:, :NOPE + ROPE].reshape(hps * (NOPE + ROPE), ql)
    grads = dict(attn_w_in=r_win.T[None], mla_w_q_b=r_wqb.T[None], mla_w_kv_b=r_wkv.T[None], attn_w_out=r_wout[None],
                 s5_w_glu=r_wglu.T[None],
                 ffn_w_in=jnp.stack([r_wit0.T, r_wit1.T]), ffn_w_out=jnp.stack([r_wo0, r_wo1]),
                 ple_w=jnp.stack([r_plet0.T, r_plet1.T]), ple_gate_w=jnp.stack([r_wg0, r_wg1]))

    small = dict(mla_q_norm=g_gq, mla_kv_norm=g_gkv, s5_a_re=g_a_re, s5_a_im=g_a_im, s5_log_dt=g_log_dt,
                 s5_b_re=unt(g_b_re), s5_b_im=unt(g_b_im), s5_c_re=g_c_re, s5_c_im=g_c_im, s5_d=dd[0],
                 ln1_g=jnp.stack([g0['ln1_g'], g1['ln1_g']]), ln1_b=jnp.stack([g0['ln1_b'], g1['ln1_b']]),
                 ln2_g=jnp.stack([g0['ln2_g'], g1['ln2_g']]), ln2_b=jnp.stack([g0['ln2_b'], g1['ln2_b']]))
    flat = jnp.concatenate([v_.reshape(-1) for v_ in small.values()])
    m_per = -(-flat.shape[0] // (LANES * SUBLANES)) * SUBLANES
    pack = jnp.pad(flat, (0, m_per * LANES - flat.shape[0])).reshape(m_per, LANES)
    total = _sum_devices(_allgather_small(pack), m_per).reshape(-1)
    off = 0
    for k_, v_ in small.items():
        n_ = v_.size
        piece = total[off:off + n_]
        off += n_
        if k_ == 's5_d':
            grads[k_] = lax.dynamic_slice(piece, (me * (d // N_CHIPS),), (d // N_CHIPS,)).reshape(weights[k_].shape)
        else:
            grads[k_] = piece.reshape(weights[k_].shape)

    deltas, new_m, new_v = {}, {}, {}
    for k_ in names:
        w_ = weights[k_]
        shape = w_.shape
        if w_.ndim == 3 and w_.shape[-1] >= LANES:
            two_d = (shape[0] * shape[1], shape[2])
        elif w_.ndim == 4:
            two_d = (shape[0] * shape[1], shape[2] * shape[3])
        else:
            two_d = (1, w_.size) if w_.ndim == 2 and shape[0] == 1 else (shape[0], w_.size // shape[0])
        dl, nm, nv = _adamw(f"adamw_{k_}", w_.reshape(two_d), grads[k_].reshape(two_d), m_in[k_].reshape(two_d),
                            v_in[k_].reshape(two_d))
        deltas[k_], new_m[k_], new_v[k_] = dl.reshape(shape), nm.reshape(shape), nv.reshape(shape)

    return (loss, grad_x[None], *[grads[k_] for k_ in names], *[deltas[k_] for k_ in names],
            *[new_m[k_] for k_ in names], *[new_v[k_] for k_ in names])
```

```python
import functools
import math

import jax
import jax.numpy as jnp
from jax import lax
from jax.experimental import pallas as pl
from jax.experimental.pallas import tpu as pltpu
from jax.experimental.pallas import tpu_sc as plsc

F32 = jnp.float32
BF16 = jnp.bfloat16
MESH = pl.DeviceIdType.MESH

LANES = 128
SUBLANES = 8
BF16_ROWS = 16
VMEM_LIMIT = 48 * 2 ** 20
N_CHIPS = 4
N_DEV = 8

NOPE = 128
ROPE = 64
VDIM = 128
QK_PAD = 256
DHD = 128
DIL_STEPS = 128
DIL_BRANCHES = ((128, 1), (512, 4), (2048, 16))
ROPE_THETA = 10000.0
S5_GROUP = 16
S5_STATE = 64
SLAB_GROUPS = LANES // S5_GROUP
SLAB_COLS = SLAB_GROUPS * S5_STATE
NEG = -1e30
LN_EPS = 1e-5
RMS_EPS = 1e-6

ADAM_LR = 0.001
ADAM_B1 = 0.9
ADAM_B2 = 0.999
ADAM_EPS = 1e-08
ADAM_WD = 0.01
ADAM_STEP = 10

NN = ((1,), (0,))
NT = ((1,), (1,))
TN = ((0,), (0,))


def _dot(a, b, dims):
    return lax.dot_general(a, b, (dims, ((), ())), preferred_element_type=F32)


def _bf(v):
    return v.astype(BF16)


def _pick(n, target, q=LANES, also=0):
    g = math.gcd(n, also) if also else n
    if g <= target and g == n:
        return n
    best = None
    for t in range(q, min(g, target) + 1, q):
        if g % t == 0:
            best = t
    assert best is not None, (n, target, q, also)
    return best


def _params(*sem):
    return pltpu.CompilerParams(dimension_semantics=sem, vmem_limit_bytes=VMEM_LIMIT)


def _sigmoid(v):
    return 1.0 / (1.0 + jnp.exp(-v))


def _matmul(name, a, b, form, out_dtype=F32, a_win=None, b_win=None, tm=1024, tn=1024, tk=2048):
    c0, aw = a_win if a_win else (0, a.shape[1])
    if form == 'nt':
        assert b_win is None
        n, kdim = b.shape
        d0 = 0
    else:
        kdim = b.shape[0]
        d0, n = b_win if b_win else (0, b.shape[1])
    if form == 'tn':
        m = aw
        assert a.shape[0] == kdim, (name, a.shape, b.shape)
        tm = _pick(m, tm, also=c0)
        tk = _pick(kdim, tk)
        a_off = c0 // tm
    else:
        m = a.shape[0]
        assert aw == kdim, (name, a.shape, b.shape, a_win)
        tm = _pick(m, tm)
        tk = _pick(kdim, tk, also=c0)
        a_off = c0 // tk
    tn = _pick(n, tn, also=d0)
    b_off = d0 // tn
    nk = kdim // tk
    dims = {'nn': NN, 'nt': NT, 'tn': TN}[form]

    def body(a_ref, b_ref, o_ref, *acc):
        prod = _dot(_bf(a_ref[...]), _bf(b_ref[...]), dims)
        if nk == 1:
            o_ref[...] = prod.astype(o_ref.dtype)
            return
        acc_ref, = acc
        k = pl.program_id(2)

        @pl.when(k == 0)
        def _():
            acc_ref[...] = prod

        @pl.when((k > 0) & (k < nk - 1))
        def _():
            acc_ref[...] += prod

        @pl.when(k == nk - 1)
        def _():
            o_ref[...] = (acc_ref[...] + prod).astype(o_ref.dtype)

    if form == 'tn':
        a_spec = pl.BlockSpec((tk, tm), lambda i, j, k: (k, i + a_off))
    else:
        a_spec = pl.BlockSpec((tm, tk), lambda i, j, k: (i, k + a_off))
    if form == 'nt':
        b_spec = pl.BlockSpec((tn, tk), lambda i, j, k: (j, k))
    else:
        b_spec = pl.BlockSpec((tk, tn), lambda i, j, k: (k, j + b_off))
    return pl.pallas_call(
        body, name=name,
        grid=(m // tm, n // tn, nk),
        in_specs=[a_spec, b_spec],
        out_specs=pl.BlockSpec((tm, tn), lambda i, j, k: (i, j)),
        out_shape=jax.ShapeDtypeStruct((m, n), out_dtype),
        scratch_shapes=[pltpu.VMEM((tm, tn), F32)] if nk > 1 else [],
        compiler_params=_params("parallel", "parallel", "arbitrary"),
    )(a, b)


def _nat(tile, width, cb=0):
    return pl.BlockSpec((tile, width), lambda i: (i, cb))


def _perm(tile, width, seg_tiles, ncb=1, cb=0):
    return pl.BlockSpec((tile, width), lambda i: (i % seg_tiles, (i // seg_tiles) * ncb + cb))


def _whole(shape):
    return pl.BlockSpec(shape, lambda i: (0,) * len(shape))


def _perm_view(a):
    s, w = a.shape
    return a.reshape(s // SUBLANES, SUBLANES * w)


def _row_spec(a, layout, tile, width, ncb=1, cb=0):
    if layout == 'nat':
        return a, _nat(tile, width, cb)
    seg_tiles = a.shape[0] // SUBLANES // tile
    return _perm_view(a), _perm(tile, width, seg_tiles, ncb, cb)


def _ln_fwd(name, alpha, a, adds, gate, g, b, want_perm=False, tile=256):
    s, d = a.shape
    n_add = len(adds)
    has_gate = gate is not None

    def body(*refs):
        a_ref = refs[0]
        add_refs = refs[1:1 + n_add]
        pos = 1 + n_add
        if has_gate:
            val_ref, pre_ref = refs[pos], refs[pos + 1]
            pos += 2
        g_ref, b_ref = refs[pos], refs[pos + 1]
        outs = refs[pos + 2:]
        z = alpha * a_ref[...]
        for r in add_refs:
            z = z + r[...]
        if has_gate:
            z = z + val_ref[...] * _sigmoid(pre_ref[...])
        mu = jnp.mean(z, axis=-1, keepdims=True)
        zc = z - mu
        var = jnp.mean(zc * zc, axis=-1, keepdims=True)
        rstd = lax.rsqrt(var + LN_EPS)
        xhat = zc * rstd
        h = xhat * g_ref[...] + b_ref[...]
        outs[0][...] = h
        outs[1][...] = xhat
        outs[2][...] = jnp.broadcast_to(rstd, (tile, LANES))
        outs[3][...] = _bf(h)
        if want_perm:
            outs[4][...] = h

    ins, specs = [a], [_nat(tile, d)]
    for arr, layout in adds:
        x_, sp = _row_spec(arr, layout, tile, d)
        ins.append(x_)
        specs.append(sp)
    if has_gate:
        layout = gate[0]
        for arr, ncb, cb in gate[1:]:
            x_, sp = _row_spec(arr, layout, tile, d, ncb=ncb, cb=cb)
            ins.append(x_)
            specs.append(sp)
    ins += [g.reshape(1, d), b.reshape(1, d)]
    specs += [_whole((1, d)), _whole((1, d))]
    out_shape = [jax.ShapeDtypeStruct((s, d), F32), jax.ShapeDtypeStruct((s, d), F32),
                 jax.ShapeDtypeStruct((s, LANES), F32), jax.ShapeDtypeStruct((s, d), BF16)]
    out_specs = [_nat(tile, d), _nat(tile, d), _nat(tile, LANES), _nat(tile, d)]
    if want_perm:
        seg_tiles = s // SUBLANES // tile
        out_shape.append(jax.ShapeDtypeStruct((s // SUBLANES, SUBLANES * d), F32))
        out_specs.append(_perm(tile, d, seg_tiles))
    res = pl.pallas_call(
        body, name=name, grid=(s // tile,), in_specs=specs, out_specs=out_specs, out_shape=out_shape,
        compiler_params=_params("parallel"),
    )(*ins)
    return res[0], res[1], res[2], res[3], (res[4].reshape(s, d) if want_perm else None)


def _ln_bwd(name, dparts, xhat, rstd, g, gate=None, tile=256):
    s, d = xhat.shape
    n_part = len(dparts)
    coefs = [c for _, _, c in dparts]
    has_gate = gate is not None

    def body(*refs):
        part_refs = refs[:n_part]
        xhat_ref, rstd_ref, g_ref = refs[n_part:n_part + 3]
        pos = n_part + 3
        if has_gate:
            val_ref, pre_ref = refs[pos], refs[pos + 1]
            pos += 2
        outs = list(refs[pos:])
        dz_ref = outs.pop(0)
        dzb_ref = outs.pop(0)
        dgate_ref = outs.pop(0) if has_gate else None
        dg_ref, db_ref = outs
        dh = coefs[0] * part_refs[0][...]
        for c, r in zip(coefs[1:], part_refs[1:]):
            dh = dh + c * r[...]
        xh = xhat_ref[...]
        dxh = dh * g_ref[...]
        m1 = jnp.mean(dxh, axis=-1, keepdims=True)
        m2 = jnp.mean(dxh * xh, axis=-1, keepdims=True)
        dz = rstd_ref[:, 0:1] * (dxh - m1 - xh * m2)
        dz_ref[...] = dz
        dzb_ref[...] = _bf(dz)
        if has_gate:
            sg = _sigmoid(pre_ref[...])
            dval = dz * sg
            dpre = dz * val_ref[...] * sg * (1.0 - sg)
            dgate_ref[...] = jnp.concatenate([_bf(dval), _bf(dpre)], axis=1)

        @pl.when(pl.program_id(0) == 0)
        def _():
            dg_ref[...] = jnp.zeros_like(dg_ref)
            db_ref[...] = jnp.zeros_like(db_ref)

        dg_ref[0:1, :] += jnp.sum(dh * xh, axis=0, keepdims=True)
        db_ref[0:1, :] += jnp.sum(dh, axis=0, keepdims=True)

    ins, specs = [], []
    for arr, layout, _ in dparts:
        x_, sp = _row_spec(arr, layout, tile, d)
        ins.append(x_)
        specs.append(sp)
    ins += [xhat, rstd, g.reshape(1, d)]
    specs += [_nat(tile, d), _nat(tile, LANES), _whole((1, d))]
    gate_layout = None
    if has_gate:
        gate_layout = gate[0]
        for arr, ncb, cb in gate[1:]:
            x_, sp = _row_spec(arr, gate_layout, tile, d, ncb=ncb, cb=cb)
            ins.append(x_)
            specs.append(sp)
    seg_tiles = s // SUBLANES // tile
    out_shape = [jax.ShapeDtypeStruct((s, d), F32), jax.ShapeDtypeStruct((s, d), BF16)]
    out_specs = [_nat(tile, d), _nat(tile, d)]
    if has_gate:
        if gate_layout == 'nat':
            out_shape.append(jax.ShapeDtypeStruct((s, 2 * d), BF16))
            out_specs.append(_nat(tile, 2 * d))
        else:
            out_shape.append(jax.ShapeDtypeStruct((s // SUBLANES, SUBLANES * 2 * d), BF16))
            out_specs.append(_perm(tile, 2 * d, seg_tiles))
    out_shape += [jax.ShapeDtypeStruct((SUBLANES, d), F32)] * 2
    out_specs += [_whole((SUBLANES, d))] * 2
    res = list(pl.pallas_call(
        body, name=name, grid=(s // tile,), in_specs=specs, out_specs=out_specs, out_shape=out_shape,
        compiler_params=_params("arbitrary"),
    )(*ins))
    out = [res.pop(0), res.pop(0)]
    if has_gate:
        out.append(res.pop(0).reshape(s, 2 * d))
    out += [res[0][0], res[1][0]]
    return out


def _loss_partial(h, target, tile=256):
    s, d = h.shape

    def body(h_ref, t_ref, o_ref):
        @pl.when(pl.program_id(0) == 0)
        def _():
            o_ref[...] = jnp.zeros_like(o_ref)

        e = h_ref[...] - t_ref[...]
        sq = e * e
        part = sq[:, 0:LANES]
        for k in range(1, d // LANES):
            part = part + sq[:, k * LANES:(k + 1) * LANES]
        o_ref[0:1, :] += jnp.sum(part, axis=0, keepdims=True) * (0.5 / d)

    return pl.pallas_call(
        body, name="loss_partial", grid=(s // tile,), in_specs=[_nat(tile, d), _nat(tile, d)],
        out_specs=_whole((SUBLANES, LANES)), out_shape=jax.ShapeDtypeStruct((SUBLANES, LANES), F32),
        compiler_params=_params("arbitrary"),
    )(h, target)


def _ffn_in_swiglu(name, a, wit, tm=1024, tn=704):
    m, kdim = a.shape
    f = wit.shape[0] // 2
    tm, tn = _pick(m, tm), _pick(f, tn)
    nj = f // tn

    def body(a_ref, bg_ref, bu_ref, g_ref, u_ref, act_ref):
        av = _bf(a_ref[...])
        gg = _dot(av, bg_ref[...], NT)
        uu = _dot(av, bu_ref[...], NT)
        g_ref[...] = gg
        u_ref[...] = uu
        act_ref[...] = _bf(gg * _sigmoid(gg) * uu)

    ospec = pl.BlockSpec((tm, tn), lambda i, j: (i, j))
    return pl.pallas_call(
        body, name=name, grid=(m // tm, nj),
        in_specs=[pl.BlockSpec((tm, kdim), lambda i, j: (i, 0)), pl.BlockSpec((tn, kdim), lambda i, j: (j, 0)),
                  pl.BlockSpec((tn, kdim), lambda i, j: (j + nj, 0))],
        out_specs=[ospec, ospec, ospec],
        out_shape=[jax.ShapeDtypeStruct((m, f), F32), jax.ShapeDtypeStruct((m, f), F32), jax.ShapeDtypeStruct((m, f), BF16)],
        compiler_params=_params("parallel", "parallel"),
    )(a, wit, wit)


def _swiglu_bwd(name, g, u, dact, tile=128):
    s, f = g.shape
    f2 = 2 * f

    def body(g_ref, u_ref, da_ref, o_ref):
        gg = g_ref[...]
        sg = _sigmoid(gg)
        da = da_ref[...].astype(F32)
        silu = gg * sg
        o_ref[:, :f] = _bf(da * u_ref[...] * (sg + silu * (1.0 - sg)))
        o_ref[:, f:] = _bf(da * silu)

    return pl.pallas_call(
        body, name=name, grid=(s // tile,),
        in_specs=[_nat(tile, f), _nat(tile, f), _nat(tile, f)], out_specs=_nat(tile, f2),
        out_shape=jax.ShapeDtypeStruct((s, f2), BF16), compiler_params=_params("parallel"),
    )(g, u, dact)


def _rms_fwd(proj, ql, kvl, gq, gkv, tile=256):
    s = proj.shape[0]
    assert ql == kvl

    def body(q_ref, kv_ref, gq_ref, gkv_ref, o_ref):
        def nrm(x, gg):
            return x * lax.rsqrt(jnp.mean(x * x, axis=-1, keepdims=True) + RMS_EPS) * gg

        o_ref[...] = jnp.concatenate([_bf(nrm(q_ref[...], gq_ref[...])), _bf(nrm(kv_ref[...], gkv_ref[...]))], axis=1)

    return pl.pallas_call(
        body, name="mla_rms_fwd", grid=(s // tile,),
        in_specs=[_nat(tile, ql, 0), _nat(tile, kvl, 1), _whole((1, ql)), _whole((1, kvl))],
        out_specs=_nat(tile, ql + kvl), out_shape=jax.ShapeDtypeStruct((s, ql + kvl), BF16),
        compiler_params=_params("parallel"),
    )(proj, proj, gq.reshape(1, ql), gkv.reshape(1, kvl))


def _rope_coeffs(pos, invf):
    ang = pos * invf
    cs, sn = jnp.cos(ang), jnp.sin(ang)
    lane = lax.broadcasted_iota(jnp.int32, ang.shape, 1)
    half = ROPE // 2
    c = jnp.where(lane < ROPE, cs, 0.0)
    sa = jnp.where(lane < half, -sn, 0.0)
    sb = jnp.where((lane >= half) & (lane < ROPE), sn, 0.0)
    return c, sa, sb


def _rope_prep(q_raw, kv, proj, kpe_cb, pos, invf, heads, tile=256):
    s = q_raw.shape[0]
    half = ROPE // 2

    def body(q_ref, kv_ref, kpe_ref, pos_ref, invf_ref, qf_ref, kf_ref, v_ref):
        c, sa, sb = _rope_coeffs(pos_ref[...], invf_ref[...])

        def rope(t):
            return t * c + pltpu.roll(t, LANES - half, 1) * sa + pltpu.roll(t, half, 1) * sb

        kr = _bf(rope(kpe_ref[...]))
        for hh in range(heads):
            o = hh * QK_PAD
            qf_ref[:, o:o + NOPE] = _bf(q_ref[:, o:o + NOPE])
            qf_ref[:, o + NOPE:o + QK_PAD] = _bf(rope(q_ref[:, o + NOPE:o + QK_PAD]))
            kf_ref[:, o:o + NOPE] = _bf(kv_ref[:, o:o + NOPE])
            kf_ref[:, o + NOPE:o + QK_PAD] = kr
            v_ref[:, hh * VDIM:(hh + 1) * VDIM] = _bf(kv_ref[:, o + NOPE:o + QK_PAD])

    w = heads * QK_PAD
    return pl.pallas_call(
        body, name="mla_rope_prep", grid=(s // tile,),
        in_specs=[_nat(tile, w), _nat(tile, w), _nat(tile, LANES, kpe_cb), _nat(tile, 1), _whole((1, LANES))],
        out_specs=[_nat(tile, w), _nat(tile, w), _nat(tile, heads * VDIM)],
        out_shape=[jax.ShapeDtypeStruct((s, w), BF16), jax.ShapeDtypeStruct((s, w), BF16),
                   jax.ShapeDtypeStruct((s, heads * VDIM), BF16)],
        compiler_params=_params("parallel"),
    )(q_raw, kv, proj, pos, invf)


def _rope_unprep(dqf, dkf, dv, pos, invf, heads, tile=256):
    s = dqf.shape[0]
    half = ROPE // 2

    def body(dq_ref, dk_ref, dv_ref, pos_ref, invf_ref, dqr_ref, dkv_ref, dkpe_ref):
        c, sa, sb = _rope_coeffs(pos_ref[...], invf_ref[...])

        def unrope(gt):
            return gt * c + pltpu.roll(gt * sa, half, 1) + pltpu.roll(gt * sb, LANES - half, 1)

        dkpe = jnp.zeros((tile, LANES), F32)
        for hh in range(heads):
            o = hh * QK_PAD
            dqr_ref[:, o:o + NOPE] = _bf(dq_ref[:, o:o + NOPE])
            dqr_ref[:, o + NOPE:o + QK_PAD] = _bf(unrope(dq_ref[:, o + NOPE:o + QK_PAD]))
            dkv_ref[:, o:o + NOPE] = _bf(dk_ref[:, o:o + NOPE])
            dkv_ref[:, o + NOPE:o + QK_PAD] = _bf(dv_ref[:, hh * VDIM:(hh + 1) * VDIM])
            dkpe = dkpe + dk_ref[:, o + NOPE:o + QK_PAD]
        dkpe_ref[...] = unrope(dkpe)

    w = heads * QK_PAD
    return pl.pallas_call(
        body, name="mla_rope_unprep", grid=(s // tile,),
        in_specs=[_nat(tile, w), _nat(tile, w), _nat(tile, heads * VDIM), _nat(tile, 1), _whole((1, LANES))],
        out_specs=[_nat(tile, w), _nat(tile, w), _nat(tile, LANES)],
        out_shape=[jax.ShapeDtypeStruct((s, w), BF16), jax.ShapeDtypeStruct((s, w), BF16),
                   jax.ShapeDtypeStruct((s, LANES), F32)],
        compiler_params=_params("parallel"),
    )(dqf, dkf, dv, pos, invf)


LOG2E = 1.4426950408889634
MLA_SCALE = (NOPE + ROPE) ** -0.5


def _mla_scores_t(k, q, t, masked):
    sc = _dot(k, q, NT) * (MLA_SCALE * LOG2E)
    if masked:
        row = lax.broadcasted_iota(jnp.int32, (t, t), 0)
        col = lax.broadcasted_iota(jnp.int32, (t, t), 1)
        sc = jnp.where(row <= col, sc, NEG)
    return sc


def _mla_fwd(qf, kf, vt, heads, t=512):
    s = qf.shape[0]
    t = min(t, s)
    nq = s // t

    def body(q_ref, k_ref, vt_ref, o_ref, lse_ref, m_ref, l_ref, acc_ref):
        i = pl.program_id(1)
        m_ref[...] = jnp.full_like(m_ref, NEG)
        l_ref[...] = jnp.zeros_like(l_ref)
        acc_ref[...] = jnp.zeros_like(acc_ref)
        q = q_ref[...]

        def block(j, masked):
            r0 = pl.multiple_of(j * t, t)
            sc = _mla_scores_t(k_ref[pl.ds(r0, t), :], q, t, masked)
            m_prev = m_ref[0:1, :]
            m_new = jnp.maximum(m_prev, jnp.max(sc, axis=0, keepdims=True))
            corr = jnp.exp2(m_prev - m_new)
            p = jnp.exp2(sc - m_new)
            l_new = corr * l_ref[0:1, :] + jnp.sum(p, axis=0, keepdims=True)
            acc_ref[...] = corr * acc_ref[...] + _dot(vt_ref[:, pl.ds(r0, t)], _bf(p), NN)
            m_ref[...] = jnp.broadcast_to(m_new, (SUBLANES, t))
            l_ref[...] = jnp.broadcast_to(l_new, (SUBLANES, t))

        def unmasked(j, carry):
            block(j, False)
            return carry

        lax.fori_loop(0, i, unmasked, 0)
        block(i, True)
        o_ref[...] = (acc_ref[...] / l_ref[0:1, :]).T
        lse_ref[...] = m_ref[...] + jnp.log(l_ref[...]) * LOG2E

    return pl.pallas_call(
        body, name="mla_flash_fwd", grid=(heads, nq),
        in_specs=[pl.BlockSpec((t, QK_PAD), lambda h, i: (i, h)), pl.BlockSpec((s, QK_PAD), lambda h, i: (0, h)),
                  pl.BlockSpec((VDIM, s), lambda h, i: (h, 0))],
        out_specs=[pl.BlockSpec((t, VDIM), lambda h, i: (i, h)), pl.BlockSpec((SUBLANES, t), lambda h, i: (h, i))],
        out_shape=[jax.ShapeDtypeStruct((s, heads * VDIM), F32), jax.ShapeDtypeStruct((heads * SUBLANES, s), F32)],
        scratch_shapes=[pltpu.VMEM((SUBLANES, t), F32), pltpu.VMEM((SUBLANES, t), F32), pltpu.VMEM((VDIM, t), F32)],
        compiler_params=_params("parallel", "arbitrary"),
    )(qf, kf, vt)


def _mla_bwd(qf, kf, v, do, lse_t, delta_t, heads, do_cb0, t=512):
    s = qf.shape[0]
    t = min(t, s)
    nq = s // t

    def body(q_ref, k_ref, v_ref, do_ref, lse_ref, dl_ref, dq_ref, dk_ref, dv_ref, acc_ref):
        i = pl.program_id(1)

        @pl.when(i == 0)
        def _():
            dk_ref[...] = jnp.zeros_like(dk_ref)
            dv_ref[...] = jnp.zeros_like(dv_ref)

        acc_ref[...] = jnp.zeros_like(acc_ref)
        q, dob = q_ref[...], do_ref[...]
        lse, dl = lse_ref[0:1, :], dl_ref[0:1, :]

        def block(j, masked):
            r0 = pl.multiple_of(j * t, t)
            k = k_ref[pl.ds(r0, t), :]
            p = jnp.exp2(_mla_scores_t(k, q, t, masked) - lse)
            dp = _dot(v_ref[pl.ds(r0, t), :], dob, NT)
            ds = _bf(p * (dp - dl) * MLA_SCALE)
            acc_ref[...] += _dot(ds, k, TN)
            dk_ref[pl.ds(r0, t), :] += _dot(ds, q, NN)
            dv_ref[pl.ds(r0, t), :] += _dot(_bf(p), dob, NN)

        def unmasked(j, carry):
            block(j, False)
            return carry

        lax.fori_loop(0, i, unmasked, 0)
        block(i, True)
        dq_ref[...] = acc_ref[...]

    qs = lambda w, off=0: pl.BlockSpec((t, w), lambda h, i: (i, h + off))
    ks = lambda w: pl.BlockSpec((s, w), lambda h, i: (0, h))
    st = pl.BlockSpec((SUBLANES, t), lambda h, i: (h, i))
    return pl.pallas_call(
        body, name="mla_flash_bwd", grid=(heads, nq),
        in_specs=[qs(QK_PAD), ks(QK_PAD), ks(VDIM), qs(VDIM, do_cb0), st, st],
        out_specs=[qs(QK_PAD), ks(QK_PAD), ks(VDIM)],
        out_shape=[jax.ShapeDtypeStruct((s, heads * QK_PAD), F32), jax.ShapeDtypeStruct((s, heads * QK_PAD), F32),
                   jax.ShapeDtypeStruct((s, heads * VDIM), F32)],
        scratch_shapes=[pltpu.VMEM((t, QK_PAD), F32)],
        compiler_params=_params("parallel", "arbitrary"),
    )(qf, kf, v, do, lse_t, delta_t)


def _band_mask(tq, first_block):
    row = lax.broadcasted_iota(jnp.int32, (tq, DIL_STEPS + tq), 0)
    col = lax.broadcasted_iota(jnp.int32, (tq, DIL_STEPS + tq), 1)
    dist = row + DIL_STEPS - col
    valid = (dist >= 0) & (dist <= DIL_STEPS) & (jnp.logical_not(first_block) | (col >= DIL_STEPS))
    return dist, valid


def _dil_scores(q, kp, kc, slope, dil, tq, first_block):
    sc = jnp.concatenate([_dot(q, kp, NT), _dot(q, kc, NT)], axis=1) * (DHD ** -0.5)
    dist, valid = _band_mask(tq, first_block)
    return jnp.where(valid, sc - slope * (dil * dist).astype(F32), NEG)


def _dil_specs(proj_w, dh, tq):
    pwb = proj_w // LANES
    r_of = lambda cb: cb // dh
    h_of = lambda cb: cb % dh
    cur = lambda off: pl.BlockSpec((tq, DHD), lambda cb, i: (i, r_of(cb) * pwb + off + h_of(cb)))
    prev = lambda off: pl.BlockSpec(
        (DIL_STEPS, DHD), lambda cb, i: (jnp.maximum(i * (tq // DIL_STEPS) - 1, 0), r_of(cb) * pwb + off + h_of(cb)))
    return cur, prev


def _dil_fwd(name, proj, slopes, dil, dh, q_cb, tq=512):
    s, pw = proj.shape
    l = s // dil
    tq = min(tq, l)
    nb = l // tq
    k_cb, v_cb = q_cb + dh, q_cb + 2 * dh
    cur, prev = _dil_specs(pw, dh, tq)
    pv = proj.reshape(l, dil * pw)

    def body(q_ref, kc_ref, kp_ref, vc_ref, vp_ref, sl_ref, o_ref, lse_ref):
        i = pl.program_id(1)
        sc = _dil_scores(_bf(q_ref[...]), _bf(kp_ref[...]), _bf(kc_ref[...]), sl_ref[0:1, 0:1], dil, tq, i == 0)
        m = jnp.max(sc, axis=-1, keepdims=True)
        e = jnp.exp(sc - m)
        lsum = jnp.sum(e, axis=-1, keepdims=True)
        pn = e / lsum
        o_ref[...] = (_dot(_bf(pn[:, :DIL_STEPS]), _bf(vp_ref[...]), NN)
                      + _dot(_bf(pn[:, DIL_STEPS:]), _bf(vc_ref[...]), NN))
        lse_ref[...] = jnp.broadcast_to(m + jnp.log(lsum), (tq, LANES))

    ospec = pl.BlockSpec((tq, DHD), lambda cb, i: (i, cb))
    o, lse = pl.pallas_call(
        body, name=name, grid=(dil * dh, nb),
        in_specs=[cur(q_cb), cur(k_cb), prev(k_cb), cur(v_cb), prev(v_cb),
                  pl.BlockSpec((SUBLANES, LANES), lambda cb, i: (cb % dh, 0))],
        out_specs=[ospec, ospec],
        out_shape=[jax.ShapeDtypeStruct((l, dil * dh * DHD), F32)] * 2,
        compiler_params=_params("parallel", "parallel"),
    )(pv, pv, pv, pv, pv, slopes)
    return o.reshape(s, dh * DHD), lse.reshape(s, dh * DHD)


def _dil_bwd_dq(name, proj, slopes, do, lse, delta, dil, dh, q_cb, b_cb0, tq=512):
    s, pw = proj.shape
    mixw = do.shape[1]
    l = s // dil
    tq = min(tq, l)
    nb = l // tq
    k_cb, v_cb = q_cb + dh, q_cb + 2 * dh
    cur, prev = _dil_specs(pw, dh, tq)
    pv = proj.reshape(l, dil * pw)
    mb = mixw // LANES
    mspec = pl.BlockSpec((tq, DHD), lambda cb, i: (i, (cb // dh) * mb + b_cb0 + cb % dh))
    ospec = pl.BlockSpec((tq, DHD), lambda cb, i: (i, cb))

    def body(q_ref, kc_ref, kp_ref, vc_ref, vp_ref, sl_ref, do_ref, lse_ref, dl_ref, dq_ref):
        i = pl.program_id(1)
        kp, kc = _bf(kp_ref[...]), _bf(kc_ref[...])
        sc = _dil_scores(_bf(q_ref[...]), kp, kc, sl_ref[0:1, 0:1], dil, tq, i == 0)
        p = jnp.exp(sc - lse_ref[:, 0:1])
        dob = do_ref[...]
        dp = jnp.concatenate([_dot(dob, _bf(vp_ref[...]), NT), _dot(dob, _bf(vc_ref[...]), NT)], axis=1)
        ds = _bf(p * (dp - dl_ref[:, 0:1]) * (DHD ** -0.5))
        dq_ref[...] = _dot(ds[:, :DIL_STEPS], kp, NN) + _dot(ds[:, DIL_STEPS:], kc, NN)

    dq = pl.pallas_call(
        body, name=name, grid=(dil * dh, nb),
        in_specs=[cur(q_cb), cur(k_cb), prev(k_cb), cur(v_cb), prev(v_cb),
                  pl.BlockSpec((SUBLANES, LANES), lambda cb, i: (cb % dh, 0)), mspec, ospec, mspec],
        out_specs=ospec, out_shape=jax.ShapeDtypeStruct((l, dil * dh * DHD), F32),
        compiler_params=_params("parallel", "parallel"),
    )(pv, pv, pv, pv, pv, slopes, do.reshape(l, dil * mixw), lse.reshape(l, dil * dh * DHD), delta.reshape(l, dil * mixw))
    return dq.reshape(s, dh * DHD)


def _dil_bwd_dkv(name, proj, slopes, do, lse, delta, dil, dh, q_cb, b_cb0, tk=512):
    s, pw = proj.shape
    mixw = do.shape[1]
    l = s // dil
    tk = min(tk, l)
    nb = l // tk
    k_cb, v_cb = q_cb + dh, q_cb + 2 * dh
    pwb, mb = pw // LANES, mixw // LANES
    sub = tk // DIL_STEPS
    last128 = l // DIL_STEPS - 1
    pv = proj.reshape(l, dil * pw)

    def cur(width_blocks, off):
        return pl.BlockSpec((tk, DHD), lambda cb, j: (j, (cb // dh) * width_blocks + off + cb % dh))

    def nxt(width_blocks, off):
        return pl.BlockSpec((DIL_STEPS, DHD), lambda cb, j: (jnp.minimum((j + 1) * sub, last128),
                                                               (cb // dh) * width_blocks + off + cb % dh))

    ocur = pl.BlockSpec((tk, DHD), lambda cb, j: (j, cb))
    onxt = pl.BlockSpec((DIL_STEPS, DHD), lambda cb, j: (jnp.minimum((j + 1) * sub, last128), cb))

    def body(k_ref, v_ref, qc_ref, qn_ref, sl_ref, doc_ref, don_ref, lsec_ref, lsen_ref, dlc_ref, dln_ref,
             dk_ref, dv_ref):
        j = pl.program_id(1)
        slope = sl_ref[0:1, 0:1]
        scale = DHD ** -0.5
        k, v = _bf(k_ref[...]), _bf(v_ref[...])
        qc = _bf(qc_ref[...])
        row = lax.broadcasted_iota(jnp.int32, (tk, tk), 0)
        col = lax.broadcasted_iota(jnp.int32, (tk, tk), 1)
        dist = row - col
        valid = (dist >= 0) & (dist <= DIL_STEPS)
        sc = jnp.where(valid, _dot(qc, k, NT) * scale - slope * (dil * dist).astype(F32), NEG)
        p = jnp.exp(sc - lsec_ref[:, 0:1])
        doc = doc_ref[...]
        ds = _bf(p * (_dot(doc, v, NT) - dlc_ref[:, 0:1]) * scale)
        dv_ref[...] = _dot(_bf(p), doc, TN)
        dk_ref[...] = _dot(ds, qc, TN)
        kl, vl = k[tk - DIL_STEPS:, :], v[tk - DIL_STEPS:, :]
        qn = _bf(qn_ref[...])
        row = lax.broadcasted_iota(jnp.int32, (DIL_STEPS, DIL_STEPS), 0)
        col = lax.broadcasted_iota(jnp.int32, (DIL_STEPS, DIL_STEPS), 1)
        dist = DIL_STEPS + row - col
        valid = (dist <= DIL_STEPS) & (j < nb - 1)
        sc = jnp.where(valid, _dot(qn, kl, NT) * scale - slope * (dil * dist).astype(F32), NEG)
        p = jnp.exp(sc - lsen_ref[:, 0:1])
        don = don_ref[...]
        ds = _bf(p * (_dot(don, vl, NT) - dln_ref[:, 0:1]) * scale)
        dv_ref[tk - DIL_STEPS:, :] += _dot(_bf(p), don, TN)
        dk_ref[tk - DIL_STEPS:, :] += _dot(ds, qn, TN)

    dov = do.reshape(l, dil * mixw)
    dlv = delta.reshape(l, dil * mixw)
    lsv = lse.reshape(l, dil * dh * DHD)
    dk, dv = pl.pallas_call(
        body, name=name, grid=(dil * dh, nb),
        in_specs=[cur(pwb, k_cb), cur(pwb, v_cb), cur(pwb, q_cb), nxt(pwb, q_cb),
                  pl.BlockSpec((SUBLANES, LANES), lambda cb, j: (cb % dh, 0)),
                  cur(mb, b_cb0), nxt(mb, b_cb0), ocur, onxt, cur(mb, b_cb0), nxt(mb, b_cb0)],
        out_specs=[ocur, ocur], out_shape=[jax.ShapeDtypeStruct((l, dil * dh * DHD), F32)] * 2,
        compiler_params=_params("parallel", "parallel"),
    )(pv, pv, pv, pv, slopes, dov, dov, lsv, lsv, dlv, dlv)
    return dk.reshape(s, dh * DHD), dv.reshape(s, dh * DHD)


def _dil_merge(out_a, outs, lses, tile=256):
    s, wa = out_a.shape
    wb = outs[0].shape[1]
    nbr = len(outs)

    def body(*refs):
        a_ref = refs[0]
        o_refs, l_refs = refs[1:1 + nbr], refs[1 + nbr:1 + 2 * nbr]
        att_ref, ob_ref, lse_ref = refs[1 + 2 * nbr:]
        ls = [r[...] for r in l_refs]
        m = ls[0]
        for x_ in ls[1:]:
            m = jnp.maximum(m, x_)
        es = [jnp.exp(x_ - m) for x_ in ls]
        tot = es[0]
        for e in es[1:]:
            tot = tot + e
        ob = (es[0] / tot) * o_refs[0][...]
        for e, r in zip(es[1:], o_refs[1:]):
            ob = ob + (e / tot) * r[...]
        ob_ref[...] = ob
        lse_ref[...] = m + jnp.log(tot)
        att_ref[...] = jnp.concatenate([_bf(a_ref[...]), _bf(ob)], axis=1)

    return pl.pallas_call(
        body, name="dil_merge", grid=(s // tile,),
        in_specs=[_nat(tile, wa)] + [_nat(tile, wb)] * (2 * nbr),
        out_specs=[_nat(tile, wa + wb), _nat(tile, wb), _nat(tile, wb)],
        out_shape=[jax.ShapeDtypeStruct((s, wa + wb), BF16), jax.ShapeDtypeStruct((s, wb), F32),
                   jax.ShapeDtypeStruct((s, wb), F32)],
        compiler_params=_params("parallel"),
    )(out_a, *outs, *lses)


DIL_BLOCK = 2048


def _dil_unit_rows(u, dil, block):
    sub = u // dil
    return u % dil + (dil * DIL_STEPS) * sub, sub == 0


def _dil_rows(base, dil):
    return pl.ds(base, DIL_STEPS, stride=dil) if dil > 1 else pl.ds(base, DIL_STEPS)


def _dil_unit_scores(q, kp, kc, slope, dil, no_prev):
    sc = jnp.concatenate([_dot(q, kp, NT), _dot(q, kc, NT)], axis=1) * (DHD ** -0.5)
    row = lax.broadcasted_iota(jnp.int32, (DIL_STEPS, 2 * DIL_STEPS), 0)
    col = lax.broadcasted_iota(jnp.int32, (DIL_STEPS, 2 * DIL_STEPS), 1)
    dist = row + DIL_STEPS - col
    valid = (dist >= 0) & (dist <= DIL_STEPS) & (jnp.logical_not(no_prev) | (col >= DIL_STEPS))
    return jnp.where(valid, sc - slope * (dil * dist).astype(F32), NEG)


def _dil_in_specs(pw, dh, q_cb, block, rev_nb=None):
    blk = (lambda i: i) if rev_nb is None else (lambda i: rev_nb - 1 - i)
    own = lambda off: pl.BlockSpec((block, DHD), lambda h, i: (blk(i), off + h))
    prev = lambda off: pl.BlockSpec((block, DHD), lambda h, i: (jnp.maximum(blk(i) - 1, 0), off + h))
    return [own(q_cb), own(q_cb + dh), prev(q_cb + dh), own(q_cb + 2 * dh), prev(q_cb + 2 * dh)]


def _dil_fused_fwd(proj, slopes, dh, q_cb):
    s, pw = proj.shape
    block = min(DIL_BLOCK, s)
    nb = s // block
    n_units = block // DIL_STEPS
    nbr = len(DIL_BRANCHES)
    assert block >= DIL_STEPS * max(d for _, d in DIL_BRANCHES)

    def body(q_ref, kc_ref, kp_ref, vc_ref, vp_ref, sl_ref, o_ref, lse_ref, kk, vv, *per_branch):
        og, mg, lg = per_branch[:nbr], per_branch[nbr:2 * nbr], per_branch[2 * nbr:]
        i = pl.program_id(1)
        kk[0:block, :] = kp_ref[...]
        kk[block:, :] = kc_ref[...]
        vv[0:block, :] = vp_ref[...]
        vv[block:, :] = vc_ref[...]
        slope = sl_ref[0:1, 0:1]
        for g, (_, dil) in enumerate(DIL_BRANCHES):
            rows = functools.partial(_dil_rows, dil=dil)

            def unit(u, dil=dil, rows=rows):
                q0, first = _dil_unit_rows(u, dil, block)
                q = _bf(q_ref[rows(q0), :])
                kc, kp = _bf(kk[rows(block + q0), :]), _bf(kk[rows(block + q0 - dil * DIL_STEPS), :])
                vc, vp = _bf(vv[rows(block + q0), :]), _bf(vv[rows(block + q0 - dil * DIL_STEPS), :])
                sc = _dil_unit_scores(q, kp, kc, slope, dil, first & (i == 0))
                m = jnp.max(sc, axis=-1, keepdims=True)
                e = jnp.exp(sc - m)
                o = _dot(_bf(e[:, :DIL_STEPS]), vp, NN) + _dot(_bf(e[:, DIL_STEPS:]), vc, NN)
                return q0, o, m, jnp.sum(e, axis=-1, keepdims=True)

            def pair(u, carry, g=g, rows=rows, unit=unit):
                for q0, o, m, lsum in (unit(u), unit(u + n_units // 2)):
                    og[g][rows(q0), :] = o
                    mg[g][rows(q0), :] = jnp.broadcast_to(m, (DIL_STEPS, LANES))
                    lg[g][rows(q0), :] = jnp.broadcast_to(lsum, (DIL_STEPS, LANES))
                return carry

            lax.fori_loop(0, n_units // 2, pair, 0, unroll=2)
        m_all = mg[0][...]
        for g in range(1, nbr):
            m_all = jnp.maximum(m_all, mg[g][...])
        tot = jnp.zeros((block, LANES), F32)
        acc = jnp.zeros((block, DHD), F32)
        for g in range(nbr):
            w = jnp.exp(mg[g][...] - m_all)
            tot = tot + w * lg[g][...]
            acc = acc + w * og[g][...]
        o_ref[...] = acc / tot
        lse_ref[...] = m_all + jnp.log(tot)

    ospec = pl.BlockSpec((block, DHD), lambda h, i: (i, h))
    return pl.pallas_call(
        body, name="dil_fused_fwd", grid=(dh, nb),
        in_specs=_dil_in_specs(pw, dh, q_cb, block) + [pl.BlockSpec((SUBLANES, LANES), lambda h, i: (h, 0))],
        out_specs=[ospec, ospec], out_shape=[jax.ShapeDtypeStruct((s, dh * DHD), F32)] * 2,
        scratch_shapes=[pltpu.VMEM((2 * block, DHD), F32), pltpu.VMEM((2 * block, DHD), F32)]
        + [pltpu.VMEM((block, DHD), F32)] * (3 * nbr),
        compiler_params=_params("parallel", "arbitrary"),
    )(proj, proj, proj, proj, proj, slopes)


def _dil_fused_bwd(proj, slopes, datt, lse, delta, dh, q_cb, b_cb0):
    s, pw = proj.shape
    block = min(DIL_BLOCK, s)
    nb = s // block
    n_units = block // DIL_STEPS
    scale = DHD ** -0.5

    def body(q_ref, kc_ref, kp_ref, vc_ref, vp_ref, sl_ref, do_ref, lse_ref, dl_ref, dq_ref, dk_ref, dv_ref,
             kk, vv, dkk, dvv, carry_k, carry_v):
        ii = pl.program_id(1)
        i = nb - 1 - ii

        @pl.when(ii == 0)
        def _():
            carry_k[...] = jnp.zeros_like(carry_k)
            carry_v[...] = jnp.zeros_like(carry_v)

        kk[0:block, :] = kp_ref[...]
        kk[block:, :] = kc_ref[...]
        vv[0:block, :] = vp_ref[...]
        vv[block:, :] = vc_ref[...]
        dkk[...] = jnp.zeros_like(dkk)
        dvv[...] = jnp.zeros_like(dvv)
        dq_ref[...] = jnp.zeros_like(dq_ref)
        slope = sl_ref[0:1, 0:1]
        for _, dil in DIL_BRANCHES:
            rows = functools.partial(_dil_rows, dil=dil)

            def unit(u, dil=dil, rows=rows):
                q0, first = _dil_unit_rows(u, dil, block)
                cur, prev = rows(block + q0), rows(block + q0 - dil * DIL_STEPS)
                q = _bf(q_ref[rows(q0), :])
                kc, kp, vc, vp = _bf(kk[cur, :]), _bf(kk[prev, :]), _bf(vv[cur, :]), _bf(vv[prev, :])
                dob = _bf(do_ref[rows(q0), :])
                sc = _dil_unit_scores(q, kp, kc, slope, dil, first & (i == 0))
                p = jnp.exp(sc - lse_ref[rows(q0), 0:1])
                dp = jnp.concatenate([_dot(dob, vp, NT), _dot(dob, vc, NT)], axis=1)
                ds = _bf(p * (dp - dl_ref[rows(q0), 0:1]) * scale)
                pb = _bf(p)
                return (rows(q0), cur, prev, _dot(ds[:, :DIL_STEPS], kp, NN) + _dot(ds[:, DIL_STEPS:], kc, NN),
                        _dot(ds[:, :DIL_STEPS], q, TN), _dot(ds[:, DIL_STEPS:], q, TN),
                        _dot(pb[:, :DIL_STEPS], dob, TN), _dot(pb[:, DIL_STEPS:], dob, TN))

            def pair(u, carry, unit=unit):
                for qrows, cur, prev, dq, dkp, dkc, dvp, dvc in (unit(u), unit(u + n_units // 2)):
                    dq_ref[qrows, :] += dq
                    dkk[prev, :] += dkp
                    dkk[cur, :] += dkc
                    dvv[prev, :] += dvp
                    dvv[cur, :] += dvc
                return carry

            lax.fori_loop(0, n_units // 2, pair, 0, unroll=2)
        dk_ref[...] = dkk[block:, :] + carry_k[...]
        dv_ref[...] = dvv[block:, :] + carry_v[...]
        carry_k[...] = dkk[0:block, :]
        carry_v[...] = dvv[0:block, :]

    rev = lambda i: nb - 1 - i
    mspec = pl.BlockSpec((block, DHD), lambda h, i: (rev(i), b_cb0 + h))
    ospec = pl.BlockSpec((block, DHD), lambda h, i: (rev(i), h))
    big = lambda: pltpu.VMEM((2 * block, DHD), F32)
    return pl.pallas_call(
        body, name="dil_fused_bwd", grid=(dh, nb),
        in_specs=_dil_in_specs(pw, dh, q_cb, block, rev_nb=nb)
        + [pl.BlockSpec((SUBLANES, LANES), lambda h, i: (h, 0)), mspec, ospec, mspec],
        out_specs=[ospec, ospec, ospec], out_shape=[jax.ShapeDtypeStruct((s, dh * DHD), F32)] * 3,
        scratch_shapes=[big(), big(), big(), big(), pltpu.VMEM((block, DHD), F32), pltpu.VMEM((block, DHD), F32)],
        compiler_params=_params("parallel", "arbitrary"),
    )(proj, proj, proj, proj, proj, slopes, datt, lse, delta)


def _concat_bf16(name, a, b, tile=256):
    s, wa = a.shape
    wb = b.shape[1]

    def body(a_ref, b_ref, o_ref):
        o_ref[...] = jnp.concatenate([_bf(a_ref[...]), _bf(b_ref[...])], axis=1)

    return pl.pallas_call(
        body, name=name, grid=(s // tile,), in_specs=[_nat(tile, wa), _nat(tile, wb)], out_specs=_nat(tile, wa + wb),
        out_shape=jax.ShapeDtypeStruct((s, wa + wb), BF16), compiler_params=_params("parallel"),
    )(a, b)


def _attn_bwd_prep(datt, out_a, out_b, tile=256):
    s, mixw = datt.shape
    wa = out_a.shape[1]
    heads_a = wa // LANES

    def body(d_ref, a_ref, b_ref, do_ref, dl_ref, dlt_ref):
        d = d_ref[...]
        do_ref[...] = _bf(d)
        prod = d * jnp.concatenate([a_ref[...], b_ref[...]], axis=1)
        for hh in range(mixw // LANES):
            sl = slice(hh * LANES, (hh + 1) * LANES)
            dl = jnp.broadcast_to(jnp.sum(prod[:, sl], axis=-1, keepdims=True), (tile, LANES))
            dl_ref[:, sl] = dl
            if hh < heads_a:
                dlt_ref[hh * SUBLANES:(hh + 1) * SUBLANES, :] = dl.T[0:SUBLANES, :]

    return pl.pallas_call(
        body, name="attn_bwd_prep", grid=(s // tile,),
        in_specs=[_nat(tile, mixw), _nat(tile, wa), _nat(tile, mixw - wa)],
        out_specs=[_nat(tile, mixw), _nat(tile, mixw), pl.BlockSpec((heads_a * SUBLANES, tile), lambda i: (0, i))],
        out_shape=[jax.ShapeDtypeStruct((s, mixw), BF16), jax.ShapeDtypeStruct((s, mixw), F32),
                   jax.ShapeDtypeStruct((heads_a * SUBLANES, s), F32)],
        compiler_params=_params("parallel"),
    )(datt, out_a, out_b)


def _dproj_assemble(proj, dnq, dnkv, dkpe, dqs, dks, dvs, gq, gkv, ql, tile=256):
    s, pw = proj.shape
    dw = dqs[0].shape[1]
    nbr = len(dqs)

    def body(*refs):
        ql_ref, kvl_ref, dnq_ref, dnkv_ref, dkpe_ref = refs[:5]
        br = refs[5:5 + 3 * nbr]
        gq_ref, gkv_ref = refs[5 + 3 * nbr:7 + 3 * nbr]
        dp_ref, dgq_ref, dgkv_ref = refs[7 + 3 * nbr:]

        @pl.when(pl.program_id(0) == 0)
        def _():
            dgq_ref[...] = jnp.zeros_like(dgq_ref)
            dgkv_ref[...] = jnp.zeros_like(dgkv_ref)

        def rms_bwd(x, dy, gg, dg_ref):
            r = lax.rsqrt(jnp.mean(x * x, axis=-1, keepdims=True) + RMS_EPS)
            xh = x * r
            dxh = dy * gg
            dg_ref[0:1, :] += jnp.sum(dy * xh, axis=0, keepdims=True)
            return r * (dxh - xh * jnp.mean(dxh * xh, axis=-1, keepdims=True))

        pieces = [_bf(rms_bwd(ql_ref[...], dnq_ref[...], gq_ref[...], dgq_ref)),
                  _bf(rms_bwd(kvl_ref[...], dnkv_ref[...], gkv_ref[...], dgkv_ref)),
                  _bf(dkpe_ref[...])]
        for k in range(3):
            acc = br[k * nbr][...]
            for r in br[k * nbr + 1:(k + 1) * nbr]:
                acc = acc + r[...]
            pieces.append(_bf(acc))
        dp_ref[...] = jnp.concatenate(pieces, axis=1)

    res = pl.pallas_call(
        body, name="dproj_assemble", grid=(s // tile,),
        in_specs=[_nat(tile, ql, 0), _nat(tile, ql, 1), _nat(tile, ql), _nat(tile, ql), _nat(tile, LANES)]
        + [_nat(tile, dw)] * (3 * nbr) + [_whole((1, ql)), _whole((1, ql))],
        out_specs=[_nat(tile, pw), _whole((SUBLANES, ql)), _whole((SUBLANES, ql))],
        out_shape=[jax.ShapeDtypeStruct((s, pw), BF16), jax.ShapeDtypeStruct((SUBLANES, ql), F32),
                   jax.ShapeDtypeStruct((SUBLANES, ql), F32)],
        compiler_params=_params("arbitrary"),
    )(proj, proj, dnq, dnkv, dkpe, *dqs, *dks, *dvs, gq.reshape(1, ql), gkv.reshape(1, ql))
    return res[0], res[1][0], res[2][0]


def _axpy(name, alpha, a, b, tile=256):
    s, d = a.shape

    def body(a_ref, b_ref, o_ref):
        o_ref[...] = alpha * a_ref[...] + b_ref[...]

    return pl.pallas_call(
        body, name=name, grid=(s // tile,), in_specs=[_nat(tile, d), _nat(tile, d)], out_specs=_nat(tile, d),
        out_shape=jax.ShapeDtypeStruct((s, d), F32), compiler_params=_params("parallel"),
    )(a, b)


def _cmul(ar, ai, br, bi):
    return ar * br - ai * bi, ar * bi + ai * br


def _s5_discretise(a_re, a_im, log_dt, b_re, b_im, n_sq):
    shape = a_re.shape

    def body(ar_ref, ai_ref, ldt_ref, br_ref, bi_ref, abr_ref, abi_ref, apr_ref, api_ref, bbr_ref, bbi_ref):
        ar, ai = ar_ref[...], ai_ref[...]
        dt = jnp.exp(ldt_ref[...])
        e = jnp.exp(ar * dt)
        abr, abi = e * jnp.cos(ai * dt), e * jnp.sin(ai * dt)
        den = ar * ar + ai * ai
        qr = ((abr - 1.0) * ar + abi * ai) / den
        qi = (abi * ar - (abr - 1.0) * ai) / den
        bbr, bbi = _cmul(qr, qi, br_ref[...], bi_ref[...])
        abr_ref[...], abi_ref[...] = abr, abi
        bbr_ref[...], bbi_ref[...] = bbr, bbi
        pr, pi = abr, abi
        for _ in range(n_sq):
            pr, pi = _cmul(pr, pi, pr, pi)
        apr_ref[...], api_ref[...] = pr, pi

    return pl.pallas_call(
        body, name="s5_discretise", out_shape=[jax.ShapeDtypeStruct(shape, F32)] * 6,
        compiler_params=pltpu.CompilerParams(vmem_limit_bytes=VMEM_LIMIT),
    )(a_re, a_im, log_dt, b_re, b_im)


def _s5_discretise_bwd(a16, b16, ag, gab, gbb):
    rows, p = a16[0].shape
    g = rows // S5_GROUP

    def disc(ar, ai, ldt):
        dt = jnp.exp(ldt)
        e = jnp.exp(ar * dt)
        abr, abi = e * jnp.cos(ai * dt), e * jnp.sin(ai * dt)
        den = ar * ar + ai * ai
        inv_r, inv_i = ar / den, -ai / den
        qr, qi = _cmul(abr - 1.0, abi, inv_r, inv_i)
        return dt, abr, abi, inv_r, inv_i, qr, qi

    def body(ar16_ref, ai16_ref, ldt16_ref, br_ref, bi_ref, ar_ref, ai_ref, ldt_ref, gar_ref, gai_ref, gbr_ref, gbi_ref,
             dar_ref, dai_ref, dldt_ref, dbr_ref, dbi_ref):
        _, _, _, _, _, qr16, qi16 = disc(ar16_ref[...], ai16_ref[...], ldt16_ref[...])
        gbr, gbi = gbr_ref[...], gbi_ref[...]
        dbr_ref[...], dbi_ref[...] = _cmul(qr16, -qi16, gbr, gbi)
        cr, ci = _cmul(br_ref[...], -bi_ref[...], gbr, gbi)
        gqr = jnp.sum(cr.reshape(g, S5_GROUP, p), axis=1)
        gqi = jnp.sum(ci.reshape(g, S5_GROUP, p), axis=1)
        ar, ai = ar_ref[...], ai_ref[...]
        dt, abr, abi, inv_r, inv_i, qr, qi = disc(ar, ai, ldt_ref[...])
        t_r, t_i = _cmul(inv_r, -inv_i, gqr, gqi)
        gab_r = gar_ref[...] + t_r
        gab_i = gai_ref[...] + t_i
        qa_r, qa_i = _cmul(qr, qi, inv_r, inv_i)
        a1_r, a1_i = _cmul(qa_r, -qa_i, gqr, gqi)
        gl_r, gl_i = _cmul(abr, -abi, gab_r, gab_i)
        dar_ref[...] = dt * gl_r - a1_r
        dai_ref[...] = dt * gl_i - a1_i
        gdt = jnp.sum(ar * gl_r + ai * gl_i, axis=-1, keepdims=True)
        dldt_ref[...] = gdt * dt[:, 0:1]

    return pl.pallas_call(
        body, name="s5_discretise_bwd",
        out_shape=[jax.ShapeDtypeStruct((g, p), F32), jax.ShapeDtypeStruct((g, p), F32),
                   jax.ShapeDtypeStruct((g, 1), F32), jax.ShapeDtypeStruct((rows, p), F32),
                   jax.ShapeDtypeStruct((rows, p), F32)],
        compiler_params=pltpu.CompilerParams(vmem_limit_bytes=VMEM_LIMIT),
    )(*a16, *b16, *ag, *gab, *gbb)


def _slab_tile(re, im, nsl):
    row = jnp.concatenate([re.reshape(nsl, SLAB_COLS), im.reshape(nsl, SLAB_COLS)], axis=-1)
    return jnp.repeat(row, SUBLANES, axis=0)


def _slab_in_matrix(b_re, b_im, nsl):
    eye = jnp.eye(SLAB_GROUPS, dtype=F32)

    def blk(b):
        b = b.reshape(nsl, SLAB_GROUPS, S5_GROUP, S5_STATE)
        return jnp.einsum('sgcp,gh->sgchp', b, eye).reshape(nsl, LANES, SLAB_COLS)

    return jnp.concatenate([blk(b_re), blk(b_im)], axis=-1)


def _slab_in_extract(m, nsl):
    eye = jnp.eye(SLAB_GROUPS, dtype=F32)

    def ext(x_):
        x_ = x_.reshape(nsl, SLAB_GROUPS, S5_GROUP, SLAB_GROUPS, S5_STATE)
        return jnp.einsum('sgchp,gh->sgcp', x_, eye).reshape(nsl * LANES, S5_STATE)

    return ext(m[..., :SLAB_COLS]), ext(m[..., SLAB_COLS:])


def _slab_out_matrix(c_re, c_im, nsl):
    eye = jnp.eye(SLAB_GROUPS, dtype=F32)

    def blk(c):
        c = c.reshape(nsl, SLAB_GROUPS, S5_GROUP, S5_STATE)
        return jnp.einsum('sgcp,gh->sgphc', c, eye).reshape(nsl, SLAB_COLS, LANES)

    return jnp.concatenate([blk(c_re), -blk(c_im)], axis=1)


def _slab_out_extract(m, nsl):
    eye = jnp.eye(SLAB_GROUPS, dtype=F32)

    def ext(x_):
        x_ = x_.reshape(nsl, SLAB_GROUPS, S5_STATE, SLAB_GROUPS, S5_GROUP)
        return jnp.einsum('sgphc,gh->sgcp', x_, eye).reshape(nsl * SLAB_GROUPS, S5_GROUP, S5_STATE)

    return ext(m[:, :SLAB_COLS]), -ext(m[:, SLAB_COLS:])


def _gelu(y):
    t = jnp.tanh(0.7978845608028654 * (y + 0.044715 * y * y * y))
    return 0.5 * y * (1.0 + t)


def _gelu_grad(y):
    t = jnp.tanh(0.7978845608028654 * (y + 0.044715 * y * y * y))
    return 0.5 * (1.0 + t) + 0.5 * y * (1.0 - t * t) * 0.7978845608028654 * (1.0 + 3.0 * 0.044715 * y * y)


def _scan_rows(ref, n_steps, ar, ai, state, reverse, conj, keep=True):
    sgn = -1.0 if conj else 1.0

    def step(k, carry):
        xr, xi = carry
        t = (n_steps - 1 - k) if reverse else k
        r0 = pl.multiple_of(t * SUBLANES, SUBLANES)
        nr = ar * xr - sgn * ai * xi + ref[pl.ds(r0, SUBLANES), :SLAB_COLS]
        ni = ar * xi + sgn * ai * xr + ref[pl.ds(r0, SUBLANES), SLAB_COLS:]
        if keep:
            ref[pl.ds(r0, SUBLANES), :SLAB_COLS] = nr
            ref[pl.ds(r0, SUBLANES), SLAB_COLS:] = ni
        return nr, ni

    return lax.fori_loop(0, n_steps, step, state, unroll=4)


def _s5_pass1(hp, bblk, ab_tile, rc=1024):
    s, d = hp.shape
    nsl = d // LANES
    rc = min(rc, s)
    nch = s // rc
    w = 2 * SLAB_COLS

    def body(u_ref, b_ref, ab_ref, end_ref, st_ref, x_ref):
        j = pl.program_id(1)

        @pl.when(j == 0)
        def _():
            st_ref[...] = jnp.zeros_like(st_ref)

        x_ref[...] = _dot(_bf(u_ref[...]), b_ref[0], NN)
        xr, xi = _scan_rows(x_ref, rc // SUBLANES, ab_ref[:, :SLAB_COLS], ab_ref[:, SLAB_COLS:],
                            (st_ref[:, :SLAB_COLS], st_ref[:, SLAB_COLS:]), False, False, keep=False)
        st_ref[:, :SLAB_COLS] = xr
        st_ref[:, SLAB_COLS:] = xi

        @pl.when(j == nch - 1)
        def _():
            end_ref[...] = st_ref[...]

    return pl.pallas_call(
        body, name="s5_scan_local", grid=(nsl, nch),
        in_specs=[pl.BlockSpec((rc, LANES), lambda sl, j: (j, sl)), pl.BlockSpec((1, LANES, w), lambda sl, j: (sl, 0, 0)),
                  pl.BlockSpec((SUBLANES, w), lambda sl, j: (sl, 0))],
        out_specs=pl.BlockSpec((SUBLANES, w), lambda sl, j: (sl, 0)),
        out_shape=jax.ShapeDtypeStruct((nsl * SUBLANES, w), F32),
        scratch_shapes=[pltpu.VMEM((SUBLANES, w), F32), pltpu.VMEM((rc, w), F32)],
        compiler_params=_params("parallel", "arbitrary"),
    )(hp, bblk, ab_tile)


def _s5_carry(name, ends, ap_tile, reverse):
    rows, w = ends.shape
    nsl = rows // SUBLANES
    sgn = -1.0 if reverse else 1.0

    def body(e_ref, ap_ref, c_ref):
        pr, pi = ap_ref[0:1, :SLAB_COLS], sgn * ap_ref[0:1, SLAB_COLS:]
        tr = jnp.zeros((1, SLAB_COLS), F32)
        ti = jnp.zeros((1, SLAB_COLS), F32)
        order = range(SUBLANES - 1, -1, -1) if reverse else range(SUBLANES)
        for seg in order:
            c_ref[seg:seg + 1, :SLAB_COLS] = tr
            c_ref[seg:seg + 1, SLAB_COLS:] = ti
            mr, mi = _cmul(pr, pi, tr, ti)
            tr = e_ref[seg:seg + 1, :SLAB_COLS] + mr
            ti = e_ref[seg:seg + 1, SLAB_COLS:] + mi

    spec = pl.BlockSpec((SUBLANES, w), lambda sl: (sl, 0))
    return pl.pallas_call(
        body, name=name, grid=(nsl,), in_specs=[spec, spec], out_specs=spec,
        out_shape=jax.ShapeDtypeStruct((rows, w), F32), compiler_params=_params("parallel"),
    )(ends, ap_tile)


def _s5_pass2(hp, bblk, cin, ab_tile, cblk, dvec, rc=1024):
    s, d = hp.shape
    nsl = d // LANES
    rc = min(rc, s)
    nch = s // rc
    w = 2 * SLAB_COLS

    def body(h_ref, b_ref, cin_ref, ab_ref, c_ref, d_ref, x_ref, y_ref, z_ref, st_ref):
        j = pl.program_id(1)

        @pl.when(j == 0)
        def _():
            st_ref[...] = cin_ref[...]

        hv = h_ref[...]
        x_ref[...] = _dot(_bf(hv), b_ref[0], NN)
        xr, xi = _scan_rows(x_ref, rc // SUBLANES, ab_ref[:, :SLAB_COLS], ab_ref[:, SLAB_COLS:],
                            (st_ref[:, :SLAB_COLS], st_ref[:, SLAB_COLS:]), False, False)
        st_ref[:, :SLAB_COLS] = xr
        st_ref[:, SLAB_COLS:] = xi
        y = _dot(_bf(x_ref[...]), c_ref[0], NN) + d_ref[...] * hv
        y_ref[...] = y
        z_ref[...] = _bf(_gelu(y))

    tile = lambda wd: pl.BlockSpec((rc, wd), lambda sl, j: (j, sl))
    small = pl.BlockSpec((SUBLANES, w), lambda sl, j: (sl, 0))
    return pl.pallas_call(
        body, name="s5_scan_carry_out", grid=(nsl, nch),
        in_specs=[tile(LANES), pl.BlockSpec((1, LANES, w), lambda sl, j: (sl, 0, 0)), small, small,
                  pl.BlockSpec((1, w, LANES), lambda sl, j: (sl, 0, 0)), pl.BlockSpec((1, LANES), lambda sl, j: (0, sl))],
        out_specs=[tile(w), tile(LANES), tile(LANES)],
        out_shape=[jax.ShapeDtypeStruct((s, nsl * w), F32), jax.ShapeDtypeStruct((s, d), F32),
                   jax.ShapeDtypeStruct((s, d), BF16)],
        scratch_shapes=[pltpu.VMEM((SUBLANES, w), F32)],
        compiler_params=_params("parallel", "arbitrary"),
    )(hp, bblk, cin, ab_tile, cblk, dvec)


def _s5_bwd_pass1(dzg, ypre, cblk, ab_tile, hp, rc=1024):
    s, d = hp.shape
    nsl = d // LANES
    rc = min(rc, s)
    nch = s // rc
    w = 2 * SLAB_COLS

    def body(dz_ref, y_ref, c_ref, ab_ref, h_ref, st_out_ref, dy_ref, dd_ref, st_ref, lam_ref):
        j = pl.program_id(1)

        @pl.when(j == 0)
        def _():
            st_ref[...] = jnp.zeros_like(st_ref)
            dd_ref[...] = jnp.zeros_like(dd_ref)

        dy = dz_ref[...] * _gelu_grad(y_ref[...])
        dy_ref[...] = dy
        dd_ref[0:1, :] += jnp.sum(dy * h_ref[...], axis=0, keepdims=True)
        lam_ref[...] = _dot(_bf(dy), c_ref[0], NT)
        lr, li = _scan_rows(lam_ref, rc // SUBLANES, ab_ref[:, :SLAB_COLS], ab_ref[:, SLAB_COLS:],
                            (st_ref[:, :SLAB_COLS], st_ref[:, SLAB_COLS:]), True, True, keep=False)
        st_ref[:, :SLAB_COLS] = lr
        st_ref[:, SLAB_COLS:] = li

        @pl.when(j == nch - 1)
        def _():
            st_out_ref[...] = st_ref[...]

    tile = lambda wd: pl.BlockSpec((rc, wd), lambda sl, j: (nch - 1 - j, sl))
    small = pl.BlockSpec((SUBLANES, w), lambda sl, j: (sl, 0))
    return pl.pallas_call(
        body, name="s5_adjoint_local", grid=(nsl, nch),
        in_specs=[tile(LANES), tile(LANES), pl.BlockSpec((1, w, LANES), lambda sl, j: (sl, 0, 0)), small, tile(LANES)],
        out_specs=[small, tile(LANES), pl.BlockSpec((SUBLANES, LANES), lambda sl, j: (0, sl))],
        out_shape=[jax.ShapeDtypeStruct((nsl * SUBLANES, w), F32),
                   jax.ShapeDtypeStruct((s, d), F32), jax.ShapeDtypeStruct((SUBLANES, d), F32)],
        scratch_shapes=[pltpu.VMEM((SUBLANES, w), F32), pltpu.VMEM((rc, w), F32)],
        compiler_params=_params("parallel", "arbitrary"),
    )(dzg, ypre, cblk, ab_tile, hp)


def _s5_bwd_pass2(dy, cblk, cinl, ab_tile, xtrue, cinx, hp, bblk, dvec, rc=1024):
    s, d = hp.shape
    nsl = d // LANES
    rc = min(rc, s)
    nch = s // rc
    w = 2 * SLAB_COLS
    n_steps = rc // SUBLANES

    def body(dy_ref, c_ref, cl_ref, ab_ref, x_ref, xp_ref, cx_ref, h_ref, b_ref, d_ref,
             du_ref, db_ref, dc_ref, da_ref, st_ref, lam_ref, acc_ref):
        j = pl.program_id(1)

        @pl.when(j == 0)
        def _():
            st_ref[...] = cl_ref[...]
            acc_ref[...] = jnp.zeros_like(acc_ref)
            db_ref[...] = jnp.zeros_like(db_ref)
            dc_ref[...] = jnp.zeros_like(dc_ref)

        ar, ai = ab_ref[:, :SLAB_COLS], ab_ref[:, SLAB_COLS:]
        lam_ref[...] = _dot(_bf(dy_ref[...]), c_ref[0], NT)

        def advance(lr, li, r0):
            nr = ar * lr + ai * li + lam_ref[pl.ds(r0, SUBLANES), :SLAB_COLS]
            ni = ar * li - ai * lr + lam_ref[pl.ds(r0, SUBLANES), SLAB_COLS:]
            lam_ref[pl.ds(r0, SUBLANES), :SLAB_COLS] = nr
            lam_ref[pl.ds(r0, SUBLANES), SLAB_COLS:] = ni
            return nr, ni

        def step(k, carry):
            lr, li, dr, di = carry
            t = n_steps - 1 - k
            nr, ni = advance(lr, li, pl.multiple_of(t * SUBLANES, SUBLANES))
            rx = pl.multiple_of((t - 1) * SUBLANES, SUBLANES)
            xr, xi = x_ref[pl.ds(rx, SUBLANES), :SLAB_COLS], x_ref[pl.ds(rx, SUBLANES), SLAB_COLS:]
            return nr, ni, dr + xr * nr + xi * ni, di + xr * ni - xi * nr

        lr, li, dr, di = lax.fori_loop(
            0, n_steps - 1, step,
            (st_ref[:, :SLAB_COLS], st_ref[:, SLAB_COLS:], acc_ref[:, :SLAB_COLS], acc_ref[:, SLAB_COLS:]), unroll=4)
        lr, li = advance(lr, li, 0)
        st_ref[:, :SLAB_COLS] = lr
        st_ref[:, SLAB_COLS:] = li
        first_chunk = j == nch - 1
        xr = jnp.where(first_chunk, cx_ref[:, :SLAB_COLS], xp_ref[:, :SLAB_COLS])
        xi = jnp.where(first_chunk, cx_ref[:, SLAB_COLS:], xp_ref[:, SLAB_COLS:])
        acc_ref[:, :SLAB_COLS] = dr + xr * lr + xi * li
        acc_ref[:, SLAB_COLS:] = di + xr * li - xi * lr

        lam_b = _bf(lam_ref[...])
        dyv = dy_ref[...]
        db_ref[0] += _dot(_bf(h_ref[...]), lam_b, TN)
        dc_ref[0] += _dot(_bf(dyv), _bf(x_ref[...]), TN)
        du_ref[...] = _dot(lam_b, b_ref[0], NT) + d_ref[...] * dyv

        @pl.when(j == nch - 1)
        def _():
            da_ref[...] = jnp.broadcast_to(jnp.sum(acc_ref[...], axis=0, keepdims=True), (SUBLANES, w))

    sub = rc // SUBLANES
    tile = lambda wd: pl.BlockSpec((rc, wd), lambda sl, j: (nch - 1 - j, sl))
    small = pl.BlockSpec((SUBLANES, w), lambda sl, j: (sl, 0))
    prev = pl.BlockSpec((SUBLANES, w), lambda sl, j: (jnp.maximum((nch - 1 - j) * sub - 1, 0), sl))
    return pl.pallas_call(
        body, name="s5_adjoint_carry_grads", grid=(nsl, nch),
        in_specs=[tile(LANES), pl.BlockSpec((1, w, LANES), lambda sl, j: (sl, 0, 0)), small, small, tile(w), prev, small,
                  tile(LANES), pl.BlockSpec((1, LANES, w), lambda sl, j: (sl, 0, 0)),
                  pl.BlockSpec((1, LANES), lambda sl, j: (0, sl))],
        out_specs=[tile(LANES), pl.BlockSpec((1, LANES, w), lambda sl, j: (sl, 0, 0)),
                   pl.BlockSpec((1, LANES, w), lambda sl, j: (sl, 0, 0)), small],
        out_shape=[jax.ShapeDtypeStruct((s, d), F32), jax.ShapeDtypeStruct((nsl, LANES, w), F32),
                   jax.ShapeDtypeStruct((nsl, LANES, w), F32), jax.ShapeDtypeStruct((nsl * SUBLANES, w), F32)],
        scratch_shapes=[pltpu.VMEM((SUBLANES, w), F32), pltpu.VMEM((rc, w), F32), pltpu.VMEM((SUBLANES, w), F32)],
        compiler_params=_params("parallel", "arbitrary"),
    )(dy, cblk, cinl, ab_tile, xtrue, xtrue, cinx, hp, bblk, dvec)


def _adamw(name, w, g, m, v):
    r, c = w.shape
    tile = r if r * c <= 512 * 1024 else _pick(r, max(SUBLANES, (512 * 1024 // c) // SUBLANES * SUBLANES), q=SUBLANES)
    c1 = 1.0 / (1.0 - ADAM_B1 ** ADAM_STEP)
    c2 = 1.0 / (1.0 - ADAM_B2 ** ADAM_STEP)

    def body(w_ref, g_ref, m_ref, v_ref, d_ref, nm_ref, nv_ref):
        gg = g_ref[...]
        nm = ADAM_B1 * m_ref[...] + (1.0 - ADAM_B1) * gg
        nv = ADAM_B2 * v_ref[...] + (1.0 - ADAM_B2) * gg * gg
        d_ref[...] = -ADAM_LR * ((nm * c1) / (jnp.sqrt(nv * c2) + ADAM_EPS) + ADAM_WD * w_ref[...])
        nm_ref[...] = nm
        nv_ref[...] = nv

    spec = _nat(tile, c)
    return pl.pallas_call(
        body, name=name, grid=(r // tile,), in_specs=[spec] * 4, out_specs=[spec] * 3,
        out_shape=[jax.ShapeDtypeStruct((r, c), F32)] * 3, compiler_params=_params("parallel"),
    )(w, g, m, v)


def _place():
    x, y, c = lax.axis_index("x"), lax.axis_index("y"), lax.axis_index("c")
    return x, y, c, [(1 - x, y), (x, 1 - y), (1 - x, 1 - y)]


_ANY = pl.BlockSpec(memory_space=pl.ANY)


def _gather_weights(shards):
    n = len(shards)

    def body(*refs):
        ins, outs = refs[:n], refs[n:2 * n]
        send_sems, recv_sems, local_sems = refs[2 * n:]
        x, y, c, chips = _place()
        me = 2 * x + y
        sibling = (x, y, 1 - c)
        started = []
        for a in range(n):
            local = pltpu.make_async_copy(ins[a], outs[a].at[me], local_sems.at[a])
            local.start()
            started.append(local)

        def half(a, chip, h):
            hw = ins[a].shape[1] // 2
            return outs[a].at[chip, :, pl.ds(pl.multiple_of(h * hw, LANES), hw)]

        def copy(a, k, src, chip, h, to):
            return pltpu.make_async_remote_copy(
                src_ref=src, dst_ref=half(a, chip, h), send_sem=send_sems.at[a, k], recv_sem=recv_sems.at[a, k],
                device_id=to, device_id_type=MESH)

        sends = []
        for a in range(n):
            hw = ins[a].shape[1] // 2
            mine = ins[a].at[:, pl.ds(pl.multiple_of(c * hw, LANES), hw)]
            for k, chip in enumerate(chips):
                cp = copy(a, k, mine, me, c, (*chip, c))
                cp.start()
                sends.append(cp)
        for a in range(n):
            for k, (cx, cy) in enumerate(chips):
                src_chip = 2 * cx + cy
                copy(a, k, half(a, src_chip, c), src_chip, c, (x, y, c)).wait_recv()
                fwd = copy(a, 3 + k, half(a, src_chip, c), src_chip, c, sibling)
                fwd.start()
                sends.append(fwd)
        for a in range(n):
            for k, (cx, cy) in enumerate(chips):
                src_chip = 2 * cx + cy
                copy(a, 3 + k, half(a, src_chip, 1 - c), src_chip, 1 - c, (x, y, c)).wait_recv()
        for cp in sends:
            cp.wait_send()
        for cp in started:
            cp.wait()

    return pl.pallas_call(
        body, name="gather_weights",
        in_specs=[_ANY] * n, out_specs=[_ANY] * n,
        out_shape=[jax.ShapeDtypeStruct((N_CHIPS,) + s_.shape, s_.dtype) for s_ in shards],
        scratch_shapes=[pltpu.SemaphoreType.DMA((n, 6)), pltpu.SemaphoreType.DMA((n, 6)), pltpu.SemaphoreType.DMA((n,))],

    )(*shards)


def _gather_weights_async(shards):
    n = len(shards)
    srcs = [jax.new_ref(s_, memory_space=pltpu.MemorySpace.HBM) for s_ in shards]
    outs = [jax.empty_ref(jax.ShapeDtypeStruct((N_CHIPS,) + s_.shape, s_.dtype), memory_space=pltpu.MemorySpace.HBM)
            for s_ in shards]

    @pl.kernel(mesh=plsc.ScalarSubcoreMesh(axis_name="seq", num_cores=1), name="gather_weights_async",
               scratch_types=(pltpu.SemaphoreType.DMA((n, 6)), pltpu.SemaphoreType.DMA((n, 6)),
                              pltpu.SemaphoreType.DMA((n,))),
               compiler_params=pltpu.CompilerParams(collective_id=1))
    def launch(send_sems, recv_sems, local_sems):
        x, y, c, chips = _place()
        me = 2 * x + y
        sibling = (x, y, 1 - c)
        barrier = pltpu.get_barrier_semaphore()
        for peer in [sibling] + [(*chip, c) for chip in chips]:
            pl.semaphore_signal(barrier, inc=1, device_id=peer, device_id_type=MESH)
        pl.semaphore_wait(barrier, 4)

        def half(a, chip, h):
            hw = srcs[a].shape[1] // 2
            return outs[a].at[chip, :, pl.ds(pl.multiple_of(h * hw, LANES), hw)]

        def copy(a, k, src, chip, h, to):
            return pltpu.make_async_remote_copy(
                src_ref=src, dst_ref=half(a, chip, h), send_sem=send_sems.at[a, k], recv_sem=recv_sems.at[a, k],
                device_id=to, device_id_type=MESH)

        locals_, sends = [], []
        for a in range(n):
            local = pltpu.make_async_copy(srcs[a], outs[a].at[me], local_sems.at[a])
            local.start()
            locals_.append(local)
            hw = srcs[a].shape[1] // 2
            mine = srcs[a].at[:, pl.ds(pl.multiple_of(c * hw, LANES), hw)]
            for k, chip in enumerate(chips):
                cp = copy(a, k, mine, me, c, (*chip, c))
                cp.start()
                sends.append(cp)
        for a in range(n):
            for k, (cx, cy) in enumerate(chips):
                src_chip = 2 * cx + cy
                copy(a, k, half(a, src_chip, c), src_chip, c, (x, y, c)).wait_recv()
                fwd = copy(a, 3 + k, half(a, src_chip, c), src_chip, c, sibling)
                fwd.start()
                sends.append(fwd)
        for a in range(n):
            for k, (cx, cy) in enumerate(chips):
                src_chip = 2 * cx + cy
                copy(a, 3 + k, half(a, src_chip, 1 - c), src_chip, 1 - c, (x, y, c)).wait_recv()
        for cp in sends:
            cp.wait_send()
        for cp in locals_:
            cp.wait()

    launch()
    return [o[...] for o in outs]


def _on_sequencer(name, cid, inputs, out_shapes, sem_types, peers, body):
    srcs = [jax.new_ref(a, memory_space=pltpu.MemorySpace.HBM) for a in inputs]
    outs = [jax.empty_ref(sd, memory_space=pltpu.MemorySpace.HBM) for sd in out_shapes]

    @pl.kernel(mesh=plsc.ScalarSubcoreMesh(axis_name="seq", num_cores=1), name=name, scratch_types=tuple(sem_types),
               compiler_params=pltpu.CompilerParams(collective_id=cid))
    def launch(*sems):
        x, y, c, chips = _place()
        barrier = pltpu.get_barrier_semaphore()
        ps = peers(x, y, c, chips)
        for peer in ps:
            pl.semaphore_signal(barrier, inc=1, device_id=peer, device_id_type=MESH)
        pl.semaphore_wait(barrier, len(ps))
        body(srcs, outs, *sems)

    launch()
    return [o[...] for o in outs]


def _sibling_only(x, y, c, chips):
    return [(x, y, 1 - c)]


def _same_core_of_other_chips(x, y, c, chips):
    return [(*chip, c) for chip in chips]


def _swap_halves_to_sibling(name, cid, grads):
    n = len(grads)

    def body(ins, outs, send_sems, recv_sems):
        x, y, c, _ = _place()
        cps = []
        for a in range(n):
            hw = ins[a].shape[2] // 2
            src = ins[a].at[:, :, pl.ds(pl.multiple_of((1 - c) * hw, LANES), hw)]
            cp = pltpu.make_async_remote_copy(src_ref=src, dst_ref=outs[a], send_sem=send_sems.at[a],
                                              recv_sem=recv_sems.at[a], device_id=(x, y, 1 - c), device_id_type=MESH)
            cp.start()
            cps.append(cp)
        for cp in cps:
            cp.wait()

    return _on_sequencer(
        name, cid, grads, [jax.ShapeDtypeStruct(g.shape[:2] + (g.shape[2] // 2,), g.dtype) for g in grads],
        [pltpu.SemaphoreType.DMA((n,)), pltpu.SemaphoreType.DMA((n,))], _sibling_only, body)


def _exchange_quarters(name, cid, parts):
    n = len(parts)

    def body(ins, outs, send_sems, recv_sems):
        x, y, c, chips = _place()
        cps = []
        for a in range(n):
            for k, (cx, cy) in enumerate(chips):
                cp = pltpu.make_async_remote_copy(
                    src_ref=ins[a].at[2 * cx + cy], dst_ref=outs[a].at[k], send_sem=send_sems.at[a, k],
                    recv_sem=recv_sems.at[a, k], device_id=(cx, cy, c), device_id_type=MESH)
                cp.start()
                cps.append(cp)
        for cp in cps:
            cp.wait()

    return _on_sequencer(
        name, cid, parts, [jax.ShapeDtypeStruct((3,) + p_.shape[1:], p_.dtype) for p_ in parts],
        [pltpu.SemaphoreType.DMA((n, 3)), pltpu.SemaphoreType.DMA((n, 3))], _same_core_of_other_chips, body)


def _swap_final_halves(name, cid, halves):
    n = len(halves)

    def body(ins, outs, send_sems, recv_sems):
        x, y, c, _ = _place()
        cps = []
        for a in range(n):
            cp = pltpu.make_async_remote_copy(src_ref=ins[a], dst_ref=outs[a], send_sem=send_sems.at[a],
                                              recv_sem=recv_sems.at[a], device_id=(x, y, 1 - c), device_id_type=MESH)
            cp.start()
            cps.append(cp)
        for cp in cps:
            cp.wait()

    return _on_sequencer(
        name, cid, halves, [jax.ShapeDtypeStruct(h.shape, h.dtype) for h in halves],
        [pltpu.SemaphoreType.DMA((n,)), pltpu.SemaphoreType.DMA((n,))], _sibling_only, body)


def _add_half(name, grad, recv):
    nchip, r, cfull = grad.shape
    hw = cfull // 2
    tile = _pick(r, max(BF16_ROWS, (256 * 1024 // hw) // BF16_ROWS * BF16_ROWS), q=BF16_ROWS)
    c = lax.axis_index("c")

    def body(c_ref, g_ref, r_ref, o_ref):
        o_ref[...] = _bf(g_ref[...] + r_ref[...])

    return pl.pallas_call(
        body, name=name,
        grid_spec=pltpu.PrefetchScalarGridSpec(
            num_scalar_prefetch=1, grid=(nchip, r // tile),
            in_specs=[pl.BlockSpec((1, tile, hw), lambda k, i, cr: (k, i, cr[0])),
                      pl.BlockSpec((1, tile, hw), lambda k, i, cr: (k, i, 0))],
            out_specs=pl.BlockSpec((1, tile, hw), lambda k, i, cr: (k, i, 0))),
        out_shape=jax.ShapeDtypeStruct((nchip, r, hw), BF16), compiler_params=_params("parallel", "parallel"),
    )(c.reshape(1).astype(jnp.int32), grad, recv)


def _add_quarters(name, part, recv):
    _, r, hw = part.shape
    tile = _pick(r, max(BF16_ROWS, (256 * 1024 // hw) // BF16_ROWS * BF16_ROWS), q=BF16_ROWS)
    me = 2 * lax.axis_index("x") + lax.axis_index("y")

    def body(me_ref, p_ref, r_ref, o_ref):
        f = lambda v: v.astype(F32)
        o_ref[...] = ((f(p_ref[0]) + f(r_ref[0])) + f(r_ref[1])) + f(r_ref[2])

    return pl.pallas_call(
        body, name=name,
        grid_spec=pltpu.PrefetchScalarGridSpec(
            num_scalar_prefetch=1, grid=(r // tile,),
            in_specs=[pl.BlockSpec((1, tile, hw), lambda i, mr: (mr[0], i, 0)),
                      pl.BlockSpec((3, tile, hw), lambda i, mr: (0, i, 0))],
            out_specs=pl.BlockSpec((tile, hw), lambda i, mr: (i, 0))),
        out_shape=jax.ShapeDtypeStruct((r, hw), F32), compiler_params=_params("parallel"),
    )(me.reshape(1).astype(jnp.int32), part, recv)


class _ReduceScatter:
    def __init__(self, tag, first_cid, grads):
        self.tag, self.cid = tag, first_cid
        self.stacks = [g.reshape(N_CHIPS, g.shape[0] // N_CHIPS, g.shape[1]) for g in grads]

    def start(self, anchor):
        self.stacks, anchor = lax.optimization_barrier((self.stacks, anchor))
        self.recv = _swap_halves_to_sibling(f"rs_swap_halves_{self.tag}", self.cid, self.stacks)
        return anchor

    def exchange(self, anchor):
        parts = [_add_half(f"rs_add_half_{self.tag}{a}", g, r) for a, (g, r) in enumerate(zip(self.stacks, self.recv))]
        self.parts, anchor = lax.optimization_barrier((parts, anchor))
        self.quarters = _exchange_quarters(f"rs_exchange_{self.tag}", self.cid + 1, self.parts)
        return anchor

    def join(self, anchor):
        halves = [_add_quarters(f"rs_add_quarters_{self.tag}{a}", p_, q_)
                  for a, (p_, q_) in enumerate(zip(self.parts, self.quarters))]
        self.halves, anchor = lax.optimization_barrier((halves, anchor))
        self.others = _swap_final_halves(f"rs_swap_final_{self.tag}", self.cid + 2, self.halves)
        return anchor

    def result(self):
        south = lax.axis_index("c") == 0
        return [jnp.concatenate([jnp.where(south, h, o), jnp.where(south, o, h)], axis=1)
                for h, o in zip(self.halves, self.others)]


def _allgather_small(pack):
    m_per, n = pack.shape

    def body(x_ref, out_ref, send_sems, recv_sems, local_sem):
        x, y, c, chips = _place()
        me, sibling = (x, y, c), (x, y, 1 - c)

        def rows(px, py, pc):
            return out_ref.at[pl.ds(pl.multiple_of((4 * px + 2 * py + pc) * m_per, SUBLANES), m_per), :]

        def copy(k, block, to, src=None):
            return pltpu.make_async_remote_copy(
                src_ref=rows(*block) if src is None else src, dst_ref=rows(*block),
                send_sem=send_sems.at[k], recv_sem=recv_sems.at[k], device_id=to, device_id_type=MESH)

        mine = pltpu.make_async_copy(x_ref, rows(*me), local_sem)
        mine.start()
        first = [copy(0, me, sibling, src=x_ref)]
        first += [copy(1 + j, me, (*chip, c), src=x_ref) for j, chip in enumerate(chips)]
        for cp in first:
            cp.start()
        passed = [copy(4 + j, (*chip, c), sibling) for j, chip in enumerate(chips)]
        for j, chip in enumerate(chips):
            copy(1 + j, (*chip, c), me).wait_recv()
            passed[j].start()
        copy(0, sibling, me).wait_recv()
        for j, chip in enumerate(chips):
            copy(4 + j, (*chip, 1 - c), me).wait_recv()
        for cp in first + passed:
            cp.wait_send()
        mine.wait()

    return pl.pallas_call(
        body, name="allgather_small_grads",
        out_shape=jax.ShapeDtypeStruct((N_DEV * m_per, n), pack.dtype),
        in_specs=[pl.BlockSpec(memory_space=pltpu.VMEM)], out_specs=pl.BlockSpec(memory_space=pltpu.VMEM),
        scratch_shapes=[pltpu.SemaphoreType.DMA((7,)), pltpu.SemaphoreType.DMA((7,)), pltpu.SemaphoreType.DMA],
        compiler_params=pltpu.CompilerParams(vmem_limit_bytes=VMEM_LIMIT),
    )(pack)


def _sum_devices(packs, m_per):
    tile = _pick(m_per, 512, q=SUBLANES)
    nt = m_per // tile

    def body(*refs):
        acc = refs[0][...]
        for r in refs[1:N_DEV]:
            acc = acc + r[...]
        refs[N_DEV][...] = acc

    return pl.pallas_call(
        body, name="sum_small_grads", grid=(nt,),
        in_specs=[pl.BlockSpec((tile, LANES), functools.partial(lambda i, k: (k * nt + i, 0), k=k)) for k in range(N_DEV)],
        out_specs=_nat(tile, LANES), out_shape=jax.ShapeDtypeStruct((m_per, LANES), F32),
        compiler_params=_params("parallel"),
    )(*([packs] * N_DEV))


def _tail_fwd(tag, alpha, h_in, adds, mix_gate, ln1, ln2, p_l, w, want_perm):
    h_mid, xh1, rs1, h_mid_b, _ = _ln_fwd(f"ln1_fwd_{tag}", alpha, h_in, adds, mix_gate, *ln1)
    gp = _matmul(f"ple_gate_fwd_{tag}", h_mid_b, w['wg'], 'nn')
    pw = _matmul(f"ple_proj_fwd_{tag}", p_l, w['plet'], 'nt')
    gg, uu, act = _ffn_in_swiglu(f"ffn_in_fwd_{tag}", h_mid_b, w['wit'])
    ffn = _matmul(f"ffn_out_fwd_{tag}", act, w['wo'], 'nn', tk=2816)
    h_out, xh2, rs2, _, h_perm = _ln_fwd(f"ln2_fwd_{tag}", alpha, h_mid, [(ffn, 'nat')],
                                         ('nat', (pw, 1, 0), (gp, 1, 0)), *ln2, want_perm=want_perm)
    saved = dict(h_mid_b=h_mid_b, xh1=xh1, rs1=rs1, gp=gp, pw=pw, g=gg, u=uu, act=act, xh2=xh2, rs2=rs2)
    return h_out, h_perm, saved


def _tail_bwd(tag, alpha, dparts, sv, ln1_g, ln2_g, p_l, w, mix_gate):
    d = sv['h_mid_b'].shape[1]
    dz2, dz2b, dgate, dg2, db2 = _ln_bwd(f"ln2_bwd_{tag}", dparts, sv['xh2'], sv['rs2'], ln2_g,
                                         gate=('nat', (sv['pw'], 1, 0), (sv['gp'], 1, 0)))
    grads = dict(ln2_g=dg2, ln2_b=db2)
    grads['plet'] = _matmul(f"ple_proj_dw_{tag}", dgate, p_l, 'tn', a_win=(0, d))
    grads['wg'] = _matmul(f"ple_gate_dw_{tag}", sv['h_mid_b'], dgate, 'tn', b_win=(d, d))
    dx_gate = _matmul(f"ple_gate_dx_{tag}", dgate, w['wg'], 'nt', a_win=(d, d))
    dact = _matmul(f"ffn_out_dx_{tag}", dz2b, w['wo'], 'nt', out_dtype=BF16, tn=1408)
    grads['wo'] = _matmul(f"ffn_out_dw_{tag}", sv['act'], dz2b, 'tn', tm=1408)
    dgu = _swiglu_bwd(f"swiglu_bwd_{tag}", sv['g'], sv['u'], dact)
    grads['wit'] = _matmul(f"ffn_in_dw_{tag}", dgu, sv['h_mid_b'], 'tn')
    dx_ffn = _matmul(f"ffn_in_dx_{tag}", dgu, w['wit'], 'nn', tk=2816)
    res = _ln_bwd(f"ln1_bwd_{tag}", [(dz2, 'nat', alpha), (dx_gate, 'nat', 1.0), (dx_ffn, 'nat', 1.0)],
                  sv['xh1'], sv['rs1'], ln1_g, gate=mix_gate)
    grads['ln1_g'], grads['ln1_b'] = res[-2], res[-1]
    return res[:-2], grads


def kernel(x, p, positions, attn_w_in, mla_q_norm, mla_w_q_b, mla_kv_norm, mla_w_kv_b, attn_w_out, s5_a_re, s5_a_im, s5_log_dt, s5_b_re, s5_b_im, s5_c_re, s5_c_im, s5_d, s5_w_glu, ln1_g, ln1_b, ffn_w_in, ffn_w_out, ple_w, ple_gate_w, ln2_g, ln2_b, loss_target, m_attn_w_in, m_mla_q_norm, m_mla_w_q_b, m_mla_kv_norm, m_mla_w_kv_b, m_attn_w_out, m_s5_a_re, m_s5_a_im, m_s5_log_dt, m_s5_b_re, m_s5_b_im, m_s5_c_re, m_s5_c_im, m_s5_d, m_s5_w_glu, m_ln1_g, m_ln1_b, m_ffn_w_in, m_ffn_w_out, m_ple_w, m_ple_gate_w, m_ln2_g, m_ln2_b, v_attn_w_in, v_mla_q_norm, v_mla_w_q_b, v_mla_kv_norm, v_mla_w_kv_b, v_attn_w_out, v_s5_a_re, v_s5_a_im, v_s5_log_dt, v_s5_b_re, v_s5_b_im, v_s5_c_re, v_s5_c_im, v_s5_d, v_s5_w_glu, v_ln1_g, v_ln1_b, v_ffn_w_in, v_ffn_w_out, v_ple_w, v_ple_gate_w, v_ln2_g, v_ln2_b):
    weights = dict(attn_w_in=attn_w_in, mla_q_norm=mla_q_norm, mla_w_q_b=mla_w_q_b, mla_kv_norm=mla_kv_norm,
                   mla_w_kv_b=mla_w_kv_b, attn_w_out=attn_w_out, s5_a_re=s5_a_re, s5_a_im=s5_a_im, s5_log_dt=s5_log_dt,
                   s5_b_re=s5_b_re, s5_b_im=s5_b_im, s5_c_re=s5_c_re, s5_c_im=s5_c_im, s5_d=s5_d, s5_w_glu=s5_w_glu,
                   ln1_g=ln1_g, ln1_b=ln1_b, ffn_w_in=ffn_w_in, ffn_w_out=ffn_w_out, ple_w=ple_w, ple_gate_w=ple_gate_w,
                   ln2_g=ln2_g, ln2_b=ln2_b)
    m_in = dict(attn_w_in=m_attn_w_in, mla_q_norm=m_mla_q_norm, mla_w_q_b=m_mla_w_q_b, mla_kv_norm=m_mla_kv_norm,
                mla_w_kv_b=m_mla_w_kv_b, attn_w_out=m_attn_w_out, s5_a_re=m_s5_a_re, s5_a_im=m_s5_a_im,
                s5_log_dt=m_s5_log_dt, s5_b_re=m_s5_b_re, s5_b_im=m_s5_b_im, s5_c_re=m_s5_c_re, s5_c_im=m_s5_c_im,
                s5_d=m_s5_d, s5_w_glu=m_s5_w_glu, ln1_g=m_ln1_g, ln1_b=m_ln1_b, ffn_w_in=m_ffn_w_in,
                ffn_w_out=m_ffn_w_out, ple_w=m_ple_w, ple_gate_w=m_ple_gate_w, ln2_g=m_ln2_g, ln2_b=m_ln2_b)
    v_in = dict(attn_w_in=v_attn_w_in, mla_q_norm=v_mla_q_norm, mla_w_q_b=v_mla_w_q_b, mla_kv_norm=v_mla_kv_norm,
                mla_w_kv_b=v_mla_w_kv_b, attn_w_out=v_attn_w_out, s5_a_re=v_s5_a_re, s5_a_im=v_s5_a_im,
                s5_log_dt=v_s5_log_dt, s5_b_re=v_s5_b_re, s5_b_im=v_s5_b_im, s5_c_re=v_s5_c_re, s5_c_im=v_s5_c_im,
                s5_d=v_s5_d, s5_w_glu=v_s5_w_glu, ln1_g=v_ln1_g, ln1_b=v_ln1_b, ffn_w_in=v_ffn_w_in,
                ffn_w_out=v_ffn_w_out, ple_w=v_ple_w, ple_gate_w=v_ple_gate_w, ln2_g=v_ln2_g, ln2_b=v_ln2_b)
    names = list(weights)

    s, d = x.shape[1], x.shape[2]
    depth = ln1_g.shape[0]
    assert depth == 2
    alpha = (2.0 * depth) ** 0.25
    ql, kvl = mla_q_norm.shape[1], mla_kv_norm.shape[1]
    in_cols = N_CHIPS * attn_w_in.shape[2]
    heads = N_CHIPS * mla_w_q_b.shape[2] // (NOPE + ROPE)
    hps = heads // N_CHIPS
    dw = (in_cols - ql - kvl - ROPE) // 3
    dh = dw // DHD
    assert ql % LANES == 0 and kvl == ql and dw % DHD == 0 and heads % N_CHIPS == 0
    ngroups, nstate = s5_a_re.shape[1], s5_a_re.shape[2]
    assert nstate == S5_STATE and ngroups * S5_GROUP == d and d % LANES == 0
    nsl = d // LANES
    seg_len = s // SUBLANES
    n_sq = seg_len.bit_length() - 1
    assert 1 << n_sq == seg_len, "the segment length of the S5 scan must be a power of two"
    for window, dil in DIL_BRANCHES:
        assert window // dil == DIL_STEPS and (s // dil) % DIL_STEPS == 0
    me = 2 * lax.axis_index("x") + lax.axis_index("y")

    xb = x[0]
    target = loss_target[0]
    p_layers = [p[0, 0], p[1, 0]]
    pos = positions[0].astype(F32).reshape(s, 1)
    inv_freq = ROPE_THETA ** (-jnp.arange(ROPE // 2, dtype=F32) / (ROPE // 2))
    invf = jnp.concatenate([inv_freq, inv_freq, jnp.zeros((LANES - ROPE,), F32)]).reshape(1, LANES)
    slopes = 2.0 ** (-8.0 * jnp.arange(1, dh + 1, dtype=F32) / dh)
    slopes = jnp.broadcast_to(jnp.repeat(slopes, SUBLANES)[:, None], (dh * SUBLANES, LANES))

    wqb_t = mla_w_q_b[0].T.reshape(hps, NOPE + ROPE, ql)
    wqb_t = jnp.pad(wqb_t, ((0, 0), (0, QK_PAD - NOPE - ROPE), (0, 0))).reshape(hps * QK_PAD, ql)
    d_cols = max(d // N_CHIPS, 2 * LANES)
    d_pad = jnp.zeros((SUBLANES, d_cols), F32).at[0, :d // N_CHIPS].set(s5_d[0])
    shards = [_bf(attn_w_in[0].T), _bf(wqb_t), _bf(mla_w_kv_b[0].T), _bf(attn_w_out[0]), _bf(s5_w_glu[0].T)]
    for l in range(depth):
        shards += [_bf(ffn_w_in[l].T), _bf(ffn_w_out[l]), _bf(ple_w[l].T), _bf(ple_gate_w[l])]
    shards.append(d_pad)
    first, later = lax.optimization_barrier((list(_gather_weights(shards[:3])), shards[3:]))
    gathered = first + _gather_weights_async(later)
    full = [g.reshape(N_CHIPS * g.shape[1], g.shape[2]) for g in gathered]
    win_t, wqb_t_f, wkv_t, wout, wglu_t = full[:5]
    lw = [dict(wit=full[5 + 4 * l], wo=full[6 + 4 * l], plet=full[7 + 4 * l], wg=full[8 + 4 * l]) for l in range(depth)]
    dvec = full[-1].reshape(N_CHIPS, SUBLANES, d_cols)[:, 0, :d // N_CHIPS].reshape(1, d)
    lat = ql + kvl
    win_t = jnp.concatenate([win_t[:lat + ROPE], jnp.zeros((LANES - ROPE, d), BF16), win_t[lat + ROPE:]], axis=0)
    kpe_cb = lat // LANES
    q_cb = kpe_cb + 1
    a_cb = heads * VDIM // LANES

    xbb = _bf(xb)
    proj = _matmul("attn_in_fwd", xbb, win_t, 'nt', tn=1408)
    nrm = _rms_fwd(proj, ql, kvl, mla_q_norm[0], mla_kv_norm[0])
    q_raw = _matmul("mla_q_up_fwd", nrm, wqb_t_f, 'nt', a_win=(0, ql))
    kv = _matmul("mla_kv_up_fwd", nrm, wkv_t, 'nt', a_win=(ql, kvl))
    qf, kf, vv = _rope_prep(q_raw, kv, proj, kpe_cb, pos, invf, heads)
    out_a, lse_a = _mla_fwd(qf, kf, vv.T, heads)
    out_b, lse_b = _dil_fused_fwd(proj, slopes, dh, q_cb)
    att = _concat_bf16("attn_heads_concat", out_a, out_b)
    mix0 = _matmul("attn_out_fwd", att, wout, 'nn')
    h2, h2p, sv0 = _tail_fwd("l0", alpha, xb, [(mix0, 'nat')], None, (ln1_g[0], ln1_b[0]), (ln2_g[0], ln2_b[0]),
                             p_layers[0], lw[0], want_perm=True)

    rep = lambda a: jnp.repeat(a, S5_GROUP, axis=0)
    ag = (s5_a_re[0], s5_a_im[0], jnp.broadcast_to(s5_log_dt[0][:, None], (ngroups, nstate)))
    a16 = tuple(rep(a) for a in ag)
    b16 = tuple(b[0].transpose(0, 2, 1).reshape(ngroups * S5_GROUP, nstate) for b in (s5_b_re, s5_b_im))
    abr, abi, apr, api, bbr, bbi = _s5_discretise(*a16, *b16, n_sq)
    ab_tile = _slab_tile(abr[::S5_GROUP], abi[::S5_GROUP], nsl)
    ap_tile = _slab_tile(apr[::S5_GROUP], api[::S5_GROUP], nsl)
    bblk = _bf(_slab_in_matrix(bbr.reshape(ngroups, S5_GROUP, nstate), bbi.reshape(ngroups, S5_GROUP, nstate), nsl))
    cblk = _bf(_slab_out_matrix(s5_c_re[0], s5_c_im[0], nsl))
    ends = _s5_pass1(h2p, bblk, ab_tile)
    cinx = _s5_carry("s5_carry_fwd", ends, ap_tile, False)
    xtrue, ypre, zg = _s5_pass2(h2p, bblk, cinx, ab_tile, cblk, dvec)
    vg = _matmul("s5_glu_fwd", zg, wglu_t, 'nt')
    glu_gate = ('perm', (vg, 2, 0), (vg, 2, 1))
    h4, _, sv1 = _tail_fwd("l1", alpha, h2, [], glu_gate, (ln1_g[1], ln1_b[1]), (ln2_g[1], ln2_b[1]),
                           p_layers[1], lw[1], want_perm=False)
    loss = lax.psum(jnp.sum(_loss_partial(h4, target)), ("x", "y", "c"))

    (dz1_1, _, dvg), g1 = _tail_bwd("l1", alpha, [(h4, 'nat', 1.0 / d), (target, 'nat', -1.0 / d)], sv1, ln1_g[1],
                                    ln2_g[1], p_layers[1], lw[1], glu_gate)
    d_wglu_t = _matmul("s5_glu_dw", dvg, zg, 'tn')
    dzg = _matmul("s5_glu_dx", dvg, wglu_t, 'nn')
    rs_l1 = _ReduceScatter("l1", 2, [d_wglu_t, g1['wit'], g1['wo'], g1['plet'], g1['wg']])
    dzg = rs_l1.start(dzg)
    starts, dy, dd = _s5_bwd_pass1(dzg, ypre, cblk, ab_tile, h2p)
    cinl = _s5_carry("s5_carry_bwd", starts, ap_tile, True)
    du_p, d_bblk, d_cblk, d_ab = _s5_bwd_pass2(dy, cblk, cinl, ab_tile, xtrue, cinx, h2p, bblk, dvec)
    gbb = _slab_in_extract(d_bblk, nsl)
    g_c_re, g_c_im = _slab_out_extract(jnp.swapaxes(d_cblk, 1, 2), nsl)
    d_ab = d_ab[::SUBLANES]
    gab = (d_ab[:, :SLAB_COLS].reshape(ngroups, nstate), d_ab[:, SLAB_COLS:].reshape(ngroups, nstate))
    g_a_re, g_a_im, g_log_dt, g_b_re, g_b_im = _s5_discretise_bwd(a16, b16, ag, gab, gbb)
    unt = lambda b: b.reshape(ngroups, S5_GROUP, nstate).transpose(0, 2, 1)

    du_p = rs_l1.exchange(du_p)
    (dz1_0, dz1_0b), g0 = _tail_bwd("l0", alpha, [(dz1_1, 'nat', alpha), (du_p, 'perm', 1.0)], sv0, ln1_g[0], ln2_g[0],
                                    p_layers[0], lw[0], None)
    dz1_0b = rs_l1.join(dz1_0b)
    d_wout = _matmul("attn_out_dw", att, dz1_0b, 'tn')
    rs_l0 = _ReduceScatter("l0", 5, [g0['wit'], g0['wo'], g0['plet'], g0['wg'], d_wout])
    dz1_0b = rs_l0.start(dz1_0b)
    datt = _matmul("attn_out_dx", dz1_0b, wout, 'nt')
    do, delta, delta_t = _attn_bwd_prep(datt, out_a, out_b)
    dqf, dkf, dvv = _mla_bwd(qf, kf, vv, do, lse_a, delta_t, heads, 0)
    dqf = rs_l0.exchange(dqf)
    dq_raw, dkv, dkpe = _rope_unprep(dqf, dkf, dvv, pos, invf, heads)
    d_wqb_t = _matmul("mla_q_up_dw", dq_raw, nrm, 'tn', b_win=(0, ql))
    d_wkv_t = _matmul("mla_kv_up_dw", dkv, nrm, 'tn', b_win=(ql, kvl))
    dnq = _matmul("mla_q_up_dx", dq_raw, wqb_t_f, 'nn')
    dnkv = _matmul("mla_kv_up_dx", dkv, wkv_t, 'nn')
    dqd, dkd, dvd = _dil_fused_bwd(proj, slopes, datt, lse_b, delta, dh, q_cb, a_cb)
    dkpe = rs_l0.join(dkpe)
    dproj, g_gq, g_gkv = _dproj_assemble(proj, dnq, dnkv, dkpe, [dqd], [dkd], [dvd], mla_q_norm[0], mla_kv_norm[0], ql)
    d_win_t = _matmul("attn_in_dw", dproj, xbb, 'tn', tm=1408)
    dx_attn = _matmul("attn_in_dx", dproj, win_t, 'nn')
    grad_x = _axpy("grad_x", alpha, dz1_0, dx_attn)

    d_win_t = jnp.concatenate([d_win_t[:lat + ROPE], d_win_t[lat + LANES:]], axis=0)
    rs_at = _ReduceScatter("attn", 8, [d_win_t, d_wqb_t, d_wkv_t])
    r_wglu, r_wit1, r_wo1, r_plet1, r_wg1 = rs_l1.result()
    r_wit0, r_wo0, r_plet0, r_wg0, r_wout = rs_l0.result()
    grad_x = rs_at.start(grad_x)
    r_wit0, r_wit1, r_wo0, r_wo1 = rs_at.exchange((r_wit0, r_wit1, r_wo0, r_wo1))
    grad_x = rs_at.join(grad_x)
    r_win, r_wqb, r_wkv = rs_at.result()
    r_wqb = r_wqb.reshape(hps, QK_PAD, ql)[:, :NOPE + ROPE].reshape(hps * (NOPE + ROPE), ql)
    grads = dict(attn_w_in=r_win.T[None], mla_w_q_b=r_wqb.T[None], mla_w_kv_b=r_wkv.T[None], attn_w_out=r_wout[None],
                 s5_w_glu=r_wglu.T[None],
                 ffn_w_in=jnp.stack([r_wit0.T, r_wit1.T]), ffn_w_out=jnp.stack([r_wo0, r_wo1]),
                 ple_w=jnp.stack([r_plet0.T, r_plet1.T]), ple_gate_w=jnp.stack([r_wg0, r_wg1]))

    small = dict(mla_q_norm=g_gq, mla_kv_norm=g_gkv, s5_a_re=g_a_re, s5_a_im=g_a_im, s5_log_dt=g_log_dt,
                 s5_b_re=unt(g_b_re), s5_b_im=unt(g_b_im), s5_c_re=g_c_re, s5_c_im=g_c_im, s5_d=dd[0],
                 ln1_g=jnp.stack([g0['ln1_g'], g1['ln1_g']]), ln1_b=jnp.stack([g0['ln1_b'], g1['ln1_b']]),
                 ln2_g=jnp.stack([g0['ln2_g'], g1['ln2_g']]), ln2_b=jnp.stack([g0['ln2_b'], g1['ln2_b']]))
    flat = jnp.concatenate([v_.reshape(-1) for v_ in small.values()])
    m_per = -(-flat.shape[0] // (LANES * SUBLANES)) * SUBLANES
    pack = jnp.pad(flat, (0, m_per * LANES - flat.shape[0])).reshape(m_per, LANES)
    total = _sum_devices(_allgather_small(pack), m_per).reshape(-1)
    off = 0
    for k_, v_ in small.items():
        n_ = v_.size
        piece = total[off:off + n_]
        off += n_
        if k_ == 's5_d':
            grads[k_] = lax.dynamic_slice(piece, (me * (d // N_CHIPS),), (d // N_CHIPS,)).reshape(weights[k_].shape)
        else:
            grads[k_] = piece.reshape(weights[k_].shape)

    deltas, new_m, new_v = {}, {}, {}
    for k_ in names:
        w_ = weights[k_]
        shape = w_.shape
        if w_.ndim == 3 and w_.shape[-1] >= LANES:
            two_d = (shape[0] * shape[1], shape[2])
        elif w_.ndim == 4:
            two_d = (shape[0] * shape[1], shape[2] * shape[3])
        else:
            two_d = (1, w_.size) if w_.ndim == 2 and shape[0] == 1 else (shape[0], w_.size // shape[0])
        dl, nm, nv = _adamw(f"adamw_{k_}", w_.reshape(two_d), grads[k_].reshape(two_d), m_in[k_].reshape(two_d),
                            v_in[k_].reshape(two_d))
        deltas[k_], new_m[k_], new_v[k_] = dl.reshape(shape), nm.reshape(shape), nv.reshape(shape)

    return (loss, grad_x[None], *[grads[k_] for k_ in names], *[deltas[k_] for k_ in names],
            *[new_m[k_] for k_ in names], *[new_v[k_] for k_ in names])
```

```python
import functools
import math

import jax
import jax.numpy as jnp
from jax import lax
from jax.experimental import pallas as pl
from jax.experimental.pallas import tpu as pltpu
from jax.experimental.pallas import tpu_sc as plsc

F32 = jnp.float32
BF16 = jnp.bfloat16
MESH = pl.DeviceIdType.MESH

LANES = 128
SUBLANES = 8
BF16_ROWS = 16
VMEM_LIMIT = 48 * 2 ** 20
N_CHIPS = 4
N_DEV = 8

NOPE = 128
ROPE = 64
VDIM = 128
QK_PAD = 256
DHD = 128
DIL_STEPS = 128
DIL_BRANCHES = ((128, 1), (512, 4), (2048, 16))
ROPE_THETA = 10000.0
S5_GROUP = 16
S5_STATE = 64
SLAB_GROUPS = LANES // S5_GROUP
SLAB_COLS = SLAB_GROUPS * S5_STATE
NEG = -1e30
LN_EPS = 1e-5
RMS_EPS = 1e-6

ADAM_LR = 0.001
ADAM_B1 = 0.9
ADAM_B2 = 0.999
ADAM_EPS = 1e-08
ADAM_WD = 0.01
ADAM_STEP = 10

NN = ((1,), (0,))
NT = ((1,), (1,))
TN = ((0,), (0,))


def _dot(a, b, dims):
    return lax.dot_general(a, b, (dims, ((), ())), preferred_element_type=F32)


def _bf(v):
    return v.astype(BF16)


def _pick(n, target, q=LANES, also=0):
    g = math.gcd(n, also) if also else n
    if g <= target and g == n:
        return n
    best = None
    for t in range(q, min(g, target) + 1, q):
        if g % t == 0:
            best = t
    assert best is not None, (n, target, q, also)
    return best


def _params(*sem):
    return pltpu.CompilerParams(dimension_semantics=sem, vmem_limit_bytes=VMEM_LIMIT)


def _sigmoid(v):
    return 1.0 / (1.0 + jnp.exp(-v))


def _matmul(name, a, b, form, out_dtype=F32, a_win=None, b_win=None, tm=1024, tn=1024, tk=2048):
    c0, aw = a_win if a_win else (0, a.shape[1])
    if form == 'nt':
        assert b_win is None
        n, kdim = b.shape
        d0 = 0
    else:
        kdim = b.shape[0]
        d0, n = b_win if b_win else (0, b.shape[1])
    if form == 'tn':
        m = aw
        assert a.shape[0] == kdim, (name, a.shape, b.shape)
        tm = _pick(m, tm, also=c0)
        tk = _pick(kdim, tk)
        a_off = c0 // tm
    else:
        m = a.shape[0]
        assert aw == kdim, (name, a.shape, b.shape, a_win)
        tm = _pick(m, tm)
        tk = _pick(kdim, tk, also=c0)
        a_off = c0 // tk
    tn = _pick(n, tn, also=d0)
    b_off = d0 // tn
    nk = kdim // tk
    dims = {'nn': NN, 'nt': NT, 'tn': TN}[form]

    def body(a_ref, b_ref, o_ref, *acc):
        prod = _dot(_bf(a_ref[...]), _bf(b_ref[...]), dims)
        if nk == 1:
            o_ref[...] = prod.astype(o_ref.dtype)
            return
        acc_ref, = acc
        k = pl.program_id(2)

        @pl.when(k == 0)
        def _():
            acc_ref[...] = prod

        @pl.when((k > 0) & (k < nk - 1))
        def _():
            acc_ref[...] += prod

        @pl.when(k == nk - 1)
        def _():
            o_ref[...] = (acc_ref[...] + prod).astype(o_ref.dtype)

    if form == 'tn':
        a_spec = pl.BlockSpec((tk, tm), lambda i, j, k: (k, i + a_off))
    else:
        a_spec = pl.BlockSpec((tm, tk), lambda i, j, k: (i, k + a_off))
    if form == 'nt':
        b_spec = pl.BlockSpec((tn, tk), lambda i, j, k: (j, k))
    else:
        b_spec = pl.BlockSpec((tk, tn), lambda i, j, k: (k, j + b_off))
    return pl.pallas_call(
        body, name=name,
        grid=(m // tm, n // tn, nk),
        in_specs=[a_spec, b_spec],
        out_specs=pl.BlockSpec((tm, tn), lambda i, j, k: (i, j)),
        out_shape=jax.ShapeDtypeStruct((m, n), out_dtype),
        scratch_shapes=[pltpu.VMEM((tm, tn), F32)] if nk > 1 else [],
        compiler_params=_params("parallel", "parallel", "arbitrary"),
    )(a, b)


def _nat(tile, width, cb=0):
    return pl.BlockSpec((tile, width), lambda i: (i, cb))


def _perm(tile, width, seg_tiles, ncb=1, cb=0):
    return pl.BlockSpec((tile, width), lambda i: (i % seg_tiles, (i // seg_tiles) * ncb + cb))


def _whole(shape):
    return pl.BlockSpec(shape, lambda i: (0,) * len(shape))


def _perm_view(a):
    s, w = a.shape
    return a.reshape(s // SUBLANES, SUBLANES * w)


def _row_spec(a, layout, tile, width, ncb=1, cb=0):
    if layout == 'nat':
        return a, _nat(tile, width, cb)
    seg_tiles = a.shape[0] // SUBLANES // tile
    return _perm_view(a), _perm(tile, width, seg_tiles, ncb, cb)


def _ln_fwd(name, alpha, a, adds, gate, g, b, want_perm=False, tile=256):
    s, d = a.shape
    n_add = len(adds)
    has_gate = gate is not None

    def body(*refs):
        a_ref = refs[0]
        add_refs = refs[1:1 + n_add]
        pos = 1 + n_add
        if has_gate:
            val_ref, pre_ref = refs[pos], refs[pos + 1]
            pos += 2
        g_ref, b_ref = refs[pos], refs[pos + 1]
        outs = refs[pos + 2:]
        z = alpha * a_ref[...]
        for r in add_refs:
            z = z + r[...]
        if has_gate:
            z = z + val_ref[...] * _sigmoid(pre_ref[...])
        mu = jnp.mean(z, axis=-1, keepdims=True)
        zc = z - mu
        var = jnp.mean(zc * zc, axis=-1, keepdims=True)
        rstd = lax.rsqrt(var + LN_EPS)
        xhat = zc * rstd
        h = xhat * g_ref[...] + b_ref[...]
        outs[0][...] = h
        outs[1][...] = xhat
        outs[2][...] = jnp.broadcast_to(rstd, (tile, LANES))
        outs[3][...] = _bf(h)
        if want_perm:
            outs[4][...] = h

    ins, specs = [a], [_nat(tile, d)]
    for arr, layout in adds:
        x_, sp = _row_spec(arr, layout, tile, d)
        ins.append(x_)
        specs.append(sp)
    if has_gate:
        layout = gate[0]
        for arr, ncb, cb in gate[1:]:
            x_, sp = _row_spec(arr, layout, tile, d, ncb=ncb, cb=cb)
            ins.append(x_)
            specs.append(sp)
    ins += [g.reshape(1, d), b.reshape(1, d)]
    specs += [_whole((1, d)), _whole((1, d))]
    out_shape = [jax.ShapeDtypeStruct((s, d), F32), jax.ShapeDtypeStruct((s, d), F32),
                 jax.ShapeDtypeStruct((s, LANES), F32), jax.ShapeDtypeStruct((s, d), BF16)]
    out_specs = [_nat(tile, d), _nat(tile, d), _nat(tile, LANES), _nat(tile, d)]
    if want_perm:
        seg_tiles = s // SUBLANES // tile
        out_shape.append(jax.ShapeDtypeStruct((s // SUBLANES, SUBLANES * d), F32))
        out_specs.append(_perm(tile, d, seg_tiles))
    res = pl.pallas_call(
        body, name=name, grid=(s // tile,), in_specs=specs, out_specs=out_specs, out_shape=out_shape,
        compiler_params=_params("parallel"),
    )(*ins)
    return res[0], res[1], res[2], res[3], (res[4].reshape(s, d) if want_perm else None)


def _ln_bwd(name, dparts, xhat, rstd, g, gate=None, tile=256):
    s, d = xhat.shape
    n_part = len(dparts)
    coefs = [c for _, _, c in dparts]
    has_gate = gate is not None

    def body(*refs):
        part_refs = refs[:n_part]
        xhat_ref, rstd_ref, g_ref = refs[n_part:n_part + 3]
        pos = n_part + 3
        if has_gate:
            val_ref, pre_ref = refs[pos], refs[pos + 1]
            pos += 2
        outs = list(refs[pos:])
        dz_ref = outs.pop(0)
        dzb_ref = outs.pop(0)
        dgate_ref = outs.pop(0) if has_gate else None
        dg_ref, db_ref = outs
        dh = coefs[0] * part_refs[0][...]
        for c, r in zip(coefs[1:], part_refs[1:]):
            dh = dh + c * r[...]
        xh = xhat_ref[...]
        dxh = dh * g_ref[...]
        m1 = jnp.mean(dxh, axis=-1, keepdims=True)
        m2 = jnp.mean(dxh * xh, axis=-1, keepdims=True)
        dz = rstd_ref[:, 0:1] * (dxh - m1 - xh * m2)
        dz_ref[...] = dz
        dzb_ref[...] = _bf(dz)
        if has_gate:
            sg = _sigmoid(pre_ref[...])
            dval = dz * sg
            dpre = dz * val_ref[...] * sg * (1.0 - sg)
            dgate_ref[...] = jnp.concatenate([_bf(dval), _bf(dpre)], axis=1)

        @pl.when(pl.program_id(0) == 0)
        def _():
            dg_ref[...] = jnp.zeros_like(dg_ref)
            db_ref[...] = jnp.zeros_like(db_ref)

        dg_ref[0:1, :] += jnp.sum(dh * xh, axis=0, keepdims=True)
        db_ref[0:1, :] += jnp.sum(dh, axis=0, keepdims=True)

    ins, specs = [], []
    for arr, layout, _ in dparts:
        x_, sp = _row_spec(arr, layout, tile, d)
        ins.append(x_)
        specs.append(sp)
    ins += [xhat, rstd, g.reshape(1, d)]
    specs += [_nat(tile, d), _nat(tile, LANES), _whole((1, d))]
    gate_layout = None
    if has_gate:
        gate_layout = gate[0]
        for arr, ncb, cb in gate[1:]:
            x_, sp = _row_spec(arr, gate_layout, tile, d, ncb=ncb, cb=cb)
            ins.append(x_)
            specs.append(sp)
    seg_tiles = s // SUBLANES // tile
    out_shape = [jax.ShapeDtypeStruct((s, d), F32), jax.ShapeDtypeStruct((s, d), BF16)]
    out_specs = [_nat(tile, d), _nat(tile, d)]
    if has_gate:
        if gate_layout == 'nat':
            out_shape.append(jax.ShapeDtypeStruct((s, 2 * d), BF16))
            out_specs.append(_nat(tile, 2 * d))
        else:
            out_shape.append(jax.ShapeDtypeStruct((s // SUBLANES, SUBLANES * 2 * d), BF16))
            out_specs.append(_perm(tile, 2 * d, seg_tiles))
    out_shape += [jax.ShapeDtypeStruct((SUBLANES, d), F32)] * 2
    out_specs += [_whole((SUBLANES, d))] * 2
    res = list(pl.pallas_call(
        body, name=name, grid=(s // tile,), in_specs=specs, out_specs=out_specs, out_shape=out_shape,
        compiler_params=_params("arbitrary"),
    )(*ins))
    out = [res.pop(0), res.pop(0)]
    if has_gate:
        out.append(res.pop(0).reshape(s, 2 * d))
    out += [res[0][0], res[1][0]]
    return out


def _loss_partial(h, target, tile=256):
    s, d = h.shape

    def body(h_ref, t_ref, o_ref):
        @pl.when(pl.program_id(0) == 0)
        def _():
            o_ref[...] = jnp.zeros_like(o_ref)

        e = h_ref[...] - t_ref[...]
        sq = e * e
        part = sq[:, 0:LANES]
        for k in range(1, d // LANES):
            part = part + sq[:, k * LANES:(k + 1) * LANES]
        o_ref[0:1, :] += jnp.sum(part, axis=0, keepdims=True) * (0.5 / d)

    return pl.pallas_call(
        body, name="loss_partial", grid=(s // tile,), in_specs=[_nat(tile, d), _nat(tile, d)],
        out_specs=_whole((SUBLANES, LANES)), out_shape=jax.ShapeDtypeStruct((SUBLANES, LANES), F32),
        compiler_params=_params("arbitrary"),
    )(h, target)


def _ffn_in_swiglu(name, a, wit, tm=1024, tn=704):
    m, kdim = a.shape
    f = wit.shape[0] // 2
    tm, tn = _pick(m, tm), _pick(f, tn)
    nj = f // tn

    def body(a_ref, bg_ref, bu_ref, g_ref, u_ref, act_ref):
        av = _bf(a_ref[...])
        gg = _dot(av, bg_ref[...], NT)
        uu = _dot(av, bu_ref[...], NT)
        g_ref[...] = gg
        u_ref[...] = uu
        act_ref[...] = _bf(gg * _sigmoid(gg) * uu)

    ospec = pl.BlockSpec((tm, tn), lambda i, j: (i, j))
    return pl.pallas_call(
        body, name=name, grid=(m // tm, nj),
        in_specs=[pl.BlockSpec((tm, kdim), lambda i, j: (i, 0)), pl.BlockSpec((tn, kdim), lambda i, j: (j, 0)),
                  pl.BlockSpec((tn, kdim), lambda i, j: (j + nj, 0))],
        out_specs=[ospec, ospec, ospec],
        out_shape=[jax.ShapeDtypeStruct((m, f), F32), jax.ShapeDtypeStruct((m, f), F32), jax.ShapeDtypeStruct((m, f), BF16)],
        compiler_params=_params("parallel", "parallel"),
    )(a, wit, wit)


def _swiglu_bwd(name, g, u, dact, tile=128):
    s, f = g.shape
    f2 = 2 * f

    def body(g_ref, u_ref, da_ref, o_ref):
        gg = g_ref[...]
        sg = _sigmoid(gg)
        da = da_ref[...].astype(F32)
        silu = gg * sg
        o_ref[:, :f] = _bf(da * u_ref[...] * (sg + silu * (1.0 - sg)))
        o_ref[:, f:] = _bf(da * silu)

    return pl.pallas_call(
        body, name=name, grid=(s // tile,),
        in_specs=[_nat(tile, f), _nat(tile, f), _nat(tile, f)], out_specs=_nat(tile, f2),
        out_shape=jax.ShapeDtypeStruct((s, f2), BF16), compiler_params=_params("parallel"),
    )(g, u, dact)


def _rms_fwd(proj, ql, kvl, gq, gkv, tile=256):
    s = proj.shape[0]
    assert ql == kvl

    def body(q_ref, kv_ref, gq_ref, gkv_ref, o_ref):
        def nrm(x, gg):
            return x * lax.rsqrt(jnp.mean(x * x, axis=-1, keepdims=True) + RMS_EPS) * gg

        o_ref[...] = jnp.concatenate([_bf(nrm(q_ref[...], gq_ref[...])), _bf(nrm(kv_ref[...], gkv_ref[...]))], axis=1)

    return pl.pallas_call(
        body, name="mla_rms_fwd", grid=(s // tile,),
        in_specs=[_nat(tile, ql, 0), _nat(tile, kvl, 1), _whole((1, ql)), _whole((1, kvl))],
        out_specs=_nat(tile, ql + kvl), out_shape=jax.ShapeDtypeStruct((s, ql + kvl), BF16),
        compiler_params=_params("parallel"),
    )(proj, proj, gq.reshape(1, ql), gkv.reshape(1, kvl))


def _rope_coeffs(pos, invf):
    ang = pos * invf
    cs, sn = jnp.cos(ang), jnp.sin(ang)
    lane = lax.broadcasted_iota(jnp.int32, ang.shape, 1)
    half = ROPE // 2
    c = jnp.where(lane < ROPE, cs, 0.0)
    sa = jnp.where(lane < half, -sn, 0.0)
    sb = jnp.where((lane >= half) & (lane < ROPE), sn, 0.0)
    return c, sa, sb


def _rope_prep(q_raw, kv, proj, kpe_cb, pos, invf, heads, tile=256):
    s = q_raw.shape[0]
    half = ROPE // 2

    def body(q_ref, kv_ref, kpe_ref, pos_ref, invf_ref, qf_ref, kf_ref, v_ref):
        c, sa, sb = _rope_coeffs(pos_ref[...], invf_ref[...])

        def rope(t):
            return t * c + pltpu.roll(t, LANES - half, 1) * sa + pltpu.roll(t, half, 1) * sb

        kr = _bf(rope(kpe_ref[...]))
        for hh in range(heads):
            o = hh * QK_PAD
            qf_ref[:, o:o + NOPE] = _bf(q_ref[:, o:o + NOPE])
            qf_ref[:, o + NOPE:o + QK_PAD] = _bf(rope(q_ref[:, o + NOPE:o + QK_PAD]))
            kf_ref[:, o:o + NOPE] = _bf(kv_ref[:, o:o + NOPE])
            kf_ref[:, o + NOPE:o + QK_PAD] = kr
            v_ref[:, hh * VDIM:(hh + 1) * VDIM] = _bf(kv_ref[:, o + NOPE:o + QK_PAD])

    w = heads * QK_PAD
    return pl.pallas_call(
        body, name="mla_rope_prep", grid=(s // tile,),
        in_specs=[_nat(tile, w), _nat(tile, w), _nat(tile, LANES, kpe_cb), _nat(tile, 1), _whole((1, LANES))],
        out_specs=[_nat(tile, w), _nat(tile, w), _nat(tile, heads * VDIM)],
        out_shape=[jax.ShapeDtypeStruct((s, w), BF16), jax.ShapeDtypeStruct((s, w), BF16),
                   jax.ShapeDtypeStruct((s, heads * VDIM), BF16)],
        compiler_params=_params("parallel"),
    )(q_raw, kv, proj, pos, invf)


def _rope_unprep(dqf, dkf, dv, pos, invf, heads, tile=256):
    s = dqf.shape[0]
    half = ROPE // 2

    def body(dq_ref, dk_ref, dv_ref, pos_ref, invf_ref, dqr_ref, dkv_ref, dkpe_ref):
        c, sa, sb = _rope_coeffs(pos_ref[...], invf_ref[...])

        def unrope(gt):
            return gt * c + pltpu.roll(gt * sa, half, 1) + pltpu.roll(gt * sb, LANES - half, 1)

        dkpe = jnp.zeros((tile, LANES), F32)
        for hh in range(heads):
            o = hh * QK_PAD
            dqr_ref[:, o:o + NOPE] = _bf(dq_ref[:, o:o + NOPE])
            dqr_ref[:, o + NOPE:o + QK_PAD] = _bf(unrope(dq_ref[:, o + NOPE:o + QK_PAD]))
            dkv_ref[:, o:o + NOPE] = _bf(dk_ref[:, o:o + NOPE])
            dkv_ref[:, o + NOPE:o + QK_PAD] = _bf(dv_ref[:, hh * VDIM:(hh + 1) * VDIM])
            dkpe = dkpe + dk_ref[:, o + NOPE:o + QK_PAD]
        dkpe_ref[...] = unrope(dkpe)

    w = heads * QK_PAD
    return pl.pallas_call(
        body, name="mla_rope_unprep", grid=(s // tile,),
        in_specs=[_nat(tile, w), _nat(tile, w), _nat(tile, heads * VDIM), _nat(tile, 1), _whole((1, LANES))],
        out_specs=[_nat(tile, w), _nat(tile, w), _nat(tile, LANES)],
        out_shape=[jax.ShapeDtypeStruct((s, w), BF16), jax.ShapeDtypeStruct((s, w), BF16),
                   jax.ShapeDtypeStruct((s, LANES), F32)],
        compiler_params=_params("parallel"),
    )(dqf, dkf, dv, pos, invf)


LOG2E = 1.4426950408889634
MLA_SCALE = (NOPE + ROPE) ** -0.5


def _mla_scores_t(k, q, t, masked):
    sc = _dot(k, q, NT) * (MLA_SCALE * LOG2E)
    if masked:
        row = lax.broadcasted_iota(jnp.int32, (t, t), 0)
        col = lax.broadcasted_iota(jnp.int32, (t, t), 1)
        sc = jnp.where(row <= col, sc, NEG)
    return sc


def _mla_fwd(qf, kf, vt, heads, t=512):
    s = qf.shape[0]
    t = min(t, s)
    nq = s // t

    def body(q_ref, k_ref, vt_ref, o_ref, lse_ref, m_ref, l_ref, acc_ref):
        i = pl.program_id(1)
        m_ref[...] = jnp.full_like(m_ref, NEG)
        l_ref[...] = jnp.zeros_like(l_ref)
        acc_ref[...] = jnp.zeros_like(acc_ref)
        q = q_ref[...]

        def block(j, masked):
            r0 = pl.multiple_of(j * t, t)
            sc = _mla_scores_t(k_ref[pl.ds(r0, t), :], q, t, masked)
            m_prev = m_ref[0:1, :]
            m_new = jnp.maximum(m_prev, jnp.max(sc, axis=0, keepdims=True))
            corr = jnp.exp2(m_prev - m_new)
            p = jnp.exp2(sc - m_new)
            l_new = corr * l_ref[0:1, :] + jnp.sum(p, axis=0, keepdims=True)
            acc_ref[...] = corr * acc_ref[...] + _dot(vt_ref[:, pl.ds(r0, t)], _bf(p), NN)
            m_ref[...] = jnp.broadcast_to(m_new, (SUBLANES, t))
            l_ref[...] = jnp.broadcast_to(l_new, (SUBLANES, t))

        def unmasked(j, carry):
            block(j, False)
            return carry

        lax.fori_loop(0, i, unmasked, 0)
        block(i, True)
        o_ref[...] = (acc_ref[...] / l_ref[0:1, :]).T
        lse_ref[...] = m_ref[...] + jnp.log(l_ref[...]) * LOG2E

    return pl.pallas_call(
        body, name="mla_flash_fwd", grid=(heads, nq),
        in_specs=[pl.BlockSpec((t, QK_PAD), lambda h, i: (i, h)), pl.BlockSpec((s, QK_PAD), lambda h, i: (0, h)),
                  pl.BlockSpec((VDIM, s), lambda h, i: (h, 0))],
        out_specs=[pl.BlockSpec((t, VDIM), lambda h, i: (i, h)), pl.BlockSpec((SUBLANES, t), lambda h, i: (h, i))],
        out_shape=[jax.ShapeDtypeStruct((s, heads * VDIM), F32), jax.ShapeDtypeStruct((heads * SUBLANES, s), F32)],
        scratch_shapes=[pltpu.VMEM((SUBLANES, t), F32), pltpu.VMEM((SUBLANES, t), F32), pltpu.VMEM((VDIM, t), F32)],
        compiler_params=_params("parallel", "arbitrary"),
    )(qf, kf, vt)


def _mla_bwd(qf, kf, v, do, lse_t, delta_t, heads, do_cb0, t=512):
    s = qf.shape[0]
    t = min(t, s)
    nq = s // t

    def body(q_ref, k_ref, v_ref, do_ref, lse_ref, dl_ref, dq_ref, dk_ref, dv_ref, acc_ref):
        i = pl.program_id(1)

        @pl.when(i == 0)
        def _():
            dk_ref[...] = jnp.zeros_like(dk_ref)
            dv_ref[...] = jnp.zeros_like(dv_ref)

        acc_ref[...] = jnp.zeros_like(acc_ref)
        q, dob = q_ref[...], do_ref[...]
        lse, dl = lse_ref[0:1, :], dl_ref[0:1, :]

        def block(j, masked):
            r0 = pl.multiple_of(j * t, t)
            k = k_ref[pl.ds(r0, t), :]
            p = jnp.exp2(_mla_scores_t(k, q, t, masked) - lse)
            dp = _dot(v_ref[pl.ds(r0, t), :], dob, NT)
            ds = _bf(p * (dp - dl) * MLA_SCALE)
            acc_ref[...] += _dot(ds, k, TN)
            dk_ref[pl.ds(r0, t), :] += _dot(ds, q, NN)
            dv_ref[pl.ds(r0, t), :] += _dot(_bf(p), dob, NN)

        def unmasked(j, carry):
            block(j, False)
            return carry

        lax.fori_loop(0, i, unmasked, 0)
        block(i, True)
        dq_ref[...] = acc_ref[...]

    qs = lambda w, off=0: pl.BlockSpec((t, w), lambda h, i: (i, h + off))
    ks = lambda w: pl.BlockSpec((s, w), lambda h, i: (0, h))
    st = pl.BlockSpec((SUBLANES, t), lambda h, i: (h, i))
    return pl.pallas_call(
        body, name="mla_flash_bwd", grid=(heads, nq),
        in_specs=[qs(QK_PAD), ks(QK_PAD), ks(VDIM), qs(VDIM, do_cb0), st, st],
        out_specs=[qs(QK_PAD), ks(QK_PAD), ks(VDIM)],
        out_shape=[jax.ShapeDtypeStruct((s, heads * QK_PAD), F32), jax.ShapeDtypeStruct((s, heads * QK_PAD), F32),
                   jax.ShapeDtypeStruct((s, heads * VDIM), F32)],
        scratch_shapes=[pltpu.VMEM((t, QK_PAD), F32)],
        compiler_params=_params("parallel", "arbitrary"),
    )(qf, kf, v, do, lse_t, delta_t)


DIL_BLOCK = 2048


def _dil_unit_rows(u, dil, block):
    sub = u // dil
    return u % dil + (dil * DIL_STEPS) * sub, sub == 0


def _dil_rows(base, dil):
    return pl.ds(base, DIL_STEPS, stride=dil) if dil > 1 else pl.ds(base, DIL_STEPS)


def _dil_unit_scores(q, kp, kc, slope, dil, no_prev):
    sc = jnp.concatenate([_dot(q, kp, NT), _dot(q, kc, NT)], axis=1) * (DHD ** -0.5)
    row = lax.broadcasted_iota(jnp.int32, (DIL_STEPS, 2 * DIL_STEPS), 0)
    col = lax.broadcasted_iota(jnp.int32, (DIL_STEPS, 2 * DIL_STEPS), 1)
    dist = row + DIL_STEPS - col
    valid = (dist >= 0) & (dist <= DIL_STEPS) & (jnp.logical_not(no_prev) | (col >= DIL_STEPS))
    return jnp.where(valid, sc - slope * (dil * dist).astype(F32), NEG)


def _dil_in_specs(pw, dh, q_cb, block, rev_nb=None):
    blk = (lambda i: i) if rev_nb is None else (lambda i: rev_nb - 1 - i)
    own = lambda off: pl.BlockSpec((block, DHD), lambda h, i: (blk(i), off + h))
    prev = lambda off: pl.BlockSpec((block, DHD), lambda h, i: (jnp.maximum(blk(i) - 1, 0), off + h))
    return [own(q_cb), own(q_cb + dh), prev(q_cb + dh), own(q_cb + 2 * dh), prev(q_cb + 2 * dh)]


def _dil_fused_fwd(proj, slopes, dh, q_cb):
    s, pw = proj.shape
    block = min(DIL_BLOCK, s)
    nb = s // block
    n_units = block // DIL_STEPS
    nbr = len(DIL_BRANCHES)
    assert block >= DIL_STEPS * max(d for _, d in DIL_BRANCHES)

    def body(q_ref, kc_ref, kp_ref, vc_ref, vp_ref, sl_ref, o_ref, lse_ref, kk, vv, *per_branch):
        og, mg, lg = per_branch[:nbr], per_branch[nbr:2 * nbr], per_branch[2 * nbr:]
        i = pl.program_id(1)
        kk[0:block, :] = kp_ref[...]
        kk[block:, :] = kc_ref[...]
        vv[0:block, :] = vp_ref[...]
        vv[block:, :] = vc_ref[...]
        slope = sl_ref[0:1, 0:1]
        for g, (_, dil) in enumerate(DIL_BRANCHES):
            rows = functools.partial(_dil_rows, dil=dil)

            def unit(u, dil=dil, rows=rows):
                q0, first = _dil_unit_rows(u, dil, block)
                q = _bf(q_ref[rows(q0), :])
                kc, kp = _bf(kk[rows(block + q0), :]), _bf(kk[rows(block + q0 - dil * DIL_STEPS), :])
                vc, vp = _bf(vv[rows(block + q0), :]), _bf(vv[rows(block + q0 - dil * DIL_STEPS), :])
                sc = _dil_unit_scores(q, kp, kc, slope, dil, first & (i == 0))
                m = jnp.max(sc, axis=-1, keepdims=True)
                e = jnp.exp(sc - m)
                o = _dot(_bf(e[:, :DIL_STEPS]), vp, NN) + _dot(_bf(e[:, DIL_STEPS:]), vc, NN)
                return q0, o, m, jnp.sum(e, axis=-1, keepdims=True)

            def pair(u, carry, g=g, rows=rows, unit=unit):
                for q0, o, m, lsum in (unit(u), unit(u + n_units // 2)):
                    og[g][rows(q0), :] = o
                    mg[g][rows(q0), :] = jnp.broadcast_to(m, (DIL_STEPS, LANES))
                    lg[g][rows(q0), :] = jnp.broadcast_to(lsum, (DIL_STEPS, LANES))
                return carry

            lax.fori_loop(0, n_units // 2, pair, 0, unroll=2)
        m_all = mg[0][...]
        for g in range(1, nbr):
            m_all = jnp.maximum(m_all, mg[g][...])
        tot = jnp.zeros((block, LANES), F32)
        acc = jnp.zeros((block, DHD), F32)
        for g in range(nbr):
            w = jnp.exp(mg[g][...] - m_all)
            tot = tot + w * lg[g][...]
            acc = acc + w * og[g][...]
        o_ref[...] = acc / tot
        lse_ref[...] = m_all + jnp.log(tot)

    ospec = pl.BlockSpec((block, DHD), lambda h, i: (i, h))
    return pl.pallas_call(
        body, name="dil_fused_fwd", grid=(dh, nb),
        in_specs=_dil_in_specs(pw, dh, q_cb, block) + [pl.BlockSpec((SUBLANES, LANES), lambda h, i: (h, 0))],
        out_specs=[ospec, ospec], out_shape=[jax.ShapeDtypeStruct((s, dh * DHD), F32)] * 2,
        scratch_shapes=[pltpu.VMEM((2 * block, DHD), F32), pltpu.VMEM((2 * block, DHD), F32)]
        + [pltpu.VMEM((block, DHD), F32)] * (3 * nbr),
        compiler_params=_params("parallel", "arbitrary"),
    )(proj, proj, proj, proj, proj, slopes)


def _dil_fused_bwd(proj, slopes, datt, lse, delta, dh, q_cb, b_cb0):
    s, pw = proj.shape
    block = min(DIL_BLOCK, s)
    nb = s // block
    n_units = block // DIL_STEPS
    scale = DHD ** -0.5

    def body(q_ref, kc_ref, kp_ref, vc_ref, vp_ref, sl_ref, do_ref, lse_ref, dl_ref, dq_ref, dk_ref, dv_ref,
             kk, vv, dkk, dvv, carry_k, carry_v):
        ii = pl.program_id(1)
        i = nb - 1 - ii

        @pl.when(ii == 0)
        def _():
            carry_k[...] = jnp.zeros_like(carry_k)
            carry_v[...] = jnp.zeros_like(carry_v)

        kk[0:block, :] = kp_ref[...]
        kk[block:, :] = kc_ref[...]
        vv[0:block, :] = vp_ref[...]
        vv[block:, :] = vc_ref[...]
        dkk[...] = jnp.zeros_like(dkk)
        dvv[...] = jnp.zeros_like(dvv)
        dq_ref[...] = jnp.zeros_like(dq_ref)
        slope = sl_ref[0:1, 0:1]
        for _, dil in DIL_BRANCHES:
            rows = functools.partial(_dil_rows, dil=dil)

            def unit(u, dil=dil, rows=rows):
                q0, first = _dil_unit_rows(u, dil, block)
                cur, prev = rows(block + q0), rows(block + q0 - dil * DIL_STEPS)
                q = _bf(q_ref[rows(q0), :])
                kc, kp, vc, vp = _bf(kk[cur, :]), _bf(kk[prev, :]), _bf(vv[cur, :]), _bf(vv[prev, :])
                dob = _bf(do_ref[rows(q0), :])
                sc = _dil_unit_scores(q, kp, kc, slope, dil, first & (i == 0))
                p = jnp.exp(sc - lse_ref[rows(q0), 0:1])
                dp = jnp.concatenate([_dot(dob, vp, NT), _dot(dob, vc, NT)], axis=1)
                ds = _bf(p * (dp - dl_ref[rows(q0), 0:1]) * scale)
                pb = _bf(p)
                return (rows(q0), cur, prev, _dot(ds[:, :DIL_STEPS], kp, NN) + _dot(ds[:, DIL_STEPS:], kc, NN),
                        _dot(ds[:, :DIL_STEPS], q, TN), _dot(ds[:, DIL_STEPS:], q, TN),
                        _dot(pb[:, :DIL_STEPS], dob, TN), _dot(pb[:, DIL_STEPS:], dob, TN))

            def pair(u, carry, unit=unit):
                for qrows, cur, prev, dq, dkp, dkc, dvp, dvc in (unit(u), unit(u + n_units // 2)):
                    dq_ref[qrows, :] += dq
                    dkk[prev, :] += dkp
                    dkk[cur, :] += dkc
                    dvv[prev, :] += dvp
                    dvv[cur, :] += dvc
                return carry

            lax.fori_loop(0, n_units // 2, pair, 0, unroll=2)
        dk_ref[...] = dkk[block:, :] + carry_k[...]
        dv_ref[...] = dvv[block:, :] + carry_v[...]
        carry_k[...] = dkk[0:block, :]
        carry_v[...] = dvv[0:block, :]

    rev = lambda i: nb - 1 - i
    mspec = pl.BlockSpec((block, DHD), lambda h, i: (rev(i), b_cb0 + h))
    ospec = pl.BlockSpec((block, DHD), lambda h, i: (rev(i), h))
    big = lambda: pltpu.VMEM((2 * block, DHD), F32)
    return pl.pallas_call(
        body, name="dil_fused_bwd", grid=(dh, nb),
        in_specs=_dil_in_specs(pw, dh, q_cb, block, rev_nb=nb)
        + [pl.BlockSpec((SUBLANES, LANES), lambda h, i: (h, 0)), mspec, ospec, mspec],
        out_specs=[ospec, ospec, ospec], out_shape=[jax.ShapeDtypeStruct((s, dh * DHD), F32)] * 3,
        scratch_shapes=[big(), big(), big(), big(), pltpu.VMEM((block, DHD), F32), pltpu.VMEM((block, DHD), F32)],
        compiler_params=_params("parallel", "arbitrary"),
    )(proj, proj, proj, proj, proj, slopes, datt, lse, delta)


def _concat_bf16(name, a, b, tile=256):
    s, wa = a.shape
    wb = b.shape[1]

    def body(a_ref, b_ref, o_ref):
        o_ref[...] = jnp.concatenate([_bf(a_ref[...]), _bf(b_ref[...])], axis=1)

    return pl.pallas_call(
        body, name=name, grid=(s // tile,), in_specs=[_nat(tile, wa), _nat(tile, wb)], out_specs=_nat(tile, wa + wb),
        out_shape=jax.ShapeDtypeStruct((s, wa + wb), BF16), compiler_params=_params("parallel"),
    )(a, b)


def _attn_bwd_prep(datt, out_a, out_b, tile=256):
    s, mixw = datt.shape
    wa = out_a.shape[1]
    heads_a = wa // LANES

    def body(d_ref, a_ref, b_ref, do_ref, dl_ref, dlt_ref):
        d = d_ref[...]
        do_ref[...] = _bf(d)
        prod = d * jnp.concatenate([a_ref[...], b_ref[...]], axis=1)
        for hh in range(mixw // LANES):
            sl = slice(hh * LANES, (hh + 1) * LANES)
            dl = jnp.broadcast_to(jnp.sum(prod[:, sl], axis=-1, keepdims=True), (tile, LANES))
            dl_ref[:, sl] = dl
            if hh < heads_a:
                dlt_ref[hh * SUBLANES:(hh + 1) * SUBLANES, :] = dl.T[0:SUBLANES, :]

    return pl.pallas_call(
        body, name="attn_bwd_prep", grid=(s // tile,),
        in_specs=[_nat(tile, mixw), _nat(tile, wa), _nat(tile, mixw - wa)],
        out_specs=[_nat(tile, mixw), _nat(tile, mixw), pl.BlockSpec((heads_a * SUBLANES, tile), lambda i: (0, i))],
        out_shape=[jax.ShapeDtypeStruct((s, mixw), BF16), jax.ShapeDtypeStruct((s, mixw), F32),
                   jax.ShapeDtypeStruct((heads_a * SUBLANES, s), F32)],
        compiler_params=_params("parallel"),
    )(datt, out_a, out_b)


def _dproj_assemble(proj, dnq, dnkv, dkpe, dqs, dks, dvs, gq, gkv, ql, tile=256):
    s, pw = proj.shape
    dw = dqs[0].shape[1]
    nbr = len(dqs)

    def body(*refs):
        ql_ref, kvl_ref, dnq_ref, dnkv_ref, dkpe_ref = refs[:5]
        br = refs[5:5 + 3 * nbr]
        gq_ref, gkv_ref = refs[5 + 3 * nbr:7 + 3 * nbr]
        dp_ref, dgq_ref, dgkv_ref = refs[7 + 3 * nbr:]

        @pl.when(pl.program_id(0) == 0)
        def _():
            dgq_ref[...] = jnp.zeros_like(dgq_ref)
            dgkv_ref[...] = jnp.zeros_like(dgkv_ref)

        def rms_bwd(x, dy, gg, dg_ref):
            r = lax.rsqrt(jnp.mean(x * x, axis=-1, keepdims=True) + RMS_EPS)
            xh = x * r
            dxh = dy * gg
            dg_ref[0:1, :] += jnp.sum(dy * xh, axis=0, keepdims=True)
            return r * (dxh - xh * jnp.mean(dxh * xh, axis=-1, keepdims=True))

        pieces = [_bf(rms_bwd(ql_ref[...], dnq_ref[...], gq_ref[...], dgq_ref)),
                  _bf(rms_bwd(kvl_ref[...], dnkv_ref[...], gkv_ref[...], dgkv_ref)),
                  _bf(dkpe_ref[...])]
        for k in range(3):
            acc = br[k * nbr][...]
            for r in br[k * nbr + 1:(k + 1) * nbr]:
                acc = acc + r[...]
            pieces.append(_bf(acc))
        dp_ref[...] = jnp.concatenate(pieces, axis=1)

    res = pl.pallas_call(
        body, name="dproj_assemble", grid=(s // tile,),
        in_specs=[_nat(tile, ql, 0), _nat(tile, ql, 1), _nat(tile, ql), _nat(tile, ql), _nat(tile, LANES)]
        + [_nat(tile, dw)] * (3 * nbr) + [_whole((1, ql)), _whole((1, ql))],
        out_specs=[_nat(tile, pw), _whole((SUBLANES, ql)), _whole((SUBLANES, ql))],
        out_shape=[jax.ShapeDtypeStruct((s, pw), BF16), jax.ShapeDtypeStruct((SUBLANES, ql), F32),
                   jax.ShapeDtypeStruct((SUBLANES, ql), F32)],
        compiler_params=_params("arbitrary"),
    )(proj, proj, dnq, dnkv, dkpe, *dqs, *dks, *dvs, gq.reshape(1, ql), gkv.reshape(1, ql))
    return res[0], res[1][0], res[2][0]


def _axpy(name, alpha, a, b, tile=256):
    s, d = a.shape

    def body(a_ref, b_ref, o_ref):
        o_ref[...] = alpha * a_ref[...] + b_ref[...]

    return pl.pallas_call(
        body, name=name, grid=(s // tile,), in_specs=[_nat(tile, d), _nat(tile, d)], out_specs=_nat(tile, d),
        out_shape=jax.ShapeDtypeStruct((s, d), F32), compiler_params=_params("parallel"),
    )(a, b)


def _cmul(ar, ai, br, bi):
    return ar * br - ai * bi, ar * bi + ai * br


def _s5_discretise(a_re, a_im, log_dt, b_re, b_im, n_sq):
    shape = a_re.shape

    def body(ar_ref, ai_ref, ldt_ref, br_ref, bi_ref, abr_ref, abi_ref, apr_ref, api_ref, bbr_ref, bbi_ref):
        ar, ai = ar_ref[...], ai_ref[...]
        dt = jnp.exp(ldt_ref[...])
        e = jnp.exp(ar * dt)
        abr, abi = e * jnp.cos(ai * dt), e * jnp.sin(ai * dt)
        den = ar * ar + ai * ai
        qr = ((abr - 1.0) * ar + abi * ai) / den
        qi = (abi * ar - (abr - 1.0) * ai) / den
        bbr, bbi = _cmul(qr, qi, br_ref[...], bi_ref[...])
        abr_ref[...], abi_ref[...] = abr, abi
        bbr_ref[...], bbi_ref[...] = bbr, bbi
        pr, pi = abr, abi
        for _ in range(n_sq):
            pr, pi = _cmul(pr, pi, pr, pi)
        apr_ref[...], api_ref[...] = pr, pi

    return pl.pallas_call(
        body, name="s5_discretise", out_shape=[jax.ShapeDtypeStruct(shape, F32)] * 6,
        compiler_params=pltpu.CompilerParams(vmem_limit_bytes=VMEM_LIMIT),
    )(a_re, a_im, log_dt, b_re, b_im)


def _s5_discretise_bwd(a16, b16, ag, gab, gbb):
    rows, p = a16[0].shape
    g = rows // S5_GROUP

    def disc(ar, ai, ldt):
        dt = jnp.exp(ldt)
        e = jnp.exp(ar * dt)
        abr, abi = e * jnp.cos(ai * dt), e * jnp.sin(ai * dt)
        den = ar * ar + ai * ai
        inv_r, inv_i = ar / den, -ai / den
        qr, qi = _cmul(abr - 1.0, abi, inv_r, inv_i)
        return dt, abr, abi, inv_r, inv_i, qr, qi

    def body(ar16_ref, ai16_ref, ldt16_ref, br_ref, bi_ref, ar_ref, ai_ref, ldt_ref, gar_ref, gai_ref, gbr_ref, gbi_ref,
             dar_ref, dai_ref, dldt_ref, dbr_ref, dbi_ref):
        _, _, _, _, _, qr16, qi16 = disc(ar16_ref[...], ai16_ref[...], ldt16_ref[...])
        gbr, gbi = gbr_ref[...], gbi_ref[...]
        dbr_ref[...], dbi_ref[...] = _cmul(qr16, -qi16, gbr, gbi)
        cr, ci = _cmul(br_ref[...], -bi_ref[...], gbr, gbi)
        gqr = jnp.sum(cr.reshape(g, S5_GROUP, p), axis=1)
        gqi = jnp.sum(ci.reshape(g, S5_GROUP, p), axis=1)
        ar, ai = ar_ref[...], ai_ref[...]
        dt, abr, abi, inv_r, inv_i, qr, qi = disc(ar, ai, ldt_ref[...])
        t_r, t_i = _cmul(inv_r, -inv_i, gqr, gqi)
        gab_r = gar_ref[...] + t_r
        gab_i = gai_ref[...] + t_i
        qa_r, qa_i = _cmul(qr, qi, inv_r, inv_i)
        a1_r, a1_i = _cmul(qa_r, -qa_i, gqr, gqi)
        gl_r, gl_i = _cmul(abr, -abi, gab_r, gab_i)
        dar_ref[...] = dt * gl_r - a1_r
        dai_ref[...] = dt * gl_i - a1_i
        gdt = jnp.sum(ar * gl_r + ai * gl_i, axis=-1, keepdims=True)
        dldt_ref[...] = gdt * dt[:, 0:1]

    return pl.pallas_call(
        body, name="s5_discretise_bwd",
        out_shape=[jax.ShapeDtypeStruct((g, p), F32), jax.ShapeDtypeStruct((g, p), F32),
                   jax.ShapeDtypeStruct((g, 1), F32), jax.ShapeDtypeStruct((rows, p), F32),
                   jax.ShapeDtypeStruct((rows, p), F32)],
        compiler_params=pltpu.CompilerParams(vmem_limit_bytes=VMEM_LIMIT),
    )(*a16, *b16, *ag, *gab, *gbb)


def _slab_tile(re, im, nsl):
    row = jnp.concatenate([re.reshape(nsl, SLAB_COLS), im.reshape(nsl, SLAB_COLS)], axis=-1)
    return jnp.repeat(row, SUBLANES, axis=0)


def _slab_in_matrix(b_re, b_im, nsl):
    eye = jnp.eye(SLAB_GROUPS, dtype=F32)

    def blk(b):
        b = b.reshape(nsl, SLAB_GROUPS, S5_GROUP, S5_STATE)
        return jnp.einsum('sgcp,gh->sgchp', b, eye).reshape(nsl, LANES, SLAB_COLS)

    return jnp.concatenate([blk(b_re), blk(b_im)], axis=-1)


def _slab_in_extract(m, nsl):
    eye = jnp.eye(SLAB_GROUPS, dtype=F32)

    def ext(x_):
        x_ = x_.reshape(nsl, SLAB_GROUPS, S5_GROUP, SLAB_GROUPS, S5_STATE)
        return jnp.einsum('sgchp,gh->sgcp', x_, eye).reshape(nsl * LANES, S5_STATE)

    return ext(m[..., :SLAB_COLS]), ext(m[..., SLAB_COLS:])


def _slab_out_matrix(c_re, c_im, nsl):
    eye = jnp.eye(SLAB_GROUPS, dtype=F32)

    def blk(c):
        c = c.reshape(nsl, SLAB_GROUPS, S5_GROUP, S5_STATE)
        return jnp.einsum('sgcp,gh->sgphc', c, eye).reshape(nsl, SLAB_COLS, LANES)

    return jnp.concatenate([blk(c_re), -blk(c_im)], axis=1)


def _slab_out_extract(m, nsl):
    eye = jnp.eye(SLAB_GROUPS, dtype=F32)

    def ext(x_):
        x_ = x_.reshape(nsl, SLAB_GROUPS, S5_STATE, SLAB_GROUPS, S5_GROUP)
        return jnp.einsum('sgphc,gh->sgcp', x_, eye).reshape(nsl * SLAB_GROUPS, S5_GROUP, S5_STATE)

    return ext(m[:, :SLAB_COLS]), -ext(m[:, SLAB_COLS:])


def _gelu(y):
    t = jnp.tanh(0.7978845608028654 * (y + 0.044715 * y * y * y))
    return 0.5 * y * (1.0 + t)


def _gelu_grad(y):
    t = jnp.tanh(0.7978845608028654 * (y + 0.044715 * y * y * y))
    return 0.5 * (1.0 + t) + 0.5 * y * (1.0 - t * t) * 0.7978845608028654 * (1.0 + 3.0 * 0.044715 * y * y)


def _scan_rows(ref, n_steps, ar, ai, state, reverse, conj, keep=True):
    sgn = -1.0 if conj else 1.0

    def step(k, carry):
        xr, xi = carry
        t = (n_steps - 1 - k) if reverse else k
        r0 = pl.multiple_of(t * SUBLANES, SUBLANES)
        nr = ar * xr - sgn * ai * xi + ref[pl.ds(r0, SUBLANES), :SLAB_COLS]
        ni = ar * xi + sgn * ai * xr + ref[pl.ds(r0, SUBLANES), SLAB_COLS:]
        if keep:
            ref[pl.ds(r0, SUBLANES), :SLAB_COLS] = nr
            ref[pl.ds(r0, SUBLANES), SLAB_COLS:] = ni
        return nr, ni

    return lax.fori_loop(0, n_steps, step, state, unroll=4)


def _s5_pass1(hp, bblk, ab_tile, rc=1024):
    s, d = hp.shape
    nsl = d // LANES
    rc = min(rc, s)
    nch = s // rc
    w = 2 * SLAB_COLS

    def body(u_ref, b_ref, ab_ref, end_ref, st_ref, x_ref):
        j = pl.program_id(1)

        @pl.when(j == 0)
        def _():
            st_ref[...] = jnp.zeros_like(st_ref)

        x_ref[...] = _dot(_bf(u_ref[...]), b_ref[0], NN)
        xr, xi = _scan_rows(x_ref, rc // SUBLANES, ab_ref[:, :SLAB_COLS], ab_ref[:, SLAB_COLS:],
                            (st_ref[:, :SLAB_COLS], st_ref[:, SLAB_COLS:]), False, False, keep=False)
        st_ref[:, :SLAB_COLS] = xr
        st_ref[:, SLAB_COLS:] = xi

        @pl.when(j == nch - 1)
        def _():
            end_ref[...] = st_ref[...]

    return pl.pallas_call(
        body, name="s5_scan_local", grid=(nsl, nch),
        in_specs=[pl.BlockSpec((rc, LANES), lambda sl, j: (j, sl)), pl.BlockSpec((1, LANES, w), lambda sl, j: (sl, 0, 0)),
                  pl.BlockSpec((SUBLANES, w), lambda sl, j: (sl, 0))],
        out_specs=pl.BlockSpec((SUBLANES, w), lambda sl, j: (sl, 0)),
        out_shape=jax.ShapeDtypeStruct((nsl * SUBLANES, w), F32),
        scratch_shapes=[pltpu.VMEM((SUBLANES, w), F32), pltpu.VMEM((rc, w), F32)],
        compiler_params=_params("parallel", "arbitrary"),
    )(hp, bblk, ab_tile)


def _s5_carry(name, ends, ap_tile, reverse):
    rows, w = ends.shape
    nsl = rows // SUBLANES
    sgn = -1.0 if reverse else 1.0

    def body(e_ref, ap_ref, c_ref):
        pr, pi = ap_ref[0:1, :SLAB_COLS], sgn * ap_ref[0:1, SLAB_COLS:]
        tr = jnp.zeros((1, SLAB_COLS), F32)
        ti = jnp.zeros((1, SLAB_COLS), F32)
        order = range(SUBLANES - 1, -1, -1) if reverse else range(SUBLANES)
        for seg in order:
            c_ref[seg:seg + 1, :SLAB_COLS] = tr
            c_ref[seg:seg + 1, SLAB_COLS:] = ti
            mr, mi = _cmul(pr, pi, tr, ti)
            tr = e_ref[seg:seg + 1, :SLAB_COLS] + mr
            ti = e_ref[seg:seg + 1, SLAB_COLS:] + mi

    spec = pl.BlockSpec((SUBLANES, w), lambda sl: (sl, 0))
    return pl.pallas_call(
        body, name=name, grid=(nsl,), in_specs=[spec, spec], out_specs=spec,
        out_shape=jax.ShapeDtypeStruct((rows, w), F32), compiler_params=_params("parallel"),
    )(ends, ap_tile)


def _s5_pass2(hp, bblk, cin, ab_tile, cblk, dvec, rc=1024):
    s, d = hp.shape
    nsl = d // LANES
    rc = min(rc, s)
    nch = s // rc
    w = 2 * SLAB_COLS

    def body(h_ref, b_ref, cin_ref, ab_ref, c_ref, d_ref, x_ref, y_ref, z_ref, st_ref):
        j = pl.program_id(1)

        @pl.when(j == 0)
        def _():
            st_ref[...] = cin_ref[...]

        hv = h_ref[...]
        x_ref[...] = _dot(_bf(hv), b_ref[0], NN)
        xr, xi = _scan_rows(x_ref, rc // SUBLANES, ab_ref[:, :SLAB_COLS], ab_ref[:, SLAB_COLS:],
                            (st_ref[:, :SLAB_COLS], st_ref[:, SLAB_COLS:]), False, False)
        st_ref[:, :SLAB_COLS] = xr
        st_ref[:, SLAB_COLS:] = xi
        y = _dot(_bf(x_ref[...]), c_ref[0], NN) + d_ref[...] * hv
        y_ref[...] = y
        z_ref[...] = _bf(_gelu(y))

    tile = lambda wd: pl.BlockSpec((rc, wd), lambda sl, j: (j, sl))
    small = pl.BlockSpec((SUBLANES, w), lambda sl, j: (sl, 0))
    return pl.pallas_call(
        body, name="s5_scan_carry_out", grid=(nsl, nch),
        in_specs=[tile(LANES), pl.BlockSpec((1, LANES, w), lambda sl, j: (sl, 0, 0)), small, small,
                  pl.BlockSpec((1, w, LANES), lambda sl, j: (sl, 0, 0)), pl.BlockSpec((1, LANES), lambda sl, j: (0, sl))],
        out_specs=[tile(w), tile(LANES), tile(LANES)],
        out_shape=[jax.ShapeDtypeStruct((s, nsl * w), F32), jax.ShapeDtypeStruct((s, d), F32),
                   jax.ShapeDtypeStruct((s, d), BF16)],
        scratch_shapes=[pltpu.VMEM((SUBLANES, w), F32)],
        compiler_params=_params("parallel", "arbitrary"),
    )(hp, bblk, cin, ab_tile, cblk, dvec)


def _s5_bwd_pass1(dzg, ypre, cblk, ab_tile, hp, rc=1024):
    s, d = hp.shape
    nsl = d // LANES
    rc = min(rc, s)
    nch = s // rc
    w = 2 * SLAB_COLS

    def body(dz_ref, y_ref, c_ref, ab_ref, h_ref, st_out_ref, dy_ref, dd_ref, st_ref, lam_ref):
        j = pl.program_id(1)

        @pl.when(j == 0)
        def _():
            st_ref[...] = jnp.zeros_like(st_ref)
            dd_ref[...] = jnp.zeros_like(dd_ref)

        dy = dz_ref[...] * _gelu_grad(y_ref[...])
        dy_ref[...] = dy
        dd_ref[0:1, :] += jnp.sum(dy * h_ref[...], axis=0, keepdims=True)
        lam_ref[...] = _dot(_bf(dy), c_ref[0], NT)
        lr, li = _scan_rows(lam_ref, rc // SUBLANES, ab_ref[:, :SLAB_COLS], ab_ref[:, SLAB_COLS:],
                            (st_ref[:, :SLAB_COLS], st_ref[:, SLAB_COLS:]), True, True, keep=False)
        st_ref[:, :SLAB_COLS] = lr
        st_ref[:, SLAB_COLS:] = li

        @pl.when(j == nch - 1)
        def _():
            st_out_ref[...] = st_ref[...]

    tile = lambda wd: pl.BlockSpec((rc, wd), lambda sl, j: (nch - 1 - j, sl))
    small = pl.BlockSpec((SUBLANES, w), lambda sl, j: (sl, 0))
    return pl.pallas_call(
        body, name="s5_adjoint_local", grid=(nsl, nch),
        in_specs=[tile(LANES), tile(LANES), pl.BlockSpec((1, w, LANES), lambda sl, j: (sl, 0, 0)), small, tile(LANES)],
        out_specs=[small, tile(LANES), pl.BlockSpec((SUBLANES, LANES), lambda sl, j: (0, sl))],
        out_shape=[jax.ShapeDtypeStruct((nsl * SUBLANES, w), F32),
                   jax.ShapeDtypeStruct((s, d), F32), jax.ShapeDtypeStruct((SUBLANES, d), F32)],
        scratch_shapes=[pltpu.VMEM((SUBLANES, w), F32), pltpu.VMEM((rc, w), F32)],
        compiler_params=_params("parallel", "arbitrary"),
    )(dzg, ypre, cblk, ab_tile, hp)


def _s5_bwd_pass2(dy, cblk, cinl, ab_tile, xtrue, cinx, hp, bblk, dvec, rc=1024):
    s, d = hp.shape
    nsl = d // LANES
    rc = min(rc, s)
    nch = s // rc
    w = 2 * SLAB_COLS
    n_steps = rc // SUBLANES

    def body(dy_ref, c_ref, cl_ref, ab_ref, x_ref, xp_ref, cx_ref, h_ref, b_ref, d_ref,
             du_ref, db_ref, dc_ref, da_ref, st_ref, lam_ref, acc_ref):
        j = pl.program_id(1)

        @pl.when(j == 0)
        def _():
            st_ref[...] = cl_ref[...]
            acc_ref[...] = jnp.zeros_like(acc_ref)
            db_ref[...] = jnp.zeros_like(db_ref)
            dc_ref[...] = jnp.zeros_like(dc_ref)

        ar, ai = ab_ref[:, :SLAB_COLS], ab_ref[:, SLAB_COLS:]
        lam_ref[...] = _dot(_bf(dy_ref[...]), c_ref[0], NT)

        def advance(lr, li, r0):
            nr = ar * lr + ai * li + lam_ref[pl.ds(r0, SUBLANES), :SLAB_COLS]
            ni = ar * li - ai * lr + lam_ref[pl.ds(r0, SUBLANES), SLAB_COLS:]
            lam_ref[pl.ds(r0, SUBLANES), :SLAB_COLS] = nr
            lam_ref[pl.ds(r0, SUBLANES), SLAB_COLS:] = ni
            return nr, ni

        def step(k, carry):
            lr, li, dr, di = carry
            t = n_steps - 1 - k
            nr, ni = advance(lr, li, pl.multiple_of(t * SUBLANES, SUBLANES))
            rx = pl.multiple_of((t - 1) * SUBLANES, SUBLANES)
            xr, xi = x_ref[pl.ds(rx, SUBLANES), :SLAB_COLS], x_ref[pl.ds(rx, SUBLANES), SLAB_COLS:]
            return nr, ni, dr + xr * nr + xi * ni, di + xr * ni - xi * nr

        lr, li, dr, di = lax.fori_loop(
            0, n_steps - 1, step,
            (st_ref[:, :SLAB_COLS], st_ref[:, SLAB_COLS:], acc_ref[:, :SLAB_COLS], acc_ref[:, SLAB_COLS:]), unroll=4)
        lr, li = advance(lr, li, 0)
        st_ref[:, :SLAB_COLS] = lr
        st_ref[:, SLAB_COLS:] = li
        first_chunk = j == nch - 1
        xr = jnp.where(first_chunk, cx_ref[:, :SLAB_COLS], xp_ref[:, :SLAB_COLS])
        xi = jnp.where(first_chunk, cx_ref[:, SLAB_COLS:], xp_ref[:, SLAB_COLS:])
        acc_ref[:, :SLAB_COLS] = dr + xr * lr + xi * li
        acc_ref[:, SLAB_COLS:] = di + xr * li - xi * lr

        lam_b = _bf(lam_ref[...])
        dyv = dy_ref[...]
        db_ref[0] += _dot(_bf(h_ref[...]), lam_b, TN)
        dc_ref[0] += _dot(_bf(dyv), _bf(x_ref[...]), TN)
        du_ref[...] = _dot(lam_b, b_ref[0], NT) + d_ref[...] * dyv

        @pl.when(j == nch - 1)
        def _():
            da_ref[...] = jnp.broadcast_to(jnp.sum(acc_ref[...], axis=0, keepdims=True), (SUBLANES, w))

    sub = rc // SUBLANES
    tile = lambda wd: pl.BlockSpec((rc, wd), lambda sl, j: (nch - 1 - j, sl))
    small = pl.BlockSpec((SUBLANES, w), lambda sl, j: (sl, 0))
    prev = pl.BlockSpec((SUBLANES, w), lambda sl, j: (jnp.maximum((nch - 1 - j) * sub - 1, 0), sl))
    return pl.pallas_call(
        body, name="s5_adjoint_carry_grads", grid=(nsl, nch),
        in_specs=[tile(LANES), pl.BlockSpec((1, w, LANES), lambda sl, j: (sl, 0, 0)), small, small, tile(w), prev, small,
                  tile(LANES), pl.BlockSpec((1, LANES, w), lambda sl, j: (sl, 0, 0)),
                  pl.BlockSpec((1, LANES), lambda sl, j: (0, sl))],
        out_specs=[tile(LANES), pl.BlockSpec((1, LANES, w), lambda sl, j: (sl, 0, 0)),
                   pl.BlockSpec((1, LANES, w), lambda sl, j: (sl, 0, 0)), small],
        out_shape=[jax.ShapeDtypeStruct((s, d), F32), jax.ShapeDtypeStruct((nsl, LANES, w), F32),
                   jax.ShapeDtypeStruct((nsl, LANES, w), F32), jax.ShapeDtypeStruct((nsl * SUBLANES, w), F32)],
        scratch_shapes=[pltpu.VMEM((SUBLANES, w), F32), pltpu.VMEM((rc, w), F32), pltpu.VMEM((SUBLANES, w), F32)],
        compiler_params=_params("parallel", "arbitrary"),
    )(dy, cblk, cinl, ab_tile, xtrue, xtrue, cinx, hp, bblk, dvec)


def _adamw(name, w, g, m, v):
    r, c = w.shape
    tile = r if r * c <= 512 * 1024 else _pick(r, max(SUBLANES, (512 * 1024 // c) // SUBLANES * SUBLANES), q=SUBLANES)
    c1 = 1.0 / (1.0 - ADAM_B1 ** ADAM_STEP)
    c2 = 1.0 / (1.0 - ADAM_B2 ** ADAM_STEP)

    def body(w_ref, g_ref, m_ref, v_ref, d_ref, nm_ref, nv_ref):
        gg = g_ref[...]
        nm = ADAM_B1 * m_ref[...] + (1.0 - ADAM_B1) * gg
        nv = ADAM_B2 * v_ref[...] + (1.0 - ADAM_B2) * gg * gg
        d_ref[...] = -ADAM_LR * ((nm * c1) / (jnp.sqrt(nv * c2) + ADAM_EPS) + ADAM_WD * w_ref[...])
        nm_ref[...] = nm
        nv_ref[...] = nv

    spec = _nat(tile, c)
    return pl.pallas_call(
        body, name=name, grid=(r // tile,), in_specs=[spec] * 4, out_specs=[spec] * 3,
        out_shape=[jax.ShapeDtypeStruct((r, c), F32)] * 3, compiler_params=_params("parallel"),
    )(w, g, m, v)


def _place():
    x, y, c = lax.axis_index("x"), lax.axis_index("y"), lax.axis_index("c")
    return x, y, c, [(1 - x, y), (x, 1 - y), (1 - x, 1 - y)]


_ANY = pl.BlockSpec(memory_space=pl.ANY)


def _gather_weights(shards):
    n = len(shards)

    def body(*refs):
        ins, outs = refs[:n], refs[n:2 * n]
        send_sems, recv_sems, local_sems = refs[2 * n:]
        x, y, c, chips = _place()
        me = 2 * x + y
        sibling = (x, y, 1 - c)
        started = []
        for a in range(n):
            local = pltpu.make_async_copy(ins[a], outs[a].at[me], local_sems.at[a])
            local.start()
            started.append(local)

        def half(a, chip, h):
            hw = ins[a].shape[1] // 2
            return outs[a].at[chip, :, pl.ds(pl.multiple_of(h * hw, LANES), hw)]

        def copy(a, k, src, chip, h, to):
            return pltpu.make_async_remote_copy(
                src_ref=src, dst_ref=half(a, chip, h), send_sem=send_sems.at[a, k], recv_sem=recv_sems.at[a, k],
                device_id=to, device_id_type=MESH)

        sends = []
        for a in range(n):
            hw = ins[a].shape[1] // 2
            mine = ins[a].at[:, pl.ds(pl.multiple_of(c * hw, LANES), hw)]
            for k, chip in enumerate(chips):
                cp = copy(a, k, mine, me, c, (*chip, c))
                cp.start()
                sends.append(cp)
        for a in range(n):
            for k, (cx, cy) in enumerate(chips):
                src_chip = 2 * cx + cy
                copy(a, k, half(a, src_chip, c), src_chip, c, (x, y, c)).wait_recv()
                fwd = copy(a, 3 + k, half(a, src_chip, c), src_chip, c, sibling)
                fwd.start()
                sends.append(fwd)
        for a in range(n):
            for k, (cx, cy) in enumerate(chips):
                src_chip = 2 * cx + cy
                copy(a, 3 + k, half(a, src_chip, 1 - c), src_chip, 1 - c, (x, y, c)).wait_recv()
        for cp in sends:
            cp.wait_send()
        for cp in started:
            cp.wait()

    return pl.pallas_call(
        body, name="gather_weights",
        in_specs=[_ANY] * n, out_specs=[_ANY] * n,
        out_shape=[jax.ShapeDtypeStruct((N_CHIPS,) + s_.shape, s_.dtype) for s_ in shards],
        scratch_shapes=[pltpu.SemaphoreType.DMA((n, 6)), pltpu.SemaphoreType.DMA((n, 6)), pltpu.SemaphoreType.DMA((n,))],

    )(*shards)


def _gather_weights_async(shards):
    n = len(shards)
    srcs = [jax.new_ref(s_, memory_space=pltpu.MemorySpace.HBM) for s_ in shards]
    outs = [jax.empty_ref(jax.ShapeDtypeStruct((N_CHIPS,) + s_.shape, s_.dtype), memory_space=pltpu.MemorySpace.HBM)
            for s_ in shards]

    @pl.kernel(mesh=plsc.ScalarSubcoreMesh(axis_name="seq", num_cores=1), name="gather_weights_async",
               scratch_types=(pltpu.SemaphoreType.DMA((n, 6)), pltpu.SemaphoreType.DMA((n, 6)),
                              pltpu.SemaphoreType.DMA((n,))),
               compiler_params=pltpu.CompilerParams(collective_id=1))
    def launch(send_sems, recv_sems, local_sems):
        x, y, c, chips = _place()
        me = 2 * x + y
        sibling = (x, y, 1 - c)
        barrier = pltpu.get_barrier_semaphore()
        for peer in [sibling] + [(*chip, c) for chip in chips]:
            pl.semaphore_signal(barrier, inc=1, device_id=peer, device_id_type=MESH)
        pl.semaphore_wait(barrier, 4)

        def half(a, chip, h):
            hw = srcs[a].shape[1] // 2
            return outs[a].at[chip, :, pl.ds(pl.multiple_of(h * hw, LANES), hw)]

        def copy(a, k, src, chip, h, to):
            return pltpu.make_async_remote_copy(
                src_ref=src, dst_ref=half(a, chip, h), send_sem=send_sems.at[a, k], recv_sem=recv_sems.at[a, k],
                device_id=to, device_id_type=MESH)

        locals_, sends = [], []
        for a in range(n):
            local = pltpu.make_async_copy(srcs[a], outs[a].at[me], local_sems.at[a])
            local.start()
            locals_.append(local)
            hw = srcs[a].shape[1] // 2
            mine = srcs[a].at[:, pl.ds(pl.multiple_of(c * hw, LANES), hw)]
            for k, chip in enumerate(chips):
                cp = copy(a, k, mine, me, c, (*chip, c))
                cp.start()
                sends.append(cp)
        for a in range(n):
            for k, (cx, cy) in enumerate(chips):
                src_chip = 2 * cx + cy
                copy(a, k, half(a, src_chip, c), src_chip, c, (x, y, c)).wait_recv()
                fwd = copy(a, 3 + k, half(a, src_chip, c), src_chip, c, sibling)
                fwd.start()
                sends.append(fwd)
        for a in range(n):
            for k, (cx, cy) in enumerate(chips):
                src_chip = 2 * cx + cy
                copy(a, 3 + k, half(a, src_chip, 1 - c), src_chip, 1 - c, (x, y, c)).wait_recv()
        for cp in sends:
            cp.wait_send()
        for cp in locals_:
            cp.wait()

    launch()
    return [o[...] for o in outs]


def _on_sequencer(name, cid, inputs, out_shapes, sem_types, peers, body):
    srcs = [jax.new_ref(a, memory_space=pltpu.MemorySpace.HBM) for a in inputs]
    outs = [jax.empty_ref(sd, memory_space=pltpu.MemorySpace.HBM) for sd in out_shapes]

    @pl.kernel(mesh=plsc.ScalarSubcoreMesh(axis_name="seq", num_cores=1), name=name, scratch_types=tuple(sem_types),
               compiler_params=pltpu.CompilerParams(collective_id=cid))
    def launch(*sems):
        x, y, c, chips = _place()
        barrier = pltpu.get_barrier_semaphore()
        ps = peers(x, y, c, chips)
        for peer in ps:
            pl.semaphore_signal(barrier, inc=1, device_id=peer, device_id_type=MESH)
        pl.semaphore_wait(barrier, len(ps))
        body(srcs, outs, *sems)

    launch()
    return [o[...] for o in outs]


def _sibling_only(x, y, c, chips):
    return [(x, y, 1 - c)]


def _same_core_of_other_chips(x, y, c, chips):
    return [(*chip, c) for chip in chips]


def _swap_halves_to_sibling(name, cid, grads):
    n = len(grads)

    def body(ins, outs, send_sems, recv_sems):
        x, y, c, _ = _place()
        cps = []
        for a in range(n):
            hw = ins[a].shape[2] // 2
            src = ins[a].at[:, :, pl.ds(pl.multiple_of((1 - c) * hw, LANES), hw)]
            cp = pltpu.make_async_remote_copy(src_ref=src, dst_ref=outs[a], send_sem=send_sems.at[a],
                                              recv_sem=recv_sems.at[a], device_id=(x, y, 1 - c), device_id_type=MESH)
            cp.start()
            cps.append(cp)
        for cp in cps:
            cp.wait()

    return _on_sequencer(
        name, cid, grads, [jax.ShapeDtypeStruct(g.shape[:2] + (g.shape[2] // 2,), g.dtype) for g in grads],
        [pltpu.SemaphoreType.DMA((n,)), pltpu.SemaphoreType.DMA((n,))], _sibling_only, body)


def _exchange_quarters(name, cid, parts):
    n = len(parts)

    def body(ins, outs, send_sems, recv_sems):
        x, y, c, chips = _place()
        cps = []
        for a in range(n):
            for k, (cx, cy) in enumerate(chips):
                cp = pltpu.make_async_remote_copy(
                    src_ref=ins[a].at[2 * cx + cy], dst_ref=outs[a].at[k], send_sem=send_sems.at[a, k],
                    recv_sem=recv_sems.at[a, k], device_id=(cx, cy, c), device_id_type=MESH)
                cp.start()
                cps.append(cp)
        for cp in cps:
            cp.wait()

    return _on_sequencer(
        name, cid, parts, [jax.ShapeDtypeStruct((3,) + p_.shape[1:], p_.dtype) for p_ in parts],
        [pltpu.SemaphoreType.DMA((n, 3)), pltpu.SemaphoreType.DMA((n, 3))], _same_core_of_other_chips, body)


def _swap_final_halves(name, cid, halves):
    n = len(halves)

    def body(ins, outs, send_sems, recv_sems):
        x, y, c, _ = _place()
        cps = []
        for a in range(n):
            cp = pltpu.make_async_remote_copy(src_ref=ins[a], dst_ref=outs[a], send_sem=send_sems.at[a],
                                              recv_sem=recv_sems.at[a], device_id=(x, y, 1 - c), device_id_type=MESH)
            cp.start()
            cps.append(cp)
        for cp in cps:
            cp.wait()

    return _on_sequencer(
        name, cid, halves, [jax.ShapeDtypeStruct(h.shape, h.dtype) for h in halves],
        [pltpu.SemaphoreType.DMA((n,)), pltpu.SemaphoreType.DMA((n,))], _sibling_only, body)


def _add_half(name, grad, recv):
    nchip, r, cfull = grad.shape
    hw = cfull // 2
    tile = _pick(r, max(BF16_ROWS, (256 * 1024 // hw) // BF16_ROWS * BF16_ROWS), q=BF16_ROWS)
    c = lax.axis_index("c")

    def body(c_ref, g_ref, r_ref, o_ref):
        o_ref[...] = _bf(g_ref[...] + r_ref[...])

    return pl.pallas_call(
        body, name=name,
        grid_spec=pltpu.PrefetchScalarGridSpec(
            num_scalar_prefetch=1, grid=(nchip, r // tile),
            in_specs=[pl.BlockSpec((1, tile, hw), lambda k, i, cr: (k, i, cr[0])),
                      pl.BlockSpec((1, tile, hw), lambda k, i, cr: (k, i, 0))],
            out_specs=pl.BlockSpec((1, tile, hw), lambda k, i, cr: (k, i, 0))),
        out_shape=jax.ShapeDtypeStruct((nchip, r, hw), BF16), compiler_params=_params("parallel", "parallel"),
    )(c.reshape(1).astype(jnp.int32), grad, recv)


def _add_quarters(name, part, recv):
    _, r, hw = part.shape
    tile = _pick(r, max(BF16_ROWS, (256 * 1024 // hw) // BF16_ROWS * BF16_ROWS), q=BF16_ROWS)
    me = 2 * lax.axis_index("x") + lax.axis_index("y")

    def body(me_ref, p_ref, r_ref, o_ref):
        f = lambda v: v.astype(F32)
        o_ref[...] = ((f(p_ref[0]) + f(r_ref[0])) + f(r_ref[1])) + f(r_ref[2])

    return pl.pallas_call(
        body, name=name,
        grid_spec=pltpu.PrefetchScalarGridSpec(
            num_scalar_prefetch=1, grid=(r // tile,),
            in_specs=[pl.BlockSpec((1, tile, hw), lambda i, mr: (mr[0], i, 0)),
                      pl.BlockSpec((3, tile, hw), lambda i, mr: (0, i, 0))],
            out_specs=pl.BlockSpec((tile, hw), lambda i, mr: (i, 0))),
        out_shape=jax.ShapeDtypeStruct((r, hw), F32), compiler_params=_params("parallel"),
    )(me.reshape(1).astype(jnp.int32), part, recv)


class _ReduceScatter:
    def __init__(self, tag, first_cid, grads):
        self.tag, self.cid = tag, first_cid
        self.stacks = [g.reshape(N_CHIPS, g.shape[0] // N_CHIPS, g.shape[1]) for g in grads]

    def start(self, anchor):
        self.stacks, anchor = lax.optimization_barrier((self.stacks, anchor))
        self.recv = _swap_halves_to_sibling(f"rs_swap_halves_{self.tag}", self.cid, self.stacks)
        return anchor

    def exchange(self, anchor):
        parts = [_add_half(f"rs_add_half_{self.tag}{a}", g, r) for a, (g, r) in enumerate(zip(self.stacks, self.recv))]
        self.parts, anchor = lax.optimization_barrier((parts, anchor))
        self.quarters = _exchange_quarters(f"rs_exchange_{self.tag}", self.cid + 1, self.parts)
        return anchor

    def join(self, anchor):
        halves = [_add_quarters(f"rs_add_quarters_{self.tag}{a}", p_, q_)
                  for a, (p_, q_) in enumerate(zip(self.parts, self.quarters))]
        self.halves, anchor = lax.optimization_barrier((halves, anchor))
        self.others = _swap_final_halves(f"rs_swap_final_{self.tag}", self.cid + 2, self.halves)
        return anchor

    def result(self):
        south = lax.axis_index("c") == 0
        return [jnp.concatenate([jnp.where(south, h, o), jnp.where(south, o, h)], axis=1)
                for h, o in zip(self.halves, self.others)]


def _allgather_small(pack):
    m_per, n = pack.shape

    def body(x_ref, out_ref, send_sems, recv_sems, local_sem):
        x, y, c, chips = _place()
        me, sibling = (x, y, c), (x, y, 1 - c)

        def rows(px, py, pc):
            return out_ref.at[pl.ds(pl.multiple_of((4 * px + 2 * py + pc) * m_per, SUBLANES), m_per), :]

        def copy(k, block, to, src=None):
            return pltpu.make_async_remote_copy(
                src_ref=rows(*block) if src is None else src, dst_ref=rows(*block),
                send_sem=send_sems.at[k], recv_sem=recv_sems.at[k], device_id=to, device_id_type=MESH)

        mine = pltpu.make_async_copy(x_ref, rows(*me), local_sem)
        mine.start()
        first = [copy(0, me, sibling, src=x_ref)]
        first += [copy(1 + j, me, (*chip, c), src=x_ref) for j, chip in enumerate(chips)]
        for cp in first:
            cp.start()
        passed = [copy(4 + j, (*chip, c), sibling) for j, chip in enumerate(chips)]
        for j, chip in enumerate(chips):
            copy(1 + j, (*chip, c), me).wait_recv()
            passed[j].start()
        copy(0, sibling, me).wait_recv()
        for j, chip in enumerate(chips):
            copy(4 + j, (*chip, 1 - c), me).wait_recv()
        for cp in first + passed:
            cp.wait_send()
        mine.wait()

    return pl.pallas_call(
        body, name="allgather_small_grads",
        out_shape=jax.ShapeDtypeStruct((N_DEV * m_per, n), pack.dtype),
        in_specs=[pl.BlockSpec(memory_space=pltpu.VMEM)], out_specs=pl.BlockSpec(memory_space=pltpu.VMEM),
        scratch_shapes=[pltpu.SemaphoreType.DMA((7,)), pltpu.SemaphoreType.DMA((7,)), pltpu.SemaphoreType.DMA],
        compiler_params=pltpu.CompilerParams(vmem_limit_bytes=VMEM_LIMIT),
    )(pack)


def _sum_devices(packs, m_per):
    tile = _pick(m_per, 512, q=SUBLANES)
    nt = m_per // tile

    def body(*refs):
        acc = refs[0][...]
        for r in refs[1:N_DEV]:
            acc = acc + r[...]
        refs[N_DEV][...] = acc

    return pl.pallas_call(
        body, name="sum_small_grads", grid=(nt,),
        in_specs=[pl.BlockSpec((tile, LANES), functools.partial(lambda i, k: (k * nt + i, 0), k=k)) for k in range(N_DEV)],
        out_specs=_nat(tile, LANES), out_shape=jax.ShapeDtypeStruct((m_per, LANES), F32),
        compiler_params=_params("parallel"),
    )(*([packs] * N_DEV))


def _tail_fwd(tag, alpha, h_in, adds, mix_gate, ln1, ln2, p_l, w, want_perm):
    h_mid, xh1, rs1, h_mid_b, _ = _ln_fwd(f"ln1_fwd_{tag}", alpha, h_in, adds, mix_gate, *ln1)
    gp = _matmul(f"ple_gate_fwd_{tag}", h_mid_b, w['wg'], 'nn')
    pw = _matmul(f"ple_proj_fwd_{tag}", p_l, w['plet'], 'nt')
    gg, uu, act = _ffn_in_swiglu(f"ffn_in_fwd_{tag}", h_mid_b, w['wit'])
    ffn = _matmul(f"ffn_out_fwd_{tag}", act, w['wo'], 'nn', tk=2816)
    h_out, xh2, rs2, _, h_perm = _ln_fwd(f"ln2_fwd_{tag}", alpha, h_mid, [(ffn, 'nat')],
                                         ('nat', (pw, 1, 0), (gp, 1, 0)), *ln2, want_perm=want_perm)
    saved = dict(h_mid_b=h_mid_b, xh1=xh1, rs1=rs1, gp=gp, pw=pw, g=gg, u=uu, act=act, xh2=xh2, rs2=rs2)
    return h_out, h_perm, saved


def _tail_bwd(tag, alpha, dparts, sv, ln1_g, ln2_g, p_l, w, mix_gate):
    d = sv['h_mid_b'].shape[1]
    dz2, dz2b, dgate, dg2, db2 = _ln_bwd(f"ln2_bwd_{tag}", dparts, sv['xh2'], sv['rs2'], ln2_g,
                                         gate=('nat', (sv['pw'], 1, 0), (sv['gp'], 1, 0)))
    grads = dict(ln2_g=dg2, ln2_b=db2)
    grads['plet'] = _matmul(f"ple_proj_dw_{tag}", dgate, p_l, 'tn', a_win=(0, d))
    grads['wg'] = _matmul(f"ple_gate_dw_{tag}", sv['h_mid_b'], dgate, 'tn', b_win=(d, d))
    dx_gate = _matmul(f"ple_gate_dx_{tag}", dgate, w['wg'], 'nt', a_win=(d, d))
    dact = _matmul(f"ffn_out_dx_{tag}", dz2b, w['wo'], 'nt', out_dtype=BF16, tn=1408)
    grads['wo'] = _matmul(f"ffn_out_dw_{tag}", sv['act'], dz2b, 'tn', tm=1408)
    dgu = _swiglu_bwd(f"swiglu_bwd_{tag}", sv['g'], sv['u'], dact)
    grads['wit'] = _matmul(f"ffn_in_dw_{tag}", dgu, sv['h_mid_b'], 'tn')
    dx_ffn = _matmul(f"ffn_in_dx_{tag}", dgu, w['wit'], 'nn', tk=2816)
    res = _ln_bwd(f"ln1_bwd_{tag}", [(dz2, 'nat', alpha), (dx_gate, 'nat', 1.0), (dx_ffn, 'nat', 1.0)],
                  sv['xh1'], sv['rs1'], ln1_g, gate=mix_gate)
    grads['ln1_g'], grads['ln1_b'] = res[-2], res[-1]
    return res[:-2], grads


def kernel(x, p, positions, attn_w_in, mla_q_norm, mla_w_q_b, mla_kv_norm, mla_w_kv_b, attn_w_out, s5_a_re, s5_a_im, s5_log_dt, s5_b_re, s5_b_im, s5_c_re, s5_c_im, s5_d, s5_w_glu, ln1_g, ln1_b, ffn_w_in, ffn_w_out, ple_w, ple_gate_w, ln2_g, ln2_b, loss_target, m_attn_w_in, m_mla_q_norm, m_mla_w_q_b, m_mla_kv_norm, m_mla_w_kv_b, m_attn_w_out, m_s5_a_re, m_s5_a_im, m_s5_log_dt, m_s5_b_re, m_s5_b_im, m_s5_c_re, m_s5_c_im, m_s5_d, m_s5_w_glu, m_ln1_g, m_ln1_b, m_ffn_w_in, m_ffn_w_out, m_ple_w, m_ple_gate_w, m_ln2_g, m_ln2_b, v_attn_w_in, v_mla_q_norm, v_mla_w_q_b, v_mla_kv_norm, v_mla_w_kv_b, v_attn_w_out, v_s5_a_re, v_s5_a_im, v_s5_log_dt, v_s5_b_re, v_s5_b_im, v_s5_c_re, v_s5_c_im, v_s5_d, v_s5_w_glu, v_ln1_g, v_ln1_b, v_ffn_w_in, v_ffn_w_out, v_ple_w, v_ple_gate_w, v_ln2_g, v_ln2_b):
    weights = dict(attn_w_in=attn_w_in, mla_q_norm=mla_q_norm, mla_w_q_b=mla_w_q_b, mla_kv_norm=mla_kv_norm,
                   mla_w_kv_b=mla_w_kv_b, attn_w_out=attn_w_out, s5_a_re=s5_a_re, s5_a_im=s5_a_im, s5_log_dt=s5_log_dt,
                   s5_b_re=s5_b_re, s5_b_im=s5_b_im, s5_c_re=s5_c_re, s5_c_im=s5_c_im, s5_d=s5_d, s5_w_glu=s5_w_glu,
                   ln1_g=ln1_g, ln1_b=ln1_b, ffn_w_in=ffn_w_in, ffn_w_out=ffn_w_out, ple_w=ple_w, ple_gate_w=ple_gate_w,
                   ln2_g=ln2_g, ln2_b=ln2_b)
    m_in = dict(attn_w_in=m_attn_w_in, mla_q_norm=m_mla_q_norm, mla_w_q_b=m_mla_w_q_b, mla_kv_norm=m_mla_kv_norm,
                mla_w_kv_b=m_mla_w_kv_b, attn_w_out=m_attn_w_out, s5_a_re=m_s5_a_re, s5_a_im=m_s5_a_im,
                s5_log_dt=m_s5_log_dt, s5_b_re=m_s5_b_re, s5_b_im=m_s5_b_im, s5_c_re=m_s5_c_re, s5_c_im=m_s5_c_im,
                s5_d=m_s5_d, s5_w_glu=m_s5_w_glu, ln1_g=m_ln1_g, ln1_b=m_ln1_b, ffn_w_in=m_ffn_w_in,
                ffn_w_out=m_ffn_w_out, ple_w=m_ple_w, ple_gate_w=m_ple_gate_w, ln2_g=m_ln2_g, ln2_b=m_ln2_b)
    v_in = dict(attn_w_in=v_attn_w_in, mla_q_norm=v_mla_q_norm, mla_w_q_b=v_mla_w_q_b, mla_kv_norm=v_mla_kv_norm,
                mla_w_kv_b=v_mla_w_kv_b, attn_w_out=v_attn_w_out, s5_a_re=v_s5_a_re, s5_a_im=v_s5_a_im,
                s5_log_dt=v_s5_log_dt, s5_b_re=v_s5_b_re, s5_b_im=v_s5_b_im, s5_c_re=v_s5_c_re, s5_c_im=v_s5_c_im,
                s5_d=v_s5_d, s5_w_glu=v_s5_w_glu, ln1_g=v_ln1_g, ln1_b=v_ln1_b, ffn_w_in=v_ffn_w_in,
                ffn_w_out=v_ffn_w_out, ple_w=v_ple_w, ple_gate_w=v_ple_gate_w, ln2_g=v_ln2_g, ln2_b=v_ln2_b)
    names = list(weights)

    s, d = x.shape[1], x.shape[2]
    depth = ln1_g.shape[0]
    assert depth == 2
    alpha = (2.0 * depth) ** 0.25
    ql, kvl = mla_q_norm.shape[1], mla_kv_norm.shape[1]
    in_cols = N_CHIPS * attn_w_in.shape[2]
    heads = N_CHIPS * mla_w_q_b.shape[2] // (NOPE + ROPE)
    hps = heads // N_CHIPS
    dw = (in_cols - ql - kvl - ROPE) // 3
    dh = dw // DHD
    assert ql % LANES == 0 and kvl == ql and dw % DHD == 0 and heads % N_CHIPS == 0
    ngroups, nstate = s5_a_re.shape[1], s5_a_re.shape[2]
    assert nstate == S5_STATE and ngroups * S5_GROUP == d and d % LANES == 0
    nsl = d // LANES
    seg_len = s // SUBLANES
    n_sq = seg_len.bit_length() - 1
    assert 1 << n_sq == seg_len, "the segment length of the S5 scan must be a power of two"
    for window, dil in DIL_BRANCHES:
        assert window // dil == DIL_STEPS and (s // dil) % DIL_STEPS == 0
    me = 2 * lax.axis_index("x") + lax.axis_index("y")

    xb = x[0]
    target = loss_target[0]
    p_layers = [p[0, 0], p[1, 0]]
    pos = positions[0].astype(F32).reshape(s, 1)
    inv_freq = ROPE_THETA ** (-jnp.arange(ROPE // 2, dtype=F32) / (ROPE // 2))
    invf = jnp.concatenate([inv_freq, inv_freq, jnp.zeros((LANES - ROPE,), F32)]).reshape(1, LANES)
    slopes = 2.0 ** (-8.0 * jnp.arange(1, dh + 1, dtype=F32) / dh)
    slopes = jnp.broadcast_to(jnp.repeat(slopes, SUBLANES)[:, None], (dh * SUBLANES, LANES))

    wqb_t = mla_w_q_b[0].T.reshape(hps, NOPE + ROPE, ql)
    wqb_t = jnp.pad(wqb_t, ((0, 0), (0, QK_PAD - NOPE - ROPE), (0, 0))).reshape(hps * QK_PAD, ql)
    d_cols = max(d // N_CHIPS, 2 * LANES)
    d_pad = jnp.zeros((SUBLANES, d_cols), F32).at[0, :d // N_CHIPS].set(s5_d[0])
    shards = [_bf(attn_w_in[0].T), _bf(wqb_t), _bf(mla_w_kv_b[0].T), _bf(attn_w_out[0]), _bf(s5_w_glu[0].T)]
    for l in range(depth):
        shards += [_bf(ffn_w_in[l].T), _bf(ffn_w_out[l]), _bf(ple_w[l].T), _bf(ple_gate_w[l])]
    shards.append(d_pad)
    first, later = lax.optimization_barrier((list(_gather_weights(shards[:3])), shards[3:]))
    gathered = first + _gather_weights_async(later)
    full = [g.reshape(N_CHIPS * g.shape[1], g.shape[2]) for g in gathered]
    win_t, wqb_t_f, wkv_t, wout, wglu_t = full[:5]
    lw = [dict(wit=full[5 + 4 * l], wo=full[6 + 4 * l], plet=full[7 + 4 * l], wg=full[8 + 4 * l]) for l in range(depth)]
    dvec = full[-1].reshape(N_CHIPS, SUBLANES, d_cols)[:, 0, :d // N_CHIPS].reshape(1, d)
    lat = ql + kvl
    win_t = jnp.concatenate([win_t[:lat + ROPE], jnp.zeros((LANES - ROPE, d), BF16), win_t[lat + ROPE:]], axis=0)
    kpe_cb = lat // LANES
    q_cb = kpe_cb + 1
    a_cb = heads * VDIM // LANES

    xbb = _bf(xb)
    proj = _matmul("attn_in_fwd", xbb, win_t, 'nt', tn=1408)
    nrm = _rms_fwd(proj, ql, kvl, mla_q_norm[0], mla_kv_norm[0])
    q_raw = _matmul("mla_q_up_fwd", nrm, wqb_t_f, 'nt', a_win=(0, ql))
    kv = _matmul("mla_kv_up_fwd", nrm, wkv_t, 'nt', a_win=(ql, kvl))
    qf, kf, vv = _rope_prep(q_raw, kv, proj, kpe_cb, pos, invf, heads)
    out_a, lse_a = _mla_fwd(qf, kf, vv.T, heads)
    out_b, lse_b = _dil_fused_fwd(proj, slopes, dh, q_cb)
    att = _concat_bf16("attn_heads_concat", out_a, out_b)
    mix0 = _matmul("attn_out_fwd", att, wout, 'nn')
    h2, h2p, sv0 = _tail_fwd("l0", alpha, xb, [(mix0, 'nat')], None, (ln1_g[0], ln1_b[0]), (ln2_g[0], ln2_b[0]),
                             p_layers[0], lw[0], want_perm=True)

    rep = lambda a: jnp.repeat(a, S5_GROUP, axis=0)
    ag = (s5_a_re[0], s5_a_im[0], jnp.broadcast_to(s5_log_dt[0][:, None], (ngroups, nstate)))
    a16 = tuple(rep(a) for a in ag)
    b16 = tuple(b[0].transpose(0, 2, 1).reshape(ngroups * S5_GROUP, nstate) for b in (s5_b_re, s5_b_im))
    abr, abi, apr, api, bbr, bbi = _s5_discretise(*a16, *b16, n_sq)
    ab_tile = _slab_tile(abr[::S5_GROUP], abi[::S5_GROUP], nsl)
    ap_tile = _slab_tile(apr[::S5_GROUP], api[::S5_GROUP], nsl)
    bblk = _bf(_slab_in_matrix(bbr.reshape(ngroups, S5_GROUP, nstate), bbi.reshape(ngroups, S5_GROUP, nstate), nsl))
    cblk = _bf(_slab_out_matrix(s5_c_re[0], s5_c_im[0], nsl))
    ends = _s5_pass1(h2p, bblk, ab_tile)
    cinx = _s5_carry("s5_carry_fwd", ends, ap_tile, False)
    xtrue, ypre, zg = _s5_pass2(h2p, bblk, cinx, ab_tile, cblk, dvec)
    vg = _matmul("s5_glu_fwd", zg, wglu_t, 'nt')
    glu_gate = ('perm', (vg, 2, 0), (vg, 2, 1))
    h4, _, sv1 = _tail_fwd("l1", alpha, h2, [], glu_gate, (ln1_g[1], ln1_b[1]), (ln2_g[1], ln2_b[1]),
                           p_layers[1], lw[1], want_perm=False)
    loss = lax.psum(jnp.sum(_loss_partial(h4, target)), ("x", "y", "c"))

    (dz1_1, _, dvg), g1 = _tail_bwd("l1", alpha, [(h4, 'nat', 1.0 / d), (target, 'nat', -1.0 / d)], sv1, ln1_g[1],
                                    ln2_g[1], p_layers[1], lw[1], glu_gate)
    d_wglu_t = _matmul("s5_glu_dw", dvg, zg, 'tn')
    dzg = _matmul("s5_glu_dx", dvg, wglu_t, 'nn')
    rs_l1 = _ReduceScatter("l1", 2, [d_wglu_t, g1['wit'], g1['wo'], g1['plet'], g1['wg']])
    dzg = rs_l1.start(dzg)
    starts, dy, dd = _s5_bwd_pass1(dzg, ypre, cblk, ab_tile, h2p)
    cinl = _s5_carry("s5_carry_bwd", starts, ap_tile, True)
    du_p, d_bblk, d_cblk, d_ab = _s5_bwd_pass2(dy, cblk, cinl, ab_tile, xtrue, cinx, h2p, bblk, dvec)
    gbb = _slab_in_extract(d_bblk, nsl)
    g_c_re, g_c_im = _slab_out_extract(jnp.swapaxes(d_cblk, 1, 2), nsl)
    d_ab = d_ab[::SUBLANES]
    gab = (d_ab[:, :SLAB_COLS].reshape(ngroups, nstate), d_ab[:, SLAB_COLS:].reshape(ngroups, nstate))
    g_a_re, g_a_im, g_log_dt, g_b_re, g_b_im = _s5_discretise_bwd(a16, b16, ag, gab, gbb)
    unt = lambda b: b.reshape(ngroups, S5_GROUP, nstate).transpose(0, 2, 1)

    du_p = rs_l1.exchange(du_p)
    (dz1_0, dz1_0b), g0 = _tail_bwd("l0", alpha, [(dz1_1, 'nat', alpha), (du_p, 'perm', 1.0)], sv0, ln1_g[0], ln2_g[0],
                                    p_layers[0], lw[0], None)
    dz1_0b = rs_l1.join(dz1_0b)
    d_wout = _matmul("attn_out_dw", att, dz1_0b, 'tn')
    rs_l0 = _ReduceScatter("l0", 5, [g0['wit'], g0['wo'], g0['plet'], g0['wg'], d_wout])
    dz1_0b = rs_l0.start(dz1_0b)
    datt = _matmul("attn_out_dx", dz1_0b, wout, 'nt')
    do, delta, delta_t = _attn_bwd_prep(datt, out_a, out_b)
    dqf, dkf, dvv = _mla_bwd(qf, kf, vv, do, lse_a, delta_t, heads, 0)
    dqf = rs_l0.exchange(dqf)
    dq_raw, dkv, dkpe = _rope_unprep(dqf, dkf, dvv, pos, invf, heads)
    d_wqb_t = _matmul("mla_q_up_dw", dq_raw, nrm, 'tn', b_win=(0, ql))
    d_wkv_t = _matmul("mla_kv_up_dw", dkv, nrm, 'tn', b_win=(ql, kvl))
    dnq = _matmul("mla_q_up_dx", dq_raw, wqb_t_f, 'nn')
    dnkv = _matmul("mla_kv_up_dx", dkv, wkv_t, 'nn')
    dqd, dkd, dvd = _dil_fused_bwd(proj, slopes, datt, lse_b, delta, dh, q_cb, a_cb)
    dkpe = rs_l0.join(dkpe)
    dproj, g_gq, g_gkv = _dproj_assemble(proj, dnq, dnkv, dkpe, [dqd], [dkd], [dvd], mla_q_norm[0], mla_kv_norm[0], ql)
    d_win_t = _matmul("attn_in_dw", dproj, xbb, 'tn', tm=1408)

    d_win_t = jnp.concatenate([d_win_t[:lat + ROPE], d_win_t[lat + LANES:]], axis=0)
    rs_at = _ReduceScatter("attn", 8, [d_win_t, d_wqb_t, d_wkv_t])
    r_wglu, r_wit1, r_wo1, r_plet1, r_wg1 = rs_l1.result()
    r_wit0, r_wo0, r_plet0, r_wg0, r_wout = rs_l0.result()
    dproj = rs_at.start(dproj)
    dx_attn = _matmul("attn_in_dx", dproj, win_t, 'nn')
    dx_attn, (r_wit0, r_wit1, r_wo0, r_wo1) = rs_at.exchange((dx_attn, (r_wit0, r_wit1, r_wo0, r_wo1)))
    grad_x = _axpy("grad_x", alpha, dz1_0, dx_attn)
    grad_x = rs_at.join(grad_x)
    r_win, r_wqb, r_wkv = rs_at.result()
    r_wqb = r_wqb.reshape(hps, QK_PAD, ql)[:, :NOPE + ROPE].reshape(hps * (NOPE + ROPE), ql)
    grads = dict(attn_w_in=r_win.T[None], mla_w_q_b=r_wqb.T[None], mla_w_kv_b=r_wkv.T[None], attn_w_out=r_wout[None],
                 s5_w_glu=r_wglu.T[None],
                 ffn_w_in=jnp.stack([r_wit0.T, r_wit1.T]), ffn_w_out=jnp.stack([r_wo0, r_wo1]),
                 ple_w=jnp.stack([r_plet0.T, r_plet1.T]), ple_gate_w=jnp.stack([r_wg0, r_wg1]))

    small = dict(mla_q_norm=g_gq, mla_kv_norm=g_gkv, s5_a_re=g_a_re, s5_a_im=g_a_im, s5_log_dt=g_log_dt,
                 s5_b_re=unt(g_b_re), s5_b_im=unt(g_b_im), s5_c_re=g_c_re, s5_c_im=g_c_im, s5_d=dd[0],
                 ln1_g=jnp.stack([g0['ln1_g'], g1['ln1_g']]), ln1_b=jnp.stack([g0['ln1_b'], g1['ln1_b']]),
                 ln2_g=jnp.stack([g0['ln2_g'], g1['ln2_g']]), ln2_b=jnp.stack([g0['ln2_b'], g1['ln2_b']]))
    flat = jnp.concatenate([v_.reshape(-1) for v_ in small.values()])
    m_per = -(-flat.shape[0] // (LANES * SUBLANES)) * SUBLANES
    pack = jnp.pad(flat, (0, m_per * LANES - flat.shape[0])).reshape(m_per, LANES)
    total = _sum_devices(_allgather_small(pack), m_per).reshape(-1)
    off = 0
    for k_, v_ in small.items():
        n_ = v_.size
        piece = total[off:off + n_]
        off += n_
        if k_ == 's5_d':
            grads[k_] = lax.dynamic_slice(piece, (me * (d // N_CHIPS),), (d // N_CHIPS,)).reshape(weights[k_].shape)
        else:
            grads[k_] = piece.reshape(weights[k_].shape)

    deltas, new_m, new_v = {}, {}, {}
    for k_ in names:
        w_ = weights[k_]
        shape = w_.shape
        if w_.ndim == 3 and w_.shape[-1] >= LANES:
            two_d = (shape[0] * shape[1], shape[2])
        elif w_.ndim == 4:
            two_d = (shape[0] * shape[1], shape[2] * shape[3])
        else:
            two_d = (1, w_.size) if w_.ndim == 2 and shape[0] == 1 else (shape[0], w_.size // shape[0])
        dl, nm, nv = _adamw(f"adamw_{k_}", w_.reshape(two_d), grads[k_].reshape(two_d), m_in[k_].reshape(two_d),
                            v_in[k_].reshape(two_d))
        deltas[k_], new_m[k_], new_v[k_] = dl.reshape(shape), nm.reshape(shape), nv.reshape(shape)

    return (loss, grad_x[None], *[grads[k_] for k_ in names], *[deltas[k_] for k_ in names],
            *[new_m[k_] for k_ in names], *[new_v[k_] for k_ in names])
```

```python
import functools
import math

import jax
import jax.numpy as jnp
from jax import lax
from jax.experimental import pallas as pl
from jax.experimental.pallas import tpu as pltpu
from jax.experimental.pallas import tpu_sc as plsc

F32 = jnp.float32
BF16 = jnp.bfloat16
MESH = pl.DeviceIdType.MESH

LANES = 128
SUBLANES = 8
BF16_ROWS = 16
VMEM_LIMIT = 48 * 2 ** 20
N_CHIPS = 4
N_DEV = 8

NOPE = 128
ROPE = 64
VDIM = 128
QK_PAD = 256
DHD = 128
DIL_STEPS = 128
DIL_BRANCHES = ((128, 1), (512, 4), (2048, 16))
ROPE_THETA = 10000.0
S5_GROUP = 16
S5_STATE = 64
SLAB_GROUPS = LANES // S5_GROUP
SLAB_COLS = SLAB_GROUPS * S5_STATE
NEG = -1e30
LN_EPS = 1e-5
RMS_EPS = 1e-6

ADAM_LR = 0.001
ADAM_B1 = 0.9
ADAM_B2 = 0.999
ADAM_EPS = 1e-08
ADAM_WD = 0.01
ADAM_STEP = 10

NN = ((1,), (0,))
NT = ((1,), (1,))
TN = ((0,), (0,))


def _dot(a, b, dims):
    return lax.dot_general(a, b, (dims, ((), ())), preferred_element_type=F32)


def _bf(v):
    return v.astype(BF16)


def _pick(n, target, q=LANES, also=0):
    g = math.gcd(n, also) if also else n
    if g <= target and g == n:
        return n
    best = None
    for t in range(q, min(g, target) + 1, q):
        if g % t == 0:
            best = t
    assert best is not None, (n, target, q, also)
    return best


def _params(*sem):
    return pltpu.CompilerParams(dimension_semantics=sem, vmem_limit_bytes=VMEM_LIMIT)


def _sigmoid(v):
    return 1.0 / (1.0 + jnp.exp(-v))


def _matmul(name, a, b, form, out_dtype=F32, a_win=None, b_win=None, tm=1024, tn=1024, tk=2048):
    c0, aw = a_win if a_win else (0, a.shape[1])
    if form == 'nt':
        assert b_win is None
        n, kdim = b.shape
        d0 = 0
    else:
        kdim = b.shape[0]
        d0, n = b_win if b_win else (0, b.shape[1])
    if form == 'tn':
        m = aw
        assert a.shape[0] == kdim, (name, a.shape, b.shape)
        tm = _pick(m, tm, also=c0)
        tk = _pick(kdim, tk)
        a_off = c0 // tm
    else:
        m = a.shape[0]
        assert aw == kdim, (name, a.shape, b.shape, a_win)
        tm = _pick(m, tm)
        tk = _pick(kdim, tk, also=c0)
        a_off = c0 // tk
    tn = _pick(n, tn, also=d0)
    b_off = d0 // tn
    nk = kdim // tk
    dims = {'nn': NN, 'nt': NT, 'tn': TN}[form]

    def body(a_ref, b_ref, o_ref, *acc):
        prod = _dot(_bf(a_ref[...]), _bf(b_ref[...]), dims)
        if nk == 1:
            o_ref[...] = prod.astype(o_ref.dtype)
            return
        acc_ref, = acc
        k = pl.program_id(2)

        @pl.when(k == 0)
        def _():
            acc_ref[...] = prod

        @pl.when((k > 0) & (k < nk - 1))
        def _():
            acc_ref[...] += prod

        @pl.when(k == nk - 1)
        def _():
            o_ref[...] = (acc_ref[...] + prod).astype(o_ref.dtype)

    if form == 'tn':
        a_spec = pl.BlockSpec((tk, tm), lambda i, j, k: (k, i + a_off))
    else:
        a_spec = pl.BlockSpec((tm, tk), lambda i, j, k: (i, k + a_off))
    if form == 'nt':
        b_spec = pl.BlockSpec((tn, tk), lambda i, j, k: (j, k))
    else:
        b_spec = pl.BlockSpec((tk, tn), lambda i, j, k: (k, j + b_off))
    return pl.pallas_call(
        body, name=name,
        grid=(m // tm, n // tn, nk),
        in_specs=[a_spec, b_spec],
        out_specs=pl.BlockSpec((tm, tn), lambda i, j, k: (i, j)),
        out_shape=jax.ShapeDtypeStruct((m, n), out_dtype),
        scratch_shapes=[pltpu.VMEM((tm, tn), F32)] if nk > 1 else [],
        compiler_params=_params("parallel", "parallel", "arbitrary"),
    )(a, b)


def _nat(tile, width, cb=0):
    return pl.BlockSpec((tile, width), lambda i: (i, cb))


def _perm(tile, width, seg_tiles, ncb=1, cb=0):
    return pl.BlockSpec((tile, width), lambda i: (i % seg_tiles, (i // seg_tiles) * ncb + cb))


def _whole(shape):
    return pl.BlockSpec(shape, lambda i: (0,) * len(shape))


def _perm_view(a):
    s, w = a.shape
    return a.reshape(s // SUBLANES, SUBLANES * w)


def _row_spec(a, layout, tile, width, ncb=1, cb=0):
    if layout == 'nat':
        return a, _nat(tile, width, cb)
    seg_tiles = a.shape[0] // SUBLANES // tile
    return _perm_view(a), _perm(tile, width, seg_tiles, ncb, cb)


def _ln_fwd(name, alpha, a, adds, gate, g, b, want_perm=False, tile=256):
    s, d = a.shape
    n_add = len(adds)
    has_gate = gate is not None

    def body(*refs):
        a_ref = refs[0]
        add_refs = refs[1:1 + n_add]
        pos = 1 + n_add
        if has_gate:
            val_ref, pre_ref = refs[pos], refs[pos + 1]
            pos += 2
        g_ref, b_ref = refs[pos], refs[pos + 1]
        outs = refs[pos + 2:]
        z = alpha * a_ref[...]
        for r in add_refs:
            z = z + r[...]
        if has_gate:
            z = z + val_ref[...] * _sigmoid(pre_ref[...])
        mu = jnp.mean(z, axis=-1, keepdims=True)
        zc = z - mu
        var = jnp.mean(zc * zc, axis=-1, keepdims=True)
        rstd = lax.rsqrt(var + LN_EPS)
        xhat = zc * rstd
        h = xhat * g_ref[...] + b_ref[...]
        outs[0][...] = h
        outs[1][...] = xhat
        outs[2][...] = jnp.broadcast_to(rstd, (tile, LANES))
        outs[3][...] = _bf(h)
        if want_perm:
            outs[4][...] = h

    ins, specs = [a], [_nat(tile, d)]
    for arr, layout in adds:
        x_, sp = _row_spec(arr, layout, tile, d)
        ins.append(x_)
        specs.append(sp)
    if has_gate:
        layout = gate[0]
        for arr, ncb, cb in gate[1:]:
            x_, sp = _row_spec(arr, layout, tile, d, ncb=ncb, cb=cb)
            ins.append(x_)
            specs.append(sp)
    ins += [g.reshape(1, d), b.reshape(1, d)]
    specs += [_whole((1, d)), _whole((1, d))]
    out_shape = [jax.ShapeDtypeStruct((s, d), F32), jax.ShapeDtypeStruct((s, d), F32),
                 jax.ShapeDtypeStruct((s, LANES), F32), jax.ShapeDtypeStruct((s, d), BF16)]
    out_specs = [_nat(tile, d), _nat(tile, d), _nat(tile, LANES), _nat(tile, d)]
    if want_perm:
        seg_tiles = s // SUBLANES // tile
        out_shape.append(jax.ShapeDtypeStruct((s // SUBLANES, SUBLANES * d), F32))
        out_specs.append(_perm(tile, d, seg_tiles))
    res = pl.pallas_call(
        body, name=name, grid=(s // tile,), in_specs=specs, out_specs=out_specs, out_shape=out_shape,
        compiler_params=_params("parallel"),
    )(*ins)
    return res[0], res[1], res[2], res[3], (res[4].reshape(s, d) if want_perm else None)


def _ln_bwd(name, dparts, xhat, rstd, g, gate=None, tile=256):
    s, d = xhat.shape
    n_part = len(dparts)
    coefs = [c for _, _, c in dparts]
    has_gate = gate is not None

    def body(*refs):
        part_refs = refs[:n_part]
        xhat_ref, rstd_ref, g_ref = refs[n_part:n_part + 3]
        pos = n_part + 3
        if has_gate:
            val_ref, pre_ref = refs[pos], refs[pos + 1]
            pos += 2
        outs = list(refs[pos:])
        dz_ref = outs.pop(0)
        dzb_ref = outs.pop(0)
        dgate_ref = outs.pop(0) if has_gate else None
        dg_ref, db_ref = outs
        dh = coefs[0] * part_refs[0][...]
        for c, r in zip(coefs[1:], part_refs[1:]):
            dh = dh + c * r[...]
        xh = xhat_ref[...]
        dxh = dh * g_ref[...]
        m1 = jnp.mean(dxh, axis=-1, keepdims=True)
        m2 = jnp.mean(dxh * xh, axis=-1, keepdims=True)
        dz = rstd_ref[:, 0:1] * (dxh - m1 - xh * m2)
        dz_ref[...] = dz
        dzb_ref[...] = _bf(dz)
        if has_gate:
            sg = _sigmoid(pre_ref[...])
            dval = dz * sg
            dpre = dz * val_ref[...] * sg * (1.0 - sg)
            dgate_ref[...] = jnp.concatenate([_bf(dval), _bf(dpre)], axis=1)

        @pl.when(pl.program_id(0) == 0)
        def _():
            dg_ref[...] = jnp.zeros_like(dg_ref)
            db_ref[...] = jnp.zeros_like(db_ref)

        dg_ref[0:1, :] += jnp.sum(dh * xh, axis=0, keepdims=True)
        db_ref[0:1, :] += jnp.sum(dh, axis=0, keepdims=True)

    ins, specs = [], []
    for arr, layout, _ in dparts:
        x_, sp = _row_spec(arr, layout, tile, d)
        ins.append(x_)
        specs.append(sp)
    ins += [xhat, rstd, g.reshape(1, d)]
    specs += [_nat(tile, d), _nat(tile, LANES), _whole((1, d))]
    gate_layout = None
    if has_gate:
        gate_layout = gate[0]
        for arr, ncb, cb in gate[1:]:
            x_, sp = _row_spec(arr, gate_layout, tile, d, ncb=ncb, cb=cb)
            ins.append(x_)
            specs.append(sp)
    seg_tiles = s // SUBLANES // tile
    out_shape = [jax.ShapeDtypeStruct((s, d), F32), jax.ShapeDtypeStruct((s, d), BF16)]
    out_specs = [_nat(tile, d), _nat(tile, d)]
    if has_gate:
        if gate_layout == 'nat':
            out_shape.append(jax.ShapeDtypeStruct((s, 2 * d), BF16))
            out_specs.append(_nat(tile, 2 * d))
        else:
            out_shape.append(jax.ShapeDtypeStruct((s // SUBLANES, SUBLANES * 2 * d), BF16))
            out_specs.append(_perm(tile, 2 * d, seg_tiles))
    out_shape += [jax.ShapeDtypeStruct((SUBLANES, d), F32)] * 2
    out_specs += [_whole((SUBLANES, d))] * 2
    res = list(pl.pallas_call(
        body, name=name, grid=(s // tile,), in_specs=specs, out_specs=out_specs, out_shape=out_shape,
        compiler_params=_params("arbitrary"),
    )(*ins))
    out = [res.pop(0), res.pop(0)]
    if has_gate:
        out.append(res.pop(0).reshape(s, 2 * d))
    out += [res[0][0], res[1][0]]
    return out


def _loss_partial(h, target, tile=256):
    s, d = h.shape

    def body(h_ref, t_ref, o_ref):
        @pl.when(pl.program_id(0) == 0)
        def _():
            o_ref[...] = jnp.zeros_like(o_ref)

        e = h_ref[...] - t_ref[...]
        sq = e * e
        part = sq[:, 0:LANES]
        for k in range(1, d // LANES):
            part = part + sq[:, k * LANES:(k + 1) * LANES]
        o_ref[0:1, :] += jnp.sum(part, axis=0, keepdims=True) * (0.5 / d)

    return pl.pallas_call(
        body, name="loss_partial", grid=(s // tile,), in_specs=[_nat(tile, d), _nat(tile, d)],
        out_specs=_whole((SUBLANES, LANES)), out_shape=jax.ShapeDtypeStruct((SUBLANES, LANES), F32),
        compiler_params=_params("arbitrary"),
    )(h, target)


def _ffn_in_swiglu(name, a, wit, tm=1024, tn=704):
    m, kdim = a.shape
    f = wit.shape[0] // 2
    tm, tn = _pick(m, tm), _pick(f, tn)
    nj = f // tn

    def body(a_ref, bg_ref, bu_ref, g_ref, u_ref, act_ref):
        av = _bf(a_ref[...])
        gg = _dot(av, bg_ref[...], NT)
        uu = _dot(av, bu_ref[...], NT)
        g_ref[...] = gg
        u_ref[...] = uu
        act_ref[...] = _bf(gg * _sigmoid(gg) * uu)

    ospec = pl.BlockSpec((tm, tn), lambda i, j: (i, j))
    return pl.pallas_call(
        body, name=name, grid=(m // tm, nj),
        in_specs=[pl.BlockSpec((tm, kdim), lambda i, j: (i, 0)), pl.BlockSpec((tn, kdim), lambda i, j: (j, 0)),
                  pl.BlockSpec((tn, kdim), lambda i, j: (j + nj, 0))],
        out_specs=[ospec, ospec, ospec],
        out_shape=[jax.ShapeDtypeStruct((m, f), F32), jax.ShapeDtypeStruct((m, f), F32), jax.ShapeDtypeStruct((m, f), BF16)],
        compiler_params=_params("parallel", "parallel"),
    )(a, wit, wit)


def _swiglu_bwd(name, g, u, dact, tile=128):
    s, f = g.shape
    f2 = 2 * f

    def body(g_ref, u_ref, da_ref, o_ref):
        gg = g_ref[...]
        sg = _sigmoid(gg)
        da = da_ref[...].astype(F32)
        silu = gg * sg
        o_ref[:, :f] = _bf(da * u_ref[...] * (sg + silu * (1.0 - sg)))
        o_ref[:, f:] = _bf(da * silu)

    return pl.pallas_call(
        body, name=name, grid=(s // tile,),
        in_specs=[_nat(tile, f), _nat(tile, f), _nat(tile, f)], out_specs=_nat(tile, f2),
        out_shape=jax.ShapeDtypeStruct((s, f2), BF16), compiler_params=_params("parallel"),
    )(g, u, dact)


def _rms_fwd(proj, ql, kvl, gq, gkv, tile=256):
    s = proj.shape[0]
    assert ql == kvl

    def body(q_ref, kv_ref, gq_ref, gkv_ref, o_ref):
        def nrm(x, gg):
            return x * lax.rsqrt(jnp.mean(x * x, axis=-1, keepdims=True) + RMS_EPS) * gg

        o_ref[...] = jnp.concatenate([_bf(nrm(q_ref[...], gq_ref[...])), _bf(nrm(kv_ref[...], gkv_ref[...]))], axis=1)

    return pl.pallas_call(
        body, name="mla_rms_fwd", grid=(s // tile,),
        in_specs=[_nat(tile, ql, 0), _nat(tile, kvl, 1), _whole((1, ql)), _whole((1, kvl))],
        out_specs=_nat(tile, ql + kvl), out_shape=jax.ShapeDtypeStruct((s, ql + kvl), BF16),
        compiler_params=_params("parallel"),
    )(proj, proj, gq.reshape(1, ql), gkv.reshape(1, kvl))


def _rope_coeffs(pos, invf):
    ang = pos * invf
    cs, sn = jnp.cos(ang), jnp.sin(ang)
    lane = lax.broadcasted_iota(jnp.int32, ang.shape, 1)
    half = ROPE // 2
    c = jnp.where(lane < ROPE, cs, 0.0)
    sa = jnp.where(lane < half, -sn, 0.0)
    sb = jnp.where((lane >= half) & (lane < ROPE), sn, 0.0)
    return c, sa, sb


def _rope_prep(q_raw, kv, proj, kpe_cb, pos, invf, heads, tile=256):
    s = q_raw.shape[0]
    half = ROPE // 2

    def body(q_ref, kv_ref, kpe_ref, pos_ref, invf_ref, qf_ref, kf_ref, v_ref):
        c, sa, sb = _rope_coeffs(pos_ref[...], invf_ref[...])

        def rope(t):
            return t * c + pltpu.roll(t, LANES - half, 1) * sa + pltpu.roll(t, half, 1) * sb

        kr = _bf(rope(kpe_ref[...]))
        for hh in range(heads):
            o = hh * QK_PAD
            qf_ref[:, o:o + NOPE] = _bf(q_ref[:, o:o + NOPE])
            qf_ref[:, o + NOPE:o + QK_PAD] = _bf(rope(q_ref[:, o + NOPE:o + QK_PAD]))
            kf_ref[:, o:o + NOPE] = _bf(kv_ref[:, o:o + NOPE])
            kf_ref[:, o + NOPE:o + QK_PAD] = kr
            v_ref[:, hh * VDIM:(hh + 1) * VDIM] = _bf(kv_ref[:, o + NOPE:o + QK_PAD])

    w = heads * QK_PAD
    return pl.pallas_call(
        body, name="mla_rope_prep", grid=(s // tile,),
        in_specs=[_nat(tile, w), _nat(tile, w), _nat(tile, LANES, kpe_cb), _nat(tile, 1), _whole((1, LANES))],
        out_specs=[_nat(tile, w), _nat(tile, w), _nat(tile, heads * VDIM)],
        out_shape=[jax.ShapeDtypeStruct((s, w), BF16), jax.ShapeDtypeStruct((s, w), BF16),
                   jax.ShapeDtypeStruct((s, heads * VDIM), BF16)],
        compiler_params=_params("parallel"),
    )(q_raw, kv, proj, pos, invf)


def _rope_unprep(dqf, dkf, dv, pos, invf, heads, tile=256):
    s = dqf.shape[0]
    half = ROPE // 2

    def body(dq_ref, dk_ref, dv_ref, pos_ref, invf_ref, dqr_ref, dkv_ref, dkpe_ref):
        c, sa, sb = _rope_coeffs(pos_ref[...], invf_ref[...])

        def unrope(gt):
            return gt * c + pltpu.roll(gt * sa, half, 1) + pltpu.roll(gt * sb, LANES - half, 1)

        dkpe = jnp.zeros((tile, LANES), F32)
        for hh in range(heads):
            o = hh * QK_PAD
            dqr_ref[:, o:o + NOPE] = _bf(dq_ref[:, o:o + NOPE])
            dqr_ref[:, o + NOPE:o + QK_PAD] = _bf(unrope(dq_ref[:, o + NOPE:o + QK_PAD]))
            dkv_ref[:, o:o + NOPE] = _bf(dk_ref[:, o:o + NOPE])
            dkv_ref[:, o + NOPE:o + QK_PAD] = _bf(dv_ref[:, hh * VDIM:(hh + 1) * VDIM])
            dkpe = dkpe + dk_ref[:, o + NOPE:o + QK_PAD]
        dkpe_ref[...] = unrope(dkpe)

    w = heads * QK_PAD
    return pl.pallas_call(
        body, name="mla_rope_unprep", grid=(s // tile,),
        in_specs=[_nat(tile, w), _nat(tile, w), _nat(tile, heads * VDIM), _nat(tile, 1), _whole((1, LANES))],
        out_specs=[_nat(tile, w), _nat(tile, w), _nat(tile, LANES)],
        out_shape=[jax.ShapeDtypeStruct((s, w), BF16), jax.ShapeDtypeStruct((s, w), BF16),
                   jax.ShapeDtypeStruct((s, LANES), F32)],
        compiler_params=_params("parallel"),
    )(dqf, dkf, dv, pos, invf)


LOG2E = 1.4426950408889634
MLA_SCALE = (NOPE + ROPE) ** -0.5


def _mla_scores_t(k, q, t, masked):
    sc = _dot(k, q, NT) * (MLA_SCALE * LOG2E)
    if masked:
        row = lax.broadcasted_iota(jnp.int32, (t, t), 0)
        col = lax.broadcasted_iota(jnp.int32, (t, t), 1)
        sc = jnp.where(row <= col, sc, NEG)
    return sc


def _mla_fwd(qf, kf, vt, heads, t=512):
    s = qf.shape[0]
    t = min(t, s)
    nq = s // t

    def body(q_ref, k_ref, vt_ref, o_ref, lse_ref, m_ref, l_ref, acc_ref):
        i = pl.program_id(1)
        m_ref[...] = jnp.full_like(m_ref, NEG)
        l_ref[...] = jnp.zeros_like(l_ref)
        acc_ref[...] = jnp.zeros_like(acc_ref)
        q = q_ref[...]

        def block(j, masked):
            r0 = pl.multiple_of(j * t, t)
            sc = _mla_scores_t(k_ref[pl.ds(r0, t), :], q, t, masked)
            m_prev = m_ref[0:1, :]
            m_new = jnp.maximum(m_prev, jnp.max(sc, axis=0, keepdims=True))
            corr = jnp.exp2(m_prev - m_new)
            p = jnp.exp2(sc - m_new)
            l_new = corr * l_ref[0:1, :] + jnp.sum(p, axis=0, keepdims=True)
            acc_ref[...] = corr * acc_ref[...] + _dot(vt_ref[:, pl.ds(r0, t)], _bf(p), NN)
            m_ref[...] = jnp.broadcast_to(m_new, (SUBLANES, t))
            l_ref[...] = jnp.broadcast_to(l_new, (SUBLANES, t))

        def unmasked(j, carry):
            block(j, False)
            return carry

        lax.fori_loop(0, i, unmasked, 0)
        block(i, True)
        o_ref[...] = (acc_ref[...] / l_ref[0:1, :]).T
        lse_ref[...] = m_ref[...] + jnp.log(l_ref[...]) * LOG2E

    return pl.pallas_call(
        body, name="mla_flash_fwd", grid=(heads, nq),
        in_specs=[pl.BlockSpec((t, QK_PAD), lambda h, i: (i, h)), pl.BlockSpec((s, QK_PAD), lambda h, i: (0, h)),
                  pl.BlockSpec((VDIM, s), lambda h, i: (h, 0))],
        out_specs=[pl.BlockSpec((t, VDIM), lambda h, i: (i, h)), pl.BlockSpec((SUBLANES, t), lambda h, i: (h, i))],
        out_shape=[jax.ShapeDtypeStruct((s, heads * VDIM), F32), jax.ShapeDtypeStruct((heads * SUBLANES, s), F32)],
        scratch_shapes=[pltpu.VMEM((SUBLANES, t), F32), pltpu.VMEM((SUBLANES, t), F32), pltpu.VMEM((VDIM, t), F32)],
        compiler_params=_params("parallel", "arbitrary"),
    )(qf, kf, vt)


def _mla_bwd(qf, kf, v, do, lse_t, delta_t, heads, do_cb0, t=512):
    s = qf.shape[0]
    t = min(t, s)
    nq = s // t

    def body(q_ref, k_ref, v_ref, do_ref, lse_ref, dl_ref, dq_ref, dk_ref, dv_ref, acc_ref):
        i = pl.program_id(1)

        @pl.when(i == 0)
        def _():
            dk_ref[...] = jnp.zeros_like(dk_ref)
            dv_ref[...] = jnp.zeros_like(dv_ref)

        acc_ref[...] = jnp.zeros_like(acc_ref)
        q, dob = q_ref[...], do_ref[...]
        lse, dl = lse_ref[0:1, :], dl_ref[0:1, :]

        def block(j, masked):
            r0 = pl.multiple_of(j * t, t)
            k = k_ref[pl.ds(r0, t), :]
            p = jnp.exp2(_mla_scores_t(k, q, t, masked) - lse)
            dp = _dot(v_ref[pl.ds(r0, t), :], dob, NT)
            ds = _bf(p * (dp - dl) * MLA_SCALE)
            acc_ref[...] += _dot(ds, k, TN)
            dk_ref[pl.ds(r0, t), :] += _dot(ds, q, NN)
            dv_ref[pl.ds(r0, t), :] += _dot(_bf(p), dob, NN)

        def unmasked(j, carry):
            block(j, False)
            return carry

        lax.fori_loop(0, i, unmasked, 0)
        block(i, True)
        dq_ref[...] = acc_ref[...]

    qs = lambda w, off=0: pl.BlockSpec((t, w), lambda h, i: (i, h + off))
    ks = lambda w: pl.BlockSpec((s, w), lambda h, i: (0, h))
    st = pl.BlockSpec((SUBLANES, t), lambda h, i: (h, i))
    return pl.pallas_call(
        body, name="mla_flash_bwd", grid=(heads, nq),
        in_specs=[qs(QK_PAD), ks(QK_PAD), ks(VDIM), qs(VDIM, do_cb0), st, st],
        out_specs=[qs(QK_PAD), ks(QK_PAD), ks(VDIM)],
        out_shape=[jax.ShapeDtypeStruct((s, heads * QK_PAD), F32), jax.ShapeDtypeStruct((s, heads * QK_PAD), F32),
                   jax.ShapeDtypeStruct((s, heads * VDIM), F32)],
        scratch_shapes=[pltpu.VMEM((t, QK_PAD), F32)],
        compiler_params=_params("parallel", "arbitrary"),
    )(qf, kf, v, do, lse_t, delta_t)


DIL_BLOCK = 2048


def _dil_unit_rows(u, dil, block):
    sub = u // dil
    return u % dil + (dil * DIL_STEPS) * sub, sub == 0


def _dil_rows(base, dil):
    return pl.ds(base, DIL_STEPS, stride=dil) if dil > 1 else pl.ds(base, DIL_STEPS)


def _dil_unit_scores(q, kp, kc, slope, dil, no_prev):
    sc = jnp.concatenate([_dot(q, kp, NT), _dot(q, kc, NT)], axis=1) * (DHD ** -0.5)
    row = lax.broadcasted_iota(jnp.int32, (DIL_STEPS, 2 * DIL_STEPS), 0)
    col = lax.broadcasted_iota(jnp.int32, (DIL_STEPS, 2 * DIL_STEPS), 1)
    dist = row + DIL_STEPS - col
    valid = (dist >= 0) & (dist <= DIL_STEPS) & (jnp.logical_not(no_prev) | (col >= DIL_STEPS))
    return jnp.where(valid, sc - slope * (dil * dist).astype(F32), NEG)


def _dil_in_specs(pw, dh, q_cb, block, rev_nb=None):
    blk = (lambda i: i) if rev_nb is None else (lambda i: rev_nb - 1 - i)
    own = lambda off: pl.BlockSpec((block, DHD), lambda h, i: (blk(i), off + h))
    prev = lambda off: pl.BlockSpec((block, DHD), lambda h, i: (jnp.maximum(blk(i) - 1, 0), off + h))
    return [own(q_cb), own(q_cb + dh), prev(q_cb + dh), own(q_cb + 2 * dh), prev(q_cb + 2 * dh)]


def _dil_fused_fwd(proj, slopes, dh, q_cb):
    s, pw = proj.shape
    block = min(DIL_BLOCK, s)
    nb = s // block
    n_units = block // DIL_STEPS
    nbr = len(DIL_BRANCHES)
    assert block >= DIL_STEPS * max(d for _, d in DIL_BRANCHES)

    def body(q_ref, kc_ref, kp_ref, vc_ref, vp_ref, sl_ref, o_ref, lse_ref, kk, vv, *per_branch):
        og, mg, lg = per_branch[:nbr], per_branch[nbr:2 * nbr], per_branch[2 * nbr:]
        i = pl.program_id(1)
        kk[0:block, :] = kp_ref[...]
        kk[block:, :] = kc_ref[...]
        vv[0:block, :] = vp_ref[...]
        vv[block:, :] = vc_ref[...]
        slope = sl_ref[0:1, 0:1]
        for g, (_, dil) in enumerate(DIL_BRANCHES):
            rows = functools.partial(_dil_rows, dil=dil)

            def unit(u, dil=dil, rows=rows):
                q0, first = _dil_unit_rows(u, dil, block)
                q = _bf(q_ref[rows(q0), :])
                kc, kp = _bf(kk[rows(block + q0), :]), _bf(kk[rows(block + q0 - dil * DIL_STEPS), :])
                vc, vp = _bf(vv[rows(block + q0), :]), _bf(vv[rows(block + q0 - dil * DIL_STEPS), :])
                sc = _dil_unit_scores(q, kp, kc, slope, dil, first & (i == 0))
                m = jnp.max(sc, axis=-1, keepdims=True)
                e = jnp.exp(sc - m)
                o = _dot(_bf(e[:, :DIL_STEPS]), vp, NN) + _dot(_bf(e[:, DIL_STEPS:]), vc, NN)
                return q0, o, m, jnp.sum(e, axis=-1, keepdims=True)

            def pair(u, carry, g=g, rows=rows, unit=unit):
                for q0, o, m, lsum in (unit(u), unit(u + n_units // 2)):
                    og[g][rows(q0), :] = o
                    mg[g][rows(q0), :] = jnp.broadcast_to(m, (DIL_STEPS, LANES))
                    lg[g][rows(q0), :] = jnp.broadcast_to(lsum, (DIL_STEPS, LANES))
                return carry

            lax.fori_loop(0, n_units // 2, pair, 0, unroll=2)
        m_all = mg[0][...]
        for g in range(1, nbr):
            m_all = jnp.maximum(m_all, mg[g][...])
        tot = jnp.zeros((block, LANES), F32)
        acc = jnp.zeros((block, DHD), F32)
        for g in range(nbr):
            w = jnp.exp(mg[g][...] - m_all)
            tot = tot + w * lg[g][...]
            acc = acc + w * og[g][...]
        o_ref[...] = acc / tot
        lse_ref[...] = m_all + jnp.log(tot)

    ospec = pl.BlockSpec((block, DHD), lambda h, i: (i, h))
    return pl.pallas_call(
        body, name="dil_fused_fwd", grid=(dh, nb),
        in_specs=_dil_in_specs(pw, dh, q_cb, block) + [pl.BlockSpec((SUBLANES, LANES), lambda h, i: (h, 0))],
        out_specs=[ospec, ospec], out_shape=[jax.ShapeDtypeStruct((s, dh * DHD), F32)] * 2,
        scratch_shapes=[pltpu.VMEM((2 * block, DHD), F32), pltpu.VMEM((2 * block, DHD), F32)]
        + [pltpu.VMEM((block, DHD), F32)] * (3 * nbr),
        compiler_params=_params("parallel", "arbitrary"),
    )(proj, proj, proj, proj, proj, slopes)


def _dil_fused_bwd(proj, slopes, datt, lse, delta, dh, q_cb, b_cb0):
    s, pw = proj.shape
    block = min(DIL_BLOCK, s)
    nb = s // block
    n_units = block // DIL_STEPS
    scale = DHD ** -0.5

    def body(q_ref, kc_ref, kp_ref, vc_ref, vp_ref, sl_ref, do_ref, lse_ref, dl_ref, dq_ref, dk_ref, dv_ref,
             kk, vv, dkk, dvv, carry_k, carry_v):
        ii = pl.program_id(1)
        i = nb - 1 - ii

        @pl.when(ii == 0)
        def _():
            carry_k[...] = jnp.zeros_like(carry_k)
            carry_v[...] = jnp.zeros_like(carry_v)

        kk[0:block, :] = kp_ref[...]
        kk[block:, :] = kc_ref[...]
        vv[0:block, :] = vp_ref[...]
        vv[block:, :] = vc_ref[...]
        dkk[...] = jnp.zeros_like(dkk)
        dvv[...] = jnp.zeros_like(dvv)
        dq_ref[...] = jnp.zeros_like(dq_ref)
        slope = sl_ref[0:1, 0:1]
        for _, dil in DIL_BRANCHES:
            rows = functools.partial(_dil_rows, dil=dil)

            def unit(u, dil=dil, rows=rows):
                q0, first = _dil_unit_rows(u, dil, block)
                cur, prev = rows(block + q0), rows(block + q0 - dil * DIL_STEPS)
                q = _bf(q_ref[rows(q0), :])
                kc, kp, vc, vp = _bf(kk[cur, :]), _bf(kk[prev, :]), _bf(vv[cur, :]), _bf(vv[prev, :])
                dob = _bf(do_ref[rows(q0), :])
                sc = _dil_unit_scores(q, kp, kc, slope, dil, first & (i == 0))
                p = jnp.exp(sc - lse_ref[rows(q0), 0:1])
                dp = jnp.concatenate([_dot(dob, vp, NT), _dot(dob, vc, NT)], axis=1)
                ds = _bf(p * (dp - dl_ref[rows(q0), 0:1]) * scale)
                pb = _bf(p)
                return (rows(q0), cur, prev, _dot(ds[:, :DIL_STEPS], kp, NN) + _dot(ds[:, DIL_STEPS:], kc, NN),
                        _dot(ds[:, :DIL_STEPS], q, TN), _dot(ds[:, DIL_STEPS:], q, TN),
                        _dot(pb[:, :DIL_STEPS], dob, TN), _dot(pb[:, DIL_STEPS:], dob, TN))

            def pair(u, carry, unit=unit):
                for qrows, cur, prev, dq, dkp, dkc, dvp, dvc in (unit(u), unit(u + n_units // 2)):
                    dq_ref[qrows, :] += dq
                    dkk[prev, :] += dkp
                    dkk[cur, :] += dkc
                    dvv[prev, :] += dvp
                    dvv[cur, :] += dvc
                return carry

            lax.fori_loop(0, n_units // 2, pair, 0, unroll=2)
        dk_ref[...] = dkk[block:, :] + carry_k[...]
        dv_ref[...] = dvv[block:, :] + carry_v[...]
        carry_k[...] = dkk[0:block, :]
        carry_v[...] = dvv[0:block, :]

    rev = lambda i: nb - 1 - i
    mspec = pl.BlockSpec((block, DHD), lambda h, i: (rev(i), b_cb0 + h))
    ospec = pl.BlockSpec((block, DHD), lambda h, i: (rev(i), h))
    big = lambda: pltpu.VMEM((2 * block, DHD), F32)
    return pl.pallas_call(
        body, name="dil_fused_bwd", grid=(dh, nb),
        in_specs=_dil_in_specs(pw, dh, q_cb, block, rev_nb=nb)
        + [pl.BlockSpec((SUBLANES, LANES), lambda h, i: (h, 0)), mspec, ospec, mspec],
        out_specs=[ospec, ospec, ospec], out_shape=[jax.ShapeDtypeStruct((s, dh * DHD), F32)] * 3,
        scratch_shapes=[big(), big(), big(), big(), pltpu.VMEM((block, DHD), F32), pltpu.VMEM((block, DHD), F32)],
        compiler_params=_params("parallel", "arbitrary"),
    )(proj, proj, proj, proj, proj, slopes, datt, lse, delta)


def _concat_bf16(name, a, b, tile=256):
    s, wa = a.shape
    wb = b.shape[1]

    def body(a_ref, b_ref, o_ref):
        o_ref[...] = jnp.concatenate([_bf(a_ref[...]), _bf(b_ref[...])], axis=1)

    return pl.pallas_call(
        body, name=name, grid=(s // tile,), in_specs=[_nat(tile, wa), _nat(tile, wb)], out_specs=_nat(tile, wa + wb),
        out_shape=jax.ShapeDtypeStruct((s, wa + wb), BF16), compiler_params=_params("parallel"),
    )(a, b)


def _attn_bwd_prep(datt, out_a, out_b, tile=256):
    s, mixw = datt.shape
    wa = out_a.shape[1]
    heads_a = wa // LANES

    def body(d_ref, a_ref, b_ref, do_ref, dl_ref, dlt_ref):
        d = d_ref[...]
        do_ref[...] = _bf(d)
        prod = d * jnp.concatenate([a_ref[...], b_ref[...]], axis=1)
        for hh in range(mixw // LANES):
            sl = slice(hh * LANES, (hh + 1) * LANES)
            dl = jnp.broadcast_to(jnp.sum(prod[:, sl], axis=-1, keepdims=True), (tile, LANES))
            dl_ref[:, sl] = dl
            if hh < heads_a:
                dlt_ref[hh * SUBLANES:(hh + 1) * SUBLANES, :] = dl.T[0:SUBLANES, :]

    return pl.pallas_call(
        body, name="attn_bwd_prep", grid=(s // tile,),
        in_specs=[_nat(tile, mixw), _nat(tile, wa), _nat(tile, mixw - wa)],
        out_specs=[_nat(tile, mixw), _nat(tile, mixw), pl.BlockSpec((heads_a * SUBLANES, tile), lambda i: (0, i))],
        out_shape=[jax.ShapeDtypeStruct((s, mixw), BF16), jax.ShapeDtypeStruct((s, mixw), F32),
                   jax.ShapeDtypeStruct((heads_a * SUBLANES, s), F32)],
        compiler_params=_params("parallel"),
    )(datt, out_a, out_b)


def _dproj_assemble(proj, dnq, dnkv, dkpe, dqs, dks, dvs, gq, gkv, ql, tile=256):
    s, pw = proj.shape
    dw = dqs[0].shape[1]
    nbr = len(dqs)

    def body(*refs):
        ql_ref, kvl_ref, dnq_ref, dnkv_ref, dkpe_ref = refs[:5]
        br = refs[5:5 + 3 * nbr]
        gq_ref, gkv_ref = refs[5 + 3 * nbr:7 + 3 * nbr]
        dp_ref, dgq_ref, dgkv_ref = refs[7 + 3 * nbr:]

        @pl.when(pl.program_id(0) == 0)
        def _():
            dgq_ref[...] = jnp.zeros_like(dgq_ref)
            dgkv_ref[...] = jnp.zeros_like(dgkv_ref)

        def rms_bwd(x, dy, gg, dg_ref):
            r = lax.rsqrt(jnp.mean(x * x, axis=-1, keepdims=True) + RMS_EPS)
            xh = x * r
            dxh = dy * gg
            dg_ref[0:1, :] += jnp.sum(dy * xh, axis=0, keepdims=True)
            return r * (dxh - xh * jnp.mean(dxh * xh, axis=-1, keepdims=True))

        pieces = [_bf(rms_bwd(ql_ref[...], dnq_ref[...], gq_ref[...], dgq_ref)),
                  _bf(rms_bwd(kvl_ref[...], dnkv_ref[...], gkv_ref[...], dgkv_ref)),
                  _bf(dkpe_ref[...])]
        for k in range(3):
            acc = br[k * nbr][...]
            for r in br[k * nbr + 1:(k + 1) * nbr]:
                acc = acc + r[...]
            pieces.append(_bf(acc))
        dp_ref[...] = jnp.concatenate(pieces, axis=1)

    res = pl.pallas_call(
        body, name="dproj_assemble", grid=(s // tile,),
        in_specs=[_nat(tile, ql, 0), _nat(tile, ql, 1), _nat(tile, ql), _nat(tile, ql), _nat(tile, LANES)]
        + [_nat(tile, dw)] * (3 * nbr) + [_whole((1, ql)), _whole((1, ql))],
        out_specs=[_nat(tile, pw), _whole((SUBLANES, ql)), _whole((SUBLANES, ql))],
        out_shape=[jax.ShapeDtypeStruct((s, pw), BF16), jax.ShapeDtypeStruct((SUBLANES, ql), F32),
                   jax.ShapeDtypeStruct((SUBLANES, ql), F32)],
        compiler_params=_params("arbitrary"),
    )(proj, proj, dnq, dnkv, dkpe, *dqs, *dks, *dvs, gq.reshape(1, ql), gkv.reshape(1, ql))
    return res[0], res[1][0], res[2][0]


def _axpy(name, alpha, a, b, tile=256):
    s, d = a.shape

    def body(a_ref, b_ref, o_ref):
        o_ref[...] = alpha * a_ref[...] + b_ref[...]

    return pl.pallas_call(
        body, name=name, grid=(s // tile,), in_specs=[_nat(tile, d), _nat(tile, d)], out_specs=_nat(tile, d),
        out_shape=jax.ShapeDtypeStruct((s, d), F32), compiler_params=_params("parallel"),
    )(a, b)


def _cmul(ar, ai, br, bi):
    return ar * br - ai * bi, ar * bi + ai * br


def _s5_discretise(a_re, a_im, log_dt, b_re, b_im, n_sq):
    shape = a_re.shape

    def body(ar_ref, ai_ref, ldt_ref, br_ref, bi_ref, abr_ref, abi_ref, apr_ref, api_ref, bbr_ref, bbi_ref):
        ar, ai = ar_ref[...], ai_ref[...]
        dt = jnp.exp(ldt_ref[...])
        e = jnp.exp(ar * dt)
        abr, abi = e * jnp.cos(ai * dt), e * jnp.sin(ai * dt)
        den = ar * ar + ai * ai
        qr = ((abr - 1.0) * ar + abi * ai) / den
        qi = (abi * ar - (abr - 1.0) * ai) / den
        bbr, bbi = _cmul(qr, qi, br_ref[...], bi_ref[...])
        abr_ref[...], abi_ref[...] = abr, abi
        bbr_ref[...], bbi_ref[...] = bbr, bbi
        pr, pi = abr, abi
        for _ in range(n_sq):
            pr, pi = _cmul(pr, pi, pr, pi)
        apr_ref[...], api_ref[...] = pr, pi

    return pl.pallas_call(
        body, name="s5_discretise", out_shape=[jax.ShapeDtypeStruct(shape, F32)] * 6,
        compiler_params=pltpu.CompilerParams(vmem_limit_bytes=VMEM_LIMIT),
    )(a_re, a_im, log_dt, b_re, b_im)


def _s5_discretise_bwd(a16, b16, ag, gab, gbb):
    rows, p = a16[0].shape
    g = rows // S5_GROUP

    def disc(ar, ai, ldt):
        dt = jnp.exp(ldt)
        e = jnp.exp(ar * dt)
        abr, abi = e * jnp.cos(ai * dt), e * jnp.sin(ai * dt)
        den = ar * ar + ai * ai
        inv_r, inv_i = ar / den, -ai / den
        qr, qi = _cmul(abr - 1.0, abi, inv_r, inv_i)
        return dt, abr, abi, inv_r, inv_i, qr, qi

    def body(ar16_ref, ai16_ref, ldt16_ref, br_ref, bi_ref, ar_ref, ai_ref, ldt_ref, gar_ref, gai_ref, gbr_ref, gbi_ref,
             dar_ref, dai_ref, dldt_ref, dbr_ref, dbi_ref):
        _, _, _, _, _, qr16, qi16 = disc(ar16_ref[...], ai16_ref[...], ldt16_ref[...])
        gbr, gbi = gbr_ref[...], gbi_ref[...]
        dbr_ref[...], dbi_ref[...] = _cmul(qr16, -qi16, gbr, gbi)
        cr, ci = _cmul(br_ref[...], -bi_ref[...], gbr, gbi)
        gqr = jnp.sum(cr.reshape(g, S5_GROUP, p), axis=1)
        gqi = jnp.sum(ci.reshape(g, S5_GROUP, p), axis=1)
        ar, ai = ar_ref[...], ai_ref[...]
        dt, abr, abi, inv_r, inv_i, qr, qi = disc(ar, ai, ldt_ref[...])
        t_r, t_i = _cmul(inv_r, -inv_i, gqr, gqi)
        gab_r = gar_ref[...] + t_r
        gab_i = gai_ref[...] + t_i
        qa_r, qa_i = _cmul(qr, qi, inv_r, inv_i)
        a1_r, a1_i = _cmul(qa_r, -qa_i, gqr, gqi)
        gl_r, gl_i = _cmul(abr, -abi, gab_r, gab_i)
        dar_ref[...] = dt * gl_r - a1_r
        dai_ref[...] = dt * gl_i - a1_i
        gdt = jnp.sum(ar * gl_r + ai * gl_i, axis=-1, keepdims=True)
        dldt_ref[...] = gdt * dt[:, 0:1]

    return pl.pallas_call(
        body, name="s5_discretise_bwd",
        out_shape=[jax.ShapeDtypeStruct((g, p), F32), jax.ShapeDtypeStruct((g, p), F32),
                   jax.ShapeDtypeStruct((g, 1), F32), jax.ShapeDtypeStruct((rows, p), F32),
                   jax.ShapeDtypeStruct((rows, p), F32)],
        compiler_params=pltpu.CompilerParams(vmem_limit_bytes=VMEM_LIMIT),
    )(*a16, *b16, *ag, *gab, *gbb)


def _slab_tile(re, im, nsl):
    row = jnp.concatenate([re.reshape(nsl, SLAB_COLS), im.reshape(nsl, SLAB_COLS)], axis=-1)
    return jnp.repeat(row, SUBLANES, axis=0)


def _slab_in_matrix(b_re, b_im, nsl):
    eye = jnp.eye(SLAB_GROUPS, dtype=F32)

    def blk(b):
        b = b.reshape(nsl, SLAB_GROUPS, S5_GROUP, S5_STATE)
        return jnp.einsum('sgcp,gh->sgchp', b, eye).reshape(nsl, LANES, SLAB_COLS)

    return jnp.concatenate([blk(b_re), blk(b_im)], axis=-1)


def _slab_in_extract(m, nsl):
    eye = jnp.eye(SLAB_GROUPS, dtype=F32)

    def ext(x_):
        x_ = x_.reshape(nsl, SLAB_GROUPS, S5_GROUP, SLAB_GROUPS, S5_STATE)
        return jnp.einsum('sgchp,gh->sgcp', x_, eye).reshape(nsl * LANES, S5_STATE)

    return ext(m[..., :SLAB_COLS]), ext(m[..., SLAB_COLS:])


def _slab_out_matrix(c_re, c_im, nsl):
    eye = jnp.eye(SLAB_GROUPS, dtype=F32)

    def blk(c):
        c = c.reshape(nsl, SLAB_GROUPS, S5_GROUP, S5_STATE)
        return jnp.einsum('sgcp,gh->sgphc', c, eye).reshape(nsl, SLAB_COLS, LANES)

    return jnp.concatenate([blk(c_re), -blk(c_im)], axis=1)


def _slab_out_extract(m, nsl):
    eye = jnp.eye(SLAB_GROUPS, dtype=F32)

    def ext(x_):
        x_ = x_.reshape(nsl, SLAB_GROUPS, S5_STATE, SLAB_GROUPS, S5_GROUP)
        return jnp.einsum('sgphc,gh->sgcp', x_, eye).reshape(nsl * SLAB_GROUPS, S5_GROUP, S5_STATE)

    return ext(m[:, :SLAB_COLS]), -ext(m[:, SLAB_COLS:])


def _gelu(y):
    t = jnp.tanh(0.7978845608028654 * (y + 0.044715 * y * y * y))
    return 0.5 * y * (1.0 + t)


def _gelu_grad(y):
    t = jnp.tanh(0.7978845608028654 * (y + 0.044715 * y * y * y))
    return 0.5 * (1.0 + t) + 0.5 * y * (1.0 - t * t) * 0.7978845608028654 * (1.0 + 3.0 * 0.044715 * y * y)


def _scan_rows(ref, n_steps, ar, ai, state, reverse, conj, keep=True):
    sgn = -1.0 if conj else 1.0

    def step(k, carry):
        xr, xi = carry
        t = (n_steps - 1 - k) if reverse else k
        r0 = pl.multiple_of(t * SUBLANES, SUBLANES)
        nr = ar * xr - sgn * ai * xi + ref[pl.ds(r0, SUBLANES), :SLAB_COLS]
        ni = ar * xi + sgn * ai * xr + ref[pl.ds(r0, SUBLANES), SLAB_COLS:]
        if keep:
            ref[pl.ds(r0, SUBLANES), :SLAB_COLS] = nr
            ref[pl.ds(r0, SUBLANES), SLAB_COLS:] = ni
        return nr, ni

    return lax.fori_loop(0, n_steps, step, state, unroll=4)


def _s5_pass1(hp, bblk, ab_tile, rc=1024):
    s, d = hp.shape
    nsl = d // LANES
    rc = min(rc, s)
    nch = s // rc
    w = 2 * SLAB_COLS

    def body(u_ref, b_ref, ab_ref, end_ref, st_ref, x_ref):
        j = pl.program_id(1)

        @pl.when(j == 0)
        def _():
            st_ref[...] = jnp.zeros_like(st_ref)

        x_ref[...] = _dot(_bf(u_ref[...]), b_ref[0], NN)
        xr, xi = _scan_rows(x_ref, rc // SUBLANES, ab_ref[:, :SLAB_COLS], ab_ref[:, SLAB_COLS:],
                            (st_ref[:, :SLAB_COLS], st_ref[:, SLAB_COLS:]), False, False, keep=False)
        st_ref[:, :SLAB_COLS] = xr
        st_ref[:, SLAB_COLS:] = xi

        @pl.when(j == nch - 1)
        def _():
            end_ref[...] = st_ref[...]

    return pl.pallas_call(
        body, name="s5_scan_local", grid=(nsl, nch),
        in_specs=[pl.BlockSpec((rc, LANES), lambda sl, j: (j, sl)), pl.BlockSpec((1, LANES, w), lambda sl, j: (sl, 0, 0)),
                  pl.BlockSpec((SUBLANES, w), lambda sl, j: (sl, 0))],
        out_specs=pl.BlockSpec((SUBLANES, w), lambda sl, j: (sl, 0)),
        out_shape=jax.ShapeDtypeStruct((nsl * SUBLANES, w), F32),
        scratch_shapes=[pltpu.VMEM((SUBLANES, w), F32), pltpu.VMEM((rc, w), F32)],
        compiler_params=_params("parallel", "arbitrary"),
    )(hp, bblk, ab_tile)


def _s5_carry(name, ends, ap_tile, reverse):
    rows, w = ends.shape
    nsl = rows // SUBLANES
    sgn = -1.0 if reverse else 1.0

    def body(e_ref, ap_ref, c_ref):
        pr, pi = ap_ref[0:1, :SLAB_COLS], sgn * ap_ref[0:1, SLAB_COLS:]
        tr = jnp.zeros((1, SLAB_COLS), F32)
        ti = jnp.zeros((1, SLAB_COLS), F32)
        order = range(SUBLANES - 1, -1, -1) if reverse else range(SUBLANES)
        for seg in order:
            c_ref[seg:seg + 1, :SLAB_COLS] = tr
            c_ref[seg:seg + 1, SLAB_COLS:] = ti
            mr, mi = _cmul(pr, pi, tr, ti)
            tr = e_ref[seg:seg + 1, :SLAB_COLS] + mr
            ti = e_ref[seg:seg + 1, SLAB_COLS:] + mi

    spec = pl.BlockSpec((SUBLANES, w), lambda sl: (sl, 0))
    return pl.pallas_call(
        body, name=name, grid=(nsl,), in_specs=[spec, spec], out_specs=spec,
        out_shape=jax.ShapeDtypeStruct((rows, w), F32), compiler_params=_params("parallel"),
    )(ends, ap_tile)


def _s5_pass2(hp, bblk, cin, ab_tile, cblk, dvec, rc=1024):
    s, d = hp.shape
    nsl = d // LANES
    rc = min(rc, s)
    nch = s // rc
    w = 2 * SLAB_COLS

    def body(h_ref, b_ref, cin_ref, ab_ref, c_ref, d_ref, x_ref, y_ref, z_ref, st_ref):
        j = pl.program_id(1)

        @pl.when(j == 0)
        def _():
            st_ref[...] = cin_ref[...]

        hv = h_ref[...]
        x_ref[...] = _dot(_bf(hv), b_ref[0], NN)
        xr, xi = _scan_rows(x_ref, rc // SUBLANES, ab_ref[:, :SLAB_COLS], ab_ref[:, SLAB_COLS:],
                            (st_ref[:, :SLAB_COLS], st_ref[:, SLAB_COLS:]), False, False)
        st_ref[:, :SLAB_COLS] = xr
        st_ref[:, SLAB_COLS:] = xi
        y = _dot(_bf(x_ref[...]), c_ref[0], NN) + d_ref[...] * hv
        y_ref[...] = y
        z_ref[...] = _bf(_gelu(y))

    tile = lambda wd: pl.BlockSpec((rc, wd), lambda sl, j: (j, sl))
    small = pl.BlockSpec((SUBLANES, w), lambda sl, j: (sl, 0))
    return pl.pallas_call(
        body, name="s5_scan_carry_out", grid=(nsl, nch),
        in_specs=[tile(LANES), pl.BlockSpec((1, LANES, w), lambda sl, j: (sl, 0, 0)), small, small,
                  pl.BlockSpec((1, w, LANES), lambda sl, j: (sl, 0, 0)), pl.BlockSpec((1, LANES), lambda sl, j: (0, sl))],
        out_specs=[tile(w), tile(LANES), tile(LANES)],
        out_shape=[jax.ShapeDtypeStruct((s, nsl * w), F32), jax.ShapeDtypeStruct((s, d), F32),
                   jax.ShapeDtypeStruct((s, d), BF16)],
        scratch_shapes=[pltpu.VMEM((SUBLANES, w), F32)],
        compiler_params=_params("parallel", "arbitrary"),
    )(hp, bblk, cin, ab_tile, cblk, dvec)


def _s5_bwd_pass1(dzg, ypre, cblk, ab_tile, hp, rc=1024):
    s, d = hp.shape
    nsl = d // LANES
    rc = min(rc, s)
    nch = s // rc
    w = 2 * SLAB_COLS

    def body(dz_ref, y_ref, c_ref, ab_ref, h_ref, st_out_ref, dy_ref, dd_ref, st_ref, lam_ref):
        j = pl.program_id(1)

        @pl.when(j == 0)
        def _():
            st_ref[...] = jnp.zeros_like(st_ref)
            dd_ref[...] = jnp.zeros_like(dd_ref)

        dy = dz_ref[...] * _gelu_grad(y_ref[...])
        dy_ref[...] = dy
        dd_ref[0:1, :] += jnp.sum(dy * h_ref[...], axis=0, keepdims=True)
        lam_ref[...] = _dot(_bf(dy), c_ref[0], NT)
        lr, li = _scan_rows(lam_ref, rc // SUBLANES, ab_ref[:, :SLAB_COLS], ab_ref[:, SLAB_COLS:],
                            (st_ref[:, :SLAB_COLS], st_ref[:, SLAB_COLS:]), True, True, keep=False)
        st_ref[:, :SLAB_COLS] = lr
        st_ref[:, SLAB_COLS:] = li

        @pl.when(j == nch - 1)
        def _():
            st_out_ref[...] = st_ref[...]

    tile = lambda wd: pl.BlockSpec((rc, wd), lambda sl, j: (nch - 1 - j, sl))
    small = pl.BlockSpec((SUBLANES, w), lambda sl, j: (sl, 0))
    return pl.pallas_call(
        body, name="s5_adjoint_local", grid=(nsl, nch),
        in_specs=[tile(LANES), tile(LANES), pl.BlockSpec((1, w, LANES), lambda sl, j: (sl, 0, 0)), small, tile(LANES)],
        out_specs=[small, tile(LANES), pl.BlockSpec((SUBLANES, LANES), lambda sl, j: (0, sl))],
        out_shape=[jax.ShapeDtypeStruct((nsl * SUBLANES, w), F32),
                   jax.ShapeDtypeStruct((s, d), F32), jax.ShapeDtypeStruct((SUBLANES, d), F32)],
        scratch_shapes=[pltpu.VMEM((SUBLANES, w), F32), pltpu.VMEM((rc, w), F32)],
        compiler_params=_params("parallel", "arbitrary"),
    )(dzg, ypre, cblk, ab_tile, hp)


def _s5_bwd_pass2(dy, cblk, cinl, ab_tile, xtrue, cinx, hp, bblk, dvec, rc=1024):
    s, d = hp.shape
    nsl = d // LANES
    rc = min(rc, s)
    nch = s // rc
    w = 2 * SLAB_COLS
    n_steps = rc // SUBLANES

    def body(dy_ref, c_ref, cl_ref, ab_ref, x_ref, xp_ref, cx_ref, h_ref, b_ref, d_ref,
             du_ref, db_ref, dc_ref, da_ref, st_ref, lam_ref, acc_ref):
        j = pl.program_id(1)

        @pl.when(j == 0)
        def _():
            st_ref[...] = cl_ref[...]
            acc_ref[...] = jnp.zeros_like(acc_ref)
            db_ref[...] = jnp.zeros_like(db_ref)
            dc_ref[...] = jnp.zeros_like(dc_ref)

        ar, ai = ab_ref[:, :SLAB_COLS], ab_ref[:, SLAB_COLS:]
        lam_ref[...] = _dot(_bf(dy_ref[...]), c_ref[0], NT)

        def advance(lr, li, r0):
            nr = ar * lr + ai * li + lam_ref[pl.ds(r0, SUBLANES), :SLAB_COLS]
            ni = ar * li - ai * lr + lam_ref[pl.ds(r0, SUBLANES), SLAB_COLS:]
            lam_ref[pl.ds(r0, SUBLANES), :SLAB_COLS] = nr
            lam_ref[pl.ds(r0, SUBLANES), SLAB_COLS:] = ni
            return nr, ni

        def step(k, carry):
            lr, li, dr, di = carry
            t = n_steps - 1 - k
            nr, ni = advance(lr, li, pl.multiple_of(t * SUBLANES, SUBLANES))
            rx = pl.multiple_of((t - 1) * SUBLANES, SUBLANES)
            xr, xi = x_ref[pl.ds(rx, SUBLANES), :SLAB_COLS], x_ref[pl.ds(rx, SUBLANES), SLAB_COLS:]
            return nr, ni, dr + xr * nr + xi * ni, di + xr * ni - xi * nr

        lr, li, dr, di = lax.fori_loop(
            0, n_steps - 1, step,
            (st_ref[:, :SLAB_COLS], st_ref[:, SLAB_COLS:], acc_ref[:, :SLAB_COLS], acc_ref[:, SLAB_COLS:]), unroll=4)
        lr, li = advance(lr, li, 0)
        st_ref[:, :SLAB_COLS] = lr
        st_ref[:, SLAB_COLS:] = li
        first_chunk = j == nch - 1
        xr = jnp.where(first_chunk, cx_ref[:, :SLAB_COLS], xp_ref[:, :SLAB_COLS])
        xi = jnp.where(first_chunk, cx_ref[:, SLAB_COLS:], xp_ref[:, SLAB_COLS:])
        acc_ref[:, :SLAB_COLS] = dr + xr * lr + xi * li
        acc_ref[:, SLAB_COLS:] = di + xr * li - xi * lr

        lam_b = _bf(lam_ref[...])
        dyv = dy_ref[...]
        db_ref[0] += _dot(_bf(h_ref[...]), lam_b, TN)
        dc_ref[0] += _dot(_bf(dyv), _bf(x_ref[...]), TN)
        du_ref[...] = _dot(lam_b, b_ref[0], NT) + d_ref[...] * dyv

        @pl.when(j == nch - 1)
        def _():
            da_ref[...] = jnp.broadcast_to(jnp.sum(acc_ref[...], axis=0, keepdims=True), (SUBLANES, w))

    sub = rc // SUBLANES
    tile = lambda wd: pl.BlockSpec((rc, wd), lambda sl, j: (nch - 1 - j, sl))
    small = pl.BlockSpec((SUBLANES, w), lambda sl, j: (sl, 0))
    prev = pl.BlockSpec((SUBLANES, w), lambda sl, j: (jnp.maximum((nch - 1 - j) * sub - 1, 0), sl))
    return pl.pallas_call(
        body, name="s5_adjoint_carry_grads", grid=(nsl, nch),
        in_specs=[tile(LANES), pl.BlockSpec((1, w, LANES), lambda sl, j: (sl, 0, 0)), small, small, tile(w), prev, small,
                  tile(LANES), pl.BlockSpec((1, LANES, w), lambda sl, j: (sl, 0, 0)),
                  pl.BlockSpec((1, LANES), lambda sl, j: (0, sl))],
        out_specs=[tile(LANES), pl.BlockSpec((1, LANES, w), lambda sl, j: (sl, 0, 0)),
                   pl.BlockSpec((1, LANES, w), lambda sl, j: (sl, 0, 0)), small],
        out_shape=[jax.ShapeDtypeStruct((s, d), F32), jax.ShapeDtypeStruct((nsl, LANES, w), F32),
                   jax.ShapeDtypeStruct((nsl, LANES, w), F32), jax.ShapeDtypeStruct((nsl * SUBLANES, w), F32)],
        scratch_shapes=[pltpu.VMEM((SUBLANES, w), F32), pltpu.VMEM((rc, w), F32), pltpu.VMEM((SUBLANES, w), F32)],
        compiler_params=_params("parallel", "arbitrary"),
    )(dy, cblk, cinl, ab_tile, xtrue, xtrue, cinx, hp, bblk, dvec)


def _adamw(name, w, g, m, v):
    r, c = w.shape
    tile = r if r * c <= 512 * 1024 else _pick(r, max(SUBLANES, (512 * 1024 // c) // SUBLANES * SUBLANES), q=SUBLANES)
    c1 = 1.0 / (1.0 - ADAM_B1 ** ADAM_STEP)
    c2 = 1.0 / (1.0 - ADAM_B2 ** ADAM_STEP)

    def body(w_ref, g_ref, m_ref, v_ref, d_ref, nm_ref, nv_ref):
        gg = g_ref[...]
        nm = ADAM_B1 * m_ref[...] + (1.0 - ADAM_B1) * gg
        nv = ADAM_B2 * v_ref[...] + (1.0 - ADAM_B2) * gg * gg
        d_ref[...] = -ADAM_LR * ((nm * c1) / (jnp.sqrt(nv * c2) + ADAM_EPS) + ADAM_WD * w_ref[...])
        nm_ref[...] = nm
        nv_ref[...] = nv

    spec = _nat(tile, c)
    return pl.pallas_call(
        body, name=name, grid=(r // tile,), in_specs=[spec] * 4, out_specs=[spec] * 3,
        out_shape=[jax.ShapeDtypeStruct((r, c), F32)] * 3, compiler_params=_params("parallel"),
    )(w, g, m, v)


def _place():
    x, y, c = lax.axis_index("x"), lax.axis_index("y"), lax.axis_index("c")
    return x, y, c, [(1 - x, y), (x, 1 - y), (1 - x, 1 - y)]


_ANY = pl.BlockSpec(memory_space=pl.ANY)


def _gather_weights(shards):
    n = len(shards)

    def body(*refs):
        ins, outs = refs[:n], refs[n:2 * n]
        send_sems, recv_sems, local_sems = refs[2 * n:]
        x, y, c, chips = _place()
        me = 2 * x + y
        sibling = (x, y, 1 - c)
        started = []
        for a in range(n):
            local = pltpu.make_async_copy(ins[a], outs[a].at[me], local_sems.at[a])
            local.start()
            started.append(local)

        def half(a, chip, h):
            hw = ins[a].shape[1] // 2
            return outs[a].at[chip, :, pl.ds(pl.multiple_of(h * hw, LANES), hw)]

        def copy(a, k, src, chip, h, to):
            return pltpu.make_async_remote_copy(
                src_ref=src, dst_ref=half(a, chip, h), send_sem=send_sems.at[a, k], recv_sem=recv_sems.at[a, k],
                device_id=to, device_id_type=MESH)

        sends = []
        for a in range(n):
            hw = ins[a].shape[1] // 2
            mine = ins[a].at[:, pl.ds(pl.multiple_of(c * hw, LANES), hw)]
            for k, chip in enumerate(chips):
                cp = copy(a, k, mine, me, c, (*chip, c))
                cp.start()
                sends.append(cp)
        for a in range(n):
            for k, (cx, cy) in enumerate(chips):
                src_chip = 2 * cx + cy
                copy(a, k, half(a, src_chip, c), src_chip, c, (x, y, c)).wait_recv()
                fwd = copy(a, 3 + k, half(a, src_chip, c), src_chip, c, sibling)
                fwd.start()
                sends.append(fwd)
        for a in range(n):
            for k, (cx, cy) in enumerate(chips):
                src_chip = 2 * cx + cy
                copy(a, 3 + k, half(a, src_chip, 1 - c), src_chip, 1 - c, (x, y, c)).wait_recv()
        for cp in sends:
            cp.wait_send()
        for cp in started:
            cp.wait()

    return pl.pallas_call(
        body, name="gather_weights",
        in_specs=[_ANY] * n, out_specs=[_ANY] * n,
        out_shape=[jax.ShapeDtypeStruct((N_CHIPS,) + s_.shape, s_.dtype) for s_ in shards],
        scratch_shapes=[pltpu.SemaphoreType.DMA((n, 6)), pltpu.SemaphoreType.DMA((n, 6)), pltpu.SemaphoreType.DMA((n,))],

    )(*shards)


def _gather_weights_async(shards):
    n = len(shards)
    srcs = [jax.new_ref(s_, memory_space=pltpu.MemorySpace.HBM) for s_ in shards]
    outs = [jax.empty_ref(jax.ShapeDtypeStruct((N_CHIPS,) + s_.shape, s_.dtype), memory_space=pltpu.MemorySpace.HBM)
            for s_ in shards]

    @pl.kernel(mesh=plsc.ScalarSubcoreMesh(axis_name="seq", num_cores=1), name="gather_weights_async",
               scratch_types=(pltpu.SemaphoreType.DMA((n, 6)), pltpu.SemaphoreType.DMA((n, 6)),
                              pltpu.SemaphoreType.DMA((n,))),
               compiler_params=pltpu.CompilerParams(collective_id=1))
    def launch(send_sems, recv_sems, local_sems):
        x, y, c, chips = _place()
        me = 2 * x + y
        sibling = (x, y, 1 - c)
        barrier = pltpu.get_barrier_semaphore()
        for peer in [sibling] + [(*chip, c) for chip in chips]:
            pl.semaphore_signal(barrier, inc=1, device_id=peer, device_id_type=MESH)
        pl.semaphore_wait(barrier, 4)

        def half(a, chip, h):
            hw = srcs[a].shape[1] // 2
            return outs[a].at[chip, :, pl.ds(pl.multiple_of(h * hw, LANES), hw)]

        def copy(a, k, src, chip, h, to):
            return pltpu.make_async_remote_copy(
                src_ref=src, dst_ref=half(a, chip, h), send_sem=send_sems.at[a, k], recv_sem=recv_sems.at[a, k],
                device_id=to, device_id_type=MESH)

        locals_, sends = [], []
        for a in range(n):
            local = pltpu.make_async_copy(srcs[a], outs[a].at[me], local_sems.at[a])
            local.start()
            locals_.append(local)
            hw = srcs[a].shape[1] // 2
            mine = srcs[a].at[:, pl.ds(pl.multiple_of(c * hw, LANES), hw)]
            for k, chip in enumerate(chips):
                cp = copy(a, k, mine, me, c, (*chip, c))
                cp.start()
                sends.append(cp)
        for a in range(n):
            for k, (cx, cy) in enumerate(chips):
                src_chip = 2 * cx + cy
                copy(a, k, half(a, src_chip, c), src_chip, c, (x, y, c)).wait_recv()
                fwd = copy(a, 3 + k, half(a, src_chip, c), src_chip, c, sibling)
                fwd.start()
                sends.append(fwd)
        for a in range(n):
            for k, (cx, cy) in enumerate(chips):
                src_chip = 2 * cx + cy
                copy(a, 3 + k, half(a, src_chip, 1 - c), src_chip, 1 - c, (x, y, c)).wait_recv()
        for cp in sends:
            cp.wait_send()
        for cp in locals_:
            cp.wait()

    launch()
    return [o[...] for o in outs]


def _on_sequencer(name, cid, inputs, out_shapes, sem_types, peers, body):
    srcs = [jax.new_ref(a, memory_space=pltpu.MemorySpace.HBM) for a in inputs]
    outs = [jax.empty_ref(sd, memory_space=pltpu.MemorySpace.HBM) for sd in out_shapes]

    @pl.kernel(mesh=plsc.ScalarSubcoreMesh(axis_name="seq", num_cores=1), name=name, scratch_types=tuple(sem_types),
               compiler_params=pltpu.CompilerParams(collective_id=cid))
    def launch(*sems):
        x, y, c, chips = _place()
        barrier = pltpu.get_barrier_semaphore()
        ps = peers(x, y, c, chips)
        for peer in ps:
            pl.semaphore_signal(barrier, inc=1, device_id=peer, device_id_type=MESH)
        pl.semaphore_wait(barrier, len(ps))
        body(srcs, outs, *sems)

    launch()
    return [o[...] for o in outs]


def _sibling_only(x, y, c, chips):
    return [(x, y, 1 - c)]


def _same_core_of_other_chips(x, y, c, chips):
    return [(*chip, c) for chip in chips]


def _swap_halves_to_sibling(name, cid, grads):
    n = len(grads)

    def body(ins, outs, send_sems, recv_sems):
        x, y, c, _ = _place()
        cps = []
        for a in range(n):
            hw = ins[a].shape[2] // 2
            src = ins[a].at[:, :, pl.ds(pl.multiple_of((1 - c) * hw, LANES), hw)]
            cp = pltpu.make_async_remote_copy(src_ref=src, dst_ref=outs[a], send_sem=send_sems.at[a],
                                              recv_sem=recv_sems.at[a], device_id=(x, y, 1 - c), device_id_type=MESH)
            cp.start()
            cps.append(cp)
        for cp in cps:
            cp.wait()

    return _on_sequencer(
        name, cid, grads, [jax.ShapeDtypeStruct(g.shape[:2] + (g.shape[2] // 2,), g.dtype) for g in grads],
        [pltpu.SemaphoreType.DMA((n,)), pltpu.SemaphoreType.DMA((n,))], _sibling_only, body)


def _exchange_quarters(name, cid, parts):
    n = len(parts)

    def body(ins, outs, send_sems, recv_sems):
        x, y, c, chips = _place()
        cps = []
        for a in range(n):
            for k, (cx, cy) in enumerate(chips):
                cp = pltpu.make_async_remote_copy(
                    src_ref=ins[a].at[2 * cx + cy], dst_ref=outs[a].at[k], send_sem=send_sems.at[a, k],
                    recv_sem=recv_sems.at[a, k], device_id=(cx, cy, c), device_id_type=MESH)
                cp.start()
                cps.append(cp)
        for cp in cps:
            cp.wait()

    return _on_sequencer(
        name, cid, parts, [jax.ShapeDtypeStruct((3,) + p_.shape[1:], p_.dtype) for p_ in parts],
        [pltpu.SemaphoreType.DMA((n, 3)), pltpu.SemaphoreType.DMA((n, 3))], _same_core_of_other_chips, body)


def _swap_final_halves(name, cid, halves):
    n = len(halves)

    def body(ins, outs, send_sems, recv_sems):
        x, y, c, _ = _place()
        cps = []
        for a in range(n):
            cp = pltpu.make_async_remote_copy(src_ref=ins[a], dst_ref=outs[a], send_sem=send_sems.at[a],
                                              recv_sem=recv_sems.at[a], device_id=(x, y, 1 - c), device_id_type=MESH)
            cp.start()
            cps.append(cp)
        for cp in cps:
            cp.wait()

    return _on_sequencer(
        name, cid, halves, [jax.ShapeDtypeStruct(h.shape, h.dtype) for h in halves],
        [pltpu.SemaphoreType.DMA((n,)), pltpu.SemaphoreType.DMA((n,))], _sibling_only, body)


def _add_half(name, grad, recv):
    nchip, r, cfull = grad.shape
    hw = cfull // 2
    tile = _pick(r, max(BF16_ROWS, (256 * 1024 // hw) // BF16_ROWS * BF16_ROWS), q=BF16_ROWS)
    c = lax.axis_index("c")

    def body(c_ref, g_ref, r_ref, o_ref):
        o_ref[...] = _bf(g_ref[...] + r_ref[...])

    return pl.pallas_call(
        body, name=name,
        grid_spec=pltpu.PrefetchScalarGridSpec(
            num_scalar_prefetch=1, grid=(nchip, r // tile),
            in_specs=[pl.BlockSpec((1, tile, hw), lambda k, i, cr: (k, i, cr[0])),
                      pl.BlockSpec((1, tile, hw), lambda k, i, cr: (k, i, 0))],
            out_specs=pl.BlockSpec((1, tile, hw), lambda k, i, cr: (k, i, 0))),
        out_shape=jax.ShapeDtypeStruct((nchip, r, hw), BF16), compiler_params=_params("parallel", "parallel"),
    )(c.reshape(1).astype(jnp.int32), grad, recv)


def _add_quarters(name, part, recv):
    _, r, hw = part.shape
    tile = _pick(r, max(BF16_ROWS, (256 * 1024 // hw) // BF16_ROWS * BF16_ROWS), q=BF16_ROWS)
    me = 2 * lax.axis_index("x") + lax.axis_index("y")

    def body(me_ref, p_ref, r_ref, o_ref):
        f = lambda v: v.astype(F32)
        o_ref[...] = ((f(p_ref[0]) + f(r_ref[0])) + f(r_ref[1])) + f(r_ref[2])

    return pl.pallas_call(
        body, name=name,
        grid_spec=pltpu.PrefetchScalarGridSpec(
            num_scalar_prefetch=1, grid=(r // tile,),
            in_specs=[pl.BlockSpec((1, tile, hw), lambda i, mr: (mr[0], i, 0)),
                      pl.BlockSpec((3, tile, hw), lambda i, mr: (0, i, 0))],
            out_specs=pl.BlockSpec((tile, hw), lambda i, mr: (i, 0))),
        out_shape=jax.ShapeDtypeStruct((r, hw), F32), compiler_params=_params("parallel"),
    )(me.reshape(1).astype(jnp.int32), part, recv)


class _ReduceScatter:
    def __init__(self, tag, first_cid, grads):
        self.tag, self.cid = tag, first_cid
        self.stacks = [g.reshape(N_CHIPS, g.shape[0] // N_CHIPS, g.shape[1]) for g in grads]

    def start(self, anchor):
        self.stacks, anchor = lax.optimization_barrier((self.stacks, anchor))
        self.recv = _swap_halves_to_sibling(f"rs_swap_halves_{self.tag}", self.cid, self.stacks)
        return anchor

    def exchange(self, anchor):
        parts = [_add_half(f"rs_add_half_{self.tag}{a}", g, r) for a, (g, r) in enumerate(zip(self.stacks, self.recv))]
        self.parts, anchor = lax.optimization_barrier((parts, anchor))
        self.quarters = _exchange_quarters(f"rs_exchange_{self.tag}", self.cid + 1, self.parts)
        return anchor

    def join(self, anchor):
        halves = [_add_quarters(f"rs_add_quarters_{self.tag}{a}", p_, q_)
                  for a, (p_, q_) in enumerate(zip(self.parts, self.quarters))]
        self.halves, anchor = lax.optimization_barrier((halves, anchor))
        self.others = _swap_final_halves(f"rs_swap_final_{self.tag}", self.cid + 2, self.halves)
        return anchor

    def result(self):
        south = lax.axis_index("c") == 0
        return [jnp.concatenate([jnp.where(south, h, o), jnp.where(south, o, h)], axis=1)
                for h, o in zip(self.halves, self.others)]


def _allgather_small(pack):
    m_per, n = pack.shape

    def body(ins, outs, send_sems, recv_sems, local_sem):
        x_ref, out_ref = ins[0], outs[0]
        x, y, c, chips = _place()
        me, sibling = (x, y, c), (x, y, 1 - c)

        def rows(px, py, pc):
            return out_ref.at[pl.ds(pl.multiple_of((4 * px + 2 * py + pc) * m_per, SUBLANES), m_per), :]

        def copy(k, block, to, src=None):
            return pltpu.make_async_remote_copy(
                src_ref=rows(*block) if src is None else src, dst_ref=rows(*block),
                send_sem=send_sems.at[k], recv_sem=recv_sems.at[k], device_id=to, device_id_type=MESH)

        mine = pltpu.make_async_copy(x_ref, rows(*me), local_sem)
        mine.start()
        first = [copy(0, me, sibling, src=x_ref)]
        first += [copy(1 + j, me, (*chip, c), src=x_ref) for j, chip in enumerate(chips)]
        for cp in first:
            cp.start()
        passed = [copy(4 + j, (*chip, c), sibling) for j, chip in enumerate(chips)]
        for j, chip in enumerate(chips):
            copy(1 + j, (*chip, c), me).wait_recv()
            passed[j].start()
        copy(0, sibling, me).wait_recv()
        for j, chip in enumerate(chips):
            copy(4 + j, (*chip, 1 - c), me).wait_recv()
        for cp in first + passed:
            cp.wait_send()
        mine.wait()

    def peers(x, y, c, chips):
        return [(x, y, 1 - c)] + [(*chip, c) for chip in chips]

    return _on_sequencer(
        "allgather_small_grads", 11, [pack], [jax.ShapeDtypeStruct((N_DEV * m_per, n), pack.dtype)],
        [pltpu.SemaphoreType.DMA((7,)), pltpu.SemaphoreType.DMA((7,)), pltpu.SemaphoreType.DMA], peers, body)[0]


def _sum_devices(packs, m_per):
    tile = _pick(m_per, 512, q=SUBLANES)
    nt = m_per // tile

    def body(*refs):
        acc = refs[0][...]
        for r in refs[1:N_DEV]:
            acc = acc + r[...]
        refs[N_DEV][...] = acc

    return pl.pallas_call(
        body, name="sum_small_grads", grid=(nt,),
        in_specs=[pl.BlockSpec((tile, LANES), functools.partial(lambda i, k: (k * nt + i, 0), k=k)) for k in range(N_DEV)],
        out_specs=_nat(tile, LANES), out_shape=jax.ShapeDtypeStruct((m_per, LANES), F32),
        compiler_params=_params("parallel"),
    )(*([packs] * N_DEV))


def _tail_fwd(tag, alpha, h_in, adds, mix_gate, ln1, ln2, p_l, w, want_perm):
    h_mid, xh1, rs1, h_mid_b, _ = _ln_fwd(f"ln1_fwd_{tag}", alpha, h_in, adds, mix_gate, *ln1)
    gp = _matmul(f"ple_gate_fwd_{tag}", h_mid_b, w['wg'], 'nn')
    pw = _matmul(f"ple_proj_fwd_{tag}", p_l, w['plet'], 'nt')
    gg, uu, act = _ffn_in_swiglu(f"ffn_in_fwd_{tag}", h_mid_b, w['wit'])
    ffn = _matmul(f"ffn_out_fwd_{tag}", act, w['wo'], 'nn', tk=2816)
    h_out, xh2, rs2, _, h_perm = _ln_fwd(f"ln2_fwd_{tag}", alpha, h_mid, [(ffn, 'nat')],
                                         ('nat', (pw, 1, 0), (gp, 1, 0)), *ln2, want_perm=want_perm)
    saved = dict(h_mid_b=h_mid_b, xh1=xh1, rs1=rs1, gp=gp, pw=pw, g=gg, u=uu, act=act, xh2=xh2, rs2=rs2)
    return h_out, h_perm, saved


def _tail_bwd(tag, alpha, dparts, sv, ln1_g, ln2_g, p_l, w, mix_gate):
    d = sv['h_mid_b'].shape[1]
    dz2, dz2b, dgate, dg2, db2 = _ln_bwd(f"ln2_bwd_{tag}", dparts, sv['xh2'], sv['rs2'], ln2_g,
                                         gate=('nat', (sv['pw'], 1, 0), (sv['gp'], 1, 0)))
    grads = dict(ln2_g=dg2, ln2_b=db2)
    grads['plet'] = _matmul(f"ple_proj_dw_{tag}", dgate, p_l, 'tn', a_win=(0, d))
    grads['wg'] = _matmul(f"ple_gate_dw_{tag}", sv['h_mid_b'], dgate, 'tn', b_win=(d, d))
    dx_gate = _matmul(f"ple_gate_dx_{tag}", dgate, w['wg'], 'nt', a_win=(d, d))
    dact = _matmul(f"ffn_out_dx_{tag}", dz2b, w['wo'], 'nt', out_dtype=BF16, tn=1408)
    grads['wo'] = _matmul(f"ffn_out_dw_{tag}", sv['act'], dz2b, 'tn', tm=1408)
    dgu = _swiglu_bwd(f"swiglu_bwd_{tag}", sv['g'], sv['u'], dact)
    grads['wit'] = _matmul(f"ffn_in_dw_{tag}", dgu, sv['h_mid_b'], 'tn')
    dx_ffn = _matmul(f"ffn_in_dx_{tag}", dgu, w['wit'], 'nn', tk=2816)
    res = _ln_bwd(f"ln1_bwd_{tag}", [(dz2, 'nat', alpha), (dx_gate, 'nat', 1.0), (dx_ffn, 'nat', 1.0)],
                  sv['xh1'], sv['rs1'], ln1_g, gate=mix_gate)
    grads['ln1_g'], grads['ln1_b'] = res[-2], res[-1]
    return res[:-2], grads


def kernel(x, p, positions, attn_w_in, mla_q_norm, mla_w_q_b, mla_kv_norm, mla_w_kv_b, attn_w_out, s5_a_re, s5_a_im, s5_log_dt, s5_b_re, s5_b_im, s5_c_re, s5_c_im, s5_d, s5_w_glu, ln1_g, ln1_b, ffn_w_in, ffn_w_out, ple_w, ple_gate_w, ln2_g, ln2_b, loss_target, m_attn_w_in, m_mla_q_norm, m_mla_w_q_b, m_mla_kv_norm, m_mla_w_kv_b, m_attn_w_out, m_s5_a_re, m_s5_a_im, m_s5_log_dt, m_s5_b_re, m_s5_b_im, m_s5_c_re, m_s5_c_im, m_s5_d, m_s5_w_glu, m_ln1_g, m_ln1_b, m_ffn_w_in, m_ffn_w_out, m_ple_w, m_ple_gate_w, m_ln2_g, m_ln2_b, v_attn_w_in, v_mla_q_norm, v_mla_w_q_b, v_mla_kv_norm, v_mla_w_kv_b, v_attn_w_out, v_s5_a_re, v_s5_a_im, v_s5_log_dt, v_s5_b_re, v_s5_b_im, v_s5_c_re, v_s5_c_im, v_s5_d, v_s5_w_glu, v_ln1_g, v_ln1_b, v_ffn_w_in, v_ffn_w_out, v_ple_w, v_ple_gate_w, v_ln2_g, v_ln2_b):
    weights = dict(attn_w_in=attn_w_in, mla_q_norm=mla_q_norm, mla_w_q_b=mla_w_q_b, mla_kv_norm=mla_kv_norm,
                   mla_w_kv_b=mla_w_kv_b, attn_w_out=attn_w_out, s5_a_re=s5_a_re, s5_a_im=s5_a_im, s5_log_dt=s5_log_dt,
                   s5_b_re=s5_b_re, s5_b_im=s5_b_im, s5_c_re=s5_c_re, s5_c_im=s5_c_im, s5_d=s5_d, s5_w_glu=s5_w_glu,
                   ln1_g=ln1_g, ln1_b=ln1_b, ffn_w_in=ffn_w_in, ffn_w_out=ffn_w_out, ple_w=ple_w, ple_gate_w=ple_gate_w,
                   ln2_g=ln2_g, ln2_b=ln2_b)
    m_in = dict(attn_w_in=m_attn_w_in, mla_q_norm=m_mla_q_norm, mla_w_q_b=m_mla_w_q_b, mla_kv_norm=m_mla_kv_norm,
                mla_w_kv_b=m_mla_w_kv_b, attn_w_out=m_attn_w_out, s5_a_re=m_s5_a_re, s5_a_im=m_s5_a_im,
                s5_log_dt=m_s5_log_dt, s5_b_re=m_s5_b_re, s5_b_im=m_s5_b_im, s5_c_re=m_s5_c_re, s5_c_im=m_s5_c_im,
                s5_d=m_s5_d, s5_w_glu=m_s5_w_glu, ln1_g=m_ln1_g, ln1_b=m_ln1_b, ffn_w_in=m_ffn_w_in,
                ffn_w_out=m_ffn_w_out, ple_w=m_ple_w, ple_gate_w=m_ple_gate_w, ln2_g=m_ln2_g, ln2_b=m_ln2_b)
    v_in = dict(attn_w_in=v_attn_w_in, mla_q_norm=v_mla_q_norm, mla_w_q_b=v_mla_w_q_b, mla_kv_norm=v_mla_kv_norm,
                mla_w_kv_b=v_mla_w_kv_b, attn_w_out=v_attn_w_out, s5_a_re=v_s5_a_re, s5_a_im=v_s5_a_im,
                s5_log_dt=v_s5_log_dt, s5_b_re=v_s5_b_re, s5_b_im=v_s5_b_im, s5_c_re=v_s5_c_re, s5_c_im=v_s5_c_im,
                s5_d=v_s5_d, s5_w_glu=v_s5_w_glu, ln1_g=v_ln1_g, ln1_b=v_ln1_b, ffn_w_in=v_ffn_w_in,
                ffn_w_out=v_ffn_w_out, ple_w=v_ple_w, ple_gate_w=v_ple_gate_w, ln2_g=v_ln2_g, ln2_b=v_ln2_b)
    names = list(weights)

    s, d = x.shape[1], x.shape[2]
    depth = ln1_g.shape[0]
    assert depth == 2
    alpha = (2.0 * depth) ** 0.25
    ql, kvl = mla_q_norm.shape[1], mla_kv_norm.shape[1]
    in_cols = N_CHIPS * attn_w_in.shape[2]
    heads = N_CHIPS * mla_w_q_b.shape[2] // (NOPE + ROPE)
    hps = heads // N_CHIPS
    dw = (in_cols - ql - kvl - ROPE) // 3
    dh = dw // DHD
    assert ql % LANES == 0 and kvl == ql and dw % DHD == 0 and heads % N_CHIPS == 0
    ngroups, nstate = s5_a_re.shape[1], s5_a_re.shape[2]
    assert nstate == S5_STATE and ngroups * S5_GROUP == d and d % LANES == 0
    nsl = d // LANES
    seg_len = s // SUBLANES
    n_sq = seg_len.bit_length() - 1
    assert 1 << n_sq == seg_len, "the segment length of the S5 scan must be a power of two"
    for window, dil in DIL_BRANCHES:
        assert window // dil == DIL_STEPS and (s // dil) % DIL_STEPS == 0
    me = 2 * lax.axis_index("x") + lax.axis_index("y")

    xb = x[0]
    target = loss_target[0]
    p_layers = [p[0, 0], p[1, 0]]
    pos = positions[0].astype(F32).reshape(s, 1)
    inv_freq = ROPE_THETA ** (-jnp.arange(ROPE // 2, dtype=F32) / (ROPE // 2))
    invf = jnp.concatenate([inv_freq, inv_freq, jnp.zeros((LANES - ROPE,), F32)]).reshape(1, LANES)
    slopes = 2.0 ** (-8.0 * jnp.arange(1, dh + 1, dtype=F32) / dh)
    slopes = jnp.broadcast_to(jnp.repeat(slopes, SUBLANES)[:, None], (dh * SUBLANES, LANES))

    wqb_t = mla_w_q_b[0].T.reshape(hps, NOPE + ROPE, ql)
    wqb_t = jnp.pad(wqb_t, ((0, 0), (0, QK_PAD - NOPE - ROPE), (0, 0))).reshape(hps * QK_PAD, ql)
    d_cols = max(d // N_CHIPS, 2 * LANES)
    d_pad = jnp.zeros((SUBLANES, d_cols), F32).at[0, :d // N_CHIPS].set(s5_d[0])
    shards = [_bf(attn_w_in[0].T), _bf(wqb_t), _bf(mla_w_kv_b[0].T), _bf(attn_w_out[0]), _bf(s5_w_glu[0].T)]
    for l in range(depth):
        shards += [_bf(ffn_w_in[l].T), _bf(ffn_w_out[l]), _bf(ple_w[l].T), _bf(ple_gate_w[l])]
    shards.append(d_pad)
    first, later = lax.optimization_barrier((list(_gather_weights(shards[:3])), shards[3:]))
    gathered = first + _gather_weights_async(later)
    full = [g.reshape(N_CHIPS * g.shape[1], g.shape[2]) for g in gathered]
    win_t, wqb_t_f, wkv_t, wout, wglu_t = full[:5]
    lw = [dict(wit=full[5 + 4 * l], wo=full[6 + 4 * l], plet=full[7 + 4 * l], wg=full[8 + 4 * l]) for l in range(depth)]
    dvec = full[-1].reshape(N_CHIPS, SUBLANES, d_cols)[:, 0, :d // N_CHIPS].reshape(1, d)
    lat = ql + kvl
    win_t = jnp.concatenate([win_t[:lat + ROPE], jnp.zeros((LANES - ROPE, d), BF16), win_t[lat + ROPE:]], axis=0)
    kpe_cb = lat // LANES
    q_cb = kpe_cb + 1
    a_cb = heads * VDIM // LANES

    xbb = _bf(xb)
    proj = _matmul("attn_in_fwd", xbb, win_t, 'nt', tn=1408)
    nrm = _rms_fwd(proj, ql, kvl, mla_q_norm[0], mla_kv_norm[0])
    q_raw = _matmul("mla_q_up_fwd", nrm, wqb_t_f, 'nt', a_win=(0, ql))
    kv = _matmul("mla_kv_up_fwd", nrm, wkv_t, 'nt', a_win=(ql, kvl))
    qf, kf, vv = _rope_prep(q_raw, kv, proj, kpe_cb, pos, invf, heads)
    out_a, lse_a = _mla_fwd(qf, kf, vv.T, heads)
    out_b, lse_b = _dil_fused_fwd(proj, slopes, dh, q_cb)
    att = _concat_bf16("attn_heads_concat", out_a, out_b)
    mix0 = _matmul("attn_out_fwd", att, wout, 'nn')
    h2, h2p, sv0 = _tail_fwd("l0", alpha, xb, [(mix0, 'nat')], None, (ln1_g[0], ln1_b[0]), (ln2_g[0], ln2_b[0]),
                             p_layers[0], lw[0], want_perm=True)

    rep = lambda a: jnp.repeat(a, S5_GROUP, axis=0)
    ag = (s5_a_re[0], s5_a_im[0], jnp.broadcast_to(s5_log_dt[0][:, None], (ngroups, nstate)))
    a16 = tuple(rep(a) for a in ag)
    b16 = tuple(b[0].transpose(0, 2, 1).reshape(ngroups * S5_GROUP, nstate) for b in (s5_b_re, s5_b_im))
    abr, abi, apr, api, bbr, bbi = _s5_discretise(*a16, *b16, n_sq)
    ab_tile = _slab_tile(abr[::S5_GROUP], abi[::S5_GROUP], nsl)
    ap_tile = _slab_tile(apr[::S5_GROUP], api[::S5_GROUP], nsl)
    bblk = _bf(_slab_in_matrix(bbr.reshape(ngroups, S5_GROUP, nstate), bbi.reshape(ngroups, S5_GROUP, nstate), nsl))
    cblk = _bf(_slab_out_matrix(s5_c_re[0], s5_c_im[0], nsl))
    ends = _s5_pass1(h2p, bblk, ab_tile)
    cinx = _s5_carry("s5_carry_fwd", ends, ap_tile, False)
    xtrue, ypre, zg = _s5_pass2(h2p, bblk, cinx, ab_tile, cblk, dvec)
    vg = _matmul("s5_glu_fwd", zg, wglu_t, 'nt')
    glu_gate = ('perm', (vg, 2, 0), (vg, 2, 1))
    h4, _, sv1 = _tail_fwd("l1", alpha, h2, [], glu_gate, (ln1_g[1], ln1_b[1]), (ln2_g[1], ln2_b[1]),
                           p_layers[1], lw[1], want_perm=False)
    loss = lax.psum(jnp.sum(_loss_partial(h4, target)), ("x", "y", "c"))

    (dz1_1, _, dvg), g1 = _tail_bwd("l1", alpha, [(h4, 'nat', 1.0 / d), (target, 'nat', -1.0 / d)], sv1, ln1_g[1],
                                    ln2_g[1], p_layers[1], lw[1], glu_gate)
    d_wglu_t = _matmul("s5_glu_dw", dvg, zg, 'tn')
    dzg = _matmul("s5_glu_dx", dvg, wglu_t, 'nn')
    rs_l1 = _ReduceScatter("l1", 2, [d_wglu_t, g1['wit'], g1['wo'], g1['plet'], g1['wg']])
    dzg = rs_l1.start(dzg)
    starts, dy, dd = _s5_bwd_pass1(dzg, ypre, cblk, ab_tile, h2p)
    cinl = _s5_carry("s5_carry_bwd", starts, ap_tile, True)
    du_p, d_bblk, d_cblk, d_ab = _s5_bwd_pass2(dy, cblk, cinl, ab_tile, xtrue, cinx, h2p, bblk, dvec)
    gbb = _slab_in_extract(d_bblk, nsl)
    g_c_re, g_c_im = _slab_out_extract(jnp.swapaxes(d_cblk, 1, 2), nsl)
    d_ab = d_ab[::SUBLANES]
    gab = (d_ab[:, :SLAB_COLS].reshape(ngroups, nstate), d_ab[:, SLAB_COLS:].reshape(ngroups, nstate))
    g_a_re, g_a_im, g_log_dt, g_b_re, g_b_im = _s5_discretise_bwd(a16, b16, ag, gab, gbb)
    unt = lambda b: b.reshape(ngroups, S5_GROUP, nstate).transpose(0, 2, 1)

    du_p = rs_l1.exchange(du_p)
    (dz1_0, dz1_0b), g0 = _tail_bwd("l0", alpha, [(dz1_1, 'nat', alpha), (du_p, 'perm', 1.0)], sv0, ln1_g[0], ln2_g[0],
                                    p_layers[0], lw[0], None)
    dz1_0b = rs_l1.join(dz1_0b)
    d_wout = _matmul("attn_out_dw", att, dz1_0b, 'tn')
    rs_l0 = _ReduceScatter("l0", 5, [g0['wit'], g0['wo'], g0['plet'], g0['wg'], d_wout])
    dz1_0b = rs_l0.start(dz1_0b)
    datt = _matmul("attn_out_dx", dz1_0b, wout, 'nt')
    do, delta, delta_t = _attn_bwd_prep(datt, out_a, out_b)
    dqf, dkf, dvv = _mla_bwd(qf, kf, vv, do, lse_a, delta_t, heads, 0)
    dqf = rs_l0.exchange(dqf)
    dq_raw, dkv, dkpe = _rope_unprep(dqf, dkf, dvv, pos, invf, heads)
    d_wqb_t = _matmul("mla_q_up_dw", dq_raw, nrm, 'tn', b_win=(0, ql))
    d_wkv_t = _matmul("mla_kv_up_dw", dkv, nrm, 'tn', b_win=(ql, kvl))
    dnq = _matmul("mla_q_up_dx", dq_raw, wqb_t_f, 'nn')
    dnkv = _matmul("mla_kv_up_dx", dkv, wkv_t, 'nn')
    dqd, dkd, dvd = _dil_fused_bwd(proj, slopes, datt, lse_b, delta, dh, q_cb, a_cb)
    dkpe = rs_l0.join(dkpe)
    dproj, g_gq, g_gkv = _dproj_assemble(proj, dnq, dnkv, dkpe, [dqd], [dkd], [dvd], mla_q_norm[0], mla_kv_norm[0], ql)

    small = dict(mla_q_norm=g_gq, mla_kv_norm=g_gkv, s5_a_re=g_a_re, s5_a_im=g_a_im, s5_log_dt=g_log_dt,
                 s5_b_re=unt(g_b_re), s5_b_im=unt(g_b_im), s5_c_re=g_c_re, s5_c_im=g_c_im, s5_d=dd[0],
                 ln1_g=jnp.stack([g0['ln1_g'], g1['ln1_g']]), ln1_b=jnp.stack([g0['ln1_b'], g1['ln1_b']]),
                 ln2_g=jnp.stack([g0['ln2_g'], g1['ln2_g']]), ln2_b=jnp.stack([g0['ln2_b'], g1['ln2_b']]))
    flat = jnp.concatenate([v_.reshape(-1) for v_ in small.values()])
    m_per = -(-flat.shape[0] // (LANES * SUBLANES)) * SUBLANES
    pack, dproj = lax.optimization_barrier((jnp.pad(flat, (0, m_per * LANES - flat.shape[0])).reshape(m_per, LANES), dproj))
    small_packs = _allgather_small(pack)
    d_win_t = _matmul("attn_in_dw", dproj, xbb, 'tn', tm=1408)

    d_win_t = jnp.concatenate([d_win_t[:lat + ROPE], d_win_t[lat + LANES:]], axis=0)
    rs_at = _ReduceScatter("attn", 8, [d_win_t, d_wqb_t, d_wkv_t])
    r_wglu, r_wit1, r_wo1, r_plet1, r_wg1 = rs_l1.result()
    r_wit0, r_wo0, r_plet0, r_wg0, r_wout = rs_l0.result()
    dproj = rs_at.start(dproj)
    dx_attn = _matmul("attn_in_dx", dproj, win_t, 'nn')
    dx_attn, (r_wit0, r_wit1, r_wo0, r_wo1) = rs_at.exchange((dx_attn, (r_wit0, r_wit1, r_wo0, r_wo1)))
    grad_x = _axpy("grad_x", alpha, dz1_0, dx_attn)
    grad_x = rs_at.join(grad_x)
    r_win, r_wqb, r_wkv = rs_at.result()
    r_wqb = r_wqb.reshape(hps, QK_PAD, ql)[:, :NOPE + ROPE].reshape(hps * (NOPE + ROPE), ql)
    grads = dict(attn_w_in=r_win.T[None], mla_w_q_b=r_wqb.T[None], mla_w_kv_b=r_wkv.T[None], attn_w_out=r_wout[None],
                 s5_w_glu=r_wglu.T[None],
                 ffn_w_in=jnp.stack([r_wit0.T, r_wit1.T]), ffn_w_out=jnp.stack([r_wo0, r_wo1]),
                 ple_w=jnp.stack([r_plet0.T, r_plet1.T]), ple_gate_w=jnp.stack([r_wg0, r_wg1]))

    total = _sum_devices(small_packs, m_per).reshape(-1)
    off = 0
    for k_, v_ in small.items():
        n_ = v_.size
        piece = total[off:off + n_]
        off += n_
        if k_ == 's5_d':
            grads[k_] = lax.dynamic_slice(piece, (me * (d // N_CHIPS),), (d // N_CHIPS,)).reshape(weights[k_].shape)
        else:
            grads[k_] = piece.reshape(weights[k_].shape)

    deltas, new_m, new_v = {}, {}, {}
    for k_ in names:
        w_ = weights[k_]
        shape = w_.shape
        if w_.ndim == 3 and w_.shape[-1] >= LANES:
            two_d = (shape[0] * shape[1], shape[2])
        elif w_.ndim == 4:
            two_d = (shape[0] * shape[1], shape[2] * shape[3])
        else:
            two_d = (1, w_.size) if w_.ndim == 2 and shape[0] == 1 else (shape[0], w_.size // shape[0])
        dl, nm, nv = _adamw(f"adamw_{k_}", w_.reshape(two_d), grads[k_].reshape(two_d), m_in[k_].reshape(two_d),
                            v_in[k_].reshape(two_d))
        deltas[k_], new_m[k_], new_v[k_] = dl.reshape(shape), nm.reshape(shape), nv.reshape(shape)

    return (loss, grad_x[None], *[grads[k_] for k_ in names], *[deltas[k_] for k_ in names],
            *[new_m[k_] for k_ in names], *[new_v[k_] for k_ in names])
```

```python
import functools
import math

import jax
import jax.numpy as jnp
from jax import lax
from jax.experimental import pallas as pl
from jax.experimental.pallas import tpu as pltpu
from jax.experimental.pallas import tpu_sc as plsc

F32 = jnp.float32
BF16 = jnp.bfloat16
MESH = pl.DeviceIdType.MESH

LANES = 128
SUBLANES = 8
BF16_ROWS = 16
VMEM_LIMIT = 48 * 2 ** 20
N_CHIPS = 4
N_DEV = 8

NOPE = 128
ROPE = 64
VDIM = 128
QK_PAD = 256
DHD = 128
DIL_STEPS = 128
DIL_BRANCHES = ((128, 1), (512, 4), (2048, 16))
ROPE_THETA = 10000.0
S5_GROUP = 16
S5_STATE = 64
SLAB_GROUPS = LANES // S5_GROUP
SLAB_COLS = SLAB_GROUPS * S5_STATE
NEG = -1e30
LN_EPS = 1e-5
RMS_EPS = 1e-6

ADAM_LR = 0.001
ADAM_B1 = 0.9
ADAM_B2 = 0.999
ADAM_EPS = 1e-08
ADAM_WD = 0.01
ADAM_STEP = 10

NN = ((1,), (0,))
NT = ((1,), (1,))
TN = ((0,), (0,))


def _dot(a, b, dims):
    return lax.dot_general(a, b, (dims, ((), ())), preferred_element_type=F32)


def _bf(v):
    return v.astype(BF16)


def _pick(n, target, q=LANES, also=0):
    g = math.gcd(n, also) if also else n
    if g <= target and g == n:
        return n
    best = None
    for t in range(q, min(g, target) + 1, q):
        if g % t == 0:
            best = t
    assert best is not None, (n, target, q, also)
    return best


def _params(*sem):
    return pltpu.CompilerParams(dimension_semantics=sem, vmem_limit_bytes=VMEM_LIMIT)


def _sigmoid(v):
    return 1.0 / (1.0 + jnp.exp(-v))


def _matmul(name, a, b, form, out_dtype=F32, a_win=None, b_win=None, tm=1024, tn=1024, tk=2048):
    c0, aw = a_win if a_win else (0, a.shape[1])
    if form == 'nt':
        assert b_win is None
        n, kdim = b.shape
        d0 = 0
    else:
        kdim = b.shape[0]
        d0, n = b_win if b_win else (0, b.shape[1])
    if form == 'tn':
        m = aw
        assert a.shape[0] == kdim, (name, a.shape, b.shape)
        tm = _pick(m, tm, also=c0)
        tk = _pick(kdim, tk)
        a_off = c0 // tm
    else:
        m = a.shape[0]
        assert aw == kdim, (name, a.shape, b.shape, a_win)
        tm = _pick(m, tm)
        tk = _pick(kdim, tk, also=c0)
        a_off = c0 // tk
    tn = _pick(n, tn, also=d0)
    b_off = d0 // tn
    nk = kdim // tk
    dims = {'nn': NN, 'nt': NT, 'tn': TN}[form]

    def body(a_ref, b_ref, o_ref, *acc):
        prod = _dot(_bf(a_ref[...]), _bf(b_ref[...]), dims)
        if nk == 1:
            o_ref[...] = prod.astype(o_ref.dtype)
            return
        acc_ref, = acc
        k = pl.program_id(2)

        @pl.when(k == 0)
        def _():
            acc_ref[...] = prod

        @pl.when((k > 0) & (k < nk - 1))
        def _():
            acc_ref[...] += prod

        @pl.when(k == nk - 1)
        def _():
            o_ref[...] = (acc_ref[...] + prod).astype(o_ref.dtype)

    if form == 'tn':
        a_spec = pl.BlockSpec((tk, tm), lambda i, j, k: (k, i + a_off))
    else:
        a_spec = pl.BlockSpec((tm, tk), lambda i, j, k: (i, k + a_off))
    if form == 'nt':
        b_spec = pl.BlockSpec((tn, tk), lambda i, j, k: (j, k))
    else:
        b_spec = pl.BlockSpec((tk, tn), lambda i, j, k: (k, j + b_off))
    return pl.pallas_call(
        body, name=name,
        grid=(m // tm, n // tn, nk),
        in_specs=[a_spec, b_spec],
        out_specs=pl.BlockSpec((tm, tn), lambda i, j, k: (i, j)),
        out_shape=jax.ShapeDtypeStruct((m, n), out_dtype),
        scratch_shapes=[pltpu.VMEM((tm, tn), F32)] if nk > 1 else [],
        compiler_params=_params("parallel", "parallel", "arbitrary"),
    )(a, b)


def _nat(tile, width, cb=0):
    return pl.BlockSpec((tile, width), lambda i: (i, cb))


def _perm(tile, width, seg_tiles, ncb=1, cb=0):
    return pl.BlockSpec((tile, width), lambda i: (i % seg_tiles, (i // seg_tiles) * ncb + cb))


def _whole(shape):
    return pl.BlockSpec(shape, lambda i: (0,) * len(shape))


def _perm_view(a):
    s, w = a.shape
    return a.reshape(s // SUBLANES, SUBLANES * w)


def _row_spec(a, layout, tile, width, ncb=1, cb=0):
    if layout == 'nat':
        return a, _nat(tile, width, cb)
    seg_tiles = a.shape[0] // SUBLANES // tile
    return _perm_view(a), _perm(tile, width, seg_tiles, ncb, cb)


def _ln_fwd(name, alpha, a, adds, gate, g, b, want_perm=False, tile=256):
    s, d = a.shape
    n_add = len(adds)
    has_gate = gate is not None

    def body(*refs):
        a_ref = refs[0]
        add_refs = refs[1:1 + n_add]
        pos = 1 + n_add
        if has_gate:
            val_ref, pre_ref = refs[pos], refs[pos + 1]
            pos += 2
        g_ref, b_ref = refs[pos], refs[pos + 1]
        outs = refs[pos + 2:]
        z = alpha * a_ref[...]
        for r in add_refs:
            z = z + r[...]
        if has_gate:
            z = z + val_ref[...] * _sigmoid(pre_ref[...])
        mu = jnp.mean(z, axis=-1, keepdims=True)
        zc = z - mu
        var = jnp.mean(zc * zc, axis=-1, keepdims=True)
        rstd = lax.rsqrt(var + LN_EPS)
        xhat = zc * rstd
        h = xhat * g_ref[...] + b_ref[...]
        outs[0][...] = h
        outs[1][...] = xhat
        outs[2][...] = jnp.broadcast_to(rstd, (tile, LANES))
        outs[3][...] = _bf(h)
        if want_perm:
            outs[4][...] = h

    ins, specs = [a], [_nat(tile, d)]
    for arr, layout in adds:
        x_, sp = _row_spec(arr, layout, tile, d)
        ins.append(x_)
        specs.append(sp)
    if has_gate:
        layout = gate[0]
        for arr, ncb, cb in gate[1:]:
            x_, sp = _row_spec(arr, layout, tile, d, ncb=ncb, cb=cb)
            ins.append(x_)
            specs.append(sp)
    ins += [g.reshape(1, d), b.reshape(1, d)]
    specs += [_whole((1, d)), _whole((1, d))]
    out_shape = [jax.ShapeDtypeStruct((s, d), F32), jax.ShapeDtypeStruct((s, d), F32),
                 jax.ShapeDtypeStruct((s, LANES), F32), jax.ShapeDtypeStruct((s, d), BF16)]
    out_specs = [_nat(tile, d), _nat(tile, d), _nat(tile, LANES), _nat(tile, d)]
    if want_perm:
        seg_tiles = s // SUBLANES // tile
        out_shape.append(jax.ShapeDtypeStruct((s // SUBLANES, SUBLANES * d), F32))
        out_specs.append(_perm(tile, d, seg_tiles))
    res = pl.pallas_call(
        body, name=name, grid=(s // tile,), in_specs=specs, out_specs=out_specs, out_shape=out_shape,
        compiler_params=_params("parallel"),
    )(*ins)
    return res[0], res[1], res[2], res[3], (res[4].reshape(s, d) if want_perm else None)


def _ln_bwd(name, dparts, xhat, rstd, g, gate=None, tile=256):
    s, d = xhat.shape
    n_part = len(dparts)
    coefs = [c for _, _, c in dparts]
    has_gate = gate is not None

    def body(*refs):
        part_refs = refs[:n_part]
        xhat_ref, rstd_ref, g_ref = refs[n_part:n_part + 3]
        pos = n_part + 3
        if has_gate:
            val_ref, pre_ref = refs[pos], refs[pos + 1]
            pos += 2
        outs = list(refs[pos:])
        dz_ref = outs.pop(0)
        dzb_ref = outs.pop(0)
        dgate_ref = outs.pop(0) if has_gate else None
        dg_ref, db_ref = outs
        dh = coefs[0] * part_refs[0][...]
        for c, r in zip(coefs[1:], part_refs[1:]):
            dh = dh + c * r[...]
        xh = xhat_ref[...]
        dxh = dh * g_ref[...]
        m1 = jnp.mean(dxh, axis=-1, keepdims=True)
        m2 = jnp.mean(dxh * xh, axis=-1, keepdims=True)
        dz = rstd_ref[:, 0:1] * (dxh - m1 - xh * m2)
        dz_ref[...] = dz
        dzb_ref[...] = _bf(dz)
        if has_gate:
            sg = _sigmoid(pre_ref[...])
            dval = dz * sg
            dpre = dz * val_ref[...] * sg * (1.0 - sg)
            dgate_ref[...] = jnp.concatenate([_bf(dval), _bf(dpre)], axis=1)

        @pl.when(pl.program_id(0) == 0)
        def _():
            dg_ref[...] = jnp.zeros_like(dg_ref)
            db_ref[...] = jnp.zeros_like(db_ref)

        dg_ref[0:1, :] += jnp.sum(dh * xh, axis=0, keepdims=True)
        db_ref[0:1, :] += jnp.sum(dh, axis=0, keepdims=True)

    ins, specs = [], []
    for arr, layout, _ in dparts:
        x_, sp = _row_spec(arr, layout, tile, d)
        ins.append(x_)
        specs.append(sp)
    ins += [xhat, rstd, g.reshape(1, d)]
    specs += [_nat(tile, d), _nat(tile, LANES), _whole((1, d))]
    gate_layout = None
    if has_gate:
        gate_layout = gate[0]
        for arr, ncb, cb in gate[1:]:
            x_, sp = _row_spec(arr, gate_layout, tile, d, ncb=ncb, cb=cb)
            ins.append(x_)
            specs.append(sp)
    seg_tiles = s // SUBLANES // tile
    out_shape = [jax.ShapeDtypeStruct((s, d), F32), jax.ShapeDtypeStruct((s, d), BF16)]
    out_specs = [_nat(tile, d), _nat(tile, d)]
    if has_gate:
        if gate_layout == 'nat':
            out_shape.append(jax.ShapeDtypeStruct((s, 2 * d), BF16))
            out_specs.append(_nat(tile, 2 * d))
        else:
            out_shape.append(jax.ShapeDtypeStruct((s // SUBLANES, SUBLANES * 2 * d), BF16))
            out_specs.append(_perm(tile, 2 * d, seg_tiles))
    out_shape += [jax.ShapeDtypeStruct((SUBLANES, d), F32)] * 2
    out_specs += [_whole((SUBLANES, d))] * 2
    res = list(pl.pallas_call(
        body, name=name, grid=(s // tile,), in_specs=specs, out_specs=out_specs, out_shape=out_shape,
        compiler_params=_params("arbitrary"),
    )(*ins))
    out = [res.pop(0), res.pop(0)]
    if has_gate:
        out.append(res.pop(0).reshape(s, 2 * d))
    out += [res[0][0], res[1][0]]
    return out


def _loss_partial(h, target, tile=256):
    s, d = h.shape

    def body(h_ref, t_ref, o_ref):
        @pl.when(pl.program_id(0) == 0)
        def _():
            o_ref[...] = jnp.zeros_like(o_ref)

        e = h_ref[...] - t_ref[...]
        sq = e * e
        part = sq[:, 0:LANES]
        for k in range(1, d // LANES):
            part = part + sq[:, k * LANES:(k + 1) * LANES]
        o_ref[0:1, :] += jnp.sum(part, axis=0, keepdims=True) * (0.5 / d)

    return pl.pallas_call(
        body, name="loss_partial", grid=(s // tile,), in_specs=[_nat(tile, d), _nat(tile, d)],
        out_specs=_whole((SUBLANES, LANES)), out_shape=jax.ShapeDtypeStruct((SUBLANES, LANES), F32),
        compiler_params=_params("arbitrary"),
    )(h, target)


def _ffn_in_swiglu(name, a, wit, tm=1024, tn=704):
    m, kdim = a.shape
    f = wit.shape[0] // 2
    tm, tn = _pick(m, tm), _pick(f, tn)
    nj = f // tn

    def body(a_ref, bg_ref, bu_ref, g_ref, u_ref, act_ref):
        av = _bf(a_ref[...])
        gg = _dot(av, bg_ref[...], NT)
        uu = _dot(av, bu_ref[...], NT)
        g_ref[...] = gg
        u_ref[...] = uu
        act_ref[...] = _bf(gg * _sigmoid(gg) * uu)

    ospec = pl.BlockSpec((tm, tn), lambda i, j: (i, j))
    return pl.pallas_call(
        body, name=name, grid=(m // tm, nj),
        in_specs=[pl.BlockSpec((tm, kdim), lambda i, j: (i, 0)), pl.BlockSpec((tn, kdim), lambda i, j: (j, 0)),
                  pl.BlockSpec((tn, kdim), lambda i, j: (j + nj, 0))],
        out_specs=[ospec, ospec, ospec],
        out_shape=[jax.ShapeDtypeStruct((m, f), F32), jax.ShapeDtypeStruct((m, f), F32), jax.ShapeDtypeStruct((m, f), BF16)],
        compiler_params=_params("parallel", "parallel"),
    )(a, wit, wit)


def _swiglu_bwd(name, g, u, dact, tile=128):
    s, f = g.shape
    f2 = 2 * f

    def body(g_ref, u_ref, da_ref, o_ref):
        gg = g_ref[...]
        sg = _sigmoid(gg)
        da = da_ref[...].astype(F32)
        silu = gg * sg
        o_ref[:, :f] = _bf(da * u_ref[...] * (sg + silu * (1.0 - sg)))
        o_ref[:, f:] = _bf(da * silu)

    return pl.pallas_call(
        body, name=name, grid=(s // tile,),
        in_specs=[_nat(tile, f), _nat(tile, f), _nat(tile, f)], out_specs=_nat(tile, f2),
        out_shape=jax.ShapeDtypeStruct((s, f2), BF16), compiler_params=_params("parallel"),
    )(g, u, dact)


def _rms_fwd(proj, ql, kvl, gq, gkv, tile=256):
    s = proj.shape[0]
    assert ql == kvl

    def body(q_ref, kv_ref, gq_ref, gkv_ref, o_ref):
        def nrm(x, gg):
            return x * lax.rsqrt(jnp.mean(x * x, axis=-1, keepdims=True) + RMS_EPS) * gg

        o_ref[...] = jnp.concatenate([_bf(nrm(q_ref[...], gq_ref[...])), _bf(nrm(kv_ref[...], gkv_ref[...]))], axis=1)

    return pl.pallas_call(
        body, name="mla_rms_fwd", grid=(s // tile,),
        in_specs=[_nat(tile, ql, 0), _nat(tile, kvl, 1), _whole((1, ql)), _whole((1, kvl))],
        out_specs=_nat(tile, ql + kvl), out_shape=jax.ShapeDtypeStruct((s, ql + kvl), BF16),
        compiler_params=_params("parallel"),
    )(proj, proj, gq.reshape(1, ql), gkv.reshape(1, kvl))


def _rope_coeffs(pos, invf):
    ang = pos * invf
    cs, sn = jnp.cos(ang), jnp.sin(ang)
    lane = lax.broadcasted_iota(jnp.int32, ang.shape, 1)
    half = ROPE // 2
    c = jnp.where(lane < ROPE, cs, 0.0)
    sa = jnp.where(lane < half, -sn, 0.0)
    sb = jnp.where((lane >= half) & (lane < ROPE), sn, 0.0)
    return c, sa, sb


def _rope_prep(q_raw, kv, proj, kpe_cb, pos, invf, heads, tile=256):
    s = q_raw.shape[0]
    half = ROPE // 2

    def body(q_ref, kv_ref, kpe_ref, pos_ref, invf_ref, qf_ref, kf_ref, v_ref):
        c, sa, sb = _rope_coeffs(pos_ref[...], invf_ref[...])

        def rope(t):
            return t * c + pltpu.roll(t, LANES - half, 1) * sa + pltpu.roll(t, half, 1) * sb

        kr = _bf(rope(kpe_ref[...]))
        for hh in range(heads):
            o = hh * QK_PAD
            qf_ref[:, o:o + NOPE] = _bf(q_ref[:, o:o + NOPE])
            qf_ref[:, o + NOPE:o + QK_PAD] = _bf(rope(q_ref[:, o + NOPE:o + QK_PAD]))
            kf_ref[:, o:o + NOPE] = _bf(kv_ref[:, o:o + NOPE])
            kf_ref[:, o + NOPE:o + QK_PAD] = kr
            v_ref[:, hh * VDIM:(hh + 1) * VDIM] = _bf(kv_ref[:, o + NOPE:o + QK_PAD])

    w = heads * QK_PAD
    return pl.pallas_call(
        body, name="mla_rope_prep", grid=(s // tile,),
        in_specs=[_nat(tile, w), _nat(tile, w), _nat(tile, LANES, kpe_cb), _nat(tile, 1), _whole((1, LANES))],
        out_specs=[_nat(tile, w), _nat(tile, w), _nat(tile, heads * VDIM)],
        out_shape=[jax.ShapeDtypeStruct((s, w), BF16), jax.ShapeDtypeStruct((s, w), BF16),
                   jax.ShapeDtypeStruct((s, heads * VDIM), BF16)],
        compiler_params=_params("parallel"),
    )(q_raw, kv, proj, pos, invf)


def _rope_unprep(dqf, dkf, dv, pos, invf, heads, tile=256):
    s = dqf.shape[0]
    half = ROPE // 2

    def body(dq_ref, dk_ref, dv_ref, pos_ref, invf_ref, dqr_ref, dkv_ref, dkpe_ref):
        c, sa, sb = _rope_coeffs(pos_ref[...], invf_ref[...])

        def unrope(gt):
            return gt * c + pltpu.roll(gt * sa, half, 1) + pltpu.roll(gt * sb, LANES - half, 1)

        dkpe = jnp.zeros((tile, LANES), F32)
        for hh in range(heads):
            o = hh * QK_PAD
            dqr_ref[:, o:o + NOPE] = _bf(dq_ref[:, o:o + NOPE])
            dqr_ref[:, o + NOPE:o + QK_PAD] = _bf(unrope(dq_ref[:, o + NOPE:o + QK_PAD]))
            dkv_ref[:, o:o + NOPE] = _bf(dk_ref[:, o:o + NOPE])
            dkv_ref[:, o + NOPE:o + QK_PAD] = _bf(dv_ref[:, hh * VDIM:(hh + 1) * VDIM])
            dkpe = dkpe + dk_ref[:, o + NOPE:o + QK_PAD]
        dkpe_ref[...] = unrope(dkpe)

    w = heads * QK_PAD
    return pl.pallas_call(
        body, name="mla_rope_unprep", grid=(s // tile,),
        in_specs=[_nat(tile, w), _nat(tile, w), _nat(tile, heads * VDIM), _nat(tile, 1), _whole((1, LANES))],
        out_specs=[_nat(tile, w), _nat(tile, w), _nat(tile, LANES)],
        out_shape=[jax.ShapeDtypeStruct((s, w), BF16), jax.ShapeDtypeStruct((s, w), BF16),
                   jax.ShapeDtypeStruct((s, LANES), F32)],
        compiler_params=_params("parallel"),
    )(dqf, dkf, dv, pos, invf)


LOG2E = 1.4426950408889634
MLA_SCALE = (NOPE + ROPE) ** -0.5


def _mla_scores_t(k, q, t, masked):
    sc = _dot(k, q, NT) * (MLA_SCALE * LOG2E)
    if masked:
        row = lax.broadcasted_iota(jnp.int32, (t, t), 0)
        col = lax.broadcasted_iota(jnp.int32, (t, t), 1)
        sc = jnp.where(row <= col, sc, NEG)
    return sc


def _mla_fwd(qf, kf, vt, heads, t=512):
    s = qf.shape[0]
    t = min(t, s)
    nq = s // t

    def body(q_ref, k_ref, vt_ref, o_ref, lse_ref, m_ref, l_ref, acc_ref):
        i = pl.program_id(1)
        m_ref[...] = jnp.full_like(m_ref, NEG)
        l_ref[...] = jnp.zeros_like(l_ref)
        acc_ref[...] = jnp.zeros_like(acc_ref)
        q = q_ref[...]

        def block(j, masked):
            r0 = pl.multiple_of(j * t, t)
            sc = _mla_scores_t(k_ref[pl.ds(r0, t), :], q, t, masked)
            m_prev = m_ref[0:1, :]
            m_new = jnp.maximum(m_prev, jnp.max(sc, axis=0, keepdims=True))
            corr = jnp.exp2(m_prev - m_new)
            p = jnp.exp2(sc - m_new)
            l_new = corr * l_ref[0:1, :] + jnp.sum(p, axis=0, keepdims=True)
            acc_ref[...] = corr * acc_ref[...] + _dot(vt_ref[:, pl.ds(r0, t)], _bf(p), NN)
            m_ref[...] = jnp.broadcast_to(m_new, (SUBLANES, t))
            l_ref[...] = jnp.broadcast_to(l_new, (SUBLANES, t))

        def unmasked(j, carry):
            block(j, False)
            return carry

        lax.fori_loop(0, i, unmasked, 0)
        block(i, True)
        o_ref[...] = (acc_ref[...] / l_ref[0:1, :]).T
        lse_ref[...] = m_ref[...] + jnp.log(l_ref[...]) * LOG2E

    return pl.pallas_call(
        body, name="mla_flash_fwd", grid=(heads, nq),
        in_specs=[pl.BlockSpec((t, QK_PAD), lambda h, i: (i, h)), pl.BlockSpec((s, QK_PAD), lambda h, i: (0, h)),
                  pl.BlockSpec((VDIM, s), lambda h, i: (h, 0))],
        out_specs=[pl.BlockSpec((t, VDIM), lambda h, i: (i, h)), pl.BlockSpec((SUBLANES, t), lambda h, i: (h, i))],
        out_shape=[jax.ShapeDtypeStruct((s, heads * VDIM), F32), jax.ShapeDtypeStruct((heads * SUBLANES, s), F32)],
        scratch_shapes=[pltpu.VMEM((SUBLANES, t), F32), pltpu.VMEM((SUBLANES, t), F32), pltpu.VMEM((VDIM, t), F32)],
        compiler_params=_params("parallel", "arbitrary"),
    )(qf, kf, vt)


def _mla_bwd(qf, kf, v, do, lse_t, delta_t, heads, do_cb0, t=512):
    s = qf.shape[0]
    t = min(t, s)
    nq = s // t

    def body(q_ref, k_ref, v_ref, do_ref, lse_ref, dl_ref, dq_ref, dk_ref, dv_ref, acc_ref):
        i = pl.program_id(1)

        @pl.when(i == 0)
        def _():
            dk_ref[...] = jnp.zeros_like(dk_ref)
            dv_ref[...] = jnp.zeros_like(dv_ref)

        acc_ref[...] = jnp.zeros_like(acc_ref)
        q, dob = q_ref[...], do_ref[...]
        lse, dl = lse_ref[0:1, :], dl_ref[0:1, :]

        def block(j, masked):
            r0 = pl.multiple_of(j * t, t)
            k = k_ref[pl.ds(r0, t), :]
            p = jnp.exp2(_mla_scores_t(k, q, t, masked) - lse)
            dp = _dot(v_ref[pl.ds(r0, t), :], dob, NT)
            ds = _bf(p * (dp - dl) * MLA_SCALE)
            acc_ref[...] += _dot(ds, k, TN)
            dk_ref[pl.ds(r0, t), :] += _dot(ds, q, NN)
            dv_ref[pl.ds(r0, t), :] += _dot(_bf(p), dob, NN)

        def unmasked(j, carry):
            block(j, False)
            return carry

        lax.fori_loop(0, i, unmasked, 0)
        block(i, True)
        dq_ref[...] = acc_ref[...]

    qs = lambda w, off=0: pl.BlockSpec((t, w), lambda h, i: (i, h + off))
    ks = lambda w: pl.BlockSpec((s, w), lambda h, i: (0, h))
    st = pl.BlockSpec((SUBLANES, t), lambda h, i: (h, i))
    return pl.pallas_call(
        body, name="mla_flash_bwd", grid=(heads, nq),
        in_specs=[qs(QK_PAD), ks(QK_PAD), ks(VDIM), qs(VDIM, do_cb0), st, st],
        out_specs=[qs(QK_PAD), ks(QK_PAD), ks(VDIM)],
        out_shape=[jax.ShapeDtypeStruct((s, heads * QK_PAD), F32), jax.ShapeDtypeStruct((s, heads * QK_PAD), F32),
                   jax.ShapeDtypeStruct((s, heads * VDIM), F32)],
        scratch_shapes=[pltpu.VMEM((t, QK_PAD), F32)],
        compiler_params=_params("parallel", "arbitrary"),
    )(qf, kf, v, do, lse_t, delta_t)


DIL_BLOCK = 2048


def _dil_unit_rows(u, dil, block):
    sub = u // dil
    return u % dil + (dil * DIL_STEPS) * sub, sub == 0


def _dil_rows(base, dil):
    return pl.ds(base, DIL_STEPS, stride=dil) if dil > 1 else pl.ds(base, DIL_STEPS)


def _dil_unit_scores(q, kp, kc, slope, dil, no_prev):
    sc = jnp.concatenate([_dot(q, kp, NT), _dot(q, kc, NT)], axis=1) * (DHD ** -0.5)
    row = lax.broadcasted_iota(jnp.int32, (DIL_STEPS, 2 * DIL_STEPS), 0)
    col = lax.broadcasted_iota(jnp.int32, (DIL_STEPS, 2 * DIL_STEPS), 1)
    dist = row + DIL_STEPS - col
    valid = (dist >= 0) & (dist <= DIL_STEPS) & (jnp.logical_not(no_prev) | (col >= DIL_STEPS))
    return jnp.where(valid, sc - slope * (dil * dist).astype(F32), NEG)


def _dil_in_specs(pw, dh, q_cb, block, rev_nb=None):
    blk = (lambda i: i) if rev_nb is None else (lambda i: rev_nb - 1 - i)
    own = lambda off: pl.BlockSpec((block, DHD), lambda h, i: (blk(i), off + h))
    prev = lambda off: pl.BlockSpec((block, DHD), lambda h, i: (jnp.maximum(blk(i) - 1, 0), off + h))
    return [own(q_cb), own(q_cb + dh), prev(q_cb + dh), own(q_cb + 2 * dh), prev(q_cb + 2 * dh)]


def _dil_fused_fwd(proj, slopes, dh, q_cb):
    s, pw = proj.shape
    block = min(DIL_BLOCK, s)
    nb = s // block
    n_units = block // DIL_STEPS
    nbr = len(DIL_BRANCHES)
    assert block >= DIL_STEPS * max(d for _, d in DIL_BRANCHES)

    def body(q_ref, kc_ref, kp_ref, vc_ref, vp_ref, sl_ref, o_ref, lse_ref, kk, vv, *per_branch):
        og, mg, lg = per_branch[:nbr], per_branch[nbr:2 * nbr], per_branch[2 * nbr:]
        i = pl.program_id(1)
        kk[0:block, :] = kp_ref[...]
        kk[block:, :] = kc_ref[...]
        vv[0:block, :] = vp_ref[...]
        vv[block:, :] = vc_ref[...]
        slope = sl_ref[0:1, 0:1]
        for g, (_, dil) in enumerate(DIL_BRANCHES):
            rows = functools.partial(_dil_rows, dil=dil)

            def unit(u, dil=dil, rows=rows):
                q0, first = _dil_unit_rows(u, dil, block)
                q = _bf(q_ref[rows(q0), :])
                kc, kp = _bf(kk[rows(block + q0), :]), _bf(kk[rows(block + q0 - dil * DIL_STEPS), :])
                vc, vp = _bf(vv[rows(block + q0), :]), _bf(vv[rows(block + q0 - dil * DIL_STEPS), :])
                sc = _dil_unit_scores(q, kp, kc, slope, dil, first & (i == 0))
                m = jnp.max(sc, axis=-1, keepdims=True)
                e = jnp.exp(sc - m)
                o = _dot(_bf(e[:, :DIL_STEPS]), vp, NN) + _dot(_bf(e[:, DIL_STEPS:]), vc, NN)
                return q0, o, m, jnp.sum(e, axis=-1, keepdims=True)

            def pair(u, carry, g=g, rows=rows, unit=unit):
                for q0, o, m, lsum in (unit(u), unit(u + n_units // 2)):
                    og[g][rows(q0), :] = o
                    mg[g][rows(q0), :] = jnp.broadcast_to(m, (DIL_STEPS, LANES))
                    lg[g][rows(q0), :] = jnp.broadcast_to(lsum, (DIL_STEPS, LANES))
                return carry

            lax.fori_loop(0, n_units // 2, pair, 0, unroll=2)
        m_all = mg[0][...]
        for g in range(1, nbr):
            m_all = jnp.maximum(m_all, mg[g][...])
        tot = jnp.zeros((block, LANES), F32)
        acc = jnp.zeros((block, DHD), F32)
        for g in range(nbr):
            w = jnp.exp(mg[g][...] - m_all)
            tot = tot + w * lg[g][...]
            acc = acc + w * og[g][...]
        o_ref[...] = acc / tot
        lse_ref[...] = m_all + jnp.log(tot)

    ospec = pl.BlockSpec((block, DHD), lambda h, i: (i, h))
    return pl.pallas_call(
        body, name="dil_fused_fwd", grid=(dh, nb),
        in_specs=_dil_in_specs(pw, dh, q_cb, block) + [pl.BlockSpec((SUBLANES, LANES), lambda h, i: (h, 0))],
        out_specs=[ospec, ospec], out_shape=[jax.ShapeDtypeStruct((s, dh * DHD), F32)] * 2,
        scratch_shapes=[pltpu.VMEM((2 * block, DHD), F32), pltpu.VMEM((2 * block, DHD), F32)]
        + [pltpu.VMEM((block, DHD), F32)] * (3 * nbr),
        compiler_params=_params("parallel", "arbitrary"),
    )(proj, proj, proj, proj, proj, slopes)


def _dil_fused_bwd(proj, slopes, datt, lse, delta, dh, q_cb, b_cb0):
    s, pw = proj.shape
    block = min(DIL_BLOCK, s)
    nb = s // block
    n_units = block // DIL_STEPS
    scale = DHD ** -0.5

    def body(q_ref, kc_ref, kp_ref, vc_ref, vp_ref, sl_ref, do_ref, lse_ref, dl_ref, dq_ref, dk_ref, dv_ref,
             kk, vv, dkk, dvv, carry_k, carry_v):
        ii = pl.program_id(1)
        i = nb - 1 - ii

        @pl.when(ii == 0)
        def _():
            carry_k[...] = jnp.zeros_like(carry_k)
            carry_v[...] = jnp.zeros_like(carry_v)

        kk[0:block, :] = kp_ref[...]
        kk[block:, :] = kc_ref[...]
        vv[0:block, :] = vp_ref[...]
        vv[block:, :] = vc_ref[...]
        dkk[...] = jnp.zeros_like(dkk)
        dvv[...] = jnp.zeros_like(dvv)
        dq_ref[...] = jnp.zeros_like(dq_ref)
        slope = sl_ref[0:1, 0:1]
        for _, dil in DIL_BRANCHES:
            rows = functools.partial(_dil_rows, dil=dil)

            def unit(u, dil=dil, rows=rows):
                q0, first = _dil_unit_rows(u, dil, block)
                cur, prev = rows(block + q0), rows(block + q0 - dil * DIL_STEPS)
                q = _bf(q_ref[rows(q0), :])
                kc, kp, vc, vp = _bf(kk[cur, :]), _bf(kk[prev, :]), _bf(vv[cur, :]), _bf(vv[prev, :])
                dob = _bf(do_ref[rows(q0), :])
                sc = _dil_unit_scores(q, kp, kc, slope, dil, first & (i == 0))
                p = jnp.exp(sc - lse_ref[rows(q0), 0:1])
                dp = jnp.concatenate([_dot(dob, vp, NT), _dot(dob, vc, NT)], axis=1)
                ds = _bf(p * (dp - dl_ref[rows(q0), 0:1]) * scale)
                pb = _bf(p)
                return (rows(q0), cur, prev, _dot(ds[:, :DIL_STEPS], kp, NN) + _dot(ds[:, DIL_STEPS:], kc, NN),
                        _dot(ds[:, :DIL_STEPS], q, TN), _dot(ds[:, DIL_STEPS:], q, TN),
                        _dot(pb[:, :DIL_STEPS], dob, TN), _dot(pb[:, DIL_STEPS:], dob, TN))

            def pair(u, carry, unit=unit):
                for qrows, cur, prev, dq, dkp, dkc, dvp, dvc in (unit(u), unit(u + n_units // 2)):
                    dq_ref[qrows, :] += dq
                    dkk[prev, :] += dkp
                    dkk[cur, :] += dkc
                    dvv[prev, :] += dvp
                    dvv[cur, :] += dvc
                return carry

            lax.fori_loop(0, n_units // 2, pair, 0, unroll=2)
        dk_ref[...] = dkk[block:, :] + carry_k[...]
        dv_ref[...] = dvv[block:, :] + carry_v[...]
        carry_k[...] = dkk[0:block, :]
        carry_v[...] = dvv[0:block, :]

    rev = lambda i: nb - 1 - i
    mspec = pl.BlockSpec((block, DHD), lambda h, i: (rev(i), b_cb0 + h))
    ospec = pl.BlockSpec((block, DHD), lambda h, i: (rev(i), h))
    big = lambda: pltpu.VMEM((2 * block, DHD), F32)
    return pl.pallas_call(
        body, name="dil_fused_bwd", grid=(dh, nb),
        in_specs=_dil_in_specs(pw, dh, q_cb, block, rev_nb=nb)
        + [pl.BlockSpec((SUBLANES, LANES), lambda h, i: (h, 0)), mspec, ospec, mspec],
        out_specs=[ospec, ospec, ospec], out_shape=[jax.ShapeDtypeStruct((s, dh * DHD), F32)] * 3,
        scratch_shapes=[big(), big(), big(), big(), pltpu.VMEM((block, DHD), F32), pltpu.VMEM((block, DHD), F32)],
        compiler_params=_params("parallel", "arbitrary"),
    )(proj, proj, proj, proj, proj, slopes, datt, lse, delta)


def _concat_bf16(name, a, b, tile=256):
    s, wa = a.shape
    wb = b.shape[1]

    def body(a_ref, b_ref, o_ref):
        o_ref[...] = jnp.concatenate([_bf(a_ref[...]), _bf(b_ref[...])], axis=1)

    return pl.pallas_call(
        body, name=name, grid=(s // tile,), in_specs=[_nat(tile, wa), _nat(tile, wb)], out_specs=_nat(tile, wa + wb),
        out_shape=jax.ShapeDtypeStruct((s, wa + wb), BF16), compiler_params=_params("parallel"),
    )(a, b)


def _attn_bwd_prep(datt, out_a, out_b, tile=256):
    s, mixw = datt.shape
    wa = out_a.shape[1]
    heads_a = wa // LANES

    def body(d_ref, a_ref, b_ref, do_ref, dl_ref, dlt_ref):
        d = d_ref[...]
        do_ref[...] = _bf(d)
        prod = d * jnp.concatenate([a_ref[...], b_ref[...]], axis=1)
        for hh in range(mixw // LANES):
            sl = slice(hh * LANES, (hh + 1) * LANES)
            dl = jnp.broadcast_to(jnp.sum(prod[:, sl], axis=-1, keepdims=True), (tile, LANES))
            dl_ref[:, sl] = dl
            if hh < heads_a:
                dlt_ref[hh * SUBLANES:(hh + 1) * SUBLANES, :] = dl.T[0:SUBLANES, :]

    return pl.pallas_call(
        body, name="attn_bwd_prep", grid=(s // tile,),
        in_specs=[_nat(tile, mixw), _nat(tile, wa), _nat(tile, mixw - wa)],
        out_specs=[_nat(tile, mixw), _nat(tile, mixw), pl.BlockSpec((heads_a * SUBLANES, tile), lambda i: (0, i))],
        out_shape=[jax.ShapeDtypeStruct((s, mixw), BF16), jax.ShapeDtypeStruct((s, mixw), F32),
                   jax.ShapeDtypeStruct((heads_a * SUBLANES, s), F32)],
        compiler_params=_params("parallel"),
    )(datt, out_a, out_b)


def _dproj_assemble(proj, dnq, dnkv, dkpe, dqs, dks, dvs, gq, gkv, ql, tile=256):
    s, pw = proj.shape
    dw = dqs[0].shape[1]
    nbr = len(dqs)

    def body(*refs):
        ql_ref, kvl_ref, dnq_ref, dnkv_ref, dkpe_ref = refs[:5]
        br = refs[5:5 + 3 * nbr]
        gq_ref, gkv_ref = refs[5 + 3 * nbr:7 + 3 * nbr]
        dp_ref, dgq_ref, dgkv_ref = refs[7 + 3 * nbr:]

        @pl.when(pl.program_id(0) == 0)
        def _():
            dgq_ref[...] = jnp.zeros_like(dgq_ref)
            dgkv_ref[...] = jnp.zeros_like(dgkv_ref)

        def rms_bwd(x, dy, gg, dg_ref):
            r = lax.rsqrt(jnp.mean(x * x, axis=-1, keepdims=True) + RMS_EPS)
            xh = x * r
            dxh = dy * gg
            dg_ref[0:1, :] += jnp.sum(dy * xh, axis=0, keepdims=True)
            return r * (dxh - xh * jnp.mean(dxh * xh, axis=-1, keepdims=True))

        pieces = [_bf(rms_bwd(ql_ref[...], dnq_ref[...], gq_ref[...], dgq_ref)),
                  _bf(rms_bwd(kvl_ref[...], dnkv_ref[...], gkv_ref[...], dgkv_ref)),
                  _bf(dkpe_ref[...])]
        for k in range(3):
            acc = br[k * nbr][...]
            for r in br[k * nbr + 1:(k + 1) * nbr]:
                acc = acc + r[...]
            pieces.append(_bf(acc))
        dp_ref[...] = jnp.concatenate(pieces, axis=1)

    res = pl.pallas_call(
        body, name="dproj_assemble", grid=(s // tile,),
        in_specs=[_nat(tile, ql, 0), _nat(tile, ql, 1), _nat(tile, ql), _nat(tile, ql), _nat(tile, LANES)]
        + [_nat(tile, dw)] * (3 * nbr) + [_whole((1, ql)), _whole((1, ql))],
        out_specs=[_nat(tile, pw), _whole((SUBLANES, ql)), _whole((SUBLANES, ql))],
        out_shape=[jax.ShapeDtypeStruct((s, pw), BF16), jax.ShapeDtypeStruct((SUBLANES, ql), F32),
                   jax.ShapeDtypeStruct((SUBLANES, ql), F32)],
        compiler_params=_params("arbitrary"),
    )(proj, proj, dnq, dnkv, dkpe, *dqs, *dks, *dvs, gq.reshape(1, ql), gkv.reshape(1, ql))
    return res[0], res[1][0], res[2][0]


def _axpy(name, alpha, a, b, tile=256):
    s, d = a.shape

    def body(a_ref, b_ref, o_ref):
        o_ref[...] = alpha * a_ref[...] + b_ref[...]

    return pl.pallas_call(
        body, name=name, grid=(s // tile,), in_specs=[_nat(tile, d), _nat(tile, d)], out_specs=_nat(tile, d),
        out_shape=jax.ShapeDtypeStruct((s, d), F32), compiler_params=_params("parallel"),
    )(a, b)


def _cmul(ar, ai, br, bi):
    return ar * br - ai * bi, ar * bi + ai * br


def _s5_discretise(a_re, a_im, log_dt, b_re, b_im, n_sq):
    shape = a_re.shape

    def body(ar_ref, ai_ref, ldt_ref, br_ref, bi_ref, abr_ref, abi_ref, apr_ref, api_ref, bbr_ref, bbi_ref):
        ar, ai = ar_ref[...], ai_ref[...]
        dt = jnp.exp(ldt_ref[...])
        e = jnp.exp(ar * dt)
        abr, abi = e * jnp.cos(ai * dt), e * jnp.sin(ai * dt)
        den = ar * ar + ai * ai
        qr = ((abr - 1.0) * ar + abi * ai) / den
        qi = (abi * ar - (abr - 1.0) * ai) / den
        bbr, bbi = _cmul(qr, qi, br_ref[...], bi_ref[...])
        abr_ref[...], abi_ref[...] = abr, abi
        bbr_ref[...], bbi_ref[...] = bbr, bbi
        pr, pi = abr, abi
        for _ in range(n_sq):
            pr, pi = _cmul(pr, pi, pr, pi)
        apr_ref[...], api_ref[...] = pr, pi

    return pl.pallas_call(
        body, name="s5_discretise", out_shape=[jax.ShapeDtypeStruct(shape, F32)] * 6,
        compiler_params=pltpu.CompilerParams(vmem_limit_bytes=VMEM_LIMIT),
    )(a_re, a_im, log_dt, b_re, b_im)


def _s5_discretise_bwd(a16, b16, ag, gab, gbb):
    rows, p = a16[0].shape
    g = rows // S5_GROUP

    def disc(ar, ai, ldt):
        dt = jnp.exp(ldt)
        e = jnp.exp(ar * dt)
        abr, abi = e * jnp.cos(ai * dt), e * jnp.sin(ai * dt)
        den = ar * ar + ai * ai
        inv_r, inv_i = ar / den, -ai / den
        qr, qi = _cmul(abr - 1.0, abi, inv_r, inv_i)
        return dt, abr, abi, inv_r, inv_i, qr, qi

    def body(ar16_ref, ai16_ref, ldt16_ref, br_ref, bi_ref, ar_ref, ai_ref, ldt_ref, gar_ref, gai_ref, gbr_ref, gbi_ref,
             dar_ref, dai_ref, dldt_ref, dbr_ref, dbi_ref):
        _, _, _, _, _, qr16, qi16 = disc(ar16_ref[...], ai16_ref[...], ldt16_ref[...])
        gbr, gbi = gbr_ref[...], gbi_ref[...]
        dbr_ref[...], dbi_ref[...] = _cmul(qr16, -qi16, gbr, gbi)
        cr, ci = _cmul(br_ref[...], -bi_ref[...], gbr, gbi)
        gqr = jnp.sum(cr.reshape(g, S5_GROUP, p), axis=1)
        gqi = jnp.sum(ci.reshape(g, S5_GROUP, p), axis=1)
        ar, ai = ar_ref[...], ai_ref[...]
        dt, abr, abi, inv_r, inv_i, qr, qi = disc(ar, ai, ldt_ref[...])
        t_r, t_i = _cmul(inv_r, -inv_i, gqr, gqi)
        gab_r = gar_ref[...] + t_r
        gab_i = gai_ref[...] + t_i
        qa_r, qa_i = _cmul(qr, qi, inv_r, inv_i)
        a1_r, a1_i = _cmul(qa_r, -qa_i, gqr, gqi)
        gl_r, gl_i = _cmul(abr, -abi, gab_r, gab_i)
        dar_ref[...] = dt * gl_r - a1_r
        dai_ref[...] = dt * gl_i - a1_i
        gdt = jnp.sum(ar * gl_r + ai * gl_i, axis=-1, keepdims=True)
        dldt_ref[...] = gdt * dt[:, 0:1]

    return pl.pallas_call(
        body, name="s5_discretise_bwd",
        out_shape=[jax.ShapeDtypeStruct((g, p), F32), jax.ShapeDtypeStruct((g, p), F32),
                   jax.ShapeDtypeStruct((g, 1), F32), jax.ShapeDtypeStruct((rows, p), F32),
                   jax.ShapeDtypeStruct((rows, p), F32)],
        compiler_params=pltpu.CompilerParams(vmem_limit_bytes=VMEM_LIMIT),
    )(*a16, *b16, *ag, *gab, *gbb)


def _slab_tile(re, im, nsl):
    row = jnp.concatenate([re.reshape(nsl, SLAB_COLS), im.reshape(nsl, SLAB_COLS)], axis=-1)
    return jnp.repeat(row, SUBLANES, axis=0)


def _slab_in_matrix(b_re, b_im, nsl):
    eye = jnp.eye(SLAB_GROUPS, dtype=F32)

    def blk(b):
        b = b.reshape(nsl, SLAB_GROUPS, S5_GROUP, S5_STATE)
        return jnp.einsum('sgcp,gh->sgchp', b, eye).reshape(nsl, LANES, SLAB_COLS)

    return jnp.concatenate([blk(b_re), blk(b_im)], axis=-1)


def _slab_in_extract(m, nsl):
    eye = jnp.eye(SLAB_GROUPS, dtype=F32)

    def ext(x_):
        x_ = x_.reshape(nsl, SLAB_GROUPS, S5_GROUP, SLAB_GROUPS, S5_STATE)
        return jnp.einsum('sgchp,gh->sgcp', x_, eye).reshape(nsl * LANES, S5_STATE)

    return ext(m[..., :SLAB_COLS]), ext(m[..., SLAB_COLS:])


def _slab_out_matrix(c_re, c_im, nsl):
    eye = jnp.eye(SLAB_GROUPS, dtype=F32)

    def blk(c):
        c = c.reshape(nsl, SLAB_GROUPS, S5_GROUP, S5_STATE)
        return jnp.einsum('sgcp,gh->sgphc', c, eye).reshape(nsl, SLAB_COLS, LANES)

    return jnp.concatenate([blk(c_re), -blk(c_im)], axis=1)


def _slab_out_extract(m, nsl):
    eye = jnp.eye(SLAB_GROUPS, dtype=F32)

    def ext(x_):
        x_ = x_.reshape(nsl, SLAB_GROUPS, S5_STATE, SLAB_GROUPS, S5_GROUP)
        return jnp.einsum('sgphc,gh->sgcp', x_, eye).reshape(nsl * SLAB_GROUPS, S5_GROUP, S5_STATE)

    return ext(m[:, :SLAB_COLS]), -ext(m[:, SLAB_COLS:])


def _gelu(y):
    t = jnp.tanh(0.7978845608028654 * (y + 0.044715 * y * y * y))
    return 0.5 * y * (1.0 + t)


def _gelu_grad(y):
    t = jnp.tanh(0.7978845608028654 * (y + 0.044715 * y * y * y))
    return 0.5 * (1.0 + t) + 0.5 * y * (1.0 - t * t) * 0.7978845608028654 * (1.0 + 3.0 * 0.044715 * y * y)


def _scan_rows(ref, n_steps, ar, ai, state, reverse, conj, keep=True):
    sgn = -1.0 if conj else 1.0

    def step(k, carry):
        xr, xi = carry
        t = (n_steps - 1 - k) if reverse else k
        r0 = pl.multiple_of(t * SUBLANES, SUBLANES)
        nr = ar * xr - sgn * ai * xi + ref[pl.ds(r0, SUBLANES), :SLAB_COLS]
        ni = ar * xi + sgn * ai * xr + ref[pl.ds(r0, SUBLANES), SLAB_COLS:]
        if keep:
            ref[pl.ds(r0, SUBLANES), :SLAB_COLS] = nr
            ref[pl.ds(r0, SUBLANES), SLAB_COLS:] = ni
        return nr, ni

    return lax.fori_loop(0, n_steps, step, state, unroll=4)


def _s5_pass1(hp, bblk, ab_tile, rc=1024):
    s, d = hp.shape
    nsl = d // LANES
    rc = min(rc, s)
    nch = s // rc
    w = 2 * SLAB_COLS

    def body(u_ref, b_ref, ab_ref, end_ref, st_ref, x_ref):
        j = pl.program_id(1)

        @pl.when(j == 0)
        def _():
            st_ref[...] = jnp.zeros_like(st_ref)

        x_ref[...] = _dot(_bf(u_ref[...]), b_ref[0], NN)
        xr, xi = _scan_rows(x_ref, rc // SUBLANES, ab_ref[:, :SLAB_COLS], ab_ref[:, SLAB_COLS:],
                            (st_ref[:, :SLAB_COLS], st_ref[:, SLAB_COLS:]), False, False, keep=False)
        st_ref[:, :SLAB_COLS] = xr
        st_ref[:, SLAB_COLS:] = xi

        @pl.when(j == nch - 1)
        def _():
            end_ref[...] = st_ref[...]

    return pl.pallas_call(
        body, name="s5_scan_local", grid=(nsl, nch),
        in_specs=[pl.BlockSpec((rc, LANES), lambda sl, j: (j, sl)), pl.BlockSpec((1, LANES, w), lambda sl, j: (sl, 0, 0)),
                  pl.BlockSpec((SUBLANES, w), lambda sl, j: (sl, 0))],
        out_specs=pl.BlockSpec((SUBLANES, w), lambda sl, j: (sl, 0)),
        out_shape=jax.ShapeDtypeStruct((nsl * SUBLANES, w), F32),
        scratch_shapes=[pltpu.VMEM((SUBLANES, w), F32), pltpu.VMEM((rc, w), F32)],
        compiler_params=_params("parallel", "arbitrary"),
    )(hp, bblk, ab_tile)


def _s5_carry(name, ends, ap_tile, reverse):
    rows, w = ends.shape
    nsl = rows // SUBLANES
    sgn = -1.0 if reverse else 1.0

    def body(e_ref, ap_ref, c_ref):
        pr, pi = ap_ref[0:1, :SLAB_COLS], sgn * ap_ref[0:1, SLAB_COLS:]
        tr = jnp.zeros((1, SLAB_COLS), F32)
        ti = jnp.zeros((1, SLAB_COLS), F32)
        order = range(SUBLANES - 1, -1, -1) if reverse else range(SUBLANES)
        for seg in order:
            c_ref[seg:seg + 1, :SLAB_COLS] = tr
            c_ref[seg:seg + 1, SLAB_COLS:] = ti
            mr, mi = _cmul(pr, pi, tr, ti)
            tr = e_ref[seg:seg + 1, :SLAB_COLS] + mr
            ti = e_ref[seg:seg + 1, SLAB_COLS:] + mi

    spec = pl.BlockSpec((SUBLANES, w), lambda sl: (sl, 0))
    return pl.pallas_call(
        body, name=name, grid=(nsl,), in_specs=[spec, spec], out_specs=spec,
        out_shape=jax.ShapeDtypeStruct((rows, w), F32), compiler_params=_params("parallel"),
    )(ends, ap_tile)


def _s5_pass2(hp, bblk, cin, ab_tile, cblk, dvec, rc=1024):
    s, d = hp.shape
    nsl = d // LANES
    rc = min(rc, s)
    nch = s // rc
    w = 2 * SLAB_COLS

    def body(h_ref, b_ref, cin_ref, ab_ref, c_ref, d_ref, x_ref, y_ref, z_ref, st_ref):
        j = pl.program_id(1)

        @pl.when(j == 0)
        def _():
            st_ref[...] = cin_ref[...]

        hv = h_ref[...]
        x_ref[...] = _dot(_bf(hv), b_ref[0], NN)
        xr, xi = _scan_rows(x_ref, rc // SUBLANES, ab_ref[:, :SLAB_COLS], ab_ref[:, SLAB_COLS:],
                            (st_ref[:, :SLAB_COLS], st_ref[:, SLAB_COLS:]), False, False)
        st_ref[:, :SLAB_COLS] = xr
        st_ref[:, SLAB_COLS:] = xi
        y = _dot(_bf(x_ref[...]), c_ref[0], NN) + d_ref[...] * hv
        y_ref[...] = y
        z_ref[...] = _bf(_gelu(y))

    tile = lambda wd: pl.BlockSpec((rc, wd), lambda sl, j: (j, sl))
    small = pl.BlockSpec((SUBLANES, w), lambda sl, j: (sl, 0))
    return pl.pallas_call(
        body, name="s5_scan_carry_out", grid=(nsl, nch),
        in_specs=[tile(LANES), pl.BlockSpec((1, LANES, w), lambda sl, j: (sl, 0, 0)), small, small,
                  pl.BlockSpec((1, w, LANES), lambda sl, j: (sl, 0, 0)), pl.BlockSpec((1, LANES), lambda sl, j: (0, sl))],
        out_specs=[tile(w), tile(LANES), tile(LANES)],
        out_shape=[jax.ShapeDtypeStruct((s, nsl * w), F32), jax.ShapeDtypeStruct((s, d), F32),
                   jax.ShapeDtypeStruct((s, d), BF16)],
        scratch_shapes=[pltpu.VMEM((SUBLANES, w), F32)],
        compiler_params=_params("parallel", "arbitrary"),
    )(hp, bblk, cin, ab_tile, cblk, dvec)


def _s5_bwd_pass1(dzg, ypre, cblk, ab_tile, hp, rc=1024):
    s, d = hp.shape
    nsl = d // LANES
    rc = min(rc, s)
    nch = s // rc
    w = 2 * SLAB_COLS

    def body(dz_ref, y_ref, c_ref, ab_ref, h_ref, st_out_ref, dy_ref, dd_ref, st_ref, lam_ref):
        j = pl.program_id(1)

        @pl.when(j == 0)
        def _():
            st_ref[...] = jnp.zeros_like(st_ref)
            dd_ref[...] = jnp.zeros_like(dd_ref)

        dy = dz_ref[...] * _gelu_grad(y_ref[...])
        dy_ref[...] = dy
        dd_ref[0:1, :] += jnp.sum(dy * h_ref[...], axis=0, keepdims=True)
        lam_ref[...] = _dot(_bf(dy), c_ref[0], NT)
        lr, li = _scan_rows(lam_ref, rc // SUBLANES, ab_ref[:, :SLAB_COLS], ab_ref[:, SLAB_COLS:],
                            (st_ref[:, :SLAB_COLS], st_ref[:, SLAB_COLS:]), True, True, keep=False)
        st_ref[:, :SLAB_COLS] = lr
        st_ref[:, SLAB_COLS:] = li

        @pl.when(j == nch - 1)
        def _():
            st_out_ref[...] = st_ref[...]

    tile = lambda wd: pl.BlockSpec((rc, wd), lambda sl, j: (nch - 1 - j, sl))
    small = pl.BlockSpec((SUBLANES, w), lambda sl, j: (sl, 0))
    return pl.pallas_call(
        body, name="s5_adjoint_local", grid=(nsl, nch),
        in_specs=[tile(LANES), tile(LANES), pl.BlockSpec((1, w, LANES), lambda sl, j: (sl, 0, 0)), small, tile(LANES)],
        out_specs=[small, tile(LANES), pl.BlockSpec((SUBLANES, LANES), lambda sl, j: (0, sl))],
        out_shape=[jax.ShapeDtypeStruct((nsl * SUBLANES, w), F32),
                   jax.ShapeDtypeStruct((s, d), F32), jax.ShapeDtypeStruct((SUBLANES, d), F32)],
        scratch_shapes=[pltpu.VMEM((SUBLANES, w), F32), pltpu.VMEM((rc, w), F32)],
        compiler_params=_params("parallel", "arbitrary"),
    )(dzg, ypre, cblk, ab_tile, hp)


def _s5_bwd_pass2(dy, cblk, cinl, ab_tile, xtrue, cinx, hp, bblk, dvec, rc=1024):
    s, d = hp.shape
    nsl = d // LANES
    rc = min(rc, s)
    nch = s // rc
    w = 2 * SLAB_COLS
    n_steps = rc // SUBLANES

    def body(dy_ref, c_ref, cl_ref, ab_ref, x_ref, xp_ref, cx_ref, h_ref, b_ref, d_ref,
             du_ref, db_ref, dc_ref, da_ref, st_ref, lam_ref, acc_ref):
        j = pl.program_id(1)

        @pl.when(j == 0)
        def _():
            st_ref[...] = cl_ref[...]
            acc_ref[...] = jnp.zeros_like(acc_ref)
            db_ref[...] = jnp.zeros_like(db_ref)
            dc_ref[...] = jnp.zeros_like(dc_ref)

        ar, ai = ab_ref[:, :SLAB_COLS], ab_ref[:, SLAB_COLS:]
        lam_ref[...] = _dot(_bf(dy_ref[...]), c_ref[0], NT)

        def advance(lr, li, r0):
            nr = ar * lr + ai * li + lam_ref[pl.ds(r0, SUBLANES), :SLAB_COLS]
            ni = ar * li - ai * lr + lam_ref[pl.ds(r0, SUBLANES), SLAB_COLS:]
            lam_ref[pl.ds(r0, SUBLANES), :SLAB_COLS] = nr
            lam_ref[pl.ds(r0, SUBLANES), SLAB_COLS:] = ni
            return nr, ni

        def step(k, carry):
            lr, li, dr, di = carry
            t = n_steps - 1 - k
            nr, ni = advance(lr, li, pl.multiple_of(t * SUBLANES, SUBLANES))
            rx = pl.multiple_of((t - 1) * SUBLANES, SUBLANES)
            xr, xi = x_ref[pl.ds(rx, SUBLANES), :SLAB_COLS], x_ref[pl.ds(rx, SUBLANES), SLAB_COLS:]
            return nr, ni, dr + xr * nr + xi * ni, di + xr * ni - xi * nr

        lr, li, dr, di = lax.fori_loop(
            0, n_steps - 1, step,
            (st_ref[:, :SLAB_COLS], st_ref[:, SLAB_COLS:], acc_ref[:, :SLAB_COLS], acc_ref[:, SLAB_COLS:]), unroll=4)
        lr, li = advance(lr, li, 0)
        st_ref[:, :SLAB_COLS] = lr
        st_ref[:, SLAB_COLS:] = li
        first_chunk = j == nch - 1
        xr = jnp.where(first_chunk, cx_ref[:, :SLAB_COLS], xp_ref[:, :SLAB_COLS])
        xi = jnp.where(first_chunk, cx_ref[:, SLAB_COLS:], xp_ref[:, SLAB_COLS:])
        acc_ref[:, :SLAB_COLS] = dr + xr * lr + xi * li
        acc_ref[:, SLAB_COLS:] = di + xr * li - xi * lr

        lam_b = _bf(lam_ref[...])
        dyv = dy_ref[...]
        db_ref[0] += _dot(_bf(h_ref[...]), lam_b, TN)
        dc_ref[0] += _dot(_bf(dyv), _bf(x_ref[...]), TN)
        du_ref[...] = _dot(lam_b, b_ref[0], NT) + d_ref[...] * dyv

        @pl.when(j == nch - 1)
        def _():
            da_ref[...] = jnp.broadcast_to(jnp.sum(acc_ref[...], axis=0, keepdims=True), (SUBLANES, w))

    sub = rc // SUBLANES
    tile = lambda wd: pl.BlockSpec((rc, wd), lambda sl, j: (nch - 1 - j, sl))
    small = pl.BlockSpec((SUBLANES, w), lambda sl, j: (sl, 0))
    prev = pl.BlockSpec((SUBLANES, w), lambda sl, j: (jnp.maximum((nch - 1 - j) * sub - 1, 0), sl))
    return pl.pallas_call(
        body, name="s5_adjoint_carry_grads", grid=(nsl, nch),
        in_specs=[tile(LANES), pl.BlockSpec((1, w, LANES), lambda sl, j: (sl, 0, 0)), small, small, tile(w), prev, small,
                  tile(LANES), pl.BlockSpec((1, LANES, w), lambda sl, j: (sl, 0, 0)),
                  pl.BlockSpec((1, LANES), lambda sl, j: (0, sl))],
        out_specs=[tile(LANES), pl.BlockSpec((1, LANES, w), lambda sl, j: (sl, 0, 0)),
                   pl.BlockSpec((1, LANES, w), lambda sl, j: (sl, 0, 0)), small],
        out_shape=[jax.ShapeDtypeStruct((s, d), F32), jax.ShapeDtypeStruct((nsl, LANES, w), F32),
                   jax.ShapeDtypeStruct((nsl, LANES, w), F32), jax.ShapeDtypeStruct((nsl * SUBLANES, w), F32)],
        scratch_shapes=[pltpu.VMEM((SUBLANES, w), F32), pltpu.VMEM((rc, w), F32), pltpu.VMEM((SUBLANES, w), F32)],
        compiler_params=_params("parallel", "arbitrary"),
    )(dy, cblk, cinl, ab_tile, xtrue, xtrue, cinx, hp, bblk, dvec)


def _adamw(name, w, g, m, v):
    r, c = w.shape
    tile = r if r * c <= 512 * 1024 else _pick(r, max(SUBLANES, (512 * 1024 // c) // SUBLANES * SUBLANES), q=SUBLANES)
    c1 = 1.0 / (1.0 - ADAM_B1 ** ADAM_STEP)
    c2 = 1.0 / (1.0 - ADAM_B2 ** ADAM_STEP)

    def body(w_ref, g_ref, m_ref, v_ref, d_ref, nm_ref, nv_ref):
        gg = g_ref[...]
        nm = ADAM_B1 * m_ref[...] + (1.0 - ADAM_B1) * gg
        nv = ADAM_B2 * v_ref[...] + (1.0 - ADAM_B2) * gg * gg
        d_ref[...] = -ADAM_LR * ((nm * c1) / (jnp.sqrt(nv * c2) + ADAM_EPS) + ADAM_WD * w_ref[...])
        nm_ref[...] = nm
        nv_ref[...] = nv

    spec = _nat(tile, c)
    return pl.pallas_call(
        body, name=name, grid=(r // tile,), in_specs=[spec] * 4, out_specs=[spec] * 3,
        out_shape=[jax.ShapeDtypeStruct((r, c), F32)] * 3, compiler_params=_params("parallel"),
    )(w, g, m, v)


def _place():
    x, y, c = lax.axis_index("x"), lax.axis_index("y"), lax.axis_index("c")
    return x, y, c, [(1 - x, y), (x, 1 - y), (1 - x, 1 - y)]


_ANY = pl.BlockSpec(memory_space=pl.ANY)


def _gather_weights(shards):
    n = len(shards)

    def body(*refs):
        ins, outs = refs[:n], refs[n:2 * n]
        send_sems, recv_sems, local_sems = refs[2 * n:]
        x, y, c, chips = _place()
        me = 2 * x + y
        sibling = (x, y, 1 - c)
        started = []
        for a in range(n):
            local = pltpu.make_async_copy(ins[a], outs[a].at[me], local_sems.at[a])
            local.start()
            started.append(local)

        def half(a, chip, h):
            hw = ins[a].shape[1] // 2
            return outs[a].at[chip, :, pl.ds(pl.multiple_of(h * hw, LANES), hw)]

        def copy(a, k, src, chip, h, to):
            return pltpu.make_async_remote_copy(
                src_ref=src, dst_ref=half(a, chip, h), send_sem=send_sems.at[a, k], recv_sem=recv_sems.at[a, k],
                device_id=to, device_id_type=MESH)

        sends = []
        for a in range(n):
            hw = ins[a].shape[1] // 2
            mine = ins[a].at[:, pl.ds(pl.multiple_of(c * hw, LANES), hw)]
            for k, chip in enumerate(chips):
                cp = copy(a, k, mine, me, c, (*chip, c))
                cp.start()
                sends.append(cp)
        for a in range(n):
            for k, (cx, cy) in enumerate(chips):
                src_chip = 2 * cx + cy
                copy(a, k, half(a, src_chip, c), src_chip, c, (x, y, c)).wait_recv()
                fwd = copy(a, 3 + k, half(a, src_chip, c), src_chip, c, sibling)
                fwd.start()
                sends.append(fwd)
        for a in range(n):
            for k, (cx, cy) in enumerate(chips):
                src_chip = 2 * cx + cy
                copy(a, 3 + k, half(a, src_chip, 1 - c), src_chip, 1 - c, (x, y, c)).wait_recv()
        for cp in sends:
            cp.wait_send()
        for cp in started:
            cp.wait()

    return pl.pallas_call(
        body, name="gather_weights",
        in_specs=[_ANY] * n, out_specs=[_ANY] * n,
        out_shape=[jax.ShapeDtypeStruct((N_CHIPS,) + s_.shape, s_.dtype) for s_ in shards],
        scratch_shapes=[pltpu.SemaphoreType.DMA((n, 6)), pltpu.SemaphoreType.DMA((n, 6)), pltpu.SemaphoreType.DMA((n,))],

    )(*shards)


def _gather_weights_async(shards, name="gather_weights_async", cid=1):
    n = len(shards)
    srcs = [jax.new_ref(s_, memory_space=pltpu.MemorySpace.HBM) for s_ in shards]
    outs = [jax.empty_ref(jax.ShapeDtypeStruct((N_CHIPS,) + s_.shape, s_.dtype), memory_space=pltpu.MemorySpace.HBM)
            for s_ in shards]

    @pl.kernel(mesh=plsc.ScalarSubcoreMesh(axis_name="seq", num_cores=1), name=name,
               scratch_types=(pltpu.SemaphoreType.DMA((n, 6)), pltpu.SemaphoreType.DMA((n, 6)),
                              pltpu.SemaphoreType.DMA((n,))),
               compiler_params=pltpu.CompilerParams(collective_id=cid))
    def launch(send_sems, recv_sems, local_sems):
        x, y, c, chips = _place()
        me = 2 * x + y
        sibling = (x, y, 1 - c)
        barrier = pltpu.get_barrier_semaphore()
        for peer in [sibling] + [(*chip, c) for chip in chips]:
            pl.semaphore_signal(barrier, inc=1, device_id=peer, device_id_type=MESH)
        pl.semaphore_wait(barrier, 4)

        def half(a, chip, h):
            hw = srcs[a].shape[1] // 2
            return outs[a].at[chip, :, pl.ds(pl.multiple_of(h * hw, LANES), hw)]

        def copy(a, k, src, chip, h, to):
            return pltpu.make_async_remote_copy(
                src_ref=src, dst_ref=half(a, chip, h), send_sem=send_sems.at[a, k], recv_sem=recv_sems.at[a, k],
                device_id=to, device_id_type=MESH)

        locals_, sends = [], []
        for a in range(n):
            local = pltpu.make_async_copy(srcs[a], outs[a].at[me], local_sems.at[a])
            local.start()
            locals_.append(local)
            hw = srcs[a].shape[1] // 2
            mine = srcs[a].at[:, pl.ds(pl.multiple_of(c * hw, LANES), hw)]
            for k, chip in enumerate(chips):
                cp = copy(a, k, mine, me, c, (*chip, c))
                cp.start()
                sends.append(cp)
        for a in range(n):
            for k, (cx, cy) in enumerate(chips):
                src_chip = 2 * cx + cy
                copy(a, k, half(a, src_chip, c), src_chip, c, (x, y, c)).wait_recv()
                fwd = copy(a, 3 + k, half(a, src_chip, c), src_chip, c, sibling)
                fwd.start()
                sends.append(fwd)
        for a in range(n):
            for k, (cx, cy) in enumerate(chips):
                src_chip = 2 * cx + cy
                copy(a, 3 + k, half(a, src_chip, 1 - c), src_chip, 1 - c, (x, y, c)).wait_recv()
        for cp in sends:
            cp.wait_send()
        for cp in locals_:
            cp.wait()

    launch()
    return [o[...] for o in outs]


def _on_sequencer(name, cid, inputs, out_shapes, sem_types, peers, body):
    srcs = [jax.new_ref(a, memory_space=pltpu.MemorySpace.HBM) for a in inputs]
    outs = [jax.empty_ref(sd, memory_space=pltpu.MemorySpace.HBM) for sd in out_shapes]

    @pl.kernel(mesh=plsc.ScalarSubcoreMesh(axis_name="seq", num_cores=1), name=name, scratch_types=tuple(sem_types),
               compiler_params=pltpu.CompilerParams(collective_id=cid))
    def launch(*sems):
        x, y, c, chips = _place()
        barrier = pltpu.get_barrier_semaphore()
        ps = peers(x, y, c, chips)
        for peer in ps:
            pl.semaphore_signal(barrier, inc=1, device_id=peer, device_id_type=MESH)
        pl.semaphore_wait(barrier, len(ps))
        body(srcs, outs, *sems)

    launch()
    return [o[...] for o in outs]


def _sibling_only(x, y, c, chips):
    return [(x, y, 1 - c)]


def _same_core_of_other_chips(x, y, c, chips):
    return [(*chip, c) for chip in chips]


def _swap_halves_to_sibling(name, cid, grads):
    n = len(grads)

    def body(ins, outs, send_sems, recv_sems):
        x, y, c, _ = _place()
        cps = []
        for a in range(n):
            hw = ins[a].shape[2] // 2
            src = ins[a].at[:, :, pl.ds(pl.multiple_of((1 - c) * hw, LANES), hw)]
            cp = pltpu.make_async_remote_copy(src_ref=src, dst_ref=outs[a], send_sem=send_sems.at[a],
                                              recv_sem=recv_sems.at[a], device_id=(x, y, 1 - c), device_id_type=MESH)
            cp.start()
            cps.append(cp)
        for cp in cps:
            cp.wait()

    return _on_sequencer(
        name, cid, grads, [jax.ShapeDtypeStruct(g.shape[:2] + (g.shape[2] // 2,), g.dtype) for g in grads],
        [pltpu.SemaphoreType.DMA((n,)), pltpu.SemaphoreType.DMA((n,))], _sibling_only, body)


def _exchange_quarters(name, cid, parts):
    n = len(parts)

    def body(ins, outs, send_sems, recv_sems):
        x, y, c, chips = _place()
        cps = []
        for a in range(n):
            for k, (cx, cy) in enumerate(chips):
                cp = pltpu.make_async_remote_copy(
                    src_ref=ins[a].at[2 * cx + cy], dst_ref=outs[a].at[k], send_sem=send_sems.at[a, k],
                    recv_sem=recv_sems.at[a, k], device_id=(cx, cy, c), device_id_type=MESH)
                cp.start()
                cps.append(cp)
        for cp in cps:
            cp.wait()

    return _on_sequencer(
        name, cid, parts, [jax.ShapeDtypeStruct((3,) + p_.shape[1:], p_.dtype) for p_ in parts],
        [pltpu.SemaphoreType.DMA((n, 3)), pltpu.SemaphoreType.DMA((n, 3))], _same_core_of_other_chips, body)


def _swap_final_halves(name, cid, halves):
    n = len(halves)

    def body(ins, outs, send_sems, recv_sems):
        x, y, c, _ = _place()
        cps = []
        for a in range(n):
            cp = pltpu.make_async_remote_copy(src_ref=ins[a], dst_ref=outs[a], send_sem=send_sems.at[a],
                                              recv_sem=recv_sems.at[a], device_id=(x, y, 1 - c), device_id_type=MESH)
            cp.start()
            cps.append(cp)
        for cp in cps:
            cp.wait()

    return _on_sequencer(
        name, cid, halves, [jax.ShapeDtypeStruct(h.shape, h.dtype) for h in halves],
        [pltpu.SemaphoreType.DMA((n,)), pltpu.SemaphoreType.DMA((n,))], _sibling_only, body)


def _add_half(name, grad, recv):
    nchip, r, cfull = grad.shape
    hw = cfull // 2
    tile = _pick(r, max(BF16_ROWS, (256 * 1024 // hw) // BF16_ROWS * BF16_ROWS), q=BF16_ROWS)
    c = lax.axis_index("c")

    def body(c_ref, g_ref, r_ref, o_ref):
        o_ref[...] = _bf(g_ref[...] + r_ref[...])

    return pl.pallas_call(
        body, name=name,
        grid_spec=pltpu.PrefetchScalarGridSpec(
            num_scalar_prefetch=1, grid=(nchip, r // tile),
            in_specs=[pl.BlockSpec((1, tile, hw), lambda k, i, cr: (k, i, cr[0])),
                      pl.BlockSpec((1, tile, hw), lambda k, i, cr: (k, i, 0))],
            out_specs=pl.BlockSpec((1, tile, hw), lambda k, i, cr: (k, i, 0))),
        out_shape=jax.ShapeDtypeStruct((nchip, r, hw), BF16), compiler_params=_params("parallel", "parallel"),
    )(c.reshape(1).astype(jnp.int32), grad, recv)


def _add_quarters(name, part, recv):
    _, r, hw = part.shape
    tile = _pick(r, max(BF16_ROWS, (256 * 1024 // hw) // BF16_ROWS * BF16_ROWS), q=BF16_ROWS)
    me = 2 * lax.axis_index("x") + lax.axis_index("y")

    def body(me_ref, p_ref, r_ref, o_ref):
        f = lambda v: v.astype(F32)
        o_ref[...] = ((f(p_ref[0]) + f(r_ref[0])) + f(r_ref[1])) + f(r_ref[2])

    return pl.pallas_call(
        body, name=name,
        grid_spec=pltpu.PrefetchScalarGridSpec(
            num_scalar_prefetch=1, grid=(r // tile,),
            in_specs=[pl.BlockSpec((1, tile, hw), lambda i, mr: (mr[0], i, 0)),
                      pl.BlockSpec((3, tile, hw), lambda i, mr: (0, i, 0))],
            out_specs=pl.BlockSpec((tile, hw), lambda i, mr: (i, 0))),
        out_shape=jax.ShapeDtypeStruct((r, hw), F32), compiler_params=_params("parallel"),
    )(me.reshape(1).astype(jnp.int32), part, recv)


class _ReduceScatter:
    def __init__(self, tag, first_cid, grads):
        self.tag, self.cid = tag, first_cid
        self.stacks = [g.reshape(N_CHIPS, g.shape[0] // N_CHIPS, g.shape[1]) for g in grads]

    def start(self, anchor):
        self.stacks, anchor = lax.optimization_barrier((self.stacks, anchor))
        self.recv = _swap_halves_to_sibling(f"rs_swap_halves_{self.tag}", self.cid, self.stacks)
        return anchor

    def exchange(self, anchor):
        parts = [_add_half(f"rs_add_half_{self.tag}{a}", g, r) for a, (g, r) in enumerate(zip(self.stacks, self.recv))]
        self.parts, anchor = lax.optimization_barrier((parts, anchor))
        self.quarters = _exchange_quarters(f"rs_exchange_{self.tag}", self.cid + 1, self.parts)
        return anchor

    def join(self, anchor):
        halves = [_add_quarters(f"rs_add_quarters_{self.tag}{a}", p_, q_)
                  for a, (p_, q_) in enumerate(zip(self.parts, self.quarters))]
        self.halves, anchor = lax.optimization_barrier((halves, anchor))
        self.others = _swap_final_halves(f"rs_swap_final_{self.tag}", self.cid + 2, self.halves)
        return anchor

    def result(self):
        south = lax.axis_index("c") == 0
        return [jnp.concatenate([jnp.where(south, h, o), jnp.where(south, o, h)], axis=1)
                for h, o in zip(self.halves, self.others)]


def _allgather_small(pack):
    m_per, n = pack.shape

    def body(ins, outs, send_sems, recv_sems, local_sem):
        x_ref, out_ref = ins[0], outs[0]
        x, y, c, chips = _place()
        me, sibling = (x, y, c), (x, y, 1 - c)

        def rows(px, py, pc):
            return out_ref.at[pl.ds(pl.multiple_of((4 * px + 2 * py + pc) * m_per, SUBLANES), m_per), :]

        def copy(k, block, to, src=None):
            return pltpu.make_async_remote_copy(
                src_ref=rows(*block) if src is None else src, dst_ref=rows(*block),
                send_sem=send_sems.at[k], recv_sem=recv_sems.at[k], device_id=to, device_id_type=MESH)

        mine = pltpu.make_async_copy(x_ref, rows(*me), local_sem)
        mine.start()
        first = [copy(0, me, sibling, src=x_ref)]
        first += [copy(1 + j, me, (*chip, c), src=x_ref) for j, chip in enumerate(chips)]
        for cp in first:
            cp.start()
        passed = [copy(4 + j, (*chip, c), sibling) for j, chip in enumerate(chips)]
        for j, chip in enumerate(chips):
            copy(1 + j, (*chip, c), me).wait_recv()
            passed[j].start()
        copy(0, sibling, me).wait_recv()
        for j, chip in enumerate(chips):
            copy(4 + j, (*chip, 1 - c), me).wait_recv()
        for cp in first + passed:
            cp.wait_send()
        mine.wait()

    def peers(x, y, c, chips):
        return [(x, y, 1 - c)] + [(*chip, c) for chip in chips]

    return _on_sequencer(
        "allgather_small_grads", 11, [pack], [jax.ShapeDtypeStruct((N_DEV * m_per, n), pack.dtype)],
        [pltpu.SemaphoreType.DMA((7,)), pltpu.SemaphoreType.DMA((7,)), pltpu.SemaphoreType.DMA], peers, body)[0]


def _sum_devices(packs, m_per):
    tile = _pick(m_per, 512, q=SUBLANES)
    nt = m_per // tile

    def body(*refs):
        acc = refs[0][...]
        for r in refs[1:N_DEV]:
            acc = acc + r[...]
        refs[N_DEV][...] = acc

    return pl.pallas_call(
        body, name="sum_small_grads", grid=(nt,),
        in_specs=[pl.BlockSpec((tile, LANES), functools.partial(lambda i, k: (k * nt + i, 0), k=k)) for k in range(N_DEV)],
        out_specs=_nat(tile, LANES), out_shape=jax.ShapeDtypeStruct((m_per, LANES), F32),
        compiler_params=_params("parallel"),
    )(*([packs] * N_DEV))


def _tail_fwd(tag, alpha, h_in, adds, mix_gate, ln1, ln2, p_l, w, want_perm):
    h_mid, xh1, rs1, h_mid_b, _ = _ln_fwd(f"ln1_fwd_{tag}", alpha, h_in, adds, mix_gate, *ln1)
    gp = _matmul(f"ple_gate_fwd_{tag}", h_mid_b, w['wg'], 'nn')
    pw = _matmul(f"ple_proj_fwd_{tag}", p_l, w['plet'], 'nt')
    gg, uu, act = _ffn_in_swiglu(f"ffn_in_fwd_{tag}", h_mid_b, w['wit'])
    ffn = _matmul(f"ffn_out_fwd_{tag}", act, w['wo'], 'nn', tk=2816)
    h_out, xh2, rs2, _, h_perm = _ln_fwd(f"ln2_fwd_{tag}", alpha, h_mid, [(ffn, 'nat')],
                                         ('nat', (pw, 1, 0), (gp, 1, 0)), *ln2, want_perm=want_perm)
    saved = dict(h_mid_b=h_mid_b, xh1=xh1, rs1=rs1, gp=gp, pw=pw, g=gg, u=uu, act=act, xh2=xh2, rs2=rs2)
    return h_out, h_perm, saved


def _tail_bwd(tag, alpha, dparts, sv, ln1_g, ln2_g, p_l, w, mix_gate):
    d = sv['h_mid_b'].shape[1]
    dz2, dz2b, dgate, dg2, db2 = _ln_bwd(f"ln2_bwd_{tag}", dparts, sv['xh2'], sv['rs2'], ln2_g,
                                         gate=('nat', (sv['pw'], 1, 0), (sv['gp'], 1, 0)))
    grads = dict(ln2_g=dg2, ln2_b=db2)
    grads['plet'] = _matmul(f"ple_proj_dw_{tag}", dgate, p_l, 'tn', a_win=(0, d))
    grads['wg'] = _matmul(f"ple_gate_dw_{tag}", sv['h_mid_b'], dgate, 'tn', b_win=(d, d))
    dx_gate = _matmul(f"ple_gate_dx_{tag}", dgate, w['wg'], 'nt', a_win=(d, d))
    dact = _matmul(f"ffn_out_dx_{tag}", dz2b, w['wo'], 'nt', out_dtype=BF16, tn=1408)
    grads['wo'] = _matmul(f"ffn_out_dw_{tag}", sv['act'], dz2b, 'tn', tm=1408)
    dgu = _swiglu_bwd(f"swiglu_bwd_{tag}", sv['g'], sv['u'], dact)
    grads['wit'] = _matmul(f"ffn_in_dw_{tag}", dgu, sv['h_mid_b'], 'tn')
    dx_ffn = _matmul(f"ffn_in_dx_{tag}", dgu, w['wit'], 'nn', tk=2816)
    res = _ln_bwd(f"ln1_bwd_{tag}", [(dz2, 'nat', alpha), (dx_gate, 'nat', 1.0), (dx_ffn, 'nat', 1.0)],
                  sv['xh1'], sv['rs1'], ln1_g, gate=mix_gate)
    grads['ln1_g'], grads['ln1_b'] = res[-2], res[-1]
    return res[:-2], grads


def kernel(x, p, positions, attn_w_in, mla_q_norm, mla_w_q_b, mla_kv_norm, mla_w_kv_b, attn_w_out, s5_a_re, s5_a_im, s5_log_dt, s5_b_re, s5_b_im, s5_c_re, s5_c_im, s5_d, s5_w_glu, ln1_g, ln1_b, ffn_w_in, ffn_w_out, ple_w, ple_gate_w, ln2_g, ln2_b, loss_target, m_attn_w_in, m_mla_q_norm, m_mla_w_q_b, m_mla_kv_norm, m_mla_w_kv_b, m_attn_w_out, m_s5_a_re, m_s5_a_im, m_s5_log_dt, m_s5_b_re, m_s5_b_im, m_s5_c_re, m_s5_c_im, m_s5_d, m_s5_w_glu, m_ln1_g, m_ln1_b, m_ffn_w_in, m_ffn_w_out, m_ple_w, m_ple_gate_w, m_ln2_g, m_ln2_b, v_attn_w_in, v_mla_q_norm, v_mla_w_q_b, v_mla_kv_norm, v_mla_w_kv_b, v_attn_w_out, v_s5_a_re, v_s5_a_im, v_s5_log_dt, v_s5_b_re, v_s5_b_im, v_s5_c_re, v_s5_c_im, v_s5_d, v_s5_w_glu, v_ln1_g, v_ln1_b, v_ffn_w_in, v_ffn_w_out, v_ple_w, v_ple_gate_w, v_ln2_g, v_ln2_b):
    weights = dict(attn_w_in=attn_w_in, mla_q_norm=mla_q_norm, mla_w_q_b=mla_w_q_b, mla_kv_norm=mla_kv_norm,
                   mla_w_kv_b=mla_w_kv_b, attn_w_out=attn_w_out, s5_a_re=s5_a_re, s5_a_im=s5_a_im, s5_log_dt=s5_log_dt,
                   s5_b_re=s5_b_re, s5_b_im=s5_b_im, s5_c_re=s5_c_re, s5_c_im=s5_c_im, s5_d=s5_d, s5_w_glu=s5_w_glu,
                   ln1_g=ln1_g, ln1_b=ln1_b, ffn_w_in=ffn_w_in, ffn_w_out=ffn_w_out, ple_w=ple_w, ple_gate_w=ple_gate_w,
                   ln2_g=ln2_g, ln2_b=ln2_b)
    m_in = dict(attn_w_in=m_attn_w_in, mla_q_norm=m_mla_q_norm, mla_w_q_b=m_mla_w_q_b, mla_kv_norm=m_mla_kv_norm,
                mla_w_kv_b=m_mla_w_kv_b, attn_w_out=m_attn_w_out, s5_a_re=m_s5_a_re, s5_a_im=m_s5_a_im,
                s5_log_dt=m_s5_log_dt, s5_b_re=m_s5_b_re, s5_b_im=m_s5_b_im, s5_c_re=m_s5_c_re, s5_c_im=m_s5_c_im,
                s5_d=m_s5_d, s5_w_glu=m_s5_w_glu, ln1_g=m_ln1_g, ln1_b=m_ln1_b, ffn_w_in=m_ffn_w_in,
                ffn_w_out=m_ffn_w_out, ple_w=m_ple_w, ple_gate_w=m_ple_gate_w, ln2_g=m_ln2_g, ln2_b=m_ln2_b)
    v_in = dict(attn_w_in=v_attn_w_in, mla_q_norm=v_mla_q_norm, mla_w_q_b=v_mla_w_q_b, mla_kv_norm=v_mla_kv_norm,
                mla_w_kv_b=v_mla_w_kv_b, attn_w_out=v_attn_w_out, s5_a_re=v_s5_a_re, s5_a_im=v_s5_a_im,
                s5_log_dt=v_s5_log_dt, s5_b_re=v_s5_b_re, s5_b_im=v_s5_b_im, s5_c_re=v_s5_c_re, s5_c_im=v_s5_c_im,
                s5_d=v_s5_d, s5_w_glu=v_s5_w_glu, ln1_g=v_ln1_g, ln1_b=v_ln1_b, ffn_w_in=v_ffn_w_in,
                ffn_w_out=v_ffn_w_out, ple_w=v_ple_w, ple_gate_w=v_ple_gate_w, ln2_g=v_ln2_g, ln2_b=v_ln2_b)
    names = list(weights)

    s, d = x.shape[1], x.shape[2]
    depth = ln1_g.shape[0]
    assert depth == 2
    alpha = (2.0 * depth) ** 0.25
    ql, kvl = mla_q_norm.shape[1], mla_kv_norm.shape[1]
    in_cols = N_CHIPS * attn_w_in.shape[2]
    heads = N_CHIPS * mla_w_q_b.shape[2] // (NOPE + ROPE)
    hps = heads // N_CHIPS
    dw = (in_cols - ql - kvl - ROPE) // 3
    dh = dw // DHD
    assert ql % LANES == 0 and kvl == ql and dw % DHD == 0 and heads % N_CHIPS == 0
    ngroups, nstate = s5_a_re.shape[1], s5_a_re.shape[2]
    assert nstate == S5_STATE and ngroups * S5_GROUP == d and d % LANES == 0
    nsl = d // LANES
    seg_len = s // SUBLANES
    n_sq = seg_len.bit_length() - 1
    assert 1 << n_sq == seg_len, "the segment length of the S5 scan must be a power of two"
    for window, dil in DIL_BRANCHES:
        assert window // dil == DIL_STEPS and (s // dil) % DIL_STEPS == 0
    me = 2 * lax.axis_index("x") + lax.axis_index("y")

    xb = x[0]
    target = loss_target[0]
    p_layers = [p[0, 0], p[1, 0]]
    pos = positions[0].astype(F32).reshape(s, 1)
    inv_freq = ROPE_THETA ** (-jnp.arange(ROPE // 2, dtype=F32) / (ROPE // 2))
    invf = jnp.concatenate([inv_freq, inv_freq, jnp.zeros((LANES - ROPE,), F32)]).reshape(1, LANES)
    slopes = 2.0 ** (-8.0 * jnp.arange(1, dh + 1, dtype=F32) / dh)
    slopes = jnp.broadcast_to(jnp.repeat(slopes, SUBLANES)[:, None], (dh * SUBLANES, LANES))

    wqb_t = mla_w_q_b[0].T.reshape(hps, NOPE + ROPE, ql)
    wqb_t = jnp.pad(wqb_t, ((0, 0), (0, QK_PAD - NOPE - ROPE), (0, 0))).reshape(hps * QK_PAD, ql)
    d_cols = max(d // N_CHIPS, 2 * LANES)
    d_pad = jnp.zeros((SUBLANES, d_cols), F32).at[0, :d // N_CHIPS].set(s5_d[0])
    shards = [_bf(attn_w_in[0].T), _bf(wqb_t), _bf(mla_w_kv_b[0].T), _bf(attn_w_out[0]), _bf(s5_w_glu[0].T)]
    for l in range(depth):
        shards += [_bf(ffn_w_in[l].T), _bf(ffn_w_out[l]), _bf(ple_w[l].T), _bf(ple_gate_w[l])]
    shards.append(d_pad)
    first = _gather_weights_async(shards[:3], "gather_weights_first", 12)
    first, later = lax.optimization_barrier((first, shards[3:-1]))
    gathered = first + _gather_weights_async(later) + list(_gather_weights(shards[-1:]))
    full = [g.reshape(N_CHIPS * g.shape[1], g.shape[2]) for g in gathered]
    win_t, wqb_t_f, wkv_t, wout, wglu_t = full[:5]
    lw = [dict(wit=full[5 + 4 * l], wo=full[6 + 4 * l], plet=full[7 + 4 * l], wg=full[8 + 4 * l]) for l in range(depth)]
    dvec = full[-1].reshape(N_CHIPS, SUBLANES, d_cols)[:, 0, :d // N_CHIPS].reshape(1, d)
    lat = ql + kvl
    win_t = jnp.concatenate([win_t[:lat + ROPE], jnp.zeros((LANES - ROPE, d), BF16), win_t[lat + ROPE:]], axis=0)
    kpe_cb = lat // LANES
    q_cb = kpe_cb + 1
    a_cb = heads * VDIM // LANES

    xbb = _bf(xb)
    proj = _matmul("attn_in_fwd", xbb, win_t, 'nt', tn=1408)
    nrm = _rms_fwd(proj, ql, kvl, mla_q_norm[0], mla_kv_norm[0])
    q_raw = _matmul("mla_q_up_fwd", nrm, wqb_t_f, 'nt', a_win=(0, ql))
    kv = _matmul("mla_kv_up_fwd", nrm, wkv_t, 'nt', a_win=(ql, kvl))
    qf, kf, vv = _rope_prep(q_raw, kv, proj, kpe_cb, pos, invf, heads)
    out_a, lse_a = _mla_fwd(qf, kf, vv.T, heads)
    out_b, lse_b = _dil_fused_fwd(proj, slopes, dh, q_cb)
    att = _concat_bf16("attn_heads_concat", out_a, out_b)
    mix0 = _matmul("attn_out_fwd", att, wout, 'nn')
    h2, h2p, sv0 = _tail_fwd("l0", alpha, xb, [(mix0, 'nat')], None, (ln1_g[0], ln1_b[0]), (ln2_g[0], ln2_b[0]),
                             p_layers[0], lw[0], want_perm=True)

    rep = lambda a: jnp.repeat(a, S5_GROUP, axis=0)
    ag = (s5_a_re[0], s5_a_im[0], jnp.broadcast_to(s5_log_dt[0][:, None], (ngroups, nstate)))
    a16 = tuple(rep(a) for a in ag)
    b16 = tuple(b[0].transpose(0, 2, 1).reshape(ngroups * S5_GROUP, nstate) for b in (s5_b_re, s5_b_im))
    abr, abi, apr, api, bbr, bbi = _s5_discretise(*a16, *b16, n_sq)
    ab_tile = _slab_tile(abr[::S5_GROUP], abi[::S5_GROUP], nsl)
    ap_tile = _slab_tile(apr[::S5_GROUP], api[::S5_GROUP], nsl)
    bblk = _bf(_slab_in_matrix(bbr.reshape(ngroups, S5_GROUP, nstate), bbi.reshape(ngroups, S5_GROUP, nstate), nsl))
    cblk = _bf(_slab_out_matrix(s5_c_re[0], s5_c_im[0], nsl))
    ends = _s5_pass1(h2p, bblk, ab_tile)
    cinx = _s5_carry("s5_carry_fwd", ends, ap_tile, False)
    xtrue, ypre, zg = _s5_pass2(h2p, bblk, cinx, ab_tile, cblk, dvec)
    vg = _matmul("s5_glu_fwd", zg, wglu_t, 'nt')
    glu_gate = ('perm', (vg, 2, 0), (vg, 2, 1))
    h4, _, sv1 = _tail_fwd("l1", alpha, h2, [], glu_gate, (ln1_g[1], ln1_b[1]), (ln2_g[1], ln2_b[1]),
                           p_layers[1], lw[1], want_perm=False)
    loss = lax.psum(jnp.sum(_loss_partial(h4, target)), ("x", "y", "c"))

    (dz1_1, _, dvg), g1 = _tail_bwd("l1", alpha, [(h4, 'nat', 1.0 / d), (target, 'nat', -1.0 / d)], sv1, ln1_g[1],
                                    ln2_g[1], p_layers[1], lw[1], glu_gate)
    d_wglu_t = _matmul("s5_glu_dw", dvg, zg, 'tn')
    dzg = _matmul("s5_glu_dx", dvg, wglu_t, 'nn')
    rs_l1 = _ReduceScatter("l1", 2, [d_wglu_t, g1['wit'], g1['wo'], g1['plet'], g1['wg']])
    dzg = rs_l1.start(dzg)
    starts, dy, dd = _s5_bwd_pass1(dzg, ypre, cblk, ab_tile, h2p)
    cinl = _s5_carry("s5_carry_bwd", starts, ap_tile, True)
    du_p, d_bblk, d_cblk, d_ab = _s5_bwd_pass2(dy, cblk, cinl, ab_tile, xtrue, cinx, h2p, bblk, dvec)
    gbb = _slab_in_extract(d_bblk, nsl)
    g_c_re, g_c_im = _slab_out_extract(jnp.swapaxes(d_cblk, 1, 2), nsl)
    d_ab = d_ab[::SUBLANES]
    gab = (d_ab[:, :SLAB_COLS].reshape(ngroups, nstate), d_ab[:, SLAB_COLS:].reshape(ngroups, nstate))
    g_a_re, g_a_im, g_log_dt, g_b_re, g_b_im = _s5_discretise_bwd(a16, b16, ag, gab, gbb)
    unt = lambda b: b.reshape(ngroups, S5_GROUP, nstate).transpose(0, 2, 1)

    du_p = rs_l1.exchange(du_p)
    (dz1_0, dz1_0b), g0 = _tail_bwd("l0", alpha, [(dz1_1, 'nat', alpha), (du_p, 'perm', 1.0)], sv0, ln1_g[0], ln2_g[0],
                                    p_layers[0], lw[0], None)
    dz1_0b = rs_l1.join(dz1_0b)
    d_wout = _matmul("attn_out_dw", att, dz1_0b, 'tn')
    rs_l0 = _ReduceScatter("l0", 5, [g0['wit'], g0['wo'], g0['plet'], g0['wg'], d_wout])
    dz1_0b = rs_l0.start(dz1_0b)
    datt = _matmul("attn_out_dx", dz1_0b, wout, 'nt')
    do, delta, delta_t = _attn_bwd_prep(datt, out_a, out_b)
    dqf, dkf, dvv = _mla_bwd(qf, kf, vv, do, lse_a, delta_t, heads, 0)
    dqf = rs_l0.exchange(dqf)
    dq_raw, dkv, dkpe = _rope_unprep(dqf, dkf, dvv, pos, invf, heads)
    d_wqb_t = _matmul("mla_q_up_dw", dq_raw, nrm, 'tn', b_win=(0, ql))
    d_wkv_t = _matmul("mla_kv_up_dw", dkv, nrm, 'tn', b_win=(ql, kvl))
    dnq = _matmul("mla_q_up_dx", dq_raw, wqb_t_f, 'nn')
    dnkv = _matmul("mla_kv_up_dx", dkv, wkv_t, 'nn')
    dqd, dkd, dvd = _dil_fused_bwd(proj, slopes, datt, lse_b, delta, dh, q_cb, a_cb)
    dkpe = rs_l0.join(dkpe)
    dproj, g_gq, g_gkv = _dproj_assemble(proj, dnq, dnkv, dkpe, [dqd], [dkd], [dvd], mla_q_norm[0], mla_kv_norm[0], ql)

    small = dict(mla_q_norm=g_gq, mla_kv_norm=g_gkv, s5_a_re=g_a_re, s5_a_im=g_a_im, s5_log_dt=g_log_dt,
                 s5_b_re=unt(g_b_re), s5_b_im=unt(g_b_im), s5_c_re=g_c_re, s5_c_im=g_c_im, s5_d=dd[0],
                 ln1_g=jnp.stack([g0['ln1_g'], g1['ln1_g']]), ln1_b=jnp.stack([g0['ln1_b'], g1['ln1_b']]),
                 ln2_g=jnp.stack([g0['ln2_g'], g1['ln2_g']]), ln2_b=jnp.stack([g0['ln2_b'], g1['ln2_b']]))
    flat = jnp.concatenate([v_.reshape(-1) for v_ in small.values()])
    m_per = -(-flat.shape[0] // (LANES * SUBLANES)) * SUBLANES
    pack, dproj = lax.optimization_barrier((jnp.pad(flat, (0, m_per * LANES - flat.shape[0])).reshape(m_per, LANES), dproj))
    small_packs = _allgather_small(pack)
    d_win_t = _matmul("attn_in_dw", dproj, xbb, 'tn', tm=1408)

    d_win_t = jnp.concatenate([d_win_t[:lat + ROPE], d_win_t[lat + LANES:]], axis=0)
    rs_at = _ReduceScatter("attn", 8, [d_win_t, d_wqb_t, d_wkv_t])
    r_wglu, r_wit1, r_wo1, r_plet1, r_wg1 = rs_l1.result()
    r_wit0, r_wo0, r_plet0, r_wg0, r_wout = rs_l0.result()
    dproj = rs_at.start(dproj)
    dx_attn = _matmul("attn_in_dx", dproj, win_t, 'nn')
    dx_attn, (r_wit0, r_wit1, r_wo0, r_wo1) = rs_at.exchange((dx_attn, (r_wit0, r_wit1, r_wo0, r_wo1)))
    grad_x = _axpy("grad_x", alpha, dz1_0, dx_attn)
    grad_x = rs_at.join(grad_x)
    r_win, r_wqb, r_wkv = rs_at.result()
    r_wqb = r_wqb.reshape(hps, QK_PAD, ql)[:, :NOPE + ROPE].reshape(hps * (NOPE + ROPE), ql)
    grads = dict(attn_w_in=r_win.T[None], mla_w_q_b=r_wqb.T[None], mla_w_kv_b=r_wkv.T[None], attn_w_out=r_wout[None],
                 s5_w_glu=r_wglu.T[None],
                 ffn_w_in=jnp.stack([r_wit0.T, r_wit1.T]), ffn_w_out=jnp.stack([r_wo0, r_wo1]),
                 ple_w=jnp.stack([r_plet0.T, r_plet1.T]), ple_gate_w=jnp.stack([r_wg0, r_wg1]))

    total = _sum_devices(small_packs, m_per).reshape(-1)
    off = 0
    for k_, v_ in small.items():
        n_ = v_.size
        piece = total[off:off + n_]
        off += n_
        if k_ == 's5_d':
            grads[k_] = lax.dynamic_slice(piece, (me * (d // N_CHIPS),), (d // N_CHIPS,)).reshape(weights[k_].shape)
        else:
            grads[k_] = piece.reshape(weights[k_].shape)

    deltas, new_m, new_v = {}, {}, {}
    for k_ in names:
        w_ = weights[k_]
        shape = w_.shape
        if w_.ndim == 3 and w_.shape[-1] >= LANES:
            two_d = (shape[0] * shape[1], shape[2])
        elif w_.ndim == 4:
            two_d = (shape[0] * shape[1], shape[2] * shape[3])
        else:
            two_d = (1, w_.size) if w_.ndim == 2 and shape[0] == 1 else (shape[0], w_.size // shape[0])
        dl, nm, nv = _adamw(f"adamw_{k_}", w_.reshape(two_d), grads[k_].reshape(two_d), m_in[k_].reshape(two_d),
                            v_in[k_].reshape(two_d))
        deltas[k_], new_m[k_], new_v[k_] = dl.reshape(shape), nm.reshape(shape), nv.reshape(shape)

    return (loss, grad_x[None], *[grads[k_] for k_ in names], *[deltas[k_] for k_ in names],
            *[new_m[k_] for k_ in names], *[new_v[k_] for k_ in names])
```
